```python
import jax, jax.numpy as jnp
from jax import lax
import numpy as np

D_MODEL = 1024
BATCH = 8
SEQ = 2048
DEPTH = 1
DEC_BATCH = 128
DEC_SEQ = 4
PAST_LEN = 16384
PAGE_SIZE = 128

W_A = D_MODEL // 2
HGRN_EXPAND = 128
H_A = W_A // HGRN_EXPAND
DK = HGRN_EXPAND
DV = W_A // H_A
CHUNK = 32
W_B = D_MODEL // 2
POOL_WINDOWS = (2, 4, 8, 16)
N_POOL_GROUPS = len(POOL_WINDOWS)
G_B = W_B // N_POOL_GROUPS
POOL_BUF = max(POOL_WINDOWS) - 1
D_IN = 4 * W_A + 2 * W_B + 2 * D_MODEL
EPS = 1e-6

kernel_name = 'hgrn2_pool_gated_hybrid_step'


def _rmsnorm(x, g):
    x32 = x.astype(jnp.float32)
    y = x32 * lax.rsqrt(jnp.mean(x32 * x32, axis=-1, keepdims=True) + EPS)
    return (y * g.astype(jnp.float32)).astype(x.dtype)


def _hgrn2_chunked(q, k, v, logf, s0):
    B, T, H, _ = q.shape
    C = CHUNK if T >= CHUNK else T
    n = -(-T // C)
    pad = n * C - T

    def to_chunks(a):
        a = jnp.pad(a, ((0, 0), (0, pad), (0, 0), (0, 0)))
        return a.reshape(B, n, C, H, a.shape[-1]).transpose(1, 0, 3, 2, 4)

    qc, kc, vc, gc = to_chunks(q), to_chunks(k), to_chunks(v), to_chunks(logf)
    causal = jnp.tril(jnp.ones((C, C), dtype=bool))

    def step(S, blk):
        qb, kb, vb, gb = blk
        b = jnp.cumsum(gb, axis=2)
        ref = b[:, :, C // 2:C // 2 + 1]
        b_last = b[:, :, C - 1:C]
        scores = jnp.einsum('bhtd,bhsd->bhts', qb * jnp.exp(b - ref), kb * jnp.exp(ref - b))
        scores = jnp.where(causal, scores, 0.0)
        o = (jnp.einsum('bhts,bhsv->bhtv', scores, vb)
             + jnp.einsum('bhtd,bhdv->bhtv', qb * jnp.exp(b), S))
        S = (jnp.exp(b_last[:, :, 0])[..., None] * S
             + jnp.einsum('bhsd,bhsv->bhdv', kb * jnp.exp(b_last - b), vb))
        return S, o

    S, o = lax.scan(step, s0, (qc, kc, vc, gc))
    o = o.transpose(1, 0, 3, 2, 4).reshape(B, n * C, H, -1)[:, :T]
    return o, S


def _multiscale_pool(u, buf, pos0):
    B, T, _ = u.shape
    u32 = u.astype(jnp.float32)
    ext = jnp.concatenate([buf.astype(jnp.float32), u32], axis=1)
    cs = jnp.concatenate([jnp.zeros((B, 1, W_B), jnp.float32), jnp.cumsum(ext, axis=1)], axis=1)
    pos = pos0 + jnp.arange(T)
    outs = []
    for gi, w in enumerate(POOL_WINDOWS):
        lo, hi = gi * G_B, (gi + 1) * G_B
        s = cs[:, POOL_BUF + 1:POOL_BUF + 1 + T, lo:hi] - cs[:, POOL_BUF + 1 - w:POOL_BUF + 1 - w + T, lo:hi]
        cnt = jnp.minimum(pos + 1, w).astype(jnp.float32)
        outs.append(s / cnt[None, :, None])
    return jnp.concatenate(outs, axis=-1) - u32


def _layer(x, s0, buf0, pos0, lb, norm_g, w_in, hgrn_norm_g, w_proj_a, w_pool, pool_scale, w_proj_b, w_out):
    B, T, _ = x.shape
    dt = x.dtype
    h = _rmsnorm(x, norm_g)
    z = h @ w_in
    q, f, i, g_a, u, g_b, m_a, m_b = jnp.split(
        z, np.cumsum([W_A, W_A, W_A, W_A, W_B, W_B, D_MODEL]).tolist(), axis=-1)
    fgate = lb + (1.0 - lb) * jax.nn.sigmoid(f.astype(jnp.float32))
    logf = jnp.log(fgate)
    kin = 1.0 - fgate
    qf = jax.nn.silu(q.astype(jnp.float32)) * (DK ** -0.5)
    hd = lambda a: a.reshape(B, T, H_A, -1)
    o, s_new = _hgrn2_chunked(hd(qf), hd(kin), hd(i.astype(jnp.float32)), hd(logf), s0)
    o = _rmsnorm(o, hgrn_norm_g) * jax.nn.silu(hd(g_a.astype(jnp.float32)))
    y_a = o.reshape(B, T, W_A).astype(dt) @ w_proj_a
    pooled = _multiscale_pool(u, buf0, pos0).reshape(B, T, N_POOL_GROUPS, G_B)
    mixed = jnp.einsum('btgc,gcd->btgd', pooled, w_pool.astype(jnp.float32)).reshape(B, T, W_B)
    yb = mixed * pool_scale.astype(jnp.float32) * jax.nn.silu(g_b.astype(jnp.float32))
    y_b = yb.astype(dt) @ w_proj_b
    merged = jax.nn.sigmoid(m_a) * y_a + jax.nn.sigmoid(m_b) * y_b
    out = x + merged @ w_out
    new_buf = jnp.concatenate([buf0.astype(dt), u], axis=1)[:, -POOL_BUF:]
    return out, s_new, new_buf


def setup_inputs(seed: int = 0) -> dict:
    key = jax.random.key(seed)
    ks = jax.random.split(key, 14)
    nrm = lambda k, s: jax.random.normal(k, s, jnp.float32)
    return {
        'x_prompt': nrm(ks[0], (BATCH, SEQ, D_MODEL)),
        'x_sample': nrm(ks[1], (DEC_BATCH, DEC_SEQ, D_MODEL)),
        'state_hgrn': 0.5 * nrm(ks[2], (DEPTH, DEC_BATCH, H_A, DK, DV)),
        'state_pool': nrm(ks[3], (DEPTH, DEC_BATCH, POOL_BUF, W_B)),
        'norm_g': 1.0 + 0.02 * nrm(ks[4], (DEPTH, D_MODEL)),
        'w_in': nrm(ks[5], (DEPTH, D_MODEL, D_IN)) * D_MODEL ** -0.5,
        'lb_logits': 0.1 * nrm(ks[6], (DEPTH + 1, W_A)),
        'hgrn_norm_g': 1.0 + 0.02 * nrm(ks[7], (DEPTH, DV)),
        'w_proj_a': nrm(ks[8], (DEPTH, W_A, D_MODEL)) * W_A ** -0.5,
        'w_pool': nrm(ks[9], (DEPTH, N_POOL_GROUPS, G_B, G_B)) * G_B ** -0.5,
        'pool_scale': 1.0 + 0.1 * nrm(ks[10], (DEPTH, W_B)),
        'w_proj_b': nrm(ks[11], (DEPTH, W_B, D_MODEL)) * W_B ** -0.5,
        'w_out': nrm(ks[12], (DEPTH, D_MODEL, D_MODEL)) * D_MODEL ** -0.5,
        'final_norm_g': 1.0 + 0.02 * nrm(ks[13], (D_MODEL,)),
    }


def reference(x_prompt, x_sample, state_hgrn, state_pool, norm_g, w_in, lb_logits, hgrn_norm_g,
              w_proj_a, w_pool, pool_scale, w_proj_b, w_out, final_norm_g):
    lbs = jnp.cumsum(jax.nn.softmax(lb_logits.astype(jnp.float32), axis=0), axis=0)
    yp, ys = x_prompt, x_sample
    hp, pp, hs, ps = [], [], [], []
    for l in range(DEPTH):
        s0p = jnp.zeros((BATCH, H_A, DK, DV), jnp.float32)
        b0p = jnp.zeros((BATCH, POOL_BUF, W_B), x_prompt.dtype)
        yp, s_p, b_p = _layer(yp, s0p, b0p, 0, lbs[l], norm_g[l], w_in[l], hgrn_norm_g[l],
                              w_proj_a[l], w_pool[l], pool_scale[l], w_proj_b[l], w_out[l])
        ys, s_s, b_s = _layer(ys, state_hgrn[l].astype(jnp.float32), state_pool[l], PAST_LEN, lbs[l],
                              norm_g[l], w_in[l], hgrn_norm_g[l], w_proj_a[l], w_pool[l],
                              pool_scale[l], w_proj_b[l], w_out[l])
        hp.append(s_p.astype(state_hgrn.dtype)); pp.append(b_p.astype(state_pool.dtype))
        hs.append(s_s.astype(state_hgrn.dtype)); ps.append(b_s.astype(state_pool.dtype))
    y_prompt = _rmsnorm(yp, final_norm_g)
    y_sample = _rmsnorm(ys, final_norm_g)
    return (y_prompt, y_sample, jnp.stack(hp), jnp.stack(pp), jnp.stack(hs), jnp.stack(ps))
```

```python
import functools

import jax
import jax.numpy as jnp
from jax import lax
from jax.experimental import pallas as pl
from jax.experimental.pallas import tpu as pltpu

F32 = jnp.float32
BF16 = jnp.bfloat16

D_MODEL = 1024
W_BRANCH = 512
N_HEADS = 4
DK = 128
DV = 128
CHUNK = 32
POOL_WINDOWS = (2, 4, 8, 16)
G_B = 128
POOL_BUF = 15
PAST_LEN = 16384
EPS = 1e-6
D_IN = 4 * W_BRANCH + 2 * W_BRANCH + 2 * D_MODEL
OFF_Q, OFF_F, OFF_I, OFF_GA = 0, 512, 1024, 1536
OFF_U, OFF_GB, OFF_MA, OFF_MB = 2048, 2560, 3072, 4096

SUBLANES = 8
VMEM_LIMIT_BYTES = 56 * 1024 * 1024

TM_PROMPT = 256
BB_SAMPLE = 16
ROWS_PER_SEQ = SUBLANES


def _rms(x, g):
    ms = jnp.mean(x * x, axis=-1, keepdims=True)
    return x * lax.rsqrt(ms + EPS) * g


def _lower_bound(lbl_ref):
    l0 = lbl_ref[0:1, :]
    l1 = lbl_ref[1:2, :]
    m = jnp.maximum(l0, l1)
    e0 = jnp.exp(l0 - m)
    e1 = jnp.exp(l1 - m)
    return e0 / (e0 + e1)


def _in_proj(x, ng_ref, win_ref, z_ref):
    h = _rms(x, ng_ref[...]).astype(BF16)
    z_ref[...] = jnp.dot(h, win_ref[...], preferred_element_type=F32)


def _hgrn_inputs(z_ref, lb, hh):
    c = hh * DK
    q = z_ref[:, OFF_Q + c:OFF_Q + c + DK]
    f = z_ref[:, OFF_F + c:OFF_F + c + DK]
    v = z_ref[:, OFF_I + c:OFF_I + c + DK]
    lbh = lb[:, c:c + DK]
    fg = lbh + (1.0 - lbh) * jax.nn.sigmoid(f)
    logf = jnp.log(fg)
    k = 1.0 - fg
    qf = jax.nn.silu(q) * (DK ** -0.5)
    return qf, k, v, logf


def _group_cumsum(x, group):
    row = lax.broadcasted_iota(jnp.int32, x.shape, 0) & (group - 1)
    s = 1
    while s < group:
        x = x + jnp.where(row >= s, pltpu.roll(x, s, 0), 0.0)
        s *= 2
    return x


def _hgrn_epilogue(o, z_ref, hg_ref, hh, oa_ref):
    c = hh * DK
    ga = z_ref[:, OFF_GA + c:OFF_GA + c + DK]
    on = _rms(o, hg_ref[...]) * jax.nn.silu(ga)
    oa_ref[:, c:c + DK] = on.astype(BF16)


def _pool_epilogue(pooled, z_ref, g, wpool_ref, ps_ref, yb_ref):
    c = g * G_B
    mixed = jnp.dot(pooled.astype(BF16), wpool_ref[g], preferred_element_type=F32)
    gb = z_ref[:, OFF_GB + c:OFF_GB + c + G_B]
    yb = mixed * ps_ref[:, c:c + G_B] * jax.nn.silu(gb)
    yb_ref[:, c:c + G_B] = yb.astype(BF16)


def _out_stage(x, z_ref, oa_ref, yb_ref, wpa_ref, wpb_ref, wout_ref, fg_ref):
    y_a = jnp.dot(oa_ref[...], wpa_ref[...], preferred_element_type=F32)
    y_b = jnp.dot(yb_ref[...], wpb_ref[...], preferred_element_type=F32)
    m_a = z_ref[:, OFF_MA:OFF_MA + D_MODEL]
    m_b = z_ref[:, OFF_MB:OFF_MB + D_MODEL]
    merged = jax.nn.sigmoid(m_a) * y_a + jax.nn.sigmoid(m_b) * y_b
    out = x + jnp.dot(merged.astype(BF16), wout_ref[...], preferred_element_type=F32)
    return _rms(out, fg_ref[...])


def _prompt_kernel(x_ref, ng_ref, win_ref, lbl_ref, hg_ref, wpa_ref, wpool_ref, ps_ref,
                   wpb_ref, wout_ref, fg_ref,
                   y_ref, s_out_ref, p_out_ref,
                   z_ref, st_ref, ext_ref, kx_ref, oa_ref, yb_ref):
    tm = TM_PROMPT
    nc = tm // CHUNK
    t = pl.program_id(1)
    nt = pl.num_programs(1)

    @pl.when(t == 0)
    def _():
        st_ref[...] = jnp.zeros_like(st_ref)
        ext_ref[0:2 * SUBLANES, :] = jnp.zeros((2 * SUBLANES, W_BRANCH), F32)
        kx_ref[...] = jnp.zeros_like(kx_ref)

    x = x_ref[0]
    _in_proj(x, ng_ref, win_ref, z_ref)
    lb = _lower_bound(lbl_ref)

    ri = lax.broadcasted_iota(jnp.int32, (tm, tm), 0)
    ci = lax.broadcasted_iota(jnp.int32, (tm, tm), 1)
    causal = ((ri // CHUNK) == (ci // CHUNK)) & (ci <= ri)

    for hh in range(N_HEADS):
        qf, k, v, logf = _hgrn_inputs(z_ref, lb, hh)
        b = _group_cumsum(logf, CHUNK)
        b3 = b.reshape(nc, CHUNK, DK)
        ref = b3[:, CHUNK // 2:CHUNK // 2 + 1, :]
        bl = b3[:, CHUNK - 1:CHUNK, :]
        q3 = qf.reshape(nc, CHUNK, DK)
        k3 = k.reshape(nc, CHUNK, DK)
        qs = (q3 * jnp.exp(b3 - ref)).reshape(tm, DK).astype(BF16)
        ks = (k3 * jnp.exp(ref - b3)).reshape(tm, DK).astype(BF16)
        qd = (q3 * jnp.exp(b3)).reshape(tm, DK).astype(BF16)
        kd = (k3 * jnp.exp(bl - b3)).reshape(tm, DK).astype(BF16)
        vb = v.astype(BF16)

        sc = lax.dot_general(qs, ks, (((1,), (1,)), ((), ())), preferred_element_type=F32)
        sc = jnp.where(causal, sc, 0.0)
        o = jnp.dot(sc.astype(BF16), vb, preferred_element_type=F32)

        for c in range(nc):
            kx_ref[c * CHUNK:(c + 1) * CHUNK, c * DK:(c + 1) * DK] = kd[c * CHUNK:(c + 1) * CHUNK, :]
        ut = lax.dot_general(vb, kx_ref[...], (((0,), (0,)), ((), ())), preferred_element_type=F32)

        dec = jnp.exp(bl)
        st = st_ref[hh]
        o_inter = []
        for c in range(nc):
            o_inter.append(lax.dot_general(qd[c * CHUNK:(c + 1) * CHUNK, :], st.astype(BF16),
                                           (((1,), (1,)), ((), ())), preferred_element_type=F32))
            st = dec[c] * st + ut[:, c * DK:(c + 1) * DK]
        st_ref[hh] = st
        o = o + jnp.concatenate(o_inter, axis=0)
        _hgrn_epilogue(o, z_ref, hg_ref, hh, oa_ref)

    hdr = 2 * SUBLANES
    ext_ref[hdr:hdr + tm, :] = z_ref[:, OFF_U:OFF_U + W_BRANCH]
    pos1 = t * tm + lax.broadcasted_iota(jnp.int32, (tm, 1), 0) + 1
    for g, w in enumerate(POOL_WINDOWS):
        c = g * G_B
        s = ext_ref[:, c:c + G_B]
        sh = 1
        while sh < w:
            s = s + pltpu.roll(s, sh, 0)
            sh *= 2
        inv = jnp.where(pos1 >= w, 1.0 / w, 1.0 / pos1.astype(F32))
        pooled = s[hdr:, :] * inv - z_ref[:, OFF_U + c:OFF_U + c + G_B]
        _pool_epilogue(pooled, z_ref, g, wpool_ref, ps_ref, yb_ref)
    ext_ref[0:hdr, :] = ext_ref[tm:tm + hdr, :]

    y_ref[0] = _out_stage(x, z_ref, oa_ref, yb_ref, wpa_ref, wpb_ref, wout_ref, fg_ref)

    @pl.when(t == nt - 1)
    def _():
        for hh in range(N_HEADS):
            s_out_ref[0, hh] = st_ref[hh].T
        p_out_ref[0] = ext_ref[hdr + tm - POOL_BUF:hdr + tm, :]


def _sample_kernel(x_ref, s_in_ref, p_in_ref, ng_ref, win_ref, lbl_ref, hg_ref, wpa_ref,
                   wpool_ref, ps_ref, wpb_ref, wout_ref, fg_ref,
                   y_ref, s_out_ref, p_out_ref,
                   xs_ref, z_ref, e_ref, a_ref, bm_ref, qd_ref, o_ref, oa_ref, yb_ref,
                   *, dec_seq):
    bb = BB_SAMPLE
    rs = ROWS_PER_SEQ
    rows = bb * rs

    @pl.when(pl.program_id(0) == 0)
    def _():
        xs_ref[...] = jnp.zeros_like(xs_ref)
        e_ref[...] = jnp.zeros_like(e_ref)

    for i in range(bb):
        xs_ref[i * rs:i * rs + dec_seq, :] = x_ref[i]
    x = xs_ref[...]
    _in_proj(x, ng_ref, win_ref, z_ref)
    lb = _lower_bound(lbl_ref)

    ri = lax.broadcasted_iota(jnp.int32, (rows, rows), 0)
    ci = lax.broadcasted_iota(jnp.int32, (rows, rows), 1)
    causal = ((ri // rs) == (ci // rs)) & (ci <= ri)
    r8 = lax.broadcasted_iota(jnp.int32, (rows, DK), 0) & (rs - 1)
    ones_rows = jnp.where((r8 >= dec_seq) & (r8 < dec_seq + 3), 1.0, 0.0).astype(BF16)

    for hh in range(N_HEADS):
        qf, k, v, logf = _hgrn_inputs(z_ref, lb, hh)
        b = _group_cumsum(logf, rs)
        b3 = b.reshape(bb, rs, DK)
        ref = b3[:, dec_seq // 2:dec_seq // 2 + 1, :]
        bl = b3[:, dec_seq - 1:dec_seq, :]
        q3 = qf.reshape(bb, rs, DK)
        k3 = k.reshape(bb, rs, DK)
        qs = (q3 * jnp.exp(b3 - ref)).reshape(rows, DK).astype(BF16)
        ks = (k3 * jnp.exp(ref - b3)).reshape(rows, DK).astype(BF16)
        qd = (q3 * jnp.exp(b3)).reshape(rows, DK).astype(BF16)
        kd = (k3 * jnp.exp(bl - b3)).reshape(rows, DK)
        vb = v.astype(BF16)

        sc = lax.dot_general(qs, ks, (((1,), (1,)), ((), ())), preferred_element_type=F32)
        sc = jnp.where(causal, sc, 0.0)
        o_ref[...] = jnp.dot(sc.astype(BF16), vb, preferred_element_type=F32)
        qd_ref[...] = qd

        dec = jnp.broadcast_to(jnp.exp(bl), (bb, rs, DK)).reshape(rows, DK)
        d1 = dec.astype(BF16).astype(F32)
        d2 = (dec - d1).astype(BF16).astype(F32)
        d3 = (dec - d1 - d2).astype(BF16).astype(F32)
        a = jnp.where(r8 < dec_seq, kd,
                      jnp.where(r8 == dec_seq, d1,
                                jnp.where(r8 == dec_seq + 1, d2,
                                          jnp.where(r8 == dec_seq + 2, d3, 0.0))))
        a_ref[...] = a.astype(BF16)
        bm_ref[:, 0:DV] = jnp.where(r8 < dec_seq, vb, jnp.zeros_like(vb))
        bm_ref[:, DV:2 * DV] = ones_rows

        for i in range(bb):
            s_old = s_in_ref[i, hh]
            qd_i = qd_ref[i * rs:(i + 1) * rs, :]
            o_ref[i * rs:(i + 1) * rs, :] += jnp.dot(qd_i, s_old.astype(BF16),
                                                     preferred_element_type=F32)
            upd = lax.dot_general(a_ref[i * rs:(i + 1) * rs, :], bm_ref[i * rs:(i + 1) * rs, :],
                                  (((0,), (0,)), ((), ())), preferred_element_type=F32)
            s_out_ref[i, hh] = upd[:, DV:2 * DV] * s_old + upd[:, 0:DV]

        _hgrn_epilogue(o_ref[...], z_ref, hg_ref, hh, oa_ref)

    e_ref[:, 0:POOL_BUF, :] = p_in_ref[...]
    u_all = z_ref[:, OFF_U:OFF_U + W_BRANCH]
    nrow = 2 * rs
    jrow = lax.broadcasted_iota(jnp.int32, (bb, nrow, G_B), 1)
    trow = lax.broadcasted_iota(jnp.int32, (bb, rs, G_B), 1)
    for g, w in enumerate(POOL_WINDOWS):
        c = g * G_B
        sc_ = e_ref[:, :, c:c + G_B]
        sh = 1
        while sh < nrow:
            sc_ = sc_ + jnp.where(jrow + sh < nrow, pltpu.roll(sc_, nrow - sh, 1), 0.0)
            sh *= 2
        rw = pltpu.roll(sc_, w % nrow, 1)[:, 0:rs, :]
        rw = jnp.where(trow < w, rw, 0.0).reshape(rows, G_B)
        u = u_all[:, c:c + G_B]
        cu = u
        sh = 1
        while sh < min(w, dec_seq):
            cu = cu + jnp.where(r8 >= sh, pltpu.roll(cu, sh, 0), 0.0)
            sh *= 2
        pooled = (rw + cu) * (1.0 / w) - u
        _pool_epilogue(pooled, z_ref, g, wpool_ref, ps_ref, yb_ref)

    keep = POOL_BUF - dec_seq
    jrow_w = lax.broadcasted_iota(jnp.int32, (bb, nrow, W_BRANCH), 1)
    old = pltpu.roll(e_ref[...], nrow - dec_seq, 1)
    u3 = u_all.reshape(bb, rs, W_BRANCH)
    new = pltpu.roll(jnp.concatenate([u3, jnp.zeros_like(u3)], axis=1), keep, 1)
    nb = jnp.where(jrow_w < keep, old, new)
    p_out_ref[...] = nb[:, 0:POOL_BUF, :]

    y = _out_stage(x, z_ref, oa_ref, yb_ref, wpa_ref, wpb_ref, wout_ref, fg_ref)
    for i in range(bb):
        y_ref[i] = y[i * rs:i * rs + dec_seq, :]


def _const_spec(shape):
    n = len(shape)
    return pl.BlockSpec(shape, lambda *_: (0,) * n, pipeline_mode=pl.Buffered(1))


def _weight_specs():
    return [
        _const_spec((1, D_MODEL)),
        _const_spec((D_MODEL, D_IN)),
        _const_spec((2, W_BRANCH)),
        _const_spec((1, DV)),
        _const_spec((W_BRANCH, D_MODEL)),
        _const_spec((len(POOL_WINDOWS), G_B, G_B)),
        _const_spec((1, W_BRANCH)),
        _const_spec((W_BRANCH, D_MODEL)),
        _const_spec((D_MODEL, D_MODEL)),
        _const_spec((1, D_MODEL)),
    ]


def kernel(x_prompt, x_sample, state_hgrn, state_pool, norm_g, w_in, lb_logits, hgrn_norm_g,
           w_proj_a, w_pool, pool_scale, w_proj_b, w_out, final_norm_g):
    batch, seq, _ = x_prompt.shape
    dec_batch, dec_seq, _ = x_sample.shape
    assert norm_g.shape[0] == 1 and lb_logits.shape[0] == 2, "single-layer decoder only"
    assert seq % TM_PROMPT == 0 and dec_batch % BB_SAMPLE == 0
    assert dec_seq + 3 <= ROWS_PER_SEQ and dec_seq < CHUNK and PAST_LEN >= max(POOL_WINDOWS)

    weights = (norm_g, w_in[0].astype(BF16), lb_logits, hgrn_norm_g,
               w_proj_a[0].astype(BF16), w_pool[0].astype(BF16), pool_scale,
               w_proj_b[0].astype(BF16), w_out[0].astype(BF16), final_norm_g.reshape(1, D_MODEL))

    tm = TM_PROMPT
    nt = seq // tm
    y_p, s_p, p_p = pl.pallas_call(
        _prompt_kernel,
        grid=(batch, nt),
        in_specs=[pl.BlockSpec((1, tm, D_MODEL), lambda b, t: (b, t, 0))] + _weight_specs(),
        out_specs=[
            pl.BlockSpec((1, tm, D_MODEL), lambda b, t: (b, t, 0)),
            pl.BlockSpec((1, N_HEADS, DK, DV), lambda b, t: (b, 0, 0, 0)),
            pl.BlockSpec((1, POOL_BUF, W_BRANCH), lambda b, t: (b, 0, 0)),
        ],
        out_shape=[
            jax.ShapeDtypeStruct((batch, seq, D_MODEL), F32),
            jax.ShapeDtypeStruct((batch, N_HEADS, DK, DV), F32),
            jax.ShapeDtypeStruct((batch, POOL_BUF, W_BRANCH), F32),
        ],
        scratch_shapes=[
            pltpu.VMEM((tm, D_IN), F32),
            pltpu.VMEM((N_HEADS, DV, DK), F32),
            pltpu.VMEM((tm + 2 * SUBLANES, W_BRANCH), F32),
            pltpu.VMEM((tm, (tm // CHUNK) * DK), BF16),
            pltpu.VMEM((tm, W_BRANCH), BF16),
            pltpu.VMEM((tm, W_BRANCH), BF16),
        ],
        compiler_params=pltpu.CompilerParams(
            dimension_semantics=("arbitrary", "arbitrary"),
            vmem_limit_bytes=VMEM_LIMIT_BYTES),
        name="hgrn2_pool_prompt",
    )(x_prompt, *weights)

    bb = BB_SAMPLE
    rows = bb * ROWS_PER_SEQ
    y_s, s_s, p_s = pl.pallas_call(
        functools.partial(_sample_kernel, dec_seq=dec_seq),
        grid=(dec_batch // bb,),
        in_specs=[
            pl.BlockSpec((bb, dec_seq, D_MODEL), lambda i: (i, 0, 0)),
            pl.BlockSpec((bb, N_HEADS, DK, DV), lambda i: (i, 0, 0, 0)),
            pl.BlockSpec((bb, POOL_BUF, W_BRANCH), lambda i: (i, 0, 0)),
        ] + _weight_specs(),
        out_specs=[
            pl.BlockSpec((bb, dec_seq, D_MODEL), lambda i: (i, 0, 0)),
            pl.BlockSpec((bb, N_HEADS, DK, DV), lambda i: (i, 0, 0, 0)),
            pl.BlockSpec((bb, POOL_BUF, W_BRANCH), lambda i: (i, 0, 0)),
        ],
        out_shape=[
            jax.ShapeDtypeStruct((dec_batch, dec_seq, D_MODEL), F32),
            jax.ShapeDtypeStruct((dec_batch, N_HEADS, DK, DV), F32),
            jax.ShapeDtypeStruct((dec_batch, POOL_BUF, W_BRANCH), F32),
        ],
        scratch_shapes=[
            pltpu.VMEM((rows, D_MODEL), F32),
            pltpu.VMEM((rows, D_IN), F32),
            pltpu.VMEM((bb, 2 * ROWS_PER_SEQ, W_BRANCH), F32),
            pltpu.VMEM((rows, DK), BF16),
            pltpu.VMEM((rows, 2 * DV), BF16),
            pltpu.VMEM((rows, DK), BF16),
            pltpu.VMEM((rows, DV), F32),
            pltpu.VMEM((rows, W_BRANCH), BF16),
            pltpu.VMEM((rows, W_BRANCH), BF16),
        ],
        compiler_params=pltpu.CompilerParams(
            dimension_semantics=("arbitrary",),
            vmem_limit_bytes=VMEM_LIMIT_BYTES),
        name="hgrn2_pool_sample",
    )(x_sample, state_hgrn[0], state_pool[0], *weights)

    return (y_p, y_s, s_p[None], p_p[None], s_s[None], p_s[None])
```

```python
import functools

import jax
import jax.numpy as jnp
from jax import lax
from jax.experimental import pallas as pl
from jax.experimental.pallas import tpu as pltpu

F32 = jnp.float32
BF16 = jnp.bfloat16

D_MODEL = 1024
W_BRANCH = 512
N_HEADS = 4
DK = 128
DV = 128
CHUNK = 32
POOL_WINDOWS = (2, 4, 8, 16)
G_B = 128
POOL_BUF = 15
PAST_LEN = 16384
EPS = 1e-6
D_IN = 4 * W_BRANCH + 2 * W_BRANCH + 2 * D_MODEL
OFF_Q, OFF_F, OFF_I, OFF_GA = 0, 512, 1024, 1536
OFF_U, OFF_GB, OFF_MA, OFF_MB = 2048, 2560, 3072, 4096

SUBLANES = 8
VMEM_LIMIT_BYTES = 56 * 1024 * 1024

TM_PROMPT = 256
PROJ_COLS = 256
BB_SAMPLE = 16
ROWS_PER_SEQ = SUBLANES


def _rms(x, g):
    ms = jnp.mean(x * x, axis=-1, keepdims=True)
    return x * lax.rsqrt(ms + EPS) * g


def _lower_bound(lbl_ref):
    l0 = lbl_ref[0:1, :]
    l1 = lbl_ref[1:2, :]
    m = jnp.maximum(l0, l1)
    e0 = jnp.exp(l0 - m)
    e1 = jnp.exp(l1 - m)
    return e0 / (e0 + e1)


def _in_proj(x, ng_ref, win_ref, z_ref):
    h = _rms(x, ng_ref[...]).astype(BF16)
    z_ref[...] = jnp.dot(h, win_ref[...], preferred_element_type=F32)


def _hgrn_inputs(z_ref, lb, hh):
    c = hh * DK
    q = z_ref[:, OFF_Q + c:OFF_Q + c + DK]
    f = z_ref[:, OFF_F + c:OFF_F + c + DK]
    v = z_ref[:, OFF_I + c:OFF_I + c + DK]
    lbh = lb[:, c:c + DK]
    fg = lbh + (1.0 - lbh) * jax.nn.sigmoid(f)
    logf = jnp.log(fg)
    k = 1.0 - fg
    qf = jax.nn.silu(q) * (DK ** -0.5)
    return qf, k, v, logf


def _group_cumsum(x, group):
    row = lax.broadcasted_iota(jnp.int32, x.shape, 0) & (group - 1)
    s = 1
    while s < group:
        x = x + jnp.where(row >= s, pltpu.roll(x, s, 0), 0.0)
        s *= 2
    return x


def _hgrn_epilogue(o, z_ref, hg_ref, hh, oa_ref):
    c = hh * DK
    ga = z_ref[:, OFF_GA + c:OFF_GA + c + DK]
    on = _rms(o, hg_ref[...]) * jax.nn.silu(ga)
    oa_ref[:, c:c + DK] = on.astype(BF16)


def _pool_epilogue(pooled, z_ref, g, wpool_ref, ps_ref, yb_ref):
    c = g * G_B
    mixed = jnp.dot(pooled.astype(BF16), wpool_ref[g], preferred_element_type=F32)
    gb = z_ref[:, OFF_GB + c:OFF_GB + c + G_B]
    yb = mixed * ps_ref[:, c:c + G_B] * jax.nn.silu(gb)
    yb_ref[:, c:c + G_B] = yb.astype(BF16)


def _out_stage(x, z_ref, oa_ref, yb_ref, wpa_ref, wpb_ref, wout_ref, fg_ref):
    y_a = jnp.dot(oa_ref[...], wpa_ref[...], preferred_element_type=F32)
    y_b = jnp.dot(yb_ref[...], wpb_ref[...], preferred_element_type=F32)
    m_a = z_ref[:, OFF_MA:OFF_MA + D_MODEL]
    m_b = z_ref[:, OFF_MB:OFF_MB + D_MODEL]
    merged = jax.nn.sigmoid(m_a) * y_a + jax.nn.sigmoid(m_b) * y_b
    out = x + jnp.dot(merged.astype(BF16), wout_ref[...], preferred_element_type=F32)
    return _rms(out, fg_ref[...])


def _prompt_tile(xn_ref, xb_ref, ng_ref, win_ref, lbl_ref, hg_ref, wpa_ref, wpool_ref, ps_ref,
                 wpb_ref, wout_ref, fg_ref, y_ref, s_out_ref, p_out_ref,
                 za_ref, z_ref, h_ref, hn_ref, st_ref, ext_ref, kx_ref, sn_ref, oa_ref, yb_ref,
                 *, tt):
    tm = TM_PROMPT
    nc = tm // CHUNK
    first = tt == 0
    col_tiles = iter(range(D_IN // PROJ_COLS))

    def project(n=1):
        for _ in range(n):
            j = next(col_tiles, None)
            if j is not None:
                c0 = j * PROJ_COLS
                za_ref[:, c0:c0 + PROJ_COLS] = jnp.dot(
                    h_ref[...], win_ref[:, c0:c0 + PROJ_COLS], preferred_element_type=F32)

    x = xb_ref[0]
    lb = _lower_bound(lbl_ref)

    ri = lax.broadcasted_iota(jnp.int32, (tm, tm), 0)
    ci = lax.broadcasted_iota(jnp.int32, (tm, tm), 1)
    causal = ((ri // CHUNK) == (ci // CHUNK)) & (ci <= ri)

    heads = range(N_HEADS)
    qs, ks, qd, vb, dec = [], [], [], [], []
    for hh in heads:
        project()
        qf, k, v, logf = _hgrn_inputs(z_ref, lb, hh)
        b = _group_cumsum(logf, CHUNK)
        b3 = b.reshape(nc, CHUNK, DK)
        ref = b3[:, CHUNK // 2:CHUNK // 2 + 1, :]
        bl = b3[:, CHUNK - 1:CHUNK, :]
        q3 = qf.reshape(nc, CHUNK, DK)
        k3 = k.reshape(nc, CHUNK, DK)
        qs.append((q3 * jnp.exp(b3 - ref)).reshape(tm, DK).astype(BF16))
        ks.append((k3 * jnp.exp(ref - b3)).reshape(tm, DK).astype(BF16))
        qd.append((q3 * jnp.exp(b3)).reshape(tm, DK).astype(BF16))
        kd = (k3 * jnp.exp(bl - b3)).reshape(tm, DK).astype(BF16)
        vb.append(v.astype(BF16))
        dec.append(jnp.exp(bl))
        for c in range(nc):
            kx_ref[hh, c * CHUNK:(c + 1) * CHUNK, c * DK:(c + 1) * DK] = (
                kd[c * CHUNK:(c + 1) * CHUNK, :])

    sc = [lax.dot_general(qs[hh], ks[hh], (((1,), (1,)), ((), ())), preferred_element_type=F32)
          for hh in heads]
    ut = [lax.dot_general(vb[hh], kx_ref[hh], (((0,), (0,)), ((), ())), preferred_element_type=F32)
          for hh in heads]
    project(2)
    o = [jnp.dot(jnp.where(causal, sc[hh], 0.0).astype(BF16), vb[hh], preferred_element_type=F32)
         for hh in heads]
    project(2)
    for hh in heads:
        st = jnp.where(first, 0.0, st_ref[hh])
        for c in range(nc):
            sn_ref[hh, c] = st.T.astype(BF16)
            st = dec[hh][c] * st + ut[hh][:, c * DK:(c + 1) * DK]
        st_ref[hh] = st
        s_out_ref[0, hh] = st.T
    project(2)
    for hh in heads:
        o_inter = [jnp.dot(qd[hh][c * CHUNK:(c + 1) * CHUNK, :], sn_ref[hh, c],
                           preferred_element_type=F32) for c in range(nc)]
        if hh % 2 == 1:
            project()
        _hgrn_epilogue(o[hh] + jnp.concatenate(o_inter, axis=0), z_ref, hg_ref, hh, oa_ref)

    hdr = 2 * SUBLANES
    ext_ref[0:hdr, :] = jnp.where(first, 0.0, ext_ref[tm:tm + hdr, :])
    ext_ref[hdr:hdr + tm, :] = z_ref[:, OFF_U:OFF_U + W_BRANCH]
    pos1 = tt * tm + lax.broadcasted_iota(jnp.int32, (tm, 1), 0) + 1
    for g, w in enumerate(POOL_WINDOWS):
        if g % 2 == 0:
            project()
        c = g * G_B
        s = ext_ref[:, c:c + G_B]
        sh = 1
        while sh < w:
            s = s + pltpu.roll(s, sh, 0)
            sh *= 2
        inv = jnp.where(pos1 >= w, 1.0 / w, 1.0 / pos1.astype(F32))
        pooled = s[hdr:, :] * inv - z_ref[:, OFF_U + c:OFF_U + c + G_B]
        _pool_epilogue(pooled, z_ref, g, wpool_ref, ps_ref, yb_ref)
    p_out_ref[0] = ext_ref[hdr + tm - POOL_BUF:hdr + tm, :]

    project()
    y_a = jnp.dot(oa_ref[...], wpa_ref[...], preferred_element_type=F32)
    y_b = jnp.dot(yb_ref[...], wpb_ref[...], preferred_element_type=F32)
    m_a = z_ref[:, OFF_MA:OFF_MA + D_MODEL]
    m_b = z_ref[:, OFF_MB:OFF_MB + D_MODEL]
    merged = jax.nn.sigmoid(m_a) * y_a + jax.nn.sigmoid(m_b) * y_b
    project(2)
    out = x + jnp.dot(merged.astype(BF16), wout_ref[...], preferred_element_type=F32)
    hn_ref[...] = _rms(xn_ref[0], ng_ref[...]).astype(BF16)
    project(D_IN // PROJ_COLS)
    y_ref[0] = _rms(out, fg_ref[...])


def _prompt_kernel(xn_ref, xb_ref, ng_ref, win_ref, lbl_ref, hg_ref, wpa_ref, wpool_ref, ps_ref,
                   wpb_ref, wout_ref, fg_ref,
                   y_ref, s_out_ref, p_out_ref,
                   z0_ref, z1_ref, h0_ref, h1_ref, st_ref, ext_ref, kx_ref, sn_ref, oa_ref, yb_ref,
                   *, nt):
    s = pl.program_id(0)

    @pl.when(s == 0)
    def _():
        z1_ref[...] = jnp.zeros_like(z1_ref)
        h0_ref[...] = jnp.zeros_like(h0_ref)
        st_ref[...] = jnp.zeros_like(st_ref)
        ext_ref[...] = jnp.zeros_like(ext_ref)
        kx_ref[...] = jnp.zeros_like(kx_ref)

    tt = lax.rem(jnp.maximum(s - 2, 0), nt)
    tile = functools.partial(
        _prompt_tile, xn_ref, xb_ref, ng_ref, win_ref, lbl_ref, hg_ref, wpa_ref, wpool_ref,
        ps_ref, wpb_ref, wout_ref, fg_ref, y_ref, s_out_ref, p_out_ref)
    carried = (st_ref, ext_ref, kx_ref, sn_ref, oa_ref, yb_ref)

    @pl.when(lax.rem(s, 2) == 0)
    def _():
        tile(z0_ref, z1_ref, h0_ref, h1_ref, *carried, tt=tt)

    @pl.when(lax.rem(s, 2) == 1)
    def _():
        tile(z1_ref, z0_ref, h1_ref, h0_ref, *carried, tt=tt)


def _sample_kernel(x_ref, s_in_ref, p_in_ref, ng_ref, win_ref, lbl_ref, hg_ref, wpa_ref,
                   wpool_ref, ps_ref, wpb_ref, wout_ref, fg_ref,
                   y_ref, s_out_ref, p_out_ref,
                   xs_ref, z_ref, e_ref, a_ref, bm_ref, qd_ref, o_ref, oa_ref, yb_ref,
                   *, dec_seq):
    bb = BB_SAMPLE
    rs = ROWS_PER_SEQ
    rows = bb * rs

    @pl.when(pl.program_id(0) == 0)
    def _():
        xs_ref[...] = jnp.zeros_like(xs_ref)
        e_ref[...] = jnp.zeros_like(e_ref)

    for i in range(bb):
        xs_ref[i * rs:i * rs + dec_seq, :] = x_ref[i]
    x = xs_ref[...]
    _in_proj(x, ng_ref, win_ref, z_ref)
    lb = _lower_bound(lbl_ref)

    ri = lax.broadcasted_iota(jnp.int32, (rows, rows), 0)
    ci = lax.broadcasted_iota(jnp.int32, (rows, rows), 1)
    causal = ((ri // rs) == (ci // rs)) & (ci <= ri)
    r8 = lax.broadcasted_iota(jnp.int32, (rows, DK), 0) & (rs - 1)
    ones_rows = jnp.where((r8 >= dec_seq) & (r8 < dec_seq + 3), 1.0, 0.0).astype(BF16)

    for hh in range(N_HEADS):
        qf, k, v, logf = _hgrn_inputs(z_ref, lb, hh)
        b = _group_cumsum(logf, rs)
        b3 = b.reshape(bb, rs, DK)
        ref = b3[:, dec_seq // 2:dec_seq // 2 + 1, :]
        bl = b3[:, dec_seq - 1:dec_seq, :]
        q3 = qf.reshape(bb, rs, DK)
        k3 = k.reshape(bb, rs, DK)
        qs = (q3 * jnp.exp(b3 - ref)).reshape(rows, DK).astype(BF16)
        ks = (k3 * jnp.exp(ref - b3)).reshape(rows, DK).astype(BF16)
        qd = (q3 * jnp.exp(b3)).reshape(rows, DK).astype(BF16)
        kd = (k3 * jnp.exp(bl - b3)).reshape(rows, DK)
        vb = v.astype(BF16)

        sc = lax.dot_general(qs, ks, (((1,), (1,)), ((), ())), preferred_element_type=F32)
        sc = jnp.where(causal, sc, 0.0)
        o_ref[...] = jnp.dot(sc.astype(BF16), vb, preferred_element_type=F32)
        qd_ref[...] = qd

        dec = jnp.broadcast_to(jnp.exp(bl), (bb, rs, DK)).reshape(rows, DK)
        d1 = dec.astype(BF16).astype(F32)
        d2 = (dec - d1).astype(BF16).astype(F32)
        d3 = (dec - d1 - d2).astype(BF16).astype(F32)
        a = jnp.where(r8 < dec_seq, kd,
                      jnp.where(r8 == dec_seq, d1,
                                jnp.where(r8 == dec_seq + 1, d2,
                                          jnp.where(r8 == dec_seq + 2, d3, 0.0))))
        a_ref[...] = a.astype(BF16)
        bm_ref[:, 0:DV] = jnp.where(r8 < dec_seq, vb, jnp.zeros_like(vb))
        bm_ref[:, DV:2 * DV] = ones_rows

        for i in range(bb):
            s_old = s_in_ref[i, hh]
            qd_i = qd_ref[i * rs:(i + 1) * rs, :]
            o_ref[i * rs:(i + 1) * rs, :] += jnp.dot(qd_i, s_old.astype(BF16),
                                                     preferred_element_type=F32)
            upd = lax.dot_general(a_ref[i * rs:(i + 1) * rs, :], bm_ref[i * rs:(i + 1) * rs, :],
                                  (((0,), (0,)), ((), ())), preferred_element_type=F32)
            s_out_ref[i, hh] = upd[:, DV:2 * DV] * s_old + upd[:, 0:DV]

        _hgrn_epilogue(o_ref[...], z_ref, hg_ref, hh, oa_ref)

    e_ref[:, 0:POOL_BUF, :] = p_in_ref[...]
    u_all = z_ref[:, OFF_U:OFF_U + W_BRANCH]
    nrow = 2 * rs
    jrow = lax.broadcasted_iota(jnp.int32, (bb, nrow, G_B), 1)
    trow = lax.broadcasted_iota(jnp.int32, (bb, rs, G_B), 1)
    for g, w in enumerate(POOL_WINDOWS):
        c = g * G_B
        sc_ = e_ref[:, :, c:c + G_B]
        sh = 1
        while sh < nrow:
            sc_ = sc_ + jnp.where(jrow + sh < nrow, pltpu.roll(sc_, nrow - sh, 1), 0.0)
            sh *= 2
        rw = pltpu.roll(sc_, w % nrow, 1)[:, 0:rs, :]
        rw = jnp.where(trow < w, rw, 0.0).reshape(rows, G_B)
        u = u_all[:, c:c + G_B]
        cu = u
        sh = 1
        while sh < min(w, dec_seq):
            cu = cu + jnp.where(r8 >= sh, pltpu.roll(cu, sh, 0), 0.0)
            sh *= 2
        pooled = (rw + cu) * (1.0 / w) - u
        _pool_epilogue(pooled, z_ref, g, wpool_ref, ps_ref, yb_ref)

    keep = POOL_BUF - dec_seq
    jrow_w = lax.broadcasted_iota(jnp.int32, (bb, nrow, W_BRANCH), 1)
    old = pltpu.roll(e_ref[...], nrow - dec_seq, 1)
    u3 = u_all.reshape(bb, rs, W_BRANCH)
    new = pltpu.roll(jnp.concatenate([u3, jnp.zeros_like(u3)], axis=1), keep, 1)
    nb = jnp.where(jrow_w < keep, old, new)
    p_out_ref[...] = nb[:, 0:POOL_BUF, :]

    y = _out_stage(x, z_ref, oa_ref, yb_ref, wpa_ref, wpb_ref, wout_ref, fg_ref)
    for i in range(bb):
        y_ref[i] = y[i * rs:i * rs + dec_seq, :]


def _const_spec(shape):
    n = len(shape)
    return pl.BlockSpec(shape, lambda *_: (0,) * n, pipeline_mode=pl.Buffered(1))


def _weight_specs():
    return [
        _const_spec((1, D_MODEL)),
        _const_spec((D_MODEL, D_IN)),
        _const_spec((2, W_BRANCH)),
        _const_spec((1, DV)),
        _const_spec((W_BRANCH, D_MODEL)),
        _const_spec((len(POOL_WINDOWS), G_B, G_B)),
        _const_spec((1, W_BRANCH)),
        _const_spec((W_BRANCH, D_MODEL)),
        _const_spec((D_MODEL, D_MODEL)),
        _const_spec((1, D_MODEL)),
    ]


def kernel(x_prompt, x_sample, state_hgrn, state_pool, norm_g, w_in, lb_logits, hgrn_norm_g,
           w_proj_a, w_pool, pool_scale, w_proj_b, w_out, final_norm_g):
    batch, seq, _ = x_prompt.shape
    dec_batch, dec_seq, _ = x_sample.shape
    assert norm_g.shape[0] == 1 and lb_logits.shape[0] == 2, "single-layer decoder only"
    assert seq % TM_PROMPT == 0 and dec_batch % BB_SAMPLE == 0
    assert dec_seq + 3 <= ROWS_PER_SEQ and dec_seq < CHUNK and PAST_LEN >= max(POOL_WINDOWS)

    weights = (norm_g, w_in[0].astype(BF16), lb_logits, hgrn_norm_g,
               w_proj_a[0].astype(BF16), w_pool[0].astype(BF16), pool_scale,
               w_proj_b[0].astype(BF16), w_out[0].astype(BF16), final_norm_g.reshape(1, D_MODEL))

    tm = TM_PROMPT
    nt = seq // tm
    n_tiles = batch * nt

    def normalized(s):
        j = jnp.minimum(s, n_tiles - 1)
        return j // nt, j % nt

    def finished(s):
        j = jnp.maximum(s - 2, 0)
        return j // nt, j % nt

    y_p, s_p, p_p = pl.pallas_call(
        functools.partial(_prompt_kernel, nt=nt),
        grid=(n_tiles + 2,),
        in_specs=[pl.BlockSpec((1, tm, D_MODEL), lambda s: (*normalized(s), 0)),
                  pl.BlockSpec((1, tm, D_MODEL), lambda s: (*finished(s), 0))] + _weight_specs(),
        out_specs=[
            pl.BlockSpec((1, tm, D_MODEL), lambda s: (*finished(s), 0)),
            pl.BlockSpec((1, N_HEADS, DK, DV), lambda s: (finished(s)[0], 0, 0, 0)),
            pl.BlockSpec((1, POOL_BUF, W_BRANCH), lambda s: (finished(s)[0], 0, 0)),
        ],
        out_shape=[
            jax.ShapeDtypeStruct((batch, seq, D_MODEL), F32),
            jax.ShapeDtypeStruct((batch, N_HEADS, DK, DV), F32),
            jax.ShapeDtypeStruct((batch, POOL_BUF, W_BRANCH), F32),
        ],
        scratch_shapes=[
            pltpu.VMEM((tm, D_IN), F32),
            pltpu.VMEM((tm, D_IN), F32),
            pltpu.VMEM((tm, D_MODEL), BF16),
            pltpu.VMEM((tm, D_MODEL), BF16),
            pltpu.VMEM((N_HEADS, DV, DK), F32),
            pltpu.VMEM((tm + 2 * SUBLANES, W_BRANCH), F32),
            pltpu.VMEM((N_HEADS, tm, (tm // CHUNK) * DK), BF16),
            pltpu.VMEM((N_HEADS, tm // CHUNK, DK, DV), BF16),
            pltpu.VMEM((tm, W_BRANCH), BF16),
            pltpu.VMEM((tm, W_BRANCH), BF16),
        ],
        compiler_params=pltpu.CompilerParams(
            dimension_semantics=("arbitrary",),
            vmem_limit_bytes=VMEM_LIMIT_BYTES),
        name="hgrn2_pool_prompt",
    )(x_prompt, x_prompt, *weights)

    bb = BB_SAMPLE
    rows = bb * ROWS_PER_SEQ
    y_s, s_s, p_s = pl.pallas_call(
        functools.partial(_sample_kernel, dec_seq=dec_seq),
        grid=(dec_batch // bb,),
        in_specs=[
            pl.BlockSpec((bb, dec_seq, D_MODEL), lambda i: (i, 0, 0)),
            pl.BlockSpec((bb, N_HEADS, DK, DV), lambda i: (i, 0, 0, 0)),
            pl.BlockSpec((bb, POOL_BUF, W_BRANCH), lambda i: (i, 0, 0)),
        ] + _weight_specs(),
        out_specs=[
            pl.BlockSpec((bb, dec_seq, D_MODEL), lambda i: (i, 0, 0)),
            pl.BlockSpec((bb, N_HEADS, DK, DV), lambda i: (i, 0, 0, 0)),
            pl.BlockSpec((bb, POOL_BUF, W_BRANCH), lambda i: (i, 0, 0)),
        ],
        out_shape=[
            jax.ShapeDtypeStruct((dec_batch, dec_seq, D_MODEL), F32),
            jax.ShapeDtypeStruct((dec_batch, N_HEADS, DK, DV), F32),
            jax.ShapeDtypeStruct((dec_batch, POOL_BUF, W_BRANCH), F32),
        ],
        scratch_shapes=[
            pltpu.VMEM((rows, D_MODEL), F32),
            pltpu.VMEM((rows, D_IN), F32),
            pltpu.VMEM((bb, 2 * ROWS_PER_SEQ, W_BRANCH), F32),
            pltpu.VMEM((rows, DK), BF16),
            pltpu.VMEM((rows, 2 * DV), BF16),
            pltpu.VMEM((rows, DK), BF16),
            pltpu.VMEM((rows, DV), F32),
            pltpu.VMEM((rows, W_BRANCH), BF16),
            pltpu.VMEM((rows, W_BRANCH), BF16),
        ],
        compiler_params=pltpu.CompilerParams(
            dimension_semantics=("arbitrary",),
            vmem_limit_bytes=VMEM_LIMIT_BYTES),
        name="hgrn2_pool_sample",
    )(x_sample, state_hgrn[0], state_pool[0], *weights)

    return (y_p, y_s, s_p[None], p_p[None], s_s[None], p_s[None])
```

```python
import functools

import jax
import jax.numpy as jnp
from jax import lax
from jax.experimental import pallas as pl
from jax.experimental.pallas import tpu as pltpu

F32 = jnp.float32
BF16 = jnp.bfloat16

D_MODEL = 1024
W_BRANCH = 512
N_HEADS = 4
DK = 128
DV = 128
CHUNK = 32
POOL_WINDOWS = (2, 4, 8, 16)
G_B = 128
POOL_BUF = 15
PAST_LEN = 16384
EPS = 1e-6
D_IN = 4 * W_BRANCH + 2 * W_BRANCH + 2 * D_MODEL
OFF_Q, OFF_F, OFF_I, OFF_GA = 0, 512, 1024, 1536
OFF_U, OFF_GB, OFF_MA, OFF_MB = 2048, 2560, 3072, 4096

SUBLANES = 8
VMEM_LIMIT_BYTES = 56 * 1024 * 1024

TM_PROMPT = 512
SUB_TILE = 256
BB_SAMPLE = 16
ROWS_PER_SEQ = SUBLANES


def _rms(x, g):
    ms = jnp.mean(x * x, axis=-1, keepdims=True)
    return x * lax.rsqrt(ms + EPS) * g


def _lower_bound(lbl_ref):
    l0 = lbl_ref[0:1, :]
    l1 = lbl_ref[1:2, :]
    m = jnp.maximum(l0, l1)
    e0 = jnp.exp(l0 - m)
    e1 = jnp.exp(l1 - m)
    return e0 / (e0 + e1)


def _in_proj(x, ng_ref, win_ref, z_ref):
    h = _rms(x, ng_ref[...]).astype(BF16)
    z_ref[...] = jnp.dot(h, win_ref[...], preferred_element_type=F32)


def _hgrn_inputs(z_ref, lb, hh):
    c = hh * DK
    q = z_ref[:, OFF_Q + c:OFF_Q + c + DK]
    f = z_ref[:, OFF_F + c:OFF_F + c + DK]
    v = z_ref[:, OFF_I + c:OFF_I + c + DK]
    lbh = lb[:, c:c + DK]
    fg = lbh + (1.0 - lbh) * jax.nn.sigmoid(f)
    logf = jnp.log(fg)
    k = 1.0 - fg
    qf = jax.nn.silu(q) * (DK ** -0.5)
    return qf, k, v, logf


def _group_cumsum(x, group):
    row = lax.broadcasted_iota(jnp.int32, x.shape, 0) & (group - 1)
    s = 1
    while s < group:
        x = x + jnp.where(row >= s, pltpu.roll(x, s, 0), 0.0)
        s *= 2
    return x


def _hgrn_epilogue(o, z_ref, hg_ref, hh, oa_ref):
    c = hh * DK
    ga = z_ref[:, OFF_GA + c:OFF_GA + c + DK]
    on = _rms(o, hg_ref[...]) * jax.nn.silu(ga)
    oa_ref[:, c:c + DK] = on.astype(BF16)


def _pool_epilogue(pooled, z_ref, g, wpool_ref, ps_ref, yb_ref):
    c = g * G_B
    mixed = jnp.dot(pooled.astype(BF16), wpool_ref[g], preferred_element_type=F32)
    gb = z_ref[:, OFF_GB + c:OFF_GB + c + G_B]
    yb = mixed * ps_ref[:, c:c + G_B] * jax.nn.silu(gb)
    yb_ref[:, c:c + G_B] = yb.astype(BF16)


def _out_stage(x, z_ref, oa_ref, yb_ref, wpa_ref, wpb_ref, wout_ref, fg_ref):
    y_a = jnp.dot(oa_ref[...], wpa_ref[...], preferred_element_type=F32)
    y_b = jnp.dot(yb_ref[...], wpb_ref[...], preferred_element_type=F32)
    m_a = z_ref[:, OFF_MA:OFF_MA + D_MODEL]
    m_b = z_ref[:, OFF_MB:OFF_MB + D_MODEL]
    merged = jax.nn.sigmoid(m_a) * y_a + jax.nn.sigmoid(m_b) * y_b
    out = x + jnp.dot(merged.astype(BF16), wout_ref[...], preferred_element_type=F32)
    return _rms(out, fg_ref[...])


def _prompt_kernel(x_ref, ng_ref, win_ref, lbl_ref, hg_ref, wpa_ref, wpool_ref, ps_ref,
                   wpb_ref, wout_ref, fg_ref,
                   y_ref, s_out_ref, p_out_ref,
                   z_ref, st_ref, ext_ref, kx_ref, sn_ref, qs_ref, ks_ref, qd_ref, vb_ref,
                   oa_ref, yb_ref):
    tm = TM_PROMPT
    sub = SUB_TILE
    n_sub = tm // sub
    nc = tm // CHUNK
    ncs = sub // CHUNK
    t = pl.program_id(1)
    first = t == 0
    hdr = 2 * SUBLANES

    @pl.when((pl.program_id(0) == 0) & first)
    def _():
        st_ref[...] = jnp.zeros_like(st_ref)
        ext_ref[...] = jnp.zeros_like(ext_ref)
        kx_ref[...] = jnp.zeros_like(kx_ref)

    x = x_ref[0]
    _in_proj(x, ng_ref, win_ref, z_ref)
    lb = _lower_bound(lbl_ref)

    ri = lax.broadcasted_iota(jnp.int32, (sub, sub), 0)
    ci = lax.broadcasted_iota(jnp.int32, (sub, sub), 1)
    causal = ((ri // CHUNK) == (ci // CHUNK)) & (ci <= ri)

    heads = range(N_HEADS)
    units = [(hh, s) for hh in heads for s in range(n_sub)]

    def rows(s):
        return slice(s * sub, (s + 1) * sub)

    def cols(hh):
        return slice(hh * DK, (hh + 1) * DK)

    dec = []
    for hh in heads:
        qf, k, v, logf = _hgrn_inputs(z_ref, lb, hh)
        b = _group_cumsum(logf, CHUNK)
        b3 = b.reshape(nc, CHUNK, DK)
        ref = b3[:, CHUNK // 2:CHUNK // 2 + 1, :]
        bl = b3[:, CHUNK - 1:CHUNK, :]
        q3 = qf.reshape(nc, CHUNK, DK)
        k3 = k.reshape(nc, CHUNK, DK)
        qs_ref[:, cols(hh)] = (q3 * jnp.exp(b3 - ref)).reshape(tm, DK).astype(BF16)
        ks_ref[:, cols(hh)] = (k3 * jnp.exp(ref - b3)).reshape(tm, DK).astype(BF16)
        qd_ref[:, cols(hh)] = (q3 * jnp.exp(b3)).reshape(tm, DK).astype(BF16)
        vb_ref[:, cols(hh)] = v.astype(BF16)
        kd = (k3 * jnp.exp(bl - b3)).reshape(tm, DK).astype(BF16)
        dec.append(jnp.exp(bl))
        for c in range(nc):
            cc = c % ncs
            kx_ref[hh, c * CHUNK:(c + 1) * CHUNK, cc * DK:(cc + 1) * DK] = (
                kd[c * CHUNK:(c + 1) * CHUNK, :])

    sc = {(hh, s): lax.dot_general(qs_ref[rows(s), cols(hh)], ks_ref[rows(s), cols(hh)],
                                   (((1,), (1,)), ((), ())), preferred_element_type=F32)
          for hh, s in units}
    ut = {(hh, s): lax.dot_general(vb_ref[rows(s), cols(hh)], kx_ref[hh, rows(s), :],
                                   (((0,), (0,)), ((), ())), preferred_element_type=F32)
          for hh, s in units}
    o = {(hh, s): jnp.dot(jnp.where(causal, sc[hh, s], 0.0).astype(BF16),
                          vb_ref[rows(s), cols(hh)], preferred_element_type=F32)
         for hh, s in units}
    for hh in heads:
        st = jnp.where(first, 0.0, st_ref[hh])
        for c in range(nc):
            sn_ref[hh, c] = st.T.astype(BF16)
            st = dec[hh][c] * st + ut[hh, c // ncs][:, (c % ncs) * DK:(c % ncs + 1) * DK]
        st_ref[hh] = st
        s_out_ref[0, hh] = st.T
    for hh in heads:
        o_inter = [jnp.dot(qd_ref[c * CHUNK:(c + 1) * CHUNK, cols(hh)], sn_ref[hh, c],
                           preferred_element_type=F32) for c in range(nc)]
        o_hh = jnp.concatenate([o[hh, s] for s in range(n_sub)], axis=0)
        _hgrn_epilogue(o_hh + jnp.concatenate(o_inter, axis=0), z_ref, hg_ref, hh, oa_ref)

    ext_ref[0:hdr, :] = jnp.where(first, 0.0, ext_ref[tm:tm + hdr, :])
    ext_ref[hdr:hdr + tm, :] = z_ref[:, OFF_U:OFF_U + W_BRANCH]
    pos1 = t * tm + lax.broadcasted_iota(jnp.int32, (tm, 1), 0) + 1
    for g, w in enumerate(POOL_WINDOWS):
        c = g * G_B
        s = ext_ref[:, c:c + G_B]
        sh = 1
        while sh < w:
            s = s + pltpu.roll(s, sh, 0)
            sh *= 2
        inv = jnp.where(pos1 >= w, 1.0 / w, 1.0 / pos1.astype(F32))
        pooled = s[hdr:, :] * inv - z_ref[:, OFF_U + c:OFF_U + c + G_B]
        _pool_epilogue(pooled, z_ref, g, wpool_ref, ps_ref, yb_ref)
    p_out_ref[0] = ext_ref[hdr + tm - POOL_BUF:hdr + tm, :]

    y_ref[0] = _out_stage(x, z_ref, oa_ref, yb_ref, wpa_ref, wpb_ref, wout_ref, fg_ref)


def _sample_kernel(x_ref, s_in_ref, p_in_ref, ng_ref, win_ref, lbl_ref, hg_ref, wpa_ref,
                   wpool_ref, ps_ref, wpb_ref, wout_ref, fg_ref,
                   y_ref, s_out_ref, p_out_ref,
                   xs_ref, z_ref, e_ref, a_ref, bm_ref, qd_ref, o_ref, oa_ref, yb_ref,
                   *, dec_seq):
    bb = BB_SAMPLE
    rs = ROWS_PER_SEQ
    rows = bb * rs

    @pl.when(pl.program_id(0) == 0)
    def _():
        xs_ref[...] = jnp.zeros_like(xs_ref)
        e_ref[...] = jnp.zeros_like(e_ref)

    for i in range(bb):
        xs_ref[i * rs:i * rs + dec_seq, :] = x_ref[i]
    x = xs_ref[...]
    _in_proj(x, ng_ref, win_ref, z_ref)
    lb = _lower_bound(lbl_ref)

    ri = lax.broadcasted_iota(jnp.int32, (rows, rows), 0)
    ci = lax.broadcasted_iota(jnp.int32, (rows, rows), 1)
    causal = ((ri // rs) == (ci // rs)) & (ci <= ri)
    r8 = lax.broadcasted_iota(jnp.int32, (rows, DK), 0) & (rs - 1)
    ones_rows = jnp.where((r8 >= dec_seq) & (r8 < dec_seq + 3), 1.0, 0.0).astype(BF16)

    for hh in range(N_HEADS):
        qf, k, v, logf = _hgrn_inputs(z_ref, lb, hh)
        b = _group_cumsum(logf, rs)
        b3 = b.reshape(bb, rs, DK)
        ref = b3[:, dec_seq // 2:dec_seq // 2 + 1, :]
        bl = b3[:, dec_seq - 1:dec_seq, :]
        q3 = qf.reshape(bb, rs, DK)
        k3 = k.reshape(bb, rs, DK)
        qs = (q3 * jnp.exp(b3 - ref)).reshape(rows, DK).astype(BF16)
        ks = (k3 * jnp.exp(ref - b3)).reshape(rows, DK).astype(BF16)
        qd = (q3 * jnp.exp(b3)).reshape(rows, DK).astype(BF16)
        kd = (k3 * jnp.exp(bl - b3)).reshape(rows, DK)
        vb = v.astype(BF16)

        sc = lax.dot_general(qs, ks, (((1,), (1,)), ((), ())), preferred_element_type=F32)
        sc = jnp.where(causal, sc, 0.0)
        o_ref[...] = jnp.dot(sc.astype(BF16), vb, preferred_element_type=F32)
        qd_ref[...] = qd

        dec = jnp.broadcast_to(jnp.exp(bl), (bb, rs, DK)).reshape(rows, DK)
        d1 = dec.astype(BF16).astype(F32)
        d2 = (dec - d1).astype(BF16).astype(F32)
        d3 = (dec - d1 - d2).astype(BF16).astype(F32)
        a = jnp.where(r8 < dec_seq, kd,
                      jnp.where(r8 == dec_seq, d1,
                                jnp.where(r8 == dec_seq + 1, d2,
                                          jnp.where(r8 == dec_seq + 2, d3, 0.0))))
        a_ref[...] = a.astype(BF16)
        bm_ref[:, 0:DV] = jnp.where(r8 < dec_seq, vb, jnp.zeros_like(vb))
        bm_ref[:, DV:2 * DV] = ones_rows

        for i in range(bb):
            s_old = s_in_ref[i, hh]
            qd_i = qd_ref[i * rs:(i + 1) * rs, :]
            o_ref[i * rs:(i + 1) * rs, :] += jnp.dot(qd_i, s_old.astype(BF16),
                                                     preferred_element_type=F32)
            upd = lax.dot_general(a_ref[i * rs:(i + 1) * rs, :], bm_ref[i * rs:(i + 1) * rs, :],
                                  (((0,), (0,)), ((), ())), preferred_element_type=F32)
            s_out_ref[i, hh] = upd[:, DV:2 * DV] * s_old + upd[:, 0:DV]

        _hgrn_epilogue(o_ref[...], z_ref, hg_ref, hh, oa_ref)

    e_ref[:, 0:POOL_BUF, :] = p_in_ref[...]
    u_all = z_ref[:, OFF_U:OFF_U + W_BRANCH]
    nrow = 2 * rs
    jrow = lax.broadcasted_iota(jnp.int32, (bb, nrow, G_B), 1)
    trow = lax.broadcasted_iota(jnp.int32, (bb, rs, G_B), 1)
    for g, w in enumerate(POOL_WINDOWS):
        c = g * G_B
        sc_ = e_ref[:, :, c:c + G_B]
        sh = 1
        while sh < nrow:
            sc_ = sc_ + jnp.where(jrow + sh < nrow, pltpu.roll(sc_, nrow - sh, 1), 0.0)
            sh *= 2
        rw = pltpu.roll(sc_, w % nrow, 1)[:, 0:rs, :]
        rw = jnp.where(trow < w, rw, 0.0).reshape(rows, G_B)
        u = u_all[:, c:c + G_B]
        cu = u
        sh = 1
        while sh < min(w, dec_seq):
            cu = cu + jnp.where(r8 >= sh, pltpu.roll(cu, sh, 0), 0.0)
            sh *= 2
        pooled = (rw + cu) * (1.0 / w) - u
        _pool_epilogue(pooled, z_ref, g, wpool_ref, ps_ref, yb_ref)

    keep = POOL_BUF - dec_seq
    jrow_w = lax.broadcasted_iota(jnp.int32, (bb, nrow, W_BRANCH), 1)
    old = pltpu.roll(e_ref[...], nrow - dec_seq, 1)
    u3 = u_all.reshape(bb, rs, W_BRANCH)
    new = pltpu.roll(jnp.concatenate([u3, jnp.zeros_like(u3)], axis=1), keep, 1)
    nb = jnp.where(jrow_w < keep, old, new)
    p_out_ref[...] = nb[:, 0:POOL_BUF, :]

    y = _out_stage(x, z_ref, oa_ref, yb_ref, wpa_ref, wpb_ref, wout_ref, fg_ref)
    for i in range(bb):
        y_ref[i] = y[i * rs:i * rs + dec_seq, :]


def _const_spec(shape):
    n = len(shape)
    return pl.BlockSpec(shape, lambda *_: (0,) * n, pipeline_mode=pl.Buffered(1))


def _weight_specs():
    return [
        _const_spec((1, D_MODEL)),
        _const_spec((D_MODEL, D_IN)),
        _const_spec((2, W_BRANCH)),
        _const_spec((1, DV)),
        _const_spec((W_BRANCH, D_MODEL)),
        _const_spec((len(POOL_WINDOWS), G_B, G_B)),
        _const_spec((1, W_BRANCH)),
        _const_spec((W_BRANCH, D_MODEL)),
        _const_spec((D_MODEL, D_MODEL)),
        _const_spec((1, D_MODEL)),
    ]


def kernel(x_prompt, x_sample, state_hgrn, state_pool, norm_g, w_in, lb_logits, hgrn_norm_g,
           w_proj_a, w_pool, pool_scale, w_proj_b, w_out, final_norm_g):
    batch, seq, _ = x_prompt.shape
    dec_batch, dec_seq, _ = x_sample.shape
    assert norm_g.shape[0] == 1 and lb_logits.shape[0] == 2, "single-layer decoder only"
    assert seq % TM_PROMPT == 0 and TM_PROMPT % SUB_TILE == 0 and dec_batch % BB_SAMPLE == 0
    assert dec_seq + 3 <= ROWS_PER_SEQ and dec_seq < CHUNK and PAST_LEN >= max(POOL_WINDOWS)

    weights = (norm_g, w_in[0].astype(BF16), lb_logits, hgrn_norm_g,
               w_proj_a[0].astype(BF16), w_pool[0].astype(BF16), pool_scale,
               w_proj_b[0].astype(BF16), w_out[0].astype(BF16), final_norm_g.reshape(1, D_MODEL))

    tm = TM_PROMPT
    nt = seq // tm
    y_p, s_p, p_p = pl.pallas_call(
        _prompt_kernel,
        grid=(batch, nt),
        in_specs=[pl.BlockSpec((1, tm, D_MODEL), lambda b, t: (b, t, 0))] + _weight_specs(),
        out_specs=[
            pl.BlockSpec((1, tm, D_MODEL), lambda b, t: (b, t, 0)),
            pl.BlockSpec((1, N_HEADS, DK, DV), lambda b, t: (b, 0, 0, 0)),
            pl.BlockSpec((1, POOL_BUF, W_BRANCH), lambda b, t: (b, 0, 0)),
        ],
        out_shape=[
            jax.ShapeDtypeStruct((batch, seq, D_MODEL), F32),
            jax.ShapeDtypeStruct((batch, N_HEADS, DK, DV), F32),
            jax.ShapeDtypeStruct((batch, POOL_BUF, W_BRANCH), F32),
        ],
        scratch_shapes=[
            pltpu.VMEM((tm, D_IN), F32),
            pltpu.VMEM((N_HEADS, DV, DK), F32),
            pltpu.VMEM((tm + 2 * SUBLANES, W_BRANCH), F32),
            pltpu.VMEM((N_HEADS, tm, (SUB_TILE // CHUNK) * DK), BF16),
            pltpu.VMEM((N_HEADS, tm // CHUNK, DK, DV), BF16),
            pltpu.VMEM((tm, W_BRANCH), BF16),
            pltpu.VMEM((tm, W_BRANCH), BF16),
            pltpu.VMEM((tm, W_BRANCH), BF16),
            pltpu.VMEM((tm, W_BRANCH), BF16),
            pltpu.VMEM((tm, W_BRANCH), BF16),
            pltpu.VMEM((tm, W_BRANCH), BF16),
        ],
        compiler_params=pltpu.CompilerParams(
            dimension_semantics=("arbitrary", "arbitrary"),
            vmem_limit_bytes=VMEM_LIMIT_BYTES),
        name="hgrn2_pool_prompt",
    )(x_prompt, *weights)

    bb = BB_SAMPLE
    rows = bb * ROWS_PER_SEQ
    y_s, s_s, p_s = pl.pallas_call(
        functools.partial(_sample_kernel, dec_seq=dec_seq),
        grid=(dec_batch // bb,),
        in_specs=[
            pl.BlockSpec((bb, dec_seq, D_MODEL), lambda i: (i, 0, 0)),
            pl.BlockSpec((bb, N_HEADS, DK, DV), lambda i: (i, 0, 0, 0)),
            pl.BlockSpec((bb, POOL_BUF, W_BRANCH), lambda i: (i, 0, 0)),
        ] + _weight_specs(),
        out_specs=[
            pl.BlockSpec((bb, dec_seq, D_MODEL), lambda i: (i, 0, 0)),
            pl.BlockSpec((bb, N_HEADS, DK, DV), lambda i: (i, 0, 0, 0)),
            pl.BlockSpec((bb, POOL_BUF, W_BRANCH), lambda i: (i, 0, 0)),
        ],
        out_shape=[
            jax.ShapeDtypeStruct((dec_batch, dec_seq, D_MODEL), F32),
            jax.ShapeDtypeStruct((dec_batch, N_HEADS, DK, DV), F32),
            jax.ShapeDtypeStruct((dec_batch, POOL_BUF, W_BRANCH), F32),
        ],
        scratch_shapes=[
            pltpu.VMEM((rows, D_MODEL), F32),
            pltpu.VMEM((rows, D_IN), F32),
            pltpu.VMEM((bb, 2 * ROWS_PER_SEQ, W_BRANCH), F32),
            pltpu.VMEM((rows, DK), BF16),
            pltpu.VMEM((rows, 2 * DV), BF16),
            pltpu.VMEM((rows, DK), BF16),
            pltpu.VMEM((rows, DV), F32),
            pltpu.VMEM((rows, W_BRANCH), BF16),
            pltpu.VMEM((rows, W_BRANCH), BF16),
        ],
        compiler_params=pltpu.CompilerParams(
            dimension_semantics=("arbitrary",),
            vmem_limit_bytes=VMEM_LIMIT_BYTES),
        name="hgrn2_pool_sample",
    )(x_sample, state_hgrn[0], state_pool[0], *weights)

    return (y_p, y_s, s_p[None], p_p[None], s_s[None], p_s[None])
```

```python
import functools

import jax
import jax.numpy as jnp
from jax import lax
from jax.experimental import pallas as pl
from jax.experimental.pallas import tpu as pltpu

F32 = jnp.float32
BF16 = jnp.bfloat16

D_MODEL = 1024
W_BRANCH = 512
N_HEADS = 4
DK = 128
DV = 128
CHUNK = 32
POOL_WINDOWS = (2, 4, 8, 16)
G_B = 128
POOL_BUF = 15
PAST_LEN = 16384
EPS = 1e-6
D_IN = 4 * W_BRANCH + 2 * W_BRANCH + 2 * D_MODEL
OFF_Q, OFF_F, OFF_I, OFF_GA = 0, 512, 1024, 1536
OFF_U, OFF_GB, OFF_MA, OFF_MB = 2048, 2560, 3072, 4096

SUBLANES = 8
VMEM_LIMIT_BYTES = 56 * 1024 * 1024

TM_PROMPT = 512
SUB_TILE = 256
BB_SAMPLE = 16
ROWS_PER_SEQ = SUBLANES


def _rms(x, g):
    ms = jnp.mean(x * x, axis=-1, keepdims=True)
    return x * lax.rsqrt(ms + EPS) * g


def _lower_bound(lbl_ref):
    l0 = lbl_ref[0:1, :]
    l1 = lbl_ref[1:2, :]
    m = jnp.maximum(l0, l1)
    e0 = jnp.exp(l0 - m)
    e1 = jnp.exp(l1 - m)
    return e0 / (e0 + e1)


def _in_proj(x, ng_ref, win_ref, z_ref):
    h = _rms(x, ng_ref[...]).astype(BF16)
    z_ref[...] = jnp.dot(h, win_ref[...], preferred_element_type=F32)


def _hgrn_inputs(z_ref, lb, hh):
    c = hh * DK
    q = z_ref[:, OFF_Q + c:OFF_Q + c + DK]
    f = z_ref[:, OFF_F + c:OFF_F + c + DK]
    v = z_ref[:, OFF_I + c:OFF_I + c + DK]
    lbh = lb[:, c:c + DK]
    fg = lbh + (1.0 - lbh) * jax.nn.sigmoid(f)
    logf = jnp.log(fg)
    k = 1.0 - fg
    qf = jax.nn.silu(q) * (DK ** -0.5)
    return qf, k, v, logf


def _group_cumsum(x, group):
    row = lax.broadcasted_iota(jnp.int32, x.shape, 0) & (group - 1)
    s = 1
    while s < group:
        x = x + jnp.where(row >= s, pltpu.roll(x, s, 0), 0.0)
        s *= 2
    return x


def _hgrn_epilogue(o, z_ref, hg_ref, hh, oa_ref):
    c = hh * DK
    ga = z_ref[:, OFF_GA + c:OFF_GA + c + DK]
    on = _rms(o, hg_ref[...]) * jax.nn.silu(ga)
    oa_ref[:, c:c + DK] = on.astype(BF16)


def _pool_epilogue(pooled, z_ref, g, wpool_ref, ps_ref, yb_ref):
    c = g * G_B
    mixed = jnp.dot(pooled.astype(BF16), wpool_ref[g], preferred_element_type=F32)
    gb = z_ref[:, OFF_GB + c:OFF_GB + c + G_B]
    yb = mixed * ps_ref[:, c:c + G_B] * jax.nn.silu(gb)
    yb_ref[:, c:c + G_B] = yb.astype(BF16)


def _out_stage(x, z_ref, oa_ref, yb_ref, wpa_ref, wpb_ref, wout_ref, fg_ref):
    y_a = jnp.dot(oa_ref[...], wpa_ref[...], preferred_element_type=F32)
    y_b = jnp.dot(yb_ref[...], wpb_ref[...], preferred_element_type=F32)
    m_a = z_ref[:, OFF_MA:OFF_MA + D_MODEL]
    m_b = z_ref[:, OFF_MB:OFF_MB + D_MODEL]
    merged = jax.nn.sigmoid(m_a) * y_a + jax.nn.sigmoid(m_b) * y_b
    out = x + jnp.dot(merged.astype(BF16), wout_ref[...], preferred_element_type=F32)
    return _rms(out, fg_ref[...])


def _prompt_kernel(x_ref, ng_ref, win_ref, lbl_ref, hg_ref, wpa_ref, wpool_ref, ps_ref,
                   wpb_ref, wout_ref, fg_ref,
                   y_ref, s_out_ref, p_out_ref,
                   z_ref, st_ref, ext_ref, kx_ref, sn_ref, qs_ref, ks_ref, qd_ref, vb_ref,
                   oa_ref, yb_ref):
    tm = TM_PROMPT
    sub = SUB_TILE
    n_sub = tm // sub
    nc = tm // CHUNK
    ncs = sub // CHUNK
    t = pl.program_id(1)
    first = t == 0
    hdr = 2 * SUBLANES

    @pl.when((pl.program_id(0) == 0) & first)
    def _():
        st_ref[...] = jnp.zeros_like(st_ref)
        ext_ref[...] = jnp.zeros_like(ext_ref)
        kx_ref[...] = jnp.zeros_like(kx_ref)

    x = x_ref[0]
    _in_proj(x, ng_ref, win_ref, z_ref)
    lb = _lower_bound(lbl_ref)

    ri = lax.broadcasted_iota(jnp.int32, (sub, sub), 0)
    ci = lax.broadcasted_iota(jnp.int32, (sub, sub), 1)
    causal = ((ri // CHUNK) == (ci // CHUNK)) & (ci <= ri)

    heads = range(N_HEADS)
    units = [(hh, s) for hh in heads for s in range(n_sub)]

    def rows(s):
        return slice(s * sub, (s + 1) * sub)

    def cols(hh):
        return slice(hh * DK, (hh + 1) * DK)

    dec = []
    for hh in heads:
        qf, k, v, logf = _hgrn_inputs(z_ref, lb, hh)
        b = _group_cumsum(logf, CHUNK)
        b3 = b.reshape(nc, CHUNK, DK)
        ref = b3[:, CHUNK // 2:CHUNK // 2 + 1, :]
        bl = b3[:, CHUNK - 1:CHUNK, :]
        q3 = qf.reshape(nc, CHUNK, DK)
        k3 = k.reshape(nc, CHUNK, DK)
        qs_ref[:, cols(hh)] = (q3 * jnp.exp(b3 - ref)).reshape(tm, DK).astype(BF16)
        ks_ref[:, cols(hh)] = (k3 * jnp.exp(ref - b3)).reshape(tm, DK).astype(BF16)
        qd_ref[:, cols(hh)] = (q3 * jnp.exp(b3)).reshape(tm, DK).astype(BF16)
        vb_ref[:, cols(hh)] = v.astype(BF16)
        kd = (k3 * jnp.exp(bl - b3)).reshape(tm, DK).astype(BF16)
        dec.append(jnp.exp(bl))
        for c in range(nc):
            cc = c % ncs
            kx_ref[hh, c * CHUNK:(c + 1) * CHUNK, cc * DK:(cc + 1) * DK] = (
                kd[c * CHUNK:(c + 1) * CHUNK, :])

    ext_ref[0:hdr, :] = jnp.where(first, 0.0, ext_ref[tm:tm + hdr, :])
    ext_ref[hdr:hdr + tm, :] = z_ref[:, OFF_U:OFF_U + W_BRANCH]
    pos1 = t * tm + lax.broadcasted_iota(jnp.int32, (tm, 1), 0) + 1
    for g, w in enumerate(POOL_WINDOWS):
        c = g * G_B
        s = ext_ref[:, c:c + G_B]
        sh = 1
        while sh < w:
            s = s + pltpu.roll(s, sh, 0)
            sh *= 2
        inv = jnp.where(pos1 >= w, 1.0 / w, 1.0 / pos1.astype(F32))
        pooled = s[hdr:, :] * inv - z_ref[:, OFF_U + c:OFF_U + c + G_B]
        _pool_epilogue(pooled, z_ref, g, wpool_ref, ps_ref, yb_ref)
    p_out_ref[0] = ext_ref[hdr + tm - POOL_BUF:hdr + tm, :]

    sc = {(hh, s): lax.dot_general(qs_ref[rows(s), cols(hh)], ks_ref[rows(s), cols(hh)],
                                   (((1,), (1,)), ((), ())), preferred_element_type=F32)
          for hh, s in units}
    ut = {(hh, s): lax.dot_general(vb_ref[rows(s), cols(hh)], kx_ref[hh, rows(s), :],
                                   (((0,), (0,)), ((), ())), preferred_element_type=F32)
          for hh, s in units}
    o = {(hh, s): jnp.dot(jnp.where(causal, sc[hh, s], 0.0).astype(BF16),
                          vb_ref[rows(s), cols(hh)], preferred_element_type=F32)
         for hh, s in units}
    for hh in heads:
        st = jnp.where(first, 0.0, st_ref[hh])
        for c in range(nc):
            sn_ref[hh, c] = st.T.astype(BF16)
            st = dec[hh][c] * st + ut[hh, c // ncs][:, (c % ncs) * DK:(c % ncs + 1) * DK]
        st_ref[hh] = st
        s_out_ref[0, hh] = st.T
    wcol = D_MODEL // N_HEADS
    y_b = []
    for hh in heads:
        o_inter = [jnp.dot(qd_ref[c * CHUNK:(c + 1) * CHUNK, cols(hh)], sn_ref[hh, c],
                           preferred_element_type=F32) for c in range(nc)]
        y_b.append(jnp.dot(yb_ref[...], wpb_ref[:, hh * wcol:(hh + 1) * wcol],
                           preferred_element_type=F32))
        o_hh = jnp.concatenate([o[hh, s] for s in range(n_sub)], axis=0)
        _hgrn_epilogue(o_hh + jnp.concatenate(o_inter, axis=0), z_ref, hg_ref, hh, oa_ref)
    gated_b = jax.nn.sigmoid(z_ref[:, OFF_MB:OFF_MB + D_MODEL]) * jnp.concatenate(y_b, axis=1)

    y_a = jnp.dot(oa_ref[...], wpa_ref[...], preferred_element_type=F32)
    merged = jax.nn.sigmoid(z_ref[:, OFF_MA:OFF_MA + D_MODEL]) * y_a + gated_b
    out = x + jnp.dot(merged.astype(BF16), wout_ref[...], preferred_element_type=F32)
    y_ref[0] = _rms(out, fg_ref[...])


def _sample_kernel(x_ref, s_in_ref, p_in_ref, ng_ref, win_ref, lbl_ref, hg_ref, wpa_ref,
                   wpool_ref, ps_ref, wpb_ref, wout_ref, fg_ref,
                   y_ref, s_out_ref, p_out_ref,
                   xs_ref, z_ref, e_ref, a_ref, bm_ref, qd_ref, o_ref, oa_ref, yb_ref,
                   *, dec_seq):
    bb = BB_SAMPLE
    rs = ROWS_PER_SEQ
    rows = bb * rs

    @pl.when(pl.program_id(0) == 0)
    def _():
        xs_ref[...] = jnp.zeros_like(xs_ref)
        e_ref[...] = jnp.zeros_like(e_ref)

    for i in range(bb):
        xs_ref[i * rs:i * rs + dec_seq, :] = x_ref[i]
    x = xs_ref[...]
    _in_proj(x, ng_ref, win_ref, z_ref)
    lb = _lower_bound(lbl_ref)

    ri = lax.broadcasted_iota(jnp.int32, (rows, rows), 0)
    ci = lax.broadcasted_iota(jnp.int32, (rows, rows), 1)
    causal = ((ri // rs) == (ci // rs)) & (ci <= ri)
    r8 = lax.broadcasted_iota(jnp.int32, (rows, DK), 0) & (rs - 1)
    ones_rows = jnp.where((r8 >= dec_seq) & (r8 < dec_seq + 3), 1.0, 0.0).astype(BF16)

    for hh in range(N_HEADS):
        qf, k, v, logf = _hgrn_inputs(z_ref, lb, hh)
        b = _group_cumsum(logf, rs)
        b3 = b.reshape(bb, rs, DK)
        ref = b3[:, dec_seq // 2:dec_seq // 2 + 1, :]
        bl = b3[:, dec_seq - 1:dec_seq, :]
        q3 = qf.reshape(bb, rs, DK)
        k3 = k.reshape(bb, rs, DK)
        qs = (q3 * jnp.exp(b3 - ref)).reshape(rows, DK).astype(BF16)
        ks = (k3 * jnp.exp(ref - b3)).reshape(rows, DK).astype(BF16)
        qd = (q3 * jnp.exp(b3)).reshape(rows, DK).astype(BF16)
        kd = (k3 * jnp.exp(bl - b3)).reshape(rows, DK)
        vb = v.astype(BF16)

        sc = lax.dot_general(qs, ks, (((1,), (1,)), ((), ())), preferred_element_type=F32)
        sc = jnp.where(causal, sc, 0.0)
        o_ref[...] = jnp.dot(sc.astype(BF16), vb, preferred_element_type=F32)
        qd_ref[...] = qd

        dec = jnp.broadcast_to(jnp.exp(bl), (bb, rs, DK)).reshape(rows, DK)
        d1 = dec.astype(BF16).astype(F32)
        d2 = (dec - d1).astype(BF16).astype(F32)
        d3 = (dec - d1 - d2).astype(BF16).astype(F32)
        a = jnp.where(r8 < dec_seq, kd,
                      jnp.where(r8 == dec_seq, d1,
                                jnp.where(r8 == dec_seq + 1, d2,
                                          jnp.where(r8 == dec_seq + 2, d3, 0.0))))
        a_ref[...] = a.astype(BF16)
        bm_ref[:, 0:DV] = jnp.where(r8 < dec_seq, vb, jnp.zeros_like(vb))
        bm_ref[:, DV:2 * DV] = ones_rows

        for i in range(bb):
            s_old = s_in_ref[i, hh]
            qd_i = qd_ref[i * rs:(i + 1) * rs, :]
            o_ref[i * rs:(i + 1) * rs, :] += jnp.dot(qd_i, s_old.astype(BF16),
                                                     preferred_element_type=F32)
            upd = lax.dot_general(a_ref[i * rs:(i + 1) * rs, :], bm_ref[i * rs:(i + 1) * rs, :],
                                  (((0,), (0,)), ((), ())), preferred_element_type=F32)
            s_out_ref[i, hh] = upd[:, DV:2 * DV] * s_old + upd[:, 0:DV]

        _hgrn_epilogue(o_ref[...], z_ref, hg_ref, hh, oa_ref)

    e_ref[:, 0:POOL_BUF, :] = p_in_ref[...]
    u_all = z_ref[:, OFF_U:OFF_U + W_BRANCH]
    nrow = 2 * rs
    jrow = lax.broadcasted_iota(jnp.int32, (bb, nrow, G_B), 1)
    trow = lax.broadcasted_iota(jnp.int32, (bb, rs, G_B), 1)
    for g, w in enumerate(POOL_WINDOWS):
        c = g * G_B
        sc_ = e_ref[:, :, c:c + G_B]
        sh = 1
        while sh < nrow:
            sc_ = sc_ + jnp.where(jrow + sh < nrow, pltpu.roll(sc_, nrow - sh, 1), 0.0)
            sh *= 2
        rw = pltpu.roll(sc_, w % nrow, 1)[:, 0:rs, :]
        rw = jnp.where(trow < w, rw, 0.0).reshape(rows, G_B)
        u = u_all[:, c:c + G_B]
        cu = u
        sh = 1
        while sh < min(w, dec_seq):
            cu = cu + jnp.where(r8 >= sh, pltpu.roll(cu, sh, 0), 0.0)
            sh *= 2
        pooled = (rw + cu) * (1.0 / w) - u
        _pool_epilogue(pooled, z_ref, g, wpool_ref, ps_ref, yb_ref)

    keep = POOL_BUF - dec_seq
    jrow_w = lax.broadcasted_iota(jnp.int32, (bb, nrow, W_BRANCH), 1)
    old = pltpu.roll(e_ref[...], nrow - dec_seq, 1)
    u3 = u_all.reshape(bb, rs, W_BRANCH)
    new = pltpu.roll(jnp.concatenate([u3, jnp.zeros_like(u3)], axis=1), keep, 1)
    nb = jnp.where(jrow_w < keep, old, new)
    p_out_ref[...] = nb[:, 0:POOL_BUF, :]

    y = _out_stage(x, z_ref, oa_ref, yb_ref, wpa_ref, wpb_ref, wout_ref, fg_ref)
    for i in range(bb):
        y_ref[i] = y[i * rs:i * rs + dec_seq, :]


def _const_spec(shape):
    n = len(shape)
    return pl.BlockSpec(shape, lambda *_: (0,) * n, pipeline_mode=pl.Buffered(1))


def _weight_specs():
    return [
        _const_spec((1, D_MODEL)),
        _const_spec((D_MODEL, D_IN)),
        _const_spec((2, W_BRANCH)),
        _const_spec((1, DV)),
        _const_spec((W_BRANCH, D_MODEL)),
        _const_spec((len(POOL_WINDOWS), G_B, G_B)),
        _const_spec((1, W_BRANCH)),
        _const_spec((W_BRANCH, D_MODEL)),
        _const_spec((D_MODEL, D_MODEL)),
        _const_spec((1, D_MODEL)),
    ]


def kernel(x_prompt, x_sample, state_hgrn, state_pool, norm_g, w_in, lb_logits, hgrn_norm_g,
           w_proj_a, w_pool, pool_scale, w_proj_b, w_out, final_norm_g):
    batch, seq, _ = x_prompt.shape
    dec_batch, dec_seq, _ = x_sample.shape
    assert norm_g.shape[0] == 1 and lb_logits.shape[0] == 2, "single-layer decoder only"
    assert seq % TM_PROMPT == 0 and TM_PROMPT % SUB_TILE == 0 and dec_batch % BB_SAMPLE == 0
    assert dec_seq + 3 <= ROWS_PER_SEQ and dec_seq < CHUNK and PAST_LEN >= max(POOL_WINDOWS)

    weights = (norm_g, w_in[0].astype(BF16), lb_logits, hgrn_norm_g,
               w_proj_a[0].astype(BF16), w_pool[0].astype(BF16), pool_scale,
               w_proj_b[0].astype(BF16), w_out[0].astype(BF16), final_norm_g.reshape(1, D_MODEL))

    tm = TM_PROMPT
    nt = seq // tm
    y_p, s_p, p_p = pl.pallas_call(
        _prompt_kernel,
        grid=(batch, nt),
        in_specs=[pl.BlockSpec((1, tm, D_MODEL), lambda b, t: (b, t, 0))] + _weight_specs(),
        out_specs=[
            pl.BlockSpec((1, tm, D_MODEL), lambda b, t: (b, t, 0)),
            pl.BlockSpec((1, N_HEADS, DK, DV), lambda b, t: (b, 0, 0, 0)),
            pl.BlockSpec((1, POOL_BUF, W_BRANCH), lambda b, t: (b, 0, 0)),
        ],
        out_shape=[
            jax.ShapeDtypeStruct((batch, seq, D_MODEL), F32),
            jax.ShapeDtypeStruct((batch, N_HEADS, DK, DV), F32),
            jax.ShapeDtypeStruct((batch, POOL_BUF, W_BRANCH), F32),
        ],
        scratch_shapes=[
            pltpu.VMEM((tm, D_IN), F32),
            pltpu.VMEM((N_HEADS, DV, DK), F32),
            pltpu.VMEM((tm + 2 * SUBLANES, W_BRANCH), F32),
            pltpu.VMEM((N_HEADS, tm, (SUB_TILE // CHUNK) * DK), BF16),
            pltpu.VMEM((N_HEADS, tm // CHUNK, DK, DV), BF16),
            pltpu.VMEM((tm, W_BRANCH), BF16),
            pltpu.VMEM((tm, W_BRANCH), BF16),
            pltpu.VMEM((tm, W_BRANCH), BF16),
            pltpu.VMEM((tm, W_BRANCH), BF16),
            pltpu.VMEM((tm, W_BRANCH), BF16),
            pltpu.VMEM((tm, W_BRANCH), BF16),
        ],
        compiler_params=pltpu.CompilerParams(
            dimension_semantics=("arbitrary", "arbitrary"),
            vmem_limit_bytes=VMEM_LIMIT_BYTES),
        name="hgrn2_pool_prompt",
    )(x_prompt, *weights)

    bb = BB_SAMPLE
    rows = bb * ROWS_PER_SEQ
    y_s, s_s, p_s = pl.pallas_call(
        functools.partial(_sample_kernel, dec_seq=dec_seq),
        grid=(dec_batch // bb,),
        in_specs=[
            pl.BlockSpec((bb, dec_seq, D_MODEL), lambda i: (i, 0, 0)),
            pl.BlockSpec((bb, N_HEADS, DK, DV), lambda i: (i, 0, 0, 0)),
            pl.BlockSpec((bb, POOL_BUF, W_BRANCH), lambda i: (i, 0, 0)),
        ] + _weight_specs(),
        out_specs=[
            pl.BlockSpec((bb, dec_seq, D_MODEL), lambda i: (i, 0, 0)),
            pl.BlockSpec((bb, N_HEADS, DK, DV), lambda i: (i, 0, 0, 0)),
            pl.BlockSpec((bb, POOL_BUF, W_BRANCH), lambda i: (i, 0, 0)),
        ],
        out_shape=[
            jax.ShapeDtypeStruct((dec_batch, dec_seq, D_MODEL), F32),
            jax.ShapeDtypeStruct((dec_batch, N_HEADS, DK, DV), F32),
            jax.ShapeDtypeStruct((dec_batch, POOL_BUF, W_BRANCH), F32),
        ],
        scratch_shapes=[
            pltpu.VMEM((rows, D_MODEL), F32),
            pltpu.VMEM((rows, D_IN), F32),
            pltpu.VMEM((bb, 2 * ROWS_PER_SEQ, W_BRANCH), F32),
            pltpu.VMEM((rows, DK), BF16),
            pltpu.VMEM((rows, 2 * DV), BF16),
            pltpu.VMEM((rows, DK), BF16),
            pltpu.VMEM((rows, DV), F32),
            pltpu.VMEM((rows, W_BRANCH), BF16),
            pltpu.VMEM((rows, W_BRANCH), BF16),
        ],
        compiler_params=pltpu.CompilerParams(
            dimension_semantics=("arbitrary",),
            vmem_limit_bytes=VMEM_LIMIT_BYTES),
        name="hgrn2_pool_sample",
    )(x_sample, state_hgrn[0], state_pool[0], *weights)

    return (y_p, y_s, s_p[None], p_p[None], s_s[None], p_s[None])
```

```python
import functools

import jax
import jax.numpy as jnp
from jax import lax
from jax.experimental import pallas as pl
from jax.experimental.pallas import tpu as pltpu

F32 = jnp.float32
BF16 = jnp.bfloat16

D_MODEL = 1024
W_BRANCH = 512
N_HEADS = 4
DK = 128
DV = 128
CHUNK = 32
POOL_WINDOWS = (2, 4, 8, 16)
G_B = 128
POOL_BUF = 15
PAST_LEN = 16384
EPS = 1e-6
D_IN = 4 * W_BRANCH + 2 * W_BRANCH + 2 * D_MODEL
OFF_Q, OFF_F, OFF_I, OFF_GA = 0, 512, 1024, 1536
OFF_U, OFF_GB, OFF_MA, OFF_MB = 2048, 2560, 3072, 4096

SUBLANES = 8
VMEM_LIMIT_BYTES = 56 * 1024 * 1024

TM_PROMPT = 512
SUB_TILE = 256
PROJ_COLS = 256
BB_SAMPLE = 16
ROWS_PER_SEQ = SUBLANES


def _rms(x, g):
    ms = jnp.mean(x * x, axis=-1, keepdims=True)
    return x * lax.rsqrt(ms + EPS) * g


def _lower_bound(lbl_ref):
    l0 = lbl_ref[0:1, :]
    l1 = lbl_ref[1:2, :]
    m = jnp.maximum(l0, l1)
    e0 = jnp.exp(l0 - m)
    e1 = jnp.exp(l1 - m)
    return e0 / (e0 + e1)


def _in_proj(x, ng_ref, win_ref, z_ref):
    h = _rms(x, ng_ref[...]).astype(BF16)
    z_ref[...] = jnp.dot(h, win_ref[...], preferred_element_type=F32)


def _hgrn_inputs(z_ref, lb, hh):
    c = hh * DK
    q = z_ref[:, OFF_Q + c:OFF_Q + c + DK]
    f = z_ref[:, OFF_F + c:OFF_F + c + DK]
    v = z_ref[:, OFF_I + c:OFF_I + c + DK]
    lbh = lb[:, c:c + DK]
    fg = lbh + (1.0 - lbh) * jax.nn.sigmoid(f)
    logf = jnp.log(fg)
    k = 1.0 - fg
    qf = jax.nn.silu(q) * (DK ** -0.5)
    return qf, k, v, logf


def _group_cumsum(x, group):
    row = lax.broadcasted_iota(jnp.int32, x.shape, 0) & (group - 1)
    s = 1
    while s < group:
        x = x + jnp.where(row >= s, pltpu.roll(x, s, 0), 0.0)
        s *= 2
    return x


def _hgrn_epilogue(o, z_ref, hg_ref, hh, oa_ref):
    c = hh * DK
    ga = z_ref[:, OFF_GA + c:OFF_GA + c + DK]
    on = _rms(o, hg_ref[...]) * jax.nn.silu(ga)
    oa_ref[:, c:c + DK] = on.astype(BF16)


def _pool_epilogue(pooled, z_ref, g, wpool_ref, ps_ref, yb_ref):
    c = g * G_B
    mixed = jnp.dot(pooled.astype(BF16), wpool_ref[g], preferred_element_type=F32)
    gb = z_ref[:, OFF_GB + c:OFF_GB + c + G_B]
    yb = mixed * ps_ref[:, c:c + G_B] * jax.nn.silu(gb)
    yb_ref[:, c:c + G_B] = yb.astype(BF16)


def _out_stage(x, z_ref, oa_ref, yb_ref, wpa_ref, wpb_ref, wout_ref, fg_ref):
    y_a = jnp.dot(oa_ref[...], wpa_ref[...], preferred_element_type=F32)
    y_b = jnp.dot(yb_ref[...], wpb_ref[...], preferred_element_type=F32)
    m_a = z_ref[:, OFF_MA:OFF_MA + D_MODEL]
    m_b = z_ref[:, OFF_MB:OFF_MB + D_MODEL]
    merged = jax.nn.sigmoid(m_a) * y_a + jax.nn.sigmoid(m_b) * y_b
    out = x + jnp.dot(merged.astype(BF16), wout_ref[...], preferred_element_type=F32)
    return _rms(out, fg_ref[...])


def _prompt_kernel(x_ref, ng_ref, win_ref, lbl_ref, hg_ref, wpa_ref, wpool_ref, ps_ref,
                   wpb_ref, wout_ref, fg_ref,
                   y_ref, s_out_ref, p_out_ref,
                   z_ref, h_ref, st_ref, ext_ref, kx_ref, sn_ref, qs_ref, ks_ref, qd_ref, vb_ref,
                   oa_ref, yb_ref):
    tm = TM_PROMPT
    sub = SUB_TILE
    n_sub = tm // sub
    nc = tm // CHUNK
    ncs = sub // CHUNK
    t = pl.program_id(1)
    first = t == 0
    hdr = 2 * SUBLANES

    @pl.when((pl.program_id(0) == 0) & first)
    def _():
        st_ref[...] = jnp.zeros_like(st_ref)
        ext_ref[...] = jnp.zeros_like(ext_ref)
        kx_ref[...] = jnp.zeros_like(kx_ref)

    x = x_ref[0]
    h_ref[...] = _rms(x, ng_ref[...]).astype(BF16)
    z_ref[:, 0:OFF_MA] = jnp.dot(h_ref[...], win_ref[:, 0:OFF_MA], preferred_element_type=F32)
    gate_tiles = iter(list(range(OFF_MB // PROJ_COLS, D_IN // PROJ_COLS))
                      + list(range(OFF_MA // PROJ_COLS, OFF_MB // PROJ_COLS)))

    def project_gates(n=1):
        for _ in range(n):
            c0 = next(gate_tiles) * PROJ_COLS
            z_ref[:, c0:c0 + PROJ_COLS] = jnp.dot(
                h_ref[...], win_ref[:, c0:c0 + PROJ_COLS], preferred_element_type=F32)

    lb = _lower_bound(lbl_ref)

    ri = lax.broadcasted_iota(jnp.int32, (sub, sub), 0)
    ci = lax.broadcasted_iota(jnp.int32, (sub, sub), 1)
    causal = ((ri // CHUNK) == (ci // CHUNK)) & (ci <= ri)

    heads = range(N_HEADS)
    units = [(hh, s) for hh in heads for s in range(n_sub)]

    def rows(s):
        return slice(s * sub, (s + 1) * sub)

    def cols(hh):
        return slice(hh * DK, (hh + 1) * DK)

    dec = []
    for hh in heads:
        qf, k, v, logf = _hgrn_inputs(z_ref, lb, hh)
        b = _group_cumsum(logf, CHUNK)
        b3 = b.reshape(nc, CHUNK, DK)
        ref = b3[:, CHUNK // 2:CHUNK // 2 + 1, :]
        bl = b3[:, CHUNK - 1:CHUNK, :]
        q3 = qf.reshape(nc, CHUNK, DK)
        k3 = k.reshape(nc, CHUNK, DK)
        qs_ref[:, cols(hh)] = (q3 * jnp.exp(b3 - ref)).reshape(tm, DK).astype(BF16)
        ks_ref[:, cols(hh)] = (k3 * jnp.exp(ref - b3)).reshape(tm, DK).astype(BF16)
        qd_ref[:, cols(hh)] = (q3 * jnp.exp(b3)).reshape(tm, DK).astype(BF16)
        vb_ref[:, cols(hh)] = v.astype(BF16)
        kd = (k3 * jnp.exp(bl - b3)).reshape(tm, DK).astype(BF16)
        dec.append(jnp.exp(bl))
        for c in range(nc):
            cc = c % ncs
            kx_ref[hh, c * CHUNK:(c + 1) * CHUNK, cc * DK:(cc + 1) * DK] = (
                kd[c * CHUNK:(c + 1) * CHUNK, :])
        if hh % 2 == 1:
            project_gates()

    project_gates()
    ext_ref[0:hdr, :] = jnp.where(first, 0.0, ext_ref[tm:tm + hdr, :])
    ext_ref[hdr:hdr + tm, :] = z_ref[:, OFF_U:OFF_U + W_BRANCH]
    pos1 = t * tm + lax.broadcasted_iota(jnp.int32, (tm, 1), 0) + 1
    for g, w in enumerate(POOL_WINDOWS):
        c = g * G_B
        s = ext_ref[:, c:c + G_B]
        sh = 1
        while sh < w:
            s = s + pltpu.roll(s, sh, 0)
            sh *= 2
        inv = jnp.where(pos1 >= w, 1.0 / w, 1.0 / pos1.astype(F32))
        pooled = s[hdr:, :] * inv - z_ref[:, OFF_U + c:OFF_U + c + G_B]
        _pool_epilogue(pooled, z_ref, g, wpool_ref, ps_ref, yb_ref)
    p_out_ref[0] = ext_ref[hdr + tm - POOL_BUF:hdr + tm, :]

    sc = {(hh, s): lax.dot_general(qs_ref[rows(s), cols(hh)], ks_ref[rows(s), cols(hh)],
                                   (((1,), (1,)), ((), ())), preferred_element_type=F32)
          for hh, s in units}
    ut = {(hh, s): lax.dot_general(vb_ref[rows(s), cols(hh)], kx_ref[hh, rows(s), :],
                                   (((0,), (0,)), ((), ())), preferred_element_type=F32)
          for hh, s in units}
    o = {(hh, s): jnp.dot(jnp.where(causal, sc[hh, s], 0.0).astype(BF16),
                          vb_ref[rows(s), cols(hh)], preferred_element_type=F32)
         for hh, s in units}
    project_gates()
    for hh in heads:
        st = jnp.where(first, 0.0, st_ref[hh])
        for c in range(nc):
            sn_ref[hh, c] = st.T.astype(BF16)
            st = dec[hh][c] * st + ut[hh, c // ncs][:, (c % ncs) * DK:(c % ncs + 1) * DK]
        st_ref[hh] = st
        s_out_ref[0, hh] = st.T
    project_gates()
    wcol = D_MODEL // N_HEADS
    y_b = []
    for hh in heads:
        o_inter = [jnp.dot(qd_ref[c * CHUNK:(c + 1) * CHUNK, cols(hh)], sn_ref[hh, c],
                           preferred_element_type=F32) for c in range(nc)]
        y_b.append(jnp.dot(yb_ref[...], wpb_ref[:, hh * wcol:(hh + 1) * wcol],
                           preferred_element_type=F32))
        if hh % 2 == 1:
            project_gates()
        o_hh = jnp.concatenate([o[hh, s] for s in range(n_sub)], axis=0)
        _hgrn_epilogue(o_hh + jnp.concatenate(o_inter, axis=0), z_ref, hg_ref, hh, oa_ref)
    gated_b = jax.nn.sigmoid(z_ref[:, OFF_MB:OFF_MB + D_MODEL]) * jnp.concatenate(y_b, axis=1)

    project_gates()
    y_a = jnp.dot(oa_ref[...], wpa_ref[...], preferred_element_type=F32)
    merged = jax.nn.sigmoid(z_ref[:, OFF_MA:OFF_MA + D_MODEL]) * y_a + gated_b
    out = x + jnp.dot(merged.astype(BF16), wout_ref[...], preferred_element_type=F32)
    y_ref[0] = _rms(out, fg_ref[...])


def _sample_kernel(x_ref, s_in_ref, p_in_ref, ng_ref, win_ref, lbl_ref, hg_ref, wpa_ref,
                   wpool_ref, ps_ref, wpb_ref, wout_ref, fg_ref,
                   y_ref, s_out_ref, p_out_ref,
                   xs_ref, z_ref, e_ref, a_ref, bm_ref, qd_ref, o_ref, oa_ref, yb_ref,
                   *, dec_seq):
    bb = BB_SAMPLE
    rs = ROWS_PER_SEQ
    rows = bb * rs

    @pl.when(pl.program_id(0) == 0)
    def _():
        xs_ref[...] = jnp.zeros_like(xs_ref)
        e_ref[...] = jnp.zeros_like(e_ref)

    for i in range(bb):
        xs_ref[i * rs:i * rs + dec_seq, :] = x_ref[i]
    x = xs_ref[...]
    _in_proj(x, ng_ref, win_ref, z_ref)
    lb = _lower_bound(lbl_ref)

    ri = lax.broadcasted_iota(jnp.int32, (rows, rows), 0)
    ci = lax.broadcasted_iota(jnp.int32, (rows, rows), 1)
    causal = ((ri // rs) == (ci // rs)) & (ci <= ri)
    r8 = lax.broadcasted_iota(jnp.int32, (rows, DK), 0) & (rs - 1)
    ones_rows = jnp.where((r8 >= dec_seq) & (r8 < dec_seq + 3), 1.0, 0.0).astype(BF16)

    for hh in range(N_HEADS):
        qf, k, v, logf = _hgrn_inputs(z_ref, lb, hh)
        b = _group_cumsum(logf, rs)
        b3 = b.reshape(bb, rs, DK)
        ref = b3[:, dec_seq // 2:dec_seq // 2 + 1, :]
        bl = b3[:, dec_seq - 1:dec_seq, :]
        q3 = qf.reshape(bb, rs, DK)
        k3 = k.reshape(bb, rs, DK)
        qs = (q3 * jnp.exp(b3 - ref)).reshape(rows, DK).astype(BF16)
        ks = (k3 * jnp.exp(ref - b3)).reshape(rows, DK).astype(BF16)
        qd = (q3 * jnp.exp(b3)).reshape(rows, DK).astype(BF16)
        kd = (k3 * jnp.exp(bl - b3)).reshape(rows, DK)
        vb = v.astype(BF16)

        sc = lax.dot_general(qs, ks, (((1,), (1,)), ((), ())), preferred_element_type=F32)
        sc = jnp.where(causal, sc, 0.0)
        o_ref[...] = jnp.dot(sc.astype(BF16), vb, preferred_element_type=F32)
        qd_ref[...] = qd

        dec = jnp.broadcast_to(jnp.exp(bl), (bb, rs, DK)).reshape(rows, DK)
        d1 = dec.astype(BF16).astype(F32)
        d2 = (dec - d1).astype(BF16).astype(F32)
        d3 = (dec - d1 - d2).astype(BF16).astype(F32)
        a = jnp.where(r8 < dec_seq, kd,
                      jnp.where(r8 == dec_seq, d1,
                                jnp.where(r8 == dec_seq + 1, d2,
                                          jnp.where(r8 == dec_seq + 2, d3, 0.0))))
        a_ref[...] = a.astype(BF16)
        bm_ref[:, 0:DV] = jnp.where(r8 < dec_seq, vb, jnp.zeros_like(vb))
        bm_ref[:, DV:2 * DV] = ones_rows

        for i in range(bb):
            s_old = s_in_ref[i, hh]
            qd_i = qd_ref[i * rs:(i + 1) * rs, :]
            o_ref[i * rs:(i + 1) * rs, :] += jnp.dot(qd_i, s_old.astype(BF16),
                                                     preferred_element_type=F32)
            upd = lax.dot_general(a_ref[i * rs:(i + 1) * rs, :], bm_ref[i * rs:(i + 1) * rs, :],
                                  (((0,), (0,)), ((), ())), preferred_element_type=F32)
            s_out_ref[i, hh] = upd[:, DV:2 * DV] * s_old + upd[:, 0:DV]

        _hgrn_epilogue(o_ref[...], z_ref, hg_ref, hh, oa_ref)

    e_ref[:, 0:POOL_BUF, :] = p_in_ref[...]
    u_all = z_ref[:, OFF_U:OFF_U + W_BRANCH]
    nrow = 2 * rs
    jrow = lax.broadcasted_iota(jnp.int32, (bb, nrow, G_B), 1)
    trow = lax.broadcasted_iota(jnp.int32, (bb, rs, G_B), 1)
    for g, w in enumerate(POOL_WINDOWS):
        c = g * G_B
        sc_ = e_ref[:, :, c:c + G_B]
        sh = 1
        while sh < nrow:
            sc_ = sc_ + jnp.where(jrow + sh < nrow, pltpu.roll(sc_, nrow - sh, 1), 0.0)
            sh *= 2
        rw = pltpu.roll(sc_, w % nrow, 1)[:, 0:rs, :]
        rw = jnp.where(trow < w, rw, 0.0).reshape(rows, G_B)
        u = u_all[:, c:c + G_B]
        cu = u
        sh = 1
        while sh < min(w, dec_seq):
            cu = cu + jnp.where(r8 >= sh, pltpu.roll(cu, sh, 0), 0.0)
            sh *= 2
        pooled = (rw + cu) * (1.0 / w) - u
        _pool_epilogue(pooled, z_ref, g, wpool_ref, ps_ref, yb_ref)

    keep = POOL_BUF - dec_seq
    jrow_w = lax.broadcasted_iota(jnp.int32, (bb, nrow, W_BRANCH), 1)
    old = pltpu.roll(e_ref[...], nrow - dec_seq, 1)
    u3 = u_all.reshape(bb, rs, W_BRANCH)
    new = pltpu.roll(jnp.concatenate([u3, jnp.zeros_like(u3)], axis=1), keep, 1)
    nb = jnp.where(jrow_w < keep, old, new)
    p_out_ref[...] = nb[:, 0:POOL_BUF, :]

    y = _out_stage(x, z_ref, oa_ref, yb_ref, wpa_ref, wpb_ref, wout_ref, fg_ref)
    for i in range(bb):
        y_ref[i] = y[i * rs:i * rs + dec_seq, :]


def _const_spec(shape):
    n = len(shape)
    return pl.BlockSpec(shape, lambda *_: (0,) * n, pipeline_mode=pl.Buffered(1))


def _weight_specs():
    return [
        _const_spec((1, D_MODEL)),
        _const_spec((D_MODEL, D_IN)),
        _const_spec((2, W_BRANCH)),
        _const_spec((1, DV)),
        _const_spec((W_BRANCH, D_MODEL)),
        _const_spec((len(POOL_WINDOWS), G_B, G_B)),
        _const_spec((1, W_BRANCH)),
        _const_spec((W_BRANCH, D_MODEL)),
        _const_spec((D_MODEL, D_MODEL)),
        _const_spec((1, D_MODEL)),
    ]


def kernel(x_prompt, x_sample, state_hgrn, state_pool, norm_g, w_in, lb_logits, hgrn_norm_g,
           w_proj_a, w_pool, pool_scale, w_proj_b, w_out, final_norm_g):
    batch, seq, _ = x_prompt.shape
    dec_batch, dec_seq, _ = x_sample.shape
    assert norm_g.shape[0] == 1 and lb_logits.shape[0] == 2, "single-layer decoder only"
    assert seq % TM_PROMPT == 0 and TM_PROMPT % SUB_TILE == 0 and dec_batch % BB_SAMPLE == 0
    assert dec_seq + 3 <= ROWS_PER_SEQ and dec_seq < CHUNK and PAST_LEN >= max(POOL_WINDOWS)

    weights = (norm_g, w_in[0].astype(BF16), lb_logits, hgrn_norm_g,
               w_proj_a[0].astype(BF16), w_pool[0].astype(BF16), pool_scale,
               w_proj_b[0].astype(BF16), w_out[0].astype(BF16), final_norm_g.reshape(1, D_MODEL))

    tm = TM_PROMPT
    nt = seq // tm
    y_p, s_p, p_p = pl.pallas_call(
        _prompt_kernel,
        grid=(batch, nt),
        in_specs=[pl.BlockSpec((1, tm, D_MODEL), lambda b, t: (b, t, 0))] + _weight_specs(),
        out_specs=[
            pl.BlockSpec((1, tm, D_MODEL), lambda b, t: (b, t, 0)),
            pl.BlockSpec((1, N_HEADS, DK, DV), lambda b, t: (b, 0, 0, 0)),
            pl.BlockSpec((1, POOL_BUF, W_BRANCH), lambda b, t: (b, 0, 0)),
        ],
        out_shape=[
            jax.ShapeDtypeStruct((batch, seq, D_MODEL), F32),
            jax.ShapeDtypeStruct((batch, N_HEADS, DK, DV), F32),
            jax.ShapeDtypeStruct((batch, POOL_BUF, W_BRANCH), F32),
        ],
        scratch_shapes=[
            pltpu.VMEM((tm, D_IN), F32),
            pltpu.VMEM((tm, D_MODEL), BF16),
            pltpu.VMEM((N_HEADS, DV, DK), F32),
            pltpu.VMEM((tm + 2 * SUBLANES, W_BRANCH), F32),
            pltpu.VMEM((N_HEADS, tm, (SUB_TILE // CHUNK) * DK), BF16),
            pltpu.VMEM((N_HEADS, tm // CHUNK, DK, DV), BF16),
            pltpu.VMEM((tm, W_BRANCH), BF16),
            pltpu.VMEM((tm, W_BRANCH), BF16),
            pltpu.VMEM((tm, W_BRANCH), BF16),
            pltpu.VMEM((tm, W_BRANCH), BF16),
            pltpu.VMEM((tm, W_BRANCH), BF16),
            pltpu.VMEM((tm, W_BRANCH), BF16),
        ],
        compiler_params=pltpu.CompilerParams(
            dimension_semantics=("arbitrary", "arbitrary"),
            vmem_limit_bytes=VMEM_LIMIT_BYTES),
        name="hgrn2_pool_prompt",
    )(x_prompt, *weights)

    bb = BB_SAMPLE
    rows = bb * ROWS_PER_SEQ
    y_s, s_s, p_s = pl.pallas_call(
        functools.partial(_sample_kernel, dec_seq=dec_seq),
        grid=(dec_batch // bb,),
        in_specs=[
            pl.BlockSpec((bb, dec_seq, D_MODEL), lambda i: (i, 0, 0)),
            pl.BlockSpec((bb, N_HEADS, DK, DV), lambda i: (i, 0, 0, 0)),
            pl.BlockSpec((bb, POOL_BUF, W_BRANCH), lambda i: (i, 0, 0)),
        ] + _weight_specs(),
        out_specs=[
            pl.BlockSpec((bb, dec_seq, D_MODEL), lambda i: (i, 0, 0)),
            pl.BlockSpec((bb, N_HEADS, DK, DV), lambda i: (i, 0, 0, 0)),
            pl.BlockSpec((bb, POOL_BUF, W_BRANCH), lambda i: (i, 0, 0)),
        ],
        out_shape=[
            jax.ShapeDtypeStruct((dec_batch, dec_seq, D_MODEL), F32),
            jax.ShapeDtypeStruct((dec_batch, N_HEADS, DK, DV), F32),
            jax.ShapeDtypeStruct((dec_batch, POOL_BUF, W_BRANCH), F32),
        ],
        scratch_shapes=[
            pltpu.VMEM((rows, D_MODEL), F32),
            pltpu.VMEM((rows, D_IN), F32),
            pltpu.VMEM((bb, 2 * ROWS_PER_SEQ, W_BRANCH), F32),
            pltpu.VMEM((rows, DK), BF16),
            pltpu.VMEM((rows, 2 * DV), BF16),
            pltpu.VMEM((rows, DK), BF16),
            pltpu.VMEM((rows, DV), F32),
            pltpu.VMEM((rows, W_BRANCH), BF16),
            pltpu.VMEM((rows, W_BRANCH), BF16),
        ],
        compiler_params=pltpu.CompilerParams(
            dimension_semantics=("arbitrary",),
            vmem_limit_bytes=VMEM_LIMIT_BYTES),
        name="hgrn2_pool_sample",
    )(x_sample, state_hgrn[0], state_pool[0], *weights)

    return (y_p, y_s, s_p[None], p_p[None], s_s[None], p_s[None])
```

```python
import functools

import jax
import jax.numpy as jnp
from jax import lax
from jax.experimental import pallas as pl
from jax.experimental.pallas import tpu as pltpu

F32 = jnp.float32
BF16 = jnp.bfloat16

D_MODEL = 1024
W_BRANCH = 512
N_HEADS = 4
DK = 128
DV = 128
CHUNK = 32
POOL_WINDOWS = (2, 4, 8, 16)
G_B = 128
POOL_BUF = 15
PAST_LEN = 16384
EPS = 1e-6
D_IN = 4 * W_BRANCH + 2 * W_BRANCH + 2 * D_MODEL
OFF_Q, OFF_F, OFF_I, OFF_GA = 0, 512, 1024, 1536
OFF_U, OFF_GB, OFF_MA, OFF_MB = 2048, 2560, 3072, 4096

SUBLANES = 8
VMEM_LIMIT_BYTES = 56 * 1024 * 1024

TM_PROMPT = 512
SUB_TILE = 256
BB_SAMPLE = 16
DEC_SEQ = 4
PAIR_ROWS = 2 * DEC_SEQ


def _rms(x, g):
    ms = jnp.mean(x * x, axis=-1, keepdims=True)
    return x * lax.rsqrt(ms + EPS) * g


def _lower_bound(lbl_ref):
    l0 = lbl_ref[0:1, :]
    l1 = lbl_ref[1:2, :]
    m = jnp.maximum(l0, l1)
    e0 = jnp.exp(l0 - m)
    e1 = jnp.exp(l1 - m)
    return e0 / (e0 + e1)


def _in_proj(x, ng_ref, win_ref, z_ref):
    h = _rms(x, ng_ref[...]).astype(BF16)
    z_ref[...] = jnp.dot(h, win_ref[...], preferred_element_type=F32)


def _hgrn_inputs(z_ref, lb, hh):
    c = hh * DK
    q = z_ref[:, OFF_Q + c:OFF_Q + c + DK]
    f = z_ref[:, OFF_F + c:OFF_F + c + DK]
    v = z_ref[:, OFF_I + c:OFF_I + c + DK]
    lbh = lb[:, c:c + DK]
    fg = lbh + (1.0 - lbh) * jax.nn.sigmoid(f)
    logf = jnp.log(fg)
    k = 1.0 - fg
    qf = jax.nn.silu(q) * (DK ** -0.5)
    return qf, k, v, logf


def _group_cumsum(x, group):
    row = lax.broadcasted_iota(jnp.int32, x.shape, 0) & (group - 1)
    s = 1
    while s < group:
        x = x + jnp.where(row >= s, pltpu.roll(x, s, 0), 0.0)
        s *= 2
    return x


def _hgrn_epilogue(o, z_ref, hg_ref, hh, oa_ref):
    c = hh * DK
    ga = z_ref[:, OFF_GA + c:OFF_GA + c + DK]
    on = _rms(o, hg_ref[...]) * jax.nn.silu(ga)
    oa_ref[:, c:c + DK] = on.astype(BF16)


def _pool_epilogue(pooled, z_ref, g, wpool_ref, ps_ref, yb_ref):
    c = g * G_B
    mixed = jnp.dot(pooled.astype(BF16), wpool_ref[g], preferred_element_type=F32)
    gb = z_ref[:, OFF_GB + c:OFF_GB + c + G_B]
    yb = mixed * ps_ref[:, c:c + G_B] * jax.nn.silu(gb)
    yb_ref[:, c:c + G_B] = yb.astype(BF16)


def _prompt_kernel(x_ref, ng_ref, win_ref, lbl_ref, hg_ref, wpa_ref, wpool_ref, ps_ref,
                   wpb_ref, wout_ref, fg_ref,
                   y_ref, s_out_ref, p_out_ref,
                   z_ref, st_ref, ext_ref, kx_ref, sn_ref, qs_ref, ks_ref, qd_ref, vb_ref,
                   oa_ref, yb_ref):
    tm = TM_PROMPT
    sub = SUB_TILE
    n_sub = tm // sub
    nc = tm // CHUNK
    ncs = sub // CHUNK
    t = pl.program_id(1)
    first = t == 0
    hdr = 2 * SUBLANES

    @pl.when((pl.program_id(0) == 0) & first)
    def _():
        st_ref[...] = jnp.zeros_like(st_ref)
        ext_ref[...] = jnp.zeros_like(ext_ref)
        kx_ref[...] = jnp.zeros_like(kx_ref)

    x = x_ref[0]
    _in_proj(x, ng_ref, win_ref, z_ref)
    lb = _lower_bound(lbl_ref)

    ri = lax.broadcasted_iota(jnp.int32, (sub, sub), 0)
    ci = lax.broadcasted_iota(jnp.int32, (sub, sub), 1)
    causal = ((ri // CHUNK) == (ci // CHUNK)) & (ci <= ri)

    heads = range(N_HEADS)
    units = [(hh, s) for hh in heads for s in range(n_sub)]

    def rows(s):
        return slice(s * sub, (s + 1) * sub)

    def cols(hh):
        return slice(hh * DK, (hh + 1) * DK)

    dec = []
    for hh in heads:
        qf, k, v, logf = _hgrn_inputs(z_ref, lb, hh)
        b = _group_cumsum(logf, CHUNK)
        b3 = b.reshape(nc, CHUNK, DK)
        ref = b3[:, CHUNK // 2:CHUNK // 2 + 1, :]
        bl = b3[:, CHUNK - 1:CHUNK, :]
        q3 = qf.reshape(nc, CHUNK, DK)
        k3 = k.reshape(nc, CHUNK, DK)
        qs_ref[:, cols(hh)] = (q3 * jnp.exp(b3 - ref)).reshape(tm, DK).astype(BF16)
        ks_ref[:, cols(hh)] = (k3 * jnp.exp(ref - b3)).reshape(tm, DK).astype(BF16)
        qd_ref[:, cols(hh)] = (q3 * jnp.exp(b3)).reshape(tm, DK).astype(BF16)
        vb_ref[:, cols(hh)] = v.astype(BF16)
        kd = (k3 * jnp.exp(bl - b3)).reshape(tm, DK).astype(BF16)
        dec.append(jnp.exp(bl))
        for c in range(nc):
            cc = c % ncs
            kx_ref[hh, c * CHUNK:(c + 1) * CHUNK, cc * DK:(cc + 1) * DK] = (
                kd[c * CHUNK:(c + 1) * CHUNK, :])

    ext_ref[0:hdr, :] = jnp.where(first, 0.0, ext_ref[tm:tm + hdr, :])
    ext_ref[hdr:hdr + tm, :] = z_ref[:, OFF_U:OFF_U + W_BRANCH]
    pos1 = t * tm + lax.broadcasted_iota(jnp.int32, (tm, 1), 0) + 1
    for g, w in enumerate(POOL_WINDOWS):
        c = g * G_B
        s = ext_ref[:, c:c + G_B]
        sh = 1
        while sh < w:
            s = s + pltpu.roll(s, sh, 0)
            sh *= 2
        inv = jnp.where(pos1 >= w, 1.0 / w, 1.0 / pos1.astype(F32))
        pooled = s[hdr:, :] * inv - z_ref[:, OFF_U + c:OFF_U + c + G_B]
        _pool_epilogue(pooled, z_ref, g, wpool_ref, ps_ref, yb_ref)
    p_out_ref[0] = ext_ref[hdr + tm - POOL_BUF:hdr + tm, :]

    sc = {(hh, s): lax.dot_general(qs_ref[rows(s), cols(hh)], ks_ref[rows(s), cols(hh)],
                                   (((1,), (1,)), ((), ())), preferred_element_type=F32)
          for hh, s in units}
    ut = {(hh, s): lax.dot_general(vb_ref[rows(s), cols(hh)], kx_ref[hh, rows(s), :],
                                   (((0,), (0,)), ((), ())), preferred_element_type=F32)
          for hh, s in units}
    o = {(hh, s): jnp.dot(jnp.where(causal, sc[hh, s], 0.0).astype(BF16),
                          vb_ref[rows(s), cols(hh)], preferred_element_type=F32)
         for hh, s in units}
    for hh in heads:
        st = jnp.where(first, 0.0, st_ref[hh])
        for c in range(nc):
            sn_ref[hh, c] = st.T.astype(BF16)
            st = dec[hh][c] * st + ut[hh, c // ncs][:, (c % ncs) * DK:(c % ncs + 1) * DK]
        st_ref[hh] = st
        s_out_ref[0, hh] = st.T
    wcol = D_MODEL // N_HEADS
    y_b = []
    for hh in heads:
        o_inter = [jnp.dot(qd_ref[c * CHUNK:(c + 1) * CHUNK, cols(hh)], sn_ref[hh, c],
                           preferred_element_type=F32) for c in range(nc)]
        y_b.append(jnp.dot(yb_ref[...], wpb_ref[:, hh * wcol:(hh + 1) * wcol],
                           preferred_element_type=F32))
        o_hh = jnp.concatenate([o[hh, s] for s in range(n_sub)], axis=0)
        _hgrn_epilogue(o_hh + jnp.concatenate(o_inter, axis=0), z_ref, hg_ref, hh, oa_ref)
    gated_b = jax.nn.sigmoid(z_ref[:, OFF_MB:OFF_MB + D_MODEL]) * jnp.concatenate(y_b, axis=1)

    y_a = jnp.dot(oa_ref[...], wpa_ref[...], preferred_element_type=F32)
    merged = jax.nn.sigmoid(z_ref[:, OFF_MA:OFF_MA + D_MODEL]) * y_a + gated_b
    out = x + jnp.dot(merged.astype(BF16), wout_ref[...], preferred_element_type=F32)
    y_ref[0] = _rms(out, fg_ref[...])


def _shift_rows(x, k):
    n = x.shape[0]
    return x if k % n == 0 else pltpu.roll(x, (-k) % n, 0)


def _group_bcast(x, j, group):
    t = lax.broadcasted_iota(jnp.int32, x.shape, 0) & (group - 1)
    out = _shift_rows(x, j - (group - 1))
    for tt in range(group - 2, -1, -1):
        out = jnp.where(t == tt, _shift_rows(x, j - tt), out)
    return out


def _permute_rows(perm, x):
    h1 = x.astype(BF16)
    r1 = x - h1.astype(F32)
    h2 = r1.astype(BF16)
    h3 = (r1 - h2.astype(F32)).astype(BF16)
    return (jnp.dot(perm, h1, preferred_element_type=F32)
            + jnp.dot(perm, h2, preferred_element_type=F32)
            + jnp.dot(perm, h3, preferred_element_type=F32))


def _decode_prep_kernel(x_ref, p_in_ref, ng_ref, win_ref, lbl_ref, wpool_ref, ps_ref, wpb_ref,
                        a_ref, bv_ref, qd_ref, oi_ref, ga_ref, ma_ref, gb_ref, xs_ref, p_out_ref,
                        z_ref, rw_ref, yb_ref, *, n_seq):
    rows = n_seq * DEC_SEQ
    for i in range(n_seq):
        xs_ref[i * DEC_SEQ:(i + 1) * DEC_SEQ, :] = x_ref[i]
    _in_proj(xs_ref[...], ng_ref, win_ref, z_ref)
    lb = _lower_bound(lbl_ref)

    r = lax.broadcasted_iota(jnp.int32, (rows, DK), 0)
    t = r & (DEC_SEQ - 1)
    r8 = r & (PAIR_ROWS - 1)
    even = r8 < DEC_SEQ
    ones_even = jnp.where((r8 >= DEC_SEQ) & (r8 < DEC_SEQ + 3), 1.0, 0.0).astype(BF16)
    ones_odd = jnp.where(r8 < 3, 1.0, 0.0).astype(BF16)
    blk = SUB_TILE
    ri = lax.broadcasted_iota(jnp.int32, (blk, blk), 0)
    ci = lax.broadcasted_iota(jnp.int32, (blk, blk), 1)
    causal = ((ri // DEC_SEQ) == (ci // DEC_SEQ)) & (ci <= ri)

    for hh in range(N_HEADS):
        cs = slice(hh * DK, (hh + 1) * DK)
        qf, k, v, logf = _hgrn_inputs(z_ref, lb, hh)
        b = _group_cumsum(logf, DEC_SEQ)
        ref = _group_bcast(b, DEC_SEQ // 2, DEC_SEQ)
        bl = _group_bcast(b, DEC_SEQ - 1, DEC_SEQ)
        qs = (qf * jnp.exp(b - ref)).astype(BF16)
        ks = (k * jnp.exp(ref - b)).astype(BF16)
        kd = k * jnp.exp(bl - b)
        vb = v.astype(BF16)
        qd_ref[:, cs] = (qf * jnp.exp(b)).astype(BF16)
        for s in range(rows // blk):
            rs = slice(s * blk, (s + 1) * blk)
            sc = lax.dot_general(qs[rs], ks[rs], (((1,), (1,)), ((), ())),
                                 preferred_element_type=F32)
            oi_ref[rs, cs] = jnp.dot(jnp.where(causal, sc, 0.0).astype(BF16), vb[rs],
                                     preferred_element_type=F32)
        dec = jnp.exp(bl)
        d1 = dec.astype(BF16).astype(F32)
        d2 = (dec - d1).astype(BF16).astype(F32)
        d3 = (dec - d1 - d2).astype(BF16).astype(F32)
        tail = jnp.where(t == 0, d1, jnp.where(t == 1, d2, jnp.where(t == 2, d3, 0.0)))
        a_ref[hh, 0] = jnp.where(even, kd, _shift_rows(tail, -DEC_SEQ)).astype(BF16)
        a_ref[hh, 1] = jnp.where(even, _shift_rows(tail, DEC_SEQ), kd).astype(BF16)
        zero = jnp.zeros_like(vb)
        bv_ref[hh, 0, :, 0:DV] = jnp.where(even, vb, zero)
        bv_ref[hh, 0, :, DV:2 * DV] = ones_even
        bv_ref[hh, 1, :, 0:DV] = jnp.where(even, zero, vb)
        bv_ref[hh, 1, :, DV:2 * DV] = ones_odd
        ga_ref[:, cs] = jax.nn.silu(z_ref[:, OFF_GA + hh * DK:OFF_GA + (hh + 1) * DK])

    rif = lax.broadcasted_iota(jnp.int32, (rows, rows), 0)
    cif = lax.broadcasted_iota(jnp.int32, (rows, rows), 1)
    to_seq_major = jnp.where(cif == (rif & (DEC_SEQ - 1)) * n_seq + (rif // DEC_SEQ),
                             1.0, 0.0).astype(BF16)
    to_tok_major = jnp.where(rif == (cif & (DEC_SEQ - 1)) * n_seq + (cif // DEC_SEQ),
                             1.0, 0.0).astype(BF16)
    for g, w in enumerate(POOL_WINDOWS):
        gc = slice(g * G_B, (g + 1) * G_B)
        acc = None
        suffix = {}
        for j in range(POOL_BUF - 1, -1, -1):
            e = p_in_ref[j, :, gc]
            acc = e if acc is None else acc + e
            suffix[j] = acc
        for tt in range(DEC_SEQ):
            j = POOL_BUF + 1 - w + tt
            rw_ref[tt * n_seq:(tt + 1) * n_seq, gc] = (
                suffix[j] if j < POOL_BUF else jnp.zeros((n_seq, G_B), F32))
    rw = _permute_rows(to_seq_major, rw_ref[...])
    u_all = z_ref[:, OFF_U:OFF_U + W_BRANCH]
    for g, w in enumerate(POOL_WINDOWS):
        gc = slice(g * G_B, (g + 1) * G_B)
        u = u_all[:, gc]
        cu = u
        sh = 1
        while sh < min(w, DEC_SEQ):
            cu = cu + jnp.where(t >= sh, _shift_rows(cu, -sh), 0.0)
            sh *= 2
        pooled = (rw[:, gc] + cu) * (1.0 / w) - u
        _pool_epilogue(pooled, z_ref, g, wpool_ref, ps_ref, yb_ref)
    gb_ref[...] = jax.nn.sigmoid(z_ref[:, OFF_MB:OFF_MB + D_MODEL]) * jnp.dot(
        yb_ref[...], wpb_ref[...], preferred_element_type=F32)
    ma_ref[...] = jax.nn.sigmoid(z_ref[:, OFF_MA:OFF_MA + D_MODEL])

    keep = POOL_BUF - DEC_SEQ
    for j in range(keep):
        p_out_ref[j] = p_in_ref[j + DEC_SEQ]
    u_tm = _permute_rows(to_tok_major, u_all)
    for tt in range(DEC_SEQ):
        p_out_ref[keep + tt] = u_tm[tt * n_seq:(tt + 1) * n_seq, :]


def _decode_state_kernel(a_ref, bv_ref, qd_ref, s_in_ref, s_out_ref, oj_ref):
    n_pairs = BB_SAMPLE // 2
    even = (lax.broadcasted_iota(jnp.int32, (PAIR_ROWS, DV), 0)) < DEC_SEQ
    for hh in range(N_HEADS):
        for lp in range(n_pairs):
            rs = slice(lp * PAIR_ROWS, (lp + 1) * PAIR_ROWS)
            for e in range(2):
                upd = lax.dot_general(a_ref[hh, e, rs, :], bv_ref[hh, e, rs, :],
                                      (((0,), (0,)), ((), ())), preferred_element_type=F32)
                s_out_ref[2 * lp + e, hh] = (upd[:, DV:2 * DV] * s_in_ref[2 * lp + e, hh]
                                             + upd[:, 0:DV])
    for hh in range(N_HEADS):
        cs = slice(hh * DK, (hh + 1) * DK)
        for lp in range(n_pairs):
            rs = slice(lp * PAIR_ROWS, (lp + 1) * PAIR_ROWS)
            q8 = qd_ref[rs, cs]
            o0 = jnp.dot(q8, s_in_ref[2 * lp, hh].astype(BF16), preferred_element_type=F32)
            o1 = jnp.dot(q8, s_in_ref[2 * lp + 1, hh].astype(BF16), preferred_element_type=F32)
            oj_ref[rs, cs] = jnp.where(even, o0, o1)


def _decode_out_kernel(oi_ref, oj_ref, ga_ref, ma_ref, gb_ref, xs_ref, hg_ref, wpa_ref, wout_ref,
                       fg_ref, y_ref, oa_ref, *, n_seq):
    for hh in range(N_HEADS):
        cs = slice(hh * DK, (hh + 1) * DK)
        o = oi_ref[:, cs] + oj_ref[:, cs]
        oa_ref[:, cs] = (_rms(o, hg_ref[...]) * ga_ref[:, cs]).astype(BF16)
    y_a = jnp.dot(oa_ref[...], wpa_ref[...], preferred_element_type=F32)
    merged = ma_ref[...] * y_a + gb_ref[...]
    out = xs_ref[...] + jnp.dot(merged.astype(BF16), wout_ref[...], preferred_element_type=F32)
    y = _rms(out, fg_ref[...])
    for i in range(n_seq):
        y_ref[i] = y[i * DEC_SEQ:(i + 1) * DEC_SEQ, :]


def _whole(shape):
    n = len(shape)
    return pl.BlockSpec(shape, lambda *_: (0,) * n)


def _const_spec(shape):
    n = len(shape)
    return pl.BlockSpec(shape, lambda *_: (0,) * n, pipeline_mode=pl.Buffered(1))


def _weight_specs():
    return [
        _const_spec((1, D_MODEL)),
        _const_spec((D_MODEL, D_IN)),
        _const_spec((2, W_BRANCH)),
        _const_spec((1, DV)),
        _const_spec((W_BRANCH, D_MODEL)),
        _const_spec((len(POOL_WINDOWS), G_B, G_B)),
        _const_spec((1, W_BRANCH)),
        _const_spec((W_BRANCH, D_MODEL)),
        _const_spec((D_MODEL, D_MODEL)),
        _const_spec((1, D_MODEL)),
    ]


def kernel(x_prompt, x_sample, state_hgrn, state_pool, norm_g, w_in, lb_logits, hgrn_norm_g,
           w_proj_a, w_pool, pool_scale, w_proj_b, w_out, final_norm_g):
    batch, seq, _ = x_prompt.shape
    dec_batch, dec_seq, _ = x_sample.shape
    assert norm_g.shape[0] == 1 and lb_logits.shape[0] == 2, "single-layer decoder only"
    assert seq % TM_PROMPT == 0 and TM_PROMPT % SUB_TILE == 0 and dec_batch % BB_SAMPLE == 0
    assert dec_seq == DEC_SEQ and PAIR_ROWS == SUBLANES and BB_SAMPLE % 2 == 0
    assert (dec_batch * DEC_SEQ) % SUB_TILE == 0 and PAST_LEN >= max(POOL_WINDOWS)

    weights = (norm_g, w_in[0].astype(BF16), lb_logits, hgrn_norm_g,
               w_proj_a[0].astype(BF16), w_pool[0].astype(BF16), pool_scale,
               w_proj_b[0].astype(BF16), w_out[0].astype(BF16), final_norm_g.reshape(1, D_MODEL))

    tm = TM_PROMPT
    nt = seq // tm
    y_p, s_p, p_p = pl.pallas_call(
        _prompt_kernel,
        grid=(batch, nt),
        in_specs=[pl.BlockSpec((1, tm, D_MODEL), lambda b, t: (b, t, 0))] + _weight_specs(),
        out_specs=[
            pl.BlockSpec((1, tm, D_MODEL), lambda b, t: (b, t, 0)),
            pl.BlockSpec((1, N_HEADS, DK, DV), lambda b, t: (b, 0, 0, 0)),
            pl.BlockSpec((1, POOL_BUF, W_BRANCH), lambda b, t: (b, 0, 0)),
        ],
        out_shape=[
            jax.ShapeDtypeStruct((batch, seq, D_MODEL), F32),
            jax.ShapeDtypeStruct((batch, N_HEADS, DK, DV), F32),
            jax.ShapeDtypeStruct((batch, POOL_BUF, W_BRANCH), F32),
        ],
        scratch_shapes=[
            pltpu.VMEM((tm, D_IN), F32),
            pltpu.VMEM((N_HEADS, DV, DK), F32),
            pltpu.VMEM((tm + 2 * SUBLANES, W_BRANCH), F32),
            pltpu.VMEM((N_HEADS, tm, (SUB_TILE // CHUNK) * DK), BF16),
            pltpu.VMEM((N_HEADS, tm // CHUNK, DK, DV), BF16),
            pltpu.VMEM((tm, W_BRANCH), BF16),
            pltpu.VMEM((tm, W_BRANCH), BF16),
            pltpu.VMEM((tm, W_BRANCH), BF16),
            pltpu.VMEM((tm, W_BRANCH), BF16),
            pltpu.VMEM((tm, W_BRANCH), BF16),
            pltpu.VMEM((tm, W_BRANCH), BF16),
        ],
        compiler_params=pltpu.CompilerParams(
            dimension_semantics=("arbitrary", "arbitrary"),
            vmem_limit_bytes=VMEM_LIMIT_BYTES),
        name="hgrn2_pool_prompt",
    )(x_prompt, *weights)

    rows = dec_batch * DEC_SEQ
    pool_in = jnp.transpose(state_pool[0], (1, 0, 2))
    ng, win_b, lbl, hg, wpa_b, wpool_b, ps, wpb_b, wout_b, fg = weights
    prep_shapes = [
        jax.ShapeDtypeStruct((N_HEADS, 2, rows, DK), BF16),
        jax.ShapeDtypeStruct((N_HEADS, 2, rows, 2 * DV), BF16),
        jax.ShapeDtypeStruct((rows, W_BRANCH), BF16),
        jax.ShapeDtypeStruct((rows, W_BRANCH), F32),
        jax.ShapeDtypeStruct((rows, W_BRANCH), F32),
        jax.ShapeDtypeStruct((rows, D_MODEL), F32),
        jax.ShapeDtypeStruct((rows, D_MODEL), F32),
        jax.ShapeDtypeStruct((rows, D_MODEL), F32),
        jax.ShapeDtypeStruct((POOL_BUF, dec_batch, W_BRANCH), F32),
    ]
    a_t, bv_t, qd_t, oi_t, ga_t, ma_t, gb_t, xs_t, pool_out = pl.pallas_call(
        functools.partial(_decode_prep_kernel, n_seq=dec_batch),
        grid=(1,),
        in_specs=[_whole(x_sample.shape), _whole(pool_in.shape), _whole(ng.shape),
                  _whole(win_b.shape), _whole(lbl.shape), _whole(wpool_b.shape),
                  _whole(ps.shape), _whole(wpb_b.shape)],
        out_specs=[_whole(s.shape) for s in prep_shapes],
        out_shape=prep_shapes,
        scratch_shapes=[
            pltpu.VMEM((rows, D_IN), F32),
            pltpu.VMEM((rows, W_BRANCH), F32),
            pltpu.VMEM((rows, W_BRANCH), BF16),
        ],
        compiler_params=pltpu.CompilerParams(
            dimension_semantics=("arbitrary",), vmem_limit_bytes=VMEM_LIMIT_BYTES),
        name="hgrn2_pool_decode_prep",
    )(x_sample, pool_in, ng, win_b, lbl, wpool_b, ps, wpb_b)

    bb = BB_SAMPLE
    brows = bb * DEC_SEQ
    s_s, oj_t = pl.pallas_call(
        _decode_state_kernel,
        grid=(dec_batch // bb,),
        in_specs=[
            pl.BlockSpec((N_HEADS, 2, brows, DK), lambda i: (0, 0, i, 0)),
            pl.BlockSpec((N_HEADS, 2, brows, 2 * DV), lambda i: (0, 0, i, 0)),
            pl.BlockSpec((brows, W_BRANCH), lambda i: (i, 0)),
            pl.BlockSpec((bb, N_HEADS, DK, DV), lambda i: (i, 0, 0, 0)),
        ],
        out_specs=[
            pl.BlockSpec((bb, N_HEADS, DK, DV), lambda i: (i, 0, 0, 0)),
            pl.BlockSpec((brows, W_BRANCH), lambda i: (i, 0)),
        ],
        out_shape=[
            jax.ShapeDtypeStruct((dec_batch, N_HEADS, DK, DV), F32),
            jax.ShapeDtypeStruct((rows, W_BRANCH), F32),
        ],
        compiler_params=pltpu.CompilerParams(
            dimension_semantics=("arbitrary",), vmem_limit_bytes=VMEM_LIMIT_BYTES),
        name="hgrn2_pool_decode_state",
    )(a_t, bv_t, qd_t, state_hgrn[0])

    y_s = pl.pallas_call(
        functools.partial(_decode_out_kernel, n_seq=dec_batch),
        grid=(1,),
        in_specs=[_whole(oi_t.shape), _whole(oj_t.shape), _whole(ga_t.shape), _whole(ma_t.shape),
                  _whole(gb_t.shape), _whole(xs_t.shape), _whole(hg.shape), _whole(wpa_b.shape),
                  _whole(wout_b.shape), _whole(fg.shape)],
        out_specs=_whole(x_sample.shape),
        out_shape=jax.ShapeDtypeStruct(x_sample.shape, F32),
        scratch_shapes=[pltpu.VMEM((rows, W_BRANCH), BF16)],
        compiler_params=pltpu.CompilerParams(
            dimension_semantics=("arbitrary",), vmem_limit_bytes=VMEM_LIMIT_BYTES),
        name="hgrn2_pool_decode_out",
    )(oi_t, oj_t, ga_t, ma_t, gb_t, xs_t, hg, wpa_b, wout_b, fg)

    p_s = jnp.transpose(pool_out, (1, 0, 2))
    return (y_p, y_s, s_p[None], p_p[None], s_s[None], p_s[None])
```

```python
import functools

import jax
import jax.numpy as jnp
from jax import lax
from jax.experimental import pallas as pl
from jax.experimental.pallas import tpu as pltpu

F32 = jnp.float32
BF16 = jnp.bfloat16

D_MODEL = 1024
W_BRANCH = 512
N_HEADS = 4
DK = 128
DV = 128
CHUNK = 32
POOL_WINDOWS = (2, 4, 8, 16)
G_B = 128
POOL_BUF = 15
PAST_LEN = 16384
EPS = 1e-6
D_IN = 4 * W_BRANCH + 2 * W_BRANCH + 2 * D_MODEL
OFF_Q, OFF_F, OFF_I, OFF_GA = 0, 512, 1024, 1536
OFF_U, OFF_GB, OFF_MA, OFF_MB = 2048, 2560, 3072, 4096

SUBLANES = 8
VMEM_LIMIT_BYTES = 56 * 1024 * 1024

TM_PROMPT = 512
SUB_TILE = 256
BB_SAMPLE = 16
DEC_SEQ = 4
PAIR_ROWS = 2 * DEC_SEQ


def _rms(x, g):
    ms = jnp.mean(x * x, axis=-1, keepdims=True)
    return x * lax.rsqrt(ms + EPS) * g


def _lower_bound(lbl_ref):
    l0 = lbl_ref[0:1, :]
    l1 = lbl_ref[1:2, :]
    m = jnp.maximum(l0, l1)
    e0 = jnp.exp(l0 - m)
    e1 = jnp.exp(l1 - m)
    return e0 / (e0 + e1)


def _in_proj(x, ng_ref, win_ref, z_ref):
    h = _rms(x, ng_ref[...]).astype(BF16)
    z_ref[...] = jnp.dot(h, win_ref[...], preferred_element_type=F32)


def _hgrn_inputs(z_ref, lb, hh):
    c = hh * DK
    q = z_ref[:, OFF_Q + c:OFF_Q + c + DK]
    f = z_ref[:, OFF_F + c:OFF_F + c + DK]
    v = z_ref[:, OFF_I + c:OFF_I + c + DK]
    lbh = lb[:, c:c + DK]
    fg = lbh + (1.0 - lbh) * jax.nn.sigmoid(f)
    logf = jnp.log(fg)
    k = 1.0 - fg
    qf = jax.nn.silu(q) * (DK ** -0.5)
    return qf, k, v, logf


def _group_cumsum(x, group):
    row = lax.broadcasted_iota(jnp.int32, x.shape, 0) & (group - 1)
    s = 1
    while s < group:
        x = x + jnp.where(row >= s, pltpu.roll(x, s, 0), 0.0)
        s *= 2
    return x


def _hgrn_epilogue(o, z_ref, hg_ref, hh, oa_ref):
    c = hh * DK
    ga = z_ref[:, OFF_GA + c:OFF_GA + c + DK]
    on = _rms(o, hg_ref[...]) * jax.nn.silu(ga)
    oa_ref[:, c:c + DK] = on.astype(BF16)


def _pool_epilogue(pooled, z_ref, g, wpool_ref, ps_ref, yb_ref):
    c = g * G_B
    mixed = jnp.dot(pooled.astype(BF16), wpool_ref[g], preferred_element_type=F32)
    gb = z_ref[:, OFF_GB + c:OFF_GB + c + G_B]
    yb = mixed * ps_ref[:, c:c + G_B] * jax.nn.silu(gb)
    yb_ref[:, c:c + G_B] = yb.astype(BF16)


def _prompt_kernel(x_ref, ng_ref, win_ref, lbl_ref, hg_ref, wpa_ref, wpool_ref, ps_ref,
                   wpb_ref, wout_ref, fg_ref,
                   y_ref, s_out_ref, p_out_ref,
                   z_ref, st_ref, ext_ref, kx_ref, sn_ref, qs_ref, ks_ref, qd_ref, vb_ref,
                   oa_ref, yb_ref):
    tm = TM_PROMPT
    sub = SUB_TILE
    n_sub = tm // sub
    nc = tm // CHUNK
    ncs = sub // CHUNK
    t = pl.program_id(1)
    first = t == 0
    hdr = 2 * SUBLANES

    @pl.when((pl.program_id(0) == 0) & first)
    def _():
        st_ref[...] = jnp.zeros_like(st_ref)
        ext_ref[...] = jnp.zeros_like(ext_ref)
        kx_ref[...] = jnp.zeros_like(kx_ref)

    x = x_ref[0]
    _in_proj(x, ng_ref, win_ref, z_ref)
    lb = _lower_bound(lbl_ref)

    ri = lax.broadcasted_iota(jnp.int32, (sub, sub), 0)
    ci = lax.broadcasted_iota(jnp.int32, (sub, sub), 1)
    causal = ((ri // CHUNK) == (ci // CHUNK)) & (ci <= ri)

    heads = range(N_HEADS)
    units = [(hh, s) for hh in heads for s in range(n_sub)]

    def rows(s):
        return slice(s * sub, (s + 1) * sub)

    def cols(hh):
        return slice(hh * DK, (hh + 1) * DK)

    dec = []
    for hh in heads:
        qf, k, v, logf = _hgrn_inputs(z_ref, lb, hh)
        b = _group_cumsum(logf, CHUNK)
        b3 = b.reshape(nc, CHUNK, DK)
        ref = b3[:, CHUNK // 2:CHUNK // 2 + 1, :]
        bl = b3[:, CHUNK - 1:CHUNK, :]
        q3 = qf.reshape(nc, CHUNK, DK)
        k3 = k.reshape(nc, CHUNK, DK)
        qs_ref[:, cols(hh)] = (q3 * jnp.exp(b3 - ref)).reshape(tm, DK).astype(BF16)
        ks_ref[:, cols(hh)] = (k3 * jnp.exp(ref - b3)).reshape(tm, DK).astype(BF16)
        qd_ref[:, cols(hh)] = (q3 * jnp.exp(b3)).reshape(tm, DK).astype(BF16)
        vb_ref[:, cols(hh)] = v.astype(BF16)
        kd = (k3 * jnp.exp(bl - b3)).reshape(tm, DK).astype(BF16)
        dec.append(jnp.exp(bl))
        for c in range(nc):
            cc = c % ncs
            kx_ref[hh, c * CHUNK:(c + 1) * CHUNK, cc * DK:(cc + 1) * DK] = (
                kd[c * CHUNK:(c + 1) * CHUNK, :])

    ext_ref[0:hdr, :] = jnp.where(first, 0.0, ext_ref[tm:tm + hdr, :])
    ext_ref[hdr:hdr + tm, :] = z_ref[:, OFF_U:OFF_U + W_BRANCH]
    pos1 = t * tm + lax.broadcasted_iota(jnp.int32, (tm, 1), 0) + 1
    for g, w in enumerate(POOL_WINDOWS):
        c = g * G_B
        s = ext_ref[:, c:c + G_B]
        sh = 1
        while sh < w:
            s = s + pltpu.roll(s, sh, 0)
            sh *= 2
        inv = jnp.where(pos1 >= w, 1.0 / w, 1.0 / pos1.astype(F32))
        pooled = s[hdr:, :] * inv - z_ref[:, OFF_U + c:OFF_U + c + G_B]
        _pool_epilogue(pooled, z_ref, g, wpool_ref, ps_ref, yb_ref)
    p_out_ref[0] = ext_ref[hdr + tm - POOL_BUF:hdr + tm, :]

    sc = {(hh, s): lax.dot_general(qs_ref[rows(s), cols(hh)], ks_ref[rows(s), cols(hh)],
                                   (((1,), (1,)), ((), ())), preferred_element_type=F32)
          for hh, s in units}
    ut = {(hh, s): lax.dot_general(vb_ref[rows(s), cols(hh)], kx_ref[hh, rows(s), :],
                                   (((0,), (0,)), ((), ())), preferred_element_type=F32)
          for hh, s in units}
    o = {(hh, s): jnp.dot(jnp.where(causal, sc[hh, s], 0.0).astype(BF16),
                          vb_ref[rows(s), cols(hh)], preferred_element_type=F32)
         for hh, s in units}
    for hh in heads:
        st = jnp.where(first, 0.0, st_ref[hh])
        for c in range(nc):
            sn_ref[hh, c] = st.T.astype(BF16)
            st = dec[hh][c] * st + ut[hh, c // ncs][:, (c % ncs) * DK:(c % ncs + 1) * DK]
        st_ref[hh] = st
        s_out_ref[0, hh] = st.T
    wcol = D_MODEL // N_HEADS
    y_b = []
    for hh in heads:
        o_inter = [jnp.dot(qd_ref[c * CHUNK:(c + 1) * CHUNK, cols(hh)], sn_ref[hh, c],
                           preferred_element_type=F32) for c in range(nc)]
        y_b.append(jnp.dot(yb_ref[...], wpb_ref[:, hh * wcol:(hh + 1) * wcol],
                           preferred_element_type=F32))
        o_hh = jnp.concatenate([o[hh, s] for s in range(n_sub)], axis=0)
        _hgrn_epilogue(o_hh + jnp.concatenate(o_inter, axis=0), z_ref, hg_ref, hh, oa_ref)
    gated_b = jax.nn.sigmoid(z_ref[:, OFF_MB:OFF_MB + D_MODEL]) * jnp.concatenate(y_b, axis=1)

    y_a = jnp.dot(oa_ref[...], wpa_ref[...], preferred_element_type=F32)
    merged = jax.nn.sigmoid(z_ref[:, OFF_MA:OFF_MA + D_MODEL]) * y_a + gated_b
    out = x + jnp.dot(merged.astype(BF16), wout_ref[...], preferred_element_type=F32)
    y_ref[0] = _rms(out, fg_ref[...])


def _shift_rows(x, k):
    n = x.shape[0]
    return x if k % n == 0 else pltpu.roll(x, (-k) % n, 0)


def _group_bcast(x, j, group):
    t = lax.broadcasted_iota(jnp.int32, x.shape, 0) & (group - 1)
    out = _shift_rows(x, j - (group - 1))
    for tt in range(group - 2, -1, -1):
        out = jnp.where(t == tt, _shift_rows(x, j - tt), out)
    return out


def _decode_prep_kernel(x_ref, p_in_ref, ng_ref, win_ref, lbl_ref, wpool_ref, ps_ref, wpb_ref,
                        a_ref, bv_ref, qd_ref, oi_ref, ga_ref, ma_ref, gb_ref, p_out_ref,
                        xs_ref, z_ref, rw_ref, u_ref, yb_ref, *, n_seq):
    rows = n_seq * DEC_SEQ
    for i in range(n_seq):
        xs_ref[i * DEC_SEQ:(i + 1) * DEC_SEQ, :] = x_ref[i]
    _in_proj(xs_ref[...], ng_ref, win_ref, z_ref)
    lb = _lower_bound(lbl_ref)

    r = lax.broadcasted_iota(jnp.int32, (rows, DK), 0)
    t = r & (DEC_SEQ - 1)
    r8 = r & (PAIR_ROWS - 1)
    even = r8 < DEC_SEQ
    ones_even = jnp.where((r8 >= DEC_SEQ) & (r8 < DEC_SEQ + 3), 1.0, 0.0).astype(BF16)
    ones_odd = jnp.where(r8 < 3, 1.0, 0.0).astype(BF16)
    blk = SUB_TILE
    ri = lax.broadcasted_iota(jnp.int32, (blk, blk), 0)
    ci = lax.broadcasted_iota(jnp.int32, (blk, blk), 1)
    causal = ((ri // DEC_SEQ) == (ci // DEC_SEQ)) & (ci <= ri)

    for hh in range(N_HEADS):
        cs = slice(hh * DK, (hh + 1) * DK)
        qf, k, v, logf = _hgrn_inputs(z_ref, lb, hh)
        b = _group_cumsum(logf, DEC_SEQ)
        ref = _group_bcast(b, DEC_SEQ // 2, DEC_SEQ)
        bl = _group_bcast(b, DEC_SEQ - 1, DEC_SEQ)
        qs = (qf * jnp.exp(b - ref)).astype(BF16)
        ks = (k * jnp.exp(ref - b)).astype(BF16)
        kd = k * jnp.exp(bl - b)
        vb = v.astype(BF16)
        qd_ref[:, cs] = (qf * jnp.exp(b)).astype(BF16)
        for s in range(rows // blk):
            rs = slice(s * blk, (s + 1) * blk)
            sc = lax.dot_general(qs[rs], ks[rs], (((1,), (1,)), ((), ())),
                                 preferred_element_type=F32)
            oi_ref[rs, cs] = jnp.dot(jnp.where(causal, sc, 0.0).astype(BF16), vb[rs],
                                     preferred_element_type=F32)
        dec = jnp.exp(bl)
        d1 = dec.astype(BF16).astype(F32)
        d2 = (dec - d1).astype(BF16).astype(F32)
        d3 = (dec - d1 - d2).astype(BF16).astype(F32)
        tail = jnp.where(t == 0, d1, jnp.where(t == 1, d2, jnp.where(t == 2, d3, 0.0)))
        a_ref[hh, 0] = jnp.where(even, kd, _shift_rows(tail, -DEC_SEQ)).astype(BF16)
        a_ref[hh, 1] = jnp.where(even, _shift_rows(tail, DEC_SEQ), kd).astype(BF16)
        zero = jnp.zeros_like(vb)
        bv_ref[hh, 0, :, 0:DV] = jnp.where(even, vb, zero)
        bv_ref[hh, 0, :, DV:2 * DV] = ones_even
        bv_ref[hh, 1, :, 0:DV] = jnp.where(even, zero, vb)
        bv_ref[hh, 1, :, DV:2 * DV] = ones_odd
        ga_ref[:, cs] = jax.nn.silu(z_ref[:, OFF_GA + hh * DK:OFF_GA + (hh + 1) * DK])

    for g, w in enumerate(POOL_WINDOWS):
        gc = slice(g * G_B, (g + 1) * G_B)
        acc = None
        suffix = {}
        for j in range(POOL_BUF - 1, -1, -1):
            e = p_in_ref[j, :, gc]
            acc = e if acc is None else acc + e
            suffix[j] = acc
        for tt in range(DEC_SEQ):
            j = POOL_BUF + 1 - w + tt
            rw_ref[g, pl.ds(tt, n_seq, stride=DEC_SEQ), :] = (
                suffix[j] if j < POOL_BUF else jnp.zeros((n_seq, G_B), F32))
        u = z_ref[:, OFF_U + g * G_B:OFF_U + (g + 1) * G_B]
        u_ref[g] = u
        cu = u
        sh = 1
        while sh < min(w, DEC_SEQ):
            cu = cu + jnp.where(t >= sh, _shift_rows(cu, -sh), 0.0)
            sh *= 2
        pooled = (rw_ref[g] + cu) * (1.0 / w) - u
        _pool_epilogue(pooled, z_ref, g, wpool_ref, ps_ref, yb_ref)
    gb_ref[...] = jax.nn.sigmoid(z_ref[:, OFF_MB:OFF_MB + D_MODEL]) * jnp.dot(
        yb_ref[...], wpb_ref[...], preferred_element_type=F32)
    ma_ref[...] = jax.nn.sigmoid(z_ref[:, OFF_MA:OFF_MA + D_MODEL])

    keep = POOL_BUF - DEC_SEQ
    for j in range(keep):
        p_out_ref[j] = p_in_ref[j + DEC_SEQ]
    for tt in range(DEC_SEQ):
        for g in range(len(POOL_WINDOWS)):
            p_out_ref[keep + tt, :, g * G_B:(g + 1) * G_B] = (
                u_ref[g, pl.ds(tt, n_seq, stride=DEC_SEQ), :])


def _decode_state_kernel(a_ref, bv_ref, qd_ref, s_in_ref, oi_ref, ga_ref, ma_ref, gb_ref, x_ref,
                         hg_ref, wpa_ref, wout_ref, fg_ref,
                         s_out_ref, y_ref, xs_ref, oa_ref):
    bb = BB_SAMPLE
    n_pairs = bb // 2
    even = (lax.broadcasted_iota(jnp.int32, (PAIR_ROWS, DV), 0)) < DEC_SEQ
    for i in range(bb):
        xs_ref[i * DEC_SEQ:(i + 1) * DEC_SEQ, :] = x_ref[i]
    for hh in range(N_HEADS):
        for lp in range(n_pairs):
            rs = slice(lp * PAIR_ROWS, (lp + 1) * PAIR_ROWS)
            for e in range(2):
                upd = lax.dot_general(a_ref[hh, e, rs, :], bv_ref[hh, e, rs, :],
                                      (((0,), (0,)), ((), ())), preferred_element_type=F32)
                s_out_ref[2 * lp + e, hh] = (upd[:, DV:2 * DV] * s_in_ref[2 * lp + e, hh]
                                             + upd[:, 0:DV])
    for hh in range(N_HEADS):
        cs = slice(hh * DK, (hh + 1) * DK)
        o_inter = []
        for lp in range(n_pairs):
            q8 = qd_ref[lp * PAIR_ROWS:(lp + 1) * PAIR_ROWS, cs]
            o0 = jnp.dot(q8, s_in_ref[2 * lp, hh].astype(BF16), preferred_element_type=F32)
            o1 = jnp.dot(q8, s_in_ref[2 * lp + 1, hh].astype(BF16), preferred_element_type=F32)
            o_inter.append(jnp.where(even, o0, o1))
        o = oi_ref[:, cs] + jnp.concatenate(o_inter, axis=0)
        oa_ref[:, cs] = (_rms(o, hg_ref[...]) * ga_ref[:, cs]).astype(BF16)
    y_a = jnp.dot(oa_ref[...], wpa_ref[...], preferred_element_type=F32)
    merged = ma_ref[...] * y_a + gb_ref[...]
    out = xs_ref[...] + jnp.dot(merged.astype(BF16), wout_ref[...], preferred_element_type=F32)
    y = _rms(out, fg_ref[...])
    for i in range(bb):
        y_ref[i] = y[i * DEC_SEQ:(i + 1) * DEC_SEQ, :]


def _whole(shape):
    n = len(shape)
    return pl.BlockSpec(shape, lambda *_: (0,) * n)


def _const_spec(shape):
    n = len(shape)
    return pl.BlockSpec(shape, lambda *_: (0,) * n, pipeline_mode=pl.Buffered(1))


def _weight_specs():
    return [
        _const_spec((1, D_MODEL)),
        _const_spec((D_MODEL, D_IN)),
        _const_spec((2, W_BRANCH)),
        _const_spec((1, DV)),
        _const_spec((W_BRANCH, D_MODEL)),
        _const_spec((len(POOL_WINDOWS), G_B, G_B)),
        _const_spec((1, W_BRANCH)),
        _const_spec((W_BRANCH, D_MODEL)),
        _const_spec((D_MODEL, D_MODEL)),
        _const_spec((1, D_MODEL)),
    ]


def kernel(x_prompt, x_sample, state_hgrn, state_pool, norm_g, w_in, lb_logits, hgrn_norm_g,
           w_proj_a, w_pool, pool_scale, w_proj_b, w_out, final_norm_g):
    batch, seq, _ = x_prompt.shape
    dec_batch, dec_seq, _ = x_sample.shape
    assert norm_g.shape[0] == 1 and lb_logits.shape[0] == 2, "single-layer decoder only"
    assert seq % TM_PROMPT == 0 and TM_PROMPT % SUB_TILE == 0 and dec_batch % BB_SAMPLE == 0
    assert dec_seq == DEC_SEQ and PAIR_ROWS == SUBLANES and BB_SAMPLE % 2 == 0
    assert (dec_batch * DEC_SEQ) % SUB_TILE == 0 and PAST_LEN >= max(POOL_WINDOWS)

    weights = (norm_g, w_in[0].astype(BF16), lb_logits, hgrn_norm_g,
               w_proj_a[0].astype(BF16), w_pool[0].astype(BF16), pool_scale,
               w_proj_b[0].astype(BF16), w_out[0].astype(BF16), final_norm_g.reshape(1, D_MODEL))

    tm = TM_PROMPT
    nt = seq // tm
    y_p, s_p, p_p = pl.pallas_call(
        _prompt_kernel,
        grid=(batch, nt),
        in_specs=[pl.BlockSpec((1, tm, D_MODEL), lambda b, t: (b, t, 0))] + _weight_specs(),
        out_specs=[
            pl.BlockSpec((1, tm, D_MODEL), lambda b, t: (b, t, 0)),
            pl.BlockSpec((1, N_HEADS, DK, DV), lambda b, t: (b, 0, 0, 0)),
            pl.BlockSpec((1, POOL_BUF, W_BRANCH), lambda b, t: (b, 0, 0)),
        ],
        out_shape=[
            jax.ShapeDtypeStruct((batch, seq, D_MODEL), F32),
            jax.ShapeDtypeStruct((batch, N_HEADS, DK, DV), F32),
            jax.ShapeDtypeStruct((batch, POOL_BUF, W_BRANCH), F32),
        ],
        scratch_shapes=[
            pltpu.VMEM((tm, D_IN), F32),
            pltpu.VMEM((N_HEADS, DV, DK), F32),
            pltpu.VMEM((tm + 2 * SUBLANES, W_BRANCH), F32),
            pltpu.VMEM((N_HEADS, tm, (SUB_TILE // CHUNK) * DK), BF16),
            pltpu.VMEM((N_HEADS, tm // CHUNK, DK, DV), BF16),
            pltpu.VMEM((tm, W_BRANCH), BF16),
            pltpu.VMEM((tm, W_BRANCH), BF16),
            pltpu.VMEM((tm, W_BRANCH), BF16),
            pltpu.VMEM((tm, W_BRANCH), BF16),
            pltpu.VMEM((tm, W_BRANCH), BF16),
            pltpu.VMEM((tm, W_BRANCH), BF16),
        ],
        compiler_params=pltpu.CompilerParams(
            dimension_semantics=("arbitrary", "arbitrary"),
            vmem_limit_bytes=VMEM_LIMIT_BYTES),
        name="hgrn2_pool_prompt",
    )(x_prompt, *weights)

    rows = dec_batch * DEC_SEQ
    pool_in = jnp.transpose(state_pool[0], (1, 0, 2))
    ng, win_b, lbl, hg, wpa_b, wpool_b, ps, wpb_b, wout_b, fg = weights
    prep_shapes = [
        jax.ShapeDtypeStruct((N_HEADS, 2, rows, DK), BF16),
        jax.ShapeDtypeStruct((N_HEADS, 2, rows, 2 * DV), BF16),
        jax.ShapeDtypeStruct((rows, W_BRANCH), BF16),
        jax.ShapeDtypeStruct((rows, W_BRANCH), F32),
        jax.ShapeDtypeStruct((rows, W_BRANCH), F32),
        jax.ShapeDtypeStruct((rows, D_MODEL), F32),
        jax.ShapeDtypeStruct((rows, D_MODEL), F32),
        jax.ShapeDtypeStruct((POOL_BUF, dec_batch, W_BRANCH), F32),
    ]
    a_t, bv_t, qd_t, oi_t, ga_t, ma_t, gb_t, pool_out = pl.pallas_call(
        functools.partial(_decode_prep_kernel, n_seq=dec_batch),
        grid=(1,),
        in_specs=[_whole(x_sample.shape), _whole(pool_in.shape), _whole(ng.shape),
                  _whole(win_b.shape), _whole(lbl.shape), _whole(wpool_b.shape),
                  _whole(ps.shape), _whole(wpb_b.shape)],
        out_specs=[_whole(s.shape) for s in prep_shapes],
        out_shape=prep_shapes,
        scratch_shapes=[
            pltpu.VMEM((rows, D_MODEL), F32),
            pltpu.VMEM((rows, D_IN), F32),
            pltpu.VMEM((len(POOL_WINDOWS), rows, G_B), F32),
            pltpu.VMEM((len(POOL_WINDOWS), rows, G_B), F32),
            pltpu.VMEM((rows, W_BRANCH), BF16),
        ],
        compiler_params=pltpu.CompilerParams(
            dimension_semantics=("arbitrary",), vmem_limit_bytes=VMEM_LIMIT_BYTES),
        name="hgrn2_pool_decode_prep",
    )(x_sample, pool_in, ng, win_b, lbl, wpool_b, ps, wpb_b)

    bb = BB_SAMPLE
    brows = bb * DEC_SEQ
    s_s, y_s = pl.pallas_call(
        _decode_state_kernel,
        grid=(dec_batch // bb,),
        in_specs=[
            pl.BlockSpec((N_HEADS, 2, brows, DK), lambda i: (0, 0, i, 0)),
            pl.BlockSpec((N_HEADS, 2, brows, 2 * DV), lambda i: (0, 0, i, 0)),
            pl.BlockSpec((brows, W_BRANCH), lambda i: (i, 0)),
            pl.BlockSpec((bb, N_HEADS, DK, DV), lambda i: (i, 0, 0, 0)),
            pl.BlockSpec((brows, W_BRANCH), lambda i: (i, 0)),
            pl.BlockSpec((brows, W_BRANCH), lambda i: (i, 0)),
            pl.BlockSpec((brows, D_MODEL), lambda i: (i, 0)),
            pl.BlockSpec((brows, D_MODEL), lambda i: (i, 0)),
            pl.BlockSpec((bb, DEC_SEQ, D_MODEL), lambda i: (i, 0, 0)),
            _const_spec(hg.shape), _const_spec(wpa_b.shape), _const_spec(wout_b.shape),
            _const_spec(fg.shape),
        ],
        out_specs=[
            pl.BlockSpec((bb, N_HEADS, DK, DV), lambda i: (i, 0, 0, 0)),
            pl.BlockSpec((bb, DEC_SEQ, D_MODEL), lambda i: (i, 0, 0)),
        ],
        out_shape=[
            jax.ShapeDtypeStruct((dec_batch, N_HEADS, DK, DV), F32),
            jax.ShapeDtypeStruct(x_sample.shape, F32),
        ],
        scratch_shapes=[
            pltpu.VMEM((brows, D_MODEL), F32),
            pltpu.VMEM((brows, W_BRANCH), BF16),
        ],
        compiler_params=pltpu.CompilerParams(
            dimension_semantics=("arbitrary",), vmem_limit_bytes=VMEM_LIMIT_BYTES),
        name="hgrn2_pool_decode_state",
    )(a_t, bv_t, qd_t, state_hgrn[0], oi_t, ga_t, ma_t, gb_t, x_sample, hg, wpa_b, wout_b, fg)

    p_s = jnp.transpose(pool_out, (1, 0, 2))
    return (y_p, y_s, s_p[None], p_p[None], s_s[None], p_s[None])
```

```python
import functools

import jax
import jax.numpy as jnp
from jax import lax
from jax.experimental import pallas as pl
from jax.experimental.pallas import tpu as pltpu

F32 = jnp.float32
BF16 = jnp.bfloat16

D_MODEL = 1024
W_BRANCH = 512
N_HEADS = 4
DK = 128
DV = 128
CHUNK = 32
POOL_WINDOWS = (2, 4, 8, 16)
G_B = 128
POOL_BUF = 15
PAST_LEN = 16384
EPS = 1e-6
D_IN = 4 * W_BRANCH + 2 * W_BRANCH + 2 * D_MODEL
OFF_Q, OFF_F, OFF_I, OFF_GA = 0, 512, 1024, 1536
OFF_U, OFF_GB, OFF_MA, OFF_MB = 2048, 2560, 3072, 4096

SUBLANES = 8
VMEM_LIMIT_BYTES = 56 * 1024 * 1024

TM_PROMPT = 512
SUB_TILE = 256
PREP_SEQS = 64
BB_SAMPLE = 16
DEC_SEQ = 4
PAIR_ROWS = 2 * DEC_SEQ


def _rms(x, g):
    ms = jnp.mean(x * x, axis=-1, keepdims=True)
    return x * lax.rsqrt(ms + EPS) * g


def _lower_bound(lbl_ref):
    l0 = lbl_ref[0:1, :]
    l1 = lbl_ref[1:2, :]
    m = jnp.maximum(l0, l1)
    e0 = jnp.exp(l0 - m)
    e1 = jnp.exp(l1 - m)
    return e0 / (e0 + e1)


def _in_proj(x, ng_ref, win_ref, z_ref):
    h = _rms(x, ng_ref[...]).astype(BF16)
    z_ref[...] = jnp.dot(h, win_ref[...], preferred_element_type=F32)


def _hgrn_inputs(z_ref, lb, hh):
    c = hh * DK
    q = z_ref[:, OFF_Q + c:OFF_Q + c + DK]
    f = z_ref[:, OFF_F + c:OFF_F + c + DK]
    v = z_ref[:, OFF_I + c:OFF_I + c + DK]
    lbh = lb[:, c:c + DK]
    fg = lbh + (1.0 - lbh) * jax.nn.sigmoid(f)
    logf = jnp.log(fg)
    k = 1.0 - fg
    qf = jax.nn.silu(q) * (DK ** -0.5)
    return qf, k, v, logf


def _group_cumsum(x, group):
    row = lax.broadcasted_iota(jnp.int32, x.shape, 0) & (group - 1)
    s = 1
    while s < group:
        x = x + jnp.where(row >= s, pltpu.roll(x, s, 0), 0.0)
        s *= 2
    return x


def _hgrn_epilogue(o, z_ref, hg_ref, hh, oa_ref):
    c = hh * DK
    ga = z_ref[:, OFF_GA + c:OFF_GA + c + DK]
    on = _rms(o, hg_ref[...]) * jax.nn.silu(ga)
    oa_ref[:, c:c + DK] = on.astype(BF16)


def _pool_epilogue(pooled, z_ref, g, wpool_ref, ps_ref, yb_ref):
    c = g * G_B
    mixed = jnp.dot(pooled.astype(BF16), wpool_ref[g], preferred_element_type=F32)
    gb = z_ref[:, OFF_GB + c:OFF_GB + c + G_B]
    yb = mixed * ps_ref[:, c:c + G_B] * jax.nn.silu(gb)
    yb_ref[:, c:c + G_B] = yb.astype(BF16)


def _prompt_kernel(x_ref, ng_ref, win_ref, lbl_ref, hg_ref, wpa_ref, wpool_ref, ps_ref,
                   wpb_ref, wout_ref, fg_ref,
                   y_ref, s_out_ref, p_out_ref,
                   z_ref, st_ref, ext_ref, kx_ref, sn_ref, qs_ref, ks_ref, qd_ref, vb_ref,
                   oa_ref, yb_ref):
    tm = TM_PROMPT
    sub = SUB_TILE
    n_sub = tm // sub
    nc = tm // CHUNK
    ncs = sub // CHUNK
    t = pl.program_id(1)
    first = t == 0
    hdr = 2 * SUBLANES

    @pl.when((pl.program_id(0) == 0) & first)
    def _():
        st_ref[...] = jnp.zeros_like(st_ref)
        ext_ref[...] = jnp.zeros_like(ext_ref)
        kx_ref[...] = jnp.zeros_like(kx_ref)

    x = x_ref[0]
    _in_proj(x, ng_ref, win_ref, z_ref)
    lb = _lower_bound(lbl_ref)

    ri = lax.broadcasted_iota(jnp.int32, (sub, sub), 0)
    ci = lax.broadcasted_iota(jnp.int32, (sub, sub), 1)
    causal = ((ri // CHUNK) == (ci // CHUNK)) & (ci <= ri)

    heads = range(N_HEADS)
    units = [(hh, s) for hh in heads for s in range(n_sub)]

    def rows(s):
        return slice(s * sub, (s + 1) * sub)

    def cols(hh):
        return slice(hh * DK, (hh + 1) * DK)

    dec = []
    for hh in heads:
        qf, k, v, logf = _hgrn_inputs(z_ref, lb, hh)
        b = _group_cumsum(logf, CHUNK)
        b3 = b.reshape(nc, CHUNK, DK)
        ref = b3[:, CHUNK // 2:CHUNK // 2 + 1, :]
        bl = b3[:, CHUNK - 1:CHUNK, :]
        q3 = qf.reshape(nc, CHUNK, DK)
        k3 = k.reshape(nc, CHUNK, DK)
        qs_ref[:, cols(hh)] = (q3 * jnp.exp(b3 - ref)).reshape(tm, DK).astype(BF16)
        ks_ref[:, cols(hh)] = (k3 * jnp.exp(ref - b3)).reshape(tm, DK).astype(BF16)
        qd_ref[:, cols(hh)] = (q3 * jnp.exp(b3)).reshape(tm, DK).astype(BF16)
        vb_ref[:, cols(hh)] = v.astype(BF16)
        kd = (k3 * jnp.exp(bl - b3)).reshape(tm, DK).astype(BF16)
        dec.append(jnp.exp(bl))
        for c in range(nc):
            cc = c % ncs
            kx_ref[hh, c * CHUNK:(c + 1) * CHUNK, cc * DK:(cc + 1) * DK] = (
                kd[c * CHUNK:(c + 1) * CHUNK, :])

    ext_ref[0:hdr, :] = jnp.where(first, 0.0, ext_ref[tm:tm + hdr, :])
    ext_ref[hdr:hdr + tm, :] = z_ref[:, OFF_U:OFF_U + W_BRANCH]
    pos1 = t * tm + lax.broadcasted_iota(jnp.int32, (tm, 1), 0) + 1
    for g, w in enumerate(POOL_WINDOWS):
        c = g * G_B
        s = ext_ref[:, c:c + G_B]
        sh = 1
        while sh < w:
            s = s + pltpu.roll(s, sh, 0)
            sh *= 2
        inv = jnp.where(pos1 >= w, 1.0 / w, 1.0 / pos1.astype(F32))
        pooled = s[hdr:, :] * inv - z_ref[:, OFF_U + c:OFF_U + c + G_B]
        _pool_epilogue(pooled, z_ref, g, wpool_ref, ps_ref, yb_ref)
    p_out_ref[0] = ext_ref[hdr + tm - POOL_BUF:hdr + tm, :]

    sc = {(hh, s): lax.dot_general(qs_ref[rows(s), cols(hh)], ks_ref[rows(s), cols(hh)],
                                   (((1,), (1,)), ((), ())), preferred_element_type=F32)
          for hh, s in units}
    ut = {(hh, s): lax.dot_general(vb_ref[rows(s), cols(hh)], kx_ref[hh, rows(s), :],
                                   (((0,), (0,)), ((), ())), preferred_element_type=F32)
          for hh, s in units}
    o = {(hh, s): jnp.dot(jnp.where(causal, sc[hh, s], 0.0).astype(BF16),
                          vb_ref[rows(s), cols(hh)], preferred_element_type=F32)
         for hh, s in units}
    for hh in heads:
        st = jnp.where(first, 0.0, st_ref[hh])
        for c in range(nc):
            sn_ref[hh, c] = st.T.astype(BF16)
            st = dec[hh][c] * st + ut[hh, c // ncs][:, (c % ncs) * DK:(c % ncs + 1) * DK]
        st_ref[hh] = st
        s_out_ref[0, hh] = st.T
    wcol = D_MODEL // N_HEADS
    y_b = []
    for hh in heads:
        o_inter = [jnp.dot(qd_ref[c * CHUNK:(c + 1) * CHUNK, cols(hh)], sn_ref[hh, c],
                           preferred_element_type=F32) for c in range(nc)]
        y_b.append(jnp.dot(yb_ref[...], wpb_ref[:, hh * wcol:(hh + 1) * wcol],
                           preferred_element_type=F32))
        o_hh = jnp.concatenate([o[hh, s] for s in range(n_sub)], axis=0)
        _hgrn_epilogue(o_hh + jnp.concatenate(o_inter, axis=0), z_ref, hg_ref, hh, oa_ref)
    gated_b = jax.nn.sigmoid(z_ref[:, OFF_MB:OFF_MB + D_MODEL]) * jnp.concatenate(y_b, axis=1)

    y_a = jnp.dot(oa_ref[...], wpa_ref[...], preferred_element_type=F32)
    merged = jax.nn.sigmoid(z_ref[:, OFF_MA:OFF_MA + D_MODEL]) * y_a + gated_b
    out = x + jnp.dot(merged.astype(BF16), wout_ref[...], preferred_element_type=F32)
    y_ref[0] = _rms(out, fg_ref[...])


def _shift_rows(x, k):
    n = x.shape[0]
    return x if k % n == 0 else pltpu.roll(x, (-k) % n, 0)


def _group_bcast(x, j, group):
    t = lax.broadcasted_iota(jnp.int32, x.shape, 0) & (group - 1)
    out = _shift_rows(x, j - (group - 1))
    for tt in range(group - 2, -1, -1):
        out = jnp.where(t == tt, _shift_rows(x, j - tt), out)
    return out


def _decode_prep_kernel(x_ref, p_in_ref, ng_ref, win_ref, lbl_ref, wpool_ref, ps_ref, wpb_ref,
                        a_ref, bv_ref, qd_ref, oi_ref, ga_ref, ma_ref, gb_ref, p_out_ref,
                        xs_ref, z_ref, rw_ref, u_ref, yb_ref, *, n_seq):
    rows = n_seq * DEC_SEQ
    for i in range(n_seq):
        xs_ref[i * DEC_SEQ:(i + 1) * DEC_SEQ, :] = x_ref[i]
    _in_proj(xs_ref[...], ng_ref, win_ref, z_ref)
    lb = _lower_bound(lbl_ref)

    r = lax.broadcasted_iota(jnp.int32, (rows, DK), 0)
    t = r & (DEC_SEQ - 1)
    r8 = r & (PAIR_ROWS - 1)
    even = r8 < DEC_SEQ
    ones_even = jnp.where((r8 >= DEC_SEQ) & (r8 < DEC_SEQ + 3), 1.0, 0.0).astype(BF16)
    ones_odd = jnp.where(r8 < 3, 1.0, 0.0).astype(BF16)
    blk = SUB_TILE
    ri = lax.broadcasted_iota(jnp.int32, (blk, blk), 0)
    ci = lax.broadcasted_iota(jnp.int32, (blk, blk), 1)
    causal = ((ri // DEC_SEQ) == (ci // DEC_SEQ)) & (ci <= ri)

    for hh in range(N_HEADS):
        cs = slice(hh * DK, (hh + 1) * DK)
        qf, k, v, logf = _hgrn_inputs(z_ref, lb, hh)
        b = _group_cumsum(logf, DEC_SEQ)
        ref = _group_bcast(b, DEC_SEQ // 2, DEC_SEQ)
        bl = _group_bcast(b, DEC_SEQ - 1, DEC_SEQ)
        qs = (qf * jnp.exp(b - ref)).astype(BF16)
        ks = (k * jnp.exp(ref - b)).astype(BF16)
        kd = k * jnp.exp(bl - b)
        vb = v.astype(BF16)
        qd_ref[:, cs] = (qf * jnp.exp(b)).astype(BF16)
        for s in range(rows // blk):
            rs = slice(s * blk, (s + 1) * blk)
            sc = lax.dot_general(qs[rs], ks[rs], (((1,), (1,)), ((), ())),
                                 preferred_element_type=F32)
            oi_ref[rs, cs] = jnp.dot(jnp.where(causal, sc, 0.0).astype(BF16), vb[rs],
                                     preferred_element_type=F32)
        dec = jnp.exp(bl)
        d1 = dec.astype(BF16).astype(F32)
        d2 = (dec - d1).astype(BF16).astype(F32)
        d3 = (dec - d1 - d2).astype(BF16).astype(F32)
        tail = jnp.where(t == 0, d1, jnp.where(t == 1, d2, jnp.where(t == 2, d3, 0.0)))
        a_ref[hh, 0] = jnp.where(even, kd, _shift_rows(tail, -DEC_SEQ)).astype(BF16)
        a_ref[hh, 1] = jnp.where(even, _shift_rows(tail, DEC_SEQ), kd).astype(BF16)
        zero = jnp.zeros_like(vb)
        bv_ref[hh, 0, :, 0:DV] = jnp.where(even, vb, zero)
        bv_ref[hh, 0, :, DV:2 * DV] = ones_even
        bv_ref[hh, 1, :, 0:DV] = jnp.where(even, zero, vb)
        bv_ref[hh, 1, :, DV:2 * DV] = ones_odd
        ga_ref[:, cs] = jax.nn.silu(z_ref[:, OFF_GA + hh * DK:OFF_GA + (hh + 1) * DK])

    for g, w in enumerate(POOL_WINDOWS):
        gc = slice(g * G_B, (g + 1) * G_B)
        acc = None
        suffix = {}
        for j in range(POOL_BUF - 1, -1, -1):
            e = p_in_ref[j, :, gc]
            acc = e if acc is None else acc + e
            suffix[j] = acc
        for tt in range(DEC_SEQ):
            j = POOL_BUF + 1 - w + tt
            rw_ref[g, pl.ds(tt, n_seq, stride=DEC_SEQ), :] = (
                suffix[j] if j < POOL_BUF else jnp.zeros((n_seq, G_B), F32))
        u = z_ref[:, OFF_U + g * G_B:OFF_U + (g + 1) * G_B]
        u_ref[g] = u
        cu = u
        sh = 1
        while sh < min(w, DEC_SEQ):
            cu = cu + jnp.where(t >= sh, _shift_rows(cu, -sh), 0.0)
            sh *= 2
        pooled = (rw_ref[g] + cu) * (1.0 / w) - u
        _pool_epilogue(pooled, z_ref, g, wpool_ref, ps_ref, yb_ref)
    gb_ref[...] = jax.nn.sigmoid(z_ref[:, OFF_MB:OFF_MB + D_MODEL]) * jnp.dot(
        yb_ref[...], wpb_ref[...], preferred_element_type=F32)
    ma_ref[...] = jax.nn.sigmoid(z_ref[:, OFF_MA:OFF_MA + D_MODEL])

    keep = POOL_BUF - DEC_SEQ
    for j in range(keep):
        p_out_ref[j] = p_in_ref[j + DEC_SEQ]
    for tt in range(DEC_SEQ):
        for g in range(len(POOL_WINDOWS)):
            p_out_ref[keep + tt, :, g * G_B:(g + 1) * G_B] = (
                u_ref[g, pl.ds(tt, n_seq, stride=DEC_SEQ), :])


def _decode_state_kernel(a_ref, bv_ref, qd_ref, s_in_ref, oi_ref, ga_ref, ma_ref, gb_ref, x_ref,
                         hg_ref, wpa_ref, wout_ref, fg_ref,
                         s_out_ref, y_ref, xs_ref, oa_ref):
    bb = BB_SAMPLE
    n_pairs = bb // 2
    even = (lax.broadcasted_iota(jnp.int32, (PAIR_ROWS, DV), 0)) < DEC_SEQ
    for i in range(bb):
        xs_ref[i * DEC_SEQ:(i + 1) * DEC_SEQ, :] = x_ref[i]
    for hh in range(N_HEADS):
        for lp in range(n_pairs):
            rs = slice(lp * PAIR_ROWS, (lp + 1) * PAIR_ROWS)
            for e in range(2):
                upd = lax.dot_general(a_ref[hh, e, rs, :], bv_ref[hh, e, rs, :],
                                      (((0,), (0,)), ((), ())), preferred_element_type=F32)
                s_out_ref[2 * lp + e, hh] = (upd[:, DV:2 * DV] * s_in_ref[2 * lp + e, hh]
                                             + upd[:, 0:DV])
    for hh in range(N_HEADS):
        cs = slice(hh * DK, (hh + 1) * DK)
        o_inter = []
        for lp in range(n_pairs):
            q8 = qd_ref[lp * PAIR_ROWS:(lp + 1) * PAIR_ROWS, cs]
            o0 = jnp.dot(q8, s_in_ref[2 * lp, hh].astype(BF16), preferred_element_type=F32)
            o1 = jnp.dot(q8, s_in_ref[2 * lp + 1, hh].astype(BF16), preferred_element_type=F32)
            o_inter.append(jnp.where(even, o0, o1))
        o = oi_ref[:, cs] + jnp.concatenate(o_inter, axis=0)
        oa_ref[:, cs] = (_rms(o, hg_ref[...]) * ga_ref[:, cs]).astype(BF16)
    y_a = jnp.dot(oa_ref[...], wpa_ref[...], preferred_element_type=F32)
    merged = ma_ref[...] * y_a + gb_ref[...]
    out = xs_ref[...] + jnp.dot(merged.astype(BF16), wout_ref[...], preferred_element_type=F32)
    y = _rms(out, fg_ref[...])
    for i in range(bb):
        y_ref[i] = y[i * DEC_SEQ:(i + 1) * DEC_SEQ, :]


def _const_spec(shape):
    n = len(shape)
    return pl.BlockSpec(shape, lambda *_: (0,) * n, pipeline_mode=pl.Buffered(1))


def _weight_specs():
    return [
        _const_spec((1, D_MODEL)),
        _const_spec((D_MODEL, D_IN)),
        _const_spec((2, W_BRANCH)),
        _const_spec((1, DV)),
        _const_spec((W_BRANCH, D_MODEL)),
        _const_spec((len(POOL_WINDOWS), G_B, G_B)),
        _const_spec((1, W_BRANCH)),
        _const_spec((W_BRANCH, D_MODEL)),
        _const_spec((D_MODEL, D_MODEL)),
        _const_spec((1, D_MODEL)),
    ]


def kernel(x_prompt, x_sample, state_hgrn, state_pool, norm_g, w_in, lb_logits, hgrn_norm_g,
           w_proj_a, w_pool, pool_scale, w_proj_b, w_out, final_norm_g):
    batch, seq, _ = x_prompt.shape
    dec_batch, dec_seq, _ = x_sample.shape
    assert norm_g.shape[0] == 1 and lb_logits.shape[0] == 2, "single-layer decoder only"
    assert seq % TM_PROMPT == 0 and TM_PROMPT % SUB_TILE == 0 and dec_batch % BB_SAMPLE == 0
    assert dec_seq == DEC_SEQ and PAIR_ROWS == SUBLANES and BB_SAMPLE % 2 == 0
    assert dec_batch % PREP_SEQS == 0 and (PREP_SEQS * DEC_SEQ) % SUB_TILE == 0
    assert PAST_LEN >= max(POOL_WINDOWS)

    weights = (norm_g, w_in[0].astype(BF16), lb_logits, hgrn_norm_g,
               w_proj_a[0].astype(BF16), w_pool[0].astype(BF16), pool_scale,
               w_proj_b[0].astype(BF16), w_out[0].astype(BF16), final_norm_g.reshape(1, D_MODEL))

    tm = TM_PROMPT
    nt = seq // tm
    y_p, s_p, p_p = pl.pallas_call(
        _prompt_kernel,
        grid=(batch, nt),
        in_specs=[pl.BlockSpec((1, tm, D_MODEL), lambda b, t: (b, t, 0))] + _weight_specs(),
        out_specs=[
            pl.BlockSpec((1, tm, D_MODEL), lambda b, t: (b, t, 0)),
            pl.BlockSpec((1, N_HEADS, DK, DV), lambda b, t: (b, 0, 0, 0)),
            pl.BlockSpec((1, POOL_BUF, W_BRANCH), lambda b, t: (b, 0, 0)),
        ],
        out_shape=[
            jax.ShapeDtypeStruct((batch, seq, D_MODEL), F32),
            jax.ShapeDtypeStruct((batch, N_HEADS, DK, DV), F32),
            jax.ShapeDtypeStruct((batch, POOL_BUF, W_BRANCH), F32),
        ],
        scratch_shapes=[
            pltpu.VMEM((tm, D_IN), F32),
            pltpu.VMEM((N_HEADS, DV, DK), F32),
            pltpu.VMEM((tm + 2 * SUBLANES, W_BRANCH), F32),
            pltpu.VMEM((N_HEADS, tm, (SUB_TILE // CHUNK) * DK), BF16),
            pltpu.VMEM((N_HEADS, tm // CHUNK, DK, DV), BF16),
            pltpu.VMEM((tm, W_BRANCH), BF16),
            pltpu.VMEM((tm, W_BRANCH), BF16),
            pltpu.VMEM((tm, W_BRANCH), BF16),
            pltpu.VMEM((tm, W_BRANCH), BF16),
            pltpu.VMEM((tm, W_BRANCH), BF16),
            pltpu.VMEM((tm, W_BRANCH), BF16),
        ],
        compiler_params=pltpu.CompilerParams(
            dimension_semantics=("arbitrary", "arbitrary"),
            vmem_limit_bytes=VMEM_LIMIT_BYTES),
        name="hgrn2_pool_prompt",
    )(x_prompt, *weights)

    rows = dec_batch * DEC_SEQ
    pool_in = jnp.transpose(state_pool[0], (1, 0, 2))
    ng, win_b, lbl, hg, wpa_b, wpool_b, ps, wpb_b, wout_b, fg = weights
    prep_shapes = [
        jax.ShapeDtypeStruct((N_HEADS, 2, rows, DK), BF16),
        jax.ShapeDtypeStruct((N_HEADS, 2, rows, 2 * DV), BF16),
        jax.ShapeDtypeStruct((rows, W_BRANCH), BF16),
        jax.ShapeDtypeStruct((rows, W_BRANCH), F32),
        jax.ShapeDtypeStruct((rows, W_BRANCH), F32),
        jax.ShapeDtypeStruct((rows, D_MODEL), F32),
        jax.ShapeDtypeStruct((rows, D_MODEL), F32),
        jax.ShapeDtypeStruct((POOL_BUF, dec_batch, W_BRANCH), F32),
    ]
    pseq = PREP_SEQS
    prows = pseq * DEC_SEQ
    a_t, bv_t, qd_t, oi_t, ga_t, ma_t, gb_t, pool_out = pl.pallas_call(
        functools.partial(_decode_prep_kernel, n_seq=pseq),
        grid=(dec_batch // pseq,),
        in_specs=[pl.BlockSpec((pseq, DEC_SEQ, D_MODEL), lambda i: (i, 0, 0)),
                  pl.BlockSpec((POOL_BUF, pseq, W_BRANCH), lambda i: (0, i, 0)),
                  _const_spec(ng.shape), _const_spec(win_b.shape), _const_spec(lbl.shape),
                  _const_spec(wpool_b.shape), _const_spec(ps.shape), _const_spec(wpb_b.shape)],
        out_specs=[
            pl.BlockSpec((N_HEADS, 2, prows, DK), lambda i: (0, 0, i, 0)),
            pl.BlockSpec((N_HEADS, 2, prows, 2 * DV), lambda i: (0, 0, i, 0)),
            pl.BlockSpec((prows, W_BRANCH), lambda i: (i, 0)),
            pl.BlockSpec((prows, W_BRANCH), lambda i: (i, 0)),
            pl.BlockSpec((prows, W_BRANCH), lambda i: (i, 0)),
            pl.BlockSpec((prows, D_MODEL), lambda i: (i, 0)),
            pl.BlockSpec((prows, D_MODEL), lambda i: (i, 0)),
            pl.BlockSpec((POOL_BUF, pseq, W_BRANCH), lambda i: (0, i, 0)),
        ],
        out_shape=prep_shapes,
        scratch_shapes=[
            pltpu.VMEM((prows, D_MODEL), F32),
            pltpu.VMEM((prows, D_IN), F32),
            pltpu.VMEM((len(POOL_WINDOWS), prows, G_B), F32),
            pltpu.VMEM((len(POOL_WINDOWS), prows, G_B), F32),
            pltpu.VMEM((prows, W_BRANCH), BF16),
        ],
        compiler_params=pltpu.CompilerParams(
            dimension_semantics=("arbitrary",), vmem_limit_bytes=VMEM_LIMIT_BYTES),
        name="hgrn2_pool_decode_prep",
    )(x_sample, pool_in, ng, win_b, lbl, wpool_b, ps, wpb_b)

    bb = BB_SAMPLE
    brows = bb * DEC_SEQ
    s_s, y_s = pl.pallas_call(
        _decode_state_kernel,
        grid=(dec_batch // bb,),
        in_specs=[
            pl.BlockSpec((N_HEADS, 2, brows, DK), lambda i: (0, 0, i, 0)),
            pl.BlockSpec((N_HEADS, 2, brows, 2 * DV), lambda i: (0, 0, i, 0)),
            pl.BlockSpec((brows, W_BRANCH), lambda i: (i, 0)),
            pl.BlockSpec((bb, N_HEADS, DK, DV), lambda i: (i, 0, 0, 0)),
            pl.BlockSpec((brows, W_BRANCH), lambda i: (i, 0)),
            pl.BlockSpec((brows, W_BRANCH), lambda i: (i, 0)),
            pl.BlockSpec((brows, D_MODEL), lambda i: (i, 0)),
            pl.BlockSpec((brows, D_MODEL), lambda i: (i, 0)),
            pl.BlockSpec((bb, DEC_SEQ, D_MODEL), lambda i: (i, 0, 0)),
            _const_spec(hg.shape), _const_spec(wpa_b.shape), _const_spec(wout_b.shape),
            _const_spec(fg.shape),
        ],
        out_specs=[
            pl.BlockSpec((bb, N_HEADS, DK, DV), lambda i: (i, 0, 0, 0)),
            pl.BlockSpec((bb, DEC_SEQ, D_MODEL), lambda i: (i, 0, 0)),
        ],
        out_shape=[
            jax.ShapeDtypeStruct((dec_batch, N_HEADS, DK, DV), F32),
            jax.ShapeDtypeStruct(x_sample.shape, F32),
        ],
        scratch_shapes=[
            pltpu.VMEM((brows, D_MODEL), F32),
            pltpu.VMEM((brows, W_BRANCH), BF16),
        ],
        compiler_params=pltpu.CompilerParams(
            dimension_semantics=("arbitrary",), vmem_limit_bytes=VMEM_LIMIT_BYTES),
        name="hgrn2_pool_decode_state",
    )(a_t, bv_t, qd_t, state_hgrn[0], oi_t, ga_t, ma_t, gb_t, x_sample, hg, wpa_b, wout_b, fg)

    p_s = jnp.transpose(pool_out, (1, 0, 2))
    return (y_p, y_s, s_p[None], p_p[None], s_s[None], p_s[None])
```

```python
import functools

import jax
import jax.numpy as jnp
from jax import lax
from jax.experimental import pallas as pl
from jax.experimental.pallas import tpu as pltpu

F32 = jnp.float32
BF16 = jnp.bfloat16

D_MODEL = 1024
W_BRANCH = 512
N_HEADS = 4
DK = 128
DV = 128
CHUNK = 32
POOL_WINDOWS = (2, 4, 8, 16)
G_B = 128
POOL_BUF = 15
PAST_LEN = 16384
EPS = 1e-6
D_IN = 4 * W_BRANCH + 2 * W_BRANCH + 2 * D_MODEL
OFF_Q, OFF_F, OFF_I, OFF_GA = 0, 512, 1024, 1536
OFF_U, OFF_GB, OFF_MA, OFF_MB = 2048, 2560, 3072, 4096

SUBLANES = 8
VMEM_LIMIT_BYTES = 56 * 1024 * 1024

TM_PROMPT = 512
SUB_TILE = 256
PREP_SEQS = 64
BB_SAMPLE = 16
DEC_SEQ = 4
PAIR_ROWS = 2 * DEC_SEQ


def _rms(x, g):
    ms = jnp.mean(x * x, axis=-1, keepdims=True)
    return x * lax.rsqrt(ms + EPS) * g


def _lower_bound(lbl_ref):
    l0 = lbl_ref[0:1, :]
    l1 = lbl_ref[1:2, :]
    m = jnp.maximum(l0, l1)
    e0 = jnp.exp(l0 - m)
    e1 = jnp.exp(l1 - m)
    return e0 / (e0 + e1)


def _in_proj(x, ng_ref, win_ref, z_ref):
    h = _rms(x, ng_ref[...]).astype(BF16)
    z_ref[...] = jnp.dot(h, win_ref[...], preferred_element_type=F32)


def _hgrn_inputs(z_ref, lb, hh):
    c = hh * DK
    q = z_ref[:, OFF_Q + c:OFF_Q + c + DK]
    f = z_ref[:, OFF_F + c:OFF_F + c + DK]
    v = z_ref[:, OFF_I + c:OFF_I + c + DK]
    lbh = lb[:, c:c + DK]
    fg = lbh + (1.0 - lbh) * jax.nn.sigmoid(f)
    logf = jnp.log(fg)
    k = 1.0 - fg
    qf = jax.nn.silu(q) * (DK ** -0.5)
    return qf, k, v, logf


def _group_cumsum(x, group):
    row = lax.broadcasted_iota(jnp.int32, x.shape, 0) & (group - 1)
    s = 1
    while s < group:
        x = x + jnp.where(row >= s, pltpu.roll(x, s, 0), 0.0)
        s *= 2
    return x


def _hgrn_epilogue(o, z_ref, hg_ref, hh, oa_ref):
    c = hh * DK
    ga = z_ref[:, OFF_GA + c:OFF_GA + c + DK]
    on = _rms(o, hg_ref[...]) * jax.nn.silu(ga)
    oa_ref[:, c:c + DK] = on.astype(BF16)


def _pool_epilogue(pooled, z_ref, g, wpool_ref, ps_ref, yb_ref):
    c = g * G_B
    mixed = jnp.dot(pooled.astype(BF16), wpool_ref[g], preferred_element_type=F32)
    gb = z_ref[:, OFF_GB + c:OFF_GB + c + G_B]
    yb = mixed * ps_ref[:, c:c + G_B] * jax.nn.silu(gb)
    yb_ref[:, c:c + G_B] = yb.astype(BF16)


def _prompt_tile(x_ref, xp_ref, ng_ref, win_ref, lbl_ref, hg_ref, wpa_ref, wpool_ref, ps_ref,
                 wpb_ref, wout_ref, fg_ref,
                 y_ref, s_out_ref, p_out_ref,
                 z_ref, st_ref, ext_ref, kx_ref, sn_ref, qs_ref, ks_ref, qd_ref, vb_ref,
                 oa_ref, yb_ref, mg_ref, *, t):
    tm = TM_PROMPT
    sub = SUB_TILE
    n_sub = tm // sub
    nc = tm // CHUNK
    ncs = sub // CHUNK
    first = t == 0
    hdr = 2 * SUBLANES

    prev_out = jnp.dot(mg_ref[...], wout_ref[...], preferred_element_type=F32)
    h = _rms(x_ref[0], ng_ref[...]).astype(BF16)
    y_ref[0] = _rms(xp_ref[0] + prev_out, fg_ref[...])
    z_ref[...] = jnp.dot(h, win_ref[...], preferred_element_type=F32)
    lb = _lower_bound(lbl_ref)

    ri = lax.broadcasted_iota(jnp.int32, (sub, sub), 0)
    ci = lax.broadcasted_iota(jnp.int32, (sub, sub), 1)
    causal = ((ri // CHUNK) == (ci // CHUNK)) & (ci <= ri)

    heads = range(N_HEADS)
    units = [(hh, s) for hh in heads for s in range(n_sub)]

    def rows(s):
        return slice(s * sub, (s + 1) * sub)

    def cols(hh):
        return slice(hh * DK, (hh + 1) * DK)

    dec = []
    for hh in heads:
        qf, k, v, logf = _hgrn_inputs(z_ref, lb, hh)
        b = _group_cumsum(logf, CHUNK)
        b3 = b.reshape(nc, CHUNK, DK)
        ref = b3[:, CHUNK // 2:CHUNK // 2 + 1, :]
        bl = b3[:, CHUNK - 1:CHUNK, :]
        q3 = qf.reshape(nc, CHUNK, DK)
        k3 = k.reshape(nc, CHUNK, DK)
        qs_ref[:, cols(hh)] = (q3 * jnp.exp(b3 - ref)).reshape(tm, DK).astype(BF16)
        ks_ref[:, cols(hh)] = (k3 * jnp.exp(ref - b3)).reshape(tm, DK).astype(BF16)
        qd_ref[:, cols(hh)] = (q3 * jnp.exp(b3)).reshape(tm, DK).astype(BF16)
        vb_ref[:, cols(hh)] = v.astype(BF16)
        kd = (k3 * jnp.exp(bl - b3)).reshape(tm, DK).astype(BF16)
        dec.append(jnp.exp(bl))
        for c in range(nc):
            cc = c % ncs
            kx_ref[hh, c * CHUNK:(c + 1) * CHUNK, cc * DK:(cc + 1) * DK] = (
                kd[c * CHUNK:(c + 1) * CHUNK, :])

    ext_ref[0:hdr, :] = jnp.where(first, 0.0, ext_ref[tm:tm + hdr, :])
    ext_ref[hdr:hdr + tm, :] = z_ref[:, OFF_U:OFF_U + W_BRANCH]
    pos1 = t * tm + lax.broadcasted_iota(jnp.int32, (tm, 1), 0) + 1
    for g, w in enumerate(POOL_WINDOWS):
        c = g * G_B
        s = ext_ref[:, c:c + G_B]
        sh = 1
        while sh < w:
            s = s + pltpu.roll(s, sh, 0)
            sh *= 2
        inv = jnp.where(pos1 >= w, 1.0 / w, 1.0 / pos1.astype(F32))
        pooled = s[hdr:, :] * inv - z_ref[:, OFF_U + c:OFF_U + c + G_B]
        _pool_epilogue(pooled, z_ref, g, wpool_ref, ps_ref, yb_ref)
    p_out_ref[0] = ext_ref[hdr + tm - POOL_BUF:hdr + tm, :]

    sc = {(hh, s): lax.dot_general(qs_ref[rows(s), cols(hh)], ks_ref[rows(s), cols(hh)],
                                   (((1,), (1,)), ((), ())), preferred_element_type=F32)
          for hh, s in units}
    ut = {(hh, s): lax.dot_general(vb_ref[rows(s), cols(hh)], kx_ref[hh, rows(s), :],
                                   (((0,), (0,)), ((), ())), preferred_element_type=F32)
          for hh, s in units}
    o = {(hh, s): jnp.dot(jnp.where(causal, sc[hh, s], 0.0).astype(BF16),
                          vb_ref[rows(s), cols(hh)], preferred_element_type=F32)
         for hh, s in units}
    for hh in heads:
        st = jnp.where(first, 0.0, st_ref[hh])
        for c in range(nc):
            sn_ref[hh, c] = st.T.astype(BF16)
            st = dec[hh][c] * st + ut[hh, c // ncs][:, (c % ncs) * DK:(c % ncs + 1) * DK]
        st_ref[hh] = st
        s_out_ref[0, hh] = st.T
    wcol = D_MODEL // N_HEADS
    y_b = []
    for hh in heads:
        o_inter = [jnp.dot(qd_ref[c * CHUNK:(c + 1) * CHUNK, cols(hh)], sn_ref[hh, c],
                           preferred_element_type=F32) for c in range(nc)]
        y_b.append(jnp.dot(yb_ref[...], wpb_ref[:, hh * wcol:(hh + 1) * wcol],
                           preferred_element_type=F32))
        o_hh = jnp.concatenate([o[hh, s] for s in range(n_sub)], axis=0)
        _hgrn_epilogue(o_hh + jnp.concatenate(o_inter, axis=0), z_ref, hg_ref, hh, oa_ref)
    gated_b = jax.nn.sigmoid(z_ref[:, OFF_MB:OFF_MB + D_MODEL]) * jnp.concatenate(y_b, axis=1)

    y_a = jnp.dot(oa_ref[...], wpa_ref[...], preferred_element_type=F32)
    merged = jax.nn.sigmoid(z_ref[:, OFF_MA:OFF_MA + D_MODEL]) * y_a + gated_b
    mg_ref[...] = merged.astype(BF16)


def _prompt_kernel(x_ref, xp_ref, ng_ref, win_ref, lbl_ref, hg_ref, wpa_ref, wpool_ref, ps_ref,
                   wpb_ref, wout_ref, fg_ref,
                   y_ref, s_out_ref, p_out_ref,
                   z_ref, st_ref, ext_ref, kx_ref, sn_ref, qs_ref, ks_ref, qd_ref, vb_ref,
                   oa_ref, yb_ref, mg_ref, *, nt, n_tiles):
    s = pl.program_id(0)

    @pl.when(s == 0)
    def _():
        st_ref[...] = jnp.zeros_like(st_ref)
        ext_ref[...] = jnp.zeros_like(ext_ref)
        kx_ref[...] = jnp.zeros_like(kx_ref)
        mg_ref[...] = jnp.zeros_like(mg_ref)

    @pl.when(s < n_tiles)
    def _():
        _prompt_tile(x_ref, xp_ref, ng_ref, win_ref, lbl_ref, hg_ref, wpa_ref, wpool_ref, ps_ref,
                     wpb_ref, wout_ref, fg_ref, y_ref, s_out_ref, p_out_ref,
                     z_ref, st_ref, ext_ref, kx_ref, sn_ref, qs_ref, ks_ref, qd_ref, vb_ref,
                     oa_ref, yb_ref, mg_ref, t=lax.rem(s, nt))

    @pl.when(s == n_tiles)
    def _():
        out = xp_ref[0] + jnp.dot(mg_ref[...], wout_ref[...], preferred_element_type=F32)
        y_ref[0] = _rms(out, fg_ref[...])


def _shift_rows(x, k):
    n = x.shape[0]
    return x if k % n == 0 else pltpu.roll(x, (-k) % n, 0)


def _group_bcast(x, j, group):
    t = lax.broadcasted_iota(jnp.int32, x.shape, 0) & (group - 1)
    out = _shift_rows(x, j - (group - 1))
    for tt in range(group - 2, -1, -1):
        out = jnp.where(t == tt, _shift_rows(x, j - tt), out)
    return out


def _decode_prep_kernel(x_ref, p_in_ref, ng_ref, win_ref, lbl_ref, wpool_ref, ps_ref, wpb_ref,
                        a_ref, bv_ref, qd_ref, oi_ref, ga_ref, ma_ref, gb_ref, p_out_ref,
                        xs_ref, z_ref, rw_ref, u_ref, yb_ref, *, n_seq):
    rows = n_seq * DEC_SEQ
    for i in range(n_seq):
        xs_ref[i * DEC_SEQ:(i + 1) * DEC_SEQ, :] = x_ref[i]
    _in_proj(xs_ref[...], ng_ref, win_ref, z_ref)
    lb = _lower_bound(lbl_ref)

    r = lax.broadcasted_iota(jnp.int32, (rows, DK), 0)
    t = r & (DEC_SEQ - 1)
    r8 = r & (PAIR_ROWS - 1)
    even = r8 < DEC_SEQ
    ones_even = jnp.where((r8 >= DEC_SEQ) & (r8 < DEC_SEQ + 3), 1.0, 0.0).astype(BF16)
    ones_odd = jnp.where(r8 < 3, 1.0, 0.0).astype(BF16)
    blk = SUB_TILE
    ri = lax.broadcasted_iota(jnp.int32, (blk, blk), 0)
    ci = lax.broadcasted_iota(jnp.int32, (blk, blk), 1)
    causal = ((ri // DEC_SEQ) == (ci // DEC_SEQ)) & (ci <= ri)

    for hh in range(N_HEADS):
        cs = slice(hh * DK, (hh + 1) * DK)
        qf, k, v, logf = _hgrn_inputs(z_ref, lb, hh)
        b = _group_cumsum(logf, DEC_SEQ)
        ref = _group_bcast(b, DEC_SEQ // 2, DEC_SEQ)
        bl = _group_bcast(b, DEC_SEQ - 1, DEC_SEQ)
        qs = (qf * jnp.exp(b - ref)).astype(BF16)
        ks = (k * jnp.exp(ref - b)).astype(BF16)
        kd = k * jnp.exp(bl - b)
        vb = v.astype(BF16)
        qd_ref[:, cs] = (qf * jnp.exp(b)).astype(BF16)
        for s in range(rows // blk):
            rs = slice(s * blk, (s + 1) * blk)
            sc = lax.dot_general(qs[rs], ks[rs], (((1,), (1,)), ((), ())),
                                 preferred_element_type=F32)
            oi_ref[rs, cs] = jnp.dot(jnp.where(causal, sc, 0.0).astype(BF16), vb[rs],
                                     preferred_element_type=F32)
        dec = jnp.exp(bl)
        d1 = dec.astype(BF16).astype(F32)
        d2 = (dec - d1).astype(BF16).astype(F32)
        d3 = (dec - d1 - d2).astype(BF16).astype(F32)
        tail = jnp.where(t == 0, d1, jnp.where(t == 1, d2, jnp.where(t == 2, d3, 0.0)))
        a_ref[hh, 0] = jnp.where(even, kd, _shift_rows(tail, -DEC_SEQ)).astype(BF16)
        a_ref[hh, 1] = jnp.where(even, _shift_rows(tail, DEC_SEQ), kd).astype(BF16)
        zero = jnp.zeros_like(vb)
        bv_ref[hh, 0, :, 0:DV] = jnp.where(even, vb, zero)
        bv_ref[hh, 0, :, DV:2 * DV] = ones_even
        bv_ref[hh, 1, :, 0:DV] = jnp.where(even, zero, vb)
        bv_ref[hh, 1, :, DV:2 * DV] = ones_odd
        ga_ref[:, cs] = jax.nn.silu(z_ref[:, OFF_GA + hh * DK:OFF_GA + (hh + 1) * DK])

    for g, w in enumerate(POOL_WINDOWS):
        gc = slice(g * G_B, (g + 1) * G_B)
        acc = None
        suffix = {}
        for j in range(POOL_BUF - 1, -1, -1):
            e = p_in_ref[j, :, gc]
            acc = e if acc is None else acc + e
            suffix[j] = acc
        for tt in range(DEC_SEQ):
            j = POOL_BUF + 1 - w + tt
            rw_ref[g, pl.ds(tt, n_seq, stride=DEC_SEQ), :] = (
                suffix[j] if j < POOL_BUF else jnp.zeros((n_seq, G_B), F32))
        u = z_ref[:, OFF_U + g * G_B:OFF_U + (g + 1) * G_B]
        u_ref[g] = u
        cu = u
        sh = 1
        while sh < min(w, DEC_SEQ):
            cu = cu + jnp.where(t >= sh, _shift_rows(cu, -sh), 0.0)
            sh *= 2
        pooled = (rw_ref[g] + cu) * (1.0 / w) - u
        _pool_epilogue(pooled, z_ref, g, wpool_ref, ps_ref, yb_ref)
    gb_ref[...] = jax.nn.sigmoid(z_ref[:, OFF_MB:OFF_MB + D_MODEL]) * jnp.dot(
        yb_ref[...], wpb_ref[...], preferred_element_type=F32)
    ma_ref[...] = jax.nn.sigmoid(z_ref[:, OFF_MA:OFF_MA + D_MODEL])

    keep = POOL_BUF - DEC_SEQ
    for j in range(keep):
        p_out_ref[j] = p_in_ref[j + DEC_SEQ]
    for tt in range(DEC_SEQ):
        for g in range(len(POOL_WINDOWS)):
            p_out_ref[keep + tt, :, g * G_B:(g + 1) * G_B] = (
                u_ref[g, pl.ds(tt, n_seq, stride=DEC_SEQ), :])


def _decode_state_kernel(a_ref, bv_ref, qd_ref, s_in_ref, oi_ref, ga_ref, ma_ref, gb_ref, x_ref,
                         hg_ref, wpa_ref, wout_ref, fg_ref,
                         s_out_ref, y_ref, xs_ref, oa_ref):
    bb = BB_SAMPLE
    n_pairs = bb // 2
    even = (lax.broadcasted_iota(jnp.int32, (PAIR_ROWS, DV), 0)) < DEC_SEQ
    for i in range(bb):
        xs_ref[i * DEC_SEQ:(i + 1) * DEC_SEQ, :] = x_ref[i]
    for hh in range(N_HEADS):
        for lp in range(n_pairs):
            rs = slice(lp * PAIR_ROWS, (lp + 1) * PAIR_ROWS)
            for e in range(2):
                upd = lax.dot_general(a_ref[hh, e, rs, :], bv_ref[hh, e, rs, :],
                                      (((0,), (0,)), ((), ())), preferred_element_type=F32)
                s_out_ref[2 * lp + e, hh] = (upd[:, DV:2 * DV] * s_in_ref[2 * lp + e, hh]
                                             + upd[:, 0:DV])
    for hh in range(N_HEADS):
        cs = slice(hh * DK, (hh + 1) * DK)
        o_inter = []
        for lp in range(n_pairs):
            q8 = qd_ref[lp * PAIR_ROWS:(lp + 1) * PAIR_ROWS, cs]
            o0 = jnp.dot(q8, s_in_ref[2 * lp, hh].astype(BF16), preferred_element_type=F32)
            o1 = jnp.dot(q8, s_in_ref[2 * lp + 1, hh].astype(BF16), preferred_element_type=F32)
            o_inter.append(jnp.where(even, o0, o1))
        o = oi_ref[:, cs] + jnp.concatenate(o_inter, axis=0)
        oa_ref[:, cs] = (_rms(o, hg_ref[...]) * ga_ref[:, cs]).astype(BF16)
    y_a = jnp.dot(oa_ref[...], wpa_ref[...], preferred_element_type=F32)
    merged = ma_ref[...] * y_a + gb_ref[...]
    out = xs_ref[...] + jnp.dot(merged.astype(BF16), wout_ref[...], preferred_element_type=F32)
    y = _rms(out, fg_ref[...])
    for i in range(bb):
        y_ref[i] = y[i * DEC_SEQ:(i + 1) * DEC_SEQ, :]


def _const_spec(shape):
    n = len(shape)
    return pl.BlockSpec(shape, lambda *_: (0,) * n, pipeline_mode=pl.Buffered(1))


def _weight_specs():
    return [
        _const_spec((1, D_MODEL)),
        _const_spec((D_MODEL, D_IN)),
        _const_spec((2, W_BRANCH)),
        _const_spec((1, DV)),
        _const_spec((W_BRANCH, D_MODEL)),
        _const_spec((len(POOL_WINDOWS), G_B, G_B)),
        _const_spec((1, W_BRANCH)),
        _const_spec((W_BRANCH, D_MODEL)),
        _const_spec((D_MODEL, D_MODEL)),
        _const_spec((1, D_MODEL)),
    ]


def kernel(x_prompt, x_sample, state_hgrn, state_pool, norm_g, w_in, lb_logits, hgrn_norm_g,
           w_proj_a, w_pool, pool_scale, w_proj_b, w_out, final_norm_g):
    batch, seq, _ = x_prompt.shape
    dec_batch, dec_seq, _ = x_sample.shape
    assert norm_g.shape[0] == 1 and lb_logits.shape[0] == 2, "single-layer decoder only"
    assert seq % TM_PROMPT == 0 and TM_PROMPT % SUB_TILE == 0 and dec_batch % BB_SAMPLE == 0
    assert dec_seq == DEC_SEQ and PAIR_ROWS == SUBLANES and BB_SAMPLE % 2 == 0
    assert dec_batch % PREP_SEQS == 0 and (PREP_SEQS * DEC_SEQ) % SUB_TILE == 0
    assert PAST_LEN >= max(POOL_WINDOWS)

    weights = (norm_g, w_in[0].astype(BF16), lb_logits, hgrn_norm_g,
               w_proj_a[0].astype(BF16), w_pool[0].astype(BF16), pool_scale,
               w_proj_b[0].astype(BF16), w_out[0].astype(BF16), final_norm_g.reshape(1, D_MODEL))

    tm = TM_PROMPT
    nt = seq // tm
    n_tiles = batch * nt

    def current(s):
        j = jnp.minimum(s, n_tiles - 1)
        return j // nt, j % nt

    def previous(s):
        j = jnp.maximum(s - 1, 0)
        return j // nt, j % nt

    y_p, s_p, p_p = pl.pallas_call(
        functools.partial(_prompt_kernel, nt=nt, n_tiles=n_tiles),
        grid=(n_tiles + 1,),
        in_specs=[pl.BlockSpec((1, tm, D_MODEL), lambda s: (*current(s), 0)),
                  pl.BlockSpec((1, tm, D_MODEL), lambda s: (*previous(s), 0))] + _weight_specs(),
        out_specs=[
            pl.BlockSpec((1, tm, D_MODEL), lambda s: (*previous(s), 0)),
            pl.BlockSpec((1, N_HEADS, DK, DV), lambda s: (current(s)[0], 0, 0, 0)),
            pl.BlockSpec((1, POOL_BUF, W_BRANCH), lambda s: (current(s)[0], 0, 0)),
        ],
        out_shape=[
            jax.ShapeDtypeStruct((batch, seq, D_MODEL), F32),
            jax.ShapeDtypeStruct((batch, N_HEADS, DK, DV), F32),
            jax.ShapeDtypeStruct((batch, POOL_BUF, W_BRANCH), F32),
        ],
        scratch_shapes=[
            pltpu.VMEM((tm, D_IN), F32),
            pltpu.VMEM((N_HEADS, DV, DK), F32),
            pltpu.VMEM((tm + 2 * SUBLANES, W_BRANCH), F32),
            pltpu.VMEM((N_HEADS, tm, (SUB_TILE // CHUNK) * DK), BF16),
            pltpu.VMEM((N_HEADS, tm // CHUNK, DK, DV), BF16),
            pltpu.VMEM((tm, W_BRANCH), BF16),
            pltpu.VMEM((tm, W_BRANCH), BF16),
            pltpu.VMEM((tm, W_BRANCH), BF16),
            pltpu.VMEM((tm, W_BRANCH), BF16),
            pltpu.VMEM((tm, W_BRANCH), BF16),
            pltpu.VMEM((tm, W_BRANCH), BF16),
            pltpu.VMEM((tm, D_MODEL), BF16),
        ],
        compiler_params=pltpu.CompilerParams(
            dimension_semantics=("arbitrary",),
            vmem_limit_bytes=VMEM_LIMIT_BYTES),
        name="hgrn2_pool_prompt",
    )(x_prompt, x_prompt, *weights)

    rows = dec_batch * DEC_SEQ
    pool_in = jnp.transpose(state_pool[0], (1, 0, 2))
    ng, win_b, lbl, hg, wpa_b, wpool_b, ps, wpb_b, wout_b, fg = weights
    prep_shapes = [
        jax.ShapeDtypeStruct((N_HEADS, 2, rows, DK), BF16),
        jax.ShapeDtypeStruct((N_HEADS, 2, rows, 2 * DV), BF16),
        jax.ShapeDtypeStruct((rows, W_BRANCH), BF16),
        jax.ShapeDtypeStruct((rows, W_BRANCH), F32),
        jax.ShapeDtypeStruct((rows, W_BRANCH), F32),
        jax.ShapeDtypeStruct((rows, D_MODEL), F32),
        jax.ShapeDtypeStruct((rows, D_MODEL), F32),
        jax.ShapeDtypeStruct((POOL_BUF, dec_batch, W_BRANCH), F32),
    ]
    pseq = PREP_SEQS
    prows = pseq * DEC_SEQ
    a_t, bv_t, qd_t, oi_t, ga_t, ma_t, gb_t, pool_out = pl.pallas_call(
        functools.partial(_decode_prep_kernel, n_seq=pseq),
        grid=(dec_batch // pseq,),
        in_specs=[pl.BlockSpec((pseq, DEC_SEQ, D_MODEL), lambda i: (i, 0, 0)),
                  pl.BlockSpec((POOL_BUF, pseq, W_BRANCH), lambda i: (0, i, 0)),
                  _const_spec(ng.shape), _const_spec(win_b.shape), _const_spec(lbl.shape),
                  _const_spec(wpool_b.shape), _const_spec(ps.shape), _const_spec(wpb_b.shape)],
        out_specs=[
            pl.BlockSpec((N_HEADS, 2, prows, DK), lambda i: (0, 0, i, 0)),
            pl.BlockSpec((N_HEADS, 2, prows, 2 * DV), lambda i: (0, 0, i, 0)),
            pl.BlockSpec((prows, W_BRANCH), lambda i: (i, 0)),
            pl.BlockSpec((prows, W_BRANCH), lambda i: (i, 0)),
            pl.BlockSpec((prows, W_BRANCH), lambda i: (i, 0)),
            pl.BlockSpec((prows, D_MODEL), lambda i: (i, 0)),
            pl.BlockSpec((prows, D_MODEL), lambda i: (i, 0)),
            pl.BlockSpec((POOL_BUF, pseq, W_BRANCH), lambda i: (0, i, 0)),
        ],
        out_shape=prep_shapes,
        scratch_shapes=[
            pltpu.VMEM((prows, D_MODEL), F32),
            pltpu.VMEM((prows, D_IN), F32),
            pltpu.VMEM((len(POOL_WINDOWS), prows, G_B), F32),
            pltpu.VMEM((len(POOL_WINDOWS), prows, G_B), F32),
            pltpu.VMEM((prows, W_BRANCH), BF16),
        ],
        compiler_params=pltpu.CompilerParams(
            dimension_semantics=("arbitrary",), vmem_limit_bytes=VMEM_LIMIT_BYTES),
        name="hgrn2_pool_decode_prep",
    )(x_sample, pool_in, ng, win_b, lbl, wpool_b, ps, wpb_b)

    bb = BB_SAMPLE
    brows = bb * DEC_SEQ
    s_s, y_s = pl.pallas_call(
        _decode_state_kernel,
        grid=(dec_batch // bb,),
        in_specs=[
            pl.BlockSpec((N_HEADS, 2, brows, DK), lambda i: (0, 0, i, 0)),
            pl.BlockSpec((N_HEADS, 2, brows, 2 * DV), lambda i: (0, 0, i, 0)),
            pl.BlockSpec((brows, W_BRANCH), lambda i: (i, 0)),
            pl.BlockSpec((bb, N_HEADS, DK, DV), lambda i: (i, 0, 0, 0)),
            pl.BlockSpec((brows, W_BRANCH), lambda i: (i, 0)),
            pl.BlockSpec((brows, W_BRANCH), lambda i: (i, 0)),
            pl.BlockSpec((brows, D_MODEL), lambda i: (i, 0)),
            pl.BlockSpec((brows, D_MODEL), lambda i: (i, 0)),
            pl.BlockSpec((bb, DEC_SEQ, D_MODEL), lambda i: (i, 0, 0)),
            _const_spec(hg.shape), _const_spec(wpa_b.shape), _const_spec(wout_b.shape),
            _const_spec(fg.shape),
        ],
        out_specs=[
            pl.BlockSpec((bb, N_HEADS, DK, DV), lambda i: (i, 0, 0, 0)),
            pl.BlockSpec((bb, DEC_SEQ, D_MODEL), lambda i: (i, 0, 0)),
        ],
        out_shape=[
            jax.ShapeDtypeStruct((dec_batch, N_HEADS, DK, DV), F32),
            jax.ShapeDtypeStruct(x_sample.shape, F32),
        ],
        scratch_shapes=[
            pltpu.VMEM((brows, D_MODEL), F32),
            pltpu.VMEM((brows, W_BRANCH), BF16),
        ],
        compiler_params=pltpu.CompilerParams(
            dimension_semantics=("arbitrary",), vmem_limit_bytes=VMEM_LIMIT_BYTES),
        name="hgrn2_pool_decode_state",
    )(a_t, bv_t, qd_t, state_hgrn[0], oi_t, ga_t, ma_t, gb_t, x_sample, hg, wpa_b, wout_b, fg)

    p_s = jnp.transpose(pool_out, (1, 0, 2))
    return (y_p, y_s, s_p[None], p_p[None], s_s[None], p_s[None])
```

```python
import functools

import jax
import jax.numpy as jnp
from jax import lax
from jax.experimental import pallas as pl
from jax.experimental.pallas import tpu as pltpu

F32 = jnp.float32
BF16 = jnp.bfloat16

D_MODEL = 1024
W_BRANCH = 512
N_HEADS = 4
DK = 128
DV = 128
CHUNK = 32
POOL_WINDOWS = (2, 4, 8, 16)
G_B = 128
POOL_BUF = 15
PAST_LEN = 16384
EPS = 1e-6
D_IN = 4 * W_BRANCH + 2 * W_BRANCH + 2 * D_MODEL
OFF_Q, OFF_F, OFF_I, OFF_GA = 0, 512, 1024, 1536
OFF_U, OFF_GB, OFF_MA, OFF_MB = 2048, 2560, 3072, 4096

SUBLANES = 8
VMEM_LIMIT_BYTES = 56 * 1024 * 1024

TM_PROMPT = 512
SUB_TILE = 256
PREP_SEQS = 64
BB_SAMPLE = 16
DEC_SEQ = 4
PAIR_ROWS = 2 * DEC_SEQ


def _rms(x, g):
    ms = jnp.mean(x * x, axis=-1, keepdims=True)
    return x * lax.rsqrt(ms + EPS) * g


def _lower_bound(lbl_ref):
    l0 = lbl_ref[0:1, :]
    l1 = lbl_ref[1:2, :]
    m = jnp.maximum(l0, l1)
    e0 = jnp.exp(l0 - m)
    e1 = jnp.exp(l1 - m)
    return e0 / (e0 + e1)


def _in_proj(x, ng_ref, win_ref, z_ref):
    h = _rms(x, ng_ref[...]).astype(BF16)
    z_ref[...] = jnp.dot(h, win_ref[...], preferred_element_type=F32)


def _hgrn_inputs(z_ref, lb, hh):
    c = hh * DK
    q = z_ref[:, OFF_Q + c:OFF_Q + c + DK]
    f = z_ref[:, OFF_F + c:OFF_F + c + DK]
    v = z_ref[:, OFF_I + c:OFF_I + c + DK]
    lbh = lb[:, c:c + DK]
    fg = lbh + (1.0 - lbh) * jax.nn.sigmoid(f)
    logf = jnp.log(fg)
    k = 1.0 - fg
    qf = jax.nn.silu(q) * (DK ** -0.5)
    return qf, k, v, logf


def _group_cumsum(x, group):
    row = lax.broadcasted_iota(jnp.int32, x.shape, 0) & (group - 1)
    s = 1
    while s < group:
        x = x + jnp.where(row >= s, pltpu.roll(x, s, 0), 0.0)
        s *= 2
    return x


def _hgrn_epilogue(o, z_ref, hg_ref, hh, oa_ref):
    c = hh * DK
    ga = z_ref[:, OFF_GA + c:OFF_GA + c + DK]
    on = _rms(o, hg_ref[...]) * jax.nn.silu(ga)
    oa_ref[:, c:c + DK] = on.astype(BF16)


def _pool_epilogue(pooled, z_ref, g, wpool_ref, ps_ref, yb_ref):
    c = g * G_B
    mixed = jnp.dot(pooled.astype(BF16), wpool_ref[g], preferred_element_type=F32)
    gb = z_ref[:, OFF_GB + c:OFF_GB + c + G_B]
    yb = mixed * ps_ref[:, c:c + G_B] * jax.nn.silu(gb)
    yb_ref[:, c:c + G_B] = yb.astype(BF16)


def _prompt_kernel(x_ref, ng_ref, win_ref, lbl_ref, hg_ref, wpa_ref, wpool_ref, ps_ref,
                   wpb_ref, wout_ref, fg_ref,
                   y_ref, s_out_ref, p_out_ref,
                   z_ref, st_ref, ext_ref, kx_ref, sn_ref, qs_ref, ks_ref, qd_ref, kd_ref, qp_ref,
                   vb_ref, oa_ref, yb_ref):
    tm = TM_PROMPT
    sub = SUB_TILE
    n_sub = tm // sub
    nc = tm // CHUNK
    ncs = sub // CHUNK
    t = pl.program_id(1)
    first = t == 0
    hdr = 2 * SUBLANES

    @pl.when((pl.program_id(0) == 0) & first)
    def _():
        st_ref[...] = jnp.zeros_like(st_ref)
        ext_ref[...] = jnp.zeros_like(ext_ref)
        kx_ref[...] = jnp.zeros_like(kx_ref)

    x = x_ref[0]
    _in_proj(x, ng_ref, win_ref, z_ref)
    lb = _lower_bound(lbl_ref)

    ri = lax.broadcasted_iota(jnp.int32, (sub, sub), 0)
    ci = lax.broadcasted_iota(jnp.int32, (sub, sub), 1)
    causal = ((ri // CHUNK) == (ci // CHUNK)) & (ci <= ri)
    cross = (((ri // (2 * CHUNK)) == (ci // (2 * CHUNK)))
             & ((ri // CHUNK) % 2 == 1) & ((ci // CHUNK) % 2 == 0))

    heads = range(N_HEADS)
    units = [(hh, s) for hh in heads for s in range(n_sub)]

    def rows(s):
        return slice(s * sub, (s + 1) * sub)

    def cols(hh):
        return slice(hh * DK, (hh + 1) * DK)

    np_ = nc // 2
    nps = ncs // 2
    pair_rows = 2 * CHUNK
    dpair = []
    for hh in heads:
        qf, k, v, logf = _hgrn_inputs(z_ref, lb, hh)
        b = _group_cumsum(logf, CHUNK)
        b3 = b.reshape(nc, CHUNK, DK)
        ref = b3[:, CHUNK // 2:CHUNK // 2 + 1, :]
        bl = b3[:, CHUNK - 1:CHUNK, :]
        q3 = qf.reshape(nc, CHUNK, DK)
        k3 = k.reshape(nc, CHUNK, DK)
        qs_ref[:, cols(hh)] = (q3 * jnp.exp(b3 - ref)).reshape(tm, DK).astype(BF16)
        ks_ref[:, cols(hh)] = (k3 * jnp.exp(ref - b3)).reshape(tm, DK).astype(BF16)
        vb_ref[:, cols(hh)] = v.astype(BF16)
        qd3 = q3 * jnp.exp(b3)
        kd3 = k3 * jnp.exp(bl - b3)
        qd_ref[:, cols(hh)] = qd3.reshape(tm, DK).astype(BF16)
        kd_ref[:, cols(hh)] = kd3.reshape(tm, DK).astype(BF16)
        dec4 = jnp.exp(bl).reshape(np_, 2, 1, DK)
        d_a, d_b = dec4[:, 0:1], dec4[:, 1:2]
        qd4 = qd3.reshape(np_, 2, CHUNK, DK)
        kd4 = kd3.reshape(np_, 2, CHUNK, DK)
        qp_ref[:, cols(hh)] = jnp.concatenate(
            [qd4[:, 0:1], qd4[:, 1:2] * d_a], axis=1).reshape(tm, DK).astype(BF16)
        kp = jnp.concatenate(
            [kd4[:, 0:1] * d_b, kd4[:, 1:2]], axis=1).reshape(tm, DK).astype(BF16)
        dpair.append((d_a * d_b).reshape(np_, 1, DK))
        for p in range(np_):
            pc = p % nps
            kx_ref[hh, p * pair_rows:(p + 1) * pair_rows, pc * DK:(pc + 1) * DK] = (
                kp[p * pair_rows:(p + 1) * pair_rows, :])

    ext_ref[0:hdr, :] = jnp.where(first, 0.0, ext_ref[tm:tm + hdr, :])
    ext_ref[hdr:hdr + tm, :] = z_ref[:, OFF_U:OFF_U + W_BRANCH]
    pos1 = t * tm + lax.broadcasted_iota(jnp.int32, (tm, 1), 0) + 1
    for g, w in enumerate(POOL_WINDOWS):
        c = g * G_B
        s = ext_ref[:, c:c + G_B]
        sh = 1
        while sh < w:
            s = s + pltpu.roll(s, sh, 0)
            sh *= 2
        inv = jnp.where(pos1 >= w, 1.0 / w, 1.0 / pos1.astype(F32))
        pooled = s[hdr:, :] * inv - z_ref[:, OFF_U + c:OFF_U + c + G_B]
        _pool_epilogue(pooled, z_ref, g, wpool_ref, ps_ref, yb_ref)
    p_out_ref[0] = ext_ref[hdr + tm - POOL_BUF:hdr + tm, :]

    nt_dims = (((1,), (1,)), ((), ()))
    sc = {(hh, s): lax.dot_general(qs_ref[rows(s), cols(hh)], ks_ref[rows(s), cols(hh)], nt_dims,
                                   preferred_element_type=F32) for hh, s in units}
    sx = {(hh, s): lax.dot_general(qd_ref[rows(s), cols(hh)], kd_ref[rows(s), cols(hh)], nt_dims,
                                   preferred_element_type=F32) for hh, s in units}
    ut = {(hh, s): lax.dot_general(vb_ref[rows(s), cols(hh)], kx_ref[hh, rows(s), :],
                                   (((0,), (0,)), ((), ())), preferred_element_type=F32)
          for hh, s in units}
    o = {(hh, s): jnp.dot(
        jnp.where(causal, sc[hh, s], jnp.where(cross, sx[hh, s], 0.0)).astype(BF16),
        vb_ref[rows(s), cols(hh)], preferred_element_type=F32) for hh, s in units}
    for hh in heads:
        st = jnp.where(first, 0.0, st_ref[hh])
        for p in range(np_):
            sn_ref[hh, p] = st.T.astype(BF16)
            st = dpair[hh][p] * st + ut[hh, p // nps][:, (p % nps) * DK:(p % nps + 1) * DK]
        st_ref[hh] = st
        s_out_ref[0, hh] = st.T
    wcol = D_MODEL // N_HEADS
    y_b = []
    for hh in heads:
        o_inter = [jnp.dot(qp_ref[p * pair_rows:(p + 1) * pair_rows, cols(hh)], sn_ref[hh, p],
                           preferred_element_type=F32) for p in range(np_)]
        y_b.append(jnp.dot(yb_ref[...], wpb_ref[:, hh * wcol:(hh + 1) * wcol],
                           preferred_element_type=F32))
        o_hh = jnp.concatenate([o[hh, s] for s in range(n_sub)], axis=0)
        _hgrn_epilogue(o_hh + jnp.concatenate(o_inter, axis=0), z_ref, hg_ref, hh, oa_ref)
    gated_b = jax.nn.sigmoid(z_ref[:, OFF_MB:OFF_MB + D_MODEL]) * jnp.concatenate(y_b, axis=1)

    y_a = jnp.dot(oa_ref[...], wpa_ref[...], preferred_element_type=F32)
    merged = jax.nn.sigmoid(z_ref[:, OFF_MA:OFF_MA + D_MODEL]) * y_a + gated_b
    out = x + jnp.dot(merged.astype(BF16), wout_ref[...], preferred_element_type=F32)
    y_ref[0] = _rms(out, fg_ref[...])


def _shift_rows(x, k):
    n = x.shape[0]
    return x if k % n == 0 else pltpu.roll(x, (-k) % n, 0)


def _group_bcast(x, j, group):
    t = lax.broadcasted_iota(jnp.int32, x.shape, 0) & (group - 1)
    out = _shift_rows(x, j - (group - 1))
    for tt in range(group - 2, -1, -1):
        out = jnp.where(t == tt, _shift_rows(x, j - tt), out)
    return out


def _decode_prep_kernel(x_ref, p_in_ref, ng_ref, win_ref, lbl_ref, wpool_ref, ps_ref, wpb_ref,
                        a_ref, bv_ref, qd_ref, oi_ref, ga_ref, ma_ref, gb_ref, p_out_ref,
                        xs_ref, z_ref, rw_ref, u_ref, yb_ref, *, n_seq):
    rows = n_seq * DEC_SEQ
    for i in range(n_seq):
        xs_ref[i * DEC_SEQ:(i + 1) * DEC_SEQ, :] = x_ref[i]
    _in_proj(xs_ref[...], ng_ref, win_ref, z_ref)
    lb = _lower_bound(lbl_ref)

    r = lax.broadcasted_iota(jnp.int32, (rows, DK), 0)
    t = r & (DEC_SEQ - 1)
    r8 = r & (PAIR_ROWS - 1)
    even = r8 < DEC_SEQ
    ones_even = jnp.where((r8 >= DEC_SEQ) & (r8 < DEC_SEQ + 3), 1.0, 0.0).astype(BF16)
    ones_odd = jnp.where(r8 < 3, 1.0, 0.0).astype(BF16)
    blk = SUB_TILE
    ri = lax.broadcasted_iota(jnp.int32, (blk, blk), 0)
    ci = lax.broadcasted_iota(jnp.int32, (blk, blk), 1)
    causal = ((ri // DEC_SEQ) == (ci // DEC_SEQ)) & (ci <= ri)

    for hh in range(N_HEADS):
        cs = slice(hh * DK, (hh + 1) * DK)
        qf, k, v, logf = _hgrn_inputs(z_ref, lb, hh)
        b = _group_cumsum(logf, DEC_SEQ)
        ref = _group_bcast(b, DEC_SEQ // 2, DEC_SEQ)
        bl = _group_bcast(b, DEC_SEQ - 1, DEC_SEQ)
        qs = (qf * jnp.exp(b - ref)).astype(BF16)
        ks = (k * jnp.exp(ref - b)).astype(BF16)
        kd = k * jnp.exp(bl - b)
        vb = v.astype(BF16)
        qd_ref[:, cs] = (qf * jnp.exp(b)).astype(BF16)
        for s in range(rows // blk):
            rs = slice(s * blk, (s + 1) * blk)
            sc = lax.dot_general(qs[rs], ks[rs], (((1,), (1,)), ((), ())),
                                 preferred_element_type=F32)
            oi_ref[rs, cs] = jnp.dot(jnp.where(causal, sc, 0.0).astype(BF16), vb[rs],
                                     preferred_element_type=F32)
        dec = jnp.exp(bl)
        d1 = dec.astype(BF16).astype(F32)
        d2 = (dec - d1).astype(BF16).astype(F32)
        d3 = (dec - d1 - d2).astype(BF16).astype(F32)
        tail = jnp.where(t == 0, d1, jnp.where(t == 1, d2, jnp.where(t == 2, d3, 0.0)))
        a_ref[hh, 0] = jnp.where(even, kd, _shift_rows(tail, -DEC_SEQ)).astype(BF16)
        a_ref[hh, 1] = jnp.where(even, _shift_rows(tail, DEC_SEQ), kd).astype(BF16)
        zero = jnp.zeros_like(vb)
        bv_ref[hh, 0, :, 0:DV] = jnp.where(even, vb, zero)
        bv_ref[hh, 0, :, DV:2 * DV] = ones_even
        bv_ref[hh, 1, :, 0:DV] = jnp.where(even, zero, vb)
        bv_ref[hh, 1, :, DV:2 * DV] = ones_odd
        ga_ref[:, cs] = jax.nn.silu(z_ref[:, OFF_GA + hh * DK:OFF_GA + (hh + 1) * DK])

    for g, w in enumerate(POOL_WINDOWS):
        gc = slice(g * G_B, (g + 1) * G_B)
        acc = None
        suffix = {}
        for j in range(POOL_BUF - 1, -1, -1):
            e = p_in_ref[j, :, gc]
            acc = e if acc is None else acc + e
            suffix[j] = acc
        for tt in range(DEC_SEQ):
            j = POOL_BUF + 1 - w + tt
            rw_ref[g, pl.ds(tt, n_seq, stride=DEC_SEQ), :] = (
                suffix[j] if j < POOL_BUF else jnp.zeros((n_seq, G_B), F32))
        u = z_ref[:, OFF_U + g * G_B:OFF_U + (g + 1) * G_B]
        u_ref[g] = u
        cu = u
        sh = 1
        while sh < min(w, DEC_SEQ):
            cu = cu + jnp.where(t >= sh, _shift_rows(cu, -sh), 0.0)
            sh *= 2
        pooled = (rw_ref[g] + cu) * (1.0 / w) - u
        _pool_epilogue(pooled, z_ref, g, wpool_ref, ps_ref, yb_ref)
    gb_ref[...] = jax.nn.sigmoid(z_ref[:, OFF_MB:OFF_MB + D_MODEL]) * jnp.dot(
        yb_ref[...], wpb_ref[...], preferred_element_type=F32)
    ma_ref[...] = jax.nn.sigmoid(z_ref[:, OFF_MA:OFF_MA + D_MODEL])

    keep = POOL_BUF - DEC_SEQ
    for j in range(keep):
        p_out_ref[j] = p_in_ref[j + DEC_SEQ]
    for tt in range(DEC_SEQ):
        for g in range(len(POOL_WINDOWS)):
            p_out_ref[keep + tt, :, g * G_B:(g + 1) * G_B] = (
                u_ref[g, pl.ds(tt, n_seq, stride=DEC_SEQ), :])


def _decode_state_kernel(a_ref, bv_ref, qd_ref, s_in_ref, oi_ref, ga_ref, ma_ref, gb_ref, x_ref,
                         hg_ref, wpa_ref, wout_ref, fg_ref,
                         s_out_ref, y_ref, xs_ref, oa_ref):
    bb = BB_SAMPLE
    n_pairs = bb // 2
    even = (lax.broadcasted_iota(jnp.int32, (PAIR_ROWS, DV), 0)) < DEC_SEQ
    for i in range(bb):
        xs_ref[i * DEC_SEQ:(i + 1) * DEC_SEQ, :] = x_ref[i]
    for hh in range(N_HEADS):
        for lp in range(n_pairs):
            rs = slice(lp * PAIR_ROWS, (lp + 1) * PAIR_ROWS)
            for e in range(2):
                upd = lax.dot_general(a_ref[hh, e, rs, :], bv_ref[hh, e, rs, :],
                                      (((0,), (0,)), ((), ())), preferred_element_type=F32)
                s_out_ref[2 * lp + e, hh] = (upd[:, DV:2 * DV] * s_in_ref[2 * lp + e, hh]
                                             + upd[:, 0:DV])
    for hh in range(N_HEADS):
        cs = slice(hh * DK, (hh + 1) * DK)
        o_inter = []
        for lp in range(n_pairs):
            q8 = qd_ref[lp * PAIR_ROWS:(lp + 1) * PAIR_ROWS, cs]
            o0 = jnp.dot(q8, s_in_ref[2 * lp, hh].astype(BF16), preferred_element_type=F32)
            o1 = jnp.dot(q8, s_in_ref[2 * lp + 1, hh].astype(BF16), preferred_element_type=F32)
            o_inter.append(jnp.where(even, o0, o1))
        o = oi_ref[:, cs] + jnp.concatenate(o_inter, axis=0)
        oa_ref[:, cs] = (_rms(o, hg_ref[...]) * ga_ref[:, cs]).astype(BF16)
    y_a = jnp.dot(oa_ref[...], wpa_ref[...], preferred_element_type=F32)
    merged = ma_ref[...] * y_a + gb_ref[...]
    out = xs_ref[...] + jnp.dot(merged.astype(BF16), wout_ref[...], preferred_element_type=F32)
    y = _rms(out, fg_ref[...])
    for i in range(bb):
        y_ref[i] = y[i * DEC_SEQ:(i + 1) * DEC_SEQ, :]


def _const_spec(shape):
    n = len(shape)
    return pl.BlockSpec(shape, lambda *_: (0,) * n, pipeline_mode=pl.Buffered(1))


def _weight_specs():
    return [
        _const_spec((1, D_MODEL)),
        _const_spec((D_MODEL, D_IN)),
        _const_spec((2, W_BRANCH)),
        _const_spec((1, DV)),
        _const_spec((W_BRANCH, D_MODEL)),
        _const_spec((len(POOL_WINDOWS), G_B, G_B)),
        _const_spec((1, W_BRANCH)),
        _const_spec((W_BRANCH, D_MODEL)),
        _const_spec((D_MODEL, D_MODEL)),
        _const_spec((1, D_MODEL)),
    ]


def kernel(x_prompt, x_sample, state_hgrn, state_pool, norm_g, w_in, lb_logits, hgrn_norm_g,
           w_proj_a, w_pool, pool_scale, w_proj_b, w_out, final_norm_g):
    batch, seq, _ = x_prompt.shape
    dec_batch, dec_seq, _ = x_sample.shape
    assert norm_g.shape[0] == 1 and lb_logits.shape[0] == 2, "single-layer decoder only"
    assert seq % TM_PROMPT == 0 and TM_PROMPT % SUB_TILE == 0 and dec_batch % BB_SAMPLE == 0
    assert dec_seq == DEC_SEQ and PAIR_ROWS == SUBLANES and BB_SAMPLE % 2 == 0
    assert dec_batch % PREP_SEQS == 0 and (PREP_SEQS * DEC_SEQ) % SUB_TILE == 0
    assert PAST_LEN >= max(POOL_WINDOWS)

    weights = (norm_g, w_in[0].astype(BF16), lb_logits, hgrn_norm_g,
               w_proj_a[0].astype(BF16), w_pool[0].astype(BF16), pool_scale,
               w_proj_b[0].astype(BF16), w_out[0].astype(BF16), final_norm_g.reshape(1, D_MODEL))

    tm = TM_PROMPT
    nt = seq // tm
    y_p, s_p, p_p = pl.pallas_call(
        _prompt_kernel,
        grid=(batch, nt),
        in_specs=[pl.BlockSpec((1, tm, D_MODEL), lambda b, t: (b, t, 0))] + _weight_specs(),
        out_specs=[
            pl.BlockSpec((1, tm, D_MODEL), lambda b, t: (b, t, 0)),
            pl.BlockSpec((1, N_HEADS, DK, DV), lambda b, t: (b, 0, 0, 0)),
            pl.BlockSpec((1, POOL_BUF, W_BRANCH), lambda b, t: (b, 0, 0)),
        ],
        out_shape=[
            jax.ShapeDtypeStruct((batch, seq, D_MODEL), F32),
            jax.ShapeDtypeStruct((batch, N_HEADS, DK, DV), F32),
            jax.ShapeDtypeStruct((batch, POOL_BUF, W_BRANCH), F32),
        ],
        scratch_shapes=[
            pltpu.VMEM((tm, D_IN), F32),
            pltpu.VMEM((N_HEADS, DV, DK), F32),
            pltpu.VMEM((tm + 2 * SUBLANES, W_BRANCH), F32),
            pltpu.VMEM((N_HEADS, tm, (SUB_TILE // (2 * CHUNK)) * DK), BF16),
            pltpu.VMEM((N_HEADS, tm // (2 * CHUNK), DK, DV), BF16),
            pltpu.VMEM((tm, W_BRANCH), BF16),
            pltpu.VMEM((tm, W_BRANCH), BF16),
            pltpu.VMEM((tm, W_BRANCH), BF16),
            pltpu.VMEM((tm, W_BRANCH), BF16),
            pltpu.VMEM((tm, W_BRANCH), BF16),
            pltpu.VMEM((tm, W_BRANCH), BF16),
            pltpu.VMEM((tm, W_BRANCH), BF16),
            pltpu.VMEM((tm, W_BRANCH), BF16),
        ],
        compiler_params=pltpu.CompilerParams(
            dimension_semantics=("arbitrary", "arbitrary"),
            vmem_limit_bytes=VMEM_LIMIT_BYTES),
        name="hgrn2_pool_prompt",
    )(x_prompt, *weights)

    rows = dec_batch * DEC_SEQ
    pool_in = jnp.transpose(state_pool[0], (1, 0, 2))
    ng, win_b, lbl, hg, wpa_b, wpool_b, ps, wpb_b, wout_b, fg = weights
    prep_shapes = [
        jax.ShapeDtypeStruct((N_HEADS, 2, rows, DK), BF16),
        jax.ShapeDtypeStruct((N_HEADS, 2, rows, 2 * DV), BF16),
        jax.ShapeDtypeStruct((rows, W_BRANCH), BF16),
        jax.ShapeDtypeStruct((rows, W_BRANCH), F32),
        jax.ShapeDtypeStruct((rows, W_BRANCH), F32),
        jax.ShapeDtypeStruct((rows, D_MODEL), F32),
        jax.ShapeDtypeStruct((rows, D_MODEL), F32),
        jax.ShapeDtypeStruct((POOL_BUF, dec_batch, W_BRANCH), F32),
    ]
    pseq = PREP_SEQS
    prows = pseq * DEC_SEQ
    a_t, bv_t, qd_t, oi_t, ga_t, ma_t, gb_t, pool_out = pl.pallas_call(
        functools.partial(_decode_prep_kernel, n_seq=pseq),
        grid=(dec_batch // pseq,),
        in_specs=[pl.BlockSpec((pseq, DEC_SEQ, D_MODEL), lambda i: (i, 0, 0)),
                  pl.BlockSpec((POOL_BUF, pseq, W_BRANCH), lambda i: (0, i, 0)),
                  _const_spec(ng.shape), _const_spec(win_b.shape), _const_spec(lbl.shape),
                  _const_spec(wpool_b.shape), _const_spec(ps.shape), _const_spec(wpb_b.shape)],
        out_specs=[
            pl.BlockSpec((N_HEADS, 2, prows, DK), lambda i: (0, 0, i, 0)),
            pl.BlockSpec((N_HEADS, 2, prows, 2 * DV), lambda i: (0, 0, i, 0)),
            pl.BlockSpec((prows, W_BRANCH), lambda i: (i, 0)),
            pl.BlockSpec((prows, W_BRANCH), lambda i: (i, 0)),
            pl.BlockSpec((prows, W_BRANCH), lambda i: (i, 0)),
            pl.BlockSpec((prows, D_MODEL), lambda i: (i, 0)),
            pl.BlockSpec((prows, D_MODEL), lambda i: (i, 0)),
            pl.BlockSpec((POOL_BUF, pseq, W_BRANCH), lambda i: (0, i, 0)),
        ],
        out_shape=prep_shapes,
        scratch_shapes=[
            pltpu.VMEM((prows, D_MODEL), F32),
            pltpu.VMEM((prows, D_IN), F32),
            pltpu.VMEM((len(POOL_WINDOWS), prows, G_B), F32),
            pltpu.VMEM((len(POOL_WINDOWS), prows, G_B), F32),
            pltpu.VMEM((prows, W_BRANCH), BF16),
        ],
        compiler_params=pltpu.CompilerParams(
            dimension_semantics=("arbitrary",), vmem_limit_bytes=VMEM_LIMIT_BYTES),
        name="hgrn2_pool_decode_prep",
    )(x_sample, pool_in, ng, win_b, lbl, wpool_b, ps, wpb_b)

    bb = BB_SAMPLE
    brows = bb * DEC_SEQ
    s_s, y_s = pl.pallas_call(
        _decode_state_kernel,
        grid=(dec_batch // bb,),
        in_specs=[
            pl.BlockSpec((N_HEADS, 2, brows, DK), lambda i: (0, 0, i, 0)),
            pl.BlockSpec((N_HEADS, 2, brows, 2 * DV), lambda i: (0, 0, i, 0)),
            pl.BlockSpec((brows, W_BRANCH), lambda i: (i, 0)),
            pl.BlockSpec((bb, N_HEADS, DK, DV), lambda i: (i, 0, 0, 0)),
            pl.BlockSpec((brows, W_BRANCH), lambda i: (i, 0)),
            pl.BlockSpec((brows, W_BRANCH), lambda i: (i, 0)),
            pl.BlockSpec((brows, D_MODEL), lambda i: (i, 0)),
            pl.BlockSpec((brows, D_MODEL), lambda i: (i, 0)),
            pl.BlockSpec((bb, DEC_SEQ, D_MODEL), lambda i: (i, 0, 0)),
            _const_spec(hg.shape), _const_spec(wpa_b.shape), _const_spec(wout_b.shape),
            _const_spec(fg.shape),
        ],
        out_specs=[
            pl.BlockSpec((bb, N_HEADS, DK, DV), lambda i: (i, 0, 0, 0)),
            pl.BlockSpec((bb, DEC_SEQ, D_MODEL), lambda i: (i, 0, 0)),
        ],
        out_shape=[
            jax.ShapeDtypeStruct((dec_batch, N_HEADS, DK, DV), F32),
            jax.ShapeDtypeStruct(x_sample.shape, F32),
        ],
        scratch_shapes=[
            pltpu.VMEM((brows, D_MODEL), F32),
            pltpu.VMEM((brows, W_BRANCH), BF16),
        ],
        compiler_params=pltpu.CompilerParams(
            dimension_semantics=("arbitrary",), vmem_limit_bytes=VMEM_LIMIT_BYTES),
        name="hgrn2_pool_decode_state",
    )(a_t, bv_t, qd_t, state_hgrn[0], oi_t, ga_t, ma_t, gb_t, x_sample, hg, wpa_b, wout_b, fg)

    p_s = jnp.transpose(pool_out, (1, 0, 2))
    return (y_p, y_s, s_p[None], p_p[None], s_s[None], p_s[None])
```

```python
import functools

import jax
import jax.numpy as jnp
from jax import lax
from jax.experimental import pallas as pl
from jax.experimental.pallas import tpu as pltpu

F32 = jnp.float32
BF16 = jnp.bfloat16

D_MODEL = 1024
W_BRANCH = 512
N_HEADS = 4
DK = 128
DV = 128
CHUNK = 32
POOL_WINDOWS = (2, 4, 8, 16)
G_B = 128
POOL_BUF = 15
PAST_LEN = 16384
EPS = 1e-6
D_IN = 4 * W_BRANCH + 2 * W_BRANCH + 2 * D_MODEL
OFF_Q, OFF_F, OFF_I, OFF_GA = 0, 512, 1024, 1536
OFF_U, OFF_GB, OFF_MA, OFF_MB = 2048, 2560, 3072, 4096

SUBLANES = 8
VMEM_LIMIT_BYTES = 56 * 1024 * 1024

TM_PROMPT = 512
SUB_TILE = 256
CAST_STEPS = 8
PREP_SEQS = 64
BB_SAMPLE = 16
DEC_SEQ = 4
PAIR_ROWS = 2 * DEC_SEQ


def _rms(x, g):
    ms = jnp.mean(x * x, axis=-1, keepdims=True)
    return x * lax.rsqrt(ms + EPS) * g


def _lower_bound(lbl_ref):
    l0 = lbl_ref[0:1, :]
    l1 = lbl_ref[1:2, :]
    m = jnp.maximum(l0, l1)
    e0 = jnp.exp(l0 - m)
    e1 = jnp.exp(l1 - m)
    return e0 / (e0 + e1)


def _in_proj(x, ng_ref, win_ref, z_ref):
    h = _rms(x, ng_ref[...]).astype(BF16)
    z_ref[...] = jnp.dot(h, win_ref[...], preferred_element_type=F32)


def _hgrn_inputs(z_ref, lb, hh):
    c = hh * DK
    q = z_ref[:, OFF_Q + c:OFF_Q + c + DK]
    f = z_ref[:, OFF_F + c:OFF_F + c + DK]
    v = z_ref[:, OFF_I + c:OFF_I + c + DK]
    lbh = lb[:, c:c + DK]
    fg = lbh + (1.0 - lbh) * jax.nn.sigmoid(f)
    logf = jnp.log(fg)
    k = 1.0 - fg
    qf = jax.nn.silu(q) * (DK ** -0.5)
    return qf, k, v, logf


def _group_cumsum(x, group):
    row = lax.broadcasted_iota(jnp.int32, x.shape, 0) & (group - 1)
    s = 1
    while s < group:
        x = x + jnp.where(row >= s, pltpu.roll(x, s, 0), 0.0)
        s *= 2
    return x


def _hgrn_epilogue(o, z_ref, hg_ref, hh, oa_ref):
    c = hh * DK
    ga = z_ref[:, OFF_GA + c:OFF_GA + c + DK]
    on = _rms(o, hg_ref[...]) * jax.nn.silu(ga)
    oa_ref[:, c:c + DK] = on.astype(BF16)


def _pool_epilogue(pooled, z_ref, g, wpool_ref, ps_ref, yb_ref):
    c = g * G_B
    mixed = jnp.dot(pooled.astype(BF16), wpool_ref[g], preferred_element_type=F32)
    gb = z_ref[:, OFF_GB + c:OFF_GB + c + G_B]
    yb = mixed * ps_ref[:, c:c + G_B] * jax.nn.silu(gb)
    yb_ref[:, c:c + G_B] = yb.astype(BF16)


def _prompt_kernel(x_ref, ng_ref, win_ref, lbl_ref, hg_ref, wpa_ref, wpool_ref, ps_ref,
                   wpb_ref, wout_ref, fg_ref,
                   y_ref, s_out_ref, p_out_ref,
                   z_ref, st_ref, ext_ref, kx_ref, sn_ref, qs_ref, ks_ref, qd_ref, vb_ref,
                   oa_ref, yb_ref):
    tm = TM_PROMPT
    sub = SUB_TILE
    n_sub = tm // sub
    nc = tm // CHUNK
    ncs = sub // CHUNK
    t = pl.program_id(1)
    first = t == 0
    hdr = 2 * SUBLANES

    @pl.when((pl.program_id(0) == 0) & first)
    def _():
        st_ref[...] = jnp.zeros_like(st_ref)
        ext_ref[...] = jnp.zeros_like(ext_ref)
        kx_ref[...] = jnp.zeros_like(kx_ref)

    x = x_ref[0]
    _in_proj(x, ng_ref, win_ref, z_ref)
    lb = _lower_bound(lbl_ref)

    ri = lax.broadcasted_iota(jnp.int32, (sub, sub), 0)
    ci = lax.broadcasted_iota(jnp.int32, (sub, sub), 1)
    causal = ((ri // CHUNK) == (ci // CHUNK)) & (ci <= ri)

    heads = range(N_HEADS)
    units = [(hh, s) for hh in heads for s in range(n_sub)]

    def rows(s):
        return slice(s * sub, (s + 1) * sub)

    def cols(hh):
        return slice(hh * DK, (hh + 1) * DK)

    dec = []
    for hh in heads:
        qf, k, v, logf = _hgrn_inputs(z_ref, lb, hh)
        b = _group_cumsum(logf, CHUNK)
        b3 = b.reshape(nc, CHUNK, DK)
        ref = b3[:, CHUNK // 2:CHUNK // 2 + 1, :]
        bl = b3[:, CHUNK - 1:CHUNK, :]
        q3 = qf.reshape(nc, CHUNK, DK)
        k3 = k.reshape(nc, CHUNK, DK)
        qs_ref[:, cols(hh)] = (q3 * jnp.exp(b3 - ref)).reshape(tm, DK).astype(BF16)
        ks_ref[:, cols(hh)] = (k3 * jnp.exp(ref - b3)).reshape(tm, DK).astype(BF16)
        qd_ref[:, cols(hh)] = (q3 * jnp.exp(b3)).reshape(tm, DK).astype(BF16)
        vb_ref[:, cols(hh)] = v.astype(BF16)
        kd = (k3 * jnp.exp(bl - b3)).reshape(tm, DK).astype(BF16)
        dec.append(jnp.exp(bl))
        for c in range(nc):
            cc = c % ncs
            kx_ref[hh, c * CHUNK:(c + 1) * CHUNK, cc * DK:(cc + 1) * DK] = (
                kd[c * CHUNK:(c + 1) * CHUNK, :])

    ext_ref[0:hdr, :] = jnp.where(first, 0.0, ext_ref[tm:tm + hdr, :])
    ext_ref[hdr:hdr + tm, :] = z_ref[:, OFF_U:OFF_U + W_BRANCH]
    pos1 = t * tm + lax.broadcasted_iota(jnp.int32, (tm, 1), 0) + 1
    for g, w in enumerate(POOL_WINDOWS):
        c = g * G_B
        s = ext_ref[:, c:c + G_B]
        sh = 1
        while sh < w:
            s = s + pltpu.roll(s, sh, 0)
            sh *= 2
        inv = jnp.where(pos1 >= w, 1.0 / w, 1.0 / pos1.astype(F32))
        pooled = s[hdr:, :] * inv - z_ref[:, OFF_U + c:OFF_U + c + G_B]
        _pool_epilogue(pooled, z_ref, g, wpool_ref, ps_ref, yb_ref)
    p_out_ref[0] = ext_ref[hdr + tm - POOL_BUF:hdr + tm, :]

    sc = {(hh, s): lax.dot_general(qs_ref[rows(s), cols(hh)], ks_ref[rows(s), cols(hh)],
                                   (((1,), (1,)), ((), ())), preferred_element_type=F32)
          for hh, s in units}
    ut = {(hh, s): lax.dot_general(vb_ref[rows(s), cols(hh)], kx_ref[hh, rows(s), :],
                                   (((0,), (0,)), ((), ())), preferred_element_type=F32)
          for hh, s in units}
    o = {(hh, s): jnp.dot(jnp.where(causal, sc[hh, s], 0.0).astype(BF16),
                          vb_ref[rows(s), cols(hh)], preferred_element_type=F32)
         for hh, s in units}
    for hh in heads:
        st = jnp.where(first, 0.0, st_ref[hh])
        for c in range(nc):
            sn_ref[hh, c] = st.T.astype(BF16)
            st = dec[hh][c] * st + ut[hh, c // ncs][:, (c % ncs) * DK:(c % ncs + 1) * DK]
        st_ref[hh] = st
        s_out_ref[0, hh] = st.T
    wcol = D_MODEL // N_HEADS
    y_b = []
    for hh in heads:
        o_inter = [jnp.dot(qd_ref[c * CHUNK:(c + 1) * CHUNK, cols(hh)], sn_ref[hh, c],
                           preferred_element_type=F32) for c in range(nc)]
        y_b.append(jnp.dot(yb_ref[...], wpb_ref[:, hh * wcol:(hh + 1) * wcol],
                           preferred_element_type=F32))
        o_hh = jnp.concatenate([o[hh, s] for s in range(n_sub)], axis=0)
        _hgrn_epilogue(o_hh + jnp.concatenate(o_inter, axis=0), z_ref, hg_ref, hh, oa_ref)
    gated_b = jax.nn.sigmoid(z_ref[:, OFF_MB:OFF_MB + D_MODEL]) * jnp.concatenate(y_b, axis=1)

    y_a = jnp.dot(oa_ref[...], wpa_ref[...], preferred_element_type=F32)
    merged = jax.nn.sigmoid(z_ref[:, OFF_MA:OFF_MA + D_MODEL]) * y_a + gated_b
    out = x + jnp.dot(merged.astype(BF16), wout_ref[...], preferred_element_type=F32)
    y_ref[0] = _rms(out, fg_ref[...])


def _shift_rows(x, k):
    n = x.shape[0]
    return x if k % n == 0 else pltpu.roll(x, (-k) % n, 0)


def _group_bcast(x, j, group):
    t = lax.broadcasted_iota(jnp.int32, x.shape, 0) & (group - 1)
    out = _shift_rows(x, j - (group - 1))
    for tt in range(group - 2, -1, -1):
        out = jnp.where(t == tt, _shift_rows(x, j - tt), out)
    return out


def _decode_prep_kernel(x_ref, p_in_ref, ng_ref, win_ref, lbl_ref, wpool_ref, ps_ref, wpb_ref,
                        a_ref, bv_ref, qd_ref, oi_ref, ga_ref, ma_ref, gb_ref, p_out_ref,
                        xs_ref, z_ref, rw_ref, u_ref, yb_ref, *, n_seq):
    rows = n_seq * DEC_SEQ
    for i in range(n_seq):
        xs_ref[i * DEC_SEQ:(i + 1) * DEC_SEQ, :] = x_ref[i]
    _in_proj(xs_ref[...], ng_ref, win_ref, z_ref)
    lb = _lower_bound(lbl_ref)

    r = lax.broadcasted_iota(jnp.int32, (rows, DK), 0)
    t = r & (DEC_SEQ - 1)
    r8 = r & (PAIR_ROWS - 1)
    even = r8 < DEC_SEQ
    ones_even = jnp.where((r8 >= DEC_SEQ) & (r8 < DEC_SEQ + 3), 1.0, 0.0).astype(BF16)
    ones_odd = jnp.where(r8 < 3, 1.0, 0.0).astype(BF16)
    blk = SUB_TILE
    ri = lax.broadcasted_iota(jnp.int32, (blk, blk), 0)
    ci = lax.broadcasted_iota(jnp.int32, (blk, blk), 1)
    causal = ((ri // DEC_SEQ) == (ci // DEC_SEQ)) & (ci <= ri)

    for hh in range(N_HEADS):
        cs = slice(hh * DK, (hh + 1) * DK)
        qf, k, v, logf = _hgrn_inputs(z_ref, lb, hh)
        b = _group_cumsum(logf, DEC_SEQ)
        ref = _group_bcast(b, DEC_SEQ // 2, DEC_SEQ)
        bl = _group_bcast(b, DEC_SEQ - 1, DEC_SEQ)
        qs = (qf * jnp.exp(b - ref)).astype(BF16)
        ks = (k * jnp.exp(ref - b)).astype(BF16)
        kd = k * jnp.exp(bl - b)
        vb = v.astype(BF16)
        qd_ref[:, cs] = (qf * jnp.exp(b)).astype(BF16)
        for s in range(rows // blk):
            rs = slice(s * blk, (s + 1) * blk)
            sc = lax.dot_general(qs[rs], ks[rs], (((1,), (1,)), ((), ())),
                                 preferred_element_type=F32)
            oi_ref[rs, cs] = jnp.dot(jnp.where(causal, sc, 0.0).astype(BF16), vb[rs],
                                     preferred_element_type=F32)
        dec = jnp.exp(bl)
        d1 = dec.astype(BF16).astype(F32)
        d2 = (dec - d1).astype(BF16).astype(F32)
        d3 = (dec - d1 - d2).astype(BF16).astype(F32)
        tail = jnp.where(t == 0, d1, jnp.where(t == 1, d2, jnp.where(t == 2, d3, 0.0)))
        a_ref[hh, 0] = jnp.where(even, kd, _shift_rows(tail, -DEC_SEQ)).astype(BF16)
        a_ref[hh, 1] = jnp.where(even, _shift_rows(tail, DEC_SEQ), kd).astype(BF16)
        zero = jnp.zeros_like(vb)
        bv_ref[hh, 0, :, 0:DV] = jnp.where(even, vb, zero)
        bv_ref[hh, 0, :, DV:2 * DV] = ones_even
        bv_ref[hh, 1, :, 0:DV] = jnp.where(even, zero, vb)
        bv_ref[hh, 1, :, DV:2 * DV] = ones_odd
        ga_ref[:, cs] = jax.nn.silu(z_ref[:, OFF_GA + hh * DK:OFF_GA + (hh + 1) * DK])

    for g, w in enumerate(POOL_WINDOWS):
        gc = slice(g * G_B, (g + 1) * G_B)
        acc = None
        suffix = {}
        for j in range(POOL_BUF - 1, -1, -1):
            e = p_in_ref[j, :, gc]
            acc = e if acc is None else acc + e
            suffix[j] = acc
        for tt in range(DEC_SEQ):
            j = POOL_BUF + 1 - w + tt
            rw_ref[g, pl.ds(tt, n_seq, stride=DEC_SEQ), :] = (
                suffix[j] if j < POOL_BUF else jnp.zeros((n_seq, G_B), F32))
        u = z_ref[:, OFF_U + g * G_B:OFF_U + (g + 1) * G_B]
        u_ref[g] = u
        cu = u
        sh = 1
        while sh < min(w, DEC_SEQ):
            cu = cu + jnp.where(t >= sh, _shift_rows(cu, -sh), 0.0)
            sh *= 2
        pooled = (rw_ref[g] + cu) * (1.0 / w) - u
        _pool_epilogue(pooled, z_ref, g, wpool_ref, ps_ref, yb_ref)
    gb_ref[...] = jax.nn.sigmoid(z_ref[:, OFF_MB:OFF_MB + D_MODEL]) * jnp.dot(
        yb_ref[...], wpb_ref[...], preferred_element_type=F32)
    ma_ref[...] = jax.nn.sigmoid(z_ref[:, OFF_MA:OFF_MA + D_MODEL])

    keep = POOL_BUF - DEC_SEQ
    for j in range(keep):
        p_out_ref[j] = p_in_ref[j + DEC_SEQ]
    for tt in range(DEC_SEQ):
        for g in range(len(POOL_WINDOWS)):
            p_out_ref[keep + tt, :, g * G_B:(g + 1) * G_B] = (
                u_ref[g, pl.ds(tt, n_seq, stride=DEC_SEQ), :])


def _decode_state_kernel(a_ref, bv_ref, qd_ref, s_in_ref, oi_ref, ga_ref, ma_ref, gb_ref, x_ref,
                         hg_ref, wpa_ref, wout_ref, fg_ref,
                         s_out_ref, y_ref, xs_ref, oa_ref):
    bb = BB_SAMPLE
    n_pairs = bb // 2
    even = (lax.broadcasted_iota(jnp.int32, (PAIR_ROWS, DV), 0)) < DEC_SEQ
    for i in range(bb):
        xs_ref[i * DEC_SEQ:(i + 1) * DEC_SEQ, :] = x_ref[i]
    for hh in range(N_HEADS):
        for lp in range(n_pairs):
            rs = slice(lp * PAIR_ROWS, (lp + 1) * PAIR_ROWS)
            for e in range(2):
                upd = lax.dot_general(a_ref[hh, e, rs, :], bv_ref[hh, e, rs, :],
                                      (((0,), (0,)), ((), ())), preferred_element_type=F32)
                s_out_ref[2 * lp + e, hh] = (upd[:, DV:2 * DV] * s_in_ref[2 * lp + e, hh]
                                             + upd[:, 0:DV])
    for hh in range(N_HEADS):
        cs = slice(hh * DK, (hh + 1) * DK)
        o_inter = []
        for lp in range(n_pairs):
            q8 = qd_ref[lp * PAIR_ROWS:(lp + 1) * PAIR_ROWS, cs]
            o0 = jnp.dot(q8, s_in_ref[2 * lp, hh].astype(BF16), preferred_element_type=F32)
            o1 = jnp.dot(q8, s_in_ref[2 * lp + 1, hh].astype(BF16), preferred_element_type=F32)
            o_inter.append(jnp.where(even, o0, o1))
        o = oi_ref[:, cs] + jnp.concatenate(o_inter, axis=0)
        oa_ref[:, cs] = (_rms(o, hg_ref[...]) * ga_ref[:, cs]).astype(BF16)
    y_a = jnp.dot(oa_ref[...], wpa_ref[...], preferred_element_type=F32)
    merged = ma_ref[...] * y_a + gb_ref[...]
    out = xs_ref[...] + jnp.dot(merged.astype(BF16), wout_ref[...], preferred_element_type=F32)
    y = _rms(out, fg_ref[...])
    for i in range(bb):
        y_ref[i] = y[i * DEC_SEQ:(i + 1) * DEC_SEQ, :]


def _cast_kernel(*refs):
    n = len(refs) // 2
    for src_ref, dst_ref in zip(refs[:n], refs[n:]):
        dst_ref[...] = src_ref[...].astype(BF16)


def _to_bf16(mats):
    specs = [pl.BlockSpec((m.shape[0] // CAST_STEPS, m.shape[1]), lambda i: (i, 0)) for m in mats]
    return pl.pallas_call(
        _cast_kernel,
        grid=(CAST_STEPS,),
        in_specs=specs,
        out_specs=specs,
        out_shape=[jax.ShapeDtypeStruct(m.shape, BF16) for m in mats],
        compiler_params=pltpu.CompilerParams(
            dimension_semantics=("arbitrary",), vmem_limit_bytes=VMEM_LIMIT_BYTES),
        name="hgrn2_pool_cast_weights",
    )(*mats)


def _const_spec(shape):
    n = len(shape)
    return pl.BlockSpec(shape, lambda *_: (0,) * n, pipeline_mode=pl.Buffered(1))


def _weight_specs():
    return [
        _const_spec((1, D_MODEL)),
        _const_spec((D_MODEL, D_IN)),
        _const_spec((2, W_BRANCH)),
        _const_spec((1, DV)),
        _const_spec((W_BRANCH, D_MODEL)),
        _const_spec((len(POOL_WINDOWS), G_B, G_B)),
        _const_spec((1, W_BRANCH)),
        _const_spec((W_BRANCH, D_MODEL)),
        _const_spec((D_MODEL, D_MODEL)),
        _const_spec((1, D_MODEL)),
    ]


def kernel(x_prompt, x_sample, state_hgrn, state_pool, norm_g, w_in, lb_logits, hgrn_norm_g,
           w_proj_a, w_pool, pool_scale, w_proj_b, w_out, final_norm_g):
    batch, seq, _ = x_prompt.shape
    dec_batch, dec_seq, _ = x_sample.shape
    assert norm_g.shape[0] == 1 and lb_logits.shape[0] == 2, "single-layer decoder only"
    assert seq % TM_PROMPT == 0 and TM_PROMPT % SUB_TILE == 0 and dec_batch % BB_SAMPLE == 0
    assert dec_seq == DEC_SEQ and PAIR_ROWS == SUBLANES and BB_SAMPLE % 2 == 0
    assert dec_batch % PREP_SEQS == 0 and (PREP_SEQS * DEC_SEQ) % SUB_TILE == 0
    assert PAST_LEN >= max(POOL_WINDOWS)

    n_groups = len(POOL_WINDOWS)
    win_b, wpa_b, wpool_b, wpb_b, wout_b = _to_bf16(
        [w_in[0], w_proj_a[0], w_pool[0].reshape(n_groups * G_B, G_B), w_proj_b[0], w_out[0]])
    weights = (norm_g, win_b, lb_logits, hgrn_norm_g, wpa_b, wpool_b.reshape(n_groups, G_B, G_B),
               pool_scale, wpb_b, wout_b, final_norm_g.reshape(1, D_MODEL))

    tm = TM_PROMPT
    nt = seq // tm
    y_p, s_p, p_p = pl.pallas_call(
        _prompt_kernel,
        grid=(batch, nt),
        in_specs=[pl.BlockSpec((1, tm, D_MODEL), lambda b, t: (b, t, 0))] + _weight_specs(),
        out_specs=[
            pl.BlockSpec((1, tm, D_MODEL), lambda b, t: (b, t, 0)),
            pl.BlockSpec((1, N_HEADS, DK, DV), lambda b, t: (b, 0, 0, 0)),
            pl.BlockSpec((1, POOL_BUF, W_BRANCH), lambda b, t: (b, 0, 0)),
        ],
        out_shape=[
            jax.ShapeDtypeStruct((batch, seq, D_MODEL), F32),
            jax.ShapeDtypeStruct((batch, N_HEADS, DK, DV), F32),
            jax.ShapeDtypeStruct((batch, POOL_BUF, W_BRANCH), F32),
        ],
        scratch_shapes=[
            pltpu.VMEM((tm, D_IN), F32),
            pltpu.VMEM((N_HEADS, DV, DK), F32),
            pltpu.VMEM((tm + 2 * SUBLANES, W_BRANCH), F32),
            pltpu.VMEM((N_HEADS, tm, (SUB_TILE // CHUNK) * DK), BF16),
            pltpu.VMEM((N_HEADS, tm // CHUNK, DK, DV), BF16),
            pltpu.VMEM((tm, W_BRANCH), BF16),
            pltpu.VMEM((tm, W_BRANCH), BF16),
            pltpu.VMEM((tm, W_BRANCH), BF16),
            pltpu.VMEM((tm, W_BRANCH), BF16),
            pltpu.VMEM((tm, W_BRANCH), BF16),
            pltpu.VMEM((tm, W_BRANCH), BF16),
        ],
        compiler_params=pltpu.CompilerParams(
            dimension_semantics=("arbitrary", "arbitrary"),
            vmem_limit_bytes=VMEM_LIMIT_BYTES),
        name="hgrn2_pool_prompt",
    )(x_prompt, *weights)

    rows = dec_batch * DEC_SEQ
    pool_in = jnp.transpose(state_pool[0], (1, 0, 2))
    ng, win_b, lbl, hg, wpa_b, wpool_b, ps, wpb_b, wout_b, fg = weights
    prep_shapes = [
        jax.ShapeDtypeStruct((N_HEADS, 2, rows, DK), BF16),
        jax.ShapeDtypeStruct((N_HEADS, 2, rows, 2 * DV), BF16),
        jax.ShapeDtypeStruct((rows, W_BRANCH), BF16),
        jax.ShapeDtypeStruct((rows, W_BRANCH), F32),
        jax.ShapeDtypeStruct((rows, W_BRANCH), F32),
        jax.ShapeDtypeStruct((rows, D_MODEL), F32),
        jax.ShapeDtypeStruct((rows, D_MODEL), F32),
        jax.ShapeDtypeStruct((POOL_BUF, dec_batch, W_BRANCH), F32),
    ]
    pseq = PREP_SEQS
    prows = pseq * DEC_SEQ
    a_t, bv_t, qd_t, oi_t, ga_t, ma_t, gb_t, pool_out = pl.pallas_call(
        functools.partial(_decode_prep_kernel, n_seq=pseq),
        grid=(dec_batch // pseq,),
        in_specs=[pl.BlockSpec((pseq, DEC_SEQ, D_MODEL), lambda i: (i, 0, 0)),
                  pl.BlockSpec((POOL_BUF, pseq, W_BRANCH), lambda i: (0, i, 0)),
                  _const_spec(ng.shape), _const_spec(win_b.shape), _const_spec(lbl.shape),
                  _const_spec(wpool_b.shape), _const_spec(ps.shape), _const_spec(wpb_b.shape)],
        out_specs=[
            pl.BlockSpec((N_HEADS, 2, prows, DK), lambda i: (0, 0, i, 0)),
            pl.BlockSpec((N_HEADS, 2, prows, 2 * DV), lambda i: (0, 0, i, 0)),
            pl.BlockSpec((prows, W_BRANCH), lambda i: (i, 0)),
            pl.BlockSpec((prows, W_BRANCH), lambda i: (i, 0)),
            pl.BlockSpec((prows, W_BRANCH), lambda i: (i, 0)),
            pl.BlockSpec((prows, D_MODEL), lambda i: (i, 0)),
            pl.BlockSpec((prows, D_MODEL), lambda i: (i, 0)),
            pl.BlockSpec((POOL_BUF, pseq, W_BRANCH), lambda i: (0, i, 0)),
        ],
        out_shape=prep_shapes,
        scratch_shapes=[
            pltpu.VMEM((prows, D_MODEL), F32),
            pltpu.VMEM((prows, D_IN), F32),
            pltpu.VMEM((len(POOL_WINDOWS), prows, G_B), F32),
            pltpu.VMEM((len(POOL_WINDOWS), prows, G_B), F32),
            pltpu.VMEM((prows, W_BRANCH), BF16),
        ],
        compiler_params=pltpu.CompilerParams(
            dimension_semantics=("arbitrary",), vmem_limit_bytes=VMEM_LIMIT_BYTES),
        name="hgrn2_pool_decode_prep",
    )(x_sample, pool_in, ng, win_b, lbl, wpool_b, ps, wpb_b)

    bb = BB_SAMPLE
    brows = bb * DEC_SEQ
    s_s, y_s = pl.pallas_call(
        _decode_state_kernel,
        grid=(dec_batch // bb,),
        in_specs=[
            pl.BlockSpec((N_HEADS, 2, brows, DK), lambda i: (0, 0, i, 0)),
            pl.BlockSpec((N_HEADS, 2, brows, 2 * DV), lambda i: (0, 0, i, 0)),
            pl.BlockSpec((brows, W_BRANCH), lambda i: (i, 0)),
            pl.BlockSpec((bb, N_HEADS, DK, DV), lambda i: (i, 0, 0, 0)),
            pl.BlockSpec((brows, W_BRANCH), lambda i: (i, 0)),
            pl.BlockSpec((brows, W_BRANCH), lambda i: (i, 0)),
            pl.BlockSpec((brows, D_MODEL), lambda i: (i, 0)),
            pl.BlockSpec((brows, D_MODEL), lambda i: (i, 0)),
            pl.BlockSpec((bb, DEC_SEQ, D_MODEL), lambda i: (i, 0, 0)),
            _const_spec(hg.shape), _const_spec(wpa_b.shape), _const_spec(wout_b.shape),
            _const_spec(fg.shape),
        ],
        out_specs=[
            pl.BlockSpec((bb, N_HEADS, DK, DV), lambda i: (i, 0, 0, 0)),
            pl.BlockSpec((bb, DEC_SEQ, D_MODEL), lambda i: (i, 0, 0)),
        ],
        out_shape=[
            jax.ShapeDtypeStruct((dec_batch, N_HEADS, DK, DV), F32),
            jax.ShapeDtypeStruct(x_sample.shape, F32),
        ],
        scratch_shapes=[
            pltpu.VMEM((brows, D_MODEL), F32),
            pltpu.VMEM((brows, W_BRANCH), BF16),
        ],
        compiler_params=pltpu.CompilerParams(
            dimension_semantics=("arbitrary",), vmem_limit_bytes=VMEM_LIMIT_BYTES),
        name="hgrn2_pool_decode_state",
    )(a_t, bv_t, qd_t, state_hgrn[0], oi_t, ga_t, ma_t, gb_t, x_sample, hg, wpa_b, wout_b, fg)

    p_s = jnp.transpose(pool_out, (1, 0, 2))
    return (y_p, y_s, s_p[None], p_p[None], s_s[None], p_s[None])
```

```python
import functools

import jax
import jax.numpy as jnp
from jax import lax
from jax.experimental import pallas as pl
from jax.experimental.pallas import tpu as pltpu

F32 = jnp.float32
BF16 = jnp.bfloat16

D_MODEL = 1024
W_BRANCH = 512
N_HEADS = 4
DK = 128
DV = 128
CHUNK = 32
POOL_WINDOWS = (2, 4, 8, 16)
G_B = 128
POOL_BUF = 15
PAST_LEN = 16384
EPS = 1e-6
D_IN = 4 * W_BRANCH + 2 * W_BRANCH + 2 * D_MODEL
OFF_Q, OFF_F, OFF_I, OFF_GA = 0, 512, 1024, 1536
OFF_U, OFF_GB, OFF_MA, OFF_MB = 2048, 2560, 3072, 4096

SUBLANES = 8
VMEM_LIMIT_BYTES = 56 * 1024 * 1024

TM_PROMPT = 512
SUB_TILE = 256
CAST_STEPS = 8
PREP_SEQS = 64
BB_SAMPLE = 16
DEC_SEQ = 4
PAIR_ROWS = 2 * DEC_SEQ


def _rms(x, g):
    ms = jnp.mean(x * x, axis=-1, keepdims=True)
    return x * lax.rsqrt(ms + EPS) * g


def _lower_bound(lbl_ref):
    l0 = lbl_ref[0:1, :]
    l1 = lbl_ref[1:2, :]
    m = jnp.maximum(l0, l1)
    e0 = jnp.exp(l0 - m)
    e1 = jnp.exp(l1 - m)
    return e0 / (e0 + e1)


def _in_proj(x, ng_ref, win_ref, z_ref):
    h = _rms(x, ng_ref[...])
    z_ref[...] = jnp.dot(h, win_ref[...], preferred_element_type=F32)


def _hgrn_inputs(z_ref, lb, hh):
    c = hh * DK
    q = z_ref[:, OFF_Q + c:OFF_Q + c + DK]
    f = z_ref[:, OFF_F + c:OFF_F + c + DK]
    v = z_ref[:, OFF_I + c:OFF_I + c + DK]
    lbh = lb[:, c:c + DK]
    fg = lbh + (1.0 - lbh) * jax.nn.sigmoid(f)
    logf = jnp.log(fg)
    k = 1.0 - fg
    qf = jax.nn.silu(q) * (DK ** -0.5)
    return qf, k, v, logf


def _group_cumsum(x, group):
    row = lax.broadcasted_iota(jnp.int32, x.shape, 0) & (group - 1)
    s = 1
    while s < group:
        x = x + jnp.where(row >= s, pltpu.roll(x, s, 0), 0.0)
        s *= 2
    return x


def _hgrn_epilogue(o, z_ref, hg_ref, hh, oa_ref):
    c = hh * DK
    ga = z_ref[:, OFF_GA + c:OFF_GA + c + DK]
    on = _rms(o, hg_ref[...]) * jax.nn.silu(ga)
    oa_ref[:, c:c + DK] = on.astype(BF16)


def _pool_epilogue(pooled, z_ref, g, wpool_ref, ps_ref, yb_ref):
    c = g * G_B
    mixed = jnp.dot(pooled.astype(BF16), wpool_ref[g], preferred_element_type=F32)
    gb = z_ref[:, OFF_GB + c:OFF_GB + c + G_B]
    yb = mixed * ps_ref[:, c:c + G_B] * jax.nn.silu(gb)
    yb_ref[:, c:c + G_B] = yb.astype(BF16)


def _prompt_kernel(x_ref, ng_ref, win_ref, lbl_ref, hg_ref, wpa_ref, wpool_ref, ps_ref,
                   wpb_ref, wout_ref, fg_ref,
                   y_ref, s_out_ref, p_out_ref,
                   z_ref, st_ref, ext_ref, kx_ref, sn_ref, qs_ref, ks_ref, qd_ref, vb_ref,
                   oa_ref, yb_ref):
    tm = TM_PROMPT
    sub = SUB_TILE
    n_sub = tm // sub
    nc = tm // CHUNK
    ncs = sub // CHUNK
    t = pl.program_id(1)
    first = t == 0
    hdr = 2 * SUBLANES

    @pl.when((pl.program_id(0) == 0) & first)
    def _():
        st_ref[...] = jnp.zeros_like(st_ref)
        ext_ref[...] = jnp.zeros_like(ext_ref)
        kx_ref[...] = jnp.zeros_like(kx_ref)

    x = x_ref[0]
    _in_proj(x, ng_ref, win_ref, z_ref)
    lb = _lower_bound(lbl_ref)

    ri = lax.broadcasted_iota(jnp.int32, (sub, sub), 0)
    ci = lax.broadcasted_iota(jnp.int32, (sub, sub), 1)
    causal = ((ri // CHUNK) == (ci // CHUNK)) & (ci <= ri)

    heads = range(N_HEADS)
    units = [(hh, s) for hh in heads for s in range(n_sub)]

    def rows(s):
        return slice(s * sub, (s + 1) * sub)

    def cols(hh):
        return slice(hh * DK, (hh + 1) * DK)

    dec = []
    for hh in heads:
        qf, k, v, logf = _hgrn_inputs(z_ref, lb, hh)
        b = _group_cumsum(logf, CHUNK)
        b3 = b.reshape(nc, CHUNK, DK)
        ref = b3[:, CHUNK // 2:CHUNK // 2 + 1, :]
        bl = b3[:, CHUNK - 1:CHUNK, :]
        q3 = qf.reshape(nc, CHUNK, DK)
        k3 = k.reshape(nc, CHUNK, DK)
        qs_ref[:, cols(hh)] = (q3 * jnp.exp(b3 - ref)).reshape(tm, DK).astype(BF16)
        ks_ref[:, cols(hh)] = (k3 * jnp.exp(ref - b3)).reshape(tm, DK).astype(BF16)
        qd_ref[:, cols(hh)] = (q3 * jnp.exp(b3)).reshape(tm, DK).astype(BF16)
        vb_ref[:, cols(hh)] = v.astype(BF16)
        kd = (k3 * jnp.exp(bl - b3)).reshape(tm, DK).astype(BF16)
        dec.append(jnp.exp(bl))
        for c in range(nc):
            cc = c % ncs
            kx_ref[hh, c * CHUNK:(c + 1) * CHUNK, cc * DK:(cc + 1) * DK] = (
                kd[c * CHUNK:(c + 1) * CHUNK, :])

    ext_ref[0:hdr, :] = jnp.where(first, 0.0, ext_ref[tm:tm + hdr, :])
    ext_ref[hdr:hdr + tm, :] = z_ref[:, OFF_U:OFF_U + W_BRANCH]
    pos1 = t * tm + lax.broadcasted_iota(jnp.int32, (tm, 1), 0) + 1
    for g, w in enumerate(POOL_WINDOWS):
        c = g * G_B
        s = ext_ref[:, c:c + G_B]
        sh = 1
        while sh < w:
            s = s + pltpu.roll(s, sh, 0)
            sh *= 2
        inv = jnp.where(pos1 >= w, 1.0 / w, 1.0 / pos1.astype(F32))
        pooled = s[hdr:, :] * inv - z_ref[:, OFF_U + c:OFF_U + c + G_B]
        _pool_epilogue(pooled, z_ref, g, wpool_ref, ps_ref, yb_ref)
    p_out_ref[0] = ext_ref[hdr + tm - POOL_BUF:hdr + tm, :]

    sc = {(hh, s): lax.dot_general(qs_ref[rows(s), cols(hh)], ks_ref[rows(s), cols(hh)],
                                   (((1,), (1,)), ((), ())), preferred_element_type=F32)
          for hh, s in units}
    ut = {(hh, s): lax.dot_general(vb_ref[rows(s), cols(hh)], kx_ref[hh, rows(s), :],
                                   (((0,), (0,)), ((), ())), preferred_element_type=F32)
          for hh, s in units}
    o = {(hh, s): jnp.dot(jnp.where(causal, sc[hh, s], 0.0).astype(BF16),
                          vb_ref[rows(s), cols(hh)], preferred_element_type=F32)
         for hh, s in units}
    for hh in heads:
        st = jnp.where(first, 0.0, st_ref[hh])
        for c in range(nc):
            sn_ref[hh, c] = st.T.astype(BF16)
            st = dec[hh][c] * st + ut[hh, c // ncs][:, (c % ncs) * DK:(c % ncs + 1) * DK]
        st_ref[hh] = st
        s_out_ref[0, hh] = st.T
    wcol = D_MODEL // N_HEADS
    y_b = []
    for hh in heads:
        o_inter = [jnp.dot(qd_ref[c * CHUNK:(c + 1) * CHUNK, cols(hh)], sn_ref[hh, c],
                           preferred_element_type=F32) for c in range(nc)]
        y_b.append(jnp.dot(yb_ref[...], wpb_ref[:, hh * wcol:(hh + 1) * wcol],
                           preferred_element_type=F32))
        o_hh = jnp.concatenate([o[hh, s] for s in range(n_sub)], axis=0)
        _hgrn_epilogue(o_hh + jnp.concatenate(o_inter, axis=0), z_ref, hg_ref, hh, oa_ref)
    gated_b = jax.nn.sigmoid(z_ref[:, OFF_MB:OFF_MB + D_MODEL]) * jnp.concatenate(y_b, axis=1)

    y_a = jnp.dot(oa_ref[...], wpa_ref[...], preferred_element_type=F32)
    merged = jax.nn.sigmoid(z_ref[:, OFF_MA:OFF_MA + D_MODEL]) * y_a + gated_b
    out = x + jnp.dot(merged.astype(BF16), wout_ref[...], preferred_element_type=F32)
    y_ref[0] = _rms(out, fg_ref[...])


def _shift_rows(x, k):
    n = x.shape[0]
    return x if k % n == 0 else pltpu.roll(x, (-k) % n, 0)


def _group_bcast(x, j, group):
    t = lax.broadcasted_iota(jnp.int32, x.shape, 0) & (group - 1)
    out = _shift_rows(x, j - (group - 1))
    for tt in range(group - 2, -1, -1):
        out = jnp.where(t == tt, _shift_rows(x, j - tt), out)
    return out


def _decode_prep_kernel(x_ref, p_in_ref, ng_ref, win_ref, lbl_ref, wpool_ref, ps_ref, wpb_ref,
                        a_ref, bv_ref, qd_ref, oi_ref, ga_ref, ma_ref, gb_ref, p_out_ref,
                        xs_ref, z_ref, rw_ref, u_ref, yb_ref, *, n_seq):
    rows = n_seq * DEC_SEQ
    for i in range(n_seq):
        xs_ref[i * DEC_SEQ:(i + 1) * DEC_SEQ, :] = x_ref[i]
    _in_proj(xs_ref[...], ng_ref, win_ref, z_ref)
    lb = _lower_bound(lbl_ref)

    r = lax.broadcasted_iota(jnp.int32, (rows, DK), 0)
    t = r & (DEC_SEQ - 1)
    r8 = r & (PAIR_ROWS - 1)
    even = r8 < DEC_SEQ
    ones_even = jnp.where((r8 >= DEC_SEQ) & (r8 < DEC_SEQ + 3), 1.0, 0.0).astype(BF16)
    ones_odd = jnp.where(r8 < 3, 1.0, 0.0).astype(BF16)
    blk = SUB_TILE
    ri = lax.broadcasted_iota(jnp.int32, (blk, blk), 0)
    ci = lax.broadcasted_iota(jnp.int32, (blk, blk), 1)
    causal = ((ri // DEC_SEQ) == (ci // DEC_SEQ)) & (ci <= ri)

    for hh in range(N_HEADS):
        cs = slice(hh * DK, (hh + 1) * DK)
        qf, k, v, logf = _hgrn_inputs(z_ref, lb, hh)
        b = _group_cumsum(logf, DEC_SEQ)
        ref = _group_bcast(b, DEC_SEQ // 2, DEC_SEQ)
        bl = _group_bcast(b, DEC_SEQ - 1, DEC_SEQ)
        qs = (qf * jnp.exp(b - ref)).astype(BF16)
        ks = (k * jnp.exp(ref - b)).astype(BF16)
        kd = k * jnp.exp(bl - b)
        vb = v.astype(BF16)
        qd_ref[:, cs] = (qf * jnp.exp(b)).astype(BF16)
        for s in range(rows // blk):
            rs = slice(s * blk, (s + 1) * blk)
            sc = lax.dot_general(qs[rs], ks[rs], (((1,), (1,)), ((), ())),
                                 preferred_element_type=F32)
            oi_ref[rs, cs] = jnp.dot(jnp.where(causal, sc, 0.0).astype(BF16), vb[rs],
                                     preferred_element_type=F32)
        dec = jnp.exp(bl)
        d1 = dec.astype(BF16).astype(F32)
        d2 = (dec - d1).astype(BF16).astype(F32)
        d3 = (dec - d1 - d2).astype(BF16).astype(F32)
        tail = jnp.where(t == 0, d1, jnp.where(t == 1, d2, jnp.where(t == 2, d3, 0.0)))
        a_ref[hh, 0] = jnp.where(even, kd, _shift_rows(tail, -DEC_SEQ)).astype(BF16)
        a_ref[hh, 1] = jnp.where(even, _shift_rows(tail, DEC_SEQ), kd).astype(BF16)
        zero = jnp.zeros_like(vb)
        bv_ref[hh, 0, :, 0:DV] = jnp.where(even, vb, zero)
        bv_ref[hh, 0, :, DV:2 * DV] = ones_even
        bv_ref[hh, 1, :, 0:DV] = jnp.where(even, zero, vb)
        bv_ref[hh, 1, :, DV:2 * DV] = ones_odd
        ga_ref[:, cs] = jax.nn.silu(z_ref[:, OFF_GA + hh * DK:OFF_GA + (hh + 1) * DK])

    for g, w in enumerate(POOL_WINDOWS):
        gc = slice(g * G_B, (g + 1) * G_B)
        acc = None
        suffix = {}
        for j in range(POOL_BUF - 1, -1, -1):
            e = p_in_ref[j, :, gc]
            acc = e if acc is None else acc + e
            suffix[j] = acc
        for tt in range(DEC_SEQ):
            j = POOL_BUF + 1 - w + tt
            rw_ref[g, pl.ds(tt, n_seq, stride=DEC_SEQ), :] = (
                suffix[j] if j < POOL_BUF else jnp.zeros((n_seq, G_B), F32))
        u = z_ref[:, OFF_U + g * G_B:OFF_U + (g + 1) * G_B]
        u_ref[g] = u
        cu = u
        sh = 1
        while sh < min(w, DEC_SEQ):
            cu = cu + jnp.where(t >= sh, _shift_rows(cu, -sh), 0.0)
            sh *= 2
        pooled = (rw_ref[g] + cu) * (1.0 / w) - u
        _pool_epilogue(pooled, z_ref, g, wpool_ref, ps_ref, yb_ref)
    gb_ref[...] = jax.nn.sigmoid(z_ref[:, OFF_MB:OFF_MB + D_MODEL]) * jnp.dot(
        yb_ref[...], wpb_ref[...], preferred_element_type=F32)
    ma_ref[...] = jax.nn.sigmoid(z_ref[:, OFF_MA:OFF_MA + D_MODEL])

    keep = POOL_BUF - DEC_SEQ
    for j in range(keep):
        p_out_ref[j] = p_in_ref[j + DEC_SEQ]
    for tt in range(DEC_SEQ):
        for g in range(len(POOL_WINDOWS)):
            p_out_ref[keep + tt, :, g * G_B:(g + 1) * G_B] = (
                u_ref[g, pl.ds(tt, n_seq, stride=DEC_SEQ), :])


def _decode_state_kernel(a_ref, bv_ref, qd_ref, s_in_ref, oi_ref, ga_ref, ma_ref, gb_ref, x_ref,
                         hg_ref, wpa_ref, wout_ref, fg_ref,
                         s_out_ref, y_ref, xs_ref, oa_ref):
    bb = BB_SAMPLE
    n_pairs = bb // 2
    even = (lax.broadcasted_iota(jnp.int32, (PAIR_ROWS, DV), 0)) < DEC_SEQ
    for i in range(bb):
        xs_ref[i * DEC_SEQ:(i + 1) * DEC_SEQ, :] = x_ref[i]
    for hh in range(N_HEADS):
        for lp in range(n_pairs):
            rs = slice(lp * PAIR_ROWS, (lp + 1) * PAIR_ROWS)
            for e in range(2):
                upd = lax.dot_general(a_ref[hh, e, rs, :], bv_ref[hh, e, rs, :],
                                      (((0,), (0,)), ((), ())), preferred_element_type=F32)
                s_out_ref[2 * lp + e, hh] = (upd[:, DV:2 * DV] * s_in_ref[2 * lp + e, hh]
                                             + upd[:, 0:DV])
    for hh in range(N_HEADS):
        cs = slice(hh * DK, (hh + 1) * DK)
        o_inter = []
        for lp in range(n_pairs):
            q8 = qd_ref[lp * PAIR_ROWS:(lp + 1) * PAIR_ROWS, cs]
            o0 = jnp.dot(q8, s_in_ref[2 * lp, hh].astype(BF16), preferred_element_type=F32)
            o1 = jnp.dot(q8, s_in_ref[2 * lp + 1, hh].astype(BF16), preferred_element_type=F32)
            o_inter.append(jnp.where(even, o0, o1))
        o = oi_ref[:, cs] + jnp.concatenate(o_inter, axis=0)
        oa_ref[:, cs] = (_rms(o, hg_ref[...]) * ga_ref[:, cs]).astype(BF16)
    y_a = jnp.dot(oa_ref[...], wpa_ref[...], preferred_element_type=F32)
    merged = ma_ref[...] * y_a + gb_ref[...]
    out = xs_ref[...] + jnp.dot(merged.astype(BF16), wout_ref[...], preferred_element_type=F32)
    y = _rms(out, fg_ref[...])
    for i in range(bb):
        y_ref[i] = y[i * DEC_SEQ:(i + 1) * DEC_SEQ, :]


def _cast_kernel(*refs):
    n = len(refs) // 2
    for src_ref, dst_ref in zip(refs[:n], refs[n:]):
        dst_ref[...] = src_ref[...].astype(BF16)


def _to_bf16(mats):
    specs = [pl.BlockSpec((m.shape[0] // CAST_STEPS, m.shape[1]), lambda i: (i, 0)) for m in mats]
    return pl.pallas_call(
        _cast_kernel,
        grid=(CAST_STEPS,),
        in_specs=specs,
        out_specs=specs,
        out_shape=[jax.ShapeDtypeStruct(m.shape, BF16) for m in mats],
        compiler_params=pltpu.CompilerParams(
            dimension_semantics=("arbitrary",), vmem_limit_bytes=VMEM_LIMIT_BYTES),
        name="hgrn2_pool_cast_weights",
    )(*mats)


def _const_spec(shape):
    n = len(shape)
    return pl.BlockSpec(shape, lambda *_: (0,) * n, pipeline_mode=pl.Buffered(1))


def _weight_specs():
    return [
        _const_spec((1, D_MODEL)),
        _const_spec((D_MODEL, D_IN)),
        _const_spec((2, W_BRANCH)),
        _const_spec((1, DV)),
        _const_spec((W_BRANCH, D_MODEL)),
        _const_spec((len(POOL_WINDOWS), G_B, G_B)),
        _const_spec((1, W_BRANCH)),
        _const_spec((W_BRANCH, D_MODEL)),
        _const_spec((D_MODEL, D_MODEL)),
        _const_spec((1, D_MODEL)),
    ]


def kernel(x_prompt, x_sample, state_hgrn, state_pool, norm_g, w_in, lb_logits, hgrn_norm_g,
           w_proj_a, w_pool, pool_scale, w_proj_b, w_out, final_norm_g):
    batch, seq, _ = x_prompt.shape
    dec_batch, dec_seq, _ = x_sample.shape
    assert norm_g.shape[0] == 1 and lb_logits.shape[0] == 2, "single-layer decoder only"
    assert seq % TM_PROMPT == 0 and TM_PROMPT % SUB_TILE == 0 and dec_batch % BB_SAMPLE == 0
    assert dec_seq == DEC_SEQ and PAIR_ROWS == SUBLANES and BB_SAMPLE % 2 == 0
    assert dec_batch % PREP_SEQS == 0 and (PREP_SEQS * DEC_SEQ) % SUB_TILE == 0
    assert PAST_LEN >= max(POOL_WINDOWS)

    n_groups = len(POOL_WINDOWS)
    wpa_b, wpool_b, wpb_b, wout_b = _to_bf16(
        [w_proj_a[0], w_pool[0].reshape(n_groups * G_B, G_B), w_proj_b[0], w_out[0]])
    win_b = w_in[0]
    weights = (norm_g, win_b, lb_logits, hgrn_norm_g, wpa_b, wpool_b.reshape(n_groups, G_B, G_B),
               pool_scale, wpb_b, wout_b, final_norm_g.reshape(1, D_MODEL))

    tm = TM_PROMPT
    nt = seq // tm
    y_p, s_p, p_p = pl.pallas_call(
        _prompt_kernel,
        grid=(batch, nt),
        in_specs=[pl.BlockSpec((1, tm, D_MODEL), lambda b, t: (b, t, 0))] + _weight_specs(),
        out_specs=[
            pl.BlockSpec((1, tm, D_MODEL), lambda b, t: (b, t, 0)),
            pl.BlockSpec((1, N_HEADS, DK, DV), lambda b, t: (b, 0, 0, 0)),
            pl.BlockSpec((1, POOL_BUF, W_BRANCH), lambda b, t: (b, 0, 0)),
        ],
        out_shape=[
            jax.ShapeDtypeStruct((batch, seq, D_MODEL), F32),
            jax.ShapeDtypeStruct((batch, N_HEADS, DK, DV), F32),
            jax.ShapeDtypeStruct((batch, POOL_BUF, W_BRANCH), F32),
        ],
        scratch_shapes=[
            pltpu.VMEM((tm, D_IN), F32),
            pltpu.VMEM((N_HEADS, DV, DK), F32),
            pltpu.VMEM((tm + 2 * SUBLANES, W_BRANCH), F32),
            pltpu.VMEM((N_HEADS, tm, (SUB_TILE // CHUNK) * DK), BF16),
            pltpu.VMEM((N_HEADS, tm // CHUNK, DK, DV), BF16),
            pltpu.VMEM((tm, W_BRANCH), BF16),
            pltpu.VMEM((tm, W_BRANCH), BF16),
            pltpu.VMEM((tm, W_BRANCH), BF16),
            pltpu.VMEM((tm, W_BRANCH), BF16),
            pltpu.VMEM((tm, W_BRANCH), BF16),
            pltpu.VMEM((tm, W_BRANCH), BF16),
        ],
        compiler_params=pltpu.CompilerParams(
            dimension_semantics=("arbitrary", "arbitrary"),
            vmem_limit_bytes=VMEM_LIMIT_BYTES),
        name="hgrn2_pool_prompt",
    )(x_prompt, *weights)

    rows = dec_batch * DEC_SEQ
    pool_in = jnp.transpose(state_pool[0], (1, 0, 2))
    ng, win_b, lbl, hg, wpa_b, wpool_b, ps, wpb_b, wout_b, fg = weights
    prep_shapes = [
        jax.ShapeDtypeStruct((N_HEADS, 2, rows, DK), BF16),
        jax.ShapeDtypeStruct((N_HEADS, 2, rows, 2 * DV), BF16),
        jax.ShapeDtypeStruct((rows, W_BRANCH), BF16),
        jax.ShapeDtypeStruct((rows, W_BRANCH), F32),
        jax.ShapeDtypeStruct((rows, W_BRANCH), F32),
        jax.ShapeDtypeStruct((rows, D_MODEL), F32),
        jax.ShapeDtypeStruct((rows, D_MODEL), F32),
        jax.ShapeDtypeStruct((POOL_BUF, dec_batch, W_BRANCH), F32),
    ]
    pseq = PREP_SEQS
    prows = pseq * DEC_SEQ
    a_t, bv_t, qd_t, oi_t, ga_t, ma_t, gb_t, pool_out = pl.pallas_call(
        functools.partial(_decode_prep_kernel, n_seq=pseq),
        grid=(dec_batch // pseq,),
        in_specs=[pl.BlockSpec((pseq, DEC_SEQ, D_MODEL), lambda i: (i, 0, 0)),
                  pl.BlockSpec((POOL_BUF, pseq, W_BRANCH), lambda i: (0, i, 0)),
                  _const_spec(ng.shape), _const_spec(win_b.shape), _const_spec(lbl.shape),
                  _const_spec(wpool_b.shape), _const_spec(ps.shape), _const_spec(wpb_b.shape)],
        out_specs=[
            pl.BlockSpec((N_HEADS, 2, prows, DK), lambda i: (0, 0, i, 0)),
            pl.BlockSpec((N_HEADS, 2, prows, 2 * DV), lambda i: (0, 0, i, 0)),
            pl.BlockSpec((prows, W_BRANCH), lambda i: (i, 0)),
            pl.BlockSpec((prows, W_BRANCH), lambda i: (i, 0)),
            pl.BlockSpec((prows, W_BRANCH), lambda i: (i, 0)),
            pl.BlockSpec((prows, D_MODEL), lambda i: (i, 0)),
            pl.BlockSpec((prows, D_MODEL), lambda i: (i, 0)),
            pl.BlockSpec((POOL_BUF, pseq, W_BRANCH), lambda i: (0, i, 0)),
        ],
        out_shape=prep_shapes,
        scratch_shapes=[
            pltpu.VMEM((prows, D_MODEL), F32),
            pltpu.VMEM((prows, D_IN), F32),
            pltpu.VMEM((len(POOL_WINDOWS), prows, G_B), F32),
            pltpu.VMEM((len(POOL_WINDOWS), prows, G_B), F32),
            pltpu.VMEM((prows, W_BRANCH), BF16),
        ],
        compiler_params=pltpu.CompilerParams(
            dimension_semantics=("arbitrary",), vmem_limit_bytes=VMEM_LIMIT_BYTES),
        name="hgrn2_pool_decode_prep",
    )(x_sample, pool_in, ng, win_b, lbl, wpool_b, ps, wpb_b)

    bb = BB_SAMPLE
    brows = bb * DEC_SEQ
    s_s, y_s = pl.pallas_call(
        _decode_state_kernel,
        grid=(dec_batch // bb,),
        in_specs=[
            pl.BlockSpec((N_HEADS, 2, brows, DK), lambda i: (0, 0, i, 0)),
            pl.BlockSpec((N_HEADS, 2, brows, 2 * DV), lambda i: (0, 0, i, 0)),
            pl.BlockSpec((brows, W_BRANCH), lambda i: (i, 0)),
            pl.BlockSpec((bb, N_HEADS, DK, DV), lambda i: (i, 0, 0, 0)),
            pl.BlockSpec((brows, W_BRANCH), lambda i: (i, 0)),
            pl.BlockSpec((brows, W_BRANCH), lambda i: (i, 0)),
            pl.BlockSpec((brows, D_MODEL), lambda i: (i, 0)),
            pl.BlockSpec((brows, D_MODEL), lambda i: (i, 0)),
            pl.BlockSpec((bb, DEC_SEQ, D_MODEL), lambda i: (i, 0, 0)),
            _const_spec(hg.shape), _const_spec(wpa_b.shape), _const_spec(wout_b.shape),
            _const_spec(fg.shape),
        ],
        out_specs=[
            pl.BlockSpec((bb, N_HEADS, DK, DV), lambda i: (i, 0, 0, 0)),
            pl.BlockSpec((bb, DEC_SEQ, D_MODEL), lambda i: (i, 0, 0)),
        ],
        out_shape=[
            jax.ShapeDtypeStruct((dec_batch, N_HEADS, DK, DV), F32),
            jax.ShapeDtypeStruct(x_sample.shape, F32),
        ],
        scratch_shapes=[
            pltpu.VMEM((brows, D_MODEL), F32),
            pltpu.VMEM((brows, W_BRANCH), BF16),
        ],
        compiler_params=pltpu.CompilerParams(
            dimension_semantics=("arbitrary",), vmem_limit_bytes=VMEM_LIMIT_BYTES),
        name="hgrn2_pool_decode_state",
    )(a_t, bv_t, qd_t, state_hgrn[0], oi_t, ga_t, ma_t, gb_t, x_sample, hg, wpa_b, wout_b, fg)

    p_s = jnp.transpose(pool_out, (1, 0, 2))
    return (y_p, y_s, s_p[None], p_p[None], s_s[None], p_s[None])
```

```python
import functools

import jax
import jax.numpy as jnp
from jax import lax
from jax.experimental import pallas as pl
from jax.experimental.pallas import tpu as pltpu

F32 = jnp.float32
BF16 = jnp.bfloat16

D_MODEL = 1024
W_BRANCH = 512
N_HEADS = 4
DK = 128
DV = 128
CHUNK = 32
POOL_WINDOWS = (2, 4, 8, 16)
G_B = 128
POOL_BUF = 15
PAST_LEN = 16384
EPS = 1e-6
D_IN = 4 * W_BRANCH + 2 * W_BRANCH + 2 * D_MODEL
OFF_Q, OFF_F, OFF_I, OFF_GA = 0, 512, 1024, 1536
OFF_U, OFF_GB, OFF_MA, OFF_MB = 2048, 2560, 3072, 4096

SUBLANES = 8
VMEM_LIMIT_BYTES = 56 * 1024 * 1024

TM_PROMPT = 512
SUB_TILE = 256
PREP_SEQS = 64
BB_SAMPLE = 16
DEC_SEQ = 4
PAIR_ROWS = 2 * DEC_SEQ


def _rms(x, g):
    ms = jnp.mean(x * x, axis=-1, keepdims=True)
    return x * lax.rsqrt(ms + EPS) * g


def _lower_bound(lbl_ref):
    l0 = lbl_ref[0:1, :]
    l1 = lbl_ref[1:2, :]
    m = jnp.maximum(l0, l1)
    e0 = jnp.exp(l0 - m)
    e1 = jnp.exp(l1 - m)
    return e0 / (e0 + e1)


def _in_proj(x, ng_ref, win_ref, z_ref):
    h = _rms(x, ng_ref[...]).astype(BF16)
    z_ref[...] = jnp.dot(h, win_ref[...], preferred_element_type=F32)


def _hgrn_inputs(z_ref, lb, hh):
    c = hh * DK
    q = z_ref[:, OFF_Q + c:OFF_Q + c + DK]
    f = z_ref[:, OFF_F + c:OFF_F + c + DK]
    v = z_ref[:, OFF_I + c:OFF_I + c + DK]
    lbh = lb[:, c:c + DK]
    fg = lbh + (1.0 - lbh) * jax.nn.sigmoid(f)
    logf = jnp.log(fg)
    k = 1.0 - fg
    qf = jax.nn.silu(q) * (DK ** -0.5)
    return qf, k, v, logf


def _group_cumsum(x, group):
    row = lax.broadcasted_iota(jnp.int32, x.shape, 0) & (group - 1)
    s = 1
    while s < group:
        x = x + jnp.where(row >= s, pltpu.roll(x, s, 0), 0.0)
        s *= 2
    return x


def _hgrn_epilogue(o, z_ref, hg_ref, hh, oa_ref):
    c = hh * DK
    ga = z_ref[:, OFF_GA + c:OFF_GA + c + DK]
    on = _rms(o, hg_ref[...]) * jax.nn.silu(ga)
    oa_ref[:, c:c + DK] = on.astype(BF16)


def _pool_epilogue(pooled, z_ref, g, wpool_ref, ps_ref, yb_ref):
    c = g * G_B
    mixed = jnp.dot(pooled.astype(BF16), wpool_ref[g], preferred_element_type=F32)
    gb = z_ref[:, OFF_GB + c:OFF_GB + c + G_B]
    yb = mixed * ps_ref[:, c:c + G_B] * jax.nn.silu(gb)
    yb_ref[:, c:c + G_B] = yb.astype(BF16)


def _weight_copies(hbm_refs, stage_ref, sem):
    win_hbm, wpa_hbm, wpool_hbm, wpb_hbm, wout_hbm = hbm_refs
    rows = stage_ref.shape[0]
    small = [
        (wpa_hbm, stage_ref.at[:, 0:D_MODEL]),
        (wpb_hbm, stage_ref.at[:, D_MODEL:2 * D_MODEL]),
        (wout_hbm.at[0:rows, :], stage_ref.at[:, 2 * D_MODEL:3 * D_MODEL]),
        (wout_hbm.at[rows:2 * rows, :], stage_ref.at[:, 3 * D_MODEL:4 * D_MODEL]),
        (wpool_hbm, stage_ref.at[:, 4 * D_MODEL:4 * D_MODEL + G_B]),
    ]
    small = [pltpu.make_async_copy(src, dst, sem.at[i]) for i, (src, dst) in enumerate(small)]
    blk = rows // 2

    def win(k):
        slot = k % 2
        return pltpu.make_async_copy(win_hbm.at[k * blk:(k + 1) * blk, :],
                                     stage_ref.at[slot * blk:(slot + 1) * blk, :],
                                     sem.at[len(small) + slot])
    return small, win, blk


def _weight_exports(vmem_refs, out_refs, sem):
    return [pltpu.make_async_copy(src, dst, sem.at[i])
            for i, (src, dst) in enumerate(zip(vmem_refs, out_refs))]


def _prompt_kernel(x_ref, ng_ref, win_hbm, lbl_ref, hg_ref, wpa_hbm, wpool_hbm, ps_ref,
                   wpb_hbm, wout_hbm, fg_ref,
                   y_ref, s_out_ref, p_out_ref, win_out, wpa_out, wpool_out, wpb_out, wout_out,
                   z_ref, st_ref, ext_ref, kx_ref, sn_ref, qs_ref, ks_ref, qd_ref, vb_ref,
                   oa_ref, yb_ref, win_ref, wpa_ref, wpool_ref, wpb_ref, wout_ref,
                   load_sem, export_sem):
    tm = TM_PROMPT
    sub = SUB_TILE
    n_sub = tm // sub
    nc = tm // CHUNK
    ncs = sub // CHUNK
    t = pl.program_id(1)
    first = t == 0
    hdr = 2 * SUBLANES
    bf16_weights = (win_ref, wpa_ref, wpool_ref, wpb_ref, wout_ref)
    exports = _weight_exports(bf16_weights, (win_out, wpa_out, wpool_out, wpb_out, wout_out),
                              export_sem)

    @pl.when((pl.program_id(0) == 0) & first)
    def _():
        st_ref[...] = jnp.zeros_like(st_ref)
        ext_ref[...] = jnp.zeros_like(ext_ref)
        kx_ref[...] = jnp.zeros_like(kx_ref)
        small, win, blk = _weight_copies((win_hbm, wpa_hbm, wpool_hbm, wpb_hbm, wout_hbm),
                                         z_ref, load_sem)
        for c in small:
            c.start()
        for c in small:
            c.wait()
        wpa_ref[...] = z_ref[:, 0:D_MODEL].astype(BF16)
        wpb_ref[...] = z_ref[:, D_MODEL:2 * D_MODEL].astype(BF16)
        wout_ref[0:tm, :] = z_ref[:, 2 * D_MODEL:3 * D_MODEL].astype(BF16)
        wout_ref[tm:2 * tm, :] = z_ref[:, 3 * D_MODEL:4 * D_MODEL].astype(BF16)
        for g in range(len(POOL_WINDOWS)):
            wpool_ref[g] = z_ref[g * G_B:(g + 1) * G_B,
                                 4 * D_MODEL:4 * D_MODEL + G_B].astype(BF16)
        n_blk = D_MODEL // blk
        win(0).start()
        win(1).start()
        for k in range(n_blk):
            slot = k % 2
            win(k).wait()
            win_ref[k * blk:(k + 1) * blk, :] = z_ref[slot * blk:(slot + 1) * blk, :].astype(BF16)
            if k + 2 < n_blk:
                win(k + 2).start()
        for c in exports:
            c.start()

    @pl.when((pl.program_id(0) == pl.num_programs(0) - 1) & (t == pl.num_programs(1) - 1))
    def _():
        for c in exports:
            c.wait()

    x = x_ref[0]
    _in_proj(x, ng_ref, win_ref, z_ref)
    lb = _lower_bound(lbl_ref)

    ri = lax.broadcasted_iota(jnp.int32, (sub, sub), 0)
    ci = lax.broadcasted_iota(jnp.int32, (sub, sub), 1)
    causal = ((ri // CHUNK) == (ci // CHUNK)) & (ci <= ri)

    heads = range(N_HEADS)
    units = [(hh, s) for hh in heads for s in range(n_sub)]

    def rows(s):
        return slice(s * sub, (s + 1) * sub)

    def cols(hh):
        return slice(hh * DK, (hh + 1) * DK)

    dec = []
    for hh in heads:
        qf, k, v, logf = _hgrn_inputs(z_ref, lb, hh)
        b = _group_cumsum(logf, CHUNK)
        b3 = b.reshape(nc, CHUNK, DK)
        ref = b3[:, CHUNK // 2:CHUNK // 2 + 1, :]
        bl = b3[:, CHUNK - 1:CHUNK, :]
        q3 = qf.reshape(nc, CHUNK, DK)
        k3 = k.reshape(nc, CHUNK, DK)
        qs_ref[:, cols(hh)] = (q3 * jnp.exp(b3 - ref)).reshape(tm, DK).astype(BF16)
        ks_ref[:, cols(hh)] = (k3 * jnp.exp(ref - b3)).reshape(tm, DK).astype(BF16)
        qd_ref[:, cols(hh)] = (q3 * jnp.exp(b3)).reshape(tm, DK).astype(BF16)
        vb_ref[:, cols(hh)] = v.astype(BF16)
        kd = (k3 * jnp.exp(bl - b3)).reshape(tm, DK).astype(BF16)
        dec.append(jnp.exp(bl))
        for c in range(nc):
            cc = c % ncs
            kx_ref[hh, c * CHUNK:(c + 1) * CHUNK, cc * DK:(cc + 1) * DK] = (
                kd[c * CHUNK:(c + 1) * CHUNK, :])

    ext_ref[0:hdr, :] = jnp.where(first, 0.0, ext_ref[tm:tm + hdr, :])
    ext_ref[hdr:hdr + tm, :] = z_ref[:, OFF_U:OFF_U + W_BRANCH]
    pos1 = t * tm + lax.broadcasted_iota(jnp.int32, (tm, 1), 0) + 1
    for g, w in enumerate(POOL_WINDOWS):
        c = g * G_B
        s = ext_ref[:, c:c + G_B]
        sh = 1
        while sh < w:
            s = s + pltpu.roll(s, sh, 0)
            sh *= 2
        inv = jnp.where(pos1 >= w, 1.0 / w, 1.0 / pos1.astype(F32))
        pooled = s[hdr:, :] * inv - z_ref[:, OFF_U + c:OFF_U + c + G_B]
        _pool_epilogue(pooled, z_ref, g, wpool_ref, ps_ref, yb_ref)
    p_out_ref[0] = ext_ref[hdr + tm - POOL_BUF:hdr + tm, :]

    sc = {(hh, s): lax.dot_general(qs_ref[rows(s), cols(hh)], ks_ref[rows(s), cols(hh)],
                                   (((1,), (1,)), ((), ())), preferred_element_type=F32)
          for hh, s in units}
    ut = {(hh, s): lax.dot_general(vb_ref[rows(s), cols(hh)], kx_ref[hh, rows(s), :],
                                   (((0,), (0,)), ((), ())), preferred_element_type=F32)
          for hh, s in units}
    o = {(hh, s): jnp.dot(jnp.where(causal, sc[hh, s], 0.0).astype(BF16),
                          vb_ref[rows(s), cols(hh)], preferred_element_type=F32)
         for hh, s in units}
    for hh in heads:
        st = jnp.where(first, 0.0, st_ref[hh])
        for c in range(nc):
            sn_ref[hh, c] = st.T.astype(BF16)
            st = dec[hh][c] * st + ut[hh, c // ncs][:, (c % ncs) * DK:(c % ncs + 1) * DK]
        st_ref[hh] = st
        s_out_ref[0, hh] = st.T
    wcol = D_MODEL // N_HEADS
    y_b = []
    for hh in heads:
        o_inter = [jnp.dot(qd_ref[c * CHUNK:(c + 1) * CHUNK, cols(hh)], sn_ref[hh, c],
                           preferred_element_type=F32) for c in range(nc)]
        y_b.append(jnp.dot(yb_ref[...], wpb_ref[:, hh * wcol:(hh + 1) * wcol],
                           preferred_element_type=F32))
        o_hh = jnp.concatenate([o[hh, s] for s in range(n_sub)], axis=0)
        _hgrn_epilogue(o_hh + jnp.concatenate(o_inter, axis=0), z_ref, hg_ref, hh, oa_ref)
    gated_b = jax.nn.sigmoid(z_ref[:, OFF_MB:OFF_MB + D_MODEL]) * jnp.concatenate(y_b, axis=1)

    y_a = jnp.dot(oa_ref[...], wpa_ref[...], preferred_element_type=F32)
    merged = jax.nn.sigmoid(z_ref[:, OFF_MA:OFF_MA + D_MODEL]) * y_a + gated_b
    out = x + jnp.dot(merged.astype(BF16), wout_ref[...], preferred_element_type=F32)
    y_ref[0] = _rms(out, fg_ref[...])


def _shift_rows(x, k):
    n = x.shape[0]
    return x if k % n == 0 else pltpu.roll(x, (-k) % n, 0)


def _group_bcast(x, j, group):
    t = lax.broadcasted_iota(jnp.int32, x.shape, 0) & (group - 1)
    out = _shift_rows(x, j - (group - 1))
    for tt in range(group - 2, -1, -1):
        out = jnp.where(t == tt, _shift_rows(x, j - tt), out)
    return out


def _decode_prep_kernel(x_ref, p_in_ref, ng_ref, win_ref, lbl_ref, wpool_ref, ps_ref, wpb_ref,
                        a_ref, bv_ref, qd_ref, oi_ref, ga_ref, ma_ref, gb_ref, p_out_ref,
                        xs_ref, z_ref, rw_ref, u_ref, yb_ref, *, n_seq):
    rows = n_seq * DEC_SEQ
    for i in range(n_seq):
        xs_ref[i * DEC_SEQ:(i + 1) * DEC_SEQ, :] = x_ref[i]
    _in_proj(xs_ref[...], ng_ref, win_ref, z_ref)
    lb = _lower_bound(lbl_ref)

    r = lax.broadcasted_iota(jnp.int32, (rows, DK), 0)
    t = r & (DEC_SEQ - 1)
    r8 = r & (PAIR_ROWS - 1)
    even = r8 < DEC_SEQ
    ones_even = jnp.where((r8 >= DEC_SEQ) & (r8 < DEC_SEQ + 3), 1.0, 0.0).astype(BF16)
    ones_odd = jnp.where(r8 < 3, 1.0, 0.0).astype(BF16)
    blk = SUB_TILE
    ri = lax.broadcasted_iota(jnp.int32, (blk, blk), 0)
    ci = lax.broadcasted_iota(jnp.int32, (blk, blk), 1)
    causal = ((ri // DEC_SEQ) == (ci // DEC_SEQ)) & (ci <= ri)

    for hh in range(N_HEADS):
        cs = slice(hh * DK, (hh + 1) * DK)
        qf, k, v, logf = _hgrn_inputs(z_ref, lb, hh)
        b = _group_cumsum(logf, DEC_SEQ)
        ref = _group_bcast(b, DEC_SEQ // 2, DEC_SEQ)
        bl = _group_bcast(b, DEC_SEQ - 1, DEC_SEQ)
        qs = (qf * jnp.exp(b - ref)).astype(BF16)
        ks = (k * jnp.exp(ref - b)).astype(BF16)
        kd = k * jnp.exp(bl - b)
        vb = v.astype(BF16)
        qd_ref[:, cs] = (qf * jnp.exp(b)).astype(BF16)
        for s in range(rows // blk):
            rs = slice(s * blk, (s + 1) * blk)
            sc = lax.dot_general(qs[rs], ks[rs], (((1,), (1,)), ((), ())),
                                 preferred_element_type=F32)
            oi_ref[rs, cs] = jnp.dot(jnp.where(causal, sc, 0.0).astype(BF16), vb[rs],
                                     preferred_element_type=F32)
        dec = jnp.exp(bl)
        d1 = dec.astype(BF16).astype(F32)
        d2 = (dec - d1).astype(BF16).astype(F32)
        d3 = (dec - d1 - d2).astype(BF16).astype(F32)
        tail = jnp.where(t == 0, d1, jnp.where(t == 1, d2, jnp.where(t == 2, d3, 0.0)))
        a_ref[hh, 0] = jnp.where(even, kd, _shift_rows(tail, -DEC_SEQ)).astype(BF16)
        a_ref[hh, 1] = jnp.where(even, _shift_rows(tail, DEC_SEQ), kd).astype(BF16)
        zero = jnp.zeros_like(vb)
        bv_ref[hh, 0, :, 0:DV] = jnp.where(even, vb, zero)
        bv_ref[hh, 0, :, DV:2 * DV] = ones_even
        bv_ref[hh, 1, :, 0:DV] = jnp.where(even, zero, vb)
        bv_ref[hh, 1, :, DV:2 * DV] = ones_odd
        ga_ref[:, cs] = jax.nn.silu(z_ref[:, OFF_GA + hh * DK:OFF_GA + (hh + 1) * DK])

    for g, w in enumerate(POOL_WINDOWS):
        gc = slice(g * G_B, (g + 1) * G_B)
        acc = None
        suffix = {}
        for j in range(POOL_BUF - 1, -1, -1):
            e = p_in_ref[j, :, gc]
            acc = e if acc is None else acc + e
            suffix[j] = acc
        for tt in range(DEC_SEQ):
            j = POOL_BUF + 1 - w + tt
            rw_ref[g, pl.ds(tt, n_seq, stride=DEC_SEQ), :] = (
                suffix[j] if j < POOL_BUF else jnp.zeros((n_seq, G_B), F32))
        u = z_ref[:, OFF_U + g * G_B:OFF_U + (g + 1) * G_B]
        u_ref[g] = u
        cu = u
        sh = 1
        while sh < min(w, DEC_SEQ):
            cu = cu + jnp.where(t >= sh, _shift_rows(cu, -sh), 0.0)
            sh *= 2
        pooled = (rw_ref[g] + cu) * (1.0 / w) - u
        _pool_epilogue(pooled, z_ref, g, wpool_ref, ps_ref, yb_ref)
    gb_ref[...] = jax.nn.sigmoid(z_ref[:, OFF_MB:OFF_MB + D_MODEL]) * jnp.dot(
        yb_ref[...], wpb_ref[...], preferred_element_type=F32)
    ma_ref[...] = jax.nn.sigmoid(z_ref[:, OFF_MA:OFF_MA + D_MODEL])

    keep = POOL_BUF - DEC_SEQ
    for j in range(keep):
        p_out_ref[j] = p_in_ref[j + DEC_SEQ]
    for tt in range(DEC_SEQ):
        for g in range(len(POOL_WINDOWS)):
            p_out_ref[keep + tt, :, g * G_B:(g + 1) * G_B] = (
                u_ref[g, pl.ds(tt, n_seq, stride=DEC_SEQ), :])


def _decode_state_kernel(a_ref, bv_ref, qd_ref, s_in_ref, oi_ref, ga_ref, ma_ref, gb_ref, x_ref,
                         hg_ref, wpa_ref, wout_ref, fg_ref,
                         s_out_ref, y_ref, xs_ref, oa_ref):
    bb = BB_SAMPLE
    n_pairs = bb // 2
    even = (lax.broadcasted_iota(jnp.int32, (PAIR_ROWS, DV), 0)) < DEC_SEQ
    for i in range(bb):
        xs_ref[i * DEC_SEQ:(i + 1) * DEC_SEQ, :] = x_ref[i]
    for hh in range(N_HEADS):
        for lp in range(n_pairs):
            rs = slice(lp * PAIR_ROWS, (lp + 1) * PAIR_ROWS)
            for e in range(2):
                upd = lax.dot_general(a_ref[hh, e, rs, :], bv_ref[hh, e, rs, :],
                                      (((0,), (0,)), ((), ())), preferred_element_type=F32)
                s_out_ref[2 * lp + e, hh] = (upd[:, DV:2 * DV] * s_in_ref[2 * lp + e, hh]
                                             + upd[:, 0:DV])
    for hh in range(N_HEADS):
        cs = slice(hh * DK, (hh + 1) * DK)
        o_inter = []
        for lp in range(n_pairs):
            q8 = qd_ref[lp * PAIR_ROWS:(lp + 1) * PAIR_ROWS, cs]
            o0 = jnp.dot(q8, s_in_ref[2 * lp, hh].astype(BF16), preferred_element_type=F32)
            o1 = jnp.dot(q8, s_in_ref[2 * lp + 1, hh].astype(BF16), preferred_element_type=F32)
            o_inter.append(jnp.where(even, o0, o1))
        o = oi_ref[:, cs] + jnp.concatenate(o_inter, axis=0)
        oa_ref[:, cs] = (_rms(o, hg_ref[...]) * ga_ref[:, cs]).astype(BF16)
    y_a = jnp.dot(oa_ref[...], wpa_ref[...], preferred_element_type=F32)
    merged = ma_ref[...] * y_a + gb_ref[...]
    out = xs_ref[...] + jnp.dot(merged.astype(BF16), wout_ref[...], preferred_element_type=F32)
    y = _rms(out, fg_ref[...])
    for i in range(bb):
        y_ref[i] = y[i * DEC_SEQ:(i + 1) * DEC_SEQ, :]


def _const_spec(shape):
    n = len(shape)
    return pl.BlockSpec(shape, lambda *_: (0,) * n, pipeline_mode=pl.Buffered(1))


def kernel(x_prompt, x_sample, state_hgrn, state_pool, norm_g, w_in, lb_logits, hgrn_norm_g,
           w_proj_a, w_pool, pool_scale, w_proj_b, w_out, final_norm_g):
    batch, seq, _ = x_prompt.shape
    dec_batch, dec_seq, _ = x_sample.shape
    assert norm_g.shape[0] == 1 and lb_logits.shape[0] == 2, "single-layer decoder only"
    assert seq % TM_PROMPT == 0 and TM_PROMPT % SUB_TILE == 0 and dec_batch % BB_SAMPLE == 0
    assert dec_seq == DEC_SEQ and PAIR_ROWS == SUBLANES and BB_SAMPLE % 2 == 0
    assert dec_batch % PREP_SEQS == 0 and (PREP_SEQS * DEC_SEQ) % SUB_TILE == 0
    assert PAST_LEN >= max(POOL_WINDOWS)

    n_groups = len(POOL_WINDOWS)
    tm = TM_PROMPT
    nt = seq // tm
    assert w_in.shape[1:] == (2 * tm, D_IN) and w_proj_a.shape[1] == tm, "weight staging layout"
    hbm = pl.BlockSpec(memory_space=pl.ANY)
    bf16_shapes = [(D_MODEL, D_IN), (W_BRANCH, D_MODEL), (n_groups, G_B, G_B),
                   (W_BRANCH, D_MODEL), (D_MODEL, D_MODEL)]
    y_p, s_p, p_p, win_b, wpa_b, wpool_b, wpb_b, wout_b = pl.pallas_call(
        _prompt_kernel,
        grid=(batch, nt),
        in_specs=[pl.BlockSpec((1, tm, D_MODEL), lambda b, t: (b, t, 0)),
                  _const_spec((1, D_MODEL)), hbm, _const_spec((2, W_BRANCH)), _const_spec((1, DV)),
                  hbm, hbm, _const_spec((1, W_BRANCH)), hbm, hbm, _const_spec((1, D_MODEL))],
        out_specs=[
            pl.BlockSpec((1, tm, D_MODEL), lambda b, t: (b, t, 0)),
            pl.BlockSpec((1, N_HEADS, DK, DV), lambda b, t: (b, 0, 0, 0)),
            pl.BlockSpec((1, POOL_BUF, W_BRANCH), lambda b, t: (b, 0, 0)),
        ] + [hbm] * len(bf16_shapes),
        out_shape=[
            jax.ShapeDtypeStruct((batch, seq, D_MODEL), F32),
            jax.ShapeDtypeStruct((batch, N_HEADS, DK, DV), F32),
            jax.ShapeDtypeStruct((batch, POOL_BUF, W_BRANCH), F32),
        ] + [jax.ShapeDtypeStruct(s, BF16) for s in bf16_shapes],
        scratch_shapes=[
            pltpu.VMEM((tm, D_IN), F32),
            pltpu.VMEM((N_HEADS, DV, DK), F32),
            pltpu.VMEM((tm + 2 * SUBLANES, W_BRANCH), F32),
            pltpu.VMEM((N_HEADS, tm, (SUB_TILE // CHUNK) * DK), BF16),
            pltpu.VMEM((N_HEADS, tm // CHUNK, DK, DV), BF16),
            pltpu.VMEM((tm, W_BRANCH), BF16),
            pltpu.VMEM((tm, W_BRANCH), BF16),
            pltpu.VMEM((tm, W_BRANCH), BF16),
            pltpu.VMEM((tm, W_BRANCH), BF16),
            pltpu.VMEM((tm, W_BRANCH), BF16),
            pltpu.VMEM((tm, W_BRANCH), BF16),
        ] + [pltpu.VMEM(s, BF16) for s in bf16_shapes] + [
            pltpu.SemaphoreType.DMA((7,)),
            pltpu.SemaphoreType.DMA((len(bf16_shapes),)),
        ],
        compiler_params=pltpu.CompilerParams(
            dimension_semantics=("arbitrary", "arbitrary"),
            vmem_limit_bytes=VMEM_LIMIT_BYTES),
        name="hgrn2_pool_prompt",
    )(x_prompt, norm_g, w_in[0], lb_logits, hgrn_norm_g, w_proj_a[0],
      w_pool[0].reshape(n_groups * G_B, G_B), pool_scale, w_proj_b[0], w_out[0],
      final_norm_g.reshape(1, D_MODEL))
    weights = (norm_g, win_b, lb_logits, hgrn_norm_g, wpa_b, wpool_b, pool_scale, wpb_b, wout_b,
               final_norm_g.reshape(1, D_MODEL))

    rows = dec_batch * DEC_SEQ
    pool_in = jnp.transpose(state_pool[0], (1, 0, 2))
    ng, win_b, lbl, hg, wpa_b, wpool_b, ps, wpb_b, wout_b, fg = weights
    prep_shapes = [
        jax.ShapeDtypeStruct((N_HEADS, 2, rows, DK), BF16),
        jax.ShapeDtypeStruct((N_HEADS, 2, rows, 2 * DV), BF16),
        jax.ShapeDtypeStruct((rows, W_BRANCH), BF16),
        jax.ShapeDtypeStruct((rows, W_BRANCH), F32),
        jax.ShapeDtypeStruct((rows, W_BRANCH), F32),
        jax.ShapeDtypeStruct((rows, D_MODEL), F32),
        jax.ShapeDtypeStruct((rows, D_MODEL), F32),
        jax.ShapeDtypeStruct((POOL_BUF, dec_batch, W_BRANCH), F32),
    ]
    pseq = PREP_SEQS
    prows = pseq * DEC_SEQ
    a_t, bv_t, qd_t, oi_t, ga_t, ma_t, gb_t, pool_out = pl.pallas_call(
        functools.partial(_decode_prep_kernel, n_seq=pseq),
        grid=(dec_batch // pseq,),
        in_specs=[pl.BlockSpec((pseq, DEC_SEQ, D_MODEL), lambda i: (i, 0, 0)),
                  pl.BlockSpec((POOL_BUF, pseq, W_BRANCH), lambda i: (0, i, 0)),
                  _const_spec(ng.shape), _const_spec(win_b.shape), _const_spec(lbl.shape),
                  _const_spec(wpool_b.shape), _const_spec(ps.shape), _const_spec(wpb_b.shape)],
        out_specs=[
            pl.BlockSpec((N_HEADS, 2, prows, DK), lambda i: (0, 0, i, 0)),
            pl.BlockSpec((N_HEADS, 2, prows, 2 * DV), lambda i: (0, 0, i, 0)),
            pl.BlockSpec((prows, W_BRANCH), lambda i: (i, 0)),
            pl.BlockSpec((prows, W_BRANCH), lambda i: (i, 0)),
            pl.BlockSpec((prows, W_BRANCH), lambda i: (i, 0)),
            pl.BlockSpec((prows, D_MODEL), lambda i: (i, 0)),
            pl.BlockSpec((prows, D_MODEL), lambda i: (i, 0)),
            pl.BlockSpec((POOL_BUF, pseq, W_BRANCH), lambda i: (0, i, 0)),
        ],
        out_shape=prep_shapes,
        scratch_shapes=[
            pltpu.VMEM((prows, D_MODEL), F32),
            pltpu.VMEM((prows, D_IN), F32),
            pltpu.VMEM((len(POOL_WINDOWS), prows, G_B), F32),
            pltpu.VMEM((len(POOL_WINDOWS), prows, G_B), F32),
            pltpu.VMEM((prows, W_BRANCH), BF16),
        ],
        compiler_params=pltpu.CompilerParams(
            dimension_semantics=("arbitrary",), vmem_limit_bytes=VMEM_LIMIT_BYTES),
        name="hgrn2_pool_decode_prep",
    )(x_sample, pool_in, ng, win_b, lbl, wpool_b, ps, wpb_b)

    bb = BB_SAMPLE
    brows = bb * DEC_SEQ
    s_s, y_s = pl.pallas_call(
        _decode_state_kernel,
        grid=(dec_batch // bb,),
        in_specs=[
            pl.BlockSpec((N_HEADS, 2, brows, DK), lambda i: (0, 0, i, 0)),
            pl.BlockSpec((N_HEADS, 2, brows, 2 * DV), lambda i: (0, 0, i, 0)),
            pl.BlockSpec((brows, W_BRANCH), lambda i: (i, 0)),
            pl.BlockSpec((bb, N_HEADS, DK, DV), lambda i: (i, 0, 0, 0)),
            pl.BlockSpec((brows, W_BRANCH), lambda i: (i, 0)),
            pl.BlockSpec((brows, W_BRANCH), lambda i: (i, 0)),
            pl.BlockSpec((brows, D_MODEL), lambda i: (i, 0)),
            pl.BlockSpec((brows, D_MODEL), lambda i: (i, 0)),
            pl.BlockSpec((bb, DEC_SEQ, D_MODEL), lambda i: (i, 0, 0)),
            _const_spec(hg.shape), _const_spec(wpa_b.shape), _const_spec(wout_b.shape),
            _const_spec(fg.shape),
        ],
        out_specs=[
            pl.BlockSpec((bb, N_HEADS, DK, DV), lambda i: (i, 0, 0, 0)),
            pl.BlockSpec((bb, DEC_SEQ, D_MODEL), lambda i: (i, 0, 0)),
        ],
        out_shape=[
            jax.ShapeDtypeStruct((dec_batch, N_HEADS, DK, DV), F32),
            jax.ShapeDtypeStruct(x_sample.shape, F32),
        ],
        scratch_shapes=[
            pltpu.VMEM((brows, D_MODEL), F32),
            pltpu.VMEM((brows, W_BRANCH), BF16),
        ],
        compiler_params=pltpu.CompilerParams(
            dimension_semantics=("arbitrary",), vmem_limit_bytes=VMEM_LIMIT_BYTES),
        name="hgrn2_pool_decode_state",
    )(a_t, bv_t, qd_t, state_hgrn[0], oi_t, ga_t, ma_t, gb_t, x_sample, hg, wpa_b, wout_b, fg)

    p_s = jnp.transpose(pool_out, (1, 0, 2))
    return (y_p, y_s, s_p[None], p_p[None], s_s[None], p_s[None])
```

```python
import functools

import jax
import jax.numpy as jnp
from jax import lax
from jax.experimental import pallas as pl
from jax.experimental.pallas import tpu as pltpu

F32 = jnp.float32
BF16 = jnp.bfloat16

D_MODEL = 1024
W_BRANCH = 512
N_HEADS = 4
DK = 128
DV = 128
CHUNK = 32
POOL_WINDOWS = (2, 4, 8, 16)
G_B = 128
POOL_BUF = 15
PAST_LEN = 16384
EPS = 1e-6
D_IN = 4 * W_BRANCH + 2 * W_BRANCH + 2 * D_MODEL
OFF_Q, OFF_F, OFF_I, OFF_GA = 0, 512, 1024, 1536
OFF_U, OFF_GB, OFF_MA, OFF_MB = 2048, 2560, 3072, 4096

SUBLANES = 8
VMEM_LIMIT_BYTES = 56 * 1024 * 1024
DECODE_VMEM_LIMIT_BYTES = 58 * 1024 * 1024

TM_PROMPT = 512
SUB_TILE = 256
PREP_SEQS = 64
BB_SAMPLE = 16
DEC_SEQ = 4
PAIR_ROWS = 2 * DEC_SEQ


def _rms(x, g):
    ms = jnp.mean(x * x, axis=-1, keepdims=True)
    return x * lax.rsqrt(ms + EPS) * g


def _lower_bound(lbl_ref):
    l0 = lbl_ref[0:1, :]
    l1 = lbl_ref[1:2, :]
    m = jnp.maximum(l0, l1)
    e0 = jnp.exp(l0 - m)
    e1 = jnp.exp(l1 - m)
    return e0 / (e0 + e1)


def _in_proj(x, ng_ref, win_ref, z_ref):
    h = _rms(x, ng_ref[...]).astype(BF16)
    z_ref[...] = jnp.dot(h, win_ref[...], preferred_element_type=F32)


def _hgrn_inputs(z_ref, lb, hh):
    c = hh * DK
    q = z_ref[:, OFF_Q + c:OFF_Q + c + DK]
    f = z_ref[:, OFF_F + c:OFF_F + c + DK]
    v = z_ref[:, OFF_I + c:OFF_I + c + DK]
    lbh = lb[:, c:c + DK]
    fg = lbh + (1.0 - lbh) * jax.nn.sigmoid(f)
    logf = jnp.log(fg)
    k = 1.0 - fg
    qf = jax.nn.silu(q) * (DK ** -0.5)
    return qf, k, v, logf


def _group_cumsum(x, group):
    row = lax.broadcasted_iota(jnp.int32, x.shape, 0) & (group - 1)
    s = 1
    while s < group:
        x = x + jnp.where(row >= s, pltpu.roll(x, s, 0), 0.0)
        s *= 2
    return x


def _hgrn_epilogue(o, z_ref, hg_ref, hh, oa_ref):
    c = hh * DK
    ga = z_ref[:, OFF_GA + c:OFF_GA + c + DK]
    on = _rms(o, hg_ref[...]) * jax.nn.silu(ga)
    oa_ref[:, c:c + DK] = on.astype(BF16)


def _pool_epilogue(pooled, z_ref, g, wpool_ref, ps_ref, yb_ref):
    c = g * G_B
    mixed = jnp.dot(pooled.astype(BF16), wpool_ref[g], preferred_element_type=F32)
    gb = z_ref[:, OFF_GB + c:OFF_GB + c + G_B]
    yb = mixed * ps_ref[:, c:c + G_B] * jax.nn.silu(gb)
    yb_ref[:, c:c + G_B] = yb.astype(BF16)


def _weight_copies(hbm_refs, stage_ref, sem):
    win_hbm, wpa_hbm, wpool_hbm, wpb_hbm, wout_hbm = hbm_refs
    rows = stage_ref.shape[0]
    small = [
        (wpa_hbm, stage_ref.at[:, 0:D_MODEL]),
        (wpb_hbm, stage_ref.at[:, D_MODEL:2 * D_MODEL]),
        (wout_hbm.at[0:rows, :], stage_ref.at[:, 2 * D_MODEL:3 * D_MODEL]),
        (wout_hbm.at[rows:2 * rows, :], stage_ref.at[:, 3 * D_MODEL:4 * D_MODEL]),
        (wpool_hbm, stage_ref.at[:, 4 * D_MODEL:4 * D_MODEL + G_B]),
    ]
    small = [pltpu.make_async_copy(src, dst, sem.at[i]) for i, (src, dst) in enumerate(small)]
    blk = rows // 2

    def win(k):
        slot = k % 2
        return pltpu.make_async_copy(win_hbm.at[k * blk:(k + 1) * blk, :],
                                     stage_ref.at[slot * blk:(slot + 1) * blk, :],
                                     sem.at[len(small) + slot])
    return small, win, blk


def _weight_exports(vmem_refs, out_refs, sem):
    return [pltpu.make_async_copy(src, dst, sem.at[i])
            for i, (src, dst) in enumerate(zip(vmem_refs, out_refs))]


def _prompt_kernel(x_ref, ng_ref, win_hbm, lbl_ref, hg_ref, wpa_hbm, wpool_hbm, ps_ref,
                   wpb_hbm, wout_hbm, fg_ref,
                   y_ref, s_out_ref, p_out_ref, win_out, wpa_out, wpool_out, wpb_out, wout_out,
                   z_ref, st_ref, ext_ref, kx_ref, sn_ref, qs_ref, ks_ref, qd_ref, vb_ref,
                   oa_ref, yb_ref, win_ref, wpa_ref, wpool_ref, wpb_ref, wout_ref,
                   load_sem, export_sem):
    tm = TM_PROMPT
    sub = SUB_TILE
    n_sub = tm // sub
    nc = tm // CHUNK
    ncs = sub // CHUNK
    t = pl.program_id(1)
    first = t == 0
    hdr = 2 * SUBLANES
    bf16_weights = (win_ref, wpa_ref, wpool_ref, wpb_ref, wout_ref)
    exports = _weight_exports(bf16_weights, (win_out, wpa_out, wpool_out, wpb_out, wout_out),
                              export_sem)

    @pl.when((pl.program_id(0) == 0) & first)
    def _():
        st_ref[...] = jnp.zeros_like(st_ref)
        ext_ref[...] = jnp.zeros_like(ext_ref)
        kx_ref[...] = jnp.zeros_like(kx_ref)
        small, win, blk = _weight_copies((win_hbm, wpa_hbm, wpool_hbm, wpb_hbm, wout_hbm),
                                         z_ref, load_sem)
        for c in small:
            c.start()
        for c in small:
            c.wait()
        wpa_ref[...] = z_ref[:, 0:D_MODEL].astype(BF16)
        wpb_ref[...] = z_ref[:, D_MODEL:2 * D_MODEL].astype(BF16)
        wout_ref[0:tm, :] = z_ref[:, 2 * D_MODEL:3 * D_MODEL].astype(BF16)
        wout_ref[tm:2 * tm, :] = z_ref[:, 3 * D_MODEL:4 * D_MODEL].astype(BF16)
        for g in range(len(POOL_WINDOWS)):
            wpool_ref[g] = z_ref[g * G_B:(g + 1) * G_B,
                                 4 * D_MODEL:4 * D_MODEL + G_B].astype(BF16)
        n_blk = D_MODEL // blk
        win(0).start()
        win(1).start()
        for k in range(n_blk):
            slot = k % 2
            win(k).wait()
            win_ref[k * blk:(k + 1) * blk, :] = z_ref[slot * blk:(slot + 1) * blk, :].astype(BF16)
            if k + 2 < n_blk:
                win(k + 2).start()
        for c in exports:
            c.start()

    @pl.when((pl.program_id(0) == pl.num_programs(0) - 1) & (t == pl.num_programs(1) - 1))
    def _():
        for c in exports:
            c.wait()

    x = x_ref[0]
    _in_proj(x, ng_ref, win_ref, z_ref)
    lb = _lower_bound(lbl_ref)

    ri = lax.broadcasted_iota(jnp.int32, (sub, sub), 0)
    ci = lax.broadcasted_iota(jnp.int32, (sub, sub), 1)
    causal = ((ri // CHUNK) == (ci // CHUNK)) & (ci <= ri)

    heads = range(N_HEADS)
    units = [(hh, s) for hh in heads for s in range(n_sub)]

    def rows(s):
        return slice(s * sub, (s + 1) * sub)

    def cols(hh):
        return slice(hh * DK, (hh + 1) * DK)

    dec = []
    for hh in heads:
        qf, k, v, logf = _hgrn_inputs(z_ref, lb, hh)
        b = _group_cumsum(logf, CHUNK)
        b3 = b.reshape(nc, CHUNK, DK)
        ref = b3[:, CHUNK // 2:CHUNK // 2 + 1, :]
        bl = b3[:, CHUNK - 1:CHUNK, :]
        q3 = qf.reshape(nc, CHUNK, DK)
        k3 = k.reshape(nc, CHUNK, DK)
        qs_ref[:, cols(hh)] = (q3 * jnp.exp(b3 - ref)).reshape(tm, DK).astype(BF16)
        ks_ref[:, cols(hh)] = (k3 * jnp.exp(ref - b3)).reshape(tm, DK).astype(BF16)
        qd_ref[:, cols(hh)] = (q3 * jnp.exp(b3)).reshape(tm, DK).astype(BF16)
        vb_ref[:, cols(hh)] = v.astype(BF16)
        kd = (k3 * jnp.exp(bl - b3)).reshape(tm, DK).astype(BF16)
        dec.append(jnp.exp(bl))
        for c in range(nc):
            cc = c % ncs
            kx_ref[hh, c * CHUNK:(c + 1) * CHUNK, cc * DK:(cc + 1) * DK] = (
                kd[c * CHUNK:(c + 1) * CHUNK, :])

    ext_ref[0:hdr, :] = jnp.where(first, 0.0, ext_ref[tm:tm + hdr, :])
    ext_ref[hdr:hdr + tm, :] = z_ref[:, OFF_U:OFF_U + W_BRANCH]
    pos1 = t * tm + lax.broadcasted_iota(jnp.int32, (tm, 1), 0) + 1
    for g, w in enumerate(POOL_WINDOWS):
        c = g * G_B
        s = ext_ref[:, c:c + G_B]
        sh = 1
        while sh < w:
            s = s + pltpu.roll(s, sh, 0)
            sh *= 2
        inv = jnp.where(pos1 >= w, 1.0 / w, 1.0 / pos1.astype(F32))
        pooled = s[hdr:, :] * inv - z_ref[:, OFF_U + c:OFF_U + c + G_B]
        _pool_epilogue(pooled, z_ref, g, wpool_ref, ps_ref, yb_ref)
    p_out_ref[0] = ext_ref[hdr + tm - POOL_BUF:hdr + tm, :]

    sc = {(hh, s): lax.dot_general(qs_ref[rows(s), cols(hh)], ks_ref[rows(s), cols(hh)],
                                   (((1,), (1,)), ((), ())), preferred_element_type=F32)
          for hh, s in units}
    ut = {(hh, s): lax.dot_general(vb_ref[rows(s), cols(hh)], kx_ref[hh, rows(s), :],
                                   (((0,), (0,)), ((), ())), preferred_element_type=F32)
          for hh, s in units}
    o = {(hh, s): jnp.dot(jnp.where(causal, sc[hh, s], 0.0).astype(BF16),
                          vb_ref[rows(s), cols(hh)], preferred_element_type=F32)
         for hh, s in units}
    for hh in heads:
        st = jnp.where(first, 0.0, st_ref[hh])
        for c in range(nc):
            sn_ref[hh, c] = st.T.astype(BF16)
            st = dec[hh][c] * st + ut[hh, c // ncs][:, (c % ncs) * DK:(c % ncs + 1) * DK]
        st_ref[hh] = st
        s_out_ref[0, hh] = st.T
    wcol = D_MODEL // N_HEADS
    y_b = []
    for hh in heads:
        o_inter = [jnp.dot(qd_ref[c * CHUNK:(c + 1) * CHUNK, cols(hh)], sn_ref[hh, c],
                           preferred_element_type=F32) for c in range(nc)]
        y_b.append(jnp.dot(yb_ref[...], wpb_ref[:, hh * wcol:(hh + 1) * wcol],
                           preferred_element_type=F32))
        o_hh = jnp.concatenate([o[hh, s] for s in range(n_sub)], axis=0)
        _hgrn_epilogue(o_hh + jnp.concatenate(o_inter, axis=0), z_ref, hg_ref, hh, oa_ref)
    gated_b = jax.nn.sigmoid(z_ref[:, OFF_MB:OFF_MB + D_MODEL]) * jnp.concatenate(y_b, axis=1)

    y_a = jnp.dot(oa_ref[...], wpa_ref[...], preferred_element_type=F32)
    merged = jax.nn.sigmoid(z_ref[:, OFF_MA:OFF_MA + D_MODEL]) * y_a + gated_b
    out = x + jnp.dot(merged.astype(BF16), wout_ref[...], preferred_element_type=F32)
    y_ref[0] = _rms(out, fg_ref[...])


def _shift_rows(x, k):
    n = x.shape[0]
    return x if k % n == 0 else pltpu.roll(x, (-k) % n, 0)


def _group_bcast(x, j, group):
    t = lax.broadcasted_iota(jnp.int32, x.shape, 0) & (group - 1)
    out = _shift_rows(x, j - (group - 1))
    for tt in range(group - 2, -1, -1):
        out = jnp.where(t == tt, _shift_rows(x, j - tt), out)
    return out


def _decode_prep_kernel(x_ref, p_in_ref, ng_ref, win_ref, lbl_ref, wpool_ref, ps_ref, wpb_ref,
                        a_ref, bv_ref, qd_ref, oi_ref, ga_ref, ma_ref, gb_ref, p_out_ref,
                        xs_ref, z_ref, rw_ref, u_ref, yb_ref, *, n_seq):
    rows = n_seq * DEC_SEQ
    for i in range(n_seq):
        xs_ref[i * DEC_SEQ:(i + 1) * DEC_SEQ, :] = x_ref[i]
    _in_proj(xs_ref[...], ng_ref, win_ref, z_ref)
    lb = _lower_bound(lbl_ref)

    r = lax.broadcasted_iota(jnp.int32, (rows, DK), 0)
    t = r & (DEC_SEQ - 1)
    r8 = r & (PAIR_ROWS - 1)
    even = r8 < DEC_SEQ
    blk = SUB_TILE
    ri = lax.broadcasted_iota(jnp.int32, (blk, blk), 0)
    ci = lax.broadcasted_iota(jnp.int32, (blk, blk), 1)
    causal = ((ri // DEC_SEQ) == (ci // DEC_SEQ)) & (ci <= ri)

    for hh in range(N_HEADS):
        cs = slice(hh * DK, (hh + 1) * DK)
        qf, k, v, logf = _hgrn_inputs(z_ref, lb, hh)
        b = _group_cumsum(logf, DEC_SEQ)
        ref = _group_bcast(b, DEC_SEQ // 2, DEC_SEQ)
        bl = _group_bcast(b, DEC_SEQ - 1, DEC_SEQ)
        qs = (qf * jnp.exp(b - ref)).astype(BF16)
        ks = (k * jnp.exp(ref - b)).astype(BF16)
        kd = k * jnp.exp(bl - b)
        vb = v.astype(BF16)
        qd_ref[:, cs] = (qf * jnp.exp(b)).astype(BF16)
        for s in range(rows // blk):
            rs = slice(s * blk, (s + 1) * blk)
            sc = lax.dot_general(qs[rs], ks[rs], (((1,), (1,)), ((), ())),
                                 preferred_element_type=F32)
            oi_ref[rs, cs] = jnp.dot(jnp.where(causal, sc, 0.0).astype(BF16), vb[rs],
                                     preferred_element_type=F32)
        dec = jnp.exp(bl)
        d1 = dec.astype(BF16).astype(F32)
        d2 = (dec - d1).astype(BF16).astype(F32)
        d3 = (dec - d1 - d2).astype(BF16).astype(F32)
        tail = jnp.where(t == 0, d1, jnp.where(t == 1, d2, jnp.where(t == 2, d3, 0.0)))
        a_ref[hh, 0] = jnp.where(even, kd, _shift_rows(tail, -DEC_SEQ)).astype(BF16)
        a_ref[hh, 1] = jnp.where(even, _shift_rows(tail, DEC_SEQ), kd).astype(BF16)
        zero = jnp.zeros_like(vb)
        bv_ref[hh, 0] = jnp.where(even, vb, zero)
        bv_ref[hh, 1] = jnp.where(even, zero, vb)
        ga_ref[:, cs] = jax.nn.silu(z_ref[:, OFF_GA + hh * DK:OFF_GA + (hh + 1) * DK])

    for g, w in enumerate(POOL_WINDOWS):
        gc = slice(g * G_B, (g + 1) * G_B)
        acc = None
        suffix = {}
        for j in range(POOL_BUF - 1, -1, -1):
            e = p_in_ref[j, :, gc]
            acc = e if acc is None else acc + e
            suffix[j] = acc
        for tt in range(DEC_SEQ):
            j = POOL_BUF + 1 - w + tt
            rw_ref[g, pl.ds(tt, n_seq, stride=DEC_SEQ), :] = (
                suffix[j] if j < POOL_BUF else jnp.zeros((n_seq, G_B), F32))
        u = z_ref[:, OFF_U + g * G_B:OFF_U + (g + 1) * G_B]
        u_ref[g] = u
        cu = u
        sh = 1
        while sh < min(w, DEC_SEQ):
            cu = cu + jnp.where(t >= sh, _shift_rows(cu, -sh), 0.0)
            sh *= 2
        pooled = (rw_ref[g] + cu) * (1.0 / w) - u
        _pool_epilogue(pooled, z_ref, g, wpool_ref, ps_ref, yb_ref)
    gb_ref[...] = jax.nn.sigmoid(z_ref[:, OFF_MB:OFF_MB + D_MODEL]) * jnp.dot(
        yb_ref[...], wpb_ref[...], preferred_element_type=F32)
    ma_ref[...] = jax.nn.sigmoid(z_ref[:, OFF_MA:OFF_MA + D_MODEL])

    keep = POOL_BUF - DEC_SEQ
    for j in range(keep):
        p_out_ref[j] = p_in_ref[j + DEC_SEQ]
    for tt in range(DEC_SEQ):
        for g in range(len(POOL_WINDOWS)):
            p_out_ref[keep + tt, :, g * G_B:(g + 1) * G_B] = (
                u_ref[g, pl.ds(tt, n_seq, stride=DEC_SEQ), :])


def _decode_state_step(a_ref, bv_ref, qd_ref, s_in_ref, s_out_ref, oj_ref):
    n_pairs = BB_SAMPLE // 2
    r8 = lax.broadcasted_iota(jnp.int32, (PAIR_ROWS, DV), 0)
    even = r8 < DEC_SEQ
    ones = (jnp.where((r8 >= DEC_SEQ) & (r8 < DEC_SEQ + 3), 1.0, 0.0).astype(BF16),
            jnp.where(r8 < 3, 1.0, 0.0).astype(BF16))
    for hh in range(N_HEADS):
        for lp in range(n_pairs):
            rs = slice(lp * PAIR_ROWS, (lp + 1) * PAIR_ROWS)
            for e in range(2):
                rhs = jnp.concatenate([bv_ref[hh, e, rs, :], ones[e]], axis=1)
                upd = lax.dot_general(a_ref[hh, e, rs, :], rhs, (((0,), (0,)), ((), ())),
                                      preferred_element_type=F32)
                s_out_ref[2 * lp + e, hh] = (upd[:, DV:2 * DV] * s_in_ref[2 * lp + e, hh]
                                             + upd[:, 0:DV])
    for hh in range(N_HEADS):
        cs = slice(hh * DK, (hh + 1) * DK)
        for lp in range(n_pairs):
            rs = slice(lp * PAIR_ROWS, (lp + 1) * PAIR_ROWS)
            q8 = qd_ref[rs, cs]
            o0 = jnp.dot(q8, s_in_ref[2 * lp, hh].astype(BF16), preferred_element_type=F32)
            o1 = jnp.dot(q8, s_in_ref[2 * lp + 1, hh].astype(BF16), preferred_element_type=F32)
            oj_ref[rs, cs] = jnp.where(even, o0, o1)


def _decode_finish(oi_ref, oj_ref, ga_ref, ma_ref, gb_ref, xs_ref, hg_ref, wpa_ref, wout_ref,
                   fg_ref, y_ref, oa_ref, *, n_seq):
    for hh in range(N_HEADS):
        cs = slice(hh * DK, (hh + 1) * DK)
        o = oi_ref[:, cs] + oj_ref[:, cs]
        oa_ref[:, cs] = (_rms(o, hg_ref[...]) * ga_ref[:, cs]).astype(BF16)
    y_a = jnp.dot(oa_ref[...], wpa_ref[...], preferred_element_type=F32)
    merged = ma_ref[...] * y_a + gb_ref[...]
    out = xs_ref[...] + jnp.dot(merged.astype(BF16), wout_ref[...], preferred_element_type=F32)
    y = _rms(out, fg_ref[...])
    for i in range(n_seq):
        y_ref[i] = y[i * DEC_SEQ:(i + 1) * DEC_SEQ, :]


def _decode_kernel(xp_ref, p_in_ref, s_in_ref,
                   ng_ref, win_ref, lbl_ref, wpool_ref, ps_ref, wpb_ref,
                   hg_ref, wpa_ref, wout_ref, fg_ref,
                   s_out_ref, y_ref, p_out_ref,
                   a_all, bv_all, qd_all, oi_all, oj_all, ga_all, ma_all, gb_all, xs_all,
                   z_ref, rw_ref, u_ref, yb_ref, oa_ref, *, n_prep):
    s = pl.program_id(0)
    prows = PREP_SEQS * DEC_SEQ
    brows = BB_SAMPLE * DEC_SEQ
    per_finish = PREP_SEQS // BB_SAMPLE

    @pl.when(s < n_prep)
    def _():
        r = pl.ds(pl.multiple_of(s * prows, prows), prows)
        _decode_prep_kernel(xp_ref, p_in_ref, ng_ref, win_ref, lbl_ref, wpool_ref, ps_ref, wpb_ref,
                            a_all.at[:, :, r, :], bv_all.at[:, :, r, :], qd_all.at[r, :],
                            oi_all.at[r, :], ga_all.at[r, :], ma_all.at[r, :], gb_all.at[r, :],
                            p_out_ref, xs_all.at[r, :], z_ref, rw_ref, u_ref, yb_ref,
                            n_seq=PREP_SEQS)

    @pl.when(s >= n_prep)
    def _():
        i = s - n_prep
        r = pl.ds(pl.multiple_of(i * brows, brows), brows)
        _decode_state_step(a_all.at[:, :, r, :], bv_all.at[:, :, r, :], qd_all.at[r, :],
                           s_in_ref, s_out_ref, oj_all.at[r, :])

        @pl.when(lax.rem(i, per_finish) == per_finish - 1)
        def _():
            f = pl.ds(pl.multiple_of(lax.div(i, per_finish) * prows, prows), prows)
            _decode_finish(oi_all.at[f, :], oj_all.at[f, :], ga_all.at[f, :], ma_all.at[f, :],
                           gb_all.at[f, :], xs_all.at[f, :], hg_ref, wpa_ref, wout_ref, fg_ref,
                           y_ref, oa_ref, n_seq=PREP_SEQS)


def _const_spec(shape):
    n = len(shape)
    return pl.BlockSpec(shape, lambda *_: (0,) * n, pipeline_mode=pl.Buffered(1))


def kernel(x_prompt, x_sample, state_hgrn, state_pool, norm_g, w_in, lb_logits, hgrn_norm_g,
           w_proj_a, w_pool, pool_scale, w_proj_b, w_out, final_norm_g):
    batch, seq, _ = x_prompt.shape
    dec_batch, dec_seq, _ = x_sample.shape
    assert norm_g.shape[0] == 1 and lb_logits.shape[0] == 2, "single-layer decoder only"
    assert seq % TM_PROMPT == 0 and TM_PROMPT % SUB_TILE == 0 and dec_batch % BB_SAMPLE == 0
    assert dec_seq == DEC_SEQ and PAIR_ROWS == SUBLANES and BB_SAMPLE % 2 == 0
    assert dec_batch % PREP_SEQS == 0 and (PREP_SEQS * DEC_SEQ) % SUB_TILE == 0
    assert PREP_SEQS % BB_SAMPLE == 0
    assert PAST_LEN >= max(POOL_WINDOWS)

    n_groups = len(POOL_WINDOWS)
    tm = TM_PROMPT
    nt = seq // tm
    assert w_in.shape[1:] == (2 * tm, D_IN) and w_proj_a.shape[1] == tm, "weight staging layout"
    hbm = pl.BlockSpec(memory_space=pl.ANY)
    bf16_shapes = [(D_MODEL, D_IN), (W_BRANCH, D_MODEL), (n_groups, G_B, G_B),
                   (W_BRANCH, D_MODEL), (D_MODEL, D_MODEL)]
    y_p, s_p, p_p, win_b, wpa_b, wpool_b, wpb_b, wout_b = pl.pallas_call(
        _prompt_kernel,
        grid=(batch, nt),
        in_specs=[pl.BlockSpec((1, tm, D_MODEL), lambda b, t: (b, t, 0)),
                  _const_spec((1, D_MODEL)), hbm, _const_spec((2, W_BRANCH)), _const_spec((1, DV)),
                  hbm, hbm, _const_spec((1, W_BRANCH)), hbm, hbm, _const_spec((1, D_MODEL))],
        out_specs=[
            pl.BlockSpec((1, tm, D_MODEL), lambda b, t: (b, t, 0)),
            pl.BlockSpec((1, N_HEADS, DK, DV), lambda b, t: (b, 0, 0, 0)),
            pl.BlockSpec((1, POOL_BUF, W_BRANCH), lambda b, t: (b, 0, 0)),
        ] + [hbm] * len(bf16_shapes),
        out_shape=[
            jax.ShapeDtypeStruct((batch, seq, D_MODEL), F32),
            jax.ShapeDtypeStruct((batch, N_HEADS, DK, DV), F32),
            jax.ShapeDtypeStruct((batch, POOL_BUF, W_BRANCH), F32),
        ] + [jax.ShapeDtypeStruct(s, BF16) for s in bf16_shapes],
        scratch_shapes=[
            pltpu.VMEM((tm, D_IN), F32),
            pltpu.VMEM((N_HEADS, DV, DK), F32),
            pltpu.VMEM((tm + 2 * SUBLANES, W_BRANCH), F32),
            pltpu.VMEM((N_HEADS, tm, (SUB_TILE // CHUNK) * DK), BF16),
            pltpu.VMEM((N_HEADS, tm // CHUNK, DK, DV), BF16),
            pltpu.VMEM((tm, W_BRANCH), BF16),
            pltpu.VMEM((tm, W_BRANCH), BF16),
            pltpu.VMEM((tm, W_BRANCH), BF16),
            pltpu.VMEM((tm, W_BRANCH), BF16),
            pltpu.VMEM((tm, W_BRANCH), BF16),
            pltpu.VMEM((tm, W_BRANCH), BF16),
        ] + [pltpu.VMEM(s, BF16) for s in bf16_shapes] + [
            pltpu.SemaphoreType.DMA((7,)),
            pltpu.SemaphoreType.DMA((len(bf16_shapes),)),
        ],
        compiler_params=pltpu.CompilerParams(
            dimension_semantics=("arbitrary", "arbitrary"),
            vmem_limit_bytes=VMEM_LIMIT_BYTES),
        name="hgrn2_pool_prompt",
    )(x_prompt, norm_g, w_in[0], lb_logits, hgrn_norm_g, w_proj_a[0],
      w_pool[0].reshape(n_groups * G_B, G_B), pool_scale, w_proj_b[0], w_out[0],
      final_norm_g.reshape(1, D_MODEL))
    weights = (norm_g, win_b, lb_logits, hgrn_norm_g, wpa_b, wpool_b, pool_scale, wpb_b, wout_b,
               final_norm_g.reshape(1, D_MODEL))

    rows = dec_batch * DEC_SEQ
    pool_in = jnp.transpose(state_pool[0], (1, 0, 2))
    ng, win_b, lbl, hg, wpa_b, wpool_b, ps, wpb_b, wout_b, fg = weights
    pseq = PREP_SEQS
    prows = pseq * DEC_SEQ
    n_prep = dec_batch // pseq
    bb = BB_SAMPLE
    brows = bb * DEC_SEQ

    def prep_blk(s):
        return jnp.minimum(s, n_prep - 1)

    def state_blk(s):
        return jnp.maximum(s - n_prep, 0)

    s_s, y_s, pool_out = pl.pallas_call(
        functools.partial(_decode_kernel, n_prep=n_prep),
        grid=(n_prep + dec_batch // bb,),
        in_specs=[
            pl.BlockSpec((pseq, DEC_SEQ, D_MODEL), lambda s: (prep_blk(s), 0, 0)),
            pl.BlockSpec((POOL_BUF, pseq, W_BRANCH), lambda s: (0, prep_blk(s), 0),
                         pipeline_mode=pl.Buffered(1)),
            pl.BlockSpec((bb, N_HEADS, DK, DV), lambda s: (state_blk(s), 0, 0, 0)),
            _const_spec(ng.shape), _const_spec(win_b.shape), _const_spec(lbl.shape),
            _const_spec(wpool_b.shape), _const_spec(ps.shape), _const_spec(wpb_b.shape),
            _const_spec(hg.shape), _const_spec(wpa_b.shape), _const_spec(wout_b.shape),
            _const_spec(fg.shape),
        ],
        out_specs=[
            pl.BlockSpec((bb, N_HEADS, DK, DV), lambda s: (state_blk(s), 0, 0, 0)),
            pl.BlockSpec((pseq, DEC_SEQ, D_MODEL), lambda s: (state_blk(s) // (pseq // bb), 0, 0)),
            pl.BlockSpec((POOL_BUF, pseq, W_BRANCH), lambda s: (0, prep_blk(s), 0)),
        ],
        out_shape=[
            jax.ShapeDtypeStruct((dec_batch, N_HEADS, DK, DV), F32),
            jax.ShapeDtypeStruct(x_sample.shape, F32),
            jax.ShapeDtypeStruct((POOL_BUF, dec_batch, W_BRANCH), F32),
        ],
        scratch_shapes=[
            pltpu.VMEM((N_HEADS, 2, rows, DK), BF16),
            pltpu.VMEM((N_HEADS, 2, rows, DV), BF16),
            pltpu.VMEM((rows, W_BRANCH), BF16),
            pltpu.VMEM((rows, W_BRANCH), F32),
            pltpu.VMEM((rows, W_BRANCH), F32),
            pltpu.VMEM((rows, W_BRANCH), F32),
            pltpu.VMEM((rows, D_MODEL), F32),
            pltpu.VMEM((rows, D_MODEL), F32),
            pltpu.VMEM((rows, D_MODEL), F32),
            pltpu.VMEM((prows, D_IN), F32),
            pltpu.VMEM((len(POOL_WINDOWS), prows, G_B), F32),
            pltpu.VMEM((len(POOL_WINDOWS), prows, G_B), F32),
            pltpu.VMEM((prows, W_BRANCH), BF16),
            pltpu.VMEM((prows, W_BRANCH), BF16),
        ],
        compiler_params=pltpu.CompilerParams(
            dimension_semantics=("arbitrary",), vmem_limit_bytes=DECODE_VMEM_LIMIT_BYTES),
        name="hgrn2_pool_decode",
    )(x_sample, pool_in, state_hgrn[0], ng, win_b, lbl, wpool_b, ps, wpb_b,
      hg, wpa_b, wout_b, fg)

    p_s = jnp.transpose(pool_out, (1, 0, 2))
    return (y_p, y_s, s_p[None], p_p[None], s_s[None], p_s[None])
```

```python
import functools

import jax
import jax.numpy as jnp
from jax import lax
from jax.experimental import pallas as pl
from jax.experimental.pallas import tpu as pltpu

F32 = jnp.float32
BF16 = jnp.bfloat16

D_MODEL = 1024
W_BRANCH = 512
N_HEADS = 4
DK = 128
DV = 128
CHUNK = 32
POOL_WINDOWS = (2, 4, 8, 16)
G_B = 128
POOL_BUF = 15
PAST_LEN = 16384
EPS = 1e-6
D_IN = 4 * W_BRANCH + 2 * W_BRANCH + 2 * D_MODEL
OFF_Q, OFF_F, OFF_I, OFF_GA = 0, 512, 1024, 1536
OFF_U, OFF_GB, OFF_MA, OFF_MB = 2048, 2560, 3072, 4096

SUBLANES = 8
VMEM_LIMIT_BYTES = 56 * 1024 * 1024
DECODE_VMEM_LIMIT_BYTES = 60 * 1024 * 1024

TM_PROMPT = 512
SUB_TILE = 256
PREP_SEQS = 64
BB_SAMPLE = 16
DEC_SEQ = 4
PAIR_ROWS = 2 * DEC_SEQ
PREP_PARTS = 4


def _rms(x, g):
    ms = jnp.mean(x * x, axis=-1, keepdims=True)
    return x * lax.rsqrt(ms + EPS) * g


def _lower_bound(lbl_ref):
    l0 = lbl_ref[0:1, :]
    l1 = lbl_ref[1:2, :]
    m = jnp.maximum(l0, l1)
    e0 = jnp.exp(l0 - m)
    e1 = jnp.exp(l1 - m)
    return e0 / (e0 + e1)


def _in_proj(x, ng_ref, win_ref, z_ref, c0=0, c1=D_IN):
    h = _rms(x, ng_ref[...]).astype(BF16)
    z_ref[:, c0:c1] = jnp.dot(h, win_ref[:, c0:c1], preferred_element_type=F32)


def _hgrn_inputs(z_ref, lb, hh):
    c = hh * DK
    q = z_ref[:, OFF_Q + c:OFF_Q + c + DK]
    f = z_ref[:, OFF_F + c:OFF_F + c + DK]
    v = z_ref[:, OFF_I + c:OFF_I + c + DK]
    lbh = lb[:, c:c + DK]
    fg = lbh + (1.0 - lbh) * jax.nn.sigmoid(f)
    logf = jnp.log(fg)
    k = 1.0 - fg
    qf = jax.nn.silu(q) * (DK ** -0.5)
    return qf, k, v, logf


def _group_cumsum(x, group):
    row = lax.broadcasted_iota(jnp.int32, x.shape, 0) & (group - 1)
    s = 1
    while s < group:
        x = x + jnp.where(row >= s, pltpu.roll(x, s, 0), 0.0)
        s *= 2
    return x


def _hgrn_epilogue(o, z_ref, hg_ref, hh, oa_ref):
    c = hh * DK
    ga = z_ref[:, OFF_GA + c:OFF_GA + c + DK]
    on = _rms(o, hg_ref[...]) * jax.nn.silu(ga)
    oa_ref[:, c:c + DK] = on.astype(BF16)


def _pool_epilogue(pooled, z_ref, g, wpool_ref, ps_ref, yb_ref):
    c = g * G_B
    mixed = jnp.dot(pooled.astype(BF16), wpool_ref[g], preferred_element_type=F32)
    gb = z_ref[:, OFF_GB + c:OFF_GB + c + G_B]
    yb = mixed * ps_ref[:, c:c + G_B] * jax.nn.silu(gb)
    yb_ref[:, c:c + G_B] = yb.astype(BF16)


def _weight_copies(hbm_refs, stage_ref, sem):
    win_hbm, wpa_hbm, wpool_hbm, wpb_hbm, wout_hbm = hbm_refs
    rows = stage_ref.shape[0]
    small = [
        (wpa_hbm, stage_ref.at[:, 0:D_MODEL]),
        (wpb_hbm, stage_ref.at[:, D_MODEL:2 * D_MODEL]),
        (wout_hbm.at[0:rows, :], stage_ref.at[:, 2 * D_MODEL:3 * D_MODEL]),
        (wout_hbm.at[rows:2 * rows, :], stage_ref.at[:, 3 * D_MODEL:4 * D_MODEL]),
        (wpool_hbm, stage_ref.at[:, 4 * D_MODEL:4 * D_MODEL + G_B]),
    ]
    small = [pltpu.make_async_copy(src, dst, sem.at[i]) for i, (src, dst) in enumerate(small)]
    blk = rows // 2

    def win(k):
        slot = k % 2
        return pltpu.make_async_copy(win_hbm.at[k * blk:(k + 1) * blk, :],
                                     stage_ref.at[slot * blk:(slot + 1) * blk, :],
                                     sem.at[len(small) + slot])
    return small, win, blk


def _weight_exports(vmem_refs, out_refs, sem):
    return [pltpu.make_async_copy(src, dst, sem.at[i])
            for i, (src, dst) in enumerate(zip(vmem_refs, out_refs))]


def _prompt_kernel(x_ref, ng_ref, win_hbm, lbl_ref, hg_ref, wpa_hbm, wpool_hbm, ps_ref,
                   wpb_hbm, wout_hbm, fg_ref,
                   y_ref, s_out_ref, p_out_ref, win_out, wpa_out, wpool_out, wpb_out, wout_out,
                   z_ref, st_ref, ext_ref, kx_ref, sn_ref, qs_ref, ks_ref, qd_ref, vb_ref,
                   oa_ref, yb_ref, win_ref, wpa_ref, wpool_ref, wpb_ref, wout_ref,
                   load_sem, export_sem):
    tm = TM_PROMPT
    sub = SUB_TILE
    n_sub = tm // sub
    nc = tm // CHUNK
    ncs = sub // CHUNK
    t = pl.program_id(1)
    first = t == 0
    hdr = 2 * SUBLANES
    bf16_weights = (win_ref, wpa_ref, wpool_ref, wpb_ref, wout_ref)
    exports = _weight_exports(bf16_weights, (win_out, wpa_out, wpool_out, wpb_out, wout_out),
                              export_sem)

    @pl.when((pl.program_id(0) == 0) & first)
    def _():
        st_ref[...] = jnp.zeros_like(st_ref)
        ext_ref[...] = jnp.zeros_like(ext_ref)
        kx_ref[...] = jnp.zeros_like(kx_ref)
        small, win, blk = _weight_copies((win_hbm, wpa_hbm, wpool_hbm, wpb_hbm, wout_hbm),
                                         z_ref, load_sem)
        for c in small:
            c.start()
        for c in small:
            c.wait()
        wpa_ref[...] = z_ref[:, 0:D_MODEL].astype(BF16)
        wpb_ref[...] = z_ref[:, D_MODEL:2 * D_MODEL].astype(BF16)
        wout_ref[0:tm, :] = z_ref[:, 2 * D_MODEL:3 * D_MODEL].astype(BF16)
        wout_ref[tm:2 * tm, :] = z_ref[:, 3 * D_MODEL:4 * D_MODEL].astype(BF16)
        for g in range(len(POOL_WINDOWS)):
            wpool_ref[g] = z_ref[g * G_B:(g + 1) * G_B,
                                 4 * D_MODEL:4 * D_MODEL + G_B].astype(BF16)
        n_blk = D_MODEL // blk
        win(0).start()
        win(1).start()
        for k in range(n_blk):
            slot = k % 2
            win(k).wait()
            win_ref[k * blk:(k + 1) * blk, :] = z_ref[slot * blk:(slot + 1) * blk, :].astype(BF16)
            if k + 2 < n_blk:
                win(k + 2).start()
        for c in exports:
            c.start()

    @pl.when((pl.program_id(0) == pl.num_programs(0) - 1) & (t == pl.num_programs(1) - 1))
    def _():
        for c in exports:
            c.wait()

    x = x_ref[0]
    _in_proj(x, ng_ref, win_ref, z_ref)
    lb = _lower_bound(lbl_ref)

    ri = lax.broadcasted_iota(jnp.int32, (sub, sub), 0)
    ci = lax.broadcasted_iota(jnp.int32, (sub, sub), 1)
    causal = ((ri // CHUNK) == (ci // CHUNK)) & (ci <= ri)

    heads = range(N_HEADS)
    units = [(hh, s) for hh in heads for s in range(n_sub)]

    def rows(s):
        return slice(s * sub, (s + 1) * sub)

    def cols(hh):
        return slice(hh * DK, (hh + 1) * DK)

    dec = []
    for hh in heads:
        qf, k, v, logf = _hgrn_inputs(z_ref, lb, hh)
        b = _group_cumsum(logf, CHUNK)
        b3 = b.reshape(nc, CHUNK, DK)
        ref = b3[:, CHUNK // 2:CHUNK // 2 + 1, :]
        bl = b3[:, CHUNK - 1:CHUNK, :]
        q3 = qf.reshape(nc, CHUNK, DK)
        k3 = k.reshape(nc, CHUNK, DK)
        qs_ref[:, cols(hh)] = (q3 * jnp.exp(b3 - ref)).reshape(tm, DK).astype(BF16)
        ks_ref[:, cols(hh)] = (k3 * jnp.exp(ref - b3)).reshape(tm, DK).astype(BF16)
        qd_ref[:, cols(hh)] = (q3 * jnp.exp(b3)).reshape(tm, DK).astype(BF16)
        vb_ref[:, cols(hh)] = v.astype(BF16)
        kd = (k3 * jnp.exp(bl - b3)).reshape(tm, DK).astype(BF16)
        dec.append(jnp.exp(bl))
        for c in range(nc):
            cc = c % ncs
            kx_ref[hh, c * CHUNK:(c + 1) * CHUNK, cc * DK:(cc + 1) * DK] = (
                kd[c * CHUNK:(c + 1) * CHUNK, :])

    ext_ref[0:hdr, :] = jnp.where(first, 0.0, ext_ref[tm:tm + hdr, :])
    ext_ref[hdr:hdr + tm, :] = z_ref[:, OFF_U:OFF_U + W_BRANCH]
    pos1 = t * tm + lax.broadcasted_iota(jnp.int32, (tm, 1), 0) + 1
    for g, w in enumerate(POOL_WINDOWS):
        c = g * G_B
        s = ext_ref[:, c:c + G_B]
        sh = 1
        while sh < w:
            s = s + pltpu.roll(s, sh, 0)
            sh *= 2
        inv = jnp.where(pos1 >= w, 1.0 / w, 1.0 / pos1.astype(F32))
        pooled = s[hdr:, :] * inv - z_ref[:, OFF_U + c:OFF_U + c + G_B]
        _pool_epilogue(pooled, z_ref, g, wpool_ref, ps_ref, yb_ref)
    p_out_ref[0] = ext_ref[hdr + tm - POOL_BUF:hdr + tm, :]

    sc = {(hh, s): lax.dot_general(qs_ref[rows(s), cols(hh)], ks_ref[rows(s), cols(hh)],
                                   (((1,), (1,)), ((), ())), preferred_element_type=F32)
          for hh, s in units}
    ut = {(hh, s): lax.dot_general(vb_ref[rows(s), cols(hh)], kx_ref[hh, rows(s), :],
                                   (((0,), (0,)), ((), ())), preferred_element_type=F32)
          for hh, s in units}
    o = {(hh, s): jnp.dot(jnp.where(causal, sc[hh, s], 0.0).astype(BF16),
                          vb_ref[rows(s), cols(hh)], preferred_element_type=F32)
         for hh, s in units}
    for hh in heads:
        st = jnp.where(first, 0.0, st_ref[hh])
        for c in range(nc):
            sn_ref[hh, c] = st.T.astype(BF16)
            st = dec[hh][c] * st + ut[hh, c // ncs][:, (c % ncs) * DK:(c % ncs + 1) * DK]
        st_ref[hh] = st
        s_out_ref[0, hh] = st.T
    wcol = D_MODEL // N_HEADS
    y_b = []
    for hh in heads:
        o_inter = [jnp.dot(qd_ref[c * CHUNK:(c + 1) * CHUNK, cols(hh)], sn_ref[hh, c],
                           preferred_element_type=F32) for c in range(nc)]
        y_b.append(jnp.dot(yb_ref[...], wpb_ref[:, hh * wcol:(hh + 1) * wcol],
                           preferred_element_type=F32))
        o_hh = jnp.concatenate([o[hh, s] for s in range(n_sub)], axis=0)
        _hgrn_epilogue(o_hh + jnp.concatenate(o_inter, axis=0), z_ref, hg_ref, hh, oa_ref)
    gated_b = jax.nn.sigmoid(z_ref[:, OFF_MB:OFF_MB + D_MODEL]) * jnp.concatenate(y_b, axis=1)

    y_a = jnp.dot(oa_ref[...], wpa_ref[...], preferred_element_type=F32)
    merged = jax.nn.sigmoid(z_ref[:, OFF_MA:OFF_MA + D_MODEL]) * y_a + gated_b
    out = x + jnp.dot(merged.astype(BF16), wout_ref[...], preferred_element_type=F32)
    y_ref[0] = _rms(out, fg_ref[...])


def _shift_rows(x, k):
    n = x.shape[0]
    return x if k % n == 0 else pltpu.roll(x, (-k) % n, 0)


def _group_bcast(x, j, group):
    t = lax.broadcasted_iota(jnp.int32, x.shape, 0) & (group - 1)
    out = _shift_rows(x, j - (group - 1))
    for tt in range(group - 2, -1, -1):
        out = jnp.where(t == tt, _shift_rows(x, j - tt), out)
    return out


def _decode_prep_hgrn(z_ref, lbl_ref, a_ref, bv_ref, qd_ref, oi_ref, ga_ref):
    rows = z_ref.shape[0]
    lb = _lower_bound(lbl_ref)
    r = lax.broadcasted_iota(jnp.int32, (rows, DK), 0)
    t = r & (DEC_SEQ - 1)
    even = (r & (PAIR_ROWS - 1)) < DEC_SEQ
    blk = SUB_TILE
    ri = lax.broadcasted_iota(jnp.int32, (blk, blk), 0)
    ci = lax.broadcasted_iota(jnp.int32, (blk, blk), 1)
    causal = ((ri // DEC_SEQ) == (ci // DEC_SEQ)) & (ci <= ri)

    for hh in range(N_HEADS):
        cs = slice(hh * DK, (hh + 1) * DK)
        qf, k, v, logf = _hgrn_inputs(z_ref, lb, hh)
        b = _group_cumsum(logf, DEC_SEQ)
        ref = _group_bcast(b, DEC_SEQ // 2, DEC_SEQ)
        bl = _group_bcast(b, DEC_SEQ - 1, DEC_SEQ)
        qs = (qf * jnp.exp(b - ref)).astype(BF16)
        ks = (k * jnp.exp(ref - b)).astype(BF16)
        kd = k * jnp.exp(bl - b)
        vb = v.astype(BF16)
        qd_ref[:, cs] = (qf * jnp.exp(b)).astype(BF16)
        for s in range(rows // blk):
            rs = slice(s * blk, (s + 1) * blk)
            sc = lax.dot_general(qs[rs], ks[rs], (((1,), (1,)), ((), ())),
                                 preferred_element_type=F32)
            oi_ref[rs, cs] = jnp.dot(jnp.where(causal, sc, 0.0).astype(BF16), vb[rs],
                                     preferred_element_type=F32)
        dec = jnp.exp(bl)
        d1 = dec.astype(BF16).astype(F32)
        d2 = (dec - d1).astype(BF16).astype(F32)
        d3 = (dec - d1 - d2).astype(BF16).astype(F32)
        tail = jnp.where(t == 0, d1, jnp.where(t == 1, d2, jnp.where(t == 2, d3, 0.0)))
        a_ref[hh, 0] = jnp.where(even, kd, _shift_rows(tail, -DEC_SEQ)).astype(BF16)
        a_ref[hh, 1] = jnp.where(even, _shift_rows(tail, DEC_SEQ), kd).astype(BF16)
        zero = jnp.zeros_like(vb)
        bv_ref[hh, 0] = jnp.where(even, vb, zero)
        bv_ref[hh, 1] = jnp.where(even, zero, vb)
        ga_ref[:, cs] = jax.nn.silu(z_ref[:, OFF_GA + hh * DK:OFF_GA + (hh + 1) * DK])


def _decode_prep_pool(p_in_ref, z_ref, wpool_ref, ps_ref, wpb_ref, ma_ref, gb_ref, p_out_ref,
                      rw_ref, u_ref, yb_ref):
    rows = z_ref.shape[0]
    n_seq = rows // DEC_SEQ
    t = lax.broadcasted_iota(jnp.int32, (rows, G_B), 0) & (DEC_SEQ - 1)
    for g, w in enumerate(POOL_WINDOWS):
        gc = slice(g * G_B, (g + 1) * G_B)
        acc = None
        suffix = {}
        for j in range(POOL_BUF - 1, -1, -1):
            e = p_in_ref[j, :, gc]
            acc = e if acc is None else acc + e
            suffix[j] = acc
        for tt in range(DEC_SEQ):
            j = POOL_BUF + 1 - w + tt
            rw_ref[g, pl.ds(tt, n_seq, stride=DEC_SEQ), :] = (
                suffix[j] if j < POOL_BUF else jnp.zeros((n_seq, G_B), F32))
        u = z_ref[:, OFF_U + g * G_B:OFF_U + (g + 1) * G_B]
        u_ref[g] = u
        cu = u
        sh = 1
        while sh < min(w, DEC_SEQ):
            cu = cu + jnp.where(t >= sh, _shift_rows(cu, -sh), 0.0)
            sh *= 2
        pooled = (rw_ref[g] + cu) * (1.0 / w) - u
        _pool_epilogue(pooled, z_ref, g, wpool_ref, ps_ref, yb_ref)
    gb_ref[...] = jax.nn.sigmoid(z_ref[:, OFF_MB:OFF_MB + D_MODEL]) * jnp.dot(
        yb_ref[...], wpb_ref[...], preferred_element_type=F32)
    ma_ref[...] = jax.nn.sigmoid(z_ref[:, OFF_MA:OFF_MA + D_MODEL])

    keep = POOL_BUF - DEC_SEQ
    for j in range(keep):
        p_out_ref[j] = p_in_ref[j + DEC_SEQ]
    for tt in range(DEC_SEQ):
        for g in range(len(POOL_WINDOWS)):
            p_out_ref[keep + tt, :, g * G_B:(g + 1) * G_B] = (
                u_ref[g, pl.ds(tt, n_seq, stride=DEC_SEQ), :])


def _decode_prep(part, x_ref, p_in_ref, ng_ref, win_ref, lbl_ref, wpool_ref, ps_ref, wpb_ref,
                 a_ref, bv_ref, qd_ref, oi_ref, ga_ref, ma_ref, gb_ref, p_out_ref,
                 xs_ref, z_ref, rw_ref, u_ref, yb_ref):
    if part == 0:
        for i in range(x_ref.shape[0]):
            xs_ref[i * DEC_SEQ:(i + 1) * DEC_SEQ, :] = x_ref[i]
        _in_proj(xs_ref[...], ng_ref, win_ref, z_ref, 0, OFF_U)
    elif part == 1:
        _in_proj(xs_ref[...], ng_ref, win_ref, z_ref, OFF_U, D_IN)
    elif part == 2:
        _decode_prep_hgrn(z_ref, lbl_ref, a_ref, bv_ref, qd_ref, oi_ref, ga_ref)
    else:
        _decode_prep_pool(p_in_ref, z_ref, wpool_ref, ps_ref, wpb_ref, ma_ref, gb_ref, p_out_ref,
                          rw_ref, u_ref, yb_ref)


def _decode_state_step(a_ref, bv_ref, qd_ref, s_in_ref, s_out_ref, o_ref):
    n_pairs = BB_SAMPLE // 2
    r8 = lax.broadcasted_iota(jnp.int32, (PAIR_ROWS, DV), 0)
    even = r8 < DEC_SEQ
    ones = (jnp.where((r8 >= DEC_SEQ) & (r8 < DEC_SEQ + 3), 1.0, 0.0).astype(BF16),
            jnp.where(r8 < 3, 1.0, 0.0).astype(BF16))
    for hh in range(N_HEADS):
        for lp in range(n_pairs):
            rs = slice(lp * PAIR_ROWS, (lp + 1) * PAIR_ROWS)
            for e in range(2):
                rhs = jnp.concatenate([bv_ref[hh, e, rs, :], ones[e]], axis=1)
                upd = lax.dot_general(a_ref[hh, e, rs, :], rhs, (((0,), (0,)), ((), ())),
                                      preferred_element_type=F32)
                s_out_ref[2 * lp + e, hh] = (upd[:, DV:2 * DV] * s_in_ref[2 * lp + e, hh]
                                             + upd[:, 0:DV])
    for hh in range(N_HEADS):
        cs = slice(hh * DK, (hh + 1) * DK)
        for lp in range(n_pairs):
            rs = slice(lp * PAIR_ROWS, (lp + 1) * PAIR_ROWS)
            q8 = qd_ref[rs, cs]
            o0 = jnp.dot(q8, s_in_ref[2 * lp, hh].astype(BF16), preferred_element_type=F32)
            o1 = jnp.dot(q8, s_in_ref[2 * lp + 1, hh].astype(BF16), preferred_element_type=F32)
            o_ref[rs, cs] = o_ref[rs, cs] + jnp.where(even, o0, o1)


def _decode_finish(o_ref, ga_ref, ma_ref, gb_ref, xs_ref, hg_ref, wpa_ref, wout_ref,
                   fg_ref, y_ref, oa_ref, *, n_seq):
    for hh in range(N_HEADS):
        cs = slice(hh * DK, (hh + 1) * DK)
        oa_ref[:, cs] = (_rms(o_ref[:, cs], hg_ref[...]) * ga_ref[:, cs]).astype(BF16)
    y_a = jnp.dot(oa_ref[...], wpa_ref[...], preferred_element_type=F32)
    merged = ma_ref[...] * y_a + gb_ref[...]
    out = xs_ref[...] + jnp.dot(merged.astype(BF16), wout_ref[...], preferred_element_type=F32)
    y = _rms(out, fg_ref[...])
    for i in range(n_seq):
        y_ref[i] = y[i * DEC_SEQ:(i + 1) * DEC_SEQ, :]


def _decode_kernel(xp_ref, p_in_ref, s_in_ref,
                   ng_ref, win_ref, lbl_ref, wpool_ref, ps_ref, wpb_ref,
                   hg_ref, wpa_ref, wout_ref, fg_ref,
                   s_out_ref, y_ref, p_out_ref,
                   a_all, bv_all, qd_all, o_all, ga_all, ma_all, gb_all, xs_all,
                   z_ref, rw_ref, u_ref, yb_ref, oa_ref, *, n_prep):
    s = pl.program_id(0)
    prows = PREP_SEQS * DEC_SEQ
    brows = BB_SAMPLE * DEC_SEQ
    n_state = n_prep * PREP_PARTS

    def block_rows(blk):
        start = blk * prows
        return pl.ds(start if isinstance(start, int) else pl.multiple_of(start, prows), prows)

    def prep(part, blk):
        r = block_rows(blk)
        _decode_prep(part, xp_ref, p_in_ref, ng_ref, win_ref, lbl_ref, wpool_ref, ps_ref, wpb_ref,
                     a_all.at[:, :, r, :], bv_all.at[:, :, r, :], qd_all.at[r, :],
                     o_all.at[r, :], ga_all.at[r, :], ma_all.at[r, :], gb_all.at[r, :],
                     p_out_ref, xs_all.at[r, :], z_ref, rw_ref, u_ref, yb_ref)

    def finish(blk):
        f = block_rows(blk)
        _decode_finish(o_all.at[f, :], ga_all.at[f, :], ma_all.at[f, :], gb_all.at[f, :],
                       xs_all.at[f, :], hg_ref, wpa_ref, wout_ref, fg_ref, y_ref, oa_ref,
                       n_seq=PREP_SEQS)

    i = s - 1
    blk = lax.div(i, PREP_PARTS)
    part_now = lax.rem(i, PREP_PARTS)

    @pl.when(s == 0)
    def _():
        for part in range(PREP_PARTS):
            prep(part, 0)

    @pl.when(s > 0)
    def _():
        r = pl.ds(pl.multiple_of(i * brows, brows), brows)
        _decode_state_step(a_all.at[:, :, r, :], bv_all.at[:, :, r, :], qd_all.at[r, :],
                           s_in_ref, s_out_ref, o_all.at[r, :])

    for part in range(PREP_PARTS):
        @pl.when((part_now == part) & (blk + 1 < n_prep))
        def _():
            prep(part, blk + 1)

    @pl.when((part_now == 0) & (i > 0))
    def _():
        finish(blk - 1)

    @pl.when(i == n_state - 1)
    def _():
        finish(n_prep - 1)


def _const_spec(shape):
    n = len(shape)
    return pl.BlockSpec(shape, lambda *_: (0,) * n, pipeline_mode=pl.Buffered(1))


def kernel(x_prompt, x_sample, state_hgrn, state_pool, norm_g, w_in, lb_logits, hgrn_norm_g,
           w_proj_a, w_pool, pool_scale, w_proj_b, w_out, final_norm_g):
    batch, seq, _ = x_prompt.shape
    dec_batch, dec_seq, _ = x_sample.shape
    assert norm_g.shape[0] == 1 and lb_logits.shape[0] == 2, "single-layer decoder only"
    assert seq % TM_PROMPT == 0 and TM_PROMPT % SUB_TILE == 0 and dec_batch % BB_SAMPLE == 0
    assert dec_seq == DEC_SEQ and PAIR_ROWS == SUBLANES and BB_SAMPLE % 2 == 0
    assert dec_batch % PREP_SEQS == 0 and (PREP_SEQS * DEC_SEQ) % SUB_TILE == 0
    assert PREP_SEQS == PREP_PARTS * BB_SAMPLE
    assert PAST_LEN >= max(POOL_WINDOWS)

    n_groups = len(POOL_WINDOWS)
    tm = TM_PROMPT
    nt = seq // tm
    assert w_in.shape[1:] == (2 * tm, D_IN) and w_proj_a.shape[1] == tm, "weight staging layout"
    hbm = pl.BlockSpec(memory_space=pl.ANY)
    bf16_shapes = [(D_MODEL, D_IN), (W_BRANCH, D_MODEL), (n_groups, G_B, G_B),
                   (W_BRANCH, D_MODEL), (D_MODEL, D_MODEL)]
    y_p, s_p, p_p, win_b, wpa_b, wpool_b, wpb_b, wout_b = pl.pallas_call(
        _prompt_kernel,
        grid=(batch, nt),
        in_specs=[pl.BlockSpec((1, tm, D_MODEL), lambda b, t: (b, t, 0)),
                  _const_spec((1, D_MODEL)), hbm, _const_spec((2, W_BRANCH)), _const_spec((1, DV)),
                  hbm, hbm, _const_spec((1, W_BRANCH)), hbm, hbm, _const_spec((1, D_MODEL))],
        out_specs=[
            pl.BlockSpec((1, tm, D_MODEL), lambda b, t: (b, t, 0)),
            pl.BlockSpec((1, N_HEADS, DK, DV), lambda b, t: (b, 0, 0, 0)),
            pl.BlockSpec((1, POOL_BUF, W_BRANCH), lambda b, t: (b, 0, 0)),
        ] + [hbm] * len(bf16_shapes),
        out_shape=[
            jax.ShapeDtypeStruct((batch, seq, D_MODEL), F32),
            jax.ShapeDtypeStruct((batch, N_HEADS, DK, DV), F32),
            jax.ShapeDtypeStruct((batch, POOL_BUF, W_BRANCH), F32),
        ] + [jax.ShapeDtypeStruct(s, BF16) for s in bf16_shapes],
        scratch_shapes=[
            pltpu.VMEM((tm, D_IN), F32),
            pltpu.VMEM((N_HEADS, DV, DK), F32),
            pltpu.VMEM((tm + 2 * SUBLANES, W_BRANCH), F32),
            pltpu.VMEM((N_HEADS, tm, (SUB_TILE // CHUNK) * DK), BF16),
            pltpu.VMEM((N_HEADS, tm // CHUNK, DK, DV), BF16),
            pltpu.VMEM((tm, W_BRANCH), BF16),
            pltpu.VMEM((tm, W_BRANCH), BF16),
            pltpu.VMEM((tm, W_BRANCH), BF16),
            pltpu.VMEM((tm, W_BRANCH), BF16),
            pltpu.VMEM((tm, W_BRANCH), BF16),
            pltpu.VMEM((tm, W_BRANCH), BF16),
        ] + [pltpu.VMEM(s, BF16) for s in bf16_shapes] + [
            pltpu.SemaphoreType.DMA((7,)),
            pltpu.SemaphoreType.DMA((len(bf16_shapes),)),
        ],
        compiler_params=pltpu.CompilerParams(
            dimension_semantics=("arbitrary", "arbitrary"),
            vmem_limit_bytes=VMEM_LIMIT_BYTES),
        name="hgrn2_pool_prompt",
    )(x_prompt, norm_g, w_in[0], lb_logits, hgrn_norm_g, w_proj_a[0],
      w_pool[0].reshape(n_groups * G_B, G_B), pool_scale, w_proj_b[0], w_out[0],
      final_norm_g.reshape(1, D_MODEL))
    weights = (norm_g, win_b, lb_logits, hgrn_norm_g, wpa_b, wpool_b, pool_scale, wpb_b, wout_b,
               final_norm_g.reshape(1, D_MODEL))

    rows = dec_batch * DEC_SEQ
    pool_in = jnp.transpose(state_pool[0], (1, 0, 2))
    ng, win_b, lbl, hg, wpa_b, wpool_b, ps, wpb_b, wout_b, fg = weights
    pseq = PREP_SEQS
    prows = pseq * DEC_SEQ
    n_prep = dec_batch // pseq
    bb = BB_SAMPLE
    brows = bb * DEC_SEQ

    def prep_blk(s):
        return jnp.minimum((s + PREP_PARTS - 1) // PREP_PARTS, n_prep - 1)

    def state_blk(s):
        return jnp.maximum(s - 1, 0)

    def finish_blk(s):
        return jnp.minimum(jnp.maximum(s - 2, 0) // PREP_PARTS, n_prep - 1)

    s_s, y_s, pool_out = pl.pallas_call(
        functools.partial(_decode_kernel, n_prep=n_prep),
        grid=(1 + dec_batch // bb,),
        in_specs=[
            pl.BlockSpec((pseq, DEC_SEQ, D_MODEL), lambda s: (prep_blk(s), 0, 0)),
            pl.BlockSpec((POOL_BUF, pseq, W_BRANCH), lambda s: (0, prep_blk(s), 0),
                         pipeline_mode=pl.Buffered(1)),
            pl.BlockSpec((bb, N_HEADS, DK, DV), lambda s: (state_blk(s), 0, 0, 0)),
            _const_spec(ng.shape), _const_spec(win_b.shape), _const_spec(lbl.shape),
            _const_spec(wpool_b.shape), _const_spec(ps.shape), _const_spec(wpb_b.shape),
            _const_spec(hg.shape), _const_spec(wpa_b.shape), _const_spec(wout_b.shape),
            _const_spec(fg.shape),
        ],
        out_specs=[
            pl.BlockSpec((bb, N_HEADS, DK, DV), lambda s: (state_blk(s), 0, 0, 0)),
            pl.BlockSpec((pseq, DEC_SEQ, D_MODEL), lambda s: (finish_blk(s), 0, 0)),
            pl.BlockSpec((POOL_BUF, pseq, W_BRANCH), lambda s: (0, prep_blk(s), 0)),
        ],
        out_shape=[
            jax.ShapeDtypeStruct((dec_batch, N_HEADS, DK, DV), F32),
            jax.ShapeDtypeStruct(x_sample.shape, F32),
            jax.ShapeDtypeStruct((POOL_BUF, dec_batch, W_BRANCH), F32),
        ],
        scratch_shapes=[
            pltpu.VMEM((N_HEADS, 2, rows, DK), BF16),
            pltpu.VMEM((N_HEADS, 2, rows, DV), BF16),
            pltpu.VMEM((rows, W_BRANCH), BF16),
            pltpu.VMEM((rows, W_BRANCH), F32),
            pltpu.VMEM((rows, W_BRANCH), F32),
            pltpu.VMEM((rows, D_MODEL), F32),
            pltpu.VMEM((rows, D_MODEL), F32),
            pltpu.VMEM((rows, D_MODEL), F32),
            pltpu.VMEM((prows, D_IN), F32),
            pltpu.VMEM((len(POOL_WINDOWS), prows, G_B), F32),
            pltpu.VMEM((len(POOL_WINDOWS), prows, G_B), F32),
            pltpu.VMEM((prows, W_BRANCH), BF16),
            pltpu.VMEM((prows, W_BRANCH), BF16),
        ],
        compiler_params=pltpu.CompilerParams(
            dimension_semantics=("arbitrary",), vmem_limit_bytes=DECODE_VMEM_LIMIT_BYTES),
        name="hgrn2_pool_decode",
    )(x_sample, pool_in, state_hgrn[0], ng, win_b, lbl, wpool_b, ps, wpb_b,
      hg, wpa_b, wout_b, fg)

    p_s = jnp.transpose(pool_out, (1, 0, 2))
    return (y_p, y_s, s_p[None], p_p[None], s_s[None], p_s[None])
```

```python
import functools

import jax
import jax.numpy as jnp
from jax import lax
from jax.experimental import pallas as pl
from jax.experimental.pallas import tpu as pltpu

F32 = jnp.float32
BF16 = jnp.bfloat16

D_MODEL = 1024
W_BRANCH = 512
N_HEADS = 4
DK = 128
DV = 128
CHUNK = 32
POOL_WINDOWS = (2, 4, 8, 16)
G_B = 128
POOL_BUF = 15
PAST_LEN = 16384
EPS = 1e-6
D_IN = 4 * W_BRANCH + 2 * W_BRANCH + 2 * D_MODEL
OFF_Q, OFF_F, OFF_I, OFF_GA = 0, 512, 1024, 1536
OFF_U, OFF_GB, OFF_MA, OFF_MB = 2048, 2560, 3072, 4096

SUBLANES = 8
VMEM_LIMIT_BYTES = 56 * 1024 * 1024
DECODE_VMEM_LIMIT_BYTES = 60 * 1024 * 1024

TM_PROMPT = 512
SUB_TILE = 256
PREP_SEQS = 64
BB_SAMPLE = 16
DEC_SEQ = 4
PAIR_ROWS = 2 * DEC_SEQ
PREP_PARTS = 4


def _rms(x, g):
    ms = jnp.mean(x * x, axis=-1, keepdims=True)
    return x * lax.rsqrt(ms + EPS) * g


def _lower_bound(lbl_ref):
    l0 = lbl_ref[0:1, :]
    l1 = lbl_ref[1:2, :]
    m = jnp.maximum(l0, l1)
    e0 = jnp.exp(l0 - m)
    e1 = jnp.exp(l1 - m)
    return e0 / (e0 + e1)


def _in_proj(x, ng_ref, win_ref, z_ref, c0=0, c1=D_IN):
    h = _rms(x, ng_ref[...]).astype(BF16)
    z_ref[:, c0:c1] = jnp.dot(h, win_ref[:, c0:c1], preferred_element_type=F32)


def _hgrn_inputs(z_ref, lb, hh):
    c = hh * DK
    q = z_ref[:, OFF_Q + c:OFF_Q + c + DK]
    f = z_ref[:, OFF_F + c:OFF_F + c + DK]
    v = z_ref[:, OFF_I + c:OFF_I + c + DK]
    lbh = lb[:, c:c + DK]
    fg = lbh + (1.0 - lbh) * jax.nn.sigmoid(f)
    logf = jnp.log(fg)
    k = 1.0 - fg
    qf = jax.nn.silu(q) * (DK ** -0.5)
    return qf, k, v, logf


def _group_cumsum(x, group):
    row = lax.broadcasted_iota(jnp.int32, x.shape, 0) & (group - 1)
    s = 1
    while s < group:
        x = x + jnp.where(row >= s, pltpu.roll(x, s, 0), 0.0)
        s *= 2
    return x


def _hgrn_epilogue(o, z_ref, hg_ref, hh, oa_ref):
    c = hh * DK
    ga = z_ref[:, OFF_GA + c:OFF_GA + c + DK]
    on = _rms(o, hg_ref[...]) * jax.nn.silu(ga)
    oa_ref[:, c:c + DK] = on.astype(BF16)


def _pool_epilogue(pooled, z_ref, g, wpool_ref, ps_ref, yb_ref):
    c = g * G_B
    mixed = jnp.dot(pooled.astype(BF16), wpool_ref[g], preferred_element_type=F32)
    gb = z_ref[:, OFF_GB + c:OFF_GB + c + G_B]
    yb = mixed * ps_ref[:, c:c + G_B] * jax.nn.silu(gb)
    yb_ref[:, c:c + G_B] = yb.astype(BF16)


def _weight_copies(hbm_refs, stage_ref, sem):
    win_hbm, wpa_hbm, wpool_hbm, wpb_hbm, wout_hbm = hbm_refs
    rows = stage_ref.shape[0]
    small = [
        (wpa_hbm, stage_ref.at[:, 0:D_MODEL]),
        (wpb_hbm, stage_ref.at[:, D_MODEL:2 * D_MODEL]),
        (wout_hbm.at[0:rows, :], stage_ref.at[:, 2 * D_MODEL:3 * D_MODEL]),
        (wout_hbm.at[rows:2 * rows, :], stage_ref.at[:, 3 * D_MODEL:4 * D_MODEL]),
        (wpool_hbm, stage_ref.at[:, 4 * D_MODEL:4 * D_MODEL + G_B]),
    ]
    small = [pltpu.make_async_copy(src, dst, sem.at[i]) for i, (src, dst) in enumerate(small)]
    blk = rows // 2

    def win(k):
        slot = k % 2
        return pltpu.make_async_copy(win_hbm.at[k * blk:(k + 1) * blk, :],
                                     stage_ref.at[slot * blk:(slot + 1) * blk, :],
                                     sem.at[len(small) + slot])
    return small, win, blk


def _weight_exports(vmem_refs, out_refs, sem):
    return [pltpu.make_async_copy(src, dst, sem.at[i])
            for i, (src, dst) in enumerate(zip(vmem_refs, out_refs))]


def _prompt_kernel(x_ref, ng_ref, win_hbm, lbl_ref, hg_ref, wpa_hbm, wpool_hbm, ps_ref,
                   wpb_hbm, wout_hbm, fg_ref,
                   y_ref, s_out_ref, p_out_ref, win_out, wpa_out, wpool_out, wpb_out, wout_out,
                   z_ref, st_ref, ext_ref, kx_ref, sn_ref, qs_ref, ks_ref, qd_ref, vb_ref,
                   oa_ref, yb_ref, win_ref, wpa_ref, wpool_ref, wpb_ref, wout_ref,
                   load_sem, export_sem):
    tm = TM_PROMPT
    sub = SUB_TILE
    n_sub = tm // sub
    nc = tm // CHUNK
    ncs = sub // CHUNK
    t = pl.program_id(1)
    first = t == 0
    hdr = 2 * SUBLANES
    bf16_weights = (win_ref, wpa_ref, wpool_ref, wpb_ref, wout_ref)
    exports = _weight_exports(bf16_weights, (win_out, wpa_out, wpool_out, wpb_out, wout_out),
                              export_sem)

    @pl.when((pl.program_id(0) == 0) & first)
    def _():
        st_ref[...] = jnp.zeros_like(st_ref)
        ext_ref[...] = jnp.zeros_like(ext_ref)
        kx_ref[...] = jnp.zeros_like(kx_ref)
        small, win, blk = _weight_copies((win_hbm, wpa_hbm, wpool_hbm, wpb_hbm, wout_hbm),
                                         z_ref, load_sem)
        for c in small:
            c.start()
        for c in small:
            c.wait()
        wpa_ref[...] = z_ref[:, 0:D_MODEL].astype(BF16)
        wpb_ref[...] = z_ref[:, D_MODEL:2 * D_MODEL].astype(BF16)
        wout_ref[0:tm, :] = z_ref[:, 2 * D_MODEL:3 * D_MODEL].astype(BF16)
        wout_ref[tm:2 * tm, :] = z_ref[:, 3 * D_MODEL:4 * D_MODEL].astype(BF16)
        for g in range(len(POOL_WINDOWS)):
            wpool_ref[g] = z_ref[g * G_B:(g + 1) * G_B,
                                 4 * D_MODEL:4 * D_MODEL + G_B].astype(BF16)
        n_blk = D_MODEL // blk
        win(0).start()
        win(1).start()
        for k in range(n_blk):
            slot = k % 2
            win(k).wait()
            win_ref[k * blk:(k + 1) * blk, :] = z_ref[slot * blk:(slot + 1) * blk, :].astype(BF16)
            if k + 2 < n_blk:
                win(k + 2).start()
        for c in exports:
            c.start()

    @pl.when((pl.program_id(0) == pl.num_programs(0) - 1) & (t == pl.num_programs(1) - 1))
    def _():
        for c in exports:
            c.wait()

    x = x_ref[0]
    _in_proj(x, ng_ref, win_ref, z_ref)
    lb = _lower_bound(lbl_ref)

    ri = lax.broadcasted_iota(jnp.int32, (sub, sub), 0)
    ci = lax.broadcasted_iota(jnp.int32, (sub, sub), 1)
    causal = ((ri // CHUNK) == (ci // CHUNK)) & (ci <= ri)

    heads = range(N_HEADS)
    units = [(hh, s) for hh in heads for s in range(n_sub)]

    def rows(s):
        return slice(s * sub, (s + 1) * sub)

    def cols(hh):
        return slice(hh * DK, (hh + 1) * DK)

    dec = []
    for hh in heads:
        qf, k, v, logf = _hgrn_inputs(z_ref, lb, hh)
        b = _group_cumsum(logf, CHUNK)
        b3 = b.reshape(nc, CHUNK, DK)
        ref = b3[:, CHUNK // 2:CHUNK // 2 + 1, :]
        bl = b3[:, CHUNK - 1:CHUNK, :]
        q3 = qf.reshape(nc, CHUNK, DK)
        k3 = k.reshape(nc, CHUNK, DK)
        qs_ref[:, cols(hh)] = (q3 * jnp.exp(b3 - ref)).reshape(tm, DK).astype(BF16)
        ks_ref[:, cols(hh)] = (k3 * jnp.exp(ref - b3)).reshape(tm, DK).astype(BF16)
        qd_ref[:, cols(hh)] = (q3 * jnp.exp(b3)).reshape(tm, DK).astype(BF16)
        vb_ref[:, cols(hh)] = v.astype(BF16)
        kd = (k3 * jnp.exp(bl - b3)).reshape(tm, DK).astype(BF16)
        dec.append(jnp.exp(bl))
        for c in range(nc):
            cc = c % ncs
            kx_ref[hh, c * CHUNK:(c + 1) * CHUNK, cc * DK:(cc + 1) * DK] = (
                kd[c * CHUNK:(c + 1) * CHUNK, :])

    ext_ref[0:hdr, :] = jnp.where(first, 0.0, ext_ref[tm:tm + hdr, :])
    ext_ref[hdr:hdr + tm, :] = z_ref[:, OFF_U:OFF_U + W_BRANCH]
    pos1 = t * tm + lax.broadcasted_iota(jnp.int32, (tm, 1), 0) + 1
    for g, w in enumerate(POOL_WINDOWS):
        c = g * G_B
        s = ext_ref[:, c:c + G_B]
        sh = 1
        while sh < w:
            s = s + pltpu.roll(s, sh, 0)
            sh *= 2
        inv = jnp.where(pos1 >= w, 1.0 / w, 1.0 / pos1.astype(F32))
        pooled = s[hdr:, :] * inv - z_ref[:, OFF_U + c:OFF_U + c + G_B]
        _pool_epilogue(pooled, z_ref, g, wpool_ref, ps_ref, yb_ref)
    p_out_ref[0] = ext_ref[hdr + tm - POOL_BUF:hdr + tm, :]

    sc = {(hh, s): lax.dot_general(qs_ref[rows(s), cols(hh)], ks_ref[rows(s), cols(hh)],
                                   (((1,), (1,)), ((), ())), preferred_element_type=F32)
          for hh, s in units}
    ut = {(hh, s): lax.dot_general(vb_ref[rows(s), cols(hh)], kx_ref[hh, rows(s), :],
                                   (((0,), (0,)), ((), ())), preferred_element_type=F32)
          for hh, s in units}
    o = {(hh, s): jnp.dot(jnp.where(causal, sc[hh, s], 0.0).astype(BF16),
                          vb_ref[rows(s), cols(hh)], preferred_element_type=F32)
         for hh, s in units}
    for hh in heads:
        st = jnp.where(first, 0.0, st_ref[hh])
        for c in range(nc):
            sn_ref[hh, c] = st.T.astype(BF16)
            st = dec[hh][c] * st + ut[hh, c // ncs][:, (c % ncs) * DK:(c % ncs + 1) * DK]
        st_ref[hh] = st
        s_out_ref[0, hh] = st.T
    wcol = D_MODEL // N_HEADS
    y_b = []
    for hh in heads:
        o_inter = [jnp.dot(qd_ref[c * CHUNK:(c + 1) * CHUNK, cols(hh)], sn_ref[hh, c],
                           preferred_element_type=F32) for c in range(nc)]
        y_b.append(jnp.dot(yb_ref[...], wpb_ref[:, hh * wcol:(hh + 1) * wcol],
                           preferred_element_type=F32))
        o_hh = jnp.concatenate([o[hh, s] for s in range(n_sub)], axis=0)
        _hgrn_epilogue(o_hh + jnp.concatenate(o_inter, axis=0), z_ref, hg_ref, hh, oa_ref)
    gated_b = jax.nn.sigmoid(z_ref[:, OFF_MB:OFF_MB + D_MODEL]) * jnp.concatenate(y_b, axis=1)

    y_a = jnp.dot(oa_ref[...], wpa_ref[...], preferred_element_type=F32)
    merged = jax.nn.sigmoid(z_ref[:, OFF_MA:OFF_MA + D_MODEL]) * y_a + gated_b
    out = x + jnp.dot(merged.astype(BF16), wout_ref[...], preferred_element_type=F32)
    y_ref[0] = _rms(out, fg_ref[...])


def _shift_rows(x, k):
    n = x.shape[0]
    return x if k % n == 0 else pltpu.roll(x, (-k) % n, 0)


def _group_bcast(x, j, group):
    t = lax.broadcasted_iota(jnp.int32, x.shape, 0) & (group - 1)
    out = _shift_rows(x, j - (group - 1))
    for tt in range(group - 2, -1, -1):
        out = jnp.where(t == tt, _shift_rows(x, j - tt), out)
    return out


def _decode_prep_hgrn(heads, z_ref, lbl_ref, a_ref, bv_ref, qd_ref, oi_ref, ga_ref):
    rows = z_ref.shape[0]
    lb = _lower_bound(lbl_ref)
    r = lax.broadcasted_iota(jnp.int32, (rows, DK), 0)
    t = r & (DEC_SEQ - 1)
    even = (r & (PAIR_ROWS - 1)) < DEC_SEQ
    blk = SUB_TILE
    ri = lax.broadcasted_iota(jnp.int32, (blk, blk), 0)
    ci = lax.broadcasted_iota(jnp.int32, (blk, blk), 1)
    causal = ((ri // DEC_SEQ) == (ci // DEC_SEQ)) & (ci <= ri)

    for hh in heads:
        cs = slice(hh * DK, (hh + 1) * DK)
        qf, k, v, logf = _hgrn_inputs(z_ref, lb, hh)
        b = _group_cumsum(logf, DEC_SEQ)
        ref = _group_bcast(b, DEC_SEQ // 2, DEC_SEQ)
        bl = _group_bcast(b, DEC_SEQ - 1, DEC_SEQ)
        qs = (qf * jnp.exp(b - ref)).astype(BF16)
        ks = (k * jnp.exp(ref - b)).astype(BF16)
        kd = k * jnp.exp(bl - b)
        vb = v.astype(BF16)
        qd_ref[:, cs] = (qf * jnp.exp(b)).astype(BF16)
        for s in range(rows // blk):
            rs = slice(s * blk, (s + 1) * blk)
            sc = lax.dot_general(qs[rs], ks[rs], (((1,), (1,)), ((), ())),
                                 preferred_element_type=F32)
            oi_ref[rs, cs] = jnp.dot(jnp.where(causal, sc, 0.0).astype(BF16), vb[rs],
                                     preferred_element_type=F32)
        dec = jnp.exp(bl)
        d1 = dec.astype(BF16).astype(F32)
        d2 = (dec - d1).astype(BF16).astype(F32)
        d3 = (dec - d1 - d2).astype(BF16).astype(F32)
        tail = jnp.where(t == 0, d1, jnp.where(t == 1, d2, jnp.where(t == 2, d3, 0.0)))
        a_ref[hh, 0] = jnp.where(even, kd, _shift_rows(tail, -DEC_SEQ)).astype(BF16)
        a_ref[hh, 1] = jnp.where(even, _shift_rows(tail, DEC_SEQ), kd).astype(BF16)
        zero = jnp.zeros_like(vb)
        bv_ref[hh, 0] = jnp.where(even, vb, zero)
        bv_ref[hh, 1] = jnp.where(even, zero, vb)
        ga_ref[:, cs] = jax.nn.silu(z_ref[:, OFF_GA + hh * DK:OFF_GA + (hh + 1) * DK])


def _decode_prep_history(p_in_ref, rw_ref, p_out_ref):
    n_seq = p_in_ref.shape[1]
    for g, w in enumerate(POOL_WINDOWS):
        gc = slice(g * G_B, (g + 1) * G_B)
        acc = None
        suffix = {}
        for j in range(POOL_BUF - 1, -1, -1):
            e = p_in_ref[j, :, gc]
            acc = e if acc is None else acc + e
            suffix[j] = acc
        for tt in range(DEC_SEQ):
            j = POOL_BUF + 1 - w + tt
            rw_ref[g, pl.ds(tt, n_seq, stride=DEC_SEQ), :] = (
                suffix[j] if j < POOL_BUF else jnp.zeros((n_seq, G_B), F32))
    for j in range(POOL_BUF - DEC_SEQ):
        p_out_ref[j] = p_in_ref[j + DEC_SEQ]


def _decode_prep_pool(z_ref, wpool_ref, ps_ref, wpb_ref, ma_ref, gb_ref, p_out_ref,
                      rw_ref, u_ref, yb_ref):
    rows = z_ref.shape[0]
    n_seq = rows // DEC_SEQ
    t = lax.broadcasted_iota(jnp.int32, (rows, G_B), 0) & (DEC_SEQ - 1)
    for g, w in enumerate(POOL_WINDOWS):
        u = z_ref[:, OFF_U + g * G_B:OFF_U + (g + 1) * G_B]
        u_ref[g] = u
        cu = u
        sh = 1
        while sh < min(w, DEC_SEQ):
            cu = cu + jnp.where(t >= sh, _shift_rows(cu, -sh), 0.0)
            sh *= 2
        pooled = (rw_ref[g] + cu) * (1.0 / w) - u
        _pool_epilogue(pooled, z_ref, g, wpool_ref, ps_ref, yb_ref)
    gb_ref[...] = jax.nn.sigmoid(z_ref[:, OFF_MB:OFF_MB + D_MODEL]) * jnp.dot(
        yb_ref[...], wpb_ref[...], preferred_element_type=F32)
    ma_ref[...] = jax.nn.sigmoid(z_ref[:, OFF_MA:OFF_MA + D_MODEL])
    for tt in range(DEC_SEQ):
        for g in range(len(POOL_WINDOWS)):
            p_out_ref[POOL_BUF - DEC_SEQ + tt, :, g * G_B:(g + 1) * G_B] = (
                u_ref[g, pl.ds(tt, n_seq, stride=DEC_SEQ), :])


def _decode_prep(part, x_ref, p_in_ref, ng_ref, win_ref, lbl_ref, wpool_ref, ps_ref, wpb_ref,
                 a_ref, bv_ref, qd_ref, oi_ref, ga_ref, ma_ref, gb_ref, p_out_ref,
                 xs_ref, z_ref, rw_ref, u_ref, yb_ref):
    hgrn = functools.partial(_decode_prep_hgrn, z_ref=z_ref, lbl_ref=lbl_ref, a_ref=a_ref,
                             bv_ref=bv_ref, qd_ref=qd_ref, oi_ref=oi_ref, ga_ref=ga_ref)
    if part == 0:
        for i in range(x_ref.shape[0]):
            xs_ref[i * DEC_SEQ:(i + 1) * DEC_SEQ, :] = x_ref[i]
        _in_proj(xs_ref[...], ng_ref, win_ref, z_ref, 0, OFF_U)
        _decode_prep_history(p_in_ref, rw_ref, p_out_ref)
    elif part == 1:
        _in_proj(xs_ref[...], ng_ref, win_ref, z_ref, OFF_U, D_IN)
    elif part == 2:
        hgrn(range(0, N_HEADS - 1))
    else:
        hgrn(range(N_HEADS - 1, N_HEADS))
        _decode_prep_pool(z_ref, wpool_ref, ps_ref, wpb_ref, ma_ref, gb_ref, p_out_ref,
                          rw_ref, u_ref, yb_ref)


def _decode_state_step(a_ref, bv_ref, qd_ref, s_in_ref, s_out_ref, o_ref):
    n_pairs = BB_SAMPLE // 2
    r8 = lax.broadcasted_iota(jnp.int32, (PAIR_ROWS, DV), 0)
    even = r8 < DEC_SEQ
    ones = (jnp.where((r8 >= DEC_SEQ) & (r8 < DEC_SEQ + 3), 1.0, 0.0).astype(BF16),
            jnp.where(r8 < 3, 1.0, 0.0).astype(BF16))
    for hh in range(N_HEADS):
        for lp in range(n_pairs):
            rs = slice(lp * PAIR_ROWS, (lp + 1) * PAIR_ROWS)
            for e in range(2):
                rhs = jnp.concatenate([bv_ref[hh, e, rs, :], ones[e]], axis=1)
                upd = lax.dot_general(a_ref[hh, e, rs, :], rhs, (((0,), (0,)), ((), ())),
                                      preferred_element_type=F32)
                s_out_ref[2 * lp + e, hh] = (upd[:, DV:2 * DV] * s_in_ref[2 * lp + e, hh]
                                             + upd[:, 0:DV])
    for hh in range(N_HEADS):
        cs = slice(hh * DK, (hh + 1) * DK)
        for lp in range(n_pairs):
            rs = slice(lp * PAIR_ROWS, (lp + 1) * PAIR_ROWS)
            q8 = qd_ref[rs, cs]
            o0 = jnp.dot(q8, s_in_ref[2 * lp, hh].astype(BF16), preferred_element_type=F32)
            o1 = jnp.dot(q8, s_in_ref[2 * lp + 1, hh].astype(BF16), preferred_element_type=F32)
            o_ref[rs, cs] = o_ref[rs, cs] + jnp.where(even, o0, o1)


def _decode_finish(o_ref, ga_ref, ma_ref, gb_ref, xs_ref, hg_ref, wpa_ref, wout_ref,
                   fg_ref, y_ref, oa_ref, *, n_seq):
    for hh in range(N_HEADS):
        cs = slice(hh * DK, (hh + 1) * DK)
        oa_ref[:, cs] = (_rms(o_ref[:, cs], hg_ref[...]) * ga_ref[:, cs]).astype(BF16)
    y_a = jnp.dot(oa_ref[...], wpa_ref[...], preferred_element_type=F32)
    merged = ma_ref[...] * y_a + gb_ref[...]
    out = xs_ref[...] + jnp.dot(merged.astype(BF16), wout_ref[...], preferred_element_type=F32)
    y = _rms(out, fg_ref[...])
    for i in range(n_seq):
        y_ref[i] = y[i * DEC_SEQ:(i + 1) * DEC_SEQ, :]


def _decode_kernel(xp_ref, p_in_ref, s_in_ref,
                   ng_ref, win_ref, lbl_ref, wpool_ref, ps_ref, wpb_ref,
                   hg_ref, wpa_ref, wout_ref, fg_ref,
                   s_out_ref, y_ref, p_out_ref,
                   a_all, bv_all, qd_all, o_all, ga_all, ma_all, gb_all, xs_all,
                   z_ref, rw_ref, u_ref, yb_ref, oa_ref, *, n_prep):
    s = pl.program_id(0)
    prows = PREP_SEQS * DEC_SEQ
    brows = BB_SAMPLE * DEC_SEQ
    n_state = n_prep * PREP_PARTS

    def block_rows(blk):
        start = blk * prows
        return pl.ds(start if isinstance(start, int) else pl.multiple_of(start, prows), prows)

    def prep(part, blk):
        r = block_rows(blk)
        _decode_prep(part, xp_ref, p_in_ref, ng_ref, win_ref, lbl_ref, wpool_ref, ps_ref, wpb_ref,
                     a_all.at[:, :, r, :], bv_all.at[:, :, r, :], qd_all.at[r, :],
                     o_all.at[r, :], ga_all.at[r, :], ma_all.at[r, :], gb_all.at[r, :],
                     p_out_ref, xs_all.at[r, :], z_ref, rw_ref, u_ref, yb_ref)

    def finish(blk):
        f = block_rows(blk)
        _decode_finish(o_all.at[f, :], ga_all.at[f, :], ma_all.at[f, :], gb_all.at[f, :],
                       xs_all.at[f, :], hg_ref, wpa_ref, wout_ref, fg_ref, y_ref, oa_ref,
                       n_seq=PREP_SEQS)

    i = s - 1
    blk = lax.div(i, PREP_PARTS)
    part_now = lax.rem(i, PREP_PARTS)

    @pl.when(s == 0)
    def _():
        for part in range(PREP_PARTS):
            prep(part, 0)

    @pl.when(s > 0)
    def _():
        r = pl.ds(pl.multiple_of(i * brows, brows), brows)
        _decode_state_step(a_all.at[:, :, r, :], bv_all.at[:, :, r, :], qd_all.at[r, :],
                           s_in_ref, s_out_ref, o_all.at[r, :])

    for part in range(PREP_PARTS):
        @pl.when((part_now == part) & (blk + 1 < n_prep))
        def _():
            prep(part, blk + 1)

    @pl.when((part_now == 0) & (i > 0))
    def _():
        finish(blk - 1)

    @pl.when(i == n_state - 1)
    def _():
        finish(n_prep - 1)


def _const_spec(shape):
    n = len(shape)
    return pl.BlockSpec(shape, lambda *_: (0,) * n, pipeline_mode=pl.Buffered(1))


def kernel(x_prompt, x_sample, state_hgrn, state_pool, norm_g, w_in, lb_logits, hgrn_norm_g,
           w_proj_a, w_pool, pool_scale, w_proj_b, w_out, final_norm_g):
    batch, seq, _ = x_prompt.shape
    dec_batch, dec_seq, _ = x_sample.shape
    assert norm_g.shape[0] == 1 and lb_logits.shape[0] == 2, "single-layer decoder only"
    assert seq % TM_PROMPT == 0 and TM_PROMPT % SUB_TILE == 0 and dec_batch % BB_SAMPLE == 0
    assert dec_seq == DEC_SEQ and PAIR_ROWS == SUBLANES and BB_SAMPLE % 2 == 0
    assert dec_batch % PREP_SEQS == 0 and (PREP_SEQS * DEC_SEQ) % SUB_TILE == 0
    assert PREP_SEQS == PREP_PARTS * BB_SAMPLE
    assert PAST_LEN >= max(POOL_WINDOWS)

    n_groups = len(POOL_WINDOWS)
    tm = TM_PROMPT
    nt = seq // tm
    assert w_in.shape[1:] == (2 * tm, D_IN) and w_proj_a.shape[1] == tm, "weight staging layout"
    hbm = pl.BlockSpec(memory_space=pl.ANY)
    bf16_shapes = [(D_MODEL, D_IN), (W_BRANCH, D_MODEL), (n_groups, G_B, G_B),
                   (W_BRANCH, D_MODEL), (D_MODEL, D_MODEL)]
    y_p, s_p, p_p, win_b, wpa_b, wpool_b, wpb_b, wout_b = pl.pallas_call(
        _prompt_kernel,
        grid=(batch, nt),
        in_specs=[pl.BlockSpec((1, tm, D_MODEL), lambda b, t: (b, t, 0)),
                  _const_spec((1, D_MODEL)), hbm, _const_spec((2, W_BRANCH)), _const_spec((1, DV)),
                  hbm, hbm, _const_spec((1, W_BRANCH)), hbm, hbm, _const_spec((1, D_MODEL))],
        out_specs=[
            pl.BlockSpec((1, tm, D_MODEL), lambda b, t: (b, t, 0)),
            pl.BlockSpec((1, N_HEADS, DK, DV), lambda b, t: (b, 0, 0, 0)),
            pl.BlockSpec((1, POOL_BUF, W_BRANCH), lambda b, t: (b, 0, 0)),
        ] + [hbm] * len(bf16_shapes),
        out_shape=[
            jax.ShapeDtypeStruct((batch, seq, D_MODEL), F32),
            jax.ShapeDtypeStruct((batch, N_HEADS, DK, DV), F32),
            jax.ShapeDtypeStruct((batch, POOL_BUF, W_BRANCH), F32),
        ] + [jax.ShapeDtypeStruct(s, BF16) for s in bf16_shapes],
        scratch_shapes=[
            pltpu.VMEM((tm, D_IN), F32),
            pltpu.VMEM((N_HEADS, DV, DK), F32),
            pltpu.VMEM((tm + 2 * SUBLANES, W_BRANCH), F32),
            pltpu.VMEM((N_HEADS, tm, (SUB_TILE // CHUNK) * DK), BF16),
            pltpu.VMEM((N_HEADS, tm // CHUNK, DK, DV), BF16),
            pltpu.VMEM((tm, W_BRANCH), BF16),
            pltpu.VMEM((tm, W_BRANCH), BF16),
            pltpu.VMEM((tm, W_BRANCH), BF16),
            pltpu.VMEM((tm, W_BRANCH), BF16),
            pltpu.VMEM((tm, W_BRANCH), BF16),
            pltpu.VMEM((tm, W_BRANCH), BF16),
        ] + [pltpu.VMEM(s, BF16) for s in bf16_shapes] + [
            pltpu.SemaphoreType.DMA((7,)),
            pltpu.SemaphoreType.DMA((len(bf16_shapes),)),
        ],
        compiler_params=pltpu.CompilerParams(
            dimension_semantics=("arbitrary", "arbitrary"),
            vmem_limit_bytes=VMEM_LIMIT_BYTES),
        name="hgrn2_pool_prompt",
    )(x_prompt, norm_g, w_in[0], lb_logits, hgrn_norm_g, w_proj_a[0],
      w_pool[0].reshape(n_groups * G_B, G_B), pool_scale, w_proj_b[0], w_out[0],
      final_norm_g.reshape(1, D_MODEL))
    weights = (norm_g, win_b, lb_logits, hgrn_norm_g, wpa_b, wpool_b, pool_scale, wpb_b, wout_b,
               final_norm_g.reshape(1, D_MODEL))

    rows = dec_batch * DEC_SEQ
    pool_in = jnp.transpose(state_pool[0], (1, 0, 2))
    ng, win_b, lbl, hg, wpa_b, wpool_b, ps, wpb_b, wout_b, fg = weights
    pseq = PREP_SEQS
    prows = pseq * DEC_SEQ
    n_prep = dec_batch // pseq
    bb = BB_SAMPLE
    brows = bb * DEC_SEQ

    def prep_blk(s):
        return jnp.minimum((s + PREP_PARTS - 1) // PREP_PARTS, n_prep - 1)

    def state_blk(s):
        return jnp.maximum(s - 1, 0)

    def finish_blk(s):
        return jnp.minimum(jnp.maximum(s - 2, 0) // PREP_PARTS, n_prep - 1)

    s_s, y_s, pool_out = pl.pallas_call(
        functools.partial(_decode_kernel, n_prep=n_prep),
        grid=(1 + dec_batch // bb,),
        in_specs=[
            pl.BlockSpec((pseq, DEC_SEQ, D_MODEL), lambda s: (prep_blk(s), 0, 0)),
            pl.BlockSpec((POOL_BUF, pseq, W_BRANCH), lambda s: (0, prep_blk(s), 0),
                         pipeline_mode=pl.Buffered(1)),
            pl.BlockSpec((bb, N_HEADS, DK, DV), lambda s: (state_blk(s), 0, 0, 0)),
            _const_spec(ng.shape), _const_spec(win_b.shape), _const_spec(lbl.shape),
            _const_spec(wpool_b.shape), _const_spec(ps.shape), _const_spec(wpb_b.shape),
            _const_spec(hg.shape), _const_spec(wpa_b.shape), _const_spec(wout_b.shape),
            _const_spec(fg.shape),
        ],
        out_specs=[
            pl.BlockSpec((bb, N_HEADS, DK, DV), lambda s: (state_blk(s), 0, 0, 0)),
            pl.BlockSpec((pseq, DEC_SEQ, D_MODEL), lambda s: (finish_blk(s), 0, 0)),
            pl.BlockSpec((POOL_BUF, pseq, W_BRANCH), lambda s: (0, prep_blk(s), 0)),
        ],
        out_shape=[
            jax.ShapeDtypeStruct((dec_batch, N_HEADS, DK, DV), F32),
            jax.ShapeDtypeStruct(x_sample.shape, F32),
            jax.ShapeDtypeStruct((POOL_BUF, dec_batch, W_BRANCH), F32),
        ],
        scratch_shapes=[
            pltpu.VMEM((N_HEADS, 2, rows, DK), BF16),
            pltpu.VMEM((N_HEADS, 2, rows, DV), BF16),
            pltpu.VMEM((rows, W_BRANCH), BF16),
            pltpu.VMEM((rows, W_BRANCH), F32),
            pltpu.VMEM((rows, W_BRANCH), F32),
            pltpu.VMEM((rows, D_MODEL), F32),
            pltpu.VMEM((rows, D_MODEL), F32),
            pltpu.VMEM((rows, D_MODEL), F32),
            pltpu.VMEM((prows, D_IN), F32),
            pltpu.VMEM((len(POOL_WINDOWS), prows, G_B), F32),
            pltpu.VMEM((len(POOL_WINDOWS), prows, G_B), F32),
            pltpu.VMEM((prows, W_BRANCH), BF16),
            pltpu.VMEM((prows, W_BRANCH), BF16),
        ],
        compiler_params=pltpu.CompilerParams(
            dimension_semantics=("arbitrary",), vmem_limit_bytes=DECODE_VMEM_LIMIT_BYTES),
        name="hgrn2_pool_decode",
    )(x_sample, pool_in, state_hgrn[0], ng, win_b, lbl, wpool_b, ps, wpb_b,
      hg, wpa_b, wout_b, fg)

    p_s = jnp.transpose(pool_out, (1, 0, 2))
    return (y_p, y_s, s_p[None], p_p[None], s_s[None], p_s[None])
```

```python
import functools

import jax
import jax.numpy as jnp
from jax import lax
from jax.experimental import pallas as pl
from jax.experimental.pallas import tpu as pltpu

F32 = jnp.float32
BF16 = jnp.bfloat16

D_MODEL = 1024
W_BRANCH = 512
N_HEADS = 4
DK = 128
DV = 128
CHUNK = 32
POOL_WINDOWS = (2, 4, 8, 16)
G_B = 128
POOL_BUF = 15
PAST_LEN = 16384
EPS = 1e-6
D_IN = 4 * W_BRANCH + 2 * W_BRANCH + 2 * D_MODEL
OFF_Q, OFF_F, OFF_I, OFF_GA = 0, 512, 1024, 1536
OFF_U, OFF_GB, OFF_MA, OFF_MB = 2048, 2560, 3072, 4096

SUBLANES = 8
VMEM_LIMIT_BYTES = 56 * 1024 * 1024
DECODE_VMEM_LIMIT_BYTES = 60 * 1024 * 1024

TM_PROMPT = 512
SUB_TILE = 256
PREP_SEQS = 64
BB_SAMPLE = 16
STATE_DMAS = 8
DEC_SEQ = 4
PAIR_ROWS = 2 * DEC_SEQ
PREP_PARTS = 4


def _rms(x, g):
    ms = jnp.mean(x * x, axis=-1, keepdims=True)
    return x * lax.rsqrt(ms + EPS) * g


def _lower_bound(lbl_ref):
    l0 = lbl_ref[0:1, :]
    l1 = lbl_ref[1:2, :]
    m = jnp.maximum(l0, l1)
    e0 = jnp.exp(l0 - m)
    e1 = jnp.exp(l1 - m)
    return e0 / (e0 + e1)


def _in_proj(x, ng_ref, win_ref, z_ref, c0=0, c1=D_IN):
    h = _rms(x, ng_ref[...]).astype(BF16)
    z_ref[:, c0:c1] = jnp.dot(h, win_ref[:, c0:c1], preferred_element_type=F32)


def _hgrn_inputs(z_ref, lb, hh):
    c = hh * DK
    q = z_ref[:, OFF_Q + c:OFF_Q + c + DK]
    f = z_ref[:, OFF_F + c:OFF_F + c + DK]
    v = z_ref[:, OFF_I + c:OFF_I + c + DK]
    lbh = lb[:, c:c + DK]
    fg = lbh + (1.0 - lbh) * jax.nn.sigmoid(f)
    logf = jnp.log(fg)
    k = 1.0 - fg
    qf = jax.nn.silu(q) * (DK ** -0.5)
    return qf, k, v, logf


def _group_cumsum(x, group):
    row = lax.broadcasted_iota(jnp.int32, x.shape, 0) & (group - 1)
    s = 1
    while s < group:
        x = x + jnp.where(row >= s, pltpu.roll(x, s, 0), 0.0)
        s *= 2
    return x


def _hgrn_epilogue(o, z_ref, hg_ref, hh, oa_ref):
    c = hh * DK
    ga = z_ref[:, OFF_GA + c:OFF_GA + c + DK]
    on = _rms(o, hg_ref[...]) * jax.nn.silu(ga)
    oa_ref[:, c:c + DK] = on.astype(BF16)


def _pool_epilogue(pooled, z_ref, g, wpool_ref, ps_ref, yb_ref):
    c = g * G_B
    mixed = jnp.dot(pooled.astype(BF16), wpool_ref[g], preferred_element_type=F32)
    gb = z_ref[:, OFF_GB + c:OFF_GB + c + G_B]
    yb = mixed * ps_ref[:, c:c + G_B] * jax.nn.silu(gb)
    yb_ref[:, c:c + G_B] = yb.astype(BF16)


def _weight_copies(hbm_refs, stage_ref, sem):
    win_hbm, wpa_hbm, wpool_hbm, wpb_hbm, wout_hbm = hbm_refs
    rows = stage_ref.shape[0]
    small = [
        (wpa_hbm, stage_ref.at[:, 0:D_MODEL]),
        (wpb_hbm, stage_ref.at[:, D_MODEL:2 * D_MODEL]),
        (wout_hbm.at[0:rows, :], stage_ref.at[:, 2 * D_MODEL:3 * D_MODEL]),
        (wout_hbm.at[rows:2 * rows, :], stage_ref.at[:, 3 * D_MODEL:4 * D_MODEL]),
        (wpool_hbm, stage_ref.at[:, 4 * D_MODEL:4 * D_MODEL + G_B]),
    ]
    small = [pltpu.make_async_copy(src, dst, sem.at[i]) for i, (src, dst) in enumerate(small)]
    blk = rows // 2

    def win(k):
        slot = k % 2
        return pltpu.make_async_copy(win_hbm.at[k * blk:(k + 1) * blk, :],
                                     stage_ref.at[slot * blk:(slot + 1) * blk, :],
                                     sem.at[len(small) + slot])
    return small, win, blk


def _weight_exports(vmem_refs, out_refs, sem):
    return [pltpu.make_async_copy(src, dst, sem.at[i])
            for i, (src, dst) in enumerate(zip(vmem_refs, out_refs))]


def _prompt_kernel(x_ref, ng_ref, win_hbm, lbl_ref, hg_ref, wpa_hbm, wpool_hbm, ps_ref,
                   wpb_hbm, wout_hbm, fg_ref,
                   y_ref, s_out_ref, p_out_ref, win_out, wpa_out, wpool_out, wpb_out, wout_out,
                   z_ref, st_ref, ext_ref, kx_ref, sn_ref, qs_ref, ks_ref, qd_ref, vb_ref,
                   oa_ref, yb_ref, win_ref, wpa_ref, wpool_ref, wpb_ref, wout_ref,
                   load_sem, export_sem):
    tm = TM_PROMPT
    sub = SUB_TILE
    n_sub = tm // sub
    nc = tm // CHUNK
    ncs = sub // CHUNK
    t = pl.program_id(1)
    first = t == 0
    hdr = 2 * SUBLANES
    bf16_weights = (win_ref, wpa_ref, wpool_ref, wpb_ref, wout_ref)
    exports = _weight_exports(bf16_weights, (win_out, wpa_out, wpool_out, wpb_out, wout_out),
                              export_sem)

    @pl.when((pl.program_id(0) == 0) & first)
    def _():
        st_ref[...] = jnp.zeros_like(st_ref)
        ext_ref[...] = jnp.zeros_like(ext_ref)
        kx_ref[...] = jnp.zeros_like(kx_ref)
        small, win, blk = _weight_copies((win_hbm, wpa_hbm, wpool_hbm, wpb_hbm, wout_hbm),
                                         z_ref, load_sem)
        for c in small:
            c.start()
        for c in small:
            c.wait()
        wpa_ref[...] = z_ref[:, 0:D_MODEL].astype(BF16)
        wpb_ref[...] = z_ref[:, D_MODEL:2 * D_MODEL].astype(BF16)
        wout_ref[0:tm, :] = z_ref[:, 2 * D_MODEL:3 * D_MODEL].astype(BF16)
        wout_ref[tm:2 * tm, :] = z_ref[:, 3 * D_MODEL:4 * D_MODEL].astype(BF16)
        for g in range(len(POOL_WINDOWS)):
            wpool_ref[g] = z_ref[g * G_B:(g + 1) * G_B,
                                 4 * D_MODEL:4 * D_MODEL + G_B].astype(BF16)
        n_blk = D_MODEL // blk
        win(0).start()
        win(1).start()
        for k in range(n_blk):
            slot = k % 2
            win(k).wait()
            win_ref[k * blk:(k + 1) * blk, :] = z_ref[slot * blk:(slot + 1) * blk, :].astype(BF16)
            if k + 2 < n_blk:
                win(k + 2).start()
        for c in exports:
            c.start()

    @pl.when((pl.program_id(0) == pl.num_programs(0) - 1) & (t == pl.num_programs(1) - 1))
    def _():
        for c in exports:
            c.wait()

    x = x_ref[0]
    _in_proj(x, ng_ref, win_ref, z_ref)
    lb = _lower_bound(lbl_ref)

    ri = lax.broadcasted_iota(jnp.int32, (sub, sub), 0)
    ci = lax.broadcasted_iota(jnp.int32, (sub, sub), 1)
    causal = ((ri // CHUNK) == (ci // CHUNK)) & (ci <= ri)

    heads = range(N_HEADS)
    units = [(hh, s) for hh in heads for s in range(n_sub)]

    def rows(s):
        return slice(s * sub, (s + 1) * sub)

    def cols(hh):
        return slice(hh * DK, (hh + 1) * DK)

    dec = []
    for hh in heads:
        qf, k, v, logf = _hgrn_inputs(z_ref, lb, hh)
        b = _group_cumsum(logf, CHUNK)
        b3 = b.reshape(nc, CHUNK, DK)
        ref = b3[:, CHUNK // 2:CHUNK // 2 + 1, :]
        bl = b3[:, CHUNK - 1:CHUNK, :]
        q3 = qf.reshape(nc, CHUNK, DK)
        k3 = k.reshape(nc, CHUNK, DK)
        qs_ref[:, cols(hh)] = (q3 * jnp.exp(b3 - ref)).reshape(tm, DK).astype(BF16)
        ks_ref[:, cols(hh)] = (k3 * jnp.exp(ref - b3)).reshape(tm, DK).astype(BF16)
        qd_ref[:, cols(hh)] = (q3 * jnp.exp(b3)).reshape(tm, DK).astype(BF16)
        vb_ref[:, cols(hh)] = v.astype(BF16)
        kd = (k3 * jnp.exp(bl - b3)).reshape(tm, DK).astype(BF16)
        dec.append(jnp.exp(bl))
        for c in range(nc):
            cc = c % ncs
            kx_ref[hh, c * CHUNK:(c + 1) * CHUNK, cc * DK:(cc + 1) * DK] = (
                kd[c * CHUNK:(c + 1) * CHUNK, :])

    ext_ref[0:hdr, :] = jnp.where(first, 0.0, ext_ref[tm:tm + hdr, :])
    ext_ref[hdr:hdr + tm, :] = z_ref[:, OFF_U:OFF_U + W_BRANCH]
    pos1 = t * tm + lax.broadcasted_iota(jnp.int32, (tm, 1), 0) + 1
    for g, w in enumerate(POOL_WINDOWS):
        c = g * G_B
        s = ext_ref[:, c:c + G_B]
        sh = 1
        while sh < w:
            s = s + pltpu.roll(s, sh, 0)
            sh *= 2
        inv = jnp.where(pos1 >= w, 1.0 / w, 1.0 / pos1.astype(F32))
        pooled = s[hdr:, :] * inv - z_ref[:, OFF_U + c:OFF_U + c + G_B]
        _pool_epilogue(pooled, z_ref, g, wpool_ref, ps_ref, yb_ref)
    p_out_ref[0] = ext_ref[hdr + tm - POOL_BUF:hdr + tm, :]

    sc = {(hh, s): lax.dot_general(qs_ref[rows(s), cols(hh)], ks_ref[rows(s), cols(hh)],
                                   (((1,), (1,)), ((), ())), preferred_element_type=F32)
          for hh, s in units}
    ut = {(hh, s): lax.dot_general(vb_ref[rows(s), cols(hh)], kx_ref[hh, rows(s), :],
                                   (((0,), (0,)), ((), ())), preferred_element_type=F32)
          for hh, s in units}
    o = {(hh, s): jnp.dot(jnp.where(causal, sc[hh, s], 0.0).astype(BF16),
                          vb_ref[rows(s), cols(hh)], preferred_element_type=F32)
         for hh, s in units}
    for hh in heads:
        st = jnp.where(first, 0.0, st_ref[hh])
        for c in range(nc):
            sn_ref[hh, c] = st.T.astype(BF16)
            st = dec[hh][c] * st + ut[hh, c // ncs][:, (c % ncs) * DK:(c % ncs + 1) * DK]
        st_ref[hh] = st
        s_out_ref[0, hh] = st.T
    wcol = D_MODEL // N_HEADS
    y_b = []
    for hh in heads:
        o_inter = [jnp.dot(qd_ref[c * CHUNK:(c + 1) * CHUNK, cols(hh)], sn_ref[hh, c],
                           preferred_element_type=F32) for c in range(nc)]
        y_b.append(jnp.dot(yb_ref[...], wpb_ref[:, hh * wcol:(hh + 1) * wcol],
                           preferred_element_type=F32))
        o_hh = jnp.concatenate([o[hh, s] for s in range(n_sub)], axis=0)
        _hgrn_epilogue(o_hh + jnp.concatenate(o_inter, axis=0), z_ref, hg_ref, hh, oa_ref)
    gated_b = jax.nn.sigmoid(z_ref[:, OFF_MB:OFF_MB + D_MODEL]) * jnp.concatenate(y_b, axis=1)

    y_a = jnp.dot(oa_ref[...], wpa_ref[...], preferred_element_type=F32)
    merged = jax.nn.sigmoid(z_ref[:, OFF_MA:OFF_MA + D_MODEL]) * y_a + gated_b
    out = x + jnp.dot(merged.astype(BF16), wout_ref[...], preferred_element_type=F32)
    y_ref[0] = _rms(out, fg_ref[...])


def _shift_rows(x, k):
    n = x.shape[0]
    return x if k % n == 0 else pltpu.roll(x, (-k) % n, 0)


def _group_bcast(x, j, group):
    t = lax.broadcasted_iota(jnp.int32, x.shape, 0) & (group - 1)
    out = _shift_rows(x, j - (group - 1))
    for tt in range(group - 2, -1, -1):
        out = jnp.where(t == tt, _shift_rows(x, j - tt), out)
    return out


def _decode_prep_hgrn(heads, z_ref, lbl_ref, a_ref, bv_ref, qd_ref, oi_ref, ga_ref):
    rows = z_ref.shape[0]
    lb = _lower_bound(lbl_ref)
    r = lax.broadcasted_iota(jnp.int32, (rows, DK), 0)
    t = r & (DEC_SEQ - 1)
    even = (r & (PAIR_ROWS - 1)) < DEC_SEQ
    blk = SUB_TILE
    ri = lax.broadcasted_iota(jnp.int32, (blk, blk), 0)
    ci = lax.broadcasted_iota(jnp.int32, (blk, blk), 1)
    causal = ((ri // DEC_SEQ) == (ci // DEC_SEQ)) & (ci <= ri)

    for hh in heads:
        cs = slice(hh * DK, (hh + 1) * DK)
        qf, k, v, logf = _hgrn_inputs(z_ref, lb, hh)
        b = _group_cumsum(logf, DEC_SEQ)
        ref = _group_bcast(b, DEC_SEQ // 2, DEC_SEQ)
        bl = _group_bcast(b, DEC_SEQ - 1, DEC_SEQ)
        qs = (qf * jnp.exp(b - ref)).astype(BF16)
        ks = (k * jnp.exp(ref - b)).astype(BF16)
        kd = k * jnp.exp(bl - b)
        vb = v.astype(BF16)
        qd_ref[:, cs] = (qf * jnp.exp(b)).astype(BF16)
        for s in range(rows // blk):
            rs = slice(s * blk, (s + 1) * blk)
            sc = lax.dot_general(qs[rs], ks[rs], (((1,), (1,)), ((), ())),
                                 preferred_element_type=F32)
            oi_ref[rs, cs] = jnp.dot(jnp.where(causal, sc, 0.0).astype(BF16), vb[rs],
                                     preferred_element_type=F32)
        dec = jnp.exp(bl)
        d1 = dec.astype(BF16).astype(F32)
        d2 = (dec - d1).astype(BF16).astype(F32)
        d3 = (dec - d1 - d2).astype(BF16).astype(F32)
        tail = jnp.where(t == 0, d1, jnp.where(t == 1, d2, jnp.where(t == 2, d3, 0.0)))
        a_ref[hh, 0] = jnp.where(even, kd, _shift_rows(tail, -DEC_SEQ)).astype(BF16)
        a_ref[hh, 1] = jnp.where(even, _shift_rows(tail, DEC_SEQ), kd).astype(BF16)
        zero = jnp.zeros_like(vb)
        bv_ref[hh, 0] = jnp.where(even, vb, zero)
        bv_ref[hh, 1] = jnp.where(even, zero, vb)
        ga_ref[:, cs] = jax.nn.silu(z_ref[:, OFF_GA + hh * DK:OFF_GA + (hh + 1) * DK])


def _decode_prep_history(p_in_ref, rw_ref, p_out_ref):
    n_seq = p_in_ref.shape[1]
    for g, w in enumerate(POOL_WINDOWS):
        gc = slice(g * G_B, (g + 1) * G_B)
        acc = None
        suffix = {}
        for j in range(POOL_BUF - 1, -1, -1):
            e = p_in_ref[j, :, gc]
            acc = e if acc is None else acc + e
            suffix[j] = acc
        for tt in range(DEC_SEQ):
            j = POOL_BUF + 1 - w + tt
            rw_ref[g, pl.ds(tt, n_seq, stride=DEC_SEQ), :] = (
                suffix[j] if j < POOL_BUF else jnp.zeros((n_seq, G_B), F32))
    for j in range(POOL_BUF - DEC_SEQ):
        p_out_ref[j] = p_in_ref[j + DEC_SEQ]


def _decode_prep_pool(z_ref, wpool_ref, ps_ref, wpb_ref, ma_ref, gb_ref, p_out_ref,
                      rw_ref, u_ref, yb_ref):
    rows = z_ref.shape[0]
    n_seq = rows // DEC_SEQ
    t = lax.broadcasted_iota(jnp.int32, (rows, G_B), 0) & (DEC_SEQ - 1)
    for g, w in enumerate(POOL_WINDOWS):
        u = z_ref[:, OFF_U + g * G_B:OFF_U + (g + 1) * G_B]
        u_ref[g] = u
        cu = u
        sh = 1
        while sh < min(w, DEC_SEQ):
            cu = cu + jnp.where(t >= sh, _shift_rows(cu, -sh), 0.0)
            sh *= 2
        pooled = (rw_ref[g] + cu) * (1.0 / w) - u
        _pool_epilogue(pooled, z_ref, g, wpool_ref, ps_ref, yb_ref)
    gb_ref[...] = jax.nn.sigmoid(z_ref[:, OFF_MB:OFF_MB + D_MODEL]) * jnp.dot(
        yb_ref[...], wpb_ref[...], preferred_element_type=F32)
    ma_ref[...] = jax.nn.sigmoid(z_ref[:, OFF_MA:OFF_MA + D_MODEL])
    for tt in range(DEC_SEQ):
        for g in range(len(POOL_WINDOWS)):
            p_out_ref[POOL_BUF - DEC_SEQ + tt, :, g * G_B:(g + 1) * G_B] = (
                u_ref[g, pl.ds(tt, n_seq, stride=DEC_SEQ), :])


def _decode_prep(part, x_ref, p_in_ref, ng_ref, win_ref, lbl_ref, wpool_ref, ps_ref, wpb_ref,
                 a_ref, bv_ref, qd_ref, oi_ref, ga_ref, ma_ref, gb_ref, p_out_ref,
                 xs_ref, z_ref, rw_ref, u_ref, yb_ref):
    hgrn = functools.partial(_decode_prep_hgrn, z_ref=z_ref, lbl_ref=lbl_ref, a_ref=a_ref,
                             bv_ref=bv_ref, qd_ref=qd_ref, oi_ref=oi_ref, ga_ref=ga_ref)
    if part == 0:
        for i in range(x_ref.shape[0]):
            xs_ref[i * DEC_SEQ:(i + 1) * DEC_SEQ, :] = x_ref[i]
        _in_proj(xs_ref[...], ng_ref, win_ref, z_ref, 0, OFF_U)
        _decode_prep_history(p_in_ref, rw_ref, p_out_ref)
    elif part == 1:
        _in_proj(xs_ref[...], ng_ref, win_ref, z_ref, OFF_U, D_IN)
    elif part == 2:
        hgrn(range(0, N_HEADS - 1))
    else:
        hgrn(range(N_HEADS - 1, N_HEADS))
        _decode_prep_pool(z_ref, wpool_ref, ps_ref, wpb_ref, ma_ref, gb_ref, p_out_ref,
                          rw_ref, u_ref, yb_ref)


def _decode_state_step(a_ref, bv_ref, qd_ref, s_in_ref, s_out_ref, o_ref):
    n_pairs = BB_SAMPLE // 2
    r8 = lax.broadcasted_iota(jnp.int32, (PAIR_ROWS, DV), 0)
    even = r8 < DEC_SEQ
    ones = (jnp.where((r8 >= DEC_SEQ) & (r8 < DEC_SEQ + 3), 1.0, 0.0).astype(BF16),
            jnp.where(r8 < 3, 1.0, 0.0).astype(BF16))
    for hh in range(N_HEADS):
        for lp in range(n_pairs):
            rs = slice(lp * PAIR_ROWS, (lp + 1) * PAIR_ROWS)
            for e in range(2):
                rhs = jnp.concatenate([bv_ref[hh, e, rs, :], ones[e]], axis=1)
                upd = lax.dot_general(a_ref[hh, e, rs, :], rhs, (((0,), (0,)), ((), ())),
                                      preferred_element_type=F32)
                s_out_ref[2 * lp + e, hh] = (upd[:, DV:2 * DV] * s_in_ref[2 * lp + e, hh]
                                             + upd[:, 0:DV])
    for hh in range(N_HEADS):
        cs = slice(hh * DK, (hh + 1) * DK)
        for lp in range(n_pairs):
            rs = slice(lp * PAIR_ROWS, (lp + 1) * PAIR_ROWS)
            q8 = qd_ref[rs, cs]
            o0 = jnp.dot(q8, s_in_ref[2 * lp, hh].astype(BF16), preferred_element_type=F32)
            o1 = jnp.dot(q8, s_in_ref[2 * lp + 1, hh].astype(BF16), preferred_element_type=F32)
            o_ref[rs, cs] = o_ref[rs, cs] + jnp.where(even, o0, o1)


def _decode_finish(o_ref, ga_ref, ma_ref, gb_ref, xs_ref, hg_ref, wpa_ref, wout_ref,
                   fg_ref, y_ref, oa_ref, *, n_seq):
    for hh in range(N_HEADS):
        cs = slice(hh * DK, (hh + 1) * DK)
        oa_ref[:, cs] = (_rms(o_ref[:, cs], hg_ref[...]) * ga_ref[:, cs]).astype(BF16)
    y_a = jnp.dot(oa_ref[...], wpa_ref[...], preferred_element_type=F32)
    merged = ma_ref[...] * y_a + gb_ref[...]
    out = xs_ref[...] + jnp.dot(merged.astype(BF16), wout_ref[...], preferred_element_type=F32)
    y = _rms(out, fg_ref[...])
    for i in range(n_seq):
        y_ref[i] = y[i * DEC_SEQ:(i + 1) * DEC_SEQ, :]


def _state_copies(hbm_ref, buf_ref, sem, blk, slot, to_hbm):
    n = BB_SAMPLE // STATE_DMAS
    copies = []
    for c in range(STATE_DMAS):
        in_hbm = hbm_ref.at[pl.ds(blk * BB_SAMPLE + c * n, n)]
        in_vmem = buf_ref.at[slot, pl.ds(c * n, n)]
        src, dst = (in_vmem, in_hbm) if to_hbm else (in_hbm, in_vmem)
        copies.append(pltpu.make_async_copy(src, dst, sem.at[slot, c]))
    return copies


def _decode_kernel(xp_ref, p_in_ref, s_in_hbm,
                   ng_ref, win_ref, lbl_ref, wpool_ref, ps_ref, wpb_ref,
                   hg_ref, wpa_ref, wout_ref, fg_ref,
                   s_out_hbm, y_ref, p_out_ref,
                   a_all, bv_all, qd_all, o_all, ga_all, ma_all, gb_all, xs_all,
                   z_ref, rw_ref, u_ref, yb_ref, oa_ref, s_in_buf, s_out_buf, in_sem, out_sem,
                   *, n_prep):
    s = pl.program_id(0)
    prows = PREP_SEQS * DEC_SEQ
    brows = BB_SAMPLE * DEC_SEQ
    n_state = n_prep * PREP_PARTS

    def load(blk, slot):
        return _state_copies(s_in_hbm, s_in_buf, in_sem, blk, slot, to_hbm=False)

    def store(blk, slot):
        return _state_copies(s_out_hbm, s_out_buf, out_sem, blk, slot, to_hbm=True)

    def block_rows(blk):
        start = blk * prows
        return pl.ds(start if isinstance(start, int) else pl.multiple_of(start, prows), prows)

    def prep(part, blk):
        r = block_rows(blk)
        _decode_prep(part, xp_ref, p_in_ref, ng_ref, win_ref, lbl_ref, wpool_ref, ps_ref, wpb_ref,
                     a_all.at[:, :, r, :], bv_all.at[:, :, r, :], qd_all.at[r, :],
                     o_all.at[r, :], ga_all.at[r, :], ma_all.at[r, :], gb_all.at[r, :],
                     p_out_ref, xs_all.at[r, :], z_ref, rw_ref, u_ref, yb_ref)

    def finish(blk):
        f = block_rows(blk)
        _decode_finish(o_all.at[f, :], ga_all.at[f, :], ma_all.at[f, :], gb_all.at[f, :],
                       xs_all.at[f, :], hg_ref, wpa_ref, wout_ref, fg_ref, y_ref, oa_ref,
                       n_seq=PREP_SEQS)

    i = s - 1
    blk = lax.div(i, PREP_PARTS)
    part_now = lax.rem(i, PREP_PARTS)
    slot = lax.rem(i, 2)

    @pl.when(s == 0)
    def _():
        for b in range(2):
            for c in load(b, b):
                c.start()
        for part in range(PREP_PARTS):
            prep(part, 0)

    @pl.when(s > 0)
    def _():
        @pl.when((i > 0) & (i + 1 < n_state))
        def _():
            for c in load(i + 1, 1 - slot):
                c.start()
        for c in load(i, slot):
            c.wait()

        @pl.when(i >= 2)
        def _():
            for c in store(i - 2, slot):
                c.wait()
        r = pl.ds(pl.multiple_of(i * brows, brows), brows)
        _decode_state_step(a_all.at[:, :, r, :], bv_all.at[:, :, r, :], qd_all.at[r, :],
                           s_in_buf.at[slot], s_out_buf.at[slot], o_all.at[r, :])
        for c in store(i, slot):
            c.start()

    for part in range(PREP_PARTS):
        @pl.when((part_now == part) & (blk + 1 < n_prep))
        def _():
            prep(part, blk + 1)

    @pl.when((part_now == 0) & (i > 0))
    def _():
        finish(blk - 1)

    @pl.when(i == n_state - 1)
    def _():
        finish(n_prep - 1)
        for c in store(i - 1, 1 - slot) + store(i, slot):
            c.wait()


def _const_spec(shape):
    n = len(shape)
    return pl.BlockSpec(shape, lambda *_: (0,) * n, pipeline_mode=pl.Buffered(1))


def kernel(x_prompt, x_sample, state_hgrn, state_pool, norm_g, w_in, lb_logits, hgrn_norm_g,
           w_proj_a, w_pool, pool_scale, w_proj_b, w_out, final_norm_g):
    batch, seq, _ = x_prompt.shape
    dec_batch, dec_seq, _ = x_sample.shape
    assert norm_g.shape[0] == 1 and lb_logits.shape[0] == 2, "single-layer decoder only"
    assert seq % TM_PROMPT == 0 and TM_PROMPT % SUB_TILE == 0 and dec_batch % BB_SAMPLE == 0
    assert dec_seq == DEC_SEQ and PAIR_ROWS == SUBLANES and BB_SAMPLE % 2 == 0
    assert dec_batch % PREP_SEQS == 0 and (PREP_SEQS * DEC_SEQ) % SUB_TILE == 0
    assert PREP_SEQS == PREP_PARTS * BB_SAMPLE and BB_SAMPLE % STATE_DMAS == 0
    assert PAST_LEN >= max(POOL_WINDOWS)

    n_groups = len(POOL_WINDOWS)
    tm = TM_PROMPT
    nt = seq // tm
    assert w_in.shape[1:] == (2 * tm, D_IN) and w_proj_a.shape[1] == tm, "weight staging layout"
    hbm = pl.BlockSpec(memory_space=pl.ANY)
    bf16_shapes = [(D_MODEL, D_IN), (W_BRANCH, D_MODEL), (n_groups, G_B, G_B),
                   (W_BRANCH, D_MODEL), (D_MODEL, D_MODEL)]
    y_p, s_p, p_p, win_b, wpa_b, wpool_b, wpb_b, wout_b = pl.pallas_call(
        _prompt_kernel,
        grid=(batch, nt),
        in_specs=[pl.BlockSpec((1, tm, D_MODEL), lambda b, t: (b, t, 0)),
                  _const_spec((1, D_MODEL)), hbm, _const_spec((2, W_BRANCH)), _const_spec((1, DV)),
                  hbm, hbm, _const_spec((1, W_BRANCH)), hbm, hbm, _const_spec((1, D_MODEL))],
        out_specs=[
            pl.BlockSpec((1, tm, D_MODEL), lambda b, t: (b, t, 0)),
            pl.BlockSpec((1, N_HEADS, DK, DV), lambda b, t: (b, 0, 0, 0)),
            pl.BlockSpec((1, POOL_BUF, W_BRANCH), lambda b, t: (b, 0, 0)),
        ] + [hbm] * len(bf16_shapes),
        out_shape=[
            jax.ShapeDtypeStruct((batch, seq, D_MODEL), F32),
            jax.ShapeDtypeStruct((batch, N_HEADS, DK, DV), F32),
            jax.ShapeDtypeStruct((batch, POOL_BUF, W_BRANCH), F32),
        ] + [jax.ShapeDtypeStruct(s, BF16) for s in bf16_shapes],
        scratch_shapes=[
            pltpu.VMEM((tm, D_IN), F32),
            pltpu.VMEM((N_HEADS, DV, DK), F32),
            pltpu.VMEM((tm + 2 * SUBLANES, W_BRANCH), F32),
            pltpu.VMEM((N_HEADS, tm, (SUB_TILE // CHUNK) * DK), BF16),
            pltpu.VMEM((N_HEADS, tm // CHUNK, DK, DV), BF16),
            pltpu.VMEM((tm, W_BRANCH), BF16),
            pltpu.VMEM((tm, W_BRANCH), BF16),
            pltpu.VMEM((tm, W_BRANCH), BF16),
            pltpu.VMEM((tm, W_BRANCH), BF16),
            pltpu.VMEM((tm, W_BRANCH), BF16),
            pltpu.VMEM((tm, W_BRANCH), BF16),
        ] + [pltpu.VMEM(s, BF16) for s in bf16_shapes] + [
            pltpu.SemaphoreType.DMA((7,)),
            pltpu.SemaphoreType.DMA((len(bf16_shapes),)),
        ],
        compiler_params=pltpu.CompilerParams(
            dimension_semantics=("arbitrary", "arbitrary"),
            vmem_limit_bytes=VMEM_LIMIT_BYTES),
        name="hgrn2_pool_prompt",
    )(x_prompt, norm_g, w_in[0], lb_logits, hgrn_norm_g, w_proj_a[0],
      w_pool[0].reshape(n_groups * G_B, G_B), pool_scale, w_proj_b[0], w_out[0],
      final_norm_g.reshape(1, D_MODEL))
    weights = (norm_g, win_b, lb_logits, hgrn_norm_g, wpa_b, wpool_b, pool_scale, wpb_b, wout_b,
               final_norm_g.reshape(1, D_MODEL))

    rows = dec_batch * DEC_SEQ
    pool_in = jnp.transpose(state_pool[0], (1, 0, 2))
    ng, win_b, lbl, hg, wpa_b, wpool_b, ps, wpb_b, wout_b, fg = weights
    pseq = PREP_SEQS
    prows = pseq * DEC_SEQ
    n_prep = dec_batch // pseq
    bb = BB_SAMPLE

    def prep_blk(s):
        return jnp.minimum((s + PREP_PARTS - 1) // PREP_PARTS, n_prep - 1)

    def finish_blk(s):
        return jnp.minimum(jnp.maximum(s - 2, 0) // PREP_PARTS, n_prep - 1)

    s_s, y_s, pool_out = pl.pallas_call(
        functools.partial(_decode_kernel, n_prep=n_prep),
        grid=(1 + dec_batch // bb,),
        in_specs=[
            pl.BlockSpec((pseq, DEC_SEQ, D_MODEL), lambda s: (prep_blk(s), 0, 0)),
            pl.BlockSpec((POOL_BUF, pseq, W_BRANCH), lambda s: (0, prep_blk(s), 0),
                         pipeline_mode=pl.Buffered(1)),
            hbm,
            _const_spec(ng.shape), _const_spec(win_b.shape), _const_spec(lbl.shape),
            _const_spec(wpool_b.shape), _const_spec(ps.shape), _const_spec(wpb_b.shape),
            _const_spec(hg.shape), _const_spec(wpa_b.shape), _const_spec(wout_b.shape),
            _const_spec(fg.shape),
        ],
        out_specs=[
            hbm,
            pl.BlockSpec((pseq, DEC_SEQ, D_MODEL), lambda s: (finish_blk(s), 0, 0)),
            pl.BlockSpec((POOL_BUF, pseq, W_BRANCH), lambda s: (0, prep_blk(s), 0)),
        ],
        out_shape=[
            jax.ShapeDtypeStruct((dec_batch, N_HEADS, DK, DV), F32),
            jax.ShapeDtypeStruct(x_sample.shape, F32),
            jax.ShapeDtypeStruct((POOL_BUF, dec_batch, W_BRANCH), F32),
        ],
        scratch_shapes=[
            pltpu.VMEM((N_HEADS, 2, rows, DK), BF16),
            pltpu.VMEM((N_HEADS, 2, rows, DV), BF16),
            pltpu.VMEM((rows, W_BRANCH), BF16),
            pltpu.VMEM((rows, W_BRANCH), F32),
            pltpu.VMEM((rows, W_BRANCH), F32),
            pltpu.VMEM((rows, D_MODEL), F32),
            pltpu.VMEM((rows, D_MODEL), F32),
            pltpu.VMEM((rows, D_MODEL), F32),
            pltpu.VMEM((prows, D_IN), F32),
            pltpu.VMEM((len(POOL_WINDOWS), prows, G_B), F32),
            pltpu.VMEM((len(POOL_WINDOWS), prows, G_B), F32),
            pltpu.VMEM((prows, W_BRANCH), BF16),
            pltpu.VMEM((prows, W_BRANCH), BF16),
            pltpu.VMEM((2, bb, N_HEADS, DK, DV), F32),
            pltpu.VMEM((2, bb, N_HEADS, DK, DV), F32),
            pltpu.SemaphoreType.DMA((2, STATE_DMAS)),
            pltpu.SemaphoreType.DMA((2, STATE_DMAS)),
        ],
        compiler_params=pltpu.CompilerParams(
            dimension_semantics=("arbitrary",), vmem_limit_bytes=DECODE_VMEM_LIMIT_BYTES),
        name="hgrn2_pool_decode",
    )(x_sample, pool_in, state_hgrn[0], ng, win_b, lbl, wpool_b, ps, wpb_b,
      hg, wpa_b, wout_b, fg)

    p_s = jnp.transpose(pool_out, (1, 0, 2))
    return (y_p, y_s, s_p[None], p_p[None], s_s[None], p_s[None])
```

```python
import functools

import jax
import jax.numpy as jnp
from jax import lax
from jax.experimental import pallas as pl
from jax.experimental.pallas import tpu as pltpu

F32 = jnp.float32
BF16 = jnp.bfloat16

D_MODEL = 1024
W_BRANCH = 512
N_HEADS = 4
DK = 128
DV = 128
CHUNK = 32
POOL_WINDOWS = (2, 4, 8, 16)
G_B = 128
POOL_BUF = 15
PAST_LEN = 16384
EPS = 1e-6
D_IN = 4 * W_BRANCH + 2 * W_BRANCH + 2 * D_MODEL
OFF_Q, OFF_F, OFF_I, OFF_GA = 0, 512, 1024, 1536
OFF_U, OFF_GB, OFF_MA, OFF_MB = 2048, 2560, 3072, 4096

SUBLANES = 8
VMEM_LIMIT_BYTES = 56 * 1024 * 1024
DECODE_VMEM_LIMIT_BYTES = 60 * 1024 * 1024

TM_PROMPT = 512
SUB_TILE = 256
PREP_SEQS = 64
BB_SAMPLE = 16
STATE_DMAS = 8
WEIGHT_DMA_SPLIT = 4
DEC_SEQ = 4
PAIR_ROWS = 2 * DEC_SEQ
PREP_PARTS = 4


def _rms(x, g):
    ms = jnp.mean(x * x, axis=-1, keepdims=True)
    return x * lax.rsqrt(ms + EPS) * g


def _lower_bound(lbl_ref):
    l0 = lbl_ref[0:1, :]
    l1 = lbl_ref[1:2, :]
    m = jnp.maximum(l0, l1)
    e0 = jnp.exp(l0 - m)
    e1 = jnp.exp(l1 - m)
    return e0 / (e0 + e1)


def _in_proj(x, ng_ref, win_ref, z_ref, c0=0, c1=D_IN):
    h = _rms(x, ng_ref[...]).astype(BF16)
    z_ref[:, c0:c1] = jnp.dot(h, win_ref[:, c0:c1], preferred_element_type=F32)


def _hgrn_inputs(z_ref, lb, hh):
    c = hh * DK
    q = z_ref[:, OFF_Q + c:OFF_Q + c + DK]
    f = z_ref[:, OFF_F + c:OFF_F + c + DK]
    v = z_ref[:, OFF_I + c:OFF_I + c + DK]
    lbh = lb[:, c:c + DK]
    fg = lbh + (1.0 - lbh) * jax.nn.sigmoid(f)
    logf = jnp.log(fg)
    k = 1.0 - fg
    qf = jax.nn.silu(q) * (DK ** -0.5)
    return qf, k, v, logf


def _group_cumsum(x, group):
    row = lax.broadcasted_iota(jnp.int32, x.shape, 0) & (group - 1)
    s = 1
    while s < group:
        x = x + jnp.where(row >= s, pltpu.roll(x, s, 0), 0.0)
        s *= 2
    return x


def _hgrn_epilogue(o, z_ref, hg_ref, hh, oa_ref):
    c = hh * DK
    ga = z_ref[:, OFF_GA + c:OFF_GA + c + DK]
    on = _rms(o, hg_ref[...]) * jax.nn.silu(ga)
    oa_ref[:, c:c + DK] = on.astype(BF16)


def _pool_epilogue(pooled, z_ref, g, wpool_ref, ps_ref, yb_ref):
    c = g * G_B
    mixed = jnp.dot(pooled.astype(BF16), wpool_ref[g], preferred_element_type=F32)
    gb = z_ref[:, OFF_GB + c:OFF_GB + c + G_B]
    yb = mixed * ps_ref[:, c:c + G_B] * jax.nn.silu(gb)
    yb_ref[:, c:c + G_B] = yb.astype(BF16)


def _weight_copies(hbm_refs, stage_ref, sem):
    win_hbm, wpa_hbm, wpool_hbm, wpb_hbm, wout_hbm = hbm_refs
    rows = stage_ref.shape[0]
    n = WEIGHT_DMA_SPLIT
    rp = rows // n
    small = []
    for j, (src, r0) in enumerate(((wpa_hbm, 0), (wpb_hbm, 0), (wout_hbm, 0), (wout_hbm, rows))):
        small += [(src.at[r0 + q * rp:r0 + (q + 1) * rp, :],
                   stage_ref.at[q * rp:(q + 1) * rp, j * D_MODEL:(j + 1) * D_MODEL])
                  for q in range(n)]
    small.append((wpool_hbm, stage_ref.at[:, 4 * D_MODEL:4 * D_MODEL + G_B]))
    small = [pltpu.make_async_copy(src, dst, sem.at[i]) for i, (src, dst) in enumerate(small)]
    blk = rows // 2
    cw = D_IN // n

    def win(k):
        slot = k % 2
        return [pltpu.make_async_copy(
            win_hbm.at[k * blk:(k + 1) * blk, q * cw:(q + 1) * cw],
            stage_ref.at[slot * blk:(slot + 1) * blk, q * cw:(q + 1) * cw],
            sem.at[len(small) + slot * n + q]) for q in range(n)]
    return small, win, blk


def _weight_exports(vmem_refs, out_refs, sem):
    return [pltpu.make_async_copy(src, dst, sem.at[i])
            for i, (src, dst) in enumerate(zip(vmem_refs, out_refs))]


def _prompt_kernel(x_ref, ng_ref, win_hbm, lbl_ref, hg_ref, wpa_hbm, wpool_hbm, ps_ref,
                   wpb_hbm, wout_hbm, fg_ref,
                   y_ref, s_out_ref, p_out_ref, win_out, wpa_out, wpool_out, wpb_out, wout_out,
                   z_ref, st_ref, ext_ref, kx_ref, sn_ref, qs_ref, ks_ref, qd_ref, vb_ref,
                   oa_ref, yb_ref, win_ref, wpa_ref, wpool_ref, wpb_ref, wout_ref,
                   load_sem, export_sem):
    tm = TM_PROMPT
    sub = SUB_TILE
    n_sub = tm // sub
    nc = tm // CHUNK
    ncs = sub // CHUNK
    t = pl.program_id(1)
    first = t == 0
    hdr = 2 * SUBLANES
    bf16_weights = (win_ref, wpa_ref, wpool_ref, wpb_ref, wout_ref)
    exports = _weight_exports(bf16_weights, (win_out, wpa_out, wpool_out, wpb_out, wout_out),
                              export_sem)

    @pl.when((pl.program_id(0) == 0) & first)
    def _():
        st_ref[...] = jnp.zeros_like(st_ref)
        ext_ref[...] = jnp.zeros_like(ext_ref)
        kx_ref[...] = jnp.zeros_like(kx_ref)
        small, win, blk = _weight_copies((win_hbm, wpa_hbm, wpool_hbm, wpb_hbm, wout_hbm),
                                         z_ref, load_sem)
        for c in small:
            c.start()
        for c in small:
            c.wait()
        wpa_ref[...] = z_ref[:, 0:D_MODEL].astype(BF16)
        wpb_ref[...] = z_ref[:, D_MODEL:2 * D_MODEL].astype(BF16)
        wout_ref[0:tm, :] = z_ref[:, 2 * D_MODEL:3 * D_MODEL].astype(BF16)
        wout_ref[tm:2 * tm, :] = z_ref[:, 3 * D_MODEL:4 * D_MODEL].astype(BF16)
        for g in range(len(POOL_WINDOWS)):
            wpool_ref[g] = z_ref[g * G_B:(g + 1) * G_B,
                                 4 * D_MODEL:4 * D_MODEL + G_B].astype(BF16)
        n_blk = D_MODEL // blk
        for c in win(0) + win(1):
            c.start()
        for k in range(n_blk):
            slot = k % 2
            for c in win(k):
                c.wait()
            win_ref[k * blk:(k + 1) * blk, :] = z_ref[slot * blk:(slot + 1) * blk, :].astype(BF16)
            if k + 2 < n_blk:
                for c in win(k + 2):
                    c.start()
        for c in exports:
            c.start()

    @pl.when((pl.program_id(0) == pl.num_programs(0) - 1) & (t == pl.num_programs(1) - 1))
    def _():
        for c in exports:
            c.wait()

    x = x_ref[0]
    _in_proj(x, ng_ref, win_ref, z_ref)
    lb = _lower_bound(lbl_ref)

    ri = lax.broadcasted_iota(jnp.int32, (sub, sub), 0)
    ci = lax.broadcasted_iota(jnp.int32, (sub, sub), 1)
    causal = ((ri // CHUNK) == (ci // CHUNK)) & (ci <= ri)

    heads = range(N_HEADS)
    units = [(hh, s) for hh in heads for s in range(n_sub)]

    def rows(s):
        return slice(s * sub, (s + 1) * sub)

    def cols(hh):
        return slice(hh * DK, (hh + 1) * DK)

    dec = []
    for hh in heads:
        qf, k, v, logf = _hgrn_inputs(z_ref, lb, hh)
        b = _group_cumsum(logf, CHUNK)
        b3 = b.reshape(nc, CHUNK, DK)
        ref = b3[:, CHUNK // 2:CHUNK // 2 + 1, :]
        bl = b3[:, CHUNK - 1:CHUNK, :]
        q3 = qf.reshape(nc, CHUNK, DK)
        k3 = k.reshape(nc, CHUNK, DK)
        qs_ref[:, cols(hh)] = (q3 * jnp.exp(b3 - ref)).reshape(tm, DK).astype(BF16)
        ks_ref[:, cols(hh)] = (k3 * jnp.exp(ref - b3)).reshape(tm, DK).astype(BF16)
        qd_ref[:, cols(hh)] = (q3 * jnp.exp(b3)).reshape(tm, DK).astype(BF16)
        vb_ref[:, cols(hh)] = v.astype(BF16)
        kd = (k3 * jnp.exp(bl - b3)).reshape(tm, DK).astype(BF16)
        dec.append(jnp.exp(bl))
        for c in range(nc):
            cc = c % ncs
            kx_ref[hh, c * CHUNK:(c + 1) * CHUNK, cc * DK:(cc + 1) * DK] = (
                kd[c * CHUNK:(c + 1) * CHUNK, :])

    ext_ref[0:hdr, :] = jnp.where(first, 0.0, ext_ref[tm:tm + hdr, :])
    ext_ref[hdr:hdr + tm, :] = z_ref[:, OFF_U:OFF_U + W_BRANCH]
    pos1 = t * tm + lax.broadcasted_iota(jnp.int32, (tm, 1), 0) + 1
    for g, w in enumerate(POOL_WINDOWS):
        c = g * G_B
        s = ext_ref[:, c:c + G_B]
        sh = 1
        while sh < w:
            s = s + pltpu.roll(s, sh, 0)
            sh *= 2
        inv = jnp.where(pos1 >= w, 1.0 / w, 1.0 / pos1.astype(F32))
        pooled = s[hdr:, :] * inv - z_ref[:, OFF_U + c:OFF_U + c + G_B]
        _pool_epilogue(pooled, z_ref, g, wpool_ref, ps_ref, yb_ref)
    p_out_ref[0] = ext_ref[hdr + tm - POOL_BUF:hdr + tm, :]

    sc = {(hh, s): lax.dot_general(qs_ref[rows(s), cols(hh)], ks_ref[rows(s), cols(hh)],
                                   (((1,), (1,)), ((), ())), preferred_element_type=F32)
          for hh, s in units}
    ut = {(hh, s): lax.dot_general(vb_ref[rows(s), cols(hh)], kx_ref[hh, rows(s), :],
                                   (((0,), (0,)), ((), ())), preferred_element_type=F32)
          for hh, s in units}
    o = {(hh, s): jnp.dot(jnp.where(causal, sc[hh, s], 0.0).astype(BF16),
                          vb_ref[rows(s), cols(hh)], preferred_element_type=F32)
         for hh, s in units}
    for hh in heads:
        st = jnp.where(first, 0.0, st_ref[hh])
        for c in range(nc):
            sn_ref[hh, c] = st.T.astype(BF16)
            st = dec[hh][c] * st + ut[hh, c // ncs][:, (c % ncs) * DK:(c % ncs + 1) * DK]
        st_ref[hh] = st
        s_out_ref[0, hh] = st.T
    wcol = D_MODEL // N_HEADS
    y_b = []
    for hh in heads:
        o_inter = [jnp.dot(qd_ref[c * CHUNK:(c + 1) * CHUNK, cols(hh)], sn_ref[hh, c],
                           preferred_element_type=F32) for c in range(nc)]
        y_b.append(jnp.dot(yb_ref[...], wpb_ref[:, hh * wcol:(hh + 1) * wcol],
                           preferred_element_type=F32))
        o_hh = jnp.concatenate([o[hh, s] for s in range(n_sub)], axis=0)
        _hgrn_epilogue(o_hh + jnp.concatenate(o_inter, axis=0), z_ref, hg_ref, hh, oa_ref)
    gated_b = jax.nn.sigmoid(z_ref[:, OFF_MB:OFF_MB + D_MODEL]) * jnp.concatenate(y_b, axis=1)

    y_a = jnp.dot(oa_ref[...], wpa_ref[...], preferred_element_type=F32)
    merged = jax.nn.sigmoid(z_ref[:, OFF_MA:OFF_MA + D_MODEL]) * y_a + gated_b
    out = x + jnp.dot(merged.astype(BF16), wout_ref[...], preferred_element_type=F32)
    y_ref[0] = _rms(out, fg_ref[...])


def _shift_rows(x, k):
    n = x.shape[0]
    return x if k % n == 0 else pltpu.roll(x, (-k) % n, 0)


def _group_bcast(x, j, group):
    t = lax.broadcasted_iota(jnp.int32, x.shape, 0) & (group - 1)
    out = _shift_rows(x, j - (group - 1))
    for tt in range(group - 2, -1, -1):
        out = jnp.where(t == tt, _shift_rows(x, j - tt), out)
    return out


def _decode_prep_hgrn(heads, z_ref, lbl_ref, a_ref, bv_ref, qd_ref, oi_ref, ga_ref):
    rows = z_ref.shape[0]
    lb = _lower_bound(lbl_ref)
    r = lax.broadcasted_iota(jnp.int32, (rows, DK), 0)
    t = r & (DEC_SEQ - 1)
    even = (r & (PAIR_ROWS - 1)) < DEC_SEQ
    blk = SUB_TILE
    ri = lax.broadcasted_iota(jnp.int32, (blk, blk), 0)
    ci = lax.broadcasted_iota(jnp.int32, (blk, blk), 1)
    causal = ((ri // DEC_SEQ) == (ci // DEC_SEQ)) & (ci <= ri)

    for hh in heads:
        cs = slice(hh * DK, (hh + 1) * DK)
        qf, k, v, logf = _hgrn_inputs(z_ref, lb, hh)
        b = _group_cumsum(logf, DEC_SEQ)
        ref = _group_bcast(b, DEC_SEQ // 2, DEC_SEQ)
        bl = _group_bcast(b, DEC_SEQ - 1, DEC_SEQ)
        qs = (qf * jnp.exp(b - ref)).astype(BF16)
        ks = (k * jnp.exp(ref - b)).astype(BF16)
        kd = k * jnp.exp(bl - b)
        vb = v.astype(BF16)
        qd_ref[:, cs] = (qf * jnp.exp(b)).astype(BF16)
        for s in range(rows // blk):
            rs = slice(s * blk, (s + 1) * blk)
            sc = lax.dot_general(qs[rs], ks[rs], (((1,), (1,)), ((), ())),
                                 preferred_element_type=F32)
            oi_ref[rs, cs] = jnp.dot(jnp.where(causal, sc, 0.0).astype(BF16), vb[rs],
                                     preferred_element_type=F32)
        dec = jnp.exp(bl)
        d1 = dec.astype(BF16).astype(F32)
        d2 = (dec - d1).astype(BF16).astype(F32)
        d3 = (dec - d1 - d2).astype(BF16).astype(F32)
        tail = jnp.where(t == 0, d1, jnp.where(t == 1, d2, jnp.where(t == 2, d3, 0.0)))
        a_ref[hh, 0] = jnp.where(even, kd, _shift_rows(tail, -DEC_SEQ)).astype(BF16)
        a_ref[hh, 1] = jnp.where(even, _shift_rows(tail, DEC_SEQ), kd).astype(BF16)
        zero = jnp.zeros_like(vb)
        bv_ref[hh, 0] = jnp.where(even, vb, zero)
        bv_ref[hh, 1] = jnp.where(even, zero, vb)
        ga_ref[:, cs] = jax.nn.silu(z_ref[:, OFF_GA + hh * DK:OFF_GA + (hh + 1) * DK])


def _decode_prep_history(p_in_ref, rw_ref, p_out_ref):
    n_seq = p_in_ref.shape[1]
    for g, w in enumerate(POOL_WINDOWS):
        gc = slice(g * G_B, (g + 1) * G_B)
        acc = None
        suffix = {}
        for j in range(POOL_BUF - 1, -1, -1):
            e = p_in_ref[j, :, gc]
            acc = e if acc is None else acc + e
            suffix[j] = acc
        for tt in range(DEC_SEQ):
            j = POOL_BUF + 1 - w + tt
            rw_ref[g, pl.ds(tt, n_seq, stride=DEC_SEQ), :] = (
                suffix[j] if j < POOL_BUF else jnp.zeros((n_seq, G_B), F32))
    for j in range(POOL_BUF - DEC_SEQ):
        p_out_ref[j] = p_in_ref[j + DEC_SEQ]


def _decode_prep_pool(z_ref, wpool_ref, ps_ref, wpb_ref, ma_ref, gb_ref, p_out_ref,
                      rw_ref, u_ref, yb_ref):
    rows = z_ref.shape[0]
    n_seq = rows // DEC_SEQ
    t = lax.broadcasted_iota(jnp.int32, (rows, G_B), 0) & (DEC_SEQ - 1)
    for g, w in enumerate(POOL_WINDOWS):
        u = z_ref[:, OFF_U + g * G_B:OFF_U + (g + 1) * G_B]
        u_ref[g] = u
        cu = u
        sh = 1
        while sh < min(w, DEC_SEQ):
            cu = cu + jnp.where(t >= sh, _shift_rows(cu, -sh), 0.0)
            sh *= 2
        pooled = (rw_ref[g] + cu) * (1.0 / w) - u
        _pool_epilogue(pooled, z_ref, g, wpool_ref, ps_ref, yb_ref)
    gb_ref[...] = jax.nn.sigmoid(z_ref[:, OFF_MB:OFF_MB + D_MODEL]) * jnp.dot(
        yb_ref[...], wpb_ref[...], preferred_element_type=F32)
    ma_ref[...] = jax.nn.sigmoid(z_ref[:, OFF_MA:OFF_MA + D_MODEL])
    for tt in range(DEC_SEQ):
        for g in range(len(POOL_WINDOWS)):
            p_out_ref[POOL_BUF - DEC_SEQ + tt, :, g * G_B:(g + 1) * G_B] = (
                u_ref[g, pl.ds(tt, n_seq, stride=DEC_SEQ), :])


def _decode_prep(part, x_ref, p_in_ref, ng_ref, win_ref, lbl_ref, wpool_ref, ps_ref, wpb_ref,
                 a_ref, bv_ref, qd_ref, oi_ref, ga_ref, ma_ref, gb_ref, p_out_ref,
                 xs_ref, z_ref, rw_ref, u_ref, yb_ref):
    hgrn = functools.partial(_decode_prep_hgrn, z_ref=z_ref, lbl_ref=lbl_ref, a_ref=a_ref,
                             bv_ref=bv_ref, qd_ref=qd_ref, oi_ref=oi_ref, ga_ref=ga_ref)
    if part == 0:
        for i in range(x_ref.shape[0]):
            xs_ref[i * DEC_SEQ:(i + 1) * DEC_SEQ, :] = x_ref[i]
        _in_proj(xs_ref[...], ng_ref, win_ref, z_ref, 0, OFF_U)
        _decode_prep_history(p_in_ref, rw_ref, p_out_ref)
    elif part == 1:
        _in_proj(xs_ref[...], ng_ref, win_ref, z_ref, OFF_U, D_IN)
    elif part == 2:
        hgrn(range(0, N_HEADS - 1))
    else:
        hgrn(range(N_HEADS - 1, N_HEADS))
        _decode_prep_pool(z_ref, wpool_ref, ps_ref, wpb_ref, ma_ref, gb_ref, p_out_ref,
                          rw_ref, u_ref, yb_ref)


def _decode_state_step(a_ref, bv_ref, qd_ref, s_in_ref, s_out_ref, o_ref):
    n_pairs = BB_SAMPLE // 2
    r8 = lax.broadcasted_iota(jnp.int32, (PAIR_ROWS, DV), 0)
    even = r8 < DEC_SEQ
    ones = (jnp.where((r8 >= DEC_SEQ) & (r8 < DEC_SEQ + 3), 1.0, 0.0).astype(BF16),
            jnp.where(r8 < 3, 1.0, 0.0).astype(BF16))
    for hh in range(N_HEADS):
        for lp in range(n_pairs):
            rs = slice(lp * PAIR_ROWS, (lp + 1) * PAIR_ROWS)
            for e in range(2):
                rhs = jnp.concatenate([bv_ref[hh, e, rs, :], ones[e]], axis=1)
                upd = lax.dot_general(a_ref[hh, e, rs, :], rhs, (((0,), (0,)), ((), ())),
                                      preferred_element_type=F32)
                s_out_ref[2 * lp + e, hh] = (upd[:, DV:2 * DV] * s_in_ref[2 * lp + e, hh]
                                             + upd[:, 0:DV])
    for hh in range(N_HEADS):
        cs = slice(hh * DK, (hh + 1) * DK)
        for lp in range(n_pairs):
            rs = slice(lp * PAIR_ROWS, (lp + 1) * PAIR_ROWS)
            q8 = qd_ref[rs, cs]
            o0 = jnp.dot(q8, s_in_ref[2 * lp, hh].astype(BF16), preferred_element_type=F32)
            o1 = jnp.dot(q8, s_in_ref[2 * lp + 1, hh].astype(BF16), preferred_element_type=F32)
            o_ref[rs, cs] = o_ref[rs, cs] + jnp.where(even, o0, o1)


def _decode_finish(o_ref, ga_ref, ma_ref, gb_ref, xs_ref, hg_ref, wpa_ref, wout_ref,
                   fg_ref, y_ref, oa_ref, *, n_seq):
    for hh in range(N_HEADS):
        cs = slice(hh * DK, (hh + 1) * DK)
        oa_ref[:, cs] = (_rms(o_ref[:, cs], hg_ref[...]) * ga_ref[:, cs]).astype(BF16)
    y_a = jnp.dot(oa_ref[...], wpa_ref[...], preferred_element_type=F32)
    merged = ma_ref[...] * y_a + gb_ref[...]
    out = xs_ref[...] + jnp.dot(merged.astype(BF16), wout_ref[...], preferred_element_type=F32)
    y = _rms(out, fg_ref[...])
    for i in range(n_seq):
        y_ref[i] = y[i * DEC_SEQ:(i + 1) * DEC_SEQ, :]


def _state_copies(hbm_ref, buf_ref, sem, blk, slot, to_hbm):
    n = BB_SAMPLE // STATE_DMAS
    copies = []
    for c in range(STATE_DMAS):
        in_hbm = hbm_ref.at[pl.ds(blk * BB_SAMPLE + c * n, n)]
        in_vmem = buf_ref.at[slot, pl.ds(c * n, n)]
        src, dst = (in_vmem, in_hbm) if to_hbm else (in_hbm, in_vmem)
        copies.append(pltpu.make_async_copy(src, dst, sem.at[slot, c]))
    return copies


def _decode_kernel(xp_ref, p_in_ref, s_in_hbm,
                   ng_ref, win_ref, lbl_ref, wpool_ref, ps_ref, wpb_ref,
                   hg_ref, wpa_ref, wout_ref, fg_ref,
                   s_out_hbm, y_ref, p_out_ref,
                   a_all, bv_all, qd_all, o_all, ga_all, ma_all, gb_all, xs_all,
                   z_ref, rw_ref, u_ref, yb_ref, oa_ref, s_in_buf, s_out_buf, in_sem, out_sem,
                   *, n_prep):
    s = pl.program_id(0)
    prows = PREP_SEQS * DEC_SEQ
    brows = BB_SAMPLE * DEC_SEQ
    n_state = n_prep * PREP_PARTS

    def load(blk, slot):
        return _state_copies(s_in_hbm, s_in_buf, in_sem, blk, slot, to_hbm=False)

    def store(blk, slot):
        return _state_copies(s_out_hbm, s_out_buf, out_sem, blk, slot, to_hbm=True)

    def block_rows(blk):
        start = blk * prows
        return pl.ds(start if isinstance(start, int) else pl.multiple_of(start, prows), prows)

    def prep(part, blk):
        r = block_rows(blk)
        _decode_prep(part, xp_ref, p_in_ref, ng_ref, win_ref, lbl_ref, wpool_ref, ps_ref, wpb_ref,
                     a_all.at[:, :, r, :], bv_all.at[:, :, r, :], qd_all.at[r, :],
                     o_all.at[r, :], ga_all.at[r, :], ma_all.at[r, :], gb_all.at[r, :],
                     p_out_ref, xs_all.at[r, :], z_ref, rw_ref, u_ref, yb_ref)

    def finish(blk):
        f = block_rows(blk)
        _decode_finish(o_all.at[f, :], ga_all.at[f, :], ma_all.at[f, :], gb_all.at[f, :],
                       xs_all.at[f, :], hg_ref, wpa_ref, wout_ref, fg_ref, y_ref, oa_ref,
                       n_seq=PREP_SEQS)

    i = s - 1
    blk = lax.div(i, PREP_PARTS)
    part_now = lax.rem(i, PREP_PARTS)
    slot = lax.rem(i, 2)

    @pl.when(s == 0)
    def _():
        for b in range(2):
            for c in load(b, b):
                c.start()
        for part in range(PREP_PARTS):
            prep(part, 0)

    @pl.when(s > 0)
    def _():
        @pl.when((i > 0) & (i + 1 < n_state))
        def _():
            for c in load(i + 1, 1 - slot):
                c.start()
        for c in load(i, slot):
            c.wait()

        @pl.when(i >= 2)
        def _():
            for c in store(i - 2, slot):
                c.wait()
        r = pl.ds(pl.multiple_of(i * brows, brows), brows)
        _decode_state_step(a_all.at[:, :, r, :], bv_all.at[:, :, r, :], qd_all.at[r, :],
                           s_in_buf.at[slot], s_out_buf.at[slot], o_all.at[r, :])
        for c in store(i, slot):
            c.start()

    for part in range(PREP_PARTS):
        @pl.when((part_now == part) & (blk + 1 < n_prep))
        def _():
            prep(part, blk + 1)

    @pl.when((part_now == 0) & (i > 0))
    def _():
        finish(blk - 1)

    @pl.when(i == n_state - 1)
    def _():
        finish(n_prep - 1)
        for c in store(i - 1, 1 - slot) + store(i, slot):
            c.wait()


def _const_spec(shape):
    n = len(shape)
    return pl.BlockSpec(shape, lambda *_: (0,) * n, pipeline_mode=pl.Buffered(1))


def kernel(x_prompt, x_sample, state_hgrn, state_pool, norm_g, w_in, lb_logits, hgrn_norm_g,
           w_proj_a, w_pool, pool_scale, w_proj_b, w_out, final_norm_g):
    batch, seq, _ = x_prompt.shape
    dec_batch, dec_seq, _ = x_sample.shape
    assert norm_g.shape[0] == 1 and lb_logits.shape[0] == 2, "single-layer decoder only"
    assert seq % TM_PROMPT == 0 and TM_PROMPT % SUB_TILE == 0 and dec_batch % BB_SAMPLE == 0
    assert dec_seq == DEC_SEQ and PAIR_ROWS == SUBLANES and BB_SAMPLE % 2 == 0
    assert dec_batch % PREP_SEQS == 0 and (PREP_SEQS * DEC_SEQ) % SUB_TILE == 0
    assert PREP_SEQS == PREP_PARTS * BB_SAMPLE and BB_SAMPLE % STATE_DMAS == 0
    assert PAST_LEN >= max(POOL_WINDOWS)

    n_groups = len(POOL_WINDOWS)
    tm = TM_PROMPT
    nt = seq // tm
    assert w_in.shape[1:] == (2 * tm, D_IN) and w_proj_a.shape[1] == tm, "weight staging layout"
    hbm = pl.BlockSpec(memory_space=pl.ANY)
    bf16_shapes = [(D_MODEL, D_IN), (W_BRANCH, D_MODEL), (n_groups, G_B, G_B),
                   (W_BRANCH, D_MODEL), (D_MODEL, D_MODEL)]
    y_p, s_p, p_p, win_b, wpa_b, wpool_b, wpb_b, wout_b = pl.pallas_call(
        _prompt_kernel,
        grid=(batch, nt),
        in_specs=[pl.BlockSpec((1, tm, D_MODEL), lambda b, t: (b, t, 0)),
                  _const_spec((1, D_MODEL)), hbm, _const_spec((2, W_BRANCH)), _const_spec((1, DV)),
                  hbm, hbm, _const_spec((1, W_BRANCH)), hbm, hbm, _const_spec((1, D_MODEL))],
        out_specs=[
            pl.BlockSpec((1, tm, D_MODEL), lambda b, t: (b, t, 0)),
            pl.BlockSpec((1, N_HEADS, DK, DV), lambda b, t: (b, 0, 0, 0)),
            pl.BlockSpec((1, POOL_BUF, W_BRANCH), lambda b, t: (b, 0, 0)),
        ] + [hbm] * len(bf16_shapes),
        out_shape=[
            jax.ShapeDtypeStruct((batch, seq, D_MODEL), F32),
            jax.ShapeDtypeStruct((batch, N_HEADS, DK, DV), F32),
            jax.ShapeDtypeStruct((batch, POOL_BUF, W_BRANCH), F32),
        ] + [jax.ShapeDtypeStruct(s, BF16) for s in bf16_shapes],
        scratch_shapes=[
            pltpu.VMEM((tm, D_IN), F32),
            pltpu.VMEM((N_HEADS, DV, DK), F32),
            pltpu.VMEM((tm + 2 * SUBLANES, W_BRANCH), F32),
            pltpu.VMEM((N_HEADS, tm, (SUB_TILE // CHUNK) * DK), BF16),
            pltpu.VMEM((N_HEADS, tm // CHUNK, DK, DV), BF16),
            pltpu.VMEM((tm, W_BRANCH), BF16),
            pltpu.VMEM((tm, W_BRANCH), BF16),
            pltpu.VMEM((tm, W_BRANCH), BF16),
            pltpu.VMEM((tm, W_BRANCH), BF16),
            pltpu.VMEM((tm, W_BRANCH), BF16),
            pltpu.VMEM((tm, W_BRANCH), BF16),
        ] + [pltpu.VMEM(s, BF16) for s in bf16_shapes] + [
            pltpu.SemaphoreType.DMA((6 * WEIGHT_DMA_SPLIT + 1,)),
            pltpu.SemaphoreType.DMA((len(bf16_shapes),)),
        ],
        compiler_params=pltpu.CompilerParams(
            dimension_semantics=("arbitrary", "arbitrary"),
            vmem_limit_bytes=VMEM_LIMIT_BYTES),
        name="hgrn2_pool_prompt",
    )(x_prompt, norm_g, w_in[0], lb_logits, hgrn_norm_g, w_proj_a[0],
      w_pool[0].reshape(n_groups * G_B, G_B), pool_scale, w_proj_b[0], w_out[0],
      final_norm_g.reshape(1, D_MODEL))
    weights = (norm_g, win_b, lb_logits, hgrn_norm_g, wpa_b, wpool_b, pool_scale, wpb_b, wout_b,
               final_norm_g.reshape(1, D_MODEL))

    rows = dec_batch * DEC_SEQ
    pool_in = jnp.transpose(state_pool[0], (1, 0, 2))
    ng, win_b, lbl, hg, wpa_b, wpool_b, ps, wpb_b, wout_b, fg = weights
    pseq = PREP_SEQS
    prows = pseq * DEC_SEQ
    n_prep = dec_batch // pseq
    bb = BB_SAMPLE

    def prep_blk(s):
        return jnp.minimum((s + PREP_PARTS - 1) // PREP_PARTS, n_prep - 1)

    def finish_blk(s):
        return jnp.minimum(jnp.maximum(s - 2, 0) // PREP_PARTS, n_prep - 1)

    s_s, y_s, pool_out = pl.pallas_call(
        functools.partial(_decode_kernel, n_prep=n_prep),
        grid=(1 + dec_batch // bb,),
        in_specs=[
            pl.BlockSpec((pseq, DEC_SEQ, D_MODEL), lambda s: (prep_blk(s), 0, 0)),
            pl.BlockSpec((POOL_BUF, pseq, W_BRANCH), lambda s: (0, prep_blk(s), 0),
                         pipeline_mode=pl.Buffered(1)),
            hbm,
            _const_spec(ng.shape), _const_spec(win_b.shape), _const_spec(lbl.shape),
            _const_spec(wpool_b.shape), _const_spec(ps.shape), _const_spec(wpb_b.shape),
            _const_spec(hg.shape), _const_spec(wpa_b.shape), _const_spec(wout_b.shape),
            _const_spec(fg.shape),
        ],
        out_specs=[
            hbm,
            pl.BlockSpec((pseq, DEC_SEQ, D_MODEL), lambda s: (finish_blk(s), 0, 0)),
            pl.BlockSpec((POOL_BUF, pseq, W_BRANCH), lambda s: (0, prep_blk(s), 0)),
        ],
        out_shape=[
            jax.ShapeDtypeStruct((dec_batch, N_HEADS, DK, DV), F32),
            jax.ShapeDtypeStruct(x_sample.shape, F32),
            jax.ShapeDtypeStruct((POOL_BUF, dec_batch, W_BRANCH), F32),
        ],
        scratch_shapes=[
            pltpu.VMEM((N_HEADS, 2, rows, DK), BF16),
            pltpu.VMEM((N_HEADS, 2, rows, DV), BF16),
            pltpu.VMEM((rows, W_BRANCH), BF16),
            pltpu.VMEM((rows, W_BRANCH), F32),
            pltpu.VMEM((rows, W_BRANCH), F32),
            pltpu.VMEM((rows, D_MODEL), F32),
            pltpu.VMEM((rows, D_MODEL), F32),
            pltpu.VMEM((rows, D_MODEL), F32),
            pltpu.VMEM((prows, D_IN), F32),
            pltpu.VMEM((len(POOL_WINDOWS), prows, G_B), F32),
            pltpu.VMEM((len(POOL_WINDOWS), prows, G_B), F32),
            pltpu.VMEM((prows, W_BRANCH), BF16),
            pltpu.VMEM((prows, W_BRANCH), BF16),
            pltpu.VMEM((2, bb, N_HEADS, DK, DV), F32),
            pltpu.VMEM((2, bb, N_HEADS, DK, DV), F32),
            pltpu.SemaphoreType.DMA((2, STATE_DMAS)),
            pltpu.SemaphoreType.DMA((2, STATE_DMAS)),
        ],
        compiler_params=pltpu.CompilerParams(
            dimension_semantics=("arbitrary",), vmem_limit_bytes=DECODE_VMEM_LIMIT_BYTES),
        name="hgrn2_pool_decode",
    )(x_sample, pool_in, state_hgrn[0], ng, win_b, lbl, wpool_b, ps, wpb_b,
      hg, wpa_b, wout_b, fg)

    p_s = jnp.transpose(pool_out, (1, 0, 2))
    return (y_p, y_s, s_p[None], p_p[None], s_s[None], p_s[None])
```

```python
import functools

import jax
import jax.numpy as jnp
from jax import lax
from jax.experimental import pallas as pl
from jax.experimental.pallas import tpu as pltpu

F32 = jnp.float32
BF16 = jnp.bfloat16

D_MODEL = 1024
W_BRANCH = 512
N_HEADS = 4
DK = 128
DV = 128
CHUNK = 32
POOL_WINDOWS = (2, 4, 8, 16)
G_B = 128
POOL_BUF = 15
PAST_LEN = 16384
EPS = 1e-6
D_IN = 4 * W_BRANCH + 2 * W_BRANCH + 2 * D_MODEL
OFF_Q, OFF_F, OFF_I, OFF_GA = 0, 512, 1024, 1536
OFF_U, OFF_GB, OFF_MA, OFF_MB = 2048, 2560, 3072, 4096

SUBLANES = 8
VMEM_LIMIT_BYTES = 56 * 1024 * 1024
DECODE_VMEM_LIMIT_BYTES = 60 * 1024 * 1024

TM_PROMPT = 512
SUB_TILE = 256
PREP_SEQS = 64
BB_SAMPLE = 16
STATE_DMAS = 16
DEC_SEQ = 4
PAIR_ROWS = 2 * DEC_SEQ
PREP_PARTS = 4


def _rms(x, g):
    ms = jnp.mean(x * x, axis=-1, keepdims=True)
    return x * lax.rsqrt(ms + EPS) * g


def _lower_bound(lbl_ref):
    l0 = lbl_ref[0:1, :]
    l1 = lbl_ref[1:2, :]
    m = jnp.maximum(l0, l1)
    e0 = jnp.exp(l0 - m)
    e1 = jnp.exp(l1 - m)
    return e0 / (e0 + e1)


def _in_proj(x, ng_ref, win_ref, z_ref, c0=0, c1=D_IN):
    h = _rms(x, ng_ref[...]).astype(BF16)
    z_ref[:, c0:c1] = jnp.dot(h, win_ref[:, c0:c1], preferred_element_type=F32)


def _hgrn_inputs(z_ref, lb, hh):
    c = hh * DK
    q = z_ref[:, OFF_Q + c:OFF_Q + c + DK]
    f = z_ref[:, OFF_F + c:OFF_F + c + DK]
    v = z_ref[:, OFF_I + c:OFF_I + c + DK]
    lbh = lb[:, c:c + DK]
    fg = lbh + (1.0 - lbh) * jax.nn.sigmoid(f)
    logf = jnp.log(fg)
    k = 1.0 - fg
    qf = jax.nn.silu(q) * (DK ** -0.5)
    return qf, k, v, logf


def _group_cumsum(x, group):
    row = lax.broadcasted_iota(jnp.int32, x.shape, 0) & (group - 1)
    s = 1
    while s < group:
        x = x + jnp.where(row >= s, pltpu.roll(x, s, 0), 0.0)
        s *= 2
    return x


def _hgrn_epilogue(o, z_ref, hg_ref, hh, oa_ref):
    c = hh * DK
    ga = z_ref[:, OFF_GA + c:OFF_GA + c + DK]
    on = _rms(o, hg_ref[...]) * jax.nn.silu(ga)
    oa_ref[:, c:c + DK] = on.astype(BF16)


def _pool_epilogue(pooled, z_ref, g, wpool_ref, ps_ref, yb_ref):
    c = g * G_B
    mixed = jnp.dot(pooled.astype(BF16), wpool_ref[g], preferred_element_type=F32)
    gb = z_ref[:, OFF_GB + c:OFF_GB + c + G_B]
    yb = mixed * ps_ref[:, c:c + G_B] * jax.nn.silu(gb)
    yb_ref[:, c:c + G_B] = yb.astype(BF16)


def _weight_copies(hbm_refs, stage_ref, sem):
    win_hbm, wpa_hbm, wpool_hbm, wpb_hbm, wout_hbm = hbm_refs
    rows = stage_ref.shape[0]
    small = [
        (wpa_hbm, stage_ref.at[:, 0:D_MODEL]),
        (wpb_hbm, stage_ref.at[:, D_MODEL:2 * D_MODEL]),
        (wout_hbm.at[0:rows, :], stage_ref.at[:, 2 * D_MODEL:3 * D_MODEL]),
        (wout_hbm.at[rows:2 * rows, :], stage_ref.at[:, 3 * D_MODEL:4 * D_MODEL]),
        (wpool_hbm, stage_ref.at[:, 4 * D_MODEL:4 * D_MODEL + G_B]),
    ]
    small = [pltpu.make_async_copy(src, dst, sem.at[i]) for i, (src, dst) in enumerate(small)]
    blk = rows // 2

    def win(k):
        slot = k % 2
        return pltpu.make_async_copy(win_hbm.at[k * blk:(k + 1) * blk, :],
                                     stage_ref.at[slot * blk:(slot + 1) * blk, :],
                                     sem.at[len(small) + slot])
    return small, win, blk


def _weight_exports(vmem_refs, out_refs, sem):
    return [pltpu.make_async_copy(src, dst, sem.at[i])
            for i, (src, dst) in enumerate(zip(vmem_refs, out_refs))]


def _prompt_kernel(x_ref, ng_ref, win_hbm, lbl_ref, hg_ref, wpa_hbm, wpool_hbm, ps_ref,
                   wpb_hbm, wout_hbm, fg_ref,
                   y_ref, s_out_ref, p_out_ref, win_out, wpa_out, wpool_out, wpb_out, wout_out,
                   z_ref, st_ref, ext_ref, kx_ref, sn_ref, qs_ref, ks_ref, qd_ref, vb_ref,
                   oa_ref, yb_ref, win_ref, wpa_ref, wpool_ref, wpb_ref, wout_ref,
                   load_sem, export_sem):
    tm = TM_PROMPT
    sub = SUB_TILE
    n_sub = tm // sub
    nc = tm // CHUNK
    ncs = sub // CHUNK
    t = pl.program_id(1)
    first = t == 0
    hdr = 2 * SUBLANES
    bf16_weights = (win_ref, wpa_ref, wpool_ref, wpb_ref, wout_ref)
    exports = _weight_exports(bf16_weights, (win_out, wpa_out, wpool_out, wpb_out, wout_out),
                              export_sem)

    @pl.when((pl.program_id(0) == 0) & first)
    def _():
        st_ref[...] = jnp.zeros_like(st_ref)
        ext_ref[...] = jnp.zeros_like(ext_ref)
        kx_ref[...] = jnp.zeros_like(kx_ref)
        small, win, blk = _weight_copies((win_hbm, wpa_hbm, wpool_hbm, wpb_hbm, wout_hbm),
                                         z_ref, load_sem)
        for c in small:
            c.start()
        for c in small:
            c.wait()
        wpa_ref[...] = z_ref[:, 0:D_MODEL].astype(BF16)
        wpb_ref[...] = z_ref[:, D_MODEL:2 * D_MODEL].astype(BF16)
        wout_ref[0:tm, :] = z_ref[:, 2 * D_MODEL:3 * D_MODEL].astype(BF16)
        wout_ref[tm:2 * tm, :] = z_ref[:, 3 * D_MODEL:4 * D_MODEL].astype(BF16)
        for g in range(len(POOL_WINDOWS)):
            wpool_ref[g] = z_ref[g * G_B:(g + 1) * G_B,
                                 4 * D_MODEL:4 * D_MODEL + G_B].astype(BF16)
        n_blk = D_MODEL // blk
        win(0).start()
        win(1).start()
        for k in range(n_blk):
            slot = k % 2
            win(k).wait()
            win_ref[k * blk:(k + 1) * blk, :] = z_ref[slot * blk:(slot + 1) * blk, :].astype(BF16)
            if k + 2 < n_blk:
                win(k + 2).start()
        for c in exports:
            c.start()

    @pl.when((pl.program_id(0) == pl.num_programs(0) - 1) & (t == pl.num_programs(1) - 1))
    def _():
        for c in exports:
            c.wait()

    x = x_ref[0]
    _in_proj(x, ng_ref, win_ref, z_ref)
    lb = _lower_bound(lbl_ref)

    ri = lax.broadcasted_iota(jnp.int32, (sub, sub), 0)
    ci = lax.broadcasted_iota(jnp.int32, (sub, sub), 1)
    causal = ((ri // CHUNK) == (ci // CHUNK)) & (ci <= ri)

    heads = range(N_HEADS)
    units = [(hh, s) for hh in heads for s in range(n_sub)]

    def rows(s):
        return slice(s * sub, (s + 1) * sub)

    def cols(hh):
        return slice(hh * DK, (hh + 1) * DK)

    dec = []
    for hh in heads:
        qf, k, v, logf = _hgrn_inputs(z_ref, lb, hh)
        b = _group_cumsum(logf, CHUNK)
        b3 = b.reshape(nc, CHUNK, DK)
        ref = b3[:, CHUNK // 2:CHUNK // 2 + 1, :]
        bl = b3[:, CHUNK - 1:CHUNK, :]
        q3 = qf.reshape(nc, CHUNK, DK)
        k3 = k.reshape(nc, CHUNK, DK)
        qs_ref[:, cols(hh)] = (q3 * jnp.exp(b3 - ref)).reshape(tm, DK).astype(BF16)
        ks_ref[:, cols(hh)] = (k3 * jnp.exp(ref - b3)).reshape(tm, DK).astype(BF16)
        qd_ref[:, cols(hh)] = (q3 * jnp.exp(b3)).reshape(tm, DK).astype(BF16)
        vb_ref[:, cols(hh)] = v.astype(BF16)
        kd = (k3 * jnp.exp(bl - b3)).reshape(tm, DK).astype(BF16)
        dec.append(jnp.exp(bl))
        for c in range(nc):
            cc = c % ncs
            kx_ref[hh, c * CHUNK:(c + 1) * CHUNK, cc * DK:(cc + 1) * DK] = (
                kd[c * CHUNK:(c + 1) * CHUNK, :])

    ext_ref[0:hdr, :] = jnp.where(first, 0.0, ext_ref[tm:tm + hdr, :])
    ext_ref[hdr:hdr + tm, :] = z_ref[:, OFF_U:OFF_U + W_BRANCH]
    pos1 = t * tm + lax.broadcasted_iota(jnp.int32, (tm, 1), 0) + 1
    for g, w in enumerate(POOL_WINDOWS):
        c = g * G_B
        s = ext_ref[:, c:c + G_B]
        sh = 1
        while sh < w:
            s = s + pltpu.roll(s, sh, 0)
            sh *= 2
        inv = jnp.where(pos1 >= w, 1.0 / w, 1.0 / pos1.astype(F32))
        pooled = s[hdr:, :] * inv - z_ref[:, OFF_U + c:OFF_U + c + G_B]
        _pool_epilogue(pooled, z_ref, g, wpool_ref, ps_ref, yb_ref)
    p_out_ref[0] = ext_ref[hdr + tm - POOL_BUF:hdr + tm, :]

    sc = {(hh, s): lax.dot_general(qs_ref[rows(s), cols(hh)], ks_ref[rows(s), cols(hh)],
                                   (((1,), (1,)), ((), ())), preferred_element_type=F32)
          for hh, s in units}
    ut = {(hh, s): lax.dot_general(vb_ref[rows(s), cols(hh)], kx_ref[hh, rows(s), :],
                                   (((0,), (0,)), ((), ())), preferred_element_type=F32)
          for hh, s in units}
    o = {(hh, s): jnp.dot(jnp.where(causal, sc[hh, s], 0.0).astype(BF16),
                          vb_ref[rows(s), cols(hh)], preferred_element_type=F32)
         for hh, s in units}
    for hh in heads:
        st = jnp.where(first, 0.0, st_ref[hh])
        for c in range(nc):
            sn_ref[hh, c] = st.T.astype(BF16)
            st = dec[hh][c] * st + ut[hh, c // ncs][:, (c % ncs) * DK:(c % ncs + 1) * DK]
        st_ref[hh] = st
        s_out_ref[0, hh] = st.T
    wcol = D_MODEL // N_HEADS
    y_b = []
    for hh in heads:
        o_inter = [jnp.dot(qd_ref[c * CHUNK:(c + 1) * CHUNK, cols(hh)], sn_ref[hh, c],
                           preferred_element_type=F32) for c in range(nc)]
        y_b.append(jnp.dot(yb_ref[...], wpb_ref[:, hh * wcol:(hh + 1) * wcol],
                           preferred_element_type=F32))
        o_hh = jnp.concatenate([o[hh, s] for s in range(n_sub)], axis=0)
        _hgrn_epilogue(o_hh + jnp.concatenate(o_inter, axis=0), z_ref, hg_ref, hh, oa_ref)
    gated_b = jax.nn.sigmoid(z_ref[:, OFF_MB:OFF_MB + D_MODEL]) * jnp.concatenate(y_b, axis=1)

    y_a = jnp.dot(oa_ref[...], wpa_ref[...], preferred_element_type=F32)
    merged = jax.nn.sigmoid(z_ref[:, OFF_MA:OFF_MA + D_MODEL]) * y_a + gated_b
    out = x + jnp.dot(merged.astype(BF16), wout_ref[...], preferred_element_type=F32)
    y_ref[0] = _rms(out, fg_ref[...])


def _shift_rows(x, k):
    n = x.shape[0]
    return x if k % n == 0 else pltpu.roll(x, (-k) % n, 0)


def _group_bcast(x, j, group):
    t = lax.broadcasted_iota(jnp.int32, x.shape, 0) & (group - 1)
    out = _shift_rows(x, j - (group - 1))
    for tt in range(group - 2, -1, -1):
        out = jnp.where(t == tt, _shift_rows(x, j - tt), out)
    return out


def _decode_prep_hgrn(heads, z_ref, lbl_ref, a_ref, bv_ref, qd_ref, oi_ref, ga_ref):
    rows = z_ref.shape[0]
    lb = _lower_bound(lbl_ref)
    r = lax.broadcasted_iota(jnp.int32, (rows, DK), 0)
    t = r & (DEC_SEQ - 1)
    even = (r & (PAIR_ROWS - 1)) < DEC_SEQ
    blk = SUB_TILE
    ri = lax.broadcasted_iota(jnp.int32, (blk, blk), 0)
    ci = lax.broadcasted_iota(jnp.int32, (blk, blk), 1)
    causal = ((ri // DEC_SEQ) == (ci // DEC_SEQ)) & (ci <= ri)

    for hh in heads:
        cs = slice(hh * DK, (hh + 1) * DK)
        qf, k, v, logf = _hgrn_inputs(z_ref, lb, hh)
        b = _group_cumsum(logf, DEC_SEQ)
        ref = _group_bcast(b, DEC_SEQ // 2, DEC_SEQ)
        bl = _group_bcast(b, DEC_SEQ - 1, DEC_SEQ)
        qs = (qf * jnp.exp(b - ref)).astype(BF16)
        ks = (k * jnp.exp(ref - b)).astype(BF16)
        kd = k * jnp.exp(bl - b)
        vb = v.astype(BF16)
        qd_ref[:, cs] = (qf * jnp.exp(b)).astype(BF16)
        for s in range(rows // blk):
            rs = slice(s * blk, (s + 1) * blk)
            sc = lax.dot_general(qs[rs], ks[rs], (((1,), (1,)), ((), ())),
                                 preferred_element_type=F32)
            oi_ref[rs, cs] = jnp.dot(jnp.where(causal, sc, 0.0).astype(BF16), vb[rs],
                                     preferred_element_type=F32)
        dec = jnp.exp(bl)
        d1 = dec.astype(BF16).astype(F32)
        d2 = (dec - d1).astype(BF16).astype(F32)
        d3 = (dec - d1 - d2).astype(BF16).astype(F32)
        tail = jnp.where(t == 0, d1, jnp.where(t == 1, d2, jnp.where(t == 2, d3, 0.0)))
        a_ref[hh, 0] = jnp.where(even, kd, _shift_rows(tail, -DEC_SEQ)).astype(BF16)
        a_ref[hh, 1] = jnp.where(even, _shift_rows(tail, DEC_SEQ), kd).astype(BF16)
        zero = jnp.zeros_like(vb)
        bv_ref[hh, 0] = jnp.where(even, vb, zero)
        bv_ref[hh, 1] = jnp.where(even, zero, vb)
        ga_ref[:, cs] = jax.nn.silu(z_ref[:, OFF_GA + hh * DK:OFF_GA + (hh + 1) * DK])


def _decode_prep_history(p_in_ref, rw_ref, p_out_ref):
    n_seq = p_in_ref.shape[1]
    for g, w in enumerate(POOL_WINDOWS):
        gc = slice(g * G_B, (g + 1) * G_B)
        acc = None
        suffix = {}
        for j in range(POOL_BUF - 1, -1, -1):
            e = p_in_ref[j, :, gc]
            acc = e if acc is None else acc + e
            suffix[j] = acc
        for tt in range(DEC_SEQ):
            j = POOL_BUF + 1 - w + tt
            rw_ref[g, pl.ds(tt, n_seq, stride=DEC_SEQ), :] = (
                suffix[j] if j < POOL_BUF else jnp.zeros((n_seq, G_B), F32))
    for j in range(POOL_BUF - DEC_SEQ):
        p_out_ref[j] = p_in_ref[j + DEC_SEQ]


def _decode_prep_pool(z_ref, wpool_ref, ps_ref, wpb_ref, ma_ref, gb_ref, p_out_ref,
                      rw_ref, u_ref, yb_ref):
    rows = z_ref.shape[0]
    n_seq = rows // DEC_SEQ
    t = lax.broadcasted_iota(jnp.int32, (rows, G_B), 0) & (DEC_SEQ - 1)
    for g, w in enumerate(POOL_WINDOWS):
        u = z_ref[:, OFF_U + g * G_B:OFF_U + (g + 1) * G_B]
        u_ref[g] = u
        cu = u
        sh = 1
        while sh < min(w, DEC_SEQ):
            cu = cu + jnp.where(t >= sh, _shift_rows(cu, -sh), 0.0)
            sh *= 2
        pooled = (rw_ref[g] + cu) * (1.0 / w) - u
        _pool_epilogue(pooled, z_ref, g, wpool_ref, ps_ref, yb_ref)
    gb_ref[...] = jax.nn.sigmoid(z_ref[:, OFF_MB:OFF_MB + D_MODEL]) * jnp.dot(
        yb_ref[...], wpb_ref[...], preferred_element_type=F32)
    ma_ref[...] = jax.nn.sigmoid(z_ref[:, OFF_MA:OFF_MA + D_MODEL])
    for tt in range(DEC_SEQ):
        for g in range(len(POOL_WINDOWS)):
            p_out_ref[POOL_BUF - DEC_SEQ + tt, :, g * G_B:(g + 1) * G_B] = (
                u_ref[g, pl.ds(tt, n_seq, stride=DEC_SEQ), :])


def _decode_prep(part, x_ref, p_in_ref, ng_ref, win_ref, lbl_ref, wpool_ref, ps_ref, wpb_ref,
                 a_ref, bv_ref, qd_ref, oi_ref, ga_ref, ma_ref, gb_ref, p_out_ref,
                 xs_ref, z_ref, rw_ref, u_ref, yb_ref):
    hgrn = functools.partial(_decode_prep_hgrn, z_ref=z_ref, lbl_ref=lbl_ref, a_ref=a_ref,
                             bv_ref=bv_ref, qd_ref=qd_ref, oi_ref=oi_ref, ga_ref=ga_ref)
    if part == 0:
        for i in range(x_ref.shape[0]):
            xs_ref[i * DEC_SEQ:(i + 1) * DEC_SEQ, :] = x_ref[i]
        _in_proj(xs_ref[...], ng_ref, win_ref, z_ref, 0, OFF_U)
        _decode_prep_history(p_in_ref, rw_ref, p_out_ref)
    elif part == 1:
        _in_proj(xs_ref[...], ng_ref, win_ref, z_ref, OFF_U, D_IN)
    elif part == 2:
        hgrn(range(0, N_HEADS - 1))
    else:
        hgrn(range(N_HEADS - 1, N_HEADS))
        _decode_prep_pool(z_ref, wpool_ref, ps_ref, wpb_ref, ma_ref, gb_ref, p_out_ref,
                          rw_ref, u_ref, yb_ref)


def _decode_state_step(a_ref, bv_ref, qd_ref, s_in_ref, s_out_ref, o_ref):
    n_pairs = BB_SAMPLE // 2
    r8 = lax.broadcasted_iota(jnp.int32, (PAIR_ROWS, DV), 0)
    even = r8 < DEC_SEQ
    ones = (jnp.where((r8 >= DEC_SEQ) & (r8 < DEC_SEQ + 3), 1.0, 0.0).astype(BF16),
            jnp.where(r8 < 3, 1.0, 0.0).astype(BF16))
    for hh in range(N_HEADS):
        for lp in range(n_pairs):
            rs = slice(lp * PAIR_ROWS, (lp + 1) * PAIR_ROWS)
            for e in range(2):
                rhs = jnp.concatenate([bv_ref[hh, e, rs, :], ones[e]], axis=1)
                upd = lax.dot_general(a_ref[hh, e, rs, :], rhs, (((0,), (0,)), ((), ())),
                                      preferred_element_type=F32)
                s_out_ref[2 * lp + e, hh] = (upd[:, DV:2 * DV] * s_in_ref[2 * lp + e, hh]
                                             + upd[:, 0:DV])
    for hh in range(N_HEADS):
        cs = slice(hh * DK, (hh + 1) * DK)
        for lp in range(n_pairs):
            rs = slice(lp * PAIR_ROWS, (lp + 1) * PAIR_ROWS)
            q8 = qd_ref[rs, cs]
            o0 = jnp.dot(q8, s_in_ref[2 * lp, hh].astype(BF16), preferred_element_type=F32)
            o1 = jnp.dot(q8, s_in_ref[2 * lp + 1, hh].astype(BF16), preferred_element_type=F32)
            o_ref[rs, cs] = o_ref[rs, cs] + jnp.where(even, o0, o1)


def _decode_finish(o_ref, ga_ref, ma_ref, gb_ref, xs_ref, hg_ref, wpa_ref, wout_ref,
                   fg_ref, y_ref, oa_ref, *, n_seq):
    for hh in range(N_HEADS):
        cs = slice(hh * DK, (hh + 1) * DK)
        oa_ref[:, cs] = (_rms(o_ref[:, cs], hg_ref[...]) * ga_ref[:, cs]).astype(BF16)
    y_a = jnp.dot(oa_ref[...], wpa_ref[...], preferred_element_type=F32)
    merged = ma_ref[...] * y_a + gb_ref[...]
    out = xs_ref[...] + jnp.dot(merged.astype(BF16), wout_ref[...], preferred_element_type=F32)
    y = _rms(out, fg_ref[...])
    for i in range(n_seq):
        y_ref[i] = y[i * DEC_SEQ:(i + 1) * DEC_SEQ, :]


def _state_copies(hbm_ref, buf_ref, sem, blk, slot, to_hbm):
    n = BB_SAMPLE // STATE_DMAS
    copies = []
    for c in range(STATE_DMAS):
        in_hbm = hbm_ref.at[pl.ds(blk * BB_SAMPLE + c * n, n)]
        in_vmem = buf_ref.at[slot, pl.ds(c * n, n)]
        src, dst = (in_vmem, in_hbm) if to_hbm else (in_hbm, in_vmem)
        copies.append(pltpu.make_async_copy(src, dst, sem.at[slot, c]))
    return copies


def _decode_kernel(xp_ref, p_in_ref, s_in_hbm,
                   ng_ref, win_ref, lbl_ref, wpool_ref, ps_ref, wpb_ref,
                   hg_ref, wpa_ref, wout_ref, fg_ref,
                   s_out_hbm, y_ref, p_out_ref,
                   a_all, bv_all, qd_all, o_all, ga_all, ma_all, gb_all, xs_all,
                   z_ref, rw_ref, u_ref, yb_ref, oa_ref, s_in_buf, s_out_buf, in_sem, out_sem,
                   *, n_prep):
    s = pl.program_id(0)
    prows = PREP_SEQS * DEC_SEQ
    brows = BB_SAMPLE * DEC_SEQ
    n_state = n_prep * PREP_PARTS

    def load(blk, slot):
        return _state_copies(s_in_hbm, s_in_buf, in_sem, blk, slot, to_hbm=False)

    def store(blk, slot):
        return _state_copies(s_out_hbm, s_out_buf, out_sem, blk, slot, to_hbm=True)

    def block_rows(blk):
        start = blk * prows
        return pl.ds(start if isinstance(start, int) else pl.multiple_of(start, prows), prows)

    def prep(part, blk):
        r = block_rows(blk)
        _decode_prep(part, xp_ref, p_in_ref, ng_ref, win_ref, lbl_ref, wpool_ref, ps_ref, wpb_ref,
                     a_all.at[:, :, r, :], bv_all.at[:, :, r, :], qd_all.at[r, :],
                     o_all.at[r, :], ga_all.at[r, :], ma_all.at[r, :], gb_all.at[r, :],
                     p_out_ref, xs_all.at[r, :], z_ref, rw_ref, u_ref, yb_ref)

    def finish(blk):
        f = block_rows(blk)
        _decode_finish(o_all.at[f, :], ga_all.at[f, :], ma_all.at[f, :], gb_all.at[f, :],
                       xs_all.at[f, :], hg_ref, wpa_ref, wout_ref, fg_ref, y_ref, oa_ref,
                       n_seq=PREP_SEQS)

    i = s - 1
    blk = lax.div(i, PREP_PARTS)
    part_now = lax.rem(i, PREP_PARTS)
    slot = lax.rem(i, 2)

    @pl.when(s == 0)
    def _():
        for b in range(2):
            for c in load(b, b):
                c.start()
        for part in range(PREP_PARTS):
            prep(part, 0)

    @pl.when(s > 0)
    def _():
        @pl.when((i > 0) & (i + 1 < n_state))
        def _():
            for c in load(i + 1, 1 - slot):
                c.start()
        for c in load(i, slot):
            c.wait()

        @pl.when(i >= 2)
        def _():
            for c in store(i - 2, slot):
                c.wait()
        r = pl.ds(pl.multiple_of(i * brows, brows), brows)
        _decode_state_step(a_all.at[:, :, r, :], bv_all.at[:, :, r, :], qd_all.at[r, :],
                           s_in_buf.at[slot], s_out_buf.at[slot], o_all.at[r, :])
        for c in store(i, slot):
            c.start()

    for part in range(PREP_PARTS):
        @pl.when((part_now == part) & (blk + 1 < n_prep))
        def _():
            prep(part, blk + 1)

    @pl.when((part_now == 0) & (i > 0))
    def _():
        finish(blk - 1)

    @pl.when(i == n_state - 1)
    def _():
        finish(n_prep - 1)
        for c in store(i - 1, 1 - slot) + store(i, slot):
            c.wait()


def _const_spec(shape):
    n = len(shape)
    return pl.BlockSpec(shape, lambda *_: (0,) * n, pipeline_mode=pl.Buffered(1))


def kernel(x_prompt, x_sample, state_hgrn, state_pool, norm_g, w_in, lb_logits, hgrn_norm_g,
           w_proj_a, w_pool, pool_scale, w_proj_b, w_out, final_norm_g):
    batch, seq, _ = x_prompt.shape
    dec_batch, dec_seq, _ = x_sample.shape
    assert norm_g.shape[0] == 1 and lb_logits.shape[0] == 2, "single-layer decoder only"
    assert seq % TM_PROMPT == 0 and TM_PROMPT % SUB_TILE == 0 and dec_batch % BB_SAMPLE == 0
    assert dec_seq == DEC_SEQ and PAIR_ROWS == SUBLANES and BB_SAMPLE % 2 == 0
    assert dec_batch % PREP_SEQS == 0 and (PREP_SEQS * DEC_SEQ) % SUB_TILE == 0
    assert PREP_SEQS == PREP_PARTS * BB_SAMPLE and BB_SAMPLE % STATE_DMAS == 0
    assert PAST_LEN >= max(POOL_WINDOWS)

    n_groups = len(POOL_WINDOWS)
    tm = TM_PROMPT
    nt = seq // tm
    assert w_in.shape[1:] == (2 * tm, D_IN) and w_proj_a.shape[1] == tm, "weight staging layout"
    hbm = pl.BlockSpec(memory_space=pl.ANY)
    bf16_shapes = [(D_MODEL, D_IN), (W_BRANCH, D_MODEL), (n_groups, G_B, G_B),
                   (W_BRANCH, D_MODEL), (D_MODEL, D_MODEL)]
    y_p, s_p, p_p, win_b, wpa_b, wpool_b, wpb_b, wout_b = pl.pallas_call(
        _prompt_kernel,
        grid=(batch, nt),
        in_specs=[pl.BlockSpec((1, tm, D_MODEL), lambda b, t: (b, t, 0)),
                  _const_spec((1, D_MODEL)), hbm, _const_spec((2, W_BRANCH)), _const_spec((1, DV)),
                  hbm, hbm, _const_spec((1, W_BRANCH)), hbm, hbm, _const_spec((1, D_MODEL))],
        out_specs=[
            pl.BlockSpec((1, tm, D_MODEL), lambda b, t: (b, t, 0)),
            pl.BlockSpec((1, N_HEADS, DK, DV), lambda b, t: (b, 0, 0, 0)),
            pl.BlockSpec((1, POOL_BUF, W_BRANCH), lambda b, t: (b, 0, 0)),
        ] + [hbm] * len(bf16_shapes),
        out_shape=[
            jax.ShapeDtypeStruct((batch, seq, D_MODEL), F32),
            jax.ShapeDtypeStruct((batch, N_HEADS, DK, DV), F32),
            jax.ShapeDtypeStruct((batch, POOL_BUF, W_BRANCH), F32),
        ] + [jax.ShapeDtypeStruct(s, BF16) for s in bf16_shapes],
        scratch_shapes=[
            pltpu.VMEM((tm, D_IN), F32),
            pltpu.VMEM((N_HEADS, DV, DK), F32),
            pltpu.VMEM((tm + 2 * SUBLANES, W_BRANCH), F32),
            pltpu.VMEM((N_HEADS, tm, (SUB_TILE // CHUNK) * DK), BF16),
            pltpu.VMEM((N_HEADS, tm // CHUNK, DK, DV), BF16),
            pltpu.VMEM((tm, W_BRANCH), BF16),
            pltpu.VMEM((tm, W_BRANCH), BF16),
            pltpu.VMEM((tm, W_BRANCH), BF16),
            pltpu.VMEM((tm, W_BRANCH), BF16),
            pltpu.VMEM((tm, W_BRANCH), BF16),
            pltpu.VMEM((tm, W_BRANCH), BF16),
        ] + [pltpu.VMEM(s, BF16) for s in bf16_shapes] + [
            pltpu.SemaphoreType.DMA((7,)),
            pltpu.SemaphoreType.DMA((len(bf16_shapes),)),
        ],
        compiler_params=pltpu.CompilerParams(
            dimension_semantics=("arbitrary", "arbitrary"),
            vmem_limit_bytes=VMEM_LIMIT_BYTES),
        name="hgrn2_pool_prompt",
    )(x_prompt, norm_g, w_in[0], lb_logits, hgrn_norm_g, w_proj_a[0],
      w_pool[0].reshape(n_groups * G_B, G_B), pool_scale, w_proj_b[0], w_out[0],
      final_norm_g.reshape(1, D_MODEL))
    weights = (norm_g, win_b, lb_logits, hgrn_norm_g, wpa_b, wpool_b, pool_scale, wpb_b, wout_b,
               final_norm_g.reshape(1, D_MODEL))

    rows = dec_batch * DEC_SEQ
    pool_in = jnp.transpose(state_pool[0], (1, 0, 2))
    ng, win_b, lbl, hg, wpa_b, wpool_b, ps, wpb_b, wout_b, fg = weights
    pseq = PREP_SEQS
    prows = pseq * DEC_SEQ
    n_prep = dec_batch // pseq
    bb = BB_SAMPLE

    def prep_blk(s):
        return jnp.minimum((s + PREP_PARTS - 1) // PREP_PARTS, n_prep - 1)

    def finish_blk(s):
        return jnp.minimum(jnp.maximum(s - 2, 0) // PREP_PARTS, n_prep - 1)

    s_s, y_s, pool_out = pl.pallas_call(
        functools.partial(_decode_kernel, n_prep=n_prep),
        grid=(1 + dec_batch // bb,),
        in_specs=[
            pl.BlockSpec((pseq, DEC_SEQ, D_MODEL), lambda s: (prep_blk(s), 0, 0)),
            pl.BlockSpec((POOL_BUF, pseq, W_BRANCH), lambda s: (0, prep_blk(s), 0),
                         pipeline_mode=pl.Buffered(1)),
            hbm,
            _const_spec(ng.shape), _const_spec(win_b.shape), _const_spec(lbl.shape),
            _const_spec(wpool_b.shape), _const_spec(ps.shape), _const_spec(wpb_b.shape),
            _const_spec(hg.shape), _const_spec(wpa_b.shape), _const_spec(wout_b.shape),
            _const_spec(fg.shape),
        ],
        out_specs=[
            hbm,
            pl.BlockSpec((pseq, DEC_SEQ, D_MODEL), lambda s: (finish_blk(s), 0, 0)),
            pl.BlockSpec((POOL_BUF, pseq, W_BRANCH), lambda s: (0, prep_blk(s), 0)),
        ],
        out_shape=[
            jax.ShapeDtypeStruct((dec_batch, N_HEADS, DK, DV), F32),
            jax.ShapeDtypeStruct(x_sample.shape, F32),
            jax.ShapeDtypeStruct((POOL_BUF, dec_batch, W_BRANCH), F32),
        ],
        scratch_shapes=[
            pltpu.VMEM((N_HEADS, 2, rows, DK), BF16),
            pltpu.VMEM((N_HEADS, 2, rows, DV), BF16),
            pltpu.VMEM((rows, W_BRANCH), BF16),
            pltpu.VMEM((rows, W_BRANCH), F32),
            pltpu.VMEM((rows, W_BRANCH), F32),
            pltpu.VMEM((rows, D_MODEL), F32),
            pltpu.VMEM((rows, D_MODEL), F32),
            pltpu.VMEM((rows, D_MODEL), F32),
            pltpu.VMEM((prows, D_IN), F32),
            pltpu.VMEM((len(POOL_WINDOWS), prows, G_B), F32),
            pltpu.VMEM((len(POOL_WINDOWS), prows, G_B), F32),
            pltpu.VMEM((prows, W_BRANCH), BF16),
            pltpu.VMEM((prows, W_BRANCH), BF16),
            pltpu.VMEM((2, bb, N_HEADS, DK, DV), F32),
            pltpu.VMEM((2, bb, N_HEADS, DK, DV), F32),
            pltpu.SemaphoreType.DMA((2, STATE_DMAS)),
            pltpu.SemaphoreType.DMA((2, STATE_DMAS)),
        ],
        compiler_params=pltpu.CompilerParams(
            dimension_semantics=("arbitrary",), vmem_limit_bytes=DECODE_VMEM_LIMIT_BYTES),
        name="hgrn2_pool_decode",
    )(x_sample, pool_in, state_hgrn[0], ng, win_b, lbl, wpool_b, ps, wpb_b,
      hg, wpa_b, wout_b, fg)

    p_s = jnp.transpose(pool_out, (1, 0, 2))
    return (y_p, y_s, s_p[None], p_p[None], s_s[None], p_s[None])
```

```python
import functools

import jax
import jax.numpy as jnp
from jax import lax
from jax.experimental import pallas as pl
from jax.experimental.pallas import tpu as pltpu

F32 = jnp.float32
BF16 = jnp.bfloat16

D_MODEL = 1024
W_BRANCH = 512
N_HEADS = 4
DK = 128
DV = 128
CHUNK = 32
POOL_WINDOWS = (2, 4, 8, 16)
G_B = 128
POOL_BUF = 15
PAST_LEN = 16384
EPS = 1e-6
D_IN = 4 * W_BRANCH + 2 * W_BRANCH + 2 * D_MODEL
OFF_Q, OFF_F, OFF_I, OFF_GA = 0, 512, 1024, 1536
OFF_U, OFF_GB, OFF_MA, OFF_MB = 2048, 2560, 3072, 4096

SUBLANES = 8
VMEM_LIMIT_BYTES = 56 * 1024 * 1024
DECODE_VMEM_LIMIT_BYTES = 60 * 1024 * 1024

TM_PROMPT = 512
SUB_TILE = 256
PREP_SEQS = 64
BB_SAMPLE = 16
STATE_DMAS = 8
DEC_SEQ = 4
PAIR_ROWS = 2 * DEC_SEQ
PREP_PARTS = 4


def _rms(x, g):
    ms = jnp.mean(x * x, axis=-1, keepdims=True)
    return x * lax.rsqrt(ms + EPS) * g


def _lower_bound(lbl_ref):
    l0 = lbl_ref[0:1, :]
    l1 = lbl_ref[1:2, :]
    m = jnp.maximum(l0, l1)
    e0 = jnp.exp(l0 - m)
    e1 = jnp.exp(l1 - m)
    return e0 / (e0 + e1)


def _in_proj(x, ng_ref, win_ref, z_ref, c0=0, c1=D_IN):
    h = _rms(x, ng_ref[...]).astype(BF16)
    z_ref[:, c0:c1] = jnp.dot(h, win_ref[:, c0:c1], preferred_element_type=F32)


def _hgrn_inputs(z_ref, lb, hh):
    c = hh * DK
    q = z_ref[:, OFF_Q + c:OFF_Q + c + DK]
    f = z_ref[:, OFF_F + c:OFF_F + c + DK]
    v = z_ref[:, OFF_I + c:OFF_I + c + DK]
    lbh = lb[:, c:c + DK]
    fg = lbh + (1.0 - lbh) * jax.nn.sigmoid(f)
    logf = jnp.log(fg)
    k = 1.0 - fg
    qf = jax.nn.silu(q) * (DK ** -0.5)
    return qf, k, v, logf


def _group_cumsum(x, group):
    row = lax.broadcasted_iota(jnp.int32, x.shape, 0) & (group - 1)
    s = 1
    while s < group:
        x = x + jnp.where(row >= s, pltpu.roll(x, s, 0), 0.0)
        s *= 2
    return x


def _hgrn_epilogue(o, z_ref, hg_ref, hh, oa_ref):
    c = hh * DK
    ga = z_ref[:, OFF_GA + c:OFF_GA + c + DK]
    on = _rms(o, hg_ref[...]) * jax.nn.silu(ga)
    oa_ref[:, c:c + DK] = on.astype(BF16)


def _pool_epilogue(pooled, z_ref, g, wpool_ref, ps_ref, yb_ref):
    c = g * G_B
    mixed = jnp.dot(pooled.astype(BF16), wpool_ref[g], preferred_element_type=F32)
    gb = z_ref[:, OFF_GB + c:OFF_GB + c + G_B]
    yb = mixed * ps_ref[:, c:c + G_B] * jax.nn.silu(gb)
    yb_ref[:, c:c + G_B] = yb.astype(BF16)


def _weight_copies(hbm_refs, stage_ref, sem):
    win_hbm, wpa_hbm, wpool_hbm, wpb_hbm, wout_hbm = hbm_refs
    rows = stage_ref.shape[0]
    small = [
        (wpa_hbm, stage_ref.at[:, 0:D_MODEL]),
        (wpb_hbm, stage_ref.at[:, D_MODEL:2 * D_MODEL]),
        (wout_hbm.at[0:rows, :], stage_ref.at[:, 2 * D_MODEL:3 * D_MODEL]),
        (wout_hbm.at[rows:2 * rows, :], stage_ref.at[:, 3 * D_MODEL:4 * D_MODEL]),
        (wpool_hbm, stage_ref.at[:, 4 * D_MODEL:4 * D_MODEL + G_B]),
    ]
    small = [pltpu.make_async_copy(src, dst, sem.at[i]) for i, (src, dst) in enumerate(small)]
    blk = rows // 2

    def win(k):
        slot = k % 2
        return pltpu.make_async_copy(win_hbm.at[k * blk:(k + 1) * blk, :],
                                     stage_ref.at[slot * blk:(slot + 1) * blk, :],
                                     sem.at[len(small) + slot])
    return small, win, blk


def _weight_exports(vmem_refs, out_refs, sem):
    return [pltpu.make_async_copy(src, dst, sem.at[i])
            for i, (src, dst) in enumerate(zip(vmem_refs, out_refs))]


def _prompt_kernel(x_ref, ng_ref, win_hbm, lbl_ref, hg_ref, wpa_hbm, wpool_hbm, ps_ref,
                   wpb_hbm, wout_hbm, fg_ref,
                   y_ref, s_out_ref, p_out_ref, win_out, wpa_out, wpool_out, wpb_out, wout_out,
                   z_ref, st_ref, ext_ref, kx_ref, sn_ref, qs_ref, ks_ref, qd_ref, vb_ref,
                   oa_ref, yb_ref, win_ref, wpa_ref, wpool_ref, wpb_ref, wout_ref,
                   load_sem, export_sem):
    tm = TM_PROMPT
    sub = SUB_TILE
    n_sub = tm // sub
    nc = tm // CHUNK
    ncs = sub // CHUNK
    t = pl.program_id(1)
    first = t == 0
    hdr = 2 * SUBLANES
    bf16_weights = (win_ref, wpa_ref, wpool_ref, wpb_ref, wout_ref)
    exports = _weight_exports(bf16_weights, (win_out, wpa_out, wpool_out, wpb_out, wout_out),
                              export_sem)

    @pl.when((pl.program_id(0) == 0) & first)
    def _():
        st_ref[...] = jnp.zeros_like(st_ref)
        ext_ref[...] = jnp.zeros_like(ext_ref)
        kx_ref[...] = jnp.zeros_like(kx_ref)
        small, win, blk = _weight_copies((win_hbm, wpa_hbm, wpool_hbm, wpb_hbm, wout_hbm),
                                         z_ref, load_sem)
        for c in small:
            c.start()
        for c in small:
            c.wait()
        wpa_ref[...] = z_ref[:, 0:D_MODEL].astype(BF16)
        wpb_ref[...] = z_ref[:, D_MODEL:2 * D_MODEL].astype(BF16)
        wout_ref[0:tm, :] = z_ref[:, 2 * D_MODEL:3 * D_MODEL].astype(BF16)
        wout_ref[tm:2 * tm, :] = z_ref[:, 3 * D_MODEL:4 * D_MODEL].astype(BF16)
        for g in range(len(POOL_WINDOWS)):
            wpool_ref[g] = z_ref[g * G_B:(g + 1) * G_B,
                                 4 * D_MODEL:4 * D_MODEL + G_B].astype(BF16)
        n_blk = D_MODEL // blk
        win(0).start()
        win(1).start()
        for k in range(n_blk):
            slot = k % 2
            win(k).wait()
            win_ref[k * blk:(k + 1) * blk, :] = z_ref[slot * blk:(slot + 1) * blk, :].astype(BF16)
            if k + 2 < n_blk:
                win(k + 2).start()
        for c in exports:
            c.start()

    @pl.when((pl.program_id(0) == pl.num_programs(0) - 1) & (t == pl.num_programs(1) - 1))
    def _():
        for c in exports:
            c.wait()

    x = x_ref[0]
    _in_proj(x, ng_ref, win_ref, z_ref)
    lb = _lower_bound(lbl_ref)

    ri = lax.broadcasted_iota(jnp.int32, (sub, sub), 0)
    ci = lax.broadcasted_iota(jnp.int32, (sub, sub), 1)
    causal = ((ri // CHUNK) == (ci // CHUNK)) & (ci <= ri)

    heads = range(N_HEADS)
    units = [(hh, s) for hh in heads for s in range(n_sub)]

    def rows(s):
        return slice(s * sub, (s + 1) * sub)

    def cols(hh):
        return slice(hh * DK, (hh + 1) * DK)

    dec = []
    for hh in heads:
        qf, k, v, logf = _hgrn_inputs(z_ref, lb, hh)
        b = _group_cumsum(logf, CHUNK)
        b3 = b.reshape(nc, CHUNK, DK)
        ref = b3[:, CHUNK // 2:CHUNK // 2 + 1, :]
        bl = b3[:, CHUNK - 1:CHUNK, :]
        q3 = qf.reshape(nc, CHUNK, DK)
        k3 = k.reshape(nc, CHUNK, DK)
        qs_ref[:, cols(hh)] = (q3 * jnp.exp(b3 - ref)).reshape(tm, DK).astype(BF16)
        ks_ref[:, cols(hh)] = (k3 * jnp.exp(ref - b3)).reshape(tm, DK).astype(BF16)
        qd_ref[:, cols(hh)] = (q3 * jnp.exp(b3)).reshape(tm, DK).astype(BF16)
        vb_ref[:, cols(hh)] = v.astype(BF16)
        kd = (k3 * jnp.exp(bl - b3)).reshape(tm, DK).astype(BF16)
        dec.append(jnp.exp(bl))
        for c in range(nc):
            cc = c % ncs
            kx_ref[hh, c * CHUNK:(c + 1) * CHUNK, cc * DK:(cc + 1) * DK] = (
                kd[c * CHUNK:(c + 1) * CHUNK, :])

    ext_ref[0:hdr, :] = jnp.where(first, 0.0, ext_ref[tm:tm + hdr, :])
    ext_ref[hdr:hdr + tm, :] = z_ref[:, OFF_U:OFF_U + W_BRANCH]
    pos1 = t * tm + lax.broadcasted_iota(jnp.int32, (tm, 1), 0) + 1
    for g, w in enumerate(POOL_WINDOWS):
        c = g * G_B
        s = ext_ref[:, c:c + G_B]
        sh = 1
        while sh < w:
            s = s + pltpu.roll(s, sh, 0)
            sh *= 2
        inv = jnp.where(pos1 >= w, 1.0 / w, 1.0 / pos1.astype(F32))
        pooled = s[hdr:, :] * inv - z_ref[:, OFF_U + c:OFF_U + c + G_B]
        _pool_epilogue(pooled, z_ref, g, wpool_ref, ps_ref, yb_ref)
    for j in range(POOL_BUF):
        r = hdr + tm - POOL_BUF + j
        p_out_ref[j, pl.ds(pl.program_id(0), 1), :] = ext_ref[r:r + 1, :]

    sc = {(hh, s): lax.dot_general(qs_ref[rows(s), cols(hh)], ks_ref[rows(s), cols(hh)],
                                   (((1,), (1,)), ((), ())), preferred_element_type=F32)
          for hh, s in units}
    ut = {(hh, s): lax.dot_general(vb_ref[rows(s), cols(hh)], kx_ref[hh, rows(s), :],
                                   (((0,), (0,)), ((), ())), preferred_element_type=F32)
          for hh, s in units}
    o = {(hh, s): jnp.dot(jnp.where(causal, sc[hh, s], 0.0).astype(BF16),
                          vb_ref[rows(s), cols(hh)], preferred_element_type=F32)
         for hh, s in units}
    for hh in heads:
        st = jnp.where(first, 0.0, st_ref[hh])
        for c in range(nc):
            sn_ref[hh, c] = st.T.astype(BF16)
            st = dec[hh][c] * st + ut[hh, c // ncs][:, (c % ncs) * DK:(c % ncs + 1) * DK]
        st_ref[hh] = st
        s_out_ref[0, hh] = st.T
    wcol = D_MODEL // N_HEADS
    y_b = []
    for hh in heads:
        o_inter = [jnp.dot(qd_ref[c * CHUNK:(c + 1) * CHUNK, cols(hh)], sn_ref[hh, c],
                           preferred_element_type=F32) for c in range(nc)]
        y_b.append(jnp.dot(yb_ref[...], wpb_ref[:, hh * wcol:(hh + 1) * wcol],
                           preferred_element_type=F32))
        o_hh = jnp.concatenate([o[hh, s] for s in range(n_sub)], axis=0)
        _hgrn_epilogue(o_hh + jnp.concatenate(o_inter, axis=0), z_ref, hg_ref, hh, oa_ref)
    gated_b = jax.nn.sigmoid(z_ref[:, OFF_MB:OFF_MB + D_MODEL]) * jnp.concatenate(y_b, axis=1)

    y_a = jnp.dot(oa_ref[...], wpa_ref[...], preferred_element_type=F32)
    merged = jax.nn.sigmoid(z_ref[:, OFF_MA:OFF_MA + D_MODEL]) * y_a + gated_b
    out = x + jnp.dot(merged.astype(BF16), wout_ref[...], preferred_element_type=F32)
    y_ref[0] = _rms(out, fg_ref[...])


def _shift_rows(x, k):
    n = x.shape[0]
    return x if k % n == 0 else pltpu.roll(x, (-k) % n, 0)


def _group_bcast(x, j, group):
    t = lax.broadcasted_iota(jnp.int32, x.shape, 0) & (group - 1)
    out = _shift_rows(x, j - (group - 1))
    for tt in range(group - 2, -1, -1):
        out = jnp.where(t == tt, _shift_rows(x, j - tt), out)
    return out


def _decode_prep_hgrn(heads, z_ref, lbl_ref, a_ref, bv_ref, qd_ref, oi_ref, ga_ref):
    rows = z_ref.shape[0]
    lb = _lower_bound(lbl_ref)
    r = lax.broadcasted_iota(jnp.int32, (rows, DK), 0)
    t = r & (DEC_SEQ - 1)
    even = (r & (PAIR_ROWS - 1)) < DEC_SEQ
    blk = SUB_TILE
    ri = lax.broadcasted_iota(jnp.int32, (blk, blk), 0)
    ci = lax.broadcasted_iota(jnp.int32, (blk, blk), 1)
    causal = ((ri // DEC_SEQ) == (ci // DEC_SEQ)) & (ci <= ri)

    for hh in heads:
        cs = slice(hh * DK, (hh + 1) * DK)
        qf, k, v, logf = _hgrn_inputs(z_ref, lb, hh)
        b = _group_cumsum(logf, DEC_SEQ)
        ref = _group_bcast(b, DEC_SEQ // 2, DEC_SEQ)
        bl = _group_bcast(b, DEC_SEQ - 1, DEC_SEQ)
        qs = (qf * jnp.exp(b - ref)).astype(BF16)
        ks = (k * jnp.exp(ref - b)).astype(BF16)
        kd = k * jnp.exp(bl - b)
        vb = v.astype(BF16)
        qd_ref[:, cs] = (qf * jnp.exp(b)).astype(BF16)
        for s in range(rows // blk):
            rs = slice(s * blk, (s + 1) * blk)
            sc = lax.dot_general(qs[rs], ks[rs], (((1,), (1,)), ((), ())),
                                 preferred_element_type=F32)
            oi_ref[rs, cs] = jnp.dot(jnp.where(causal, sc, 0.0).astype(BF16), vb[rs],
                                     preferred_element_type=F32)
        dec = jnp.exp(bl)
        d1 = dec.astype(BF16).astype(F32)
        d2 = (dec - d1).astype(BF16).astype(F32)
        d3 = (dec - d1 - d2).astype(BF16).astype(F32)
        tail = jnp.where(t == 0, d1, jnp.where(t == 1, d2, jnp.where(t == 2, d3, 0.0)))
        a_ref[hh, 0] = jnp.where(even, kd, _shift_rows(tail, -DEC_SEQ)).astype(BF16)
        a_ref[hh, 1] = jnp.where(even, _shift_rows(tail, DEC_SEQ), kd).astype(BF16)
        zero = jnp.zeros_like(vb)
        bv_ref[hh, 0] = jnp.where(even, vb, zero)
        bv_ref[hh, 1] = jnp.where(even, zero, vb)
        ga_ref[:, cs] = jax.nn.silu(z_ref[:, OFF_GA + hh * DK:OFF_GA + (hh + 1) * DK])


def _decode_prep_history(p_in_ref, rw_ref, p_out_ref):
    n_seq = p_in_ref.shape[1]
    for g, w in enumerate(POOL_WINDOWS):
        gc = slice(g * G_B, (g + 1) * G_B)
        acc = None
        suffix = {}
        for j in range(POOL_BUF - 1, -1, -1):
            e = p_in_ref[j, :, gc]
            acc = e if acc is None else acc + e
            suffix[j] = acc
        for tt in range(DEC_SEQ):
            j = POOL_BUF + 1 - w + tt
            rw_ref[g, pl.ds(tt, n_seq, stride=DEC_SEQ), :] = (
                suffix[j] if j < POOL_BUF else jnp.zeros((n_seq, G_B), F32))
    for j in range(POOL_BUF - DEC_SEQ):
        p_out_ref[j] = p_in_ref[j + DEC_SEQ]


def _decode_prep_pool(z_ref, wpool_ref, ps_ref, wpb_ref, ma_ref, gb_ref, p_out_ref,
                      rw_ref, u_ref, yb_ref):
    rows = z_ref.shape[0]
    n_seq = rows // DEC_SEQ
    t = lax.broadcasted_iota(jnp.int32, (rows, G_B), 0) & (DEC_SEQ - 1)
    for g, w in enumerate(POOL_WINDOWS):
        u = z_ref[:, OFF_U + g * G_B:OFF_U + (g + 1) * G_B]
        u_ref[g] = u
        cu = u
        sh = 1
        while sh < min(w, DEC_SEQ):
            cu = cu + jnp.where(t >= sh, _shift_rows(cu, -sh), 0.0)
            sh *= 2
        pooled = (rw_ref[g] + cu) * (1.0 / w) - u
        _pool_epilogue(pooled, z_ref, g, wpool_ref, ps_ref, yb_ref)
    gb_ref[...] = jax.nn.sigmoid(z_ref[:, OFF_MB:OFF_MB + D_MODEL]) * jnp.dot(
        yb_ref[...], wpb_ref[...], preferred_element_type=F32)
    ma_ref[...] = jax.nn.sigmoid(z_ref[:, OFF_MA:OFF_MA + D_MODEL])
    for tt in range(DEC_SEQ):
        for g in range(len(POOL_WINDOWS)):
            p_out_ref[POOL_BUF - DEC_SEQ + tt, :, g * G_B:(g + 1) * G_B] = (
                u_ref[g, pl.ds(tt, n_seq, stride=DEC_SEQ), :])


def _decode_prep(part, x_ref, p_in_ref, ng_ref, win_ref, lbl_ref, wpool_ref, ps_ref, wpb_ref,
                 a_ref, bv_ref, qd_ref, oi_ref, ga_ref, ma_ref, gb_ref, p_out_ref,
                 xs_ref, z_ref, rw_ref, u_ref, yb_ref):
    hgrn = functools.partial(_decode_prep_hgrn, z_ref=z_ref, lbl_ref=lbl_ref, a_ref=a_ref,
                             bv_ref=bv_ref, qd_ref=qd_ref, oi_ref=oi_ref, ga_ref=ga_ref)
    if part == 0:
        for i in range(x_ref.shape[0]):
            xs_ref[i * DEC_SEQ:(i + 1) * DEC_SEQ, :] = x_ref[i]
        _in_proj(xs_ref[...], ng_ref, win_ref, z_ref, 0, OFF_U)
        _decode_prep_history(p_in_ref, rw_ref, p_out_ref)
    elif part == 1:
        _in_proj(xs_ref[...], ng_ref, win_ref, z_ref, OFF_U, D_IN)
    elif part == 2:
        hgrn(range(0, N_HEADS - 1))
    else:
        hgrn(range(N_HEADS - 1, N_HEADS))
        _decode_prep_pool(z_ref, wpool_ref, ps_ref, wpb_ref, ma_ref, gb_ref, p_out_ref,
                          rw_ref, u_ref, yb_ref)


def _decode_state_step(a_ref, bv_ref, qd_ref, s_in_ref, s_out_ref, o_ref):
    n_pairs = BB_SAMPLE // 2
    r8 = lax.broadcasted_iota(jnp.int32, (PAIR_ROWS, DV), 0)
    even = r8 < DEC_SEQ
    ones = (jnp.where((r8 >= DEC_SEQ) & (r8 < DEC_SEQ + 3), 1.0, 0.0).astype(BF16),
            jnp.where(r8 < 3, 1.0, 0.0).astype(BF16))
    for hh in range(N_HEADS):
        for lp in range(n_pairs):
            rs = slice(lp * PAIR_ROWS, (lp + 1) * PAIR_ROWS)
            for e in range(2):
                rhs = jnp.concatenate([bv_ref[hh, e, rs, :], ones[e]], axis=1)
                upd = lax.dot_general(a_ref[hh, e, rs, :], rhs, (((0,), (0,)), ((), ())),
                                      preferred_element_type=F32)
                s_out_ref[2 * lp + e, hh] = (upd[:, DV:2 * DV] * s_in_ref[2 * lp + e, hh]
                                             + upd[:, 0:DV])
    for hh in range(N_HEADS):
        cs = slice(hh * DK, (hh + 1) * DK)
        for lp in range(n_pairs):
            rs = slice(lp * PAIR_ROWS, (lp + 1) * PAIR_ROWS)
            q8 = qd_ref[rs, cs]
            o0 = jnp.dot(q8, s_in_ref[2 * lp, hh].astype(BF16), preferred_element_type=F32)
            o1 = jnp.dot(q8, s_in_ref[2 * lp + 1, hh].astype(BF16), preferred_element_type=F32)
            o_ref[rs, cs] = o_ref[rs, cs] + jnp.where(even, o0, o1)


def _decode_finish(o_ref, ga_ref, ma_ref, gb_ref, xs_ref, hg_ref, wpa_ref, wout_ref,
                   fg_ref, y_ref, oa_ref, *, n_seq):
    for hh in range(N_HEADS):
        cs = slice(hh * DK, (hh + 1) * DK)
        oa_ref[:, cs] = (_rms(o_ref[:, cs], hg_ref[...]) * ga_ref[:, cs]).astype(BF16)
    y_a = jnp.dot(oa_ref[...], wpa_ref[...], preferred_element_type=F32)
    merged = ma_ref[...] * y_a + gb_ref[...]
    out = xs_ref[...] + jnp.dot(merged.astype(BF16), wout_ref[...], preferred_element_type=F32)
    y = _rms(out, fg_ref[...])
    for i in range(n_seq):
        y_ref[i] = y[i * DEC_SEQ:(i + 1) * DEC_SEQ, :]


def _state_copies(hbm_ref, buf_ref, sem, blk, slot, to_hbm):
    n = BB_SAMPLE // STATE_DMAS
    copies = []
    for c in range(STATE_DMAS):
        in_hbm = hbm_ref.at[pl.ds(blk * BB_SAMPLE + c * n, n)]
        in_vmem = buf_ref.at[slot, pl.ds(c * n, n)]
        src, dst = (in_vmem, in_hbm) if to_hbm else (in_hbm, in_vmem)
        copies.append(pltpu.make_async_copy(src, dst, sem.at[slot, c]))
    return copies


def _decode_kernel(xp_ref, p_in_ref, s_in_hbm,
                   ng_ref, win_ref, lbl_ref, wpool_ref, ps_ref, wpb_ref,
                   hg_ref, wpa_ref, wout_ref, fg_ref,
                   s_out_hbm, y_ref, p_out_ref,
                   a_all, bv_all, qd_all, o_all, ga_all, ma_all, gb_all, xs_all,
                   z_ref, rw_ref, u_ref, yb_ref, oa_ref, s_in_buf, s_out_buf, in_sem, out_sem,
                   *, n_prep):
    s = pl.program_id(0)
    prows = PREP_SEQS * DEC_SEQ
    brows = BB_SAMPLE * DEC_SEQ
    n_state = n_prep * PREP_PARTS

    def load(blk, slot):
        return _state_copies(s_in_hbm, s_in_buf, in_sem, blk, slot, to_hbm=False)

    def store(blk, slot):
        return _state_copies(s_out_hbm, s_out_buf, out_sem, blk, slot, to_hbm=True)

    def block_rows(blk):
        start = blk * prows
        return pl.ds(start if isinstance(start, int) else pl.multiple_of(start, prows), prows)

    def prep(part, blk):
        r = block_rows(blk)
        _decode_prep(part, xp_ref, p_in_ref, ng_ref, win_ref, lbl_ref, wpool_ref, ps_ref, wpb_ref,
                     a_all.at[:, :, r, :], bv_all.at[:, :, r, :], qd_all.at[r, :],
                     o_all.at[r, :], ga_all.at[r, :], ma_all.at[r, :], gb_all.at[r, :],
                     p_out_ref, xs_all.at[r, :], z_ref, rw_ref, u_ref, yb_ref)

    def finish(blk):
        f = block_rows(blk)
        _decode_finish(o_all.at[f, :], ga_all.at[f, :], ma_all.at[f, :], gb_all.at[f, :],
                       xs_all.at[f, :], hg_ref, wpa_ref, wout_ref, fg_ref, y_ref, oa_ref,
                       n_seq=PREP_SEQS)

    i = s - 1
    blk = lax.div(i, PREP_PARTS)
    part_now = lax.rem(i, PREP_PARTS)
    slot = lax.rem(i, 2)

    @pl.when(s == 0)
    def _():
        for b in range(2):
            for c in load(b, b):
                c.start()
        for part in range(PREP_PARTS):
            prep(part, 0)

    @pl.when(s > 0)
    def _():
        @pl.when((i > 0) & (i + 1 < n_state))
        def _():
            for c in load(i + 1, 1 - slot):
                c.start()
        for c in load(i, slot):
            c.wait()

        @pl.when(i >= 2)
        def _():
            for c in store(i - 2, slot):
                c.wait()
        r = pl.ds(pl.multiple_of(i * brows, brows), brows)
        _decode_state_step(a_all.at[:, :, r, :], bv_all.at[:, :, r, :], qd_all.at[r, :],
                           s_in_buf.at[slot], s_out_buf.at[slot], o_all.at[r, :])
        for c in store(i, slot):
            c.start()

    for part in range(PREP_PARTS):
        @pl.when((part_now == part) & (blk + 1 < n_prep))
        def _():
            prep(part, blk + 1)

    @pl.when((part_now == 0) & (i > 0))
    def _():
        finish(blk - 1)

    @pl.when(i == n_state - 1)
    def _():
        finish(n_prep - 1)
        for c in store(i - 1, 1 - slot) + store(i, slot):
            c.wait()


def _const_spec(shape):
    n = len(shape)
    return pl.BlockSpec(shape, lambda *_: (0,) * n, pipeline_mode=pl.Buffered(1))


def kernel(x_prompt, x_sample, state_hgrn, state_pool, norm_g, w_in, lb_logits, hgrn_norm_g,
           w_proj_a, w_pool, pool_scale, w_proj_b, w_out, final_norm_g):
    batch, seq, _ = x_prompt.shape
    dec_batch, dec_seq, _ = x_sample.shape
    assert norm_g.shape[0] == 1 and lb_logits.shape[0] == 2, "single-layer decoder only"
    assert seq % TM_PROMPT == 0 and TM_PROMPT % SUB_TILE == 0 and dec_batch % BB_SAMPLE == 0
    assert dec_seq == DEC_SEQ and PAIR_ROWS == SUBLANES and BB_SAMPLE % 2 == 0
    assert dec_batch % PREP_SEQS == 0 and (PREP_SEQS * DEC_SEQ) % SUB_TILE == 0
    assert PREP_SEQS == PREP_PARTS * BB_SAMPLE and BB_SAMPLE % STATE_DMAS == 0
    assert PAST_LEN >= max(POOL_WINDOWS)

    n_groups = len(POOL_WINDOWS)
    tm = TM_PROMPT
    nt = seq // tm
    assert w_in.shape[1:] == (2 * tm, D_IN) and w_proj_a.shape[1] == tm, "weight staging layout"
    hbm = pl.BlockSpec(memory_space=pl.ANY)
    bf16_shapes = [(D_MODEL, D_IN), (W_BRANCH, D_MODEL), (n_groups, G_B, G_B),
                   (W_BRANCH, D_MODEL), (D_MODEL, D_MODEL)]
    y_p, s_p, p_p, win_b, wpa_b, wpool_b, wpb_b, wout_b = pl.pallas_call(
        _prompt_kernel,
        grid=(batch, nt),
        in_specs=[pl.BlockSpec((1, tm, D_MODEL), lambda b, t: (b, t, 0)),
                  _const_spec((1, D_MODEL)), hbm, _const_spec((2, W_BRANCH)), _const_spec((1, DV)),
                  hbm, hbm, _const_spec((1, W_BRANCH)), hbm, hbm, _const_spec((1, D_MODEL))],
        out_specs=[
            pl.BlockSpec((1, tm, D_MODEL), lambda b, t: (b, t, 0)),
            pl.BlockSpec((1, N_HEADS, DK, DV), lambda b, t: (b, 0, 0, 0)),
            pl.BlockSpec((POOL_BUF, batch, W_BRANCH), lambda b, t: (0, 0, 0)),
        ] + [hbm] * len(bf16_shapes),
        out_shape=[
            jax.ShapeDtypeStruct((batch, seq, D_MODEL), F32),
            jax.ShapeDtypeStruct((batch, N_HEADS, DK, DV), F32),
            jax.ShapeDtypeStruct((POOL_BUF, batch, W_BRANCH), F32),
        ] + [jax.ShapeDtypeStruct(s, BF16) for s in bf16_shapes],
        scratch_shapes=[
            pltpu.VMEM((tm, D_IN), F32),
            pltpu.VMEM((N_HEADS, DV, DK), F32),
            pltpu.VMEM((tm + 2 * SUBLANES, W_BRANCH), F32),
            pltpu.VMEM((N_HEADS, tm, (SUB_TILE // CHUNK) * DK), BF16),
            pltpu.VMEM((N_HEADS, tm // CHUNK, DK, DV), BF16),
            pltpu.VMEM((tm, W_BRANCH), BF16),
            pltpu.VMEM((tm, W_BRANCH), BF16),
            pltpu.VMEM((tm, W_BRANCH), BF16),
            pltpu.VMEM((tm, W_BRANCH), BF16),
            pltpu.VMEM((tm, W_BRANCH), BF16),
            pltpu.VMEM((tm, W_BRANCH), BF16),
        ] + [pltpu.VMEM(s, BF16) for s in bf16_shapes] + [
            pltpu.SemaphoreType.DMA((7,)),
            pltpu.SemaphoreType.DMA((len(bf16_shapes),)),
        ],
        compiler_params=pltpu.CompilerParams(
            dimension_semantics=("arbitrary", "arbitrary"),
            vmem_limit_bytes=VMEM_LIMIT_BYTES),
        name="hgrn2_pool_prompt",
    )(x_prompt, norm_g, w_in[0], lb_logits, hgrn_norm_g, w_proj_a[0],
      w_pool[0].reshape(n_groups * G_B, G_B), pool_scale, w_proj_b[0], w_out[0],
      final_norm_g.reshape(1, D_MODEL))
    weights = (norm_g, win_b, lb_logits, hgrn_norm_g, wpa_b, wpool_b, pool_scale, wpb_b, wout_b,
               final_norm_g.reshape(1, D_MODEL))

    rows = dec_batch * DEC_SEQ
    pool_in = jnp.transpose(state_pool[0], (1, 0, 2))
    ng, win_b, lbl, hg, wpa_b, wpool_b, ps, wpb_b, wout_b, fg = weights
    pseq = PREP_SEQS
    prows = pseq * DEC_SEQ
    n_prep = dec_batch // pseq
    bb = BB_SAMPLE

    def prep_blk(s):
        return jnp.minimum((s + PREP_PARTS - 1) // PREP_PARTS, n_prep - 1)

    def finish_blk(s):
        return jnp.minimum(jnp.maximum(s - 2, 0) // PREP_PARTS, n_prep - 1)

    s_s, y_s, pool_out = pl.pallas_call(
        functools.partial(_decode_kernel, n_prep=n_prep),
        grid=(1 + dec_batch // bb,),
        in_specs=[
            pl.BlockSpec((pseq, DEC_SEQ, D_MODEL), lambda s: (prep_blk(s), 0, 0)),
            pl.BlockSpec((POOL_BUF, pseq, W_BRANCH), lambda s: (0, prep_blk(s), 0),
                         pipeline_mode=pl.Buffered(1)),
            hbm,
            _const_spec(ng.shape), _const_spec(win_b.shape), _const_spec(lbl.shape),
            _const_spec(wpool_b.shape), _const_spec(ps.shape), _const_spec(wpb_b.shape),
            _const_spec(hg.shape), _const_spec(wpa_b.shape), _const_spec(wout_b.shape),
            _const_spec(fg.shape),
        ],
        out_specs=[
            hbm,
            pl.BlockSpec((pseq, DEC_SEQ, D_MODEL), lambda s: (finish_blk(s), 0, 0)),
            pl.BlockSpec((POOL_BUF, pseq, W_BRANCH), lambda s: (0, prep_blk(s), 0)),
        ],
        out_shape=[
            jax.ShapeDtypeStruct((dec_batch, N_HEADS, DK, DV), F32),
            jax.ShapeDtypeStruct(x_sample.shape, F32),
            jax.ShapeDtypeStruct((POOL_BUF, dec_batch, W_BRANCH), F32),
        ],
        scratch_shapes=[
            pltpu.VMEM((N_HEADS, 2, rows, DK), BF16),
            pltpu.VMEM((N_HEADS, 2, rows, DV), BF16),
            pltpu.VMEM((rows, W_BRANCH), BF16),
            pltpu.VMEM((rows, W_BRANCH), F32),
            pltpu.VMEM((rows, W_BRANCH), F32),
            pltpu.VMEM((rows, D_MODEL), F32),
            pltpu.VMEM((rows, D_MODEL), F32),
            pltpu.VMEM((rows, D_MODEL), F32),
            pltpu.VMEM((prows, D_IN), F32),
            pltpu.VMEM((len(POOL_WINDOWS), prows, G_B), F32),
            pltpu.VMEM((len(POOL_WINDOWS), prows, G_B), F32),
            pltpu.VMEM((prows, W_BRANCH), BF16),
            pltpu.VMEM((prows, W_BRANCH), BF16),
            pltpu.VMEM((2, bb, N_HEADS, DK, DV), F32),
            pltpu.VMEM((2, bb, N_HEADS, DK, DV), F32),
            pltpu.SemaphoreType.DMA((2, STATE_DMAS)),
            pltpu.SemaphoreType.DMA((2, STATE_DMAS)),
        ],
        compiler_params=pltpu.CompilerParams(
            dimension_semantics=("arbitrary",), vmem_limit_bytes=DECODE_VMEM_LIMIT_BYTES),
        name="hgrn2_pool_decode",
    )(x_sample, pool_in, state_hgrn[0], ng, win_b, lbl, wpool_b, ps, wpb_b,
      hg, wpa_b, wout_b, fg)

    p_p = jnp.transpose(p_p, (1, 0, 2))
    p_s = jnp.transpose(pool_out, (1, 0, 2))
    return (y_p, y_s, s_p[None], p_p[None], s_s[None], p_s[None])
```

```python
import functools

import jax
import jax.numpy as jnp
from jax import lax
from jax.experimental import pallas as pl
from jax.experimental.pallas import tpu as pltpu

F32 = jnp.float32
BF16 = jnp.bfloat16

D_MODEL = 1024
W_BRANCH = 512
N_HEADS = 4
DK = 128
DV = 128
CHUNK = 32
POOL_WINDOWS = (2, 4, 8, 16)
G_B = 128
POOL_BUF = 15
PAST_LEN = 16384
EPS = 1e-6
D_IN = 4 * W_BRANCH + 2 * W_BRANCH + 2 * D_MODEL
OFF_Q, OFF_F, OFF_I, OFF_GA = 0, 512, 1024, 1536
OFF_U, OFF_GB, OFF_MA, OFF_MB = 2048, 2560, 3072, 4096

SUBLANES = 8
VMEM_LIMIT_BYTES = 56 * 1024 * 1024
DECODE_VMEM_LIMIT_BYTES = 60 * 1024 * 1024

TM_PROMPT = 512
SUB_TILE = 256
PREP_SEQS = 64
BB_SAMPLE = 16
STATE_DMAS = 8
DEC_SEQ = 4
PAIR_ROWS = 2 * DEC_SEQ
PREP_PARTS = 4


def _rms(x, g):
    ms = jnp.mean(x * x, axis=-1, keepdims=True)
    return x * lax.rsqrt(ms + EPS) * g


def _lower_bound(lbl_ref):
    l0 = lbl_ref[0:1, :]
    l1 = lbl_ref[1:2, :]
    m = jnp.maximum(l0, l1)
    e0 = jnp.exp(l0 - m)
    e1 = jnp.exp(l1 - m)
    return e0 / (e0 + e1)


def _in_proj(x, ng_ref, win_ref, z_ref, c0=0, c1=D_IN):
    h = _rms(x, ng_ref[...]).astype(BF16)
    z_ref[:, c0:c1] = jnp.dot(h, win_ref[:, c0:c1], preferred_element_type=F32)


def _hgrn_inputs(z_ref, lb, hh):
    c = hh * DK
    q = z_ref[:, OFF_Q + c:OFF_Q + c + DK]
    f = z_ref[:, OFF_F + c:OFF_F + c + DK]
    v = z_ref[:, OFF_I + c:OFF_I + c + DK]
    lbh = lb[:, c:c + DK]
    fg = lbh + (1.0 - lbh) * jax.nn.sigmoid(f)
    logf = jnp.log(fg)
    k = 1.0 - fg
    qf = jax.nn.silu(q) * (DK ** -0.5)
    return qf, k, v, logf


def _group_cumsum(x, group):
    row = lax.broadcasted_iota(jnp.int32, x.shape, 0) & (group - 1)
    s = 1
    while s < group:
        x = x + jnp.where(row >= s, pltpu.roll(x, s, 0), 0.0)
        s *= 2
    return x


def _hgrn_epilogue(o, z_ref, hg_ref, hh, oa_ref):
    c = hh * DK
    ga = z_ref[:, OFF_GA + c:OFF_GA + c + DK]
    on = _rms(o, hg_ref[...]) * jax.nn.silu(ga)
    oa_ref[:, c:c + DK] = on.astype(BF16)


def _pool_epilogue(pooled, z_ref, g, wpool_ref, ps_ref, yb_ref):
    c = g * G_B
    mixed = jnp.dot(pooled.astype(BF16), wpool_ref[g], preferred_element_type=F32)
    gb = z_ref[:, OFF_GB + c:OFF_GB + c + G_B]
    yb = mixed * ps_ref[:, c:c + G_B] * jax.nn.silu(gb)
    yb_ref[:, c:c + G_B] = yb.astype(BF16)


def _weight_copies(hbm_refs, stage_ref, sem):
    win_hbm, wpa_hbm, wpool_hbm, wpb_hbm, wout_hbm = hbm_refs
    rows = stage_ref.shape[0]
    small = [
        (wpa_hbm, stage_ref.at[:, 0:D_MODEL]),
        (wpb_hbm, stage_ref.at[:, D_MODEL:2 * D_MODEL]),
        (wout_hbm.at[0:rows, :], stage_ref.at[:, 2 * D_MODEL:3 * D_MODEL]),
        (wout_hbm.at[rows:2 * rows, :], stage_ref.at[:, 3 * D_MODEL:4 * D_MODEL]),
        (wpool_hbm, stage_ref.at[:, 4 * D_MODEL:4 * D_MODEL + G_B]),
    ]
    small = [pltpu.make_async_copy(src, dst, sem.at[i]) for i, (src, dst) in enumerate(small)]
    blk = rows // 2

    def win(k):
        slot = k % 2
        return pltpu.make_async_copy(win_hbm.at[k * blk:(k + 1) * blk, :],
                                     stage_ref.at[slot * blk:(slot + 1) * blk, :],
                                     sem.at[len(small) + slot])
    return small, win, blk


def _weight_exports(vmem_refs, out_refs, sem):
    return [pltpu.make_async_copy(src, dst, sem.at[i])
            for i, (src, dst) in enumerate(zip(vmem_refs, out_refs))]


def _prompt_kernel(x_ref, ng_ref, win_hbm, lbl_ref, hg_ref, wpa_hbm, wpool_hbm, ps_ref,
                   wpb_hbm, wout_hbm, fg_ref,
                   y_ref, s_out_ref, p_out_ref, win_out, wpa_out, wpool_out, wpb_out, wout_out,
                   z_ref, st_ref, ext_ref, kx_ref, sn_ref, qs_ref, ks_ref, qd_ref, vb_ref,
                   oa_ref, yb_ref, win_ref, wpa_ref, wpool_ref, wpb_ref, wout_ref,
                   load_sem, export_sem):
    tm = TM_PROMPT
    sub = SUB_TILE
    n_sub = tm // sub
    nc = tm // CHUNK
    ncs = sub // CHUNK
    t = pl.program_id(1)
    first = t == 0
    hdr = 2 * SUBLANES
    bf16_weights = (win_ref, wpa_ref, wpool_ref, wpb_ref, wout_ref)
    exports = _weight_exports(bf16_weights, (win_out, wpa_out, wpool_out, wpb_out, wout_out),
                              export_sem)

    @pl.when((pl.program_id(0) == 0) & first)
    def _():
        small, win, blk = _weight_copies((win_hbm, wpa_hbm, wpool_hbm, wpb_hbm, wout_hbm),
                                         z_ref, load_sem)
        for c in small:
            c.start()
        st_ref[...] = jnp.zeros_like(st_ref)
        ext_ref[...] = jnp.zeros_like(ext_ref)
        kx_ref[...] = jnp.zeros_like(kx_ref)
        for c in small:
            c.wait()
        wpa_ref[...] = z_ref[:, 0:D_MODEL].astype(BF16)
        wpb_ref[...] = z_ref[:, D_MODEL:2 * D_MODEL].astype(BF16)
        wout_ref[0:tm, :] = z_ref[:, 2 * D_MODEL:3 * D_MODEL].astype(BF16)
        wout_ref[tm:2 * tm, :] = z_ref[:, 3 * D_MODEL:4 * D_MODEL].astype(BF16)
        for g in range(len(POOL_WINDOWS)):
            wpool_ref[g] = z_ref[g * G_B:(g + 1) * G_B,
                                 4 * D_MODEL:4 * D_MODEL + G_B].astype(BF16)
        n_blk = D_MODEL // blk
        win(0).start()
        win(1).start()
        for k in range(n_blk):
            slot = k % 2
            win(k).wait()
            win_ref[k * blk:(k + 1) * blk, :] = z_ref[slot * blk:(slot + 1) * blk, :].astype(BF16)
            if k + 2 < n_blk:
                win(k + 2).start()
        for c in exports:
            c.start()

    @pl.when((pl.program_id(0) == pl.num_programs(0) - 1) & (t == pl.num_programs(1) - 1))
    def _():
        for c in exports:
            c.wait()

    x = x_ref[0]
    _in_proj(x, ng_ref, win_ref, z_ref)
    lb = _lower_bound(lbl_ref)

    ri = lax.broadcasted_iota(jnp.int32, (sub, sub), 0)
    ci = lax.broadcasted_iota(jnp.int32, (sub, sub), 1)
    causal = ((ri // CHUNK) == (ci // CHUNK)) & (ci <= ri)

    heads = range(N_HEADS)
    units = [(hh, s) for hh in heads for s in range(n_sub)]

    def rows(s):
        return slice(s * sub, (s + 1) * sub)

    def cols(hh):
        return slice(hh * DK, (hh + 1) * DK)

    dec = []
    for hh in heads:
        qf, k, v, logf = _hgrn_inputs(z_ref, lb, hh)
        b = _group_cumsum(logf, CHUNK)
        b3 = b.reshape(nc, CHUNK, DK)
        ref = b3[:, CHUNK // 2:CHUNK // 2 + 1, :]
        bl = b3[:, CHUNK - 1:CHUNK, :]
        q3 = qf.reshape(nc, CHUNK, DK)
        k3 = k.reshape(nc, CHUNK, DK)
        qs_ref[:, cols(hh)] = (q3 * jnp.exp(b3 - ref)).reshape(tm, DK).astype(BF16)
        ks_ref[:, cols(hh)] = (k3 * jnp.exp(ref - b3)).reshape(tm, DK).astype(BF16)
        qd_ref[:, cols(hh)] = (q3 * jnp.exp(b3)).reshape(tm, DK).astype(BF16)
        vb_ref[:, cols(hh)] = v.astype(BF16)
        kd = (k3 * jnp.exp(bl - b3)).reshape(tm, DK).astype(BF16)
        dec.append(jnp.exp(bl))
        for c in range(nc):
            cc = c % ncs
            kx_ref[hh, c * CHUNK:(c + 1) * CHUNK, cc * DK:(cc + 1) * DK] = (
                kd[c * CHUNK:(c + 1) * CHUNK, :])

    ext_ref[0:hdr, :] = jnp.where(first, 0.0, ext_ref[tm:tm + hdr, :])
    ext_ref[hdr:hdr + tm, :] = z_ref[:, OFF_U:OFF_U + W_BRANCH]
    pos1 = t * tm + lax.broadcasted_iota(jnp.int32, (tm, 1), 0) + 1
    for g, w in enumerate(POOL_WINDOWS):
        c = g * G_B
        s = ext_ref[:, c:c + G_B]
        sh = 1
        while sh < w:
            s = s + pltpu.roll(s, sh, 0)
            sh *= 2
        inv = jnp.where(pos1 >= w, 1.0 / w, 1.0 / pos1.astype(F32))
        pooled = s[hdr:, :] * inv - z_ref[:, OFF_U + c:OFF_U + c + G_B]
        _pool_epilogue(pooled, z_ref, g, wpool_ref, ps_ref, yb_ref)
    for j in range(POOL_BUF):
        r = hdr + tm - POOL_BUF + j
        p_out_ref[j, pl.ds(pl.program_id(0), 1), :] = ext_ref[r:r + 1, :]

    sc = {(hh, s): lax.dot_general(qs_ref[rows(s), cols(hh)], ks_ref[rows(s), cols(hh)],
                                   (((1,), (1,)), ((), ())), preferred_element_type=F32)
          for hh, s in units}
    ut = {(hh, s): lax.dot_general(vb_ref[rows(s), cols(hh)], kx_ref[hh, rows(s), :],
                                   (((0,), (0,)), ((), ())), preferred_element_type=F32)
          for hh, s in units}
    o = {(hh, s): jnp.dot(jnp.where(causal, sc[hh, s], 0.0).astype(BF16),
                          vb_ref[rows(s), cols(hh)], preferred_element_type=F32)
         for hh, s in units}
    for hh in heads:
        st = jnp.where(first, 0.0, st_ref[hh])
        for c in range(nc):
            sn_ref[hh, c] = st.T.astype(BF16)
            st = dec[hh][c] * st + ut[hh, c // ncs][:, (c % ncs) * DK:(c % ncs + 1) * DK]
        st_ref[hh] = st
        s_out_ref[0, hh] = st.T
    wcol = D_MODEL // N_HEADS
    y_b = []
    for hh in heads:
        o_inter = [jnp.dot(qd_ref[c * CHUNK:(c + 1) * CHUNK, cols(hh)], sn_ref[hh, c],
                           preferred_element_type=F32) for c in range(nc)]
        y_b.append(jnp.dot(yb_ref[...], wpb_ref[:, hh * wcol:(hh + 1) * wcol],
                           preferred_element_type=F32))
        o_hh = jnp.concatenate([o[hh, s] for s in range(n_sub)], axis=0)
        _hgrn_epilogue(o_hh + jnp.concatenate(o_inter, axis=0), z_ref, hg_ref, hh, oa_ref)
    gated_b = jax.nn.sigmoid(z_ref[:, OFF_MB:OFF_MB + D_MODEL]) * jnp.concatenate(y_b, axis=1)

    y_a = jnp.dot(oa_ref[...], wpa_ref[...], preferred_element_type=F32)
    merged = jax.nn.sigmoid(z_ref[:, OFF_MA:OFF_MA + D_MODEL]) * y_a + gated_b
    out = x + jnp.dot(merged.astype(BF16), wout_ref[...], preferred_element_type=F32)
    y_ref[0] = _rms(out, fg_ref[...])


def _shift_rows(x, k):
    n = x.shape[0]
    return x if k % n == 0 else pltpu.roll(x, (-k) % n, 0)


def _group_bcast(x, j, group):
    t = lax.broadcasted_iota(jnp.int32, x.shape, 0) & (group - 1)
    out = _shift_rows(x, j - (group - 1))
    for tt in range(group - 2, -1, -1):
        out = jnp.where(t == tt, _shift_rows(x, j - tt), out)
    return out


def _decode_prep_hgrn(heads, z_ref, lbl_ref, a_ref, bv_ref, qd_ref, oi_ref, ga_ref):
    rows = z_ref.shape[0]
    lb = _lower_bound(lbl_ref)
    r = lax.broadcasted_iota(jnp.int32, (rows, DK), 0)
    t = r & (DEC_SEQ - 1)
    even = (r & (PAIR_ROWS - 1)) < DEC_SEQ
    blk = SUB_TILE
    ri = lax.broadcasted_iota(jnp.int32, (blk, blk), 0)
    ci = lax.broadcasted_iota(jnp.int32, (blk, blk), 1)
    causal = ((ri // DEC_SEQ) == (ci // DEC_SEQ)) & (ci <= ri)

    for hh in heads:
        cs = slice(hh * DK, (hh + 1) * DK)
        qf, k, v, logf = _hgrn_inputs(z_ref, lb, hh)
        b = _group_cumsum(logf, DEC_SEQ)
        ref = _group_bcast(b, DEC_SEQ // 2, DEC_SEQ)
        bl = _group_bcast(b, DEC_SEQ - 1, DEC_SEQ)
        qs = (qf * jnp.exp(b - ref)).astype(BF16)
        ks = (k * jnp.exp(ref - b)).astype(BF16)
        kd = k * jnp.exp(bl - b)
        vb = v.astype(BF16)
        qd_ref[:, cs] = (qf * jnp.exp(b)).astype(BF16)
        for s in range(rows // blk):
            rs = slice(s * blk, (s + 1) * blk)
            sc = lax.dot_general(qs[rs], ks[rs], (((1,), (1,)), ((), ())),
                                 preferred_element_type=F32)
            oi_ref[rs, cs] = jnp.dot(jnp.where(causal, sc, 0.0).astype(BF16), vb[rs],
                                     preferred_element_type=F32)
        dec = jnp.exp(bl)
        d1 = dec.astype(BF16).astype(F32)
        d2 = (dec - d1).astype(BF16).astype(F32)
        d3 = (dec - d1 - d2).astype(BF16).astype(F32)
        tail = jnp.where(t == 0, d1, jnp.where(t == 1, d2, jnp.where(t == 2, d3, 0.0)))
        a_ref[hh, 0] = jnp.where(even, kd, _shift_rows(tail, -DEC_SEQ)).astype(BF16)
        a_ref[hh, 1] = jnp.where(even, _shift_rows(tail, DEC_SEQ), kd).astype(BF16)
        zero = jnp.zeros_like(vb)
        bv_ref[hh, 0] = jnp.where(even, vb, zero)
        bv_ref[hh, 1] = jnp.where(even, zero, vb)
        ga_ref[:, cs] = jax.nn.silu(z_ref[:, OFF_GA + hh * DK:OFF_GA + (hh + 1) * DK])


def _decode_prep_history(p_in_ref, rw_ref, p_out_ref):
    n_seq = p_in_ref.shape[1]
    for g, w in enumerate(POOL_WINDOWS):
        gc = slice(g * G_B, (g + 1) * G_B)
        acc = None
        suffix = {}
        for j in range(POOL_BUF - 1, -1, -1):
            e = p_in_ref[j, :, gc]
            acc = e if acc is None else acc + e
            suffix[j] = acc
        for tt in range(DEC_SEQ):
            j = POOL_BUF + 1 - w + tt
            rw_ref[g, pl.ds(tt, n_seq, stride=DEC_SEQ), :] = (
                suffix[j] if j < POOL_BUF else jnp.zeros((n_seq, G_B), F32))
    for j in range(POOL_BUF - DEC_SEQ):
        p_out_ref[j] = p_in_ref[j + DEC_SEQ]


def _decode_prep_pool(z_ref, wpool_ref, ps_ref, wpb_ref, ma_ref, gb_ref, p_out_ref,
                      rw_ref, u_ref, yb_ref):
    rows = z_ref.shape[0]
    n_seq = rows // DEC_SEQ
    t = lax.broadcasted_iota(jnp.int32, (rows, G_B), 0) & (DEC_SEQ - 1)
    for g, w in enumerate(POOL_WINDOWS):
        u = z_ref[:, OFF_U + g * G_B:OFF_U + (g + 1) * G_B]
        u_ref[g] = u
        cu = u
        sh = 1
        while sh < min(w, DEC_SEQ):
            cu = cu + jnp.where(t >= sh, _shift_rows(cu, -sh), 0.0)
            sh *= 2
        pooled = (rw_ref[g] + cu) * (1.0 / w) - u
        _pool_epilogue(pooled, z_ref, g, wpool_ref, ps_ref, yb_ref)
    gb_ref[...] = jax.nn.sigmoid(z_ref[:, OFF_MB:OFF_MB + D_MODEL]) * jnp.dot(
        yb_ref[...], wpb_ref[...], preferred_element_type=F32)
    ma_ref[...] = jax.nn.sigmoid(z_ref[:, OFF_MA:OFF_MA + D_MODEL])
    for tt in range(DEC_SEQ):
        for g in range(len(POOL_WINDOWS)):
            p_out_ref[POOL_BUF - DEC_SEQ + tt, :, g * G_B:(g + 1) * G_B] = (
                u_ref[g, pl.ds(tt, n_seq, stride=DEC_SEQ), :])


def _decode_prep(part, x_ref, p_in_ref, ng_ref, win_ref, lbl_ref, wpool_ref, ps_ref, wpb_ref,
                 a_ref, bv_ref, qd_ref, oi_ref, ga_ref, ma_ref, gb_ref, p_out_ref,
                 xs_ref, z_ref, rw_ref, u_ref, yb_ref):
    hgrn = functools.partial(_decode_prep_hgrn, z_ref=z_ref, lbl_ref=lbl_ref, a_ref=a_ref,
                             bv_ref=bv_ref, qd_ref=qd_ref, oi_ref=oi_ref, ga_ref=ga_ref)
    if part == 0:
        for i in range(x_ref.shape[0]):
            xs_ref[i * DEC_SEQ:(i + 1) * DEC_SEQ, :] = x_ref[i]
        _in_proj(xs_ref[...], ng_ref, win_ref, z_ref, 0, OFF_U)
        _decode_prep_history(p_in_ref, rw_ref, p_out_ref)
    elif part == 1:
        _in_proj(xs_ref[...], ng_ref, win_ref, z_ref, OFF_U, D_IN)
    elif part == 2:
        hgrn(range(0, N_HEADS - 1))
    else:
        hgrn(range(N_HEADS - 1, N_HEADS))
        _decode_prep_pool(z_ref, wpool_ref, ps_ref, wpb_ref, ma_ref, gb_ref, p_out_ref,
                          rw_ref, u_ref, yb_ref)


def _decode_state_step(a_ref, bv_ref, qd_ref, s_in_ref, s_out_ref, o_ref):
    n_pairs = BB_SAMPLE // 2
    r8 = lax.broadcasted_iota(jnp.int32, (PAIR_ROWS, DV), 0)
    even = r8 < DEC_SEQ
    ones = (jnp.where((r8 >= DEC_SEQ) & (r8 < DEC_SEQ + 3), 1.0, 0.0).astype(BF16),
            jnp.where(r8 < 3, 1.0, 0.0).astype(BF16))
    for hh in range(N_HEADS):
        for lp in range(n_pairs):
            rs = slice(lp * PAIR_ROWS, (lp + 1) * PAIR_ROWS)
            for e in range(2):
                rhs = jnp.concatenate([bv_ref[hh, e, rs, :], ones[e]], axis=1)
                upd = lax.dot_general(a_ref[hh, e, rs, :], rhs, (((0,), (0,)), ((), ())),
                                      preferred_element_type=F32)
                s_out_ref[2 * lp + e, hh] = (upd[:, DV:2 * DV] * s_in_ref[2 * lp + e, hh]
                                             + upd[:, 0:DV])
    for hh in range(N_HEADS):
        cs = slice(hh * DK, (hh + 1) * DK)
        for lp in range(n_pairs):
            rs = slice(lp * PAIR_ROWS, (lp + 1) * PAIR_ROWS)
            q8 = qd_ref[rs, cs]
            o0 = jnp.dot(q8, s_in_ref[2 * lp, hh].astype(BF16), preferred_element_type=F32)
            o1 = jnp.dot(q8, s_in_ref[2 * lp + 1, hh].astype(BF16), preferred_element_type=F32)
            o_ref[rs, cs] = o_ref[rs, cs] + jnp.where(even, o0, o1)


def _decode_finish(o_ref, ga_ref, ma_ref, gb_ref, xs_ref, hg_ref, wpa_ref, wout_ref,
                   fg_ref, y_ref, oa_ref, *, n_seq):
    for hh in range(N_HEADS):
        cs = slice(hh * DK, (hh + 1) * DK)
        oa_ref[:, cs] = (_rms(o_ref[:, cs], hg_ref[...]) * ga_ref[:, cs]).astype(BF16)
    y_a = jnp.dot(oa_ref[...], wpa_ref[...], preferred_element_type=F32)
    merged = ma_ref[...] * y_a + gb_ref[...]
    out = xs_ref[...] + jnp.dot(merged.astype(BF16), wout_ref[...], preferred_element_type=F32)
    y = _rms(out, fg_ref[...])
    for i in range(n_seq):
        y_ref[i] = y[i * DEC_SEQ:(i + 1) * DEC_SEQ, :]


def _state_copies(hbm_ref, buf_ref, sem, blk, slot, to_hbm):
    n = BB_SAMPLE // STATE_DMAS
    copies = []
    for c in range(STATE_DMAS):
        in_hbm = hbm_ref.at[pl.ds(blk * BB_SAMPLE + c * n, n)]
        in_vmem = buf_ref.at[slot, pl.ds(c * n, n)]
        src, dst = (in_vmem, in_hbm) if to_hbm else (in_hbm, in_vmem)
        copies.append(pltpu.make_async_copy(src, dst, sem.at[slot, c]))
    return copies


def _decode_kernel(xp_ref, p_in_ref, s_in_hbm,
                   ng_ref, win_ref, lbl_ref, wpool_ref, ps_ref, wpb_ref,
                   hg_ref, wpa_ref, wout_ref, fg_ref,
                   s_out_hbm, y_ref, p_out_ref,
                   a_all, bv_all, qd_all, o_all, ga_all, ma_all, gb_all, xs_all,
                   z_ref, rw_ref, u_ref, yb_ref, oa_ref, s_in_buf, s_out_buf, in_sem, out_sem,
                   *, n_prep):
    s = pl.program_id(0)
    prows = PREP_SEQS * DEC_SEQ
    brows = BB_SAMPLE * DEC_SEQ
    n_state = n_prep * PREP_PARTS

    def load(blk, slot):
        return _state_copies(s_in_hbm, s_in_buf, in_sem, blk, slot, to_hbm=False)

    def store(blk, slot):
        return _state_copies(s_out_hbm, s_out_buf, out_sem, blk, slot, to_hbm=True)

    def block_rows(blk):
        start = blk * prows
        return pl.ds(start if isinstance(start, int) else pl.multiple_of(start, prows), prows)

    def prep(part, blk):
        r = block_rows(blk)
        _decode_prep(part, xp_ref, p_in_ref, ng_ref, win_ref, lbl_ref, wpool_ref, ps_ref, wpb_ref,
                     a_all.at[:, :, r, :], bv_all.at[:, :, r, :], qd_all.at[r, :],
                     o_all.at[r, :], ga_all.at[r, :], ma_all.at[r, :], gb_all.at[r, :],
                     p_out_ref, xs_all.at[r, :], z_ref, rw_ref, u_ref, yb_ref)

    def finish(blk):
        f = block_rows(blk)
        _decode_finish(o_all.at[f, :], ga_all.at[f, :], ma_all.at[f, :], gb_all.at[f, :],
                       xs_all.at[f, :], hg_ref, wpa_ref, wout_ref, fg_ref, y_ref, oa_ref,
                       n_seq=PREP_SEQS)

    i = s - 1
    blk = lax.div(i, PREP_PARTS)
    part_now = lax.rem(i, PREP_PARTS)
    slot = lax.rem(i, 2)

    @pl.when(s == 0)
    def _():
        for b in range(2):
            for c in load(b, b):
                c.start()
        for part in range(PREP_PARTS):
            prep(part, 0)

    @pl.when(s > 0)
    def _():
        @pl.when((i > 0) & (i + 1 < n_state))
        def _():
            for c in load(i + 1, 1 - slot):
                c.start()
        for c in load(i, slot):
            c.wait()

        @pl.when(i >= 2)
        def _():
            for c in store(i - 2, slot):
                c.wait()
        r = pl.ds(pl.multiple_of(i * brows, brows), brows)
        _decode_state_step(a_all.at[:, :, r, :], bv_all.at[:, :, r, :], qd_all.at[r, :],
                           s_in_buf.at[slot], s_out_buf.at[slot], o_all.at[r, :])
        for c in store(i, slot):
            c.start()

    for part in range(PREP_PARTS):
        @pl.when((part_now == part) & (blk + 1 < n_prep))
        def _():
            prep(part, blk + 1)

    @pl.when((part_now == 0) & (i > 0))
    def _():
        finish(blk - 1)

    @pl.when(i == n_state - 1)
    def _():
        finish(n_prep - 1)
        for c in store(i - 1, 1 - slot) + store(i, slot):
            c.wait()


def _const_spec(shape):
    n = len(shape)
    return pl.BlockSpec(shape, lambda *_: (0,) * n, pipeline_mode=pl.Buffered(1))


def kernel(x_prompt, x_sample, state_hgrn, state_pool, norm_g, w_in, lb_logits, hgrn_norm_g,
           w_proj_a, w_pool, pool_scale, w_proj_b, w_out, final_norm_g):
    batch, seq, _ = x_prompt.shape
    dec_batch, dec_seq, _ = x_sample.shape
    assert norm_g.shape[0] == 1 and lb_logits.shape[0] == 2, "single-layer decoder only"
    assert seq % TM_PROMPT == 0 and TM_PROMPT % SUB_TILE == 0 and dec_batch % BB_SAMPLE == 0
    assert dec_seq == DEC_SEQ and PAIR_ROWS == SUBLANES and BB_SAMPLE % 2 == 0
    assert dec_batch % PREP_SEQS == 0 and (PREP_SEQS * DEC_SEQ) % SUB_TILE == 0
    assert PREP_SEQS == PREP_PARTS * BB_SAMPLE and BB_SAMPLE % STATE_DMAS == 0
    assert PAST_LEN >= max(POOL_WINDOWS)

    n_groups = len(POOL_WINDOWS)
    tm = TM_PROMPT
    nt = seq // tm
    assert w_in.shape[1:] == (2 * tm, D_IN) and w_proj_a.shape[1] == tm, "weight staging layout"
    hbm = pl.BlockSpec(memory_space=pl.ANY)
    bf16_shapes = [(D_MODEL, D_IN), (W_BRANCH, D_MODEL), (n_groups, G_B, G_B),
                   (W_BRANCH, D_MODEL), (D_MODEL, D_MODEL)]
    y_p, s_p, p_p, win_b, wpa_b, wpool_b, wpb_b, wout_b = pl.pallas_call(
        _prompt_kernel,
        grid=(batch, nt),
        in_specs=[pl.BlockSpec((1, tm, D_MODEL), lambda b, t: (b, t, 0)),
                  _const_spec((1, D_MODEL)), hbm, _const_spec((2, W_BRANCH)), _const_spec((1, DV)),
                  hbm, hbm, _const_spec((1, W_BRANCH)), hbm, hbm, _const_spec((1, D_MODEL))],
        out_specs=[
            pl.BlockSpec((1, tm, D_MODEL), lambda b, t: (b, t, 0)),
            pl.BlockSpec((1, N_HEADS, DK, DV), lambda b, t: (b, 0, 0, 0)),
            pl.BlockSpec((POOL_BUF, batch, W_BRANCH), lambda b, t: (0, 0, 0)),
        ] + [hbm] * len(bf16_shapes),
        out_shape=[
            jax.ShapeDtypeStruct((batch, seq, D_MODEL), F32),
            jax.ShapeDtypeStruct((batch, N_HEADS, DK, DV), F32),
            jax.ShapeDtypeStruct((POOL_BUF, batch, W_BRANCH), F32),
        ] + [jax.ShapeDtypeStruct(s, BF16) for s in bf16_shapes],
        scratch_shapes=[
            pltpu.VMEM((tm, D_IN), F32),
            pltpu.VMEM((N_HEADS, DV, DK), F32),
            pltpu.VMEM((tm + 2 * SUBLANES, W_BRANCH), F32),
            pltpu.VMEM((N_HEADS, tm, (SUB_TILE // CHUNK) * DK), BF16),
            pltpu.VMEM((N_HEADS, tm // CHUNK, DK, DV), BF16),
            pltpu.VMEM((tm, W_BRANCH), BF16),
            pltpu.VMEM((tm, W_BRANCH), BF16),
            pltpu.VMEM((tm, W_BRANCH), BF16),
            pltpu.VMEM((tm, W_BRANCH), BF16),
            pltpu.VMEM((tm, W_BRANCH), BF16),
            pltpu.VMEM((tm, W_BRANCH), BF16),
        ] + [pltpu.VMEM(s, BF16) for s in bf16_shapes] + [
            pltpu.SemaphoreType.DMA((7,)),
            pltpu.SemaphoreType.DMA((len(bf16_shapes),)),
        ],
        compiler_params=pltpu.CompilerParams(
            dimension_semantics=("arbitrary", "arbitrary"),
            vmem_limit_bytes=VMEM_LIMIT_BYTES),
        name="hgrn2_pool_prompt",
    )(x_prompt, norm_g, w_in[0], lb_logits, hgrn_norm_g, w_proj_a[0],
      w_pool[0].reshape(n_groups * G_B, G_B), pool_scale, w_proj_b[0], w_out[0],
      final_norm_g.reshape(1, D_MODEL))
    weights = (norm_g, win_b, lb_logits, hgrn_norm_g, wpa_b, wpool_b, pool_scale, wpb_b, wout_b,
               final_norm_g.reshape(1, D_MODEL))

    rows = dec_batch * DEC_SEQ
    pool_in = jnp.transpose(state_pool[0], (1, 0, 2))
    ng, win_b, lbl, hg, wpa_b, wpool_b, ps, wpb_b, wout_b, fg = weights
    pseq = PREP_SEQS
    prows = pseq * DEC_SEQ
    n_prep = dec_batch // pseq
    bb = BB_SAMPLE

    def prep_blk(s):
        return jnp.minimum((s + PREP_PARTS - 1) // PREP_PARTS, n_prep - 1)

    def finish_blk(s):
        return jnp.minimum(jnp.maximum(s - 2, 0) // PREP_PARTS, n_prep - 1)

    s_s, y_s, pool_out = pl.pallas_call(
        functools.partial(_decode_kernel, n_prep=n_prep),
        grid=(1 + dec_batch // bb,),
        in_specs=[
            pl.BlockSpec((pseq, DEC_SEQ, D_MODEL), lambda s: (prep_blk(s), 0, 0)),
            pl.BlockSpec((POOL_BUF, pseq, W_BRANCH), lambda s: (0, prep_blk(s), 0),
                         pipeline_mode=pl.Buffered(1)),
            hbm,
            _const_spec(ng.shape), _const_spec(win_b.shape), _const_spec(lbl.shape),
            _const_spec(wpool_b.shape), _const_spec(ps.shape), _const_spec(wpb_b.shape),
            _const_spec(hg.shape), _const_spec(wpa_b.shape), _const_spec(wout_b.shape),
            _const_spec(fg.shape),
        ],
        out_specs=[
            hbm,
            pl.BlockSpec((pseq, DEC_SEQ, D_MODEL), lambda s: (finish_blk(s), 0, 0)),
            pl.BlockSpec((POOL_BUF, pseq, W_BRANCH), lambda s: (0, prep_blk(s), 0)),
        ],
        out_shape=[
            jax.ShapeDtypeStruct((dec_batch, N_HEADS, DK, DV), F32),
            jax.ShapeDtypeStruct(x_sample.shape, F32),
            jax.ShapeDtypeStruct((POOL_BUF, dec_batch, W_BRANCH), F32),
        ],
        scratch_shapes=[
            pltpu.VMEM((N_HEADS, 2, rows, DK), BF16),
            pltpu.VMEM((N_HEADS, 2, rows, DV), BF16),
            pltpu.VMEM((rows, W_BRANCH), BF16),
            pltpu.VMEM((rows, W_BRANCH), F32),
            pltpu.VMEM((rows, W_BRANCH), F32),
            pltpu.VMEM((rows, D_MODEL), F32),
            pltpu.VMEM((rows, D_MODEL), F32),
            pltpu.VMEM((rows, D_MODEL), F32),
            pltpu.VMEM((prows, D_IN), F32),
            pltpu.VMEM((len(POOL_WINDOWS), prows, G_B), F32),
            pltpu.VMEM((len(POOL_WINDOWS), prows, G_B), F32),
            pltpu.VMEM((prows, W_BRANCH), BF16),
            pltpu.VMEM((prows, W_BRANCH), BF16),
            pltpu.VMEM((2, bb, N_HEADS, DK, DV), F32),
            pltpu.VMEM((2, bb, N_HEADS, DK, DV), F32),
            pltpu.SemaphoreType.DMA((2, STATE_DMAS)),
            pltpu.SemaphoreType.DMA((2, STATE_DMAS)),
        ],
        compiler_params=pltpu.CompilerParams(
            dimension_semantics=("arbitrary",), vmem_limit_bytes=DECODE_VMEM_LIMIT_BYTES),
        name="hgrn2_pool_decode",
    )(x_sample, pool_in, state_hgrn[0], ng, win_b, lbl, wpool_b, ps, wpb_b,
      hg, wpa_b, wout_b, fg)

    p_p = jnp.transpose(p_p, (1, 0, 2))
    p_s = jnp.transpose(pool_out, (1, 0, 2))
    return (y_p, y_s, s_p[None], p_p[None], s_s[None], p_s[None])
```

```python
import functools

import jax
import jax.numpy as jnp
from jax import lax
from jax.experimental import pallas as pl
from jax.experimental.pallas import tpu as pltpu

F32 = jnp.float32
BF16 = jnp.bfloat16

D_MODEL = 1024
W_BRANCH = 512
N_HEADS = 4
DK = 128
DV = 128
CHUNK = 32
POOL_WINDOWS = (2, 4, 8, 16)
G_B = 128
POOL_BUF = 15
PAST_LEN = 16384
EPS = 1e-6
D_IN = 4 * W_BRANCH + 2 * W_BRANCH + 2 * D_MODEL
OFF_Q, OFF_F, OFF_I, OFF_GA = 0, 512, 1024, 1536
OFF_U, OFF_GB, OFF_MA, OFF_MB = 2048, 2560, 3072, 4096

SUBLANES = 8
VMEM_LIMIT_BYTES = 56 * 1024 * 1024
DECODE_VMEM_LIMIT_BYTES = 60 * 1024 * 1024

TM_PROMPT = 512
WIN_SLOTS = 4
SUB_TILE = 256
PREP_SEQS = 64
BB_SAMPLE = 16
STATE_DMAS = 8
DEC_SEQ = 4
PAIR_ROWS = 2 * DEC_SEQ
PREP_PARTS = 4


def _rms(x, g):
    ms = jnp.mean(x * x, axis=-1, keepdims=True)
    return x * lax.rsqrt(ms + EPS) * g


def _lower_bound(lbl_ref):
    l0 = lbl_ref[0:1, :]
    l1 = lbl_ref[1:2, :]
    m = jnp.maximum(l0, l1)
    e0 = jnp.exp(l0 - m)
    e1 = jnp.exp(l1 - m)
    return e0 / (e0 + e1)


def _in_proj(x, ng_ref, win_ref, z_ref, c0=0, c1=D_IN):
    h = _rms(x, ng_ref[...]).astype(BF16)
    z_ref[:, c0:c1] = jnp.dot(h, win_ref[:, c0:c1], preferred_element_type=F32)


def _hgrn_inputs(z_ref, lb, hh):
    c = hh * DK
    q = z_ref[:, OFF_Q + c:OFF_Q + c + DK]
    f = z_ref[:, OFF_F + c:OFF_F + c + DK]
    v = z_ref[:, OFF_I + c:OFF_I + c + DK]
    lbh = lb[:, c:c + DK]
    fg = lbh + (1.0 - lbh) * jax.nn.sigmoid(f)
    logf = jnp.log(fg)
    k = 1.0 - fg
    qf = jax.nn.silu(q) * (DK ** -0.5)
    return qf, k, v, logf


def _group_cumsum(x, group):
    row = lax.broadcasted_iota(jnp.int32, x.shape, 0) & (group - 1)
    s = 1
    while s < group:
        x = x + jnp.where(row >= s, pltpu.roll(x, s, 0), 0.0)
        s *= 2
    return x


def _hgrn_epilogue(o, z_ref, hg_ref, hh, oa_ref):
    c = hh * DK
    ga = z_ref[:, OFF_GA + c:OFF_GA + c + DK]
    on = _rms(o, hg_ref[...]) * jax.nn.silu(ga)
    oa_ref[:, c:c + DK] = on.astype(BF16)


def _pool_epilogue(pooled, z_ref, g, wpool_ref, ps_ref, yb_ref):
    c = g * G_B
    mixed = jnp.dot(pooled.astype(BF16), wpool_ref[g], preferred_element_type=F32)
    gb = z_ref[:, OFF_GB + c:OFF_GB + c + G_B]
    yb = mixed * ps_ref[:, c:c + G_B] * jax.nn.silu(gb)
    yb_ref[:, c:c + G_B] = yb.astype(BF16)


def _weight_copies(hbm_refs, stage_ref, sem):
    win_hbm, wpa_hbm, wpool_hbm, wpb_hbm, wout_hbm = hbm_refs
    rows = stage_ref.shape[0]
    small = [
        (wpa_hbm, stage_ref.at[:, 0:D_MODEL]),
        (wpb_hbm, stage_ref.at[:, D_MODEL:2 * D_MODEL]),
        (wout_hbm.at[0:rows, :], stage_ref.at[:, 2 * D_MODEL:3 * D_MODEL]),
        (wout_hbm.at[rows:2 * rows, :], stage_ref.at[:, 3 * D_MODEL:4 * D_MODEL]),
        (wpool_hbm, stage_ref.at[:, 4 * D_MODEL:4 * D_MODEL + G_B]),
    ]
    small = [pltpu.make_async_copy(src, dst, sem.at[i]) for i, (src, dst) in enumerate(small)]
    blk = rows // WIN_SLOTS

    def win(k):
        slot = k % WIN_SLOTS
        return pltpu.make_async_copy(win_hbm.at[k * blk:(k + 1) * blk, :],
                                     stage_ref.at[slot * blk:(slot + 1) * blk, :],
                                     sem.at[len(small) + slot])
    return small, win, blk


def _weight_exports(vmem_refs, out_refs, sem):
    return [pltpu.make_async_copy(src, dst, sem.at[i])
            for i, (src, dst) in enumerate(zip(vmem_refs, out_refs))]


def _prompt_kernel(x_ref, ng_ref, win_hbm, lbl_ref, hg_ref, wpa_hbm, wpool_hbm, ps_ref,
                   wpb_hbm, wout_hbm, fg_ref,
                   y_ref, s_out_ref, p_out_ref, win_out, wpa_out, wpool_out, wpb_out, wout_out,
                   z_ref, st_ref, ext_ref, kx_ref, sn_ref, qs_ref, ks_ref, qd_ref, vb_ref,
                   oa_ref, yb_ref, win_ref, wpa_ref, wpool_ref, wpb_ref, wout_ref,
                   load_sem, export_sem):
    tm = TM_PROMPT
    sub = SUB_TILE
    n_sub = tm // sub
    nc = tm // CHUNK
    ncs = sub // CHUNK
    t = pl.program_id(1)
    first = t == 0
    hdr = 2 * SUBLANES
    bf16_weights = (win_ref, wpa_ref, wpool_ref, wpb_ref, wout_ref)
    exports = _weight_exports(bf16_weights, (win_out, wpa_out, wpool_out, wpb_out, wout_out),
                              export_sem)

    @pl.when((pl.program_id(0) == 0) & first)
    def _():
        small, win, blk = _weight_copies((win_hbm, wpa_hbm, wpool_hbm, wpb_hbm, wout_hbm),
                                         z_ref, load_sem)
        for c in small:
            c.start()
        st_ref[...] = jnp.zeros_like(st_ref)
        ext_ref[...] = jnp.zeros_like(ext_ref)
        kx_ref[...] = jnp.zeros_like(kx_ref)
        for c in small:
            c.wait()
        wpa_ref[...] = z_ref[:, 0:D_MODEL].astype(BF16)
        wpb_ref[...] = z_ref[:, D_MODEL:2 * D_MODEL].astype(BF16)
        wout_ref[0:tm, :] = z_ref[:, 2 * D_MODEL:3 * D_MODEL].astype(BF16)
        wout_ref[tm:2 * tm, :] = z_ref[:, 3 * D_MODEL:4 * D_MODEL].astype(BF16)
        for g in range(len(POOL_WINDOWS)):
            wpool_ref[g] = z_ref[g * G_B:(g + 1) * G_B,
                                 4 * D_MODEL:4 * D_MODEL + G_B].astype(BF16)
        n_blk = D_MODEL // blk
        for k in range(WIN_SLOTS):
            win(k).start()
        for k in range(n_blk):
            slot = k % WIN_SLOTS
            win(k).wait()
            win_ref[k * blk:(k + 1) * blk, :] = z_ref[slot * blk:(slot + 1) * blk, :].astype(BF16)
            if k + WIN_SLOTS < n_blk:
                win(k + WIN_SLOTS).start()
        for c in exports:
            c.start()

    @pl.when((pl.program_id(0) == pl.num_programs(0) - 1) & (t == pl.num_programs(1) - 1))
    def _():
        for c in exports:
            c.wait()

    x = x_ref[0]
    _in_proj(x, ng_ref, win_ref, z_ref)
    lb = _lower_bound(lbl_ref)

    ri = lax.broadcasted_iota(jnp.int32, (sub, sub), 0)
    ci = lax.broadcasted_iota(jnp.int32, (sub, sub), 1)
    causal = ((ri // CHUNK) == (ci // CHUNK)) & (ci <= ri)

    heads = range(N_HEADS)
    units = [(hh, s) for hh in heads for s in range(n_sub)]

    def rows(s):
        return slice(s * sub, (s + 1) * sub)

    def cols(hh):
        return slice(hh * DK, (hh + 1) * DK)

    dec = []
    for hh in heads:
        qf, k, v, logf = _hgrn_inputs(z_ref, lb, hh)
        b = _group_cumsum(logf, CHUNK)
        b3 = b.reshape(nc, CHUNK, DK)
        ref = b3[:, CHUNK // 2:CHUNK // 2 + 1, :]
        bl = b3[:, CHUNK - 1:CHUNK, :]
        q3 = qf.reshape(nc, CHUNK, DK)
        k3 = k.reshape(nc, CHUNK, DK)
        qs_ref[:, cols(hh)] = (q3 * jnp.exp(b3 - ref)).reshape(tm, DK).astype(BF16)
        ks_ref[:, cols(hh)] = (k3 * jnp.exp(ref - b3)).reshape(tm, DK).astype(BF16)
        qd_ref[:, cols(hh)] = (q3 * jnp.exp(b3)).reshape(tm, DK).astype(BF16)
        vb_ref[:, cols(hh)] = v.astype(BF16)
        kd = (k3 * jnp.exp(bl - b3)).reshape(tm, DK).astype(BF16)
        dec.append(jnp.exp(bl))
        for c in range(nc):
            cc = c % ncs
            kx_ref[hh, c * CHUNK:(c + 1) * CHUNK, cc * DK:(cc + 1) * DK] = (
                kd[c * CHUNK:(c + 1) * CHUNK, :])

    ext_ref[0:hdr, :] = jnp.where(first, 0.0, ext_ref[tm:tm + hdr, :])
    ext_ref[hdr:hdr + tm, :] = z_ref[:, OFF_U:OFF_U + W_BRANCH]
    pos1 = t * tm + lax.broadcasted_iota(jnp.int32, (tm, 1), 0) + 1
    for g, w in enumerate(POOL_WINDOWS):
        c = g * G_B
        s = ext_ref[:, c:c + G_B]
        sh = 1
        while sh < w:
            s = s + pltpu.roll(s, sh, 0)
            sh *= 2
        inv = jnp.where(pos1 >= w, 1.0 / w, 1.0 / pos1.astype(F32))
        pooled = s[hdr:, :] * inv - z_ref[:, OFF_U + c:OFF_U + c + G_B]
        _pool_epilogue(pooled, z_ref, g, wpool_ref, ps_ref, yb_ref)
    for j in range(POOL_BUF):
        r = hdr + tm - POOL_BUF + j
        p_out_ref[j, pl.ds(pl.program_id(0), 1), :] = ext_ref[r:r + 1, :]

    sc = {(hh, s): lax.dot_general(qs_ref[rows(s), cols(hh)], ks_ref[rows(s), cols(hh)],
                                   (((1,), (1,)), ((), ())), preferred_element_type=F32)
          for hh, s in units}
    ut = {(hh, s): lax.dot_general(vb_ref[rows(s), cols(hh)], kx_ref[hh, rows(s), :],
                                   (((0,), (0,)), ((), ())), preferred_element_type=F32)
          for hh, s in units}
    o = {(hh, s): jnp.dot(jnp.where(causal, sc[hh, s], 0.0).astype(BF16),
                          vb_ref[rows(s), cols(hh)], preferred_element_type=F32)
         for hh, s in units}
    for hh in heads:
        st = jnp.where(first, 0.0, st_ref[hh])
        for c in range(nc):
            sn_ref[hh, c] = st.T.astype(BF16)
            st = dec[hh][c] * st + ut[hh, c // ncs][:, (c % ncs) * DK:(c % ncs + 1) * DK]
        st_ref[hh] = st
        s_out_ref[0, hh] = st.T
    wcol = D_MODEL // N_HEADS
    y_b = []
    for hh in heads:
        o_inter = [jnp.dot(qd_ref[c * CHUNK:(c + 1) * CHUNK, cols(hh)], sn_ref[hh, c],
                           preferred_element_type=F32) for c in range(nc)]
        y_b.append(jnp.dot(yb_ref[...], wpb_ref[:, hh * wcol:(hh + 1) * wcol],
                           preferred_element_type=F32))
        o_hh = jnp.concatenate([o[hh, s] for s in range(n_sub)], axis=0)
        _hgrn_epilogue(o_hh + jnp.concatenate(o_inter, axis=0), z_ref, hg_ref, hh, oa_ref)
    gated_b = jax.nn.sigmoid(z_ref[:, OFF_MB:OFF_MB + D_MODEL]) * jnp.concatenate(y_b, axis=1)

    y_a = jnp.dot(oa_ref[...], wpa_ref[...], preferred_element_type=F32)
    merged = jax.nn.sigmoid(z_ref[:, OFF_MA:OFF_MA + D_MODEL]) * y_a + gated_b
    out = x + jnp.dot(merged.astype(BF16), wout_ref[...], preferred_element_type=F32)
    y_ref[0] = _rms(out, fg_ref[...])


def _shift_rows(x, k):
    n = x.shape[0]
    return x if k % n == 0 else pltpu.roll(x, (-k) % n, 0)


def _group_bcast(x, j, group):
    t = lax.broadcasted_iota(jnp.int32, x.shape, 0) & (group - 1)
    out = _shift_rows(x, j - (group - 1))
    for tt in range(group - 2, -1, -1):
        out = jnp.where(t == tt, _shift_rows(x, j - tt), out)
    return out


def _decode_prep_hgrn(heads, z_ref, lbl_ref, a_ref, bv_ref, qd_ref, oi_ref, ga_ref):
    rows = z_ref.shape[0]
    lb = _lower_bound(lbl_ref)
    r = lax.broadcasted_iota(jnp.int32, (rows, DK), 0)
    t = r & (DEC_SEQ - 1)
    even = (r & (PAIR_ROWS - 1)) < DEC_SEQ
    blk = SUB_TILE
    ri = lax.broadcasted_iota(jnp.int32, (blk, blk), 0)
    ci = lax.broadcasted_iota(jnp.int32, (blk, blk), 1)
    causal = ((ri // DEC_SEQ) == (ci // DEC_SEQ)) & (ci <= ri)

    for hh in heads:
        cs = slice(hh * DK, (hh + 1) * DK)
        qf, k, v, logf = _hgrn_inputs(z_ref, lb, hh)
        b = _group_cumsum(logf, DEC_SEQ)
        ref = _group_bcast(b, DEC_SEQ // 2, DEC_SEQ)
        bl = _group_bcast(b, DEC_SEQ - 1, DEC_SEQ)
        qs = (qf * jnp.exp(b - ref)).astype(BF16)
        ks = (k * jnp.exp(ref - b)).astype(BF16)
        kd = k * jnp.exp(bl - b)
        vb = v.astype(BF16)
        qd_ref[:, cs] = (qf * jnp.exp(b)).astype(BF16)
        for s in range(rows // blk):
            rs = slice(s * blk, (s + 1) * blk)
            sc = lax.dot_general(qs[rs], ks[rs], (((1,), (1,)), ((), ())),
                                 preferred_element_type=F32)
            oi_ref[rs, cs] = jnp.dot(jnp.where(causal, sc, 0.0).astype(BF16), vb[rs],
                                     preferred_element_type=F32)
        dec = jnp.exp(bl)
        d1 = dec.astype(BF16).astype(F32)
        d2 = (dec - d1).astype(BF16).astype(F32)
        d3 = (dec - d1 - d2).astype(BF16).astype(F32)
        tail = jnp.where(t == 0, d1, jnp.where(t == 1, d2, jnp.where(t == 2, d3, 0.0)))
        a_ref[hh, 0] = jnp.where(even, kd, _shift_rows(tail, -DEC_SEQ)).astype(BF16)
        a_ref[hh, 1] = jnp.where(even, _shift_rows(tail, DEC_SEQ), kd).astype(BF16)
        zero = jnp.zeros_like(vb)
        bv_ref[hh, 0] = jnp.where(even, vb, zero)
        bv_ref[hh, 1] = jnp.where(even, zero, vb)
        ga_ref[:, cs] = jax.nn.silu(z_ref[:, OFF_GA + hh * DK:OFF_GA + (hh + 1) * DK])


def _decode_prep_history(p_in_ref, rw_ref, p_out_ref):
    n_seq = p_in_ref.shape[1]
    for g, w in enumerate(POOL_WINDOWS):
        gc = slice(g * G_B, (g + 1) * G_B)
        acc = None
        suffix = {}
        for j in range(POOL_BUF - 1, -1, -1):
            e = p_in_ref[j, :, gc]
            acc = e if acc is None else acc + e
            suffix[j] = acc
        for tt in range(DEC_SEQ):
            j = POOL_BUF + 1 - w + tt
            rw_ref[g, pl.ds(tt, n_seq, stride=DEC_SEQ), :] = (
                suffix[j] if j < POOL_BUF else jnp.zeros((n_seq, G_B), F32))
    for j in range(POOL_BUF - DEC_SEQ):
        p_out_ref[j] = p_in_ref[j + DEC_SEQ]


def _decode_prep_pool(z_ref, wpool_ref, ps_ref, wpb_ref, ma_ref, gb_ref, p_out_ref,
                      rw_ref, u_ref, yb_ref):
    rows = z_ref.shape[0]
    n_seq = rows // DEC_SEQ
    t = lax.broadcasted_iota(jnp.int32, (rows, G_B), 0) & (DEC_SEQ - 1)
    for g, w in enumerate(POOL_WINDOWS):
        u = z_ref[:, OFF_U + g * G_B:OFF_U + (g + 1) * G_B]
        u_ref[g] = u
        cu = u
        sh = 1
        while sh < min(w, DEC_SEQ):
            cu = cu + jnp.where(t >= sh, _shift_rows(cu, -sh), 0.0)
            sh *= 2
        pooled = (rw_ref[g] + cu) * (1.0 / w) - u
        _pool_epilogue(pooled, z_ref, g, wpool_ref, ps_ref, yb_ref)
    gb_ref[...] = jax.nn.sigmoid(z_ref[:, OFF_MB:OFF_MB + D_MODEL]) * jnp.dot(
        yb_ref[...], wpb_ref[...], preferred_element_type=F32)
    ma_ref[...] = jax.nn.sigmoid(z_ref[:, OFF_MA:OFF_MA + D_MODEL])
    for tt in range(DEC_SEQ):
        for g in range(len(POOL_WINDOWS)):
            p_out_ref[POOL_BUF - DEC_SEQ + tt, :, g * G_B:(g + 1) * G_B] = (
                u_ref[g, pl.ds(tt, n_seq, stride=DEC_SEQ), :])


def _decode_prep(part, x_ref, p_in_ref, ng_ref, win_ref, lbl_ref, wpool_ref, ps_ref, wpb_ref,
                 a_ref, bv_ref, qd_ref, oi_ref, ga_ref, ma_ref, gb_ref, p_out_ref,
                 xs_ref, z_ref, rw_ref, u_ref, yb_ref):
    hgrn = functools.partial(_decode_prep_hgrn, z_ref=z_ref, lbl_ref=lbl_ref, a_ref=a_ref,
                             bv_ref=bv_ref, qd_ref=qd_ref, oi_ref=oi_ref, ga_ref=ga_ref)
    if part == 0:
        for i in range(x_ref.shape[0]):
            xs_ref[i * DEC_SEQ:(i + 1) * DEC_SEQ, :] = x_ref[i]
        _in_proj(xs_ref[...], ng_ref, win_ref, z_ref, 0, OFF_U)
        _decode_prep_history(p_in_ref, rw_ref, p_out_ref)
    elif part == 1:
        _in_proj(xs_ref[...], ng_ref, win_ref, z_ref, OFF_U, D_IN)
    elif part == 2:
        hgrn(range(0, N_HEADS - 1))
    else:
        hgrn(range(N_HEADS - 1, N_HEADS))
        _decode_prep_pool(z_ref, wpool_ref, ps_ref, wpb_ref, ma_ref, gb_ref, p_out_ref,
                          rw_ref, u_ref, yb_ref)


def _decode_state_step(a_ref, bv_ref, qd_ref, s_in_ref, s_out_ref, o_ref):
    n_pairs = BB_SAMPLE // 2
    r8 = lax.broadcasted_iota(jnp.int32, (PAIR_ROWS, DV), 0)
    even = r8 < DEC_SEQ
    ones = (jnp.where((r8 >= DEC_SEQ) & (r8 < DEC_SEQ + 3), 1.0, 0.0).astype(BF16),
            jnp.where(r8 < 3, 1.0, 0.0).astype(BF16))
    for hh in range(N_HEADS):
        for lp in range(n_pairs):
            rs = slice(lp * PAIR_ROWS, (lp + 1) * PAIR_ROWS)
            for e in range(2):
                rhs = jnp.concatenate([bv_ref[hh, e, rs, :], ones[e]], axis=1)
                upd = lax.dot_general(a_ref[hh, e, rs, :], rhs, (((0,), (0,)), ((), ())),
                                      preferred_element_type=F32)
                s_out_ref[2 * lp + e, hh] = (upd[:, DV:2 * DV] * s_in_ref[2 * lp + e, hh]
                                             + upd[:, 0:DV])
    for hh in range(N_HEADS):
        cs = slice(hh * DK, (hh + 1) * DK)
        for lp in range(n_pairs):
            rs = slice(lp * PAIR_ROWS, (lp + 1) * PAIR_ROWS)
            q8 = qd_ref[rs, cs]
            o0 = jnp.dot(q8, s_in_ref[2 * lp, hh].astype(BF16), preferred_element_type=F32)
            o1 = jnp.dot(q8, s_in_ref[2 * lp + 1, hh].astype(BF16), preferred_element_type=F32)
            o_ref[rs, cs] = o_ref[rs, cs] + jnp.where(even, o0, o1)


def _decode_finish(o_ref, ga_ref, ma_ref, gb_ref, xs_ref, hg_ref, wpa_ref, wout_ref,
                   fg_ref, y_ref, oa_ref, *, n_seq):
    for hh in range(N_HEADS):
        cs = slice(hh * DK, (hh + 1) * DK)
        oa_ref[:, cs] = (_rms(o_ref[:, cs], hg_ref[...]) * ga_ref[:, cs]).astype(BF16)
    y_a = jnp.dot(oa_ref[...], wpa_ref[...], preferred_element_type=F32)
    merged = ma_ref[...] * y_a + gb_ref[...]
    out = xs_ref[...] + jnp.dot(merged.astype(BF16), wout_ref[...], preferred_element_type=F32)
    y = _rms(out, fg_ref[...])
    for i in range(n_seq):
        y_ref[i] = y[i * DEC_SEQ:(i + 1) * DEC_SEQ, :]


def _state_copies(hbm_ref, buf_ref, sem, blk, slot, to_hbm):
    n = BB_SAMPLE // STATE_DMAS
    copies = []
    for c in range(STATE_DMAS):
        in_hbm = hbm_ref.at[pl.ds(blk * BB_SAMPLE + c * n, n)]
        in_vmem = buf_ref.at[slot, pl.ds(c * n, n)]
        src, dst = (in_vmem, in_hbm) if to_hbm else (in_hbm, in_vmem)
        copies.append(pltpu.make_async_copy(src, dst, sem.at[slot, c]))
    return copies


def _decode_kernel(xp_ref, p_in_ref, s_in_hbm,
                   ng_ref, win_ref, lbl_ref, wpool_ref, ps_ref, wpb_ref,
                   hg_ref, wpa_ref, wout_ref, fg_ref,
                   s_out_hbm, y_ref, p_out_ref,
                   a_all, bv_all, qd_all, o_all, ga_all, ma_all, gb_all, xs_all,
                   z_ref, rw_ref, u_ref, yb_ref, oa_ref, s_in_buf, s_out_buf, in_sem, out_sem,
                   *, n_prep):
    s = pl.program_id(0)
    prows = PREP_SEQS * DEC_SEQ
    brows = BB_SAMPLE * DEC_SEQ
    n_state = n_prep * PREP_PARTS

    def load(blk, slot):
        return _state_copies(s_in_hbm, s_in_buf, in_sem, blk, slot, to_hbm=False)

    def store(blk, slot):
        return _state_copies(s_out_hbm, s_out_buf, out_sem, blk, slot, to_hbm=True)

    def block_rows(blk):
        start = blk * prows
        return pl.ds(start if isinstance(start, int) else pl.multiple_of(start, prows), prows)

    def prep(part, blk):
        r = block_rows(blk)
        _decode_prep(part, xp_ref, p_in_ref, ng_ref, win_ref, lbl_ref, wpool_ref, ps_ref, wpb_ref,
                     a_all.at[:, :, r, :], bv_all.at[:, :, r, :], qd_all.at[r, :],
                     o_all.at[r, :], ga_all.at[r, :], ma_all.at[r, :], gb_all.at[r, :],
                     p_out_ref, xs_all.at[r, :], z_ref, rw_ref, u_ref, yb_ref)

    def finish(blk):
        f = block_rows(blk)
        _decode_finish(o_all.at[f, :], ga_all.at[f, :], ma_all.at[f, :], gb_all.at[f, :],
                       xs_all.at[f, :], hg_ref, wpa_ref, wout_ref, fg_ref, y_ref, oa_ref,
                       n_seq=PREP_SEQS)

    i = s - 1
    blk = lax.div(i, PREP_PARTS)
    part_now = lax.rem(i, PREP_PARTS)
    slot = lax.rem(i, 2)

    @pl.when(s == 0)
    def _():
        for b in range(2):
            for c in load(b, b):
                c.start()
        for part in range(PREP_PARTS):
            prep(part, 0)

    @pl.when(s > 0)
    def _():
        @pl.when((i > 0) & (i + 1 < n_state))
        def _():
            for c in load(i + 1, 1 - slot):
                c.start()
        for c in load(i, slot):
            c.wait()

        @pl.when(i >= 2)
        def _():
            for c in store(i - 2, slot):
                c.wait()
        r = pl.ds(pl.multiple_of(i * brows, brows), brows)
        _decode_state_step(a_all.at[:, :, r, :], bv_all.at[:, :, r, :], qd_all.at[r, :],
                           s_in_buf.at[slot], s_out_buf.at[slot], o_all.at[r, :])
        for c in store(i, slot):
            c.start()

    for part in range(PREP_PARTS):
        @pl.when((part_now == part) & (blk + 1 < n_prep))
        def _():
            prep(part, blk + 1)

    @pl.when((part_now == 0) & (i > 0))
    def _():
        finish(blk - 1)

    @pl.when(i == n_state - 1)
    def _():
        finish(n_prep - 1)
        for c in store(i - 1, 1 - slot) + store(i, slot):
            c.wait()


def _const_spec(shape):
    n = len(shape)
    return pl.BlockSpec(shape, lambda *_: (0,) * n, pipeline_mode=pl.Buffered(1))


def kernel(x_prompt, x_sample, state_hgrn, state_pool, norm_g, w_in, lb_logits, hgrn_norm_g,
           w_proj_a, w_pool, pool_scale, w_proj_b, w_out, final_norm_g):
    batch, seq, _ = x_prompt.shape
    dec_batch, dec_seq, _ = x_sample.shape
    assert norm_g.shape[0] == 1 and lb_logits.shape[0] == 2, "single-layer decoder only"
    assert seq % TM_PROMPT == 0 and TM_PROMPT % SUB_TILE == 0 and dec_batch % BB_SAMPLE == 0
    assert dec_seq == DEC_SEQ and PAIR_ROWS == SUBLANES and BB_SAMPLE % 2 == 0
    assert dec_batch % PREP_SEQS == 0 and (PREP_SEQS * DEC_SEQ) % SUB_TILE == 0
    assert PREP_SEQS == PREP_PARTS * BB_SAMPLE and BB_SAMPLE % STATE_DMAS == 0
    assert PAST_LEN >= max(POOL_WINDOWS)

    n_groups = len(POOL_WINDOWS)
    tm = TM_PROMPT
    nt = seq // tm
    assert w_in.shape[1:] == (2 * tm, D_IN) and w_proj_a.shape[1] == tm, "weight staging layout"
    assert tm % (WIN_SLOTS * 2 * SUBLANES) == 0, "bf16-tile-aligned w_in staging blocks"
    hbm = pl.BlockSpec(memory_space=pl.ANY)
    bf16_shapes = [(D_MODEL, D_IN), (W_BRANCH, D_MODEL), (n_groups, G_B, G_B),
                   (W_BRANCH, D_MODEL), (D_MODEL, D_MODEL)]
    y_p, s_p, p_p, win_b, wpa_b, wpool_b, wpb_b, wout_b = pl.pallas_call(
        _prompt_kernel,
        grid=(batch, nt),
        in_specs=[pl.BlockSpec((1, tm, D_MODEL), lambda b, t: (b, t, 0)),
                  _const_spec((1, D_MODEL)), hbm, _const_spec((2, W_BRANCH)), _const_spec((1, DV)),
                  hbm, hbm, _const_spec((1, W_BRANCH)), hbm, hbm, _const_spec((1, D_MODEL))],
        out_specs=[
            pl.BlockSpec((1, tm, D_MODEL), lambda b, t: (b, t, 0)),
            pl.BlockSpec((1, N_HEADS, DK, DV), lambda b, t: (b, 0, 0, 0)),
            pl.BlockSpec((POOL_BUF, batch, W_BRANCH), lambda b, t: (0, 0, 0)),
        ] + [hbm] * len(bf16_shapes),
        out_shape=[
            jax.ShapeDtypeStruct((batch, seq, D_MODEL), F32),
            jax.ShapeDtypeStruct((batch, N_HEADS, DK, DV), F32),
            jax.ShapeDtypeStruct((POOL_BUF, batch, W_BRANCH), F32),
        ] + [jax.ShapeDtypeStruct(s, BF16) for s in bf16_shapes],
        scratch_shapes=[
            pltpu.VMEM((tm, D_IN), F32),
            pltpu.VMEM((N_HEADS, DV, DK), F32),
            pltpu.VMEM((tm + 2 * SUBLANES, W_BRANCH), F32),
            pltpu.VMEM((N_HEADS, tm, (SUB_TILE // CHUNK) * DK), BF16),
            pltpu.VMEM((N_HEADS, tm // CHUNK, DK, DV), BF16),
            pltpu.VMEM((tm, W_BRANCH), BF16),
            pltpu.VMEM((tm, W_BRANCH), BF16),
            pltpu.VMEM((tm, W_BRANCH), BF16),
            pltpu.VMEM((tm, W_BRANCH), BF16),
            pltpu.VMEM((tm, W_BRANCH), BF16),
            pltpu.VMEM((tm, W_BRANCH), BF16),
        ] + [pltpu.VMEM(s, BF16) for s in bf16_shapes] + [
            pltpu.SemaphoreType.DMA((5 + WIN_SLOTS,)),
            pltpu.SemaphoreType.DMA((len(bf16_shapes),)),
        ],
        compiler_params=pltpu.CompilerParams(
            dimension_semantics=("arbitrary", "arbitrary"),
            vmem_limit_bytes=VMEM_LIMIT_BYTES),
        name="hgrn2_pool_prompt",
    )(x_prompt, norm_g, w_in[0], lb_logits, hgrn_norm_g, w_proj_a[0],
      w_pool[0].reshape(n_groups * G_B, G_B), pool_scale, w_proj_b[0], w_out[0],
      final_norm_g.reshape(1, D_MODEL))
    weights = (norm_g, win_b, lb_logits, hgrn_norm_g, wpa_b, wpool_b, pool_scale, wpb_b, wout_b,
               final_norm_g.reshape(1, D_MODEL))

    rows = dec_batch * DEC_SEQ
    pool_in = jnp.transpose(state_pool[0], (1, 0, 2))
    ng, win_b, lbl, hg, wpa_b, wpool_b, ps, wpb_b, wout_b, fg = weights
    pseq = PREP_SEQS
    prows = pseq * DEC_SEQ
    n_prep = dec_batch // pseq
    bb = BB_SAMPLE

    def prep_blk(s):
        return jnp.minimum((s + PREP_PARTS - 1) // PREP_PARTS, n_prep - 1)

    def finish_blk(s):
        return jnp.minimum(jnp.maximum(s - 2, 0) // PREP_PARTS, n_prep - 1)

    s_s, y_s, pool_out = pl.pallas_call(
        functools.partial(_decode_kernel, n_prep=n_prep),
        grid=(1 + dec_batch // bb,),
        in_specs=[
            pl.BlockSpec((pseq, DEC_SEQ, D_MODEL), lambda s: (prep_blk(s), 0, 0)),
            pl.BlockSpec((POOL_BUF, pseq, W_BRANCH), lambda s: (0, prep_blk(s), 0),
                         pipeline_mode=pl.Buffered(1)),
            hbm,
            _const_spec(ng.shape), _const_spec(win_b.shape), _const_spec(lbl.shape),
            _const_spec(wpool_b.shape), _const_spec(ps.shape), _const_spec(wpb_b.shape),
            _const_spec(hg.shape), _const_spec(wpa_b.shape), _const_spec(wout_b.shape),
            _const_spec(fg.shape),
        ],
        out_specs=[
            hbm,
            pl.BlockSpec((pseq, DEC_SEQ, D_MODEL), lambda s: (finish_blk(s), 0, 0)),
            pl.BlockSpec((POOL_BUF, pseq, W_BRANCH), lambda s: (0, prep_blk(s), 0)),
        ],
        out_shape=[
            jax.ShapeDtypeStruct((dec_batch, N_HEADS, DK, DV), F32),
            jax.ShapeDtypeStruct(x_sample.shape, F32),
            jax.ShapeDtypeStruct((POOL_BUF, dec_batch, W_BRANCH), F32),
        ],
        scratch_shapes=[
            pltpu.VMEM((N_HEADS, 2, rows, DK), BF16),
            pltpu.VMEM((N_HEADS, 2, rows, DV), BF16),
            pltpu.VMEM((rows, W_BRANCH), BF16),
            pltpu.VMEM((rows, W_BRANCH), F32),
            pltpu.VMEM((rows, W_BRANCH), F32),
            pltpu.VMEM((rows, D_MODEL), F32),
            pltpu.VMEM((rows, D_MODEL), F32),
            pltpu.VMEM((rows, D_MODEL), F32),
            pltpu.VMEM((prows, D_IN), F32),
            pltpu.VMEM((len(POOL_WINDOWS), prows, G_B), F32),
            pltpu.VMEM((len(POOL_WINDOWS), prows, G_B), F32),
            pltpu.VMEM((prows, W_BRANCH), BF16),
            pltpu.VMEM((prows, W_BRANCH), BF16),
            pltpu.VMEM((2, bb, N_HEADS, DK, DV), F32),
            pltpu.VMEM((2, bb, N_HEADS, DK, DV), F32),
            pltpu.SemaphoreType.DMA((2, STATE_DMAS)),
            pltpu.SemaphoreType.DMA((2, STATE_DMAS)),
        ],
        compiler_params=pltpu.CompilerParams(
            dimension_semantics=("arbitrary",), vmem_limit_bytes=DECODE_VMEM_LIMIT_BYTES),
        name="hgrn2_pool_decode",
    )(x_sample, pool_in, state_hgrn[0], ng, win_b, lbl, wpool_b, ps, wpb_b,
      hg, wpa_b, wout_b, fg)

    p_p = jnp.transpose(p_p, (1, 0, 2))
    p_s = jnp.transpose(pool_out, (1, 0, 2))
    return (y_p, y_s, s_p[None], p_p[None], s_s[None], p_s[None])
```

```python
import functools

import jax
import jax.numpy as jnp
from jax import lax
from jax.experimental import pallas as pl
from jax.experimental.pallas import tpu as pltpu

F32 = jnp.float32
BF16 = jnp.bfloat16

D_MODEL = 1024
W_BRANCH = 512
N_HEADS = 4
DK = 128
DV = 128
CHUNK = 32
POOL_WINDOWS = (2, 4, 8, 16)
G_B = 128
POOL_BUF = 15
PAST_LEN = 16384
EPS = 1e-6
D_IN = 4 * W_BRANCH + 2 * W_BRANCH + 2 * D_MODEL
OFF_Q, OFF_F, OFF_I, OFF_GA = 0, 512, 1024, 1536
OFF_U, OFF_GB, OFF_MA, OFF_MB = 2048, 2560, 3072, 4096

SUBLANES = 8
VMEM_LIMIT_BYTES = 56 * 1024 * 1024
DECODE_VMEM_LIMIT_BYTES = 60 * 1024 * 1024

TM_PROMPT = 512
WIN_SLOTS = 4
SUB_TILE = 256
PREP_SEQS = 64
BB_SAMPLE = 16
STATE_DMAS = 4
DEC_SEQ = 4
PAIR_ROWS = 2 * DEC_SEQ
PREP_PARTS = 4


def _rms(x, g):
    ms = jnp.mean(x * x, axis=-1, keepdims=True)
    return x * lax.rsqrt(ms + EPS) * g


def _lower_bound(lbl_ref):
    l0 = lbl_ref[0:1, :]
    l1 = lbl_ref[1:2, :]
    m = jnp.maximum(l0, l1)
    e0 = jnp.exp(l0 - m)
    e1 = jnp.exp(l1 - m)
    return e0 / (e0 + e1)


def _in_proj(x, ng_ref, win_ref, z_ref, c0=0, c1=D_IN):
    h = _rms(x, ng_ref[...]).astype(BF16)
    z_ref[:, c0:c1] = jnp.dot(h, win_ref[:, c0:c1], preferred_element_type=F32)


def _hgrn_inputs(z_ref, lb, hh):
    c = hh * DK
    q = z_ref[:, OFF_Q + c:OFF_Q + c + DK]
    f = z_ref[:, OFF_F + c:OFF_F + c + DK]
    v = z_ref[:, OFF_I + c:OFF_I + c + DK]
    lbh = lb[:, c:c + DK]
    fg = lbh + (1.0 - lbh) * jax.nn.sigmoid(f)
    logf = jnp.log(fg)
    k = 1.0 - fg
    qf = jax.nn.silu(q) * (DK ** -0.5)
    return qf, k, v, logf


def _group_cumsum(x, group):
    row = lax.broadcasted_iota(jnp.int32, x.shape, 0) & (group - 1)
    s = 1
    while s < group:
        x = x + jnp.where(row >= s, pltpu.roll(x, s, 0), 0.0)
        s *= 2
    return x


def _hgrn_epilogue(o, z_ref, hg_ref, hh, oa_ref):
    c = hh * DK
    ga = z_ref[:, OFF_GA + c:OFF_GA + c + DK]
    on = _rms(o, hg_ref[...]) * jax.nn.silu(ga)
    oa_ref[:, c:c + DK] = on.astype(BF16)


def _pool_epilogue(pooled, z_ref, g, wpool_ref, ps_ref, yb_ref):
    c = g * G_B
    mixed = jnp.dot(pooled.astype(BF16), wpool_ref[g], preferred_element_type=F32)
    gb = z_ref[:, OFF_GB + c:OFF_GB + c + G_B]
    yb = mixed * ps_ref[:, c:c + G_B] * jax.nn.silu(gb)
    yb_ref[:, c:c + G_B] = yb.astype(BF16)


def _weight_copies(hbm_refs, stage_ref, sem):
    win_hbm, wpa_hbm, wpool_hbm, wpb_hbm, wout_hbm = hbm_refs
    rows = stage_ref.shape[0]
    small = [
        (wpa_hbm, stage_ref.at[:, 0:D_MODEL]),
        (wpb_hbm, stage_ref.at[:, D_MODEL:2 * D_MODEL]),
        (wout_hbm.at[0:rows, :], stage_ref.at[:, 2 * D_MODEL:3 * D_MODEL]),
        (wout_hbm.at[rows:2 * rows, :], stage_ref.at[:, 3 * D_MODEL:4 * D_MODEL]),
        (wpool_hbm, stage_ref.at[:, 4 * D_MODEL:4 * D_MODEL + G_B]),
    ]
    small = [pltpu.make_async_copy(src, dst, sem.at[i]) for i, (src, dst) in enumerate(small)]
    blk = rows // WIN_SLOTS

    def win(k):
        slot = k % WIN_SLOTS
        return pltpu.make_async_copy(win_hbm.at[k * blk:(k + 1) * blk, :],
                                     stage_ref.at[slot * blk:(slot + 1) * blk, :],
                                     sem.at[len(small) + slot])
    return small, win, blk


def _weight_exports(vmem_refs, out_refs, sem):
    return [pltpu.make_async_copy(src, dst, sem.at[i])
            for i, (src, dst) in enumerate(zip(vmem_refs, out_refs))]


def _prompt_kernel(x_ref, ng_ref, win_hbm, lbl_ref, hg_ref, wpa_hbm, wpool_hbm, ps_ref,
                   wpb_hbm, wout_hbm, fg_ref,
                   y_ref, s_out_ref, p_out_ref, win_out, wpa_out, wpool_out, wpb_out, wout_out,
                   z_ref, st_ref, ext_ref, kx_ref, sn_ref, qs_ref, ks_ref, qd_ref, vb_ref,
                   oa_ref, yb_ref, win_ref, wpa_ref, wpool_ref, wpb_ref, wout_ref,
                   load_sem, export_sem):
    tm = TM_PROMPT
    sub = SUB_TILE
    n_sub = tm // sub
    nc = tm // CHUNK
    ncs = sub // CHUNK
    t = pl.program_id(1)
    first = t == 0
    hdr = 2 * SUBLANES
    bf16_weights = (win_ref, wpa_ref, wpool_ref, wpb_ref, wout_ref)
    exports = _weight_exports(bf16_weights, (win_out, wpa_out, wpool_out, wpb_out, wout_out),
                              export_sem)

    @pl.when((pl.program_id(0) == 0) & first)
    def _():
        small, win, blk = _weight_copies((win_hbm, wpa_hbm, wpool_hbm, wpb_hbm, wout_hbm),
                                         z_ref, load_sem)
        for c in small:
            c.start()
        st_ref[...] = jnp.zeros_like(st_ref)
        ext_ref[...] = jnp.zeros_like(ext_ref)
        kx_ref[...] = jnp.zeros_like(kx_ref)
        for c in small:
            c.wait()
        wpa_ref[...] = z_ref[:, 0:D_MODEL].astype(BF16)
        wpb_ref[...] = z_ref[:, D_MODEL:2 * D_MODEL].astype(BF16)
        wout_ref[0:tm, :] = z_ref[:, 2 * D_MODEL:3 * D_MODEL].astype(BF16)
        wout_ref[tm:2 * tm, :] = z_ref[:, 3 * D_MODEL:4 * D_MODEL].astype(BF16)
        for g in range(len(POOL_WINDOWS)):
            wpool_ref[g] = z_ref[g * G_B:(g + 1) * G_B,
                                 4 * D_MODEL:4 * D_MODEL + G_B].astype(BF16)
        n_blk = D_MODEL // blk
        for k in range(WIN_SLOTS):
            win(k).start()
        for k in range(n_blk):
            slot = k % WIN_SLOTS
            win(k).wait()
            win_ref[k * blk:(k + 1) * blk, :] = z_ref[slot * blk:(slot + 1) * blk, :].astype(BF16)
            if k + WIN_SLOTS < n_blk:
                win(k + WIN_SLOTS).start()
        for c in exports:
            c.start()

    @pl.when((pl.program_id(0) == pl.num_programs(0) - 1) & (t == pl.num_programs(1) - 1))
    def _():
        for c in exports:
            c.wait()

    x = x_ref[0]
    _in_proj(x, ng_ref, win_ref, z_ref)
    lb = _lower_bound(lbl_ref)

    ri = lax.broadcasted_iota(jnp.int32, (sub, sub), 0)
    ci = lax.broadcasted_iota(jnp.int32, (sub, sub), 1)
    causal = ((ri // CHUNK) == (ci // CHUNK)) & (ci <= ri)

    heads = range(N_HEADS)
    units = [(hh, s) for hh in heads for s in range(n_sub)]

    def rows(s):
        return slice(s * sub, (s + 1) * sub)

    def cols(hh):
        return slice(hh * DK, (hh + 1) * DK)

    dec = []
    for hh in heads:
        qf, k, v, logf = _hgrn_inputs(z_ref, lb, hh)
        b = _group_cumsum(logf, CHUNK)
        b3 = b.reshape(nc, CHUNK, DK)
        ref = b3[:, CHUNK // 2:CHUNK // 2 + 1, :]
        bl = b3[:, CHUNK - 1:CHUNK, :]
        q3 = qf.reshape(nc, CHUNK, DK)
        k3 = k.reshape(nc, CHUNK, DK)
        qs_ref[:, cols(hh)] = (q3 * jnp.exp(b3 - ref)).reshape(tm, DK).astype(BF16)
        ks_ref[:, cols(hh)] = (k3 * jnp.exp(ref - b3)).reshape(tm, DK).astype(BF16)
        qd_ref[:, cols(hh)] = (q3 * jnp.exp(b3)).reshape(tm, DK).astype(BF16)
        vb_ref[:, cols(hh)] = v.astype(BF16)
        kd = (k3 * jnp.exp(bl - b3)).reshape(tm, DK).astype(BF16)
        dec.append(jnp.exp(bl))
        for c in range(nc):
            cc = c % ncs
            kx_ref[hh, c * CHUNK:(c + 1) * CHUNK, cc * DK:(cc + 1) * DK] = (
                kd[c * CHUNK:(c + 1) * CHUNK, :])

    ext_ref[0:hdr, :] = jnp.where(first, 0.0, ext_ref[tm:tm + hdr, :])
    ext_ref[hdr:hdr + tm, :] = z_ref[:, OFF_U:OFF_U + W_BRANCH]
    pos1 = t * tm + lax.broadcasted_iota(jnp.int32, (tm, 1), 0) + 1
    for g, w in enumerate(POOL_WINDOWS):
        c = g * G_B
        s = ext_ref[:, c:c + G_B]
        sh = 1
        while sh < w:
            s = s + pltpu.roll(s, sh, 0)
            sh *= 2
        inv = jnp.where(pos1 >= w, 1.0 / w, 1.0 / pos1.astype(F32))
        pooled = s[hdr:, :] * inv - z_ref[:, OFF_U + c:OFF_U + c + G_B]
        _pool_epilogue(pooled, z_ref, g, wpool_ref, ps_ref, yb_ref)
    for j in range(POOL_BUF):
        r = hdr + tm - POOL_BUF + j
        p_out_ref[j, pl.ds(pl.program_id(0), 1), :] = ext_ref[r:r + 1, :]

    sc = {(hh, s): lax.dot_general(qs_ref[rows(s), cols(hh)], ks_ref[rows(s), cols(hh)],
                                   (((1,), (1,)), ((), ())), preferred_element_type=F32)
          for hh, s in units}
    ut = {(hh, s): lax.dot_general(vb_ref[rows(s), cols(hh)], kx_ref[hh, rows(s), :],
                                   (((0,), (0,)), ((), ())), preferred_element_type=F32)
          for hh, s in units}
    o = {(hh, s): jnp.dot(jnp.where(causal, sc[hh, s], 0.0).astype(BF16),
                          vb_ref[rows(s), cols(hh)], preferred_element_type=F32)
         for hh, s in units}
    for hh in heads:
        st = jnp.where(first, 0.0, st_ref[hh])
        for c in range(nc):
            sn_ref[hh, c] = st.T.astype(BF16)
            st = dec[hh][c] * st + ut[hh, c // ncs][:, (c % ncs) * DK:(c % ncs + 1) * DK]
        st_ref[hh] = st
        s_out_ref[0, hh] = st.T
    wcol = D_MODEL // N_HEADS
    y_b = []
    for hh in heads:
        o_inter = [jnp.dot(qd_ref[c * CHUNK:(c + 1) * CHUNK, cols(hh)], sn_ref[hh, c],
                           preferred_element_type=F32) for c in range(nc)]
        y_b.append(jnp.dot(yb_ref[...], wpb_ref[:, hh * wcol:(hh + 1) * wcol],
                           preferred_element_type=F32))
        o_hh = jnp.concatenate([o[hh, s] for s in range(n_sub)], axis=0)
        _hgrn_epilogue(o_hh + jnp.concatenate(o_inter, axis=0), z_ref, hg_ref, hh, oa_ref)
    gated_b = jax.nn.sigmoid(z_ref[:, OFF_MB:OFF_MB + D_MODEL]) * jnp.concatenate(y_b, axis=1)

    y_a = jnp.dot(oa_ref[...], wpa_ref[...], preferred_element_type=F32)
    merged = jax.nn.sigmoid(z_ref[:, OFF_MA:OFF_MA + D_MODEL]) * y_a + gated_b
    out = x + jnp.dot(merged.astype(BF16), wout_ref[...], preferred_element_type=F32)
    y_ref[0] = _rms(out, fg_ref[...])


def _shift_rows(x, k):
    n = x.shape[0]
    return x if k % n == 0 else pltpu.roll(x, (-k) % n, 0)


def _group_bcast(x, j, group):
    t = lax.broadcasted_iota(jnp.int32, x.shape, 0) & (group - 1)
    out = _shift_rows(x, j - (group - 1))
    for tt in range(group - 2, -1, -1):
        out = jnp.where(t == tt, _shift_rows(x, j - tt), out)
    return out


def _decode_prep_hgrn(heads, z_ref, lbl_ref, a_ref, bv_ref, qd_ref, oi_ref, ga_ref):
    rows = z_ref.shape[0]
    lb = _lower_bound(lbl_ref)
    r = lax.broadcasted_iota(jnp.int32, (rows, DK), 0)
    t = r & (DEC_SEQ - 1)
    even = (r & (PAIR_ROWS - 1)) < DEC_SEQ
    blk = SUB_TILE
    ri = lax.broadcasted_iota(jnp.int32, (blk, blk), 0)
    ci = lax.broadcasted_iota(jnp.int32, (blk, blk), 1)
    causal = ((ri // DEC_SEQ) == (ci // DEC_SEQ)) & (ci <= ri)

    for hh in heads:
        cs = slice(hh * DK, (hh + 1) * DK)
        qf, k, v, logf = _hgrn_inputs(z_ref, lb, hh)
        b = _group_cumsum(logf, DEC_SEQ)
        ref = _group_bcast(b, DEC_SEQ // 2, DEC_SEQ)
        bl = _group_bcast(b, DEC_SEQ - 1, DEC_SEQ)
        qs = (qf * jnp.exp(b - ref)).astype(BF16)
        ks = (k * jnp.exp(ref - b)).astype(BF16)
        kd = k * jnp.exp(bl - b)
        vb = v.astype(BF16)
        qd_ref[:, cs] = (qf * jnp.exp(b)).astype(BF16)
        for s in range(rows // blk):
            rs = slice(s * blk, (s + 1) * blk)
            sc = lax.dot_general(qs[rs], ks[rs], (((1,), (1,)), ((), ())),
                                 preferred_element_type=F32)
            oi_ref[rs, cs] = jnp.dot(jnp.where(causal, sc, 0.0).astype(BF16), vb[rs],
                                     preferred_element_type=F32)
        dec = jnp.exp(bl)
        d1 = dec.astype(BF16).astype(F32)
        d2 = (dec - d1).astype(BF16).astype(F32)
        d3 = (dec - d1 - d2).astype(BF16).astype(F32)
        tail = jnp.where(t == 0, d1, jnp.where(t == 1, d2, jnp.where(t == 2, d3, 0.0)))
        a_ref[hh, 0] = jnp.where(even, kd, _shift_rows(tail, -DEC_SEQ)).astype(BF16)
        a_ref[hh, 1] = jnp.where(even, _shift_rows(tail, DEC_SEQ), kd).astype(BF16)
        zero = jnp.zeros_like(vb)
        bv_ref[hh, 0] = jnp.where(even, vb, zero)
        bv_ref[hh, 1] = jnp.where(even, zero, vb)
        ga_ref[:, cs] = jax.nn.silu(z_ref[:, OFF_GA + hh * DK:OFF_GA + (hh + 1) * DK])


def _decode_prep_history(p_in_ref, rw_ref, p_out_ref):
    n_seq = p_in_ref.shape[1]
    for g, w in enumerate(POOL_WINDOWS):
        gc = slice(g * G_B, (g + 1) * G_B)
        acc = None
        suffix = {}
        for j in range(POOL_BUF - 1, -1, -1):
            e = p_in_ref[j, :, gc]
            acc = e if acc is None else acc + e
            suffix[j] = acc
        for tt in range(DEC_SEQ):
            j = POOL_BUF + 1 - w + tt
            rw_ref[g, pl.ds(tt, n_seq, stride=DEC_SEQ), :] = (
                suffix[j] if j < POOL_BUF else jnp.zeros((n_seq, G_B), F32))
    for j in range(POOL_BUF - DEC_SEQ):
        p_out_ref[j] = p_in_ref[j + DEC_SEQ]


def _decode_prep_pool(z_ref, wpool_ref, ps_ref, wpb_ref, ma_ref, gb_ref, p_out_ref,
                      rw_ref, u_ref, yb_ref):
    rows = z_ref.shape[0]
    n_seq = rows // DEC_SEQ
    t = lax.broadcasted_iota(jnp.int32, (rows, G_B), 0) & (DEC_SEQ - 1)
    for g, w in enumerate(POOL_WINDOWS):
        u = z_ref[:, OFF_U + g * G_B:OFF_U + (g + 1) * G_B]
        u_ref[g] = u
        cu = u
        sh = 1
        while sh < min(w, DEC_SEQ):
            cu = cu + jnp.where(t >= sh, _shift_rows(cu, -sh), 0.0)
            sh *= 2
        pooled = (rw_ref[g] + cu) * (1.0 / w) - u
        _pool_epilogue(pooled, z_ref, g, wpool_ref, ps_ref, yb_ref)
    gb_ref[...] = jax.nn.sigmoid(z_ref[:, OFF_MB:OFF_MB + D_MODEL]) * jnp.dot(
        yb_ref[...], wpb_ref[...], preferred_element_type=F32)
    ma_ref[...] = jax.nn.sigmoid(z_ref[:, OFF_MA:OFF_MA + D_MODEL])
    for tt in range(DEC_SEQ):
        for g in range(len(POOL_WINDOWS)):
            p_out_ref[POOL_BUF - DEC_SEQ + tt, :, g * G_B:(g + 1) * G_B] = (
                u_ref[g, pl.ds(tt, n_seq, stride=DEC_SEQ), :])


def _decode_prep(part, x_ref, p_in_ref, ng_ref, win_ref, lbl_ref, wpool_ref, ps_ref, wpb_ref,
                 a_ref, bv_ref, qd_ref, oi_ref, ga_ref, ma_ref, gb_ref, p_out_ref,
                 xs_ref, z_ref, rw_ref, u_ref, yb_ref):
    hgrn = functools.partial(_decode_prep_hgrn, z_ref=z_ref, lbl_ref=lbl_ref, a_ref=a_ref,
                             bv_ref=bv_ref, qd_ref=qd_ref, oi_ref=oi_ref, ga_ref=ga_ref)
    if part == 0:
        for i in range(x_ref.shape[0]):
            xs_ref[i * DEC_SEQ:(i + 1) * DEC_SEQ, :] = x_ref[i]
        _in_proj(xs_ref[...], ng_ref, win_ref, z_ref, 0, OFF_U)
        _decode_prep_history(p_in_ref, rw_ref, p_out_ref)
    elif part == 1:
        _in_proj(xs_ref[...], ng_ref, win_ref, z_ref, OFF_U, D_IN)
    elif part == 2:
        hgrn(range(0, N_HEADS - 1))
    else:
        hgrn(range(N_HEADS - 1, N_HEADS))
        _decode_prep_pool(z_ref, wpool_ref, ps_ref, wpb_ref, ma_ref, gb_ref, p_out_ref,
                          rw_ref, u_ref, yb_ref)


def _decode_state_step(a_ref, bv_ref, qd_ref, s_in_ref, s_out_ref, o_ref):
    n_pairs = BB_SAMPLE // 2
    r8 = lax.broadcasted_iota(jnp.int32, (PAIR_ROWS, DV), 0)
    even = r8 < DEC_SEQ
    ones = (jnp.where((r8 >= DEC_SEQ) & (r8 < DEC_SEQ + 3), 1.0, 0.0).astype(BF16),
            jnp.where(r8 < 3, 1.0, 0.0).astype(BF16))
    for hh in range(N_HEADS):
        for lp in range(n_pairs):
            rs = slice(lp * PAIR_ROWS, (lp + 1) * PAIR_ROWS)
            for e in range(2):
                rhs = jnp.concatenate([bv_ref[hh, e, rs, :], ones[e]], axis=1)
                upd = lax.dot_general(a_ref[hh, e, rs, :], rhs, (((0,), (0,)), ((), ())),
                                      preferred_element_type=F32)
                s_out_ref[2 * lp + e, hh] = (upd[:, DV:2 * DV] * s_in_ref[2 * lp + e, hh]
                                             + upd[:, 0:DV])
    for hh in range(N_HEADS):
        cs = slice(hh * DK, (hh + 1) * DK)
        for lp in range(n_pairs):
            rs = slice(lp * PAIR_ROWS, (lp + 1) * PAIR_ROWS)
            q8 = qd_ref[rs, cs]
            o0 = jnp.dot(q8, s_in_ref[2 * lp, hh].astype(BF16), preferred_element_type=F32)
            o1 = jnp.dot(q8, s_in_ref[2 * lp + 1, hh].astype(BF16), preferred_element_type=F32)
            o_ref[rs, cs] = o_ref[rs, cs] + jnp.where(even, o0, o1)


def _decode_finish(o_ref, ga_ref, ma_ref, gb_ref, xs_ref, hg_ref, wpa_ref, wout_ref,
                   fg_ref, y_ref, oa_ref, *, n_seq):
    for hh in range(N_HEADS):
        cs = slice(hh * DK, (hh + 1) * DK)
        oa_ref[:, cs] = (_rms(o_ref[:, cs], hg_ref[...]) * ga_ref[:, cs]).astype(BF16)
    y_a = jnp.dot(oa_ref[...], wpa_ref[...], preferred_element_type=F32)
    merged = ma_ref[...] * y_a + gb_ref[...]
    out = xs_ref[...] + jnp.dot(merged.astype(BF16), wout_ref[...], preferred_element_type=F32)
    y = _rms(out, fg_ref[...])
    for i in range(n_seq):
        y_ref[i] = y[i * DEC_SEQ:(i + 1) * DEC_SEQ, :]


def _state_copies(hbm_ref, buf_ref, sem, blk, slot, to_hbm):
    n = BB_SAMPLE // STATE_DMAS
    copies = []
    for c in range(STATE_DMAS):
        in_hbm = hbm_ref.at[pl.ds(blk * BB_SAMPLE + c * n, n)]
        in_vmem = buf_ref.at[slot, pl.ds(c * n, n)]
        src, dst = (in_vmem, in_hbm) if to_hbm else (in_hbm, in_vmem)
        copies.append(pltpu.make_async_copy(src, dst, sem.at[slot, c]))
    return copies


def _decode_kernel(xp_ref, p_in_ref, s_in_hbm,
                   ng_ref, win_ref, lbl_ref, wpool_ref, ps_ref, wpb_ref,
                   hg_ref, wpa_ref, wout_ref, fg_ref,
                   s_out_hbm, y_ref, p_out_ref,
                   a_all, bv_all, qd_all, o_all, ga_all, ma_all, gb_all, xs_all,
                   z_ref, rw_ref, u_ref, yb_ref, oa_ref, s_in_buf, s_out_buf, in_sem, out_sem,
                   *, n_prep):
    s = pl.program_id(0)
    prows = PREP_SEQS * DEC_SEQ
    brows = BB_SAMPLE * DEC_SEQ
    n_state = n_prep * PREP_PARTS

    def load(blk, slot):
        return _state_copies(s_in_hbm, s_in_buf, in_sem, blk, slot, to_hbm=False)

    def store(blk, slot):
        return _state_copies(s_out_hbm, s_out_buf, out_sem, blk, slot, to_hbm=True)

    def block_rows(blk):
        start = blk * prows
        return pl.ds(start if isinstance(start, int) else pl.multiple_of(start, prows), prows)

    def prep(part, blk):
        r = block_rows(blk)
        _decode_prep(part, xp_ref, p_in_ref, ng_ref, win_ref, lbl_ref, wpool_ref, ps_ref, wpb_ref,
                     a_all.at[:, :, r, :], bv_all.at[:, :, r, :], qd_all.at[r, :],
                     o_all.at[r, :], ga_all.at[r, :], ma_all.at[r, :], gb_all.at[r, :],
                     p_out_ref, xs_all.at[r, :], z_ref, rw_ref, u_ref, yb_ref)

    def finish(blk):
        f = block_rows(blk)
        _decode_finish(o_all.at[f, :], ga_all.at[f, :], ma_all.at[f, :], gb_all.at[f, :],
                       xs_all.at[f, :], hg_ref, wpa_ref, wout_ref, fg_ref, y_ref, oa_ref,
                       n_seq=PREP_SEQS)

    i = s - 1
    blk = lax.div(i, PREP_PARTS)
    part_now = lax.rem(i, PREP_PARTS)
    slot = lax.rem(i, 2)

    @pl.when(s == 0)
    def _():
        for b in range(2):
            for c in load(b, b):
                c.start()
        for part in range(PREP_PARTS):
            prep(part, 0)

    @pl.when(s > 0)
    def _():
        @pl.when((i > 0) & (i + 1 < n_state))
        def _():
            for c in load(i + 1, 1 - slot):
                c.start()
        for c in load(i, slot):
            c.wait()

        @pl.when(i >= 2)
        def _():
            for c in store(i - 2, slot):
                c.wait()
        r = pl.ds(pl.multiple_of(i * brows, brows), brows)
        _decode_state_step(a_all.at[:, :, r, :], bv_all.at[:, :, r, :], qd_all.at[r, :],
                           s_in_buf.at[slot], s_out_buf.at[slot], o_all.at[r, :])
        for c in store(i, slot):
            c.start()

    for part in range(PREP_PARTS):
        @pl.when((part_now == part) & (blk + 1 < n_prep))
        def _():
            prep(part, blk + 1)

    @pl.when((part_now == 0) & (i > 0))
    def _():
        finish(blk - 1)

    @pl.when(i == n_state - 1)
    def _():
        finish(n_prep - 1)
        for c in store(i - 1, 1 - slot) + store(i, slot):
            c.wait()


def _const_spec(shape):
    n = len(shape)
    return pl.BlockSpec(shape, lambda *_: (0,) * n, pipeline_mode=pl.Buffered(1))


def kernel(x_prompt, x_sample, state_hgrn, state_pool, norm_g, w_in, lb_logits, hgrn_norm_g,
           w_proj_a, w_pool, pool_scale, w_proj_b, w_out, final_norm_g):
    batch, seq, _ = x_prompt.shape
    dec_batch, dec_seq, _ = x_sample.shape
    assert norm_g.shape[0] == 1 and lb_logits.shape[0] == 2, "single-layer decoder only"
    assert seq % TM_PROMPT == 0 and TM_PROMPT % SUB_TILE == 0 and dec_batch % BB_SAMPLE == 0
    assert dec_seq == DEC_SEQ and PAIR_ROWS == SUBLANES and BB_SAMPLE % 2 == 0
    assert dec_batch % PREP_SEQS == 0 and (PREP_SEQS * DEC_SEQ) % SUB_TILE == 0
    assert PREP_SEQS == PREP_PARTS * BB_SAMPLE and BB_SAMPLE % STATE_DMAS == 0
    assert PAST_LEN >= max(POOL_WINDOWS)

    n_groups = len(POOL_WINDOWS)
    tm = TM_PROMPT
    nt = seq // tm
    assert w_in.shape[1:] == (2 * tm, D_IN) and w_proj_a.shape[1] == tm, "weight staging layout"
    assert tm % (WIN_SLOTS * 2 * SUBLANES) == 0, "bf16-tile-aligned w_in staging blocks"
    hbm = pl.BlockSpec(memory_space=pl.ANY)
    bf16_shapes = [(D_MODEL, D_IN), (W_BRANCH, D_MODEL), (n_groups, G_B, G_B),
                   (W_BRANCH, D_MODEL), (D_MODEL, D_MODEL)]
    y_p, s_p, p_p, win_b, wpa_b, wpool_b, wpb_b, wout_b = pl.pallas_call(
        _prompt_kernel,
        grid=(batch, nt),
        in_specs=[pl.BlockSpec((1, tm, D_MODEL), lambda b, t: (b, t, 0)),
                  _const_spec((1, D_MODEL)), hbm, _const_spec((2, W_BRANCH)), _const_spec((1, DV)),
                  hbm, hbm, _const_spec((1, W_BRANCH)), hbm, hbm, _const_spec((1, D_MODEL))],
        out_specs=[
            pl.BlockSpec((1, tm, D_MODEL), lambda b, t: (b, t, 0)),
            pl.BlockSpec((1, N_HEADS, DK, DV), lambda b, t: (b, 0, 0, 0)),
            pl.BlockSpec((POOL_BUF, batch, W_BRANCH), lambda b, t: (0, 0, 0)),
        ] + [hbm] * len(bf16_shapes),
        out_shape=[
            jax.ShapeDtypeStruct((batch, seq, D_MODEL), F32),
            jax.ShapeDtypeStruct((batch, N_HEADS, DK, DV), F32),
            jax.ShapeDtypeStruct((POOL_BUF, batch, W_BRANCH), F32),
        ] + [jax.ShapeDtypeStruct(s, BF16) for s in bf16_shapes],
        scratch_shapes=[
            pltpu.VMEM((tm, D_IN), F32),
            pltpu.VMEM((N_HEADS, DV, DK), F32),
            pltpu.VMEM((tm + 2 * SUBLANES, W_BRANCH), F32),
            pltpu.VMEM((N_HEADS, tm, (SUB_TILE // CHUNK) * DK), BF16),
            pltpu.VMEM((N_HEADS, tm // CHUNK, DK, DV), BF16),
            pltpu.VMEM((tm, W_BRANCH), BF16),
            pltpu.VMEM((tm, W_BRANCH), BF16),
            pltpu.VMEM((tm, W_BRANCH), BF16),
            pltpu.VMEM((tm, W_BRANCH), BF16),
            pltpu.VMEM((tm, W_BRANCH), BF16),
            pltpu.VMEM((tm, W_BRANCH), BF16),
        ] + [pltpu.VMEM(s, BF16) for s in bf16_shapes] + [
            pltpu.SemaphoreType.DMA((5 + WIN_SLOTS,)),
            pltpu.SemaphoreType.DMA((len(bf16_shapes),)),
        ],
        compiler_params=pltpu.CompilerParams(
            dimension_semantics=("arbitrary", "arbitrary"),
            vmem_limit_bytes=VMEM_LIMIT_BYTES),
        name="hgrn2_pool_prompt",
    )(x_prompt, norm_g, w_in[0], lb_logits, hgrn_norm_g, w_proj_a[0],
      w_pool[0].reshape(n_groups * G_B, G_B), pool_scale, w_proj_b[0], w_out[0],
      final_norm_g.reshape(1, D_MODEL))
    weights = (norm_g, win_b, lb_logits, hgrn_norm_g, wpa_b, wpool_b, pool_scale, wpb_b, wout_b,
               final_norm_g.reshape(1, D_MODEL))

    rows = dec_batch * DEC_SEQ
    pool_in = jnp.transpose(state_pool[0], (1, 0, 2))
    ng, win_b, lbl, hg, wpa_b, wpool_b, ps, wpb_b, wout_b, fg = weights
    pseq = PREP_SEQS
    prows = pseq * DEC_SEQ
    n_prep = dec_batch // pseq
    bb = BB_SAMPLE

    def prep_blk(s):
        return jnp.minimum((s + PREP_PARTS - 1) // PREP_PARTS, n_prep - 1)

    def finish_blk(s):
        return jnp.minimum(jnp.maximum(s - 2, 0) // PREP_PARTS, n_prep - 1)

    s_s, y_s, pool_out = pl.pallas_call(
        functools.partial(_decode_kernel, n_prep=n_prep),
        grid=(1 + dec_batch // bb,),
        in_specs=[
            pl.BlockSpec((pseq, DEC_SEQ, D_MODEL), lambda s: (prep_blk(s), 0, 0)),
            pl.BlockSpec((POOL_BUF, pseq, W_BRANCH), lambda s: (0, prep_blk(s), 0),
                         pipeline_mode=pl.Buffered(1)),
            hbm,
            _const_spec(ng.shape), _const_spec(win_b.shape), _const_spec(lbl.shape),
            _const_spec(wpool_b.shape), _const_spec(ps.shape), _const_spec(wpb_b.shape),
            _const_spec(hg.shape), _const_spec(wpa_b.shape), _const_spec(wout_b.shape),
            _const_spec(fg.shape),
        ],
        out_specs=[
            hbm,
            pl.BlockSpec((pseq, DEC_SEQ, D_MODEL), lambda s: (finish_blk(s), 0, 0)),
            pl.BlockSpec((POOL_BUF, pseq, W_BRANCH), lambda s: (0, prep_blk(s), 0)),
        ],
        out_shape=[
            jax.ShapeDtypeStruct((dec_batch, N_HEADS, DK, DV), F32),
            jax.ShapeDtypeStruct(x_sample.shape, F32),
            jax.ShapeDtypeStruct((POOL_BUF, dec_batch, W_BRANCH), F32),
        ],
        scratch_shapes=[
            pltpu.VMEM((N_HEADS, 2, rows, DK), BF16),
            pltpu.VMEM((N_HEADS, 2, rows, DV), BF16),
            pltpu.VMEM((rows, W_BRANCH), BF16),
            pltpu.VMEM((rows, W_BRANCH), F32),
            pltpu.VMEM((rows, W_BRANCH), F32),
            pltpu.VMEM((rows, D_MODEL), F32),
            pltpu.VMEM((rows, D_MODEL), F32),
            pltpu.VMEM((rows, D_MODEL), F32),
            pltpu.VMEM((prows, D_IN), F32),
            pltpu.VMEM((len(POOL_WINDOWS), prows, G_B), F32),
            pltpu.VMEM((len(POOL_WINDOWS), prows, G_B), F32),
            pltpu.VMEM((prows, W_BRANCH), BF16),
            pltpu.VMEM((prows, W_BRANCH), BF16),
            pltpu.VMEM((2, bb, N_HEADS, DK, DV), F32),
            pltpu.VMEM((2, bb, N_HEADS, DK, DV), F32),
            pltpu.SemaphoreType.DMA((2, STATE_DMAS)),
            pltpu.SemaphoreType.DMA((2, STATE_DMAS)),
        ],
        compiler_params=pltpu.CompilerParams(
            dimension_semantics=("arbitrary",), vmem_limit_bytes=DECODE_VMEM_LIMIT_BYTES),
        name="hgrn2_pool_decode",
    )(x_sample, pool_in, state_hgrn[0], ng, win_b, lbl, wpool_b, ps, wpb_b,
      hg, wpa_b, wout_b, fg)

    p_p = jnp.transpose(p_p, (1, 0, 2))
    p_s = jnp.transpose(pool_out, (1, 0, 2))
    return (y_p, y_s, s_p[None], p_p[None], s_s[None], p_s[None])
```

```python
import functools

import jax
import jax.numpy as jnp
from jax import lax
from jax.experimental import pallas as pl
from jax.experimental.pallas import tpu as pltpu

F32 = jnp.float32
BF16 = jnp.bfloat16

D_MODEL = 1024
W_BRANCH = 512
N_HEADS = 4
DK = 128
DV = 128
CHUNK = 32
POOL_WINDOWS = (2, 4, 8, 16)
G_B = 128
POOL_BUF = 15
PAST_LEN = 16384
EPS = 1e-6
D_IN = 4 * W_BRANCH + 2 * W_BRANCH + 2 * D_MODEL
OFF_Q, OFF_F, OFF_I, OFF_GA = 0, 512, 1024, 1536
OFF_U, OFF_GB, OFF_MA, OFF_MB = 2048, 2560, 3072, 4096

SUBLANES = 8
VMEM_LIMIT_BYTES = 56 * 1024 * 1024
DECODE_VMEM_LIMIT_BYTES = 60 * 1024 * 1024

TM_PROMPT = 512
WIN_SLOTS = 4
SUB_TILE = 256
PREP_SEQS = 64
BB_SAMPLE = 16
STATE_DMAS = 2
DEC_SEQ = 4
PAIR_ROWS = 2 * DEC_SEQ
PREP_PARTS = 4


def _rms(x, g):
    ms = jnp.mean(x * x, axis=-1, keepdims=True)
    return x * lax.rsqrt(ms + EPS) * g


def _lower_bound(lbl_ref):
    l0 = lbl_ref[0:1, :]
    l1 = lbl_ref[1:2, :]
    m = jnp.maximum(l0, l1)
    e0 = jnp.exp(l0 - m)
    e1 = jnp.exp(l1 - m)
    return e0 / (e0 + e1)


def _in_proj(x, ng_ref, win_ref, z_ref, c0=0, c1=D_IN):
    h = _rms(x, ng_ref[...]).astype(BF16)
    z_ref[:, c0:c1] = jnp.dot(h, win_ref[:, c0:c1], preferred_element_type=F32)


def _hgrn_inputs(z_ref, lb, hh):
    c = hh * DK
    q = z_ref[:, OFF_Q + c:OFF_Q + c + DK]
    f = z_ref[:, OFF_F + c:OFF_F + c + DK]
    v = z_ref[:, OFF_I + c:OFF_I + c + DK]
    lbh = lb[:, c:c + DK]
    fg = lbh + (1.0 - lbh) * jax.nn.sigmoid(f)
    logf = jnp.log(fg)
    k = 1.0 - fg
    qf = jax.nn.silu(q) * (DK ** -0.5)
    return qf, k, v, logf


def _group_cumsum(x, group):
    row = lax.broadcasted_iota(jnp.int32, x.shape, 0) & (group - 1)
    s = 1
    while s < group:
        x = x + jnp.where(row >= s, pltpu.roll(x, s, 0), 0.0)
        s *= 2
    return x


def _hgrn_epilogue(o, z_ref, hg_ref, hh, oa_ref):
    c = hh * DK
    ga = z_ref[:, OFF_GA + c:OFF_GA + c + DK]
    on = _rms(o, hg_ref[...]) * jax.nn.silu(ga)
    oa_ref[:, c:c + DK] = on.astype(BF16)


def _pool_epilogue(pooled, z_ref, g, wpool_ref, ps_ref, yb_ref):
    c = g * G_B
    mixed = jnp.dot(pooled.astype(BF16), wpool_ref[g], preferred_element_type=F32)
    gb = z_ref[:, OFF_GB + c:OFF_GB + c + G_B]
    yb = mixed * ps_ref[:, c:c + G_B] * jax.nn.silu(gb)
    yb_ref[:, c:c + G_B] = yb.astype(BF16)


def _weight_copies(hbm_refs, stage_ref, sem):
    win_hbm, wpa_hbm, wpool_hbm, wpb_hbm, wout_hbm = hbm_refs
    rows = stage_ref.shape[0]
    small = [
        (wpa_hbm, stage_ref.at[:, 0:D_MODEL]),
        (wpb_hbm, stage_ref.at[:, D_MODEL:2 * D_MODEL]),
        (wout_hbm.at[0:rows, :], stage_ref.at[:, 2 * D_MODEL:3 * D_MODEL]),
        (wout_hbm.at[rows:2 * rows, :], stage_ref.at[:, 3 * D_MODEL:4 * D_MODEL]),
        (wpool_hbm, stage_ref.at[:, 4 * D_MODEL:4 * D_MODEL + G_B]),
    ]
    small = [pltpu.make_async_copy(src, dst, sem.at[i]) for i, (src, dst) in enumerate(small)]
    blk = rows // WIN_SLOTS

    def win(k):
        slot = k % WIN_SLOTS
        return pltpu.make_async_copy(win_hbm.at[k * blk:(k + 1) * blk, :],
                                     stage_ref.at[slot * blk:(slot + 1) * blk, :],
                                     sem.at[len(small) + slot])
    return small, win, blk


def _weight_exports(vmem_refs, out_refs, sem):
    return [pltpu.make_async_copy(src, dst, sem.at[i])
            for i, (src, dst) in enumerate(zip(vmem_refs, out_refs))]


def _prompt_kernel(x_ref, ng_ref, win_hbm, lbl_ref, hg_ref, wpa_hbm, wpool_hbm, ps_ref,
                   wpb_hbm, wout_hbm, fg_ref,
                   y_ref, s_out_ref, p_out_ref, win_out, wpa_out, wpool_out, wpb_out, wout_out,
                   z_ref, st_ref, ext_ref, kx_ref, sn_ref, qs_ref, ks_ref, qd_ref, vb_ref,
                   oa_ref, yb_ref, win_ref, wpa_ref, wpool_ref, wpb_ref, wout_ref,
                   load_sem, export_sem):
    tm = TM_PROMPT
    sub = SUB_TILE
    n_sub = tm // sub
    nc = tm // CHUNK
    ncs = sub // CHUNK
    t = pl.program_id(1)
    first = t == 0
    hdr = 2 * SUBLANES
    bf16_weights = (win_ref, wpa_ref, wpool_ref, wpb_ref, wout_ref)
    exports = _weight_exports(bf16_weights, (win_out, wpa_out, wpool_out, wpb_out, wout_out),
                              export_sem)

    @pl.when((pl.program_id(0) == 0) & first)
    def _():
        small, win, blk = _weight_copies((win_hbm, wpa_hbm, wpool_hbm, wpb_hbm, wout_hbm),
                                         z_ref, load_sem)
        for c in small:
            c.start()
        st_ref[...] = jnp.zeros_like(st_ref)
        ext_ref[...] = jnp.zeros_like(ext_ref)
        kx_ref[...] = jnp.zeros_like(kx_ref)
        for c in small:
            c.wait()
        wpa_ref[...] = z_ref[:, 0:D_MODEL].astype(BF16)
        wpb_ref[...] = z_ref[:, D_MODEL:2 * D_MODEL].astype(BF16)
        wout_ref[0:tm, :] = z_ref[:, 2 * D_MODEL:3 * D_MODEL].astype(BF16)
        wout_ref[tm:2 * tm, :] = z_ref[:, 3 * D_MODEL:4 * D_MODEL].astype(BF16)
        for g in range(len(POOL_WINDOWS)):
            wpool_ref[g] = z_ref[g * G_B:(g + 1) * G_B,
                                 4 * D_MODEL:4 * D_MODEL + G_B].astype(BF16)
        n_blk = D_MODEL // blk
        for k in range(WIN_SLOTS):
            win(k).start()
        for k in range(n_blk):
            slot = k % WIN_SLOTS
            win(k).wait()
            win_ref[k * blk:(k + 1) * blk, :] = z_ref[slot * blk:(slot + 1) * blk, :].astype(BF16)
            if k + WIN_SLOTS < n_blk:
                win(k + WIN_SLOTS).start()
        for c in exports:
            c.start()

    @pl.when((pl.program_id(0) == pl.num_programs(0) - 1) & (t == pl.num_programs(1) - 1))
    def _():
        for c in exports:
            c.wait()

    x = x_ref[0]
    _in_proj(x, ng_ref, win_ref, z_ref)
    lb = _lower_bound(lbl_ref)

    ri = lax.broadcasted_iota(jnp.int32, (sub, sub), 0)
    ci = lax.broadcasted_iota(jnp.int32, (sub, sub), 1)
    causal = ((ri // CHUNK) == (ci // CHUNK)) & (ci <= ri)

    heads = range(N_HEADS)
    units = [(hh, s) for hh in heads for s in range(n_sub)]

    def rows(s):
        return slice(s * sub, (s + 1) * sub)

    def cols(hh):
        return slice(hh * DK, (hh + 1) * DK)

    dec = []
    for hh in heads:
        qf, k, v, logf = _hgrn_inputs(z_ref, lb, hh)
        b = _group_cumsum(logf, CHUNK)
        b3 = b.reshape(nc, CHUNK, DK)
        ref = b3[:, CHUNK // 2:CHUNK // 2 + 1, :]
        bl = b3[:, CHUNK - 1:CHUNK, :]
        q3 = qf.reshape(nc, CHUNK, DK)
        k3 = k.reshape(nc, CHUNK, DK)
        qs_ref[:, cols(hh)] = (q3 * jnp.exp(b3 - ref)).reshape(tm, DK).astype(BF16)
        ks_ref[:, cols(hh)] = (k3 * jnp.exp(ref - b3)).reshape(tm, DK).astype(BF16)
        qd_ref[:, cols(hh)] = (q3 * jnp.exp(b3)).reshape(tm, DK).astype(BF16)
        vb_ref[:, cols(hh)] = v.astype(BF16)
        kd = (k3 * jnp.exp(bl - b3)).reshape(tm, DK).astype(BF16)
        dec.append(jnp.exp(bl))
        for c in range(nc):
            cc = c % ncs
            kx_ref[hh, c * CHUNK:(c + 1) * CHUNK, cc * DK:(cc + 1) * DK] = (
                kd[c * CHUNK:(c + 1) * CHUNK, :])

    ext_ref[0:hdr, :] = jnp.where(first, 0.0, ext_ref[tm:tm + hdr, :])
    ext_ref[hdr:hdr + tm, :] = z_ref[:, OFF_U:OFF_U + W_BRANCH]
    pos1 = t * tm + lax.broadcasted_iota(jnp.int32, (tm, 1), 0) + 1
    for g, w in enumerate(POOL_WINDOWS):
        c = g * G_B
        s = ext_ref[:, c:c + G_B]
        sh = 1
        while sh < w:
            s = s + pltpu.roll(s, sh, 0)
            sh *= 2
        inv = jnp.where(pos1 >= w, 1.0 / w, 1.0 / pos1.astype(F32))
        pooled = s[hdr:, :] * inv - z_ref[:, OFF_U + c:OFF_U + c + G_B]
        _pool_epilogue(pooled, z_ref, g, wpool_ref, ps_ref, yb_ref)
    for j in range(POOL_BUF):
        r = hdr + tm - POOL_BUF + j
        p_out_ref[j, pl.ds(pl.program_id(0), 1), :] = ext_ref[r:r + 1, :]

    sc = {(hh, s): lax.dot_general(qs_ref[rows(s), cols(hh)], ks_ref[rows(s), cols(hh)],
                                   (((1,), (1,)), ((), ())), preferred_element_type=F32)
          for hh, s in units}
    ut = {(hh, s): lax.dot_general(vb_ref[rows(s), cols(hh)], kx_ref[hh, rows(s), :],
                                   (((0,), (0,)), ((), ())), preferred_element_type=F32)
          for hh, s in units}
    o = {(hh, s): jnp.dot(jnp.where(causal, sc[hh, s], 0.0).astype(BF16),
                          vb_ref[rows(s), cols(hh)], preferred_element_type=F32)
         for hh, s in units}
    for hh in heads:
        st = jnp.where(first, 0.0, st_ref[hh])
        for c in range(nc):
            sn_ref[hh, c] = st.T.astype(BF16)
            st = dec[hh][c] * st + ut[hh, c // ncs][:, (c % ncs) * DK:(c % ncs + 1) * DK]
        st_ref[hh] = st
        s_out_ref[0, hh] = st.T
    wcol = D_MODEL // N_HEADS
    y_b = []
    for hh in heads:
        o_inter = [jnp.dot(qd_ref[c * CHUNK:(c + 1) * CHUNK, cols(hh)], sn_ref[hh, c],
                           preferred_element_type=F32) for c in range(nc)]
        y_b.append(jnp.dot(yb_ref[...], wpb_ref[:, hh * wcol:(hh + 1) * wcol],
                           preferred_element_type=F32))
        o_hh = jnp.concatenate([o[hh, s] for s in range(n_sub)], axis=0)
        _hgrn_epilogue(o_hh + jnp.concatenate(o_inter, axis=0), z_ref, hg_ref, hh, oa_ref)
    gated_b = jax.nn.sigmoid(z_ref[:, OFF_MB:OFF_MB + D_MODEL]) * jnp.concatenate(y_b, axis=1)

    y_a = jnp.dot(oa_ref[...], wpa_ref[...], preferred_element_type=F32)
    merged = jax.nn.sigmoid(z_ref[:, OFF_MA:OFF_MA + D_MODEL]) * y_a + gated_b
    out = x + jnp.dot(merged.astype(BF16), wout_ref[...], preferred_element_type=F32)
    y_ref[0] = _rms(out, fg_ref[...])


def _shift_rows(x, k):
    n = x.shape[0]
    return x if k % n == 0 else pltpu.roll(x, (-k) % n, 0)


def _group_bcast(x, j, group):
    t = lax.broadcasted_iota(jnp.int32, x.shape, 0) & (group - 1)
    out = _shift_rows(x, j - (group - 1))
    for tt in range(group - 2, -1, -1):
        out = jnp.where(t == tt, _shift_rows(x, j - tt), out)
    return out


def _decode_prep_hgrn(heads, z_ref, lbl_ref, a_ref, bv_ref, qd_ref, oi_ref, ga_ref):
    rows = z_ref.shape[0]
    lb = _lower_bound(lbl_ref)
    r = lax.broadcasted_iota(jnp.int32, (rows, DK), 0)
    t = r & (DEC_SEQ - 1)
    even = (r & (PAIR_ROWS - 1)) < DEC_SEQ
    blk = SUB_TILE
    ri = lax.broadcasted_iota(jnp.int32, (blk, blk), 0)
    ci = lax.broadcasted_iota(jnp.int32, (blk, blk), 1)
    causal = ((ri // DEC_SEQ) == (ci // DEC_SEQ)) & (ci <= ri)

    for hh in heads:
        cs = slice(hh * DK, (hh + 1) * DK)
        qf, k, v, logf = _hgrn_inputs(z_ref, lb, hh)
        b = _group_cumsum(logf, DEC_SEQ)
        ref = _group_bcast(b, DEC_SEQ // 2, DEC_SEQ)
        bl = _group_bcast(b, DEC_SEQ - 1, DEC_SEQ)
        qs = (qf * jnp.exp(b - ref)).astype(BF16)
        ks = (k * jnp.exp(ref - b)).astype(BF16)
        kd = k * jnp.exp(bl - b)
        vb = v.astype(BF16)
        qd_ref[:, cs] = (qf * jnp.exp(b)).astype(BF16)
        for s in range(rows // blk):
            rs = slice(s * blk, (s + 1) * blk)
            sc = lax.dot_general(qs[rs], ks[rs], (((1,), (1,)), ((), ())),
                                 preferred_element_type=F32)
            oi_ref[rs, cs] = jnp.dot(jnp.where(causal, sc, 0.0).astype(BF16), vb[rs],
                                     preferred_element_type=F32)
        dec = jnp.exp(bl)
        d1 = dec.astype(BF16).astype(F32)
        d2 = (dec - d1).astype(BF16).astype(F32)
        d3 = (dec - d1 - d2).astype(BF16).astype(F32)
        tail = jnp.where(t == 0, d1, jnp.where(t == 1, d2, jnp.where(t == 2, d3, 0.0)))
        a_ref[hh, 0] = jnp.where(even, kd, _shift_rows(tail, -DEC_SEQ)).astype(BF16)
        a_ref[hh, 1] = jnp.where(even, _shift_rows(tail, DEC_SEQ), kd).astype(BF16)
        zero = jnp.zeros_like(vb)
        bv_ref[hh, 0] = jnp.where(even, vb, zero)
        bv_ref[hh, 1] = jnp.where(even, zero, vb)
        ga_ref[:, cs] = jax.nn.silu(z_ref[:, OFF_GA + hh * DK:OFF_GA + (hh + 1) * DK])


def _decode_prep_history(p_in_ref, rw_ref, p_out_ref):
    n_seq = p_in_ref.shape[1]
    for g, w in enumerate(POOL_WINDOWS):
        gc = slice(g * G_B, (g + 1) * G_B)
        acc = None
        suffix = {}
        for j in range(POOL_BUF - 1, -1, -1):
            e = p_in_ref[j, :, gc]
            acc = e if acc is None else acc + e
            suffix[j] = acc
        for tt in range(DEC_SEQ):
            j = POOL_BUF + 1 - w + tt
            rw_ref[g, pl.ds(tt, n_seq, stride=DEC_SEQ), :] = (
                suffix[j] if j < POOL_BUF else jnp.zeros((n_seq, G_B), F32))
    for j in range(POOL_BUF - DEC_SEQ):
        p_out_ref[j] = p_in_ref[j + DEC_SEQ]


def _decode_prep_pool(z_ref, wpool_ref, ps_ref, wpb_ref, ma_ref, gb_ref, p_out_ref,
                      rw_ref, u_ref, yb_ref):
    rows = z_ref.shape[0]
    n_seq = rows // DEC_SEQ
    t = lax.broadcasted_iota(jnp.int32, (rows, G_B), 0) & (DEC_SEQ - 1)
    for g, w in enumerate(POOL_WINDOWS):
        u = z_ref[:, OFF_U + g * G_B:OFF_U + (g + 1) * G_B]
        u_ref[g] = u
        cu = u
        sh = 1
        while sh < min(w, DEC_SEQ):
            cu = cu + jnp.where(t >= sh, _shift_rows(cu, -sh), 0.0)
            sh *= 2
        pooled = (rw_ref[g] + cu) * (1.0 / w) - u
        _pool_epilogue(pooled, z_ref, g, wpool_ref, ps_ref, yb_ref)
    gb_ref[...] = jax.nn.sigmoid(z_ref[:, OFF_MB:OFF_MB + D_MODEL]) * jnp.dot(
        yb_ref[...], wpb_ref[...], preferred_element_type=F32)
    ma_ref[...] = jax.nn.sigmoid(z_ref[:, OFF_MA:OFF_MA + D_MODEL])
    for tt in range(DEC_SEQ):
        for g in range(len(POOL_WINDOWS)):
            p_out_ref[POOL_BUF - DEC_SEQ + tt, :, g * G_B:(g + 1) * G_B] = (
                u_ref[g, pl.ds(tt, n_seq, stride=DEC_SEQ), :])


def _decode_prep(part, x_ref, p_in_ref, ng_ref, win_ref, lbl_ref, wpool_ref, ps_ref, wpb_ref,
                 a_ref, bv_ref, qd_ref, oi_ref, ga_ref, ma_ref, gb_ref, p_out_ref,
                 xs_ref, z_ref, rw_ref, u_ref, yb_ref):
    hgrn = functools.partial(_decode_prep_hgrn, z_ref=z_ref, lbl_ref=lbl_ref, a_ref=a_ref,
                             bv_ref=bv_ref, qd_ref=qd_ref, oi_ref=oi_ref, ga_ref=ga_ref)
    if part == 0:
        for i in range(x_ref.shape[0]):
            xs_ref[i * DEC_SEQ:(i + 1) * DEC_SEQ, :] = x_ref[i]
        _in_proj(xs_ref[...], ng_ref, win_ref, z_ref, 0, OFF_U)
        _decode_prep_history(p_in_ref, rw_ref, p_out_ref)
    elif part == 1:
        _in_proj(xs_ref[...], ng_ref, win_ref, z_ref, OFF_U, D_IN)
    elif part == 2:
        hgrn(range(0, N_HEADS - 1))
    else:
        hgrn(range(N_HEADS - 1, N_HEADS))
        _decode_prep_pool(z_ref, wpool_ref, ps_ref, wpb_ref, ma_ref, gb_ref, p_out_ref,
                          rw_ref, u_ref, yb_ref)


def _decode_state_step(a_ref, bv_ref, qd_ref, s_in_ref, s_out_ref, o_ref):
    n_pairs = BB_SAMPLE // 2
    r8 = lax.broadcasted_iota(jnp.int32, (PAIR_ROWS, DV), 0)
    even = r8 < DEC_SEQ
    ones = (jnp.where((r8 >= DEC_SEQ) & (r8 < DEC_SEQ + 3), 1.0, 0.0).astype(BF16),
            jnp.where(r8 < 3, 1.0, 0.0).astype(BF16))
    for hh in range(N_HEADS):
        for lp in range(n_pairs):
            rs = slice(lp * PAIR_ROWS, (lp + 1) * PAIR_ROWS)
            for e in range(2):
                rhs = jnp.concatenate([bv_ref[hh, e, rs, :], ones[e]], axis=1)
                upd = lax.dot_general(a_ref[hh, e, rs, :], rhs, (((0,), (0,)), ((), ())),
                                      preferred_element_type=F32)
                s_out_ref[2 * lp + e, hh] = (upd[:, DV:2 * DV] * s_in_ref[2 * lp + e, hh]
                                             + upd[:, 0:DV])
    for hh in range(N_HEADS):
        cs = slice(hh * DK, (hh + 1) * DK)
        for lp in range(n_pairs):
            rs = slice(lp * PAIR_ROWS, (lp + 1) * PAIR_ROWS)
            q8 = qd_ref[rs, cs]
            o0 = jnp.dot(q8, s_in_ref[2 * lp, hh].astype(BF16), preferred_element_type=F32)
            o1 = jnp.dot(q8, s_in_ref[2 * lp + 1, hh].astype(BF16), preferred_element_type=F32)
            o_ref[rs, cs] = o_ref[rs, cs] + jnp.where(even, o0, o1)


def _decode_finish(o_ref, ga_ref, ma_ref, gb_ref, xs_ref, hg_ref, wpa_ref, wout_ref,
                   fg_ref, y_ref, oa_ref, *, n_seq):
    for hh in range(N_HEADS):
        cs = slice(hh * DK, (hh + 1) * DK)
        oa_ref[:, cs] = (_rms(o_ref[:, cs], hg_ref[...]) * ga_ref[:, cs]).astype(BF16)
    y_a = jnp.dot(oa_ref[...], wpa_ref[...], preferred_element_type=F32)
    merged = ma_ref[...] * y_a + gb_ref[...]
    out = xs_ref[...] + jnp.dot(merged.astype(BF16), wout_ref[...], preferred_element_type=F32)
    y = _rms(out, fg_ref[...])
    for i in range(n_seq):
        y_ref[i] = y[i * DEC_SEQ:(i + 1) * DEC_SEQ, :]


def _state_copies(hbm_ref, buf_ref, sem, blk, slot, to_hbm):
    n = BB_SAMPLE // STATE_DMAS
    copies = []
    for c in range(STATE_DMAS):
        in_hbm = hbm_ref.at[pl.ds(blk * BB_SAMPLE + c * n, n)]
        in_vmem = buf_ref.at[slot, pl.ds(c * n, n)]
        src, dst = (in_vmem, in_hbm) if to_hbm else (in_hbm, in_vmem)
        copies.append(pltpu.make_async_copy(src, dst, sem.at[slot, c]))
    return copies


def _decode_kernel(xp_ref, p_in_ref, s_in_hbm,
                   ng_ref, win_ref, lbl_ref, wpool_ref, ps_ref, wpb_ref,
                   hg_ref, wpa_ref, wout_ref, fg_ref,
                   s_out_hbm, y_ref, p_out_ref,
                   a_all, bv_all, qd_all, o_all, ga_all, ma_all, gb_all, xs_all,
                   z_ref, rw_ref, u_ref, yb_ref, oa_ref, s_in_buf, s_out_buf, in_sem, out_sem,
                   *, n_prep):
    s = pl.program_id(0)
    prows = PREP_SEQS * DEC_SEQ
    brows = BB_SAMPLE * DEC_SEQ
    n_state = n_prep * PREP_PARTS

    def load(blk, slot):
        return _state_copies(s_in_hbm, s_in_buf, in_sem, blk, slot, to_hbm=False)

    def store(blk, slot):
        return _state_copies(s_out_hbm, s_out_buf, out_sem, blk, slot, to_hbm=True)

    def block_rows(blk):
        start = blk * prows
        return pl.ds(start if isinstance(start, int) else pl.multiple_of(start, prows), prows)

    def prep(part, blk):
        r = block_rows(blk)
        _decode_prep(part, xp_ref, p_in_ref, ng_ref, win_ref, lbl_ref, wpool_ref, ps_ref, wpb_ref,
                     a_all.at[:, :, r, :], bv_all.at[:, :, r, :], qd_all.at[r, :],
                     o_all.at[r, :], ga_all.at[r, :], ma_all.at[r, :], gb_all.at[r, :],
                     p_out_ref, xs_all.at[r, :], z_ref, rw_ref, u_ref, yb_ref)

    def finish(blk):
        f = block_rows(blk)
        _decode_finish(o_all.at[f, :], ga_all.at[f, :], ma_all.at[f, :], gb_all.at[f, :],
                       xs_all.at[f, :], hg_ref, wpa_ref, wout_ref, fg_ref, y_ref, oa_ref,
                       n_seq=PREP_SEQS)

    i = s - 1
    blk = lax.div(i, PREP_PARTS)
    part_now = lax.rem(i, PREP_PARTS)
    slot = lax.rem(i, 2)

    @pl.when(s == 0)
    def _():
        for b in range(2):
            for c in load(b, b):
                c.start()
        for part in range(PREP_PARTS):
            prep(part, 0)

    @pl.when(s > 0)
    def _():
        @pl.when((i > 0) & (i + 1 < n_state))
        def _():
            for c in load(i + 1, 1 - slot):
                c.start()
        for c in load(i, slot):
            c.wait()

        @pl.when(i >= 2)
        def _():
            for c in store(i - 2, slot):
                c.wait()
        r = pl.ds(pl.multiple_of(i * brows, brows), brows)
        _decode_state_step(a_all.at[:, :, r, :], bv_all.at[:, :, r, :], qd_all.at[r, :],
                           s_in_buf.at[slot], s_out_buf.at[slot], o_all.at[r, :])
        for c in store(i, slot):
            c.start()

    for part in range(PREP_PARTS):
        @pl.when((part_now == part) & (blk + 1 < n_prep))
        def _():
            prep(part, blk + 1)

    @pl.when((part_now == 0) & (i > 0))
    def _():
        finish(blk - 1)

    @pl.when(i == n_state - 1)
    def _():
        finish(n_prep - 1)
        for c in store(i - 1, 1 - slot) + store(i, slot):
            c.wait()


def _const_spec(shape):
    n = len(shape)
    return pl.BlockSpec(shape, lambda *_: (0,) * n, pipeline_mode=pl.Buffered(1))


def kernel(x_prompt, x_sample, state_hgrn, state_pool, norm_g, w_in, lb_logits, hgrn_norm_g,
           w_proj_a, w_pool, pool_scale, w_proj_b, w_out, final_norm_g):
    batch, seq, _ = x_prompt.shape
    dec_batch, dec_seq, _ = x_sample.shape
    assert norm_g.shape[0] == 1 and lb_logits.shape[0] == 2, "single-layer decoder only"
    assert seq % TM_PROMPT == 0 and TM_PROMPT % SUB_TILE == 0 and dec_batch % BB_SAMPLE == 0
    assert dec_seq == DEC_SEQ and PAIR_ROWS == SUBLANES and BB_SAMPLE % 2 == 0
    assert dec_batch % PREP_SEQS == 0 and (PREP_SEQS * DEC_SEQ) % SUB_TILE == 0
    assert PREP_SEQS == PREP_PARTS * BB_SAMPLE and BB_SAMPLE % STATE_DMAS == 0
    assert PAST_LEN >= max(POOL_WINDOWS)

    n_groups = len(POOL_WINDOWS)
    tm = TM_PROMPT
    nt = seq // tm
    assert w_in.shape[1:] == (2 * tm, D_IN) and w_proj_a.shape[1] == tm, "weight staging layout"
    assert tm % (WIN_SLOTS * 2 * SUBLANES) == 0, "bf16-tile-aligned w_in staging blocks"
    hbm = pl.BlockSpec(memory_space=pl.ANY)
    bf16_shapes = [(D_MODEL, D_IN), (W_BRANCH, D_MODEL), (n_groups, G_B, G_B),
                   (W_BRANCH, D_MODEL), (D_MODEL, D_MODEL)]
    y_p, s_p, p_p, win_b, wpa_b, wpool_b, wpb_b, wout_b = pl.pallas_call(
        _prompt_kernel,
        grid=(batch, nt),
        in_specs=[pl.BlockSpec((1, tm, D_MODEL), lambda b, t: (b, t, 0)),
                  _const_spec((1, D_MODEL)), hbm, _const_spec((2, W_BRANCH)), _const_spec((1, DV)),
                  hbm, hbm, _const_spec((1, W_BRANCH)), hbm, hbm, _const_spec((1, D_MODEL))],
        out_specs=[
            pl.BlockSpec((1, tm, D_MODEL), lambda b, t: (b, t, 0)),
            pl.BlockSpec((1, N_HEADS, DK, DV), lambda b, t: (b, 0, 0, 0)),
            pl.BlockSpec((POOL_BUF, batch, W_BRANCH), lambda b, t: (0, 0, 0)),
        ] + [hbm] * len(bf16_shapes),
        out_shape=[
            jax.ShapeDtypeStruct((batch, seq, D_MODEL), F32),
            jax.ShapeDtypeStruct((batch, N_HEADS, DK, DV), F32),
            jax.ShapeDtypeStruct((POOL_BUF, batch, W_BRANCH), F32),
        ] + [jax.ShapeDtypeStruct(s, BF16) for s in bf16_shapes],
        scratch_shapes=[
            pltpu.VMEM((tm, D_IN), F32),
            pltpu.VMEM((N_HEADS, DV, DK), F32),
            pltpu.VMEM((tm + 2 * SUBLANES, W_BRANCH), F32),
            pltpu.VMEM((N_HEADS, tm, (SUB_TILE // CHUNK) * DK), BF16),
            pltpu.VMEM((N_HEADS, tm // CHUNK, DK, DV), BF16),
            pltpu.VMEM((tm, W_BRANCH), BF16),
            pltpu.VMEM((tm, W_BRANCH), BF16),
            pltpu.VMEM((tm, W_BRANCH), BF16),
            pltpu.VMEM((tm, W_BRANCH), BF16),
            pltpu.VMEM((tm, W_BRANCH), BF16),
            pltpu.VMEM((tm, W_BRANCH), BF16),
        ] + [pltpu.VMEM(s, BF16) for s in bf16_shapes] + [
            pltpu.SemaphoreType.DMA((5 + WIN_SLOTS,)),
            pltpu.SemaphoreType.DMA((len(bf16_shapes),)),
        ],
        compiler_params=pltpu.CompilerParams(
            dimension_semantics=("arbitrary", "arbitrary"),
            vmem_limit_bytes=VMEM_LIMIT_BYTES),
        name="hgrn2_pool_prompt",
    )(x_prompt, norm_g, w_in[0], lb_logits, hgrn_norm_g, w_proj_a[0],
      w_pool[0].reshape(n_groups * G_B, G_B), pool_scale, w_proj_b[0], w_out[0],
      final_norm_g.reshape(1, D_MODEL))
    weights = (norm_g, win_b, lb_logits, hgrn_norm_g, wpa_b, wpool_b, pool_scale, wpb_b, wout_b,
               final_norm_g.reshape(1, D_MODEL))

    rows = dec_batch * DEC_SEQ
    pool_in = jnp.transpose(state_pool[0], (1, 0, 2))
    ng, win_b, lbl, hg, wpa_b, wpool_b, ps, wpb_b, wout_b, fg = weights
    pseq = PREP_SEQS
    prows = pseq * DEC_SEQ
    n_prep = dec_batch // pseq
    bb = BB_SAMPLE

    def prep_blk(s):
        return jnp.minimum((s + PREP_PARTS - 1) // PREP_PARTS, n_prep - 1)

    def finish_blk(s):
        return jnp.minimum(jnp.maximum(s - 2, 0) // PREP_PARTS, n_prep - 1)

    s_s, y_s, pool_out = pl.pallas_call(
        functools.partial(_decode_kernel, n_prep=n_prep),
        grid=(1 + dec_batch // bb,),
        in_specs=[
            pl.BlockSpec((pseq, DEC_SEQ, D_MODEL), lambda s: (prep_blk(s), 0, 0)),
            pl.BlockSpec((POOL_BUF, pseq, W_BRANCH), lambda s: (0, prep_blk(s), 0),
                         pipeline_mode=pl.Buffered(1)),
            hbm,
            _const_spec(ng.shape), _const_spec(win_b.shape), _const_spec(lbl.shape),
            _const_spec(wpool_b.shape), _const_spec(ps.shape), _const_spec(wpb_b.shape),
            _const_spec(hg.shape), _const_spec(wpa_b.shape), _const_spec(wout_b.shape),
            _const_spec(fg.shape),
        ],
        out_specs=[
            hbm,
            pl.BlockSpec((pseq, DEC_SEQ, D_MODEL), lambda s: (finish_blk(s), 0, 0)),
            pl.BlockSpec((POOL_BUF, pseq, W_BRANCH), lambda s: (0, prep_blk(s), 0)),
        ],
        out_shape=[
            jax.ShapeDtypeStruct((dec_batch, N_HEADS, DK, DV), F32),
            jax.ShapeDtypeStruct(x_sample.shape, F32),
            jax.ShapeDtypeStruct((POOL_BUF, dec_batch, W_BRANCH), F32),
        ],
        scratch_shapes=[
            pltpu.VMEM((N_HEADS, 2, rows, DK), BF16),
            pltpu.VMEM((N_HEADS, 2, rows, DV), BF16),
            pltpu.VMEM((rows, W_BRANCH), BF16),
            pltpu.VMEM((rows, W_BRANCH), F32),
            pltpu.VMEM((rows, W_BRANCH), F32),
            pltpu.VMEM((rows, D_MODEL), F32),
            pltpu.VMEM((rows, D_MODEL), F32),
            pltpu.VMEM((rows, D_MODEL), F32),
            pltpu.VMEM((prows, D_IN), F32),
            pltpu.VMEM((len(POOL_WINDOWS), prows, G_B), F32),
            pltpu.VMEM((len(POOL_WINDOWS), prows, G_B), F32),
            pltpu.VMEM((prows, W_BRANCH), BF16),
            pltpu.VMEM((prows, W_BRANCH), BF16),
            pltpu.VMEM((2, bb, N_HEADS, DK, DV), F32),
            pltpu.VMEM((2, bb, N_HEADS, DK, DV), F32),
            pltpu.SemaphoreType.DMA((2, STATE_DMAS)),
            pltpu.SemaphoreType.DMA((2, STATE_DMAS)),
        ],
        compiler_params=pltpu.CompilerParams(
            dimension_semantics=("arbitrary",), vmem_limit_bytes=DECODE_VMEM_LIMIT_BYTES),
        name="hgrn2_pool_decode",
    )(x_sample, pool_in, state_hgrn[0], ng, win_b, lbl, wpool_b, ps, wpb_b,
      hg, wpa_b, wout_b, fg)

    p_p = jnp.transpose(p_p, (1, 0, 2))
    p_s = jnp.transpose(pool_out, (1, 0, 2))
    return (y_p, y_s, s_p[None], p_p[None], s_s[None], p_s[None])
```

```python
import functools

import jax
import jax.numpy as jnp
from jax import lax
from jax.experimental import pallas as pl
from jax.experimental.pallas import tpu as pltpu

F32 = jnp.float32
BF16 = jnp.bfloat16

D_MODEL = 1024
W_BRANCH = 512
N_HEADS = 4
DK = 128
DV = 128
CHUNK = 32
POOL_WINDOWS = (2, 4, 8, 16)
G_B = 128
POOL_BUF = 15
PAST_LEN = 16384
EPS = 1e-6
D_IN = 4 * W_BRANCH + 2 * W_BRANCH + 2 * D_MODEL
OFF_Q, OFF_F, OFF_I, OFF_GA = 0, 512, 1024, 1536
OFF_U, OFF_GB, OFF_MA, OFF_MB = 2048, 2560, 3072, 4096

SUBLANES = 8
VMEM_LIMIT_BYTES = 56 * 1024 * 1024
DECODE_VMEM_LIMIT_BYTES = 60 * 1024 * 1024

TM_PROMPT = 512
WIN_SLOTS = 4
SUB_TILE = 256
PREP_SEQS = 64
BB_SAMPLE = 16
STATE_DMAS = 2
DEC_SEQ = 4
PAIR_ROWS = 2 * DEC_SEQ
PREP_PARTS = 4


def _rms(x, g):
    ms = jnp.mean(x * x, axis=-1, keepdims=True)
    return x * lax.rsqrt(ms + EPS) * g


def _lower_bound(lbl_ref):
    l0 = lbl_ref[0:1, :]
    l1 = lbl_ref[1:2, :]
    m = jnp.maximum(l0, l1)
    e0 = jnp.exp(l0 - m)
    e1 = jnp.exp(l1 - m)
    return e0 / (e0 + e1)


def _in_proj(x, ng_ref, win_ref, z_ref, c0=0, c1=D_IN):
    h = _rms(x, ng_ref[...]).astype(BF16)
    z_ref[:, c0:c1] = jnp.dot(h, win_ref[:, c0:c1], preferred_element_type=F32)


def _hgrn_inputs(z_ref, lb, hh):
    c = hh * DK
    q = z_ref[:, OFF_Q + c:OFF_Q + c + DK]
    f = z_ref[:, OFF_F + c:OFF_F + c + DK]
    v = z_ref[:, OFF_I + c:OFF_I + c + DK]
    lbh = lb[:, c:c + DK]
    fg = lbh + (1.0 - lbh) * jax.nn.sigmoid(f)
    logf = jnp.log(fg)
    k = 1.0 - fg
    qf = jax.nn.silu(q) * (DK ** -0.5)
    return qf, k, v, logf


def _group_cumsum(x, group):
    row = lax.broadcasted_iota(jnp.int32, x.shape, 0) & (group - 1)
    s = 1
    while s < group:
        x = x + jnp.where(row >= s, pltpu.roll(x, s, 0), 0.0)
        s *= 2
    return x


def _hgrn_epilogue(o, z_ref, hg_ref, hh, oa_ref):
    c = hh * DK
    ga = z_ref[:, OFF_GA + c:OFF_GA + c + DK]
    on = _rms(o, hg_ref[...]) * jax.nn.silu(ga)
    oa_ref[:, c:c + DK] = on.astype(BF16)


def _pool_epilogue(pooled, z_ref, g, wpool_ref, ps_ref, yb_ref):
    c = g * G_B
    mixed = jnp.dot(pooled.astype(BF16), wpool_ref[g], preferred_element_type=F32)
    gb = z_ref[:, OFF_GB + c:OFF_GB + c + G_B]
    yb = mixed * ps_ref[:, c:c + G_B] * jax.nn.silu(gb)
    yb_ref[:, c:c + G_B] = yb.astype(BF16)


def _weight_copies(hbm_refs, stage_ref, sem):
    win_hbm, wpa_hbm, wpool_hbm, wpb_hbm, wout_hbm = hbm_refs
    rows = stage_ref.shape[0]
    small = [
        (wpa_hbm, stage_ref.at[:, 0:D_MODEL]),
        (wpb_hbm, stage_ref.at[:, D_MODEL:2 * D_MODEL]),
        (wout_hbm.at[0:rows, :], stage_ref.at[:, 2 * D_MODEL:3 * D_MODEL]),
        (wout_hbm.at[rows:2 * rows, :], stage_ref.at[:, 3 * D_MODEL:4 * D_MODEL]),
        (wpool_hbm, stage_ref.at[:, 4 * D_MODEL:4 * D_MODEL + G_B]),
    ]
    small = [pltpu.make_async_copy(src, dst, sem.at[i]) for i, (src, dst) in enumerate(small)]
    blk = rows // WIN_SLOTS

    def win(k):
        slot = k % WIN_SLOTS
        return pltpu.make_async_copy(win_hbm.at[k * blk:(k + 1) * blk, :],
                                     stage_ref.at[slot * blk:(slot + 1) * blk, :],
                                     sem.at[len(small) + slot])
    return small, win, blk


def _weight_exports(vmem_refs, out_refs, sem):
    return [pltpu.make_async_copy(src, dst, sem.at[i])
            for i, (src, dst) in enumerate(zip(vmem_refs, out_refs))]


def _prompt_kernel(x_ref, ng_ref, win_hbm, lbl_ref, hg_ref, wpa_hbm, wpool_hbm, ps_ref,
                   wpb_hbm, wout_hbm, fg_ref,
                   y_ref, s_out_ref, p_out_ref, win_out, wpa_out, wpool_out, wpb_out, wout_out,
                   z_ref, st_ref, ext_ref, kx_ref, sn_ref, qs_ref, ks_ref, qd_ref, vb_ref,
                   oa_ref, yb_ref, win_ref, wpa_ref, wpool_ref, wpb_ref, wout_ref,
                   load_sem, export_sem):
    tm = TM_PROMPT
    sub = SUB_TILE
    n_sub = tm // sub
    nc = tm // CHUNK
    ncs = sub // CHUNK
    t = pl.program_id(1)
    first = t == 0
    hdr = 2 * SUBLANES
    bf16_weights = (win_ref, wpa_ref, wpool_ref, wpb_ref, wout_ref)
    exports = _weight_exports(bf16_weights, (win_out, wpa_out, wpool_out, wpb_out, wout_out),
                              export_sem)

    @pl.when((pl.program_id(0) == 0) & first)
    def _():
        small, win, blk = _weight_copies((win_hbm, wpa_hbm, wpool_hbm, wpb_hbm, wout_hbm),
                                         z_ref, load_sem)
        for c in small:
            c.start()
        st_ref[...] = jnp.zeros_like(st_ref)
        ext_ref[...] = jnp.zeros_like(ext_ref)
        kx_ref[...] = jnp.zeros_like(kx_ref)
        for c in small:
            c.wait()
        wpa_ref[...] = z_ref[:, 0:D_MODEL].astype(BF16)
        wpb_ref[...] = z_ref[:, D_MODEL:2 * D_MODEL].astype(BF16)
        wout_ref[0:tm, :] = z_ref[:, 2 * D_MODEL:3 * D_MODEL].astype(BF16)
        wout_ref[tm:2 * tm, :] = z_ref[:, 3 * D_MODEL:4 * D_MODEL].astype(BF16)
        for g in range(len(POOL_WINDOWS)):
            wpool_ref[g] = z_ref[g * G_B:(g + 1) * G_B,
                                 4 * D_MODEL:4 * D_MODEL + G_B].astype(BF16)
        n_blk = D_MODEL // blk
        for k in range(WIN_SLOTS):
            win(k).start()
        for k in range(n_blk):
            slot = k % WIN_SLOTS
            win(k).wait()
            win_ref[k * blk:(k + 1) * blk, :] = z_ref[slot * blk:(slot + 1) * blk, :].astype(BF16)
            if k + WIN_SLOTS < n_blk:
                win(k + WIN_SLOTS).start()
        for c in exports:
            c.start()

    @pl.when((pl.program_id(0) == pl.num_programs(0) - 1) & (t == pl.num_programs(1) - 1))
    def _():
        for c in exports:
            c.wait()

    x = x_ref[0]
    _in_proj(x, ng_ref, win_ref, z_ref)
    lb = _lower_bound(lbl_ref)

    ri = lax.broadcasted_iota(jnp.int32, (sub, sub), 0)
    ci = lax.broadcasted_iota(jnp.int32, (sub, sub), 1)
    causal = ((ri // CHUNK) == (ci // CHUNK)) & (ci <= ri)

    heads = range(N_HEADS)
    units = [(hh, s) for hh in heads for s in range(n_sub)]

    def rows(s):
        return slice(s * sub, (s + 1) * sub)

    def cols(hh):
        return slice(hh * DK, (hh + 1) * DK)

    dec = []
    for hh in heads:
        qf, k, v, logf = _hgrn_inputs(z_ref, lb, hh)
        b = _group_cumsum(logf, CHUNK)
        b3 = b.reshape(nc, CHUNK, DK)
        ref = b3[:, CHUNK // 2:CHUNK // 2 + 1, :]
        bl = b3[:, CHUNK - 1:CHUNK, :]
        q3 = qf.reshape(nc, CHUNK, DK)
        k3 = k.reshape(nc, CHUNK, DK)
        qs_ref[:, cols(hh)] = (q3 * jnp.exp(b3 - ref)).reshape(tm, DK).astype(BF16)
        ks_ref[:, cols(hh)] = (k3 * jnp.exp(ref - b3)).reshape(tm, DK).astype(BF16)
        qd_ref[:, cols(hh)] = (q3 * jnp.exp(b3)).reshape(tm, DK).astype(BF16)
        vb_ref[:, cols(hh)] = v.astype(BF16)
        kd = (k3 * jnp.exp(bl - b3)).reshape(tm, DK).astype(BF16)
        dec.append(jnp.exp(bl))
        for c in range(nc):
            cc = c % ncs
            kx_ref[hh, c * CHUNK:(c + 1) * CHUNK, cc * DK:(cc + 1) * DK] = (
                kd[c * CHUNK:(c + 1) * CHUNK, :])

    ext_ref[0:hdr, :] = jnp.where(first, 0.0, ext_ref[tm:tm + hdr, :])
    ext_ref[hdr:hdr + tm, :] = z_ref[:, OFF_U:OFF_U + W_BRANCH]
    pos1 = t * tm + lax.broadcasted_iota(jnp.int32, (tm, 1), 0) + 1
    for g, w in enumerate(POOL_WINDOWS):
        c = g * G_B
        s = ext_ref[:, c:c + G_B]
        sh = 1
        while sh < w:
            s = s + pltpu.roll(s, sh, 0)
            sh *= 2
        inv = jnp.where(pos1 >= w, 1.0 / w, 1.0 / pos1.astype(F32))
        pooled = s[hdr:, :] * inv - z_ref[:, OFF_U + c:OFF_U + c + G_B]
        _pool_epilogue(pooled, z_ref, g, wpool_ref, ps_ref, yb_ref)
    for j in range(POOL_BUF):
        r = hdr + tm - POOL_BUF + j
        p_out_ref[j, pl.ds(pl.program_id(0), 1), :] = ext_ref[r:r + 1, :]

    sc = {(hh, s): lax.dot_general(qs_ref[rows(s), cols(hh)], ks_ref[rows(s), cols(hh)],
                                   (((1,), (1,)), ((), ())), preferred_element_type=F32)
          for hh, s in units}
    ut = {(hh, s): lax.dot_general(vb_ref[rows(s), cols(hh)], kx_ref[hh, rows(s), :],
                                   (((0,), (0,)), ((), ())), preferred_element_type=F32)
          for hh, s in units}
    o = {(hh, s): jnp.dot(jnp.where(causal, sc[hh, s], 0.0).astype(BF16),
                          vb_ref[rows(s), cols(hh)], preferred_element_type=F32)
         for hh, s in units}
    for hh in heads:
        st = jnp.where(first, 0.0, st_ref[hh])
        for c in range(nc):
            sn_ref[hh, c] = st.T.astype(BF16)
            st = dec[hh][c] * st + ut[hh, c // ncs][:, (c % ncs) * DK:(c % ncs + 1) * DK]
        st_ref[hh] = st
        s_out_ref[0, hh] = st.T
    wcol = D_MODEL // N_HEADS
    y_b = []
    for hh in heads:
        o_inter = [jnp.dot(qd_ref[c * CHUNK:(c + 1) * CHUNK, cols(hh)], sn_ref[hh, c],
                           preferred_element_type=F32) for c in range(nc)]
        y_b.append(jnp.dot(yb_ref[...], wpb_ref[:, hh * wcol:(hh + 1) * wcol],
                           preferred_element_type=F32))
        o_hh = jnp.concatenate([o[hh, s] for s in range(n_sub)], axis=0)
        _hgrn_epilogue(o_hh + jnp.concatenate(o_inter, axis=0), z_ref, hg_ref, hh, oa_ref)
    gated_b = jax.nn.sigmoid(z_ref[:, OFF_MB:OFF_MB + D_MODEL]) * jnp.concatenate(y_b, axis=1)

    y_a = jnp.dot(oa_ref[...], wpa_ref[...], preferred_element_type=F32)
    merged = jax.nn.sigmoid(z_ref[:, OFF_MA:OFF_MA + D_MODEL]) * y_a + gated_b
    out = x + jnp.dot(merged.astype(BF16), wout_ref[...], preferred_element_type=F32)
    y_ref[0] = _rms(out, fg_ref[...])


def _shift_rows(x, k):
    n = x.shape[0]
    return x if k % n == 0 else pltpu.roll(x, (-k) % n, 0)


def _group_bcast(x, j, group):
    t = lax.broadcasted_iota(jnp.int32, x.shape, 0) & (group - 1)
    out = _shift_rows(x, j - (group - 1))
    for tt in range(group - 2, -1, -1):
        out = jnp.where(t == tt, _shift_rows(x, j - tt), out)
    return out


def _decode_prep_hgrn(heads, z_ref, lbl_ref, a_ref, bv_ref, qd_ref, oi_ref, ga_ref):
    rows = z_ref.shape[0]
    lb = _lower_bound(lbl_ref)
    r = lax.broadcasted_iota(jnp.int32, (rows, DK), 0)
    t = r & (DEC_SEQ - 1)
    even = (r & (PAIR_ROWS - 1)) < DEC_SEQ
    blk = SUB_TILE
    ri = lax.broadcasted_iota(jnp.int32, (blk, blk), 0)
    ci = lax.broadcasted_iota(jnp.int32, (blk, blk), 1)
    causal = ((ri // DEC_SEQ) == (ci // DEC_SEQ)) & (ci <= ri)

    for hh in heads:
        cs = slice(hh * DK, (hh + 1) * DK)
        qf, k, v, logf = _hgrn_inputs(z_ref, lb, hh)
        b = _group_cumsum(logf, DEC_SEQ)
        ref = _group_bcast(b, DEC_SEQ // 2, DEC_SEQ)
        bl = _group_bcast(b, DEC_SEQ - 1, DEC_SEQ)
        qs = (qf * jnp.exp(b - ref)).astype(BF16)
        ks = (k * jnp.exp(ref - b)).astype(BF16)
        kd = k * jnp.exp(bl - b)
        vb = v.astype(BF16)
        qd_ref[:, cs] = (qf * jnp.exp(b)).astype(BF16)
        for s in range(rows // blk):
            rs = slice(s * blk, (s + 1) * blk)
            sc = lax.dot_general(qs[rs], ks[rs], (((1,), (1,)), ((), ())),
                                 preferred_element_type=F32)
            oi_ref[rs, cs] = jnp.dot(jnp.where(causal, sc, 0.0).astype(BF16), vb[rs],
                                     preferred_element_type=F32)
        dec = jnp.exp(bl)
        d1 = dec.astype(BF16).astype(F32)
        d2 = (dec - d1).astype(BF16).astype(F32)
        d3 = (dec - d1 - d2).astype(BF16).astype(F32)
        tail = jnp.where(t == 0, d1, jnp.where(t == 1, d2, jnp.where(t == 2, d3, 0.0)))
        a_ref[hh, 0] = jnp.where(even, kd, _shift_rows(tail, -DEC_SEQ)).astype(BF16)
        a_ref[hh, 1] = jnp.where(even, _shift_rows(tail, DEC_SEQ), kd).astype(BF16)
        zero = jnp.zeros_like(vb)
        bv_ref[hh, 0] = jnp.where(even, vb, zero)
        bv_ref[hh, 1] = jnp.where(even, zero, vb)
        ga_ref[:, cs] = jax.nn.silu(z_ref[:, OFF_GA + hh * DK:OFF_GA + (hh + 1) * DK])


def _decode_prep_history(p_in_ref, rw_ref, p_out_ref):
    n_seq = p_in_ref.shape[1]
    for g, w in enumerate(POOL_WINDOWS):
        gc = slice(g * G_B, (g + 1) * G_B)
        acc = None
        suffix = {}
        for j in range(POOL_BUF - 1, -1, -1):
            e = p_in_ref[j, :, gc]
            acc = e if acc is None else acc + e
            suffix[j] = acc
        for tt in range(DEC_SEQ):
            j = POOL_BUF + 1 - w + tt
            rw_ref[g, pl.ds(tt, n_seq, stride=DEC_SEQ), :] = (
                suffix[j] if j < POOL_BUF else jnp.zeros((n_seq, G_B), F32))
    for j in range(POOL_BUF - DEC_SEQ):
        p_out_ref[j] = p_in_ref[j + DEC_SEQ]


def _decode_prep_pool(z_ref, wpool_ref, ps_ref, wpb_ref, ma_ref, gb_ref, p_out_ref,
                      rw_ref, u_ref, yb_ref):
    rows = z_ref.shape[0]
    n_seq = rows // DEC_SEQ
    t = lax.broadcasted_iota(jnp.int32, (rows, G_B), 0) & (DEC_SEQ - 1)
    for g, w in enumerate(POOL_WINDOWS):
        u = z_ref[:, OFF_U + g * G_B:OFF_U + (g + 1) * G_B]
        u_ref[g] = u
        cu = u
        sh = 1
        while sh < min(w, DEC_SEQ):
            cu = cu + jnp.where(t >= sh, _shift_rows(cu, -sh), 0.0)
            sh *= 2
        pooled = (rw_ref[g] + cu) * (1.0 / w) - u
        _pool_epilogue(pooled, z_ref, g, wpool_ref, ps_ref, yb_ref)
    gb_ref[...] = jax.nn.sigmoid(z_ref[:, OFF_MB:OFF_MB + D_MODEL]) * jnp.dot(
        yb_ref[...], wpb_ref[...], preferred_element_type=F32)
    ma_ref[...] = jax.nn.sigmoid(z_ref[:, OFF_MA:OFF_MA + D_MODEL])
    for tt in range(DEC_SEQ):
        for g in range(len(POOL_WINDOWS)):
            p_out_ref[POOL_BUF - DEC_SEQ + tt, :, g * G_B:(g + 1) * G_B] = (
                u_ref[g, pl.ds(tt, n_seq, stride=DEC_SEQ), :])


def _decode_prep(part, x_ref, p_in_ref, ng_ref, win_ref, lbl_ref, wpool_ref, ps_ref, wpb_ref,
                 a_ref, bv_ref, qd_ref, oi_ref, ga_ref, ma_ref, gb_ref, p_out_ref,
                 xs_ref, z_ref, rw_ref, u_ref, yb_ref):
    hgrn = functools.partial(_decode_prep_hgrn, z_ref=z_ref, lbl_ref=lbl_ref, a_ref=a_ref,
                             bv_ref=bv_ref, qd_ref=qd_ref, oi_ref=oi_ref, ga_ref=ga_ref)
    if part == 0:
        for i in range(x_ref.shape[0]):
            xs_ref[i * DEC_SEQ:(i + 1) * DEC_SEQ, :] = x_ref[i]
        _in_proj(xs_ref[...], ng_ref, win_ref, z_ref, 0, OFF_U)
        _decode_prep_history(p_in_ref, rw_ref, p_out_ref)
    elif part == 1:
        _in_proj(xs_ref[...], ng_ref, win_ref, z_ref, OFF_U, D_IN)
    elif part == 2:
        hgrn(range(0, N_HEADS - 1))
    else:
        hgrn(range(N_HEADS - 1, N_HEADS))
        _decode_prep_pool(z_ref, wpool_ref, ps_ref, wpb_ref, ma_ref, gb_ref, p_out_ref,
                          rw_ref, u_ref, yb_ref)


def _decode_state_step(a_ref, bv_ref, qd_ref, s_in_ref, s_out_ref, o_ref):
    n_pairs = BB_SAMPLE // 2
    r8 = lax.broadcasted_iota(jnp.int32, (PAIR_ROWS, DV), 0)
    even = r8 < DEC_SEQ
    ones = (jnp.where((r8 >= DEC_SEQ) & (r8 < DEC_SEQ + 3), 1.0, 0.0).astype(BF16),
            jnp.where(r8 < 3, 1.0, 0.0).astype(BF16))
    for hh in range(N_HEADS):
        for lp in range(n_pairs):
            rs = slice(lp * PAIR_ROWS, (lp + 1) * PAIR_ROWS)
            for e in range(2):
                rhs = jnp.concatenate([bv_ref[hh, e, rs, :], ones[e]], axis=1)
                upd = lax.dot_general(a_ref[hh, e, rs, :], rhs, (((0,), (0,)), ((), ())),
                                      preferred_element_type=F32)
                s_out_ref[2 * lp + e, hh] = (upd[:, DV:2 * DV] * s_in_ref[2 * lp + e, hh]
                                             + upd[:, 0:DV])
    for hh in range(N_HEADS):
        cs = slice(hh * DK, (hh + 1) * DK)
        for lp in range(n_pairs):
            rs = slice(lp * PAIR_ROWS, (lp + 1) * PAIR_ROWS)
            q8 = qd_ref[rs, cs]
            o0 = jnp.dot(q8, s_in_ref[2 * lp, hh].astype(BF16), preferred_element_type=F32)
            o1 = jnp.dot(q8, s_in_ref[2 * lp + 1, hh].astype(BF16), preferred_element_type=F32)
            o_ref[rs, cs] = o_ref[rs, cs] + jnp.where(even, o0, o1)


def _decode_finish(o_ref, ga_ref, ma_ref, gb_ref, xs_ref, hg_ref, wpa_ref, wout_ref,
                   fg_ref, y_ref, oa_ref, *, n_seq):
    for hh in range(N_HEADS):
        cs = slice(hh * DK, (hh + 1) * DK)
        oa_ref[:, cs] = (_rms(o_ref[:, cs], hg_ref[...]) * ga_ref[:, cs]).astype(BF16)
    y_a = jnp.dot(oa_ref[...], wpa_ref[...], preferred_element_type=F32)
    merged = ma_ref[...] * y_a + gb_ref[...]
    out = xs_ref[...] + jnp.dot(merged.astype(BF16), wout_ref[...], preferred_element_type=F32)
    y = _rms(out, fg_ref[...])
    for i in range(n_seq):
        y_ref[i] = y[i * DEC_SEQ:(i + 1) * DEC_SEQ, :]


def _state_copies(hbm_ref, buf_ref, sem, blk, slot, to_hbm):
    n = BB_SAMPLE // STATE_DMAS
    copies = []
    for c in range(STATE_DMAS):
        in_hbm = hbm_ref.at[pl.ds(blk * BB_SAMPLE + c * n, n)]
        in_vmem = buf_ref.at[slot, pl.ds(c * n, n)]
        src, dst = (in_vmem, in_hbm) if to_hbm else (in_hbm, in_vmem)
        copies.append(pltpu.make_async_copy(src, dst, sem.at[slot, c]))
    return copies


def _decode_kernel(xp_ref, p_in_ref, s_in_hbm,
                   ng_ref, win_ref, lbl_ref, wpool_ref, ps_ref, wpb_ref,
                   hg_ref, wpa_ref, wout_ref, fg_ref,
                   s_out_hbm, y_ref, p_out_ref,
                   a_all, bv_all, qd_all, o_all, ga_all, ma_all, gb_all, xs_all,
                   z_ref, rw_ref, u_ref, yb_ref, oa_ref, s_in_buf, s_out_buf, in_sem, out_sem,
                   *, n_prep):
    s = pl.program_id(0)
    prows = PREP_SEQS * DEC_SEQ
    brows = BB_SAMPLE * DEC_SEQ
    n_state = n_prep * PREP_PARTS

    def load(blk, slot):
        return _state_copies(s_in_hbm, s_in_buf, in_sem, blk, slot, to_hbm=False)

    def store(blk, slot):
        return _state_copies(s_out_hbm, s_out_buf, out_sem, blk, slot, to_hbm=True)

    def block_rows(blk):
        start = blk * prows
        return pl.ds(start if isinstance(start, int) else pl.multiple_of(start, prows), prows)

    def prep(part, blk):
        r = block_rows(blk)
        _decode_prep(part, xp_ref, p_in_ref, ng_ref, win_ref, lbl_ref, wpool_ref, ps_ref, wpb_ref,
                     a_all.at[:, :, r, :], bv_all.at[:, :, r, :], qd_all.at[r, :],
                     o_all.at[r, :], ga_all.at[r, :], ma_all.at[r, :], gb_all.at[r, :],
                     p_out_ref, xs_all.at[r, :], z_ref, rw_ref, u_ref, yb_ref)

    def finish(blk):
        f = block_rows(blk)
        _decode_finish(o_all.at[f, :], ga_all.at[f, :], ma_all.at[f, :], gb_all.at[f, :],
                       xs_all.at[f, :], hg_ref, wpa_ref, wout_ref, fg_ref, y_ref, oa_ref,
                       n_seq=PREP_SEQS)

    i = s - 1
    blk = lax.div(i, PREP_PARTS)
    part_now = lax.rem(i, PREP_PARTS)
    slot = lax.rem(i, 2)

    @pl.when(s == 0)
    def _():
        for b in range(2):
            for c in load(b, b):
                c.start()
        for part in range(PREP_PARTS):
            prep(part, 0)

    @pl.when(s > 0)
    def _():
        for c in load(i, slot):
            c.wait()

        @pl.when(i >= 2)
        def _():
            for c in store(i - 2, slot):
                c.wait()
        r = pl.ds(pl.multiple_of(i * brows, brows), brows)
        _decode_state_step(a_all.at[:, :, r, :], bv_all.at[:, :, r, :], qd_all.at[r, :],
                           s_in_buf.at[slot], s_out_buf.at[slot], o_all.at[r, :])
        for c in store(i, slot):
            c.start()

        @pl.when(i + 2 < n_state)
        def _():
            for c in load(i + 2, slot):
                c.start()

    for part in range(PREP_PARTS):
        @pl.when((part_now == part) & (blk + 1 < n_prep))
        def _():
            prep(part, blk + 1)

    @pl.when((part_now == 0) & (i > 0))
    def _():
        finish(blk - 1)

    @pl.when(i == n_state - 1)
    def _():
        finish(n_prep - 1)
        for c in store(i - 1, 1 - slot) + store(i, slot):
            c.wait()


def _const_spec(shape):
    n = len(shape)
    return pl.BlockSpec(shape, lambda *_: (0,) * n, pipeline_mode=pl.Buffered(1))


def kernel(x_prompt, x_sample, state_hgrn, state_pool, norm_g, w_in, lb_logits, hgrn_norm_g,
           w_proj_a, w_pool, pool_scale, w_proj_b, w_out, final_norm_g):
    batch, seq, _ = x_prompt.shape
    dec_batch, dec_seq, _ = x_sample.shape
    assert norm_g.shape[0] == 1 and lb_logits.shape[0] == 2, "single-layer decoder only"
    assert seq % TM_PROMPT == 0 and TM_PROMPT % SUB_TILE == 0 and dec_batch % BB_SAMPLE == 0
    assert dec_seq == DEC_SEQ and PAIR_ROWS == SUBLANES and BB_SAMPLE % 2 == 0
    assert dec_batch % PREP_SEQS == 0 and (PREP_SEQS * DEC_SEQ) % SUB_TILE == 0
    assert PREP_SEQS == PREP_PARTS * BB_SAMPLE and BB_SAMPLE % STATE_DMAS == 0
    assert PAST_LEN >= max(POOL_WINDOWS)

    n_groups = len(POOL_WINDOWS)
    tm = TM_PROMPT
    nt = seq // tm
    assert w_in.shape[1:] == (2 * tm, D_IN) and w_proj_a.shape[1] == tm, "weight staging layout"
    assert tm % (WIN_SLOTS * 2 * SUBLANES) == 0, "bf16-tile-aligned w_in staging blocks"
    hbm = pl.BlockSpec(memory_space=pl.ANY)
    bf16_shapes = [(D_MODEL, D_IN), (W_BRANCH, D_MODEL), (n_groups, G_B, G_B),
                   (W_BRANCH, D_MODEL), (D_MODEL, D_MODEL)]
    y_p, s_p, p_p, win_b, wpa_b, wpool_b, wpb_b, wout_b = pl.pallas_call(
        _prompt_kernel,
        grid=(batch, nt),
        in_specs=[pl.BlockSpec((1, tm, D_MODEL), lambda b, t: (b, t, 0)),
                  _const_spec((1, D_MODEL)), hbm, _const_spec((2, W_BRANCH)), _const_spec((1, DV)),
                  hbm, hbm, _const_spec((1, W_BRANCH)), hbm, hbm, _const_spec((1, D_MODEL))],
        out_specs=[
            pl.BlockSpec((1, tm, D_MODEL), lambda b, t: (b, t, 0)),
            pl.BlockSpec((1, N_HEADS, DK, DV), lambda b, t: (b, 0, 0, 0)),
            pl.BlockSpec((POOL_BUF, batch, W_BRANCH), lambda b, t: (0, 0, 0)),
        ] + [hbm] * len(bf16_shapes),
        out_shape=[
            jax.ShapeDtypeStruct((batch, seq, D_MODEL), F32),
            jax.ShapeDtypeStruct((batch, N_HEADS, DK, DV), F32),
            jax.ShapeDtypeStruct((POOL_BUF, batch, W_BRANCH), F32),
        ] + [jax.ShapeDtypeStruct(s, BF16) for s in bf16_shapes],
        scratch_shapes=[
            pltpu.VMEM((tm, D_IN), F32),
            pltpu.VMEM((N_HEADS, DV, DK), F32),
            pltpu.VMEM((tm + 2 * SUBLANES, W_BRANCH), F32),
            pltpu.VMEM((N_HEADS, tm, (SUB_TILE // CHUNK) * DK), BF16),
            pltpu.VMEM((N_HEADS, tm // CHUNK, DK, DV), BF16),
            pltpu.VMEM((tm, W_BRANCH), BF16),
            pltpu.VMEM((tm, W_BRANCH), BF16),
            pltpu.VMEM((tm, W_BRANCH), BF16),
            pltpu.VMEM((tm, W_BRANCH), BF16),
            pltpu.VMEM((tm, W_BRANCH), BF16),
            pltpu.VMEM((tm, W_BRANCH), BF16),
        ] + [pltpu.VMEM(s, BF16) for s in bf16_shapes] + [
            pltpu.SemaphoreType.DMA((5 + WIN_SLOTS,)),
            pltpu.SemaphoreType.DMA((len(bf16_shapes),)),
        ],
        compiler_params=pltpu.CompilerParams(
            dimension_semantics=("arbitrary", "arbitrary"),
            vmem_limit_bytes=VMEM_LIMIT_BYTES),
        name="hgrn2_pool_prompt",
    )(x_prompt, norm_g, w_in[0], lb_logits, hgrn_norm_g, w_proj_a[0],
      w_pool[0].reshape(n_groups * G_B, G_B), pool_scale, w_proj_b[0], w_out[0],
      final_norm_g.reshape(1, D_MODEL))
    weights = (norm_g, win_b, lb_logits, hgrn_norm_g, wpa_b, wpool_b, pool_scale, wpb_b, wout_b,
               final_norm_g.reshape(1, D_MODEL))

    rows = dec_batch * DEC_SEQ
    pool_in = jnp.transpose(state_pool[0], (1, 0, 2))
    ng, win_b, lbl, hg, wpa_b, wpool_b, ps, wpb_b, wout_b, fg = weights
    pseq = PREP_SEQS
    prows = pseq * DEC_SEQ
    n_prep = dec_batch // pseq
    bb = BB_SAMPLE

    def prep_blk(s):
        return jnp.minimum((s + PREP_PARTS - 1) // PREP_PARTS, n_prep - 1)

    def finish_blk(s):
        return jnp.minimum(jnp.maximum(s - 2, 0) // PREP_PARTS, n_prep - 1)

    s_s, y_s, pool_out = pl.pallas_call(
        functools.partial(_decode_kernel, n_prep=n_prep),
        grid=(1 + dec_batch // bb,),
        in_specs=[
            pl.BlockSpec((pseq, DEC_SEQ, D_MODEL), lambda s: (prep_blk(s), 0, 0)),
            pl.BlockSpec((POOL_BUF, pseq, W_BRANCH), lambda s: (0, prep_blk(s), 0),
                         pipeline_mode=pl.Buffered(1)),
            hbm,
            _const_spec(ng.shape), _const_spec(win_b.shape), _const_spec(lbl.shape),
            _const_spec(wpool_b.shape), _const_spec(ps.shape), _const_spec(wpb_b.shape),
            _const_spec(hg.shape), _const_spec(wpa_b.shape), _const_spec(wout_b.shape),
            _const_spec(fg.shape),
        ],
        out_specs=[
            hbm,
            pl.BlockSpec((pseq, DEC_SEQ, D_MODEL), lambda s: (finish_blk(s), 0, 0)),
            pl.BlockSpec((POOL_BUF, pseq, W_BRANCH), lambda s: (0, prep_blk(s), 0)),
        ],
        out_shape=[
            jax.ShapeDtypeStruct((dec_batch, N_HEADS, DK, DV), F32),
            jax.ShapeDtypeStruct(x_sample.shape, F32),
            jax.ShapeDtypeStruct((POOL_BUF, dec_batch, W_BRANCH), F32),
        ],
        scratch_shapes=[
            pltpu.VMEM((N_HEADS, 2, rows, DK), BF16),
            pltpu.VMEM((N_HEADS, 2, rows, DV), BF16),
            pltpu.VMEM((rows, W_BRANCH), BF16),
            pltpu.VMEM((rows, W_BRANCH), F32),
            pltpu.VMEM((rows, W_BRANCH), F32),
            pltpu.VMEM((rows, D_MODEL), F32),
            pltpu.VMEM((rows, D_MODEL), F32),
            pltpu.VMEM((rows, D_MODEL), F32),
            pltpu.VMEM((prows, D_IN), F32),
            pltpu.VMEM((len(POOL_WINDOWS), prows, G_B), F32),
            pltpu.VMEM((len(POOL_WINDOWS), prows, G_B), F32),
            pltpu.VMEM((prows, W_BRANCH), BF16),
            pltpu.VMEM((prows, W_BRANCH), BF16),
            pltpu.VMEM((2, bb, N_HEADS, DK, DV), F32),
            pltpu.VMEM((2, bb, N_HEADS, DK, DV), F32),
            pltpu.SemaphoreType.DMA((2, STATE_DMAS)),
            pltpu.SemaphoreType.DMA((2, STATE_DMAS)),
        ],
        compiler_params=pltpu.CompilerParams(
            dimension_semantics=("arbitrary",), vmem_limit_bytes=DECODE_VMEM_LIMIT_BYTES),
        name="hgrn2_pool_decode",
    )(x_sample, pool_in, state_hgrn[0], ng, win_b, lbl, wpool_b, ps, wpb_b,
      hg, wpa_b, wout_b, fg)

    p_p = jnp.transpose(p_p, (1, 0, 2))
    p_s = jnp.transpose(pool_out, (1, 0, 2))
    return (y_p, y_s, s_p[None], p_p[None], s_s[None], p_s[None])
```

```python
import functools

import jax
import jax.numpy as jnp
from jax import lax
from jax.experimental import pallas as pl
from jax.experimental.pallas import tpu as pltpu

F32 = jnp.float32
BF16 = jnp.bfloat16

D_MODEL = 1024
W_BRANCH = 512
N_HEADS = 4
DK = 128
DV = 128
CHUNK = 32
POOL_WINDOWS = (2, 4, 8, 16)
G_B = 128
POOL_BUF = 15
PAST_LEN = 16384
EPS = 1e-6
D_IN = 4 * W_BRANCH + 2 * W_BRANCH + 2 * D_MODEL
OFF_Q, OFF_F, OFF_I, OFF_GA = 0, 512, 1024, 1536
OFF_U, OFF_GB, OFF_MA, OFF_MB = 2048, 2560, 3072, 4096

SUBLANES = 8
VMEM_LIMIT_BYTES = 56 * 1024 * 1024
DECODE_VMEM_LIMIT_BYTES = 60 * 1024 * 1024

TM_PROMPT = 512
WIN_SLOTS = 4
SUB_TILE = 256
PREP_SEQS = 64
BB_SAMPLE = 16
STATE_DMAS = 1
DEC_SEQ = 4
PAIR_ROWS = 2 * DEC_SEQ
PREP_PARTS = 4


def _rms(x, g):
    ms = jnp.mean(x * x, axis=-1, keepdims=True)
    return x * lax.rsqrt(ms + EPS) * g


def _lower_bound(lbl_ref):
    l0 = lbl_ref[0:1, :]
    l1 = lbl_ref[1:2, :]
    m = jnp.maximum(l0, l1)
    e0 = jnp.exp(l0 - m)
    e1 = jnp.exp(l1 - m)
    return e0 / (e0 + e1)


def _in_proj(x, ng_ref, win_ref, z_ref, c0=0, c1=D_IN):
    h = _rms(x, ng_ref[...]).astype(BF16)
    z_ref[:, c0:c1] = jnp.dot(h, win_ref[:, c0:c1], preferred_element_type=F32)


def _hgrn_inputs(z_ref, lb, hh):
    c = hh * DK
    q = z_ref[:, OFF_Q + c:OFF_Q + c + DK]
    f = z_ref[:, OFF_F + c:OFF_F + c + DK]
    v = z_ref[:, OFF_I + c:OFF_I + c + DK]
    lbh = lb[:, c:c + DK]
    fg = lbh + (1.0 - lbh) * jax.nn.sigmoid(f)
    logf = jnp.log(fg)
    k = 1.0 - fg
    qf = jax.nn.silu(q) * (DK ** -0.5)
    return qf, k, v, logf


def _group_cumsum(x, group):
    row = lax.broadcasted_iota(jnp.int32, x.shape, 0) & (group - 1)
    s = 1
    while s < group:
        x = x + jnp.where(row >= s, pltpu.roll(x, s, 0), 0.0)
        s *= 2
    return x


def _hgrn_epilogue(o, z_ref, hg_ref, hh, oa_ref):
    c = hh * DK
    ga = z_ref[:, OFF_GA + c:OFF_GA + c + DK]
    on = _rms(o, hg_ref[...]) * jax.nn.silu(ga)
    oa_ref[:, c:c + DK] = on.astype(BF16)


def _pool_epilogue(pooled, z_ref, g, wpool_ref, ps_ref, yb_ref):
    c = g * G_B
    mixed = jnp.dot(pooled.astype(BF16), wpool_ref[g], preferred_element_type=F32)
    gb = z_ref[:, OFF_GB + c:OFF_GB + c + G_B]
    yb = mixed * ps_ref[:, c:c + G_B] * jax.nn.silu(gb)
    yb_ref[:, c:c + G_B] = yb.astype(BF16)


def _weight_copies(hbm_refs, stage_ref, sem):
    win_hbm, wpa_hbm, wpool_hbm, wpb_hbm, wout_hbm = hbm_refs
    rows = stage_ref.shape[0]
    blk = rows // WIN_SLOTS
    n_small = rows // blk

    def block(k):
        slot = k % WIN_SLOTS

        def dst(c0, c1):
            return stage_ref.at[slot * blk:(slot + 1) * blk, c0:c1]
        if k < n_small:
            r0 = k * blk
            pairs = [(wpa_hbm.at[r0:r0 + blk, :], dst(0, D_MODEL)),
                     (wpb_hbm.at[r0:r0 + blk, :], dst(D_MODEL, 2 * D_MODEL)),
                     (wout_hbm.at[r0:r0 + blk, :], dst(2 * D_MODEL, 3 * D_MODEL)),
                     (wout_hbm.at[rows + r0:rows + r0 + blk, :], dst(3 * D_MODEL, 4 * D_MODEL)),
                     (wpool_hbm.at[r0:r0 + blk, :], dst(4 * D_MODEL, 4 * D_MODEL + G_B))]
        else:
            r0 = (k - n_small) * blk
            pairs = [(win_hbm.at[r0:r0 + blk, :], dst(0, D_IN))]
        return [pltpu.make_async_copy(src, d, sem.at[slot, i]) for i, (src, d) in enumerate(pairs)]
    return block, blk, n_small


def _weight_exports(vmem_refs, out_refs, sem):
    return [pltpu.make_async_copy(src, dst, sem.at[i])
            for i, (src, dst) in enumerate(zip(vmem_refs, out_refs))]


def _prompt_kernel(x_ref, ng_ref, win_hbm, lbl_ref, hg_ref, wpa_hbm, wpool_hbm, ps_ref,
                   wpb_hbm, wout_hbm, fg_ref,
                   y_ref, s_out_ref, p_out_ref, win_out, wpa_out, wpool_out, wpb_out, wout_out,
                   z_ref, st_ref, ext_ref, kx_ref, sn_ref, qs_ref, ks_ref, qd_ref, vb_ref,
                   oa_ref, yb_ref, win_ref, wpa_ref, wpool_ref, wpb_ref, wout_ref,
                   load_sem, export_sem):
    tm = TM_PROMPT
    sub = SUB_TILE
    n_sub = tm // sub
    nc = tm // CHUNK
    ncs = sub // CHUNK
    t = pl.program_id(1)
    first = t == 0
    hdr = 2 * SUBLANES
    bf16_weights = (win_ref, wpa_ref, wpool_ref, wpb_ref, wout_ref)
    exports = _weight_exports(bf16_weights, (win_out, wpa_out, wpool_out, wpb_out, wout_out),
                              export_sem)

    @pl.when((pl.program_id(0) == 0) & first)
    def _():
        block, blk, n_small = _weight_copies((win_hbm, wpa_hbm, wpool_hbm, wpb_hbm, wout_hbm),
                                             z_ref, load_sem)
        n_blk = n_small + D_MODEL // blk
        for k in range(WIN_SLOTS):
            for c in block(k):
                c.start()
        st_ref[...] = jnp.zeros_like(st_ref)
        ext_ref[...] = jnp.zeros_like(ext_ref)
        kx_ref[...] = jnp.zeros_like(kx_ref)
        for k in range(n_blk):
            for c in block(k):
                c.wait()
            slot = z_ref.at[(k % WIN_SLOTS) * blk:(k % WIN_SLOTS + 1) * blk, :]
            if k < n_small:
                r = slice(k * blk, (k + 1) * blk)
                wpa_ref[r, :] = slot[:, 0:D_MODEL].astype(BF16)
                wpb_ref[r, :] = slot[:, D_MODEL:2 * D_MODEL].astype(BF16)
                wout_ref[r, :] = slot[:, 2 * D_MODEL:3 * D_MODEL].astype(BF16)
                wout_ref[tm + k * blk:tm + (k + 1) * blk, :] = (
                    slot[:, 3 * D_MODEL:4 * D_MODEL].astype(BF16))
                wpool_ref[k] = slot[:, 4 * D_MODEL:4 * D_MODEL + G_B].astype(BF16)
            else:
                r = slice((k - n_small) * blk, (k - n_small + 1) * blk)
                win_ref[r, :] = slot[...].astype(BF16)
            if k + WIN_SLOTS < n_blk:
                for c in block(k + WIN_SLOTS):
                    c.start()
        for c in exports:
            c.start()

    @pl.when((pl.program_id(0) == pl.num_programs(0) - 1) & (t == pl.num_programs(1) - 1))
    def _():
        for c in exports:
            c.wait()

    x = x_ref[0]
    _in_proj(x, ng_ref, win_ref, z_ref)
    lb = _lower_bound(lbl_ref)

    ri = lax.broadcasted_iota(jnp.int32, (sub, sub), 0)
    ci = lax.broadcasted_iota(jnp.int32, (sub, sub), 1)
    causal = ((ri // CHUNK) == (ci // CHUNK)) & (ci <= ri)

    heads = range(N_HEADS)
    units = [(hh, s) for hh in heads for s in range(n_sub)]

    def rows(s):
        return slice(s * sub, (s + 1) * sub)

    def cols(hh):
        return slice(hh * DK, (hh + 1) * DK)

    dec = []
    for hh in heads:
        qf, k, v, logf = _hgrn_inputs(z_ref, lb, hh)
        b = _group_cumsum(logf, CHUNK)
        b3 = b.reshape(nc, CHUNK, DK)
        ref = b3[:, CHUNK // 2:CHUNK // 2 + 1, :]
        bl = b3[:, CHUNK - 1:CHUNK, :]
        q3 = qf.reshape(nc, CHUNK, DK)
        k3 = k.reshape(nc, CHUNK, DK)
        qs_ref[:, cols(hh)] = (q3 * jnp.exp(b3 - ref)).reshape(tm, DK).astype(BF16)
        ks_ref[:, cols(hh)] = (k3 * jnp.exp(ref - b3)).reshape(tm, DK).astype(BF16)
        qd_ref[:, cols(hh)] = (q3 * jnp.exp(b3)).reshape(tm, DK).astype(BF16)
        vb_ref[:, cols(hh)] = v.astype(BF16)
        kd = (k3 * jnp.exp(bl - b3)).reshape(tm, DK).astype(BF16)
        dec.append(jnp.exp(bl))
        for c in range(nc):
            cc = c % ncs
            kx_ref[hh, c * CHUNK:(c + 1) * CHUNK, cc * DK:(cc + 1) * DK] = (
                kd[c * CHUNK:(c + 1) * CHUNK, :])

    ext_ref[0:hdr, :] = jnp.where(first, 0.0, ext_ref[tm:tm + hdr, :])
    ext_ref[hdr:hdr + tm, :] = z_ref[:, OFF_U:OFF_U + W_BRANCH]
    pos1 = t * tm + lax.broadcasted_iota(jnp.int32, (tm, 1), 0) + 1
    for g, w in enumerate(POOL_WINDOWS):
        c = g * G_B
        s = ext_ref[:, c:c + G_B]
        sh = 1
        while sh < w:
            s = s + pltpu.roll(s, sh, 0)
            sh *= 2
        inv = jnp.where(pos1 >= w, 1.0 / w, 1.0 / pos1.astype(F32))
        pooled = s[hdr:, :] * inv - z_ref[:, OFF_U + c:OFF_U + c + G_B]
        _pool_epilogue(pooled, z_ref, g, wpool_ref, ps_ref, yb_ref)
    for j in range(POOL_BUF):
        r = hdr + tm - POOL_BUF + j
        p_out_ref[j, pl.ds(pl.program_id(0), 1), :] = ext_ref[r:r + 1, :]

    sc = {(hh, s): lax.dot_general(qs_ref[rows(s), cols(hh)], ks_ref[rows(s), cols(hh)],
                                   (((1,), (1,)), ((), ())), preferred_element_type=F32)
          for hh, s in units}
    ut = {(hh, s): lax.dot_general(vb_ref[rows(s), cols(hh)], kx_ref[hh, rows(s), :],
                                   (((0,), (0,)), ((), ())), preferred_element_type=F32)
          for hh, s in units}
    o = {(hh, s): jnp.dot(jnp.where(causal, sc[hh, s], 0.0).astype(BF16),
                          vb_ref[rows(s), cols(hh)], preferred_element_type=F32)
         for hh, s in units}
    for hh in heads:
        st = jnp.where(first, 0.0, st_ref[hh])
        for c in range(nc):
            sn_ref[hh, c] = st.T.astype(BF16)
            st = dec[hh][c] * st + ut[hh, c // ncs][:, (c % ncs) * DK:(c % ncs + 1) * DK]
        st_ref[hh] = st
        s_out_ref[0, hh] = st.T
    wcol = D_MODEL // N_HEADS
    y_b = []
    for hh in heads:
        o_inter = [jnp.dot(qd_ref[c * CHUNK:(c + 1) * CHUNK, cols(hh)], sn_ref[hh, c],
                           preferred_element_type=F32) for c in range(nc)]
        y_b.append(jnp.dot(yb_ref[...], wpb_ref[:, hh * wcol:(hh + 1) * wcol],
                           preferred_element_type=F32))
        o_hh = jnp.concatenate([o[hh, s] for s in range(n_sub)], axis=0)
        _hgrn_epilogue(o_hh + jnp.concatenate(o_inter, axis=0), z_ref, hg_ref, hh, oa_ref)
    gated_b = jax.nn.sigmoid(z_ref[:, OFF_MB:OFF_MB + D_MODEL]) * jnp.concatenate(y_b, axis=1)

    y_a = jnp.dot(oa_ref[...], wpa_ref[...], preferred_element_type=F32)
    merged = jax.nn.sigmoid(z_ref[:, OFF_MA:OFF_MA + D_MODEL]) * y_a + gated_b
    out = x + jnp.dot(merged.astype(BF16), wout_ref[...], preferred_element_type=F32)
    y_ref[0] = _rms(out, fg_ref[...])


def _shift_rows(x, k):
    n = x.shape[0]
    return x if k % n == 0 else pltpu.roll(x, (-k) % n, 0)


def _group_bcast(x, j, group):
    t = lax.broadcasted_iota(jnp.int32, x.shape, 0) & (group - 1)
    out = _shift_rows(x, j - (group - 1))
    for tt in range(group - 2, -1, -1):
        out = jnp.where(t == tt, _shift_rows(x, j - tt), out)
    return out


def _decode_prep_hgrn(heads, z_ref, lbl_ref, a_ref, bv_ref, qd_ref, oi_ref, ga_ref):
    rows = z_ref.shape[0]
    lb = _lower_bound(lbl_ref)
    r = lax.broadcasted_iota(jnp.int32, (rows, DK), 0)
    t = r & (DEC_SEQ - 1)
    even = (r & (PAIR_ROWS - 1)) < DEC_SEQ
    blk = SUB_TILE
    ri = lax.broadcasted_iota(jnp.int32, (blk, blk), 0)
    ci = lax.broadcasted_iota(jnp.int32, (blk, blk), 1)
    causal = ((ri // DEC_SEQ) == (ci // DEC_SEQ)) & (ci <= ri)

    for hh in heads:
        cs = slice(hh * DK, (hh + 1) * DK)
        qf, k, v, logf = _hgrn_inputs(z_ref, lb, hh)
        b = _group_cumsum(logf, DEC_SEQ)
        ref = _group_bcast(b, DEC_SEQ // 2, DEC_SEQ)
        bl = _group_bcast(b, DEC_SEQ - 1, DEC_SEQ)
        qs = (qf * jnp.exp(b - ref)).astype(BF16)
        ks = (k * jnp.exp(ref - b)).astype(BF16)
        kd = k * jnp.exp(bl - b)
        vb = v.astype(BF16)
        qd_ref[:, cs] = (qf * jnp.exp(b)).astype(BF16)
        for s in range(rows // blk):
            rs = slice(s * blk, (s + 1) * blk)
            sc = lax.dot_general(qs[rs], ks[rs], (((1,), (1,)), ((), ())),
                                 preferred_element_type=F32)
            oi_ref[rs, cs] = jnp.dot(jnp.where(causal, sc, 0.0).astype(BF16), vb[rs],
                                     preferred_element_type=F32)
        dec = jnp.exp(bl)
        d1 = dec.astype(BF16).astype(F32)
        d2 = (dec - d1).astype(BF16).astype(F32)
        d3 = (dec - d1 - d2).astype(BF16).astype(F32)
        tail = jnp.where(t == 0, d1, jnp.where(t == 1, d2, jnp.where(t == 2, d3, 0.0)))
        a_ref[hh, 0] = jnp.where(even, kd, _shift_rows(tail, -DEC_SEQ)).astype(BF16)
        a_ref[hh, 1] = jnp.where(even, _shift_rows(tail, DEC_SEQ), kd).astype(BF16)
        zero = jnp.zeros_like(vb)
        bv_ref[hh, 0] = jnp.where(even, vb, zero)
        bv_ref[hh, 1] = jnp.where(even, zero, vb)
        ga_ref[:, cs] = jax.nn.silu(z_ref[:, OFF_GA + hh * DK:OFF_GA + (hh + 1) * DK])


def _decode_prep_history(p_in_ref, rw_ref, p_out_ref):
    n_seq = p_in_ref.shape[1]
    for g, w in enumerate(POOL_WINDOWS):
        gc = slice(g * G_B, (g + 1) * G_B)
        acc = None
        suffix = {}
        for j in range(POOL_BUF - 1, -1, -1):
            e = p_in_ref[j, :, gc]
            acc = e if acc is None else acc + e
            suffix[j] = acc
        for tt in range(DEC_SEQ):
            j = POOL_BUF + 1 - w + tt
            rw_ref[g, pl.ds(tt, n_seq, stride=DEC_SEQ), :] = (
                suffix[j] if j < POOL_BUF else jnp.zeros((n_seq, G_B), F32))
    for j in range(POOL_BUF - DEC_SEQ):
        p_out_ref[j] = p_in_ref[j + DEC_SEQ]


def _decode_prep_pool(z_ref, wpool_ref, ps_ref, wpb_ref, ma_ref, gb_ref, p_out_ref,
                      rw_ref, u_ref, yb_ref):
    rows = z_ref.shape[0]
    n_seq = rows // DEC_SEQ
    t = lax.broadcasted_iota(jnp.int32, (rows, G_B), 0) & (DEC_SEQ - 1)
    for g, w in enumerate(POOL_WINDOWS):
        u = z_ref[:, OFF_U + g * G_B:OFF_U + (g + 1) * G_B]
        u_ref[g] = u
        cu = u
        sh = 1
        while sh < min(w, DEC_SEQ):
            cu = cu + jnp.where(t >= sh, _shift_rows(cu, -sh), 0.0)
            sh *= 2
        pooled = (rw_ref[g] + cu) * (1.0 / w) - u
        _pool_epilogue(pooled, z_ref, g, wpool_ref, ps_ref, yb_ref)
    gb_ref[...] = jax.nn.sigmoid(z_ref[:, OFF_MB:OFF_MB + D_MODEL]) * jnp.dot(
        yb_ref[...], wpb_ref[...], preferred_element_type=F32)
    ma_ref[...] = jax.nn.sigmoid(z_ref[:, OFF_MA:OFF_MA + D_MODEL])
    for tt in range(DEC_SEQ):
        for g in range(len(POOL_WINDOWS)):
            p_out_ref[POOL_BUF - DEC_SEQ + tt, :, g * G_B:(g + 1) * G_B] = (
                u_ref[g, pl.ds(tt, n_seq, stride=DEC_SEQ), :])


def _decode_prep(part, x_ref, p_in_ref, ng_ref, win_ref, lbl_ref, wpool_ref, ps_ref, wpb_ref,
                 a_ref, bv_ref, qd_ref, oi_ref, ga_ref, ma_ref, gb_ref, p_out_ref,
                 xs_ref, z_ref, rw_ref, u_ref, yb_ref):
    hgrn = functools.partial(_decode_prep_hgrn, z_ref=z_ref, lbl_ref=lbl_ref, a_ref=a_ref,
                             bv_ref=bv_ref, qd_ref=qd_ref, oi_ref=oi_ref, ga_ref=ga_ref)
    if part == 0:
        for i in range(x_ref.shape[0]):
            xs_ref[i * DEC_SEQ:(i + 1) * DEC_SEQ, :] = x_ref[i]
        _in_proj(xs_ref[...], ng_ref, win_ref, z_ref, 0, OFF_U)
        _decode_prep_history(p_in_ref, rw_ref, p_out_ref)
    elif part == 1:
        _in_proj(xs_ref[...], ng_ref, win_ref, z_ref, OFF_U, D_IN)
    elif part == 2:
        hgrn(range(0, N_HEADS - 1))
    else:
        hgrn(range(N_HEADS - 1, N_HEADS))
        _decode_prep_pool(z_ref, wpool_ref, ps_ref, wpb_ref, ma_ref, gb_ref, p_out_ref,
                          rw_ref, u_ref, yb_ref)


def _decode_state_step(a_ref, bv_ref, qd_ref, s_in_ref, s_out_ref, o_ref):
    n_pairs = BB_SAMPLE // 2
    r8 = lax.broadcasted_iota(jnp.int32, (PAIR_ROWS, DV), 0)
    even = r8 < DEC_SEQ
    ones = (jnp.where((r8 >= DEC_SEQ) & (r8 < DEC_SEQ + 3), 1.0, 0.0).astype(BF16),
            jnp.where(r8 < 3, 1.0, 0.0).astype(BF16))
    for hh in range(N_HEADS):
        for lp in range(n_pairs):
            rs = slice(lp * PAIR_ROWS, (lp + 1) * PAIR_ROWS)
            for e in range(2):
                rhs = jnp.concatenate([bv_ref[hh, e, rs, :], ones[e]], axis=1)
                upd = lax.dot_general(a_ref[hh, e, rs, :], rhs, (((0,), (0,)), ((), ())),
                                      preferred_element_type=F32)
                s_out_ref[2 * lp + e, hh] = (upd[:, DV:2 * DV] * s_in_ref[2 * lp + e, hh]
                                             + upd[:, 0:DV])
    for hh in range(N_HEADS):
        cs = slice(hh * DK, (hh + 1) * DK)
        for lp in range(n_pairs):
            rs = slice(lp * PAIR_ROWS, (lp + 1) * PAIR_ROWS)
            q8 = qd_ref[rs, cs]
            o0 = jnp.dot(q8, s_in_ref[2 * lp, hh].astype(BF16), preferred_element_type=F32)
            o1 = jnp.dot(q8, s_in_ref[2 * lp + 1, hh].astype(BF16), preferred_element_type=F32)
            o_ref[rs, cs] = o_ref[rs, cs] + jnp.where(even, o0, o1)


def _decode_finish(o_ref, ga_ref, ma_ref, gb_ref, xs_ref, hg_ref, wpa_ref, wout_ref,
                   fg_ref, y_ref, oa_ref, *, n_seq):
    for hh in range(N_HEADS):
        cs = slice(hh * DK, (hh + 1) * DK)
        oa_ref[:, cs] = (_rms(o_ref[:, cs], hg_ref[...]) * ga_ref[:, cs]).astype(BF16)
    y_a = jnp.dot(oa_ref[...], wpa_ref[...], preferred_element_type=F32)
    merged = ma_ref[...] * y_a + gb_ref[...]
    out = xs_ref[...] + jnp.dot(merged.astype(BF16), wout_ref[...], preferred_element_type=F32)
    y = _rms(out, fg_ref[...])
    for i in range(n_seq):
        y_ref[i] = y[i * DEC_SEQ:(i + 1) * DEC_SEQ, :]


def _state_copies(hbm_ref, buf_ref, sem, blk, slot, to_hbm):
    n = BB_SAMPLE // STATE_DMAS
    copies = []
    for c in range(STATE_DMAS):
        in_hbm = hbm_ref.at[pl.ds(blk * BB_SAMPLE + c * n, n)]
        in_vmem = buf_ref.at[slot, pl.ds(c * n, n)]
        src, dst = (in_vmem, in_hbm) if to_hbm else (in_hbm, in_vmem)
        copies.append(pltpu.make_async_copy(src, dst, sem.at[slot, c]))
    return copies


def _decode_kernel(xp_ref, p_in_ref, s_in_hbm,
                   ng_ref, win_ref, lbl_ref, wpool_ref, ps_ref, wpb_ref,
                   hg_ref, wpa_ref, wout_ref, fg_ref,
                   s_out_hbm, y_ref, p_out_ref,
                   a_all, bv_all, qd_all, o_all, ga_all, ma_all, gb_all, xs_all,
                   z_ref, rw_ref, u_ref, yb_ref, oa_ref, s_in_buf, s_out_buf, in_sem, out_sem,
                   *, n_prep):
    s = pl.program_id(0)
    prows = PREP_SEQS * DEC_SEQ
    brows = BB_SAMPLE * DEC_SEQ
    n_state = n_prep * PREP_PARTS

    def load(blk, slot):
        return _state_copies(s_in_hbm, s_in_buf, in_sem, blk, slot, to_hbm=False)

    def store(blk, slot):
        return _state_copies(s_out_hbm, s_out_buf, out_sem, blk, slot, to_hbm=True)

    def block_rows(blk):
        start = blk * prows
        return pl.ds(start if isinstance(start, int) else pl.multiple_of(start, prows), prows)

    def prep(part, blk):
        r = block_rows(blk)
        _decode_prep(part, xp_ref, p_in_ref, ng_ref, win_ref, lbl_ref, wpool_ref, ps_ref, wpb_ref,
                     a_all.at[:, :, r, :], bv_all.at[:, :, r, :], qd_all.at[r, :],
                     o_all.at[r, :], ga_all.at[r, :], ma_all.at[r, :], gb_all.at[r, :],
                     p_out_ref, xs_all.at[r, :], z_ref, rw_ref, u_ref, yb_ref)

    def finish(blk):
        f = block_rows(blk)
        _decode_finish(o_all.at[f, :], ga_all.at[f, :], ma_all.at[f, :], gb_all.at[f, :],
                       xs_all.at[f, :], hg_ref, wpa_ref, wout_ref, fg_ref, y_ref, oa_ref,
                       n_seq=PREP_SEQS)

    i = s - 1
    blk = lax.div(i, PREP_PARTS)
    part_now = lax.rem(i, PREP_PARTS)
    slot = lax.rem(i, 2)

    @pl.when(s == 0)
    def _():
        for b in range(2):
            for c in load(b, b):
                c.start()
        for part in range(PREP_PARTS):
            prep(part, 0)

    @pl.when(s > 0)
    def _():
        @pl.when((i > 0) & (i + 1 < n_state))
        def _():
            for c in load(i + 1, 1 - slot):
                c.start()
        for c in load(i, slot):
            c.wait()

        @pl.when(i >= 2)
        def _():
            for c in store(i - 2, slot):
                c.wait()
        r = pl.ds(pl.multiple_of(i * brows, brows), brows)
        _decode_state_step(a_all.at[:, :, r, :], bv_all.at[:, :, r, :], qd_all.at[r, :],
                           s_in_buf.at[slot], s_out_buf.at[slot], o_all.at[r, :])
        for c in store(i, slot):
            c.start()

    for part in range(PREP_PARTS):
        @pl.when((part_now == part) & (blk + 1 < n_prep))
        def _():
            prep(part, blk + 1)

    @pl.when((part_now == 0) & (i > 0))
    def _():
        finish(blk - 1)

    @pl.when(i == n_state - 1)
    def _():
        finish(n_prep - 1)
        for c in store(i - 1, 1 - slot) + store(i, slot):
            c.wait()


def _const_spec(shape):
    n = len(shape)
    return pl.BlockSpec(shape, lambda *_: (0,) * n, pipeline_mode=pl.Buffered(1))


def kernel(x_prompt, x_sample, state_hgrn, state_pool, norm_g, w_in, lb_logits, hgrn_norm_g,
           w_proj_a, w_pool, pool_scale, w_proj_b, w_out, final_norm_g):
    batch, seq, _ = x_prompt.shape
    dec_batch, dec_seq, _ = x_sample.shape
    assert norm_g.shape[0] == 1 and lb_logits.shape[0] == 2, "single-layer decoder only"
    assert seq % TM_PROMPT == 0 and TM_PROMPT % SUB_TILE == 0 and dec_batch % BB_SAMPLE == 0
    assert dec_seq == DEC_SEQ and PAIR_ROWS == SUBLANES and BB_SAMPLE % 2 == 0
    assert dec_batch % PREP_SEQS == 0 and (PREP_SEQS * DEC_SEQ) % SUB_TILE == 0
    assert PREP_SEQS == PREP_PARTS * BB_SAMPLE and BB_SAMPLE % STATE_DMAS == 0
    assert PAST_LEN >= max(POOL_WINDOWS)

    n_groups = len(POOL_WINDOWS)
    tm = TM_PROMPT
    nt = seq // tm
    assert w_in.shape[1:] == (2 * tm, D_IN) and w_proj_a.shape[1] == tm, "weight staging layout"
    assert tm % (WIN_SLOTS * 2 * SUBLANES) == 0, "bf16-tile-aligned w_in staging blocks"
    assert tm // WIN_SLOTS == G_B, "a staging block of the small weights carries one w_pool group"
    hbm = pl.BlockSpec(memory_space=pl.ANY)
    bf16_shapes = [(D_MODEL, D_IN), (W_BRANCH, D_MODEL), (n_groups, G_B, G_B),
                   (W_BRANCH, D_MODEL), (D_MODEL, D_MODEL)]
    y_p, s_p, p_p, win_b, wpa_b, wpool_b, wpb_b, wout_b = pl.pallas_call(
        _prompt_kernel,
        grid=(batch, nt),
        in_specs=[pl.BlockSpec((1, tm, D_MODEL), lambda b, t: (b, t, 0)),
                  _const_spec((1, D_MODEL)), hbm, _const_spec((2, W_BRANCH)), _const_spec((1, DV)),
                  hbm, hbm, _const_spec((1, W_BRANCH)), hbm, hbm, _const_spec((1, D_MODEL))],
        out_specs=[
            pl.BlockSpec((1, tm, D_MODEL), lambda b, t: (b, t, 0)),
            pl.BlockSpec((1, N_HEADS, DK, DV), lambda b, t: (b, 0, 0, 0)),
            pl.BlockSpec((POOL_BUF, batch, W_BRANCH), lambda b, t: (0, 0, 0)),
        ] + [hbm] * len(bf16_shapes),
        out_shape=[
            jax.ShapeDtypeStruct((batch, seq, D_MODEL), F32),
            jax.ShapeDtypeStruct((batch, N_HEADS, DK, DV), F32),
            jax.ShapeDtypeStruct((POOL_BUF, batch, W_BRANCH), F32),
        ] + [jax.ShapeDtypeStruct(s, BF16) for s in bf16_shapes],
        scratch_shapes=[
            pltpu.VMEM((tm, D_IN), F32),
            pltpu.VMEM((N_HEADS, DV, DK), F32),
            pltpu.VMEM((tm + 2 * SUBLANES, W_BRANCH), F32),
            pltpu.VMEM((N_HEADS, tm, (SUB_TILE // CHUNK) * DK), BF16),
            pltpu.VMEM((N_HEADS, tm // CHUNK, DK, DV), BF16),
            pltpu.VMEM((tm, W_BRANCH), BF16),
            pltpu.VMEM((tm, W_BRANCH), BF16),
            pltpu.VMEM((tm, W_BRANCH), BF16),
            pltpu.VMEM((tm, W_BRANCH), BF16),
            pltpu.VMEM((tm, W_BRANCH), BF16),
            pltpu.VMEM((tm, W_BRANCH), BF16),
        ] + [pltpu.VMEM(s, BF16) for s in bf16_shapes] + [
            pltpu.SemaphoreType.DMA((WIN_SLOTS, 5)),
            pltpu.SemaphoreType.DMA((len(bf16_shapes),)),
        ],
        compiler_params=pltpu.CompilerParams(
            dimension_semantics=("arbitrary", "arbitrary"),
            vmem_limit_bytes=VMEM_LIMIT_BYTES),
        name="hgrn2_pool_prompt",
    )(x_prompt, norm_g, w_in[0], lb_logits, hgrn_norm_g, w_proj_a[0],
      w_pool[0].reshape(n_groups * G_B, G_B), pool_scale, w_proj_b[0], w_out[0],
      final_norm_g.reshape(1, D_MODEL))
    weights = (norm_g, win_b, lb_logits, hgrn_norm_g, wpa_b, wpool_b, pool_scale, wpb_b, wout_b,
               final_norm_g.reshape(1, D_MODEL))

    rows = dec_batch * DEC_SEQ
    pool_in = jnp.transpose(state_pool[0], (1, 0, 2))
    ng, win_b, lbl, hg, wpa_b, wpool_b, ps, wpb_b, wout_b, fg = weights
    pseq = PREP_SEQS
    prows = pseq * DEC_SEQ
    n_prep = dec_batch // pseq
    bb = BB_SAMPLE

    def prep_blk(s):
        return jnp.minimum((s + PREP_PARTS - 1) // PREP_PARTS, n_prep - 1)

    def finish_blk(s):
        return jnp.minimum(jnp.maximum(s - 2, 0) // PREP_PARTS, n_prep - 1)

    s_s, y_s, pool_out = pl.pallas_call(
        functools.partial(_decode_kernel, n_prep=n_prep),
        grid=(1 + dec_batch // bb,),
        in_specs=[
            pl.BlockSpec((pseq, DEC_SEQ, D_MODEL), lambda s: (prep_blk(s), 0, 0)),
            pl.BlockSpec((POOL_BUF, pseq, W_BRANCH), lambda s: (0, prep_blk(s), 0),
                         pipeline_mode=pl.Buffered(1)),
            hbm,
            _const_spec(ng.shape), _const_spec(win_b.shape), _const_spec(lbl.shape),
            _const_spec(wpool_b.shape), _const_spec(ps.shape), _const_spec(wpb_b.shape),
            _const_spec(hg.shape), _const_spec(wpa_b.shape), _const_spec(wout_b.shape),
            _const_spec(fg.shape),
        ],
        out_specs=[
            hbm,
            pl.BlockSpec((pseq, DEC_SEQ, D_MODEL), lambda s: (finish_blk(s), 0, 0)),
            pl.BlockSpec((POOL_BUF, pseq, W_BRANCH), lambda s: (0, prep_blk(s), 0)),
        ],
        out_shape=[
            jax.ShapeDtypeStruct((dec_batch, N_HEADS, DK, DV), F32),
            jax.ShapeDtypeStruct(x_sample.shape, F32),
            jax.ShapeDtypeStruct((POOL_BUF, dec_batch, W_BRANCH), F32),
        ],
        scratch_shapes=[
            pltpu.VMEM((N_HEADS, 2, rows, DK), BF16),
            pltpu.VMEM((N_HEADS, 2, rows, DV), BF16),
            pltpu.VMEM((rows, W_BRANCH), BF16),
            pltpu.VMEM((rows, W_BRANCH), F32),
            pltpu.VMEM((rows, W_BRANCH), F32),
            pltpu.VMEM((rows, D_MODEL), F32),
            pltpu.VMEM((rows, D_MODEL), F32),
            pltpu.VMEM((rows, D_MODEL), F32),
            pltpu.VMEM((prows, D_IN), F32),
            pltpu.VMEM((len(POOL_WINDOWS), prows, G_B), F32),
            pltpu.VMEM((len(POOL_WINDOWS), prows, G_B), F32),
            pltpu.VMEM((prows, W_BRANCH), BF16),
            pltpu.VMEM((prows, W_BRANCH), BF16),
            pltpu.VMEM((2, bb, N_HEADS, DK, DV), F32),
            pltpu.VMEM((2, bb, N_HEADS, DK, DV), F32),
            pltpu.SemaphoreType.DMA((2, STATE_DMAS)),
            pltpu.SemaphoreType.DMA((2, STATE_DMAS)),
        ],
        compiler_params=pltpu.CompilerParams(
            dimension_semantics=("arbitrary",), vmem_limit_bytes=DECODE_VMEM_LIMIT_BYTES),
        name="hgrn2_pool_decode",
    )(x_sample, pool_in, state_hgrn[0], ng, win_b, lbl, wpool_b, ps, wpb_b,
      hg, wpa_b, wout_b, fg)

    p_p = jnp.transpose(p_p, (1, 0, 2))
    p_s = jnp.transpose(pool_out, (1, 0, 2))
    return (y_p, y_s, s_p[None], p_p[None], s_s[None], p_s[None])
```

```python
import functools

import jax
import jax.numpy as jnp
from jax import lax
from jax.experimental import pallas as pl
from jax.experimental.pallas import tpu as pltpu

F32 = jnp.float32
BF16 = jnp.bfloat16

D_MODEL = 1024
W_BRANCH = 512
N_HEADS = 4
DK = 128
DV = 128
CHUNK = 32
POOL_WINDOWS = (2, 4, 8, 16)
G_B = 128
POOL_BUF = 15
PAST_LEN = 16384
EPS = 1e-6
D_IN = 4 * W_BRANCH + 2 * W_BRANCH + 2 * D_MODEL
OFF_Q, OFF_F, OFF_I, OFF_GA = 0, 512, 1024, 1536
OFF_U, OFF_GB, OFF_MA, OFF_MB = 2048, 2560, 3072, 4096

SUBLANES = 8
VMEM_LIMIT_BYTES = 56 * 1024 * 1024
DECODE_VMEM_LIMIT_BYTES = 60 * 1024 * 1024

TM_PROMPT = 512
WIN_SLOTS = 4
SUB_TILE = 256
PREP_SEQS = 64
BB_SAMPLE = 16
STATE_DMAS = 2
DEC_SEQ = 4
PAIR_ROWS = 2 * DEC_SEQ
PREP_PARTS = 4


def _rms(x, g):
    ms = jnp.mean(x * x, axis=-1, keepdims=True)
    return x * lax.rsqrt(ms + EPS) * g


def _lower_bound(lbl_ref):
    l0 = lbl_ref[0:1, :]
    l1 = lbl_ref[1:2, :]
    m = jnp.maximum(l0, l1)
    e0 = jnp.exp(l0 - m)
    e1 = jnp.exp(l1 - m)
    return e0 / (e0 + e1)


def _in_proj(x, ng_ref, win_ref, z_ref, c0=0, c1=D_IN):
    h = _rms(x, ng_ref[...]).astype(BF16)
    z_ref[:, c0:c1] = jnp.dot(h, win_ref[:, c0:c1], preferred_element_type=F32)


def _hgrn_inputs(z_ref, lb, hh):
    c = hh * DK
    q = z_ref[:, OFF_Q + c:OFF_Q + c + DK]
    f = z_ref[:, OFF_F + c:OFF_F + c + DK]
    v = z_ref[:, OFF_I + c:OFF_I + c + DK]
    lbh = lb[:, c:c + DK]
    fg = lbh + (1.0 - lbh) * jax.nn.sigmoid(f)
    logf = jnp.log(fg)
    k = 1.0 - fg
    qf = jax.nn.silu(q) * (DK ** -0.5)
    return qf, k, v, logf


def _group_cumsum(x, group):
    row = lax.broadcasted_iota(jnp.int32, x.shape, 0) & (group - 1)
    s = 1
    while s < group:
        x = x + jnp.where(row >= s, pltpu.roll(x, s, 0), 0.0)
        s *= 2
    return x


def _hgrn_epilogue(o, z_ref, hg_ref, hh, oa_ref):
    c = hh * DK
    ga = z_ref[:, OFF_GA + c:OFF_GA + c + DK]
    on = _rms(o, hg_ref[...]) * jax.nn.silu(ga)
    oa_ref[:, c:c + DK] = on.astype(BF16)


def _pool_epilogue(pooled, z_ref, g, wpool_ref, ps_ref, yb_ref):
    c = g * G_B
    mixed = jnp.dot(pooled.astype(BF16), wpool_ref[g], preferred_element_type=F32)
    gb = z_ref[:, OFF_GB + c:OFF_GB + c + G_B]
    yb = mixed * ps_ref[:, c:c + G_B] * jax.nn.silu(gb)
    yb_ref[:, c:c + G_B] = yb.astype(BF16)


def _weight_copies(hbm_refs, stage_ref, sem):
    win_hbm, wpa_hbm, wpool_hbm, wpb_hbm, wout_hbm = hbm_refs
    rows = stage_ref.shape[0]
    blk = rows // WIN_SLOTS
    n_small = rows // blk

    def block(k):
        slot = k % WIN_SLOTS

        def dst(c0, c1):
            return stage_ref.at[slot * blk:(slot + 1) * blk, c0:c1]
        if k < n_small:
            r0 = k * blk
            pairs = [(wpa_hbm.at[r0:r0 + blk, :], dst(0, D_MODEL)),
                     (wpb_hbm.at[r0:r0 + blk, :], dst(D_MODEL, 2 * D_MODEL)),
                     (wout_hbm.at[r0:r0 + blk, :], dst(2 * D_MODEL, 3 * D_MODEL)),
                     (wout_hbm.at[rows + r0:rows + r0 + blk, :], dst(3 * D_MODEL, 4 * D_MODEL)),
                     (wpool_hbm.at[r0:r0 + blk, :], dst(4 * D_MODEL, 4 * D_MODEL + G_B))]
        else:
            r0 = (k - n_small) * blk
            pairs = [(win_hbm.at[r0:r0 + blk, :], dst(0, D_IN))]
        return [pltpu.make_async_copy(src, d, sem.at[slot, i]) for i, (src, d) in enumerate(pairs)]
    return block, blk, n_small


def _weight_exports(vmem_refs, out_refs, sem):
    return [pltpu.make_async_copy(src, dst, sem.at[i])
            for i, (src, dst) in enumerate(zip(vmem_refs, out_refs))]


def _prompt_kernel(x_ref, ng_ref, win_hbm, lbl_ref, hg_ref, wpa_hbm, wpool_hbm, ps_ref,
                   wpb_hbm, wout_hbm, fg_ref,
                   y_ref, s_out_ref, p_out_ref, win_out, wpa_out, wpool_out, wpb_out, wout_out,
                   z_ref, st_ref, ext_ref, kx_ref, sn_ref, qs_ref, ks_ref, qd_ref, vb_ref,
                   oa_ref, yb_ref, win_ref, wpa_ref, wpool_ref, wpb_ref, wout_ref,
                   load_sem, export_sem):
    tm = TM_PROMPT
    sub = SUB_TILE
    n_sub = tm // sub
    nc = tm // CHUNK
    ncs = sub // CHUNK
    t = pl.program_id(1)
    first = t == 0
    hdr = 2 * SUBLANES
    bf16_weights = (win_ref, wpa_ref, wpool_ref, wpb_ref, wout_ref)
    exports = _weight_exports(bf16_weights, (win_out, wpa_out, wpool_out, wpb_out, wout_out),
                              export_sem)

    @pl.when((pl.program_id(0) == 0) & first)
    def _():
        block, blk, n_small = _weight_copies((win_hbm, wpa_hbm, wpool_hbm, wpb_hbm, wout_hbm),
                                             z_ref, load_sem)
        n_blk = n_small + D_MODEL // blk
        for k in range(WIN_SLOTS):
            for c in block(k):
                c.start()
        st_ref[...] = jnp.zeros_like(st_ref)
        ext_ref[...] = jnp.zeros_like(ext_ref)
        kx_ref[...] = jnp.zeros_like(kx_ref)
        for k in range(n_blk):
            for c in block(k):
                c.wait()
            slot = z_ref.at[(k % WIN_SLOTS) * blk:(k % WIN_SLOTS + 1) * blk, :]
            if k < n_small:
                r = slice(k * blk, (k + 1) * blk)
                wpa_ref[r, :] = slot[:, 0:D_MODEL].astype(BF16)
                wpb_ref[r, :] = slot[:, D_MODEL:2 * D_MODEL].astype(BF16)
                wout_ref[r, :] = slot[:, 2 * D_MODEL:3 * D_MODEL].astype(BF16)
                wout_ref[tm + k * blk:tm + (k + 1) * blk, :] = (
                    slot[:, 3 * D_MODEL:4 * D_MODEL].astype(BF16))
                wpool_ref[k] = slot[:, 4 * D_MODEL:4 * D_MODEL + G_B].astype(BF16)
            else:
                r = slice((k - n_small) * blk, (k - n_small + 1) * blk)
                win_ref[r, :] = slot[...].astype(BF16)
            if k + WIN_SLOTS < n_blk:
                for c in block(k + WIN_SLOTS):
                    c.start()
        for c in exports:
            c.start()

    @pl.when((pl.program_id(0) == pl.num_programs(0) - 1) & (t == pl.num_programs(1) - 1))
    def _():
        for c in exports:
            c.wait()

    x = x_ref[0]
    _in_proj(x, ng_ref, win_ref, z_ref)
    lb = _lower_bound(lbl_ref)

    ri = lax.broadcasted_iota(jnp.int32, (sub, sub), 0)
    ci = lax.broadcasted_iota(jnp.int32, (sub, sub), 1)
    causal = ((ri // CHUNK) == (ci // CHUNK)) & (ci <= ri)

    heads = range(N_HEADS)
    units = [(hh, s) for hh in heads for s in range(n_sub)]

    def rows(s):
        return slice(s * sub, (s + 1) * sub)

    def cols(hh):
        return slice(hh * DK, (hh + 1) * DK)

    dec = []
    for hh in heads:
        qf, k, v, logf = _hgrn_inputs(z_ref, lb, hh)
        b = _group_cumsum(logf, CHUNK)
        b3 = b.reshape(nc, CHUNK, DK)
        ref = b3[:, CHUNK // 2:CHUNK // 2 + 1, :]
        bl = b3[:, CHUNK - 1:CHUNK, :]
        q3 = qf.reshape(nc, CHUNK, DK)
        k3 = k.reshape(nc, CHUNK, DK)
        qs_ref[:, cols(hh)] = (q3 * jnp.exp(b3 - ref)).reshape(tm, DK).astype(BF16)
        ks_ref[:, cols(hh)] = (k3 * jnp.exp(ref - b3)).reshape(tm, DK).astype(BF16)
        qd_ref[:, cols(hh)] = (q3 * jnp.exp(b3)).reshape(tm, DK).astype(BF16)
        vb_ref[:, cols(hh)] = v.astype(BF16)
        kd = (k3 * jnp.exp(bl - b3)).reshape(tm, DK).astype(BF16)
        dec.append(jnp.exp(bl))
        for c in range(nc):
            cc = c % ncs
            kx_ref[hh, c * CHUNK:(c + 1) * CHUNK, cc * DK:(cc + 1) * DK] = (
                kd[c * CHUNK:(c + 1) * CHUNK, :])

    ext_ref[0:hdr, :] = jnp.where(first, 0.0, ext_ref[tm:tm + hdr, :])
    ext_ref[hdr:hdr + tm, :] = z_ref[:, OFF_U:OFF_U + W_BRANCH]
    pos1 = t * tm + lax.broadcasted_iota(jnp.int32, (tm, 1), 0) + 1
    for g, w in enumerate(POOL_WINDOWS):
        c = g * G_B
        s = ext_ref[:, c:c + G_B]
        sh = 1
        while sh < w:
            s = s + pltpu.roll(s, sh, 0)
            sh *= 2
        inv = jnp.where(pos1 >= w, 1.0 / w, 1.0 / pos1.astype(F32))
        pooled = s[hdr:, :] * inv - z_ref[:, OFF_U + c:OFF_U + c + G_B]
        _pool_epilogue(pooled, z_ref, g, wpool_ref, ps_ref, yb_ref)
    for j in range(POOL_BUF):
        r = hdr + tm - POOL_BUF + j
        p_out_ref[j, pl.ds(pl.program_id(0), 1), :] = ext_ref[r:r + 1, :]

    sc = {(hh, s): lax.dot_general(qs_ref[rows(s), cols(hh)], ks_ref[rows(s), cols(hh)],
                                   (((1,), (1,)), ((), ())), preferred_element_type=F32)
          for hh, s in units}
    ut = {(hh, s): lax.dot_general(vb_ref[rows(s), cols(hh)], kx_ref[hh, rows(s), :],
                                   (((0,), (0,)), ((), ())), preferred_element_type=F32)
          for hh, s in units}
    o = {(hh, s): jnp.dot(jnp.where(causal, sc[hh, s], 0.0).astype(BF16),
                          vb_ref[rows(s), cols(hh)], preferred_element_type=F32)
         for hh, s in units}
    for hh in heads:
        st = jnp.where(first, 0.0, st_ref[hh])
        for c in range(nc):
            sn_ref[hh, c] = st.T.astype(BF16)
            st = dec[hh][c] * st + ut[hh, c // ncs][:, (c % ncs) * DK:(c % ncs + 1) * DK]
        st_ref[hh] = st
        s_out_ref[0, hh] = st.T
    wcol = D_MODEL // N_HEADS
    y_b = []
    for hh in heads:
        o_inter = [jnp.dot(qd_ref[c * CHUNK:(c + 1) * CHUNK, cols(hh)], sn_ref[hh, c],
                           preferred_element_type=F32) for c in range(nc)]
        y_b.append(jnp.dot(yb_ref[...], wpb_ref[:, hh * wcol:(hh + 1) * wcol],
                           preferred_element_type=F32))
        o_hh = jnp.concatenate([o[hh, s] for s in range(n_sub)], axis=0)
        _hgrn_epilogue(o_hh + jnp.concatenate(o_inter, axis=0), z_ref, hg_ref, hh, oa_ref)
    gated_b = jax.nn.sigmoid(z_ref[:, OFF_MB:OFF_MB + D_MODEL]) * jnp.concatenate(y_b, axis=1)

    y_a = jnp.dot(oa_ref[...], wpa_ref[...], preferred_element_type=F32)
    merged = jax.nn.sigmoid(z_ref[:, OFF_MA:OFF_MA + D_MODEL]) * y_a + gated_b
    out = x + jnp.dot(merged.astype(BF16), wout_ref[...], preferred_element_type=F32)
    y_ref[0] = _rms(out, fg_ref[...])


def _shift_rows(x, k):
    n = x.shape[0]
    return x if k % n == 0 else pltpu.roll(x, (-k) % n, 0)


def _group_bcast(x, j, group):
    t = lax.broadcasted_iota(jnp.int32, x.shape, 0) & (group - 1)
    out = _shift_rows(x, j - (group - 1))
    for tt in range(group - 2, -1, -1):
        out = jnp.where(t == tt, _shift_rows(x, j - tt), out)
    return out


def _decode_prep_hgrn(heads, z_ref, lbl_ref, a_ref, bv_ref, qd_ref, oi_ref, ga_ref):
    rows = z_ref.shape[0]
    lb = _lower_bound(lbl_ref)
    r = lax.broadcasted_iota(jnp.int32, (rows, DK), 0)
    t = r & (DEC_SEQ - 1)
    even = (r & (PAIR_ROWS - 1)) < DEC_SEQ
    blk = SUB_TILE
    ri = lax.broadcasted_iota(jnp.int32, (blk, blk), 0)
    ci = lax.broadcasted_iota(jnp.int32, (blk, blk), 1)
    causal = ((ri // DEC_SEQ) == (ci // DEC_SEQ)) & (ci <= ri)

    for hh in heads:
        cs = slice(hh * DK, (hh + 1) * DK)
        qf, k, v, logf = _hgrn_inputs(z_ref, lb, hh)
        b = _group_cumsum(logf, DEC_SEQ)
        ref = _group_bcast(b, DEC_SEQ // 2, DEC_SEQ)
        bl = _group_bcast(b, DEC_SEQ - 1, DEC_SEQ)
        qs = (qf * jnp.exp(b - ref)).astype(BF16)
        ks = (k * jnp.exp(ref - b)).astype(BF16)
        kd = k * jnp.exp(bl - b)
        vb = v.astype(BF16)
        qd_ref[:, cs] = (qf * jnp.exp(b)).astype(BF16)
        for s in range(rows // blk):
            rs = slice(s * blk, (s + 1) * blk)
            sc = lax.dot_general(qs[rs], ks[rs], (((1,), (1,)), ((), ())),
                                 preferred_element_type=F32)
            oi_ref[rs, cs] = jnp.dot(jnp.where(causal, sc, 0.0).astype(BF16), vb[rs],
                                     preferred_element_type=F32)
        dec = jnp.exp(bl)
        d1 = dec.astype(BF16).astype(F32)
        d2 = (dec - d1).astype(BF16).astype(F32)
        d3 = (dec - d1 - d2).astype(BF16).astype(F32)
        tail = jnp.where(t == 0, d1, jnp.where(t == 1, d2, jnp.where(t == 2, d3, 0.0)))
        a_ref[hh, 0] = jnp.where(even, kd, _shift_rows(tail, -DEC_SEQ)).astype(BF16)
        a_ref[hh, 1] = jnp.where(even, _shift_rows(tail, DEC_SEQ), kd).astype(BF16)
        zero = jnp.zeros_like(vb)
        bv_ref[hh, 0] = jnp.where(even, vb, zero)
        bv_ref[hh, 1] = jnp.where(even, zero, vb)
        ga_ref[:, cs] = jax.nn.silu(z_ref[:, OFF_GA + hh * DK:OFF_GA + (hh + 1) * DK])


def _decode_prep_history(p_in_ref, rw_ref, p_out_ref):
    n_seq = p_in_ref.shape[1]
    for g, w in enumerate(POOL_WINDOWS):
        gc = slice(g * G_B, (g + 1) * G_B)
        acc = None
        suffix = {}
        for j in range(POOL_BUF - 1, -1, -1):
            e = p_in_ref[j, :, gc]
            acc = e if acc is None else acc + e
            suffix[j] = acc
        for tt in range(DEC_SEQ):
            j = POOL_BUF + 1 - w + tt
            rw_ref[g, pl.ds(tt, n_seq, stride=DEC_SEQ), :] = (
                suffix[j] if j < POOL_BUF else jnp.zeros((n_seq, G_B), F32))
    for j in range(POOL_BUF - DEC_SEQ):
        p_out_ref[j] = p_in_ref[j + DEC_SEQ]


def _decode_prep_pool(z_ref, wpool_ref, ps_ref, wpb_ref, ma_ref, gb_ref, p_out_ref,
                      rw_ref, u_ref, yb_ref):
    rows = z_ref.shape[0]
    n_seq = rows // DEC_SEQ
    t = lax.broadcasted_iota(jnp.int32, (rows, G_B), 0) & (DEC_SEQ - 1)
    for g, w in enumerate(POOL_WINDOWS):
        u = z_ref[:, OFF_U + g * G_B:OFF_U + (g + 1) * G_B]
        u_ref[g] = u
        cu = u
        sh = 1
        while sh < min(w, DEC_SEQ):
            cu = cu + jnp.where(t >= sh, _shift_rows(cu, -sh), 0.0)
            sh *= 2
        pooled = (rw_ref[g] + cu) * (1.0 / w) - u
        _pool_epilogue(pooled, z_ref, g, wpool_ref, ps_ref, yb_ref)
    gb_ref[...] = jax.nn.sigmoid(z_ref[:, OFF_MB:OFF_MB + D_MODEL]) * jnp.dot(
        yb_ref[...], wpb_ref[...], preferred_element_type=F32)
    ma_ref[...] = jax.nn.sigmoid(z_ref[:, OFF_MA:OFF_MA + D_MODEL])
    for tt in range(DEC_SEQ):
        for g in range(len(POOL_WINDOWS)):
            p_out_ref[POOL_BUF - DEC_SEQ + tt, :, g * G_B:(g + 1) * G_B] = (
                u_ref[g, pl.ds(tt, n_seq, stride=DEC_SEQ), :])


def _decode_prep(part, x_ref, p_in_ref, ng_ref, win_ref, lbl_ref, wpool_ref, ps_ref, wpb_ref,
                 a_ref, bv_ref, qd_ref, oi_ref, ga_ref, ma_ref, gb_ref, p_out_ref,
                 xs_ref, z_ref, rw_ref, u_ref, yb_ref):
    hgrn = functools.partial(_decode_prep_hgrn, z_ref=z_ref, lbl_ref=lbl_ref, a_ref=a_ref,
                             bv_ref=bv_ref, qd_ref=qd_ref, oi_ref=oi_ref, ga_ref=ga_ref)
    if part == 0:
        for i in range(x_ref.shape[0]):
            xs_ref[i * DEC_SEQ:(i + 1) * DEC_SEQ, :] = x_ref[i]
        _in_proj(xs_ref[...], ng_ref, win_ref, z_ref, 0, OFF_U)
        _decode_prep_history(p_in_ref, rw_ref, p_out_ref)
    elif part == 1:
        _in_proj(xs_ref[...], ng_ref, win_ref, z_ref, OFF_U, D_IN)
    elif part == 2:
        hgrn(range(0, N_HEADS - 1))
    else:
        hgrn(range(N_HEADS - 1, N_HEADS))
        _decode_prep_pool(z_ref, wpool_ref, ps_ref, wpb_ref, ma_ref, gb_ref, p_out_ref,
                          rw_ref, u_ref, yb_ref)


def _decode_state_step(a_ref, bv_ref, qd_ref, s_in_ref, s_out_ref, o_ref, stores):
    n_pairs = BB_SAMPLE // 2
    r8 = lax.broadcasted_iota(jnp.int32, (PAIR_ROWS, DV), 0)
    even = r8 < DEC_SEQ
    ones = (jnp.where((r8 >= DEC_SEQ) & (r8 < DEC_SEQ + 3), 1.0, 0.0).astype(BF16),
            jnp.where(r8 < 3, 1.0, 0.0).astype(BF16))
    pairs_per_store = n_pairs // len(stores)
    for g, store in enumerate(stores):
        for hh in range(N_HEADS):
            for lp in range(g * pairs_per_store, (g + 1) * pairs_per_store):
                rs = slice(lp * PAIR_ROWS, (lp + 1) * PAIR_ROWS)
                for e in range(2):
                    rhs = jnp.concatenate([bv_ref[hh, e, rs, :], ones[e]], axis=1)
                    upd = lax.dot_general(a_ref[hh, e, rs, :], rhs, (((0,), (0,)), ((), ())),
                                          preferred_element_type=F32)
                    s_out_ref[2 * lp + e, hh] = (upd[:, DV:2 * DV] * s_in_ref[2 * lp + e, hh]
                                                 + upd[:, 0:DV])
        store.start()
    for hh in range(N_HEADS):
        cs = slice(hh * DK, (hh + 1) * DK)
        for lp in range(n_pairs):
            rs = slice(lp * PAIR_ROWS, (lp + 1) * PAIR_ROWS)
            q8 = qd_ref[rs, cs]
            o0 = jnp.dot(q8, s_in_ref[2 * lp, hh].astype(BF16), preferred_element_type=F32)
            o1 = jnp.dot(q8, s_in_ref[2 * lp + 1, hh].astype(BF16), preferred_element_type=F32)
            o_ref[rs, cs] = o_ref[rs, cs] + jnp.where(even, o0, o1)


def _decode_finish(o_ref, ga_ref, ma_ref, gb_ref, xs_ref, hg_ref, wpa_ref, wout_ref,
                   fg_ref, y_ref, oa_ref, *, n_seq):
    for hh in range(N_HEADS):
        cs = slice(hh * DK, (hh + 1) * DK)
        oa_ref[:, cs] = (_rms(o_ref[:, cs], hg_ref[...]) * ga_ref[:, cs]).astype(BF16)
    y_a = jnp.dot(oa_ref[...], wpa_ref[...], preferred_element_type=F32)
    merged = ma_ref[...] * y_a + gb_ref[...]
    out = xs_ref[...] + jnp.dot(merged.astype(BF16), wout_ref[...], preferred_element_type=F32)
    y = _rms(out, fg_ref[...])
    for i in range(n_seq):
        y_ref[i] = y[i * DEC_SEQ:(i + 1) * DEC_SEQ, :]


def _state_copies(hbm_ref, buf_ref, sem, blk, slot, to_hbm):
    n = BB_SAMPLE // STATE_DMAS
    copies = []
    for c in range(STATE_DMAS):
        in_hbm = hbm_ref.at[pl.ds(blk * BB_SAMPLE + c * n, n)]
        in_vmem = buf_ref.at[slot, pl.ds(c * n, n)]
        src, dst = (in_vmem, in_hbm) if to_hbm else (in_hbm, in_vmem)
        copies.append(pltpu.make_async_copy(src, dst, sem.at[slot, c]))
    return copies


def _decode_kernel(xp_ref, p_in_ref, s_in_hbm,
                   ng_ref, win_ref, lbl_ref, wpool_ref, ps_ref, wpb_ref,
                   hg_ref, wpa_ref, wout_ref, fg_ref,
                   s_out_hbm, y_ref, p_out_ref,
                   a_all, bv_all, qd_all, o_all, ga_all, ma_all, gb_all, xs_all,
                   z_ref, rw_ref, u_ref, yb_ref, oa_ref, s_in_buf, s_out_buf, in_sem, out_sem,
                   *, n_prep):
    s = pl.program_id(0)
    prows = PREP_SEQS * DEC_SEQ
    brows = BB_SAMPLE * DEC_SEQ
    n_state = n_prep * PREP_PARTS

    def load(blk, slot):
        return _state_copies(s_in_hbm, s_in_buf, in_sem, blk, slot, to_hbm=False)

    def store(blk, slot):
        return _state_copies(s_out_hbm, s_out_buf, out_sem, blk, slot, to_hbm=True)

    def block_rows(blk):
        start = blk * prows
        return pl.ds(start if isinstance(start, int) else pl.multiple_of(start, prows), prows)

    def prep(part, blk):
        r = block_rows(blk)
        _decode_prep(part, xp_ref, p_in_ref, ng_ref, win_ref, lbl_ref, wpool_ref, ps_ref, wpb_ref,
                     a_all.at[:, :, r, :], bv_all.at[:, :, r, :], qd_all.at[r, :],
                     o_all.at[r, :], ga_all.at[r, :], ma_all.at[r, :], gb_all.at[r, :],
                     p_out_ref, xs_all.at[r, :], z_ref, rw_ref, u_ref, yb_ref)

    def finish(blk):
        f = block_rows(blk)
        _decode_finish(o_all.at[f, :], ga_all.at[f, :], ma_all.at[f, :], gb_all.at[f, :],
                       xs_all.at[f, :], hg_ref, wpa_ref, wout_ref, fg_ref, y_ref, oa_ref,
                       n_seq=PREP_SEQS)

    i = s - 1
    blk = lax.div(i, PREP_PARTS)
    part_now = lax.rem(i, PREP_PARTS)
    slot = lax.rem(i, 2)

    @pl.when(s == 0)
    def _():
        for b in range(2):
            for c in load(b, b):
                c.start()
        for part in range(PREP_PARTS):
            prep(part, 0)

    @pl.when(s > 0)
    def _():
        @pl.when((i > 0) & (i + 1 < n_state))
        def _():
            for c in load(i + 1, 1 - slot):
                c.start()
        for c in load(i, slot):
            c.wait()

        @pl.when(i >= 2)
        def _():
            for c in store(i - 2, slot):
                c.wait()
        r = pl.ds(pl.multiple_of(i * brows, brows), brows)
        _decode_state_step(a_all.at[:, :, r, :], bv_all.at[:, :, r, :], qd_all.at[r, :],
                           s_in_buf.at[slot], s_out_buf.at[slot], o_all.at[r, :],
                           stores=store(i, slot))

    for part in range(PREP_PARTS):
        @pl.when((part_now == part) & (blk + 1 < n_prep))
        def _():
            prep(part, blk + 1)

    @pl.when((part_now == 0) & (i > 0))
    def _():
        finish(blk - 1)

    @pl.when(i == n_state - 1)
    def _():
        finish(n_prep - 1)
        for c in store(i - 1, 1 - slot) + store(i, slot):
            c.wait()


def _const_spec(shape):
    n = len(shape)
    return pl.BlockSpec(shape, lambda *_: (0,) * n, pipeline_mode=pl.Buffered(1))


def kernel(x_prompt, x_sample, state_hgrn, state_pool, norm_g, w_in, lb_logits, hgrn_norm_g,
           w_proj_a, w_pool, pool_scale, w_proj_b, w_out, final_norm_g):
    batch, seq, _ = x_prompt.shape
    dec_batch, dec_seq, _ = x_sample.shape
    assert norm_g.shape[0] == 1 and lb_logits.shape[0] == 2, "single-layer decoder only"
    assert seq % TM_PROMPT == 0 and TM_PROMPT % SUB_TILE == 0 and dec_batch % BB_SAMPLE == 0
    assert dec_seq == DEC_SEQ and PAIR_ROWS == SUBLANES and BB_SAMPLE % 2 == 0
    assert dec_batch % PREP_SEQS == 0 and (PREP_SEQS * DEC_SEQ) % SUB_TILE == 0
    assert PREP_SEQS == PREP_PARTS * BB_SAMPLE and BB_SAMPLE % STATE_DMAS == 0
    assert PAST_LEN >= max(POOL_WINDOWS)

    n_groups = len(POOL_WINDOWS)
    tm = TM_PROMPT
    nt = seq // tm
    assert w_in.shape[1:] == (2 * tm, D_IN) and w_proj_a.shape[1] == tm, "weight staging layout"
    assert tm % (WIN_SLOTS * 2 * SUBLANES) == 0, "bf16-tile-aligned w_in staging blocks"
    assert tm // WIN_SLOTS == G_B, "a staging block of the small weights carries one w_pool group"
    hbm = pl.BlockSpec(memory_space=pl.ANY)
    bf16_shapes = [(D_MODEL, D_IN), (W_BRANCH, D_MODEL), (n_groups, G_B, G_B),
                   (W_BRANCH, D_MODEL), (D_MODEL, D_MODEL)]
    y_p, s_p, p_p, win_b, wpa_b, wpool_b, wpb_b, wout_b = pl.pallas_call(
        _prompt_kernel,
        grid=(batch, nt),
        in_specs=[pl.BlockSpec((1, tm, D_MODEL), lambda b, t: (b, t, 0)),
                  _const_spec((1, D_MODEL)), hbm, _const_spec((2, W_BRANCH)), _const_spec((1, DV)),
                  hbm, hbm, _const_spec((1, W_BRANCH)), hbm, hbm, _const_spec((1, D_MODEL))],
        out_specs=[
            pl.BlockSpec((1, tm, D_MODEL), lambda b, t: (b, t, 0)),
            pl.BlockSpec((1, N_HEADS, DK, DV), lambda b, t: (b, 0, 0, 0)),
            pl.BlockSpec((POOL_BUF, batch, W_BRANCH), lambda b, t: (0, 0, 0)),
        ] + [hbm] * len(bf16_shapes),
        out_shape=[
            jax.ShapeDtypeStruct((batch, seq, D_MODEL), F32),
            jax.ShapeDtypeStruct((batch, N_HEADS, DK, DV), F32),
            jax.ShapeDtypeStruct((POOL_BUF, batch, W_BRANCH), F32),
        ] + [jax.ShapeDtypeStruct(s, BF16) for s in bf16_shapes],
        scratch_shapes=[
            pltpu.VMEM((tm, D_IN), F32),
            pltpu.VMEM((N_HEADS, DV, DK), F32),
            pltpu.VMEM((tm + 2 * SUBLANES, W_BRANCH), F32),
            pltpu.VMEM((N_HEADS, tm, (SUB_TILE // CHUNK) * DK), BF16),
            pltpu.VMEM((N_HEADS, tm // CHUNK, DK, DV), BF16),
            pltpu.VMEM((tm, W_BRANCH), BF16),
            pltpu.VMEM((tm, W_BRANCH), BF16),
            pltpu.VMEM((tm, W_BRANCH), BF16),
            pltpu.VMEM((tm, W_BRANCH), BF16),
            pltpu.VMEM((tm, W_BRANCH), BF16),
            pltpu.VMEM((tm, W_BRANCH), BF16),
        ] + [pltpu.VMEM(s, BF16) for s in bf16_shapes] + [
            pltpu.SemaphoreType.DMA((WIN_SLOTS, 5)),
            pltpu.SemaphoreType.DMA((len(bf16_shapes),)),
        ],
        compiler_params=pltpu.CompilerParams(
            dimension_semantics=("arbitrary", "arbitrary"),
            vmem_limit_bytes=VMEM_LIMIT_BYTES),
        name="hgrn2_pool_prompt",
    )(x_prompt, norm_g, w_in[0], lb_logits, hgrn_norm_g, w_proj_a[0],
      w_pool[0].reshape(n_groups * G_B, G_B), pool_scale, w_proj_b[0], w_out[0],
      final_norm_g.reshape(1, D_MODEL))
    weights = (norm_g, win_b, lb_logits, hgrn_norm_g, wpa_b, wpool_b, pool_scale, wpb_b, wout_b,
               final_norm_g.reshape(1, D_MODEL))

    rows = dec_batch * DEC_SEQ
    pool_in = jnp.transpose(state_pool[0], (1, 0, 2))
    ng, win_b, lbl, hg, wpa_b, wpool_b, ps, wpb_b, wout_b, fg = weights
    pseq = PREP_SEQS
    prows = pseq * DEC_SEQ
    n_prep = dec_batch // pseq
    bb = BB_SAMPLE

    def prep_blk(s):
        return jnp.minimum((s + PREP_PARTS - 1) // PREP_PARTS, n_prep - 1)

    def finish_blk(s):
        return jnp.minimum(jnp.maximum(s - 2, 0) // PREP_PARTS, n_prep - 1)

    s_s, y_s, pool_out = pl.pallas_call(
        functools.partial(_decode_kernel, n_prep=n_prep),
        grid=(1 + dec_batch // bb,),
        in_specs=[
            pl.BlockSpec((pseq, DEC_SEQ, D_MODEL), lambda s: (prep_blk(s), 0, 0)),
            pl.BlockSpec((POOL_BUF, pseq, W_BRANCH), lambda s: (0, prep_blk(s), 0),
                         pipeline_mode=pl.Buffered(1)),
            hbm,
            _const_spec(ng.shape), _const_spec(win_b.shape), _const_spec(lbl.shape),
            _const_spec(wpool_b.shape), _const_spec(ps.shape), _const_spec(wpb_b.shape),
            _const_spec(hg.shape), _const_spec(wpa_b.shape), _const_spec(wout_b.shape),
            _const_spec(fg.shape),
        ],
        out_specs=[
            hbm,
            pl.BlockSpec((pseq, DEC_SEQ, D_MODEL), lambda s: (finish_blk(s), 0, 0)),
            pl.BlockSpec((POOL_BUF, pseq, W_BRANCH), lambda s: (0, prep_blk(s), 0)),
        ],
        out_shape=[
            jax.ShapeDtypeStruct((dec_batch, N_HEADS, DK, DV), F32),
            jax.ShapeDtypeStruct(x_sample.shape, F32),
            jax.ShapeDtypeStruct((POOL_BUF, dec_batch, W_BRANCH), F32),
        ],
        scratch_shapes=[
            pltpu.VMEM((N_HEADS, 2, rows, DK), BF16),
            pltpu.VMEM((N_HEADS, 2, rows, DV), BF16),
            pltpu.VMEM((rows, W_BRANCH), BF16),
            pltpu.VMEM((rows, W_BRANCH), F32),
            pltpu.VMEM((rows, W_BRANCH), F32),
            pltpu.VMEM((rows, D_MODEL), F32),
            pltpu.VMEM((rows, D_MODEL), F32),
            pltpu.VMEM((rows, D_MODEL), F32),
            pltpu.VMEM((prows, D_IN), F32),
            pltpu.VMEM((len(POOL_WINDOWS), prows, G_B), F32),
            pltpu.VMEM((len(POOL_WINDOWS), prows, G_B), F32),
            pltpu.VMEM((prows, W_BRANCH), BF16),
            pltpu.VMEM((prows, W_BRANCH), BF16),
            pltpu.VMEM((2, bb, N_HEADS, DK, DV), F32),
            pltpu.VMEM((2, bb, N_HEADS, DK, DV), F32),
            pltpu.SemaphoreType.DMA((2, STATE_DMAS)),
            pltpu.SemaphoreType.DMA((2, STATE_DMAS)),
        ],
        compiler_params=pltpu.CompilerParams(
            dimension_semantics=("arbitrary",), vmem_limit_bytes=DECODE_VMEM_LIMIT_BYTES),
        name="hgrn2_pool_decode",
    )(x_sample, pool_in, state_hgrn[0], ng, win_b, lbl, wpool_b, ps, wpb_b,
      hg, wpa_b, wout_b, fg)

    p_p = jnp.transpose(p_p, (1, 0, 2))
    p_s = jnp.transpose(pool_out, (1, 0, 2))
    return (y_p, y_s, s_p[None], p_p[None], s_s[None], p_s[None])
```

```python
import functools

import jax
import jax.numpy as jnp
from jax import lax
from jax.experimental import pallas as pl
from jax.experimental.pallas import tpu as pltpu

F32 = jnp.float32
BF16 = jnp.bfloat16

D_MODEL = 1024
W_BRANCH = 512
N_HEADS = 4
DK = 128
DV = 128
CHUNK = 32
POOL_WINDOWS = (2, 4, 8, 16)
G_B = 128
POOL_BUF = 15
PAST_LEN = 16384
EPS = 1e-6
D_IN = 4 * W_BRANCH + 2 * W_BRANCH + 2 * D_MODEL
OFF_Q, OFF_F, OFF_I, OFF_GA = 0, 512, 1024, 1536
OFF_U, OFF_GB, OFF_MA, OFF_MB = 2048, 2560, 3072, 4096

SUBLANES = 8
VMEM_LIMIT_BYTES = 56 * 1024 * 1024
DECODE_VMEM_LIMIT_BYTES = 60 * 1024 * 1024

TM_PROMPT = 512
WIN_SLOTS = 4
SUB_TILE = 256
PREP_SEQS = 64
BB_SAMPLE = 16
STATE_DMAS = 2
DEC_SEQ = 4
PAIR_ROWS = 2 * DEC_SEQ
PREP_PARTS = 4


def _rms(x, g):
    ms = jnp.mean(x * x, axis=-1, keepdims=True)
    return x * lax.rsqrt(ms + EPS) * g


def _lower_bound(lbl_ref):
    l0 = lbl_ref[0:1, :]
    l1 = lbl_ref[1:2, :]
    m = jnp.maximum(l0, l1)
    e0 = jnp.exp(l0 - m)
    e1 = jnp.exp(l1 - m)
    return e0 / (e0 + e1)


def _in_proj(x, ng_ref, win_ref, z_ref, c0=0, c1=D_IN):
    h = _rms(x, ng_ref[...]).astype(BF16)
    z_ref[:, c0:c1] = jnp.dot(h, win_ref[:, c0:c1], preferred_element_type=F32)


def _hgrn_inputs(z_ref, lb, hh):
    c = hh * DK
    q = z_ref[:, OFF_Q + c:OFF_Q + c + DK]
    f = z_ref[:, OFF_F + c:OFF_F + c + DK]
    v = z_ref[:, OFF_I + c:OFF_I + c + DK]
    lbh = lb[:, c:c + DK]
    fg = lbh + (1.0 - lbh) * jax.nn.sigmoid(f)
    logf = jnp.log(fg)
    k = 1.0 - fg
    qf = jax.nn.silu(q) * (DK ** -0.5)
    return qf, k, v, logf


def _group_cumsum(x, group):
    row = lax.broadcasted_iota(jnp.int32, x.shape, 0) & (group - 1)
    s = 1
    while s < group:
        x = x + jnp.where(row >= s, pltpu.roll(x, s, 0), 0.0)
        s *= 2
    return x


def _hgrn_epilogue(o, z_ref, hg_ref, hh, oa_ref):
    c = hh * DK
    ga = z_ref[:, OFF_GA + c:OFF_GA + c + DK]
    on = _rms(o, hg_ref[...]) * jax.nn.silu(ga)
    oa_ref[:, c:c + DK] = on.astype(BF16)


def _pool_epilogue(pooled, z_ref, g, wpool_ref, ps_ref, yb_ref):
    c = g * G_B
    mixed = jnp.dot(pooled.astype(BF16), wpool_ref[g], preferred_element_type=F32)
    gb = z_ref[:, OFF_GB + c:OFF_GB + c + G_B]
    yb = mixed * ps_ref[:, c:c + G_B] * jax.nn.silu(gb)
    yb_ref[:, c:c + G_B] = yb.astype(BF16)


def _weight_copies(hbm_refs, stage_ref, sem):
    win_hbm, wpa_hbm, wpool_hbm, wpb_hbm, wout_hbm = hbm_refs
    rows = stage_ref.shape[0]
    blk = rows // WIN_SLOTS
    n_small = rows // blk

    def block(k):
        slot = k % WIN_SLOTS

        def dst(c0, c1):
            return stage_ref.at[slot * blk:(slot + 1) * blk, c0:c1]
        if k < n_small:
            r0 = k * blk
            pairs = [(wpa_hbm.at[r0:r0 + blk, :], dst(0, D_MODEL)),
                     (wpb_hbm.at[r0:r0 + blk, :], dst(D_MODEL, 2 * D_MODEL)),
                     (wout_hbm.at[r0:r0 + blk, :], dst(2 * D_MODEL, 3 * D_MODEL)),
                     (wout_hbm.at[rows + r0:rows + r0 + blk, :], dst(3 * D_MODEL, 4 * D_MODEL)),
                     (wpool_hbm.at[r0:r0 + blk, :], dst(4 * D_MODEL, 4 * D_MODEL + G_B))]
        else:
            r0 = (k - n_small) * blk
            pairs = [(win_hbm.at[r0:r0 + blk, :], dst(0, D_IN))]
        return [pltpu.make_async_copy(src, d, sem.at[slot, i]) for i, (src, d) in enumerate(pairs)]
    return block, blk, n_small


def _weight_exports(vmem_refs, out_refs, sem):
    return [pltpu.make_async_copy(src, dst, sem.at[i])
            for i, (src, dst) in enumerate(zip(vmem_refs, out_refs))]


def _prompt_kernel(x_ref, ng_ref, win_hbm, lbl_ref, hg_ref, wpa_hbm, wpool_hbm, ps_ref,
                   wpb_hbm, wout_hbm, fg_ref,
                   y_ref, s_out_ref, p_out_ref, win_out, wpa_out, wpool_out, wpb_out, wout_out,
                   z_ref, st_ref, ext_ref, kx_ref, sn_ref, qs_ref, ks_ref, qd_ref, vb_ref,
                   oa_ref, yb_ref, win_ref, wpa_ref, wpool_ref, wpb_ref, wout_ref,
                   load_sem, export_sem):
    tm = TM_PROMPT
    sub = SUB_TILE
    n_sub = tm // sub
    nc = tm // CHUNK
    ncs = sub // CHUNK
    t = pl.program_id(1)
    first = t == 0
    hdr = 2 * SUBLANES
    bf16_weights = (win_ref, wpa_ref, wpool_ref, wpb_ref, wout_ref)
    exports = _weight_exports(bf16_weights, (win_out, wpa_out, wpool_out, wpb_out, wout_out),
                              export_sem)

    @pl.when((pl.program_id(0) == 0) & first)
    def _():
        block, blk, n_small = _weight_copies((win_hbm, wpa_hbm, wpool_hbm, wpb_hbm, wout_hbm),
                                             z_ref, load_sem)
        n_blk = n_small + D_MODEL // blk
        for k in range(WIN_SLOTS):
            for c in block(k):
                c.start()
        st_ref[...] = jnp.zeros_like(st_ref)
        ext_ref[...] = jnp.zeros_like(ext_ref)
        kx_ref[...] = jnp.zeros_like(kx_ref)
        for k in range(n_blk):
            for c in block(k):
                c.wait()
            slot = z_ref.at[(k % WIN_SLOTS) * blk:(k % WIN_SLOTS + 1) * blk, :]
            if k < n_small:
                r = slice(k * blk, (k + 1) * blk)
                wpa_ref[r, :] = slot[:, 0:D_MODEL].astype(BF16)
                wpb_ref[r, :] = slot[:, D_MODEL:2 * D_MODEL].astype(BF16)
                wout_ref[r, :] = slot[:, 2 * D_MODEL:3 * D_MODEL].astype(BF16)
                wout_ref[tm + k * blk:tm + (k + 1) * blk, :] = (
                    slot[:, 3 * D_MODEL:4 * D_MODEL].astype(BF16))
                wpool_ref[k] = slot[:, 4 * D_MODEL:4 * D_MODEL + G_B].astype(BF16)
            else:
                r = slice((k - n_small) * blk, (k - n_small + 1) * blk)
                win_ref[r, :] = slot[...].astype(BF16)
            if k + WIN_SLOTS < n_blk:
                for c in block(k + WIN_SLOTS):
                    c.start()
        for c in exports:
            c.start()

    @pl.when((pl.program_id(0) == pl.num_programs(0) - 1) & (t == pl.num_programs(1) - 1))
    def _():
        for c in exports:
            c.wait()

    x = x_ref[0]
    _in_proj(x, ng_ref, win_ref, z_ref)
    lb = _lower_bound(lbl_ref)

    ri = lax.broadcasted_iota(jnp.int32, (sub, sub), 0)
    ci = lax.broadcasted_iota(jnp.int32, (sub, sub), 1)
    causal = ((ri // CHUNK) == (ci // CHUNK)) & (ci <= ri)

    heads = range(N_HEADS)
    units = [(hh, s) for hh in heads for s in range(n_sub)]

    def rows(s):
        return slice(s * sub, (s + 1) * sub)

    def cols(hh):
        return slice(hh * DK, (hh + 1) * DK)

    dec = []
    for hh in heads:
        qf, k, v, logf = _hgrn_inputs(z_ref, lb, hh)
        b = _group_cumsum(logf, CHUNK)
        b3 = b.reshape(nc, CHUNK, DK)
        ref = b3[:, CHUNK // 2:CHUNK // 2 + 1, :]
        bl = b3[:, CHUNK - 1:CHUNK, :]
        q3 = qf.reshape(nc, CHUNK, DK)
        k3 = k.reshape(nc, CHUNK, DK)
        qs_ref[:, cols(hh)] = (q3 * jnp.exp(b3 - ref)).reshape(tm, DK).astype(BF16)
        ks_ref[:, cols(hh)] = (k3 * jnp.exp(ref - b3)).reshape(tm, DK).astype(BF16)
        qd_ref[:, cols(hh)] = (q3 * jnp.exp(b3)).reshape(tm, DK).astype(BF16)
        vb_ref[:, cols(hh)] = v.astype(BF16)
        kd = (k3 * jnp.exp(bl - b3)).reshape(tm, DK).astype(BF16)
        dec.append(jnp.exp(bl))
        for c in range(nc):
            cc = c % ncs
            kx_ref[hh, c * CHUNK:(c + 1) * CHUNK, cc * DK:(cc + 1) * DK] = (
                kd[c * CHUNK:(c + 1) * CHUNK, :])

    ext_ref[0:hdr, :] = jnp.where(first, 0.0, ext_ref[tm:tm + hdr, :])
    ext_ref[hdr:hdr + tm, :] = z_ref[:, OFF_U:OFF_U + W_BRANCH]
    pos1 = t * tm + lax.broadcasted_iota(jnp.int32, (tm, 1), 0) + 1
    for g, w in enumerate(POOL_WINDOWS):
        c = g * G_B
        s = ext_ref[:, c:c + G_B]
        sh = 1
        while sh < w:
            s = s + pltpu.roll(s, sh, 0)
            sh *= 2
        inv = jnp.where(pos1 >= w, 1.0 / w, 1.0 / pos1.astype(F32))
        pooled = s[hdr:, :] * inv - z_ref[:, OFF_U + c:OFF_U + c + G_B]
        _pool_epilogue(pooled, z_ref, g, wpool_ref, ps_ref, yb_ref)
    for j in range(POOL_BUF):
        r = hdr + tm - POOL_BUF + j
        p_out_ref[j, pl.ds(pl.program_id(0), 1), :] = ext_ref[r:r + 1, :]

    sc = {(hh, s): lax.dot_general(qs_ref[rows(s), cols(hh)], ks_ref[rows(s), cols(hh)],
                                   (((1,), (1,)), ((), ())), preferred_element_type=F32)
          for hh, s in units}
    ut = {(hh, s): lax.dot_general(vb_ref[rows(s), cols(hh)], kx_ref[hh, rows(s), :],
                                   (((0,), (0,)), ((), ())), preferred_element_type=F32)
          for hh, s in units}
    o = {(hh, s): jnp.dot(jnp.where(causal, sc[hh, s], 0.0).astype(BF16),
                          vb_ref[rows(s), cols(hh)], preferred_element_type=F32)
         for hh, s in units}
    for hh in heads:
        st = jnp.where(first, 0.0, st_ref[hh])
        for c in range(nc):
            sn_ref[hh, c] = st.T.astype(BF16)
            st = dec[hh][c] * st + ut[hh, c // ncs][:, (c % ncs) * DK:(c % ncs + 1) * DK]
        st_ref[hh] = st
        s_out_ref[0, hh] = st.T
    wcol = D_MODEL // N_HEADS
    y_b = []
    for hh in heads:
        o_inter = [jnp.dot(qd_ref[c * CHUNK:(c + 1) * CHUNK, cols(hh)], sn_ref[hh, c],
                           preferred_element_type=F32) for c in range(nc)]
        y_b.append(jnp.dot(yb_ref[...], wpb_ref[:, hh * wcol:(hh + 1) * wcol],
                           preferred_element_type=F32))
        o_hh = jnp.concatenate([o[hh, s] for s in range(n_sub)], axis=0)
        _hgrn_epilogue(o_hh + jnp.concatenate(o_inter, axis=0), z_ref, hg_ref, hh, oa_ref)
    gated_b = jax.nn.sigmoid(z_ref[:, OFF_MB:OFF_MB + D_MODEL]) * jnp.concatenate(y_b, axis=1)

    y_a = jnp.dot(oa_ref[...], wpa_ref[...], preferred_element_type=F32)
    merged = jax.nn.sigmoid(z_ref[:, OFF_MA:OFF_MA + D_MODEL]) * y_a + gated_b
    out = x + jnp.dot(merged.astype(BF16), wout_ref[...], preferred_element_type=F32)
    y_ref[0] = _rms(out, fg_ref[...])


def _shift_rows(x, k):
    n = x.shape[0]
    return x if k % n == 0 else pltpu.roll(x, (-k) % n, 0)


def _group_bcast(x, j, group):
    t = lax.broadcasted_iota(jnp.int32, x.shape, 0) & (group - 1)
    out = _shift_rows(x, j - (group - 1))
    for tt in range(group - 2, -1, -1):
        out = jnp.where(t == tt, _shift_rows(x, j - tt), out)
    return out


def _decode_prep_hgrn(heads, z_ref, lbl_ref, a_ref, bv_ref, qd_ref, oi_ref, ga_ref):
    rows = z_ref.shape[0]
    lb = _lower_bound(lbl_ref)
    r = lax.broadcasted_iota(jnp.int32, (rows, DK), 0)
    t = r & (DEC_SEQ - 1)
    even = (r & (PAIR_ROWS - 1)) < DEC_SEQ
    blk = SUB_TILE
    ri = lax.broadcasted_iota(jnp.int32, (blk, blk), 0)
    ci = lax.broadcasted_iota(jnp.int32, (blk, blk), 1)
    causal = ((ri // DEC_SEQ) == (ci // DEC_SEQ)) & (ci <= ri)

    for hh in heads:
        cs = slice(hh * DK, (hh + 1) * DK)
        qf, k, v, logf = _hgrn_inputs(z_ref, lb, hh)
        b = _group_cumsum(logf, DEC_SEQ)
        ref = _group_bcast(b, DEC_SEQ // 2, DEC_SEQ)
        bl = _group_bcast(b, DEC_SEQ - 1, DEC_SEQ)
        qs = (qf * jnp.exp(b - ref)).astype(BF16)
        ks = (k * jnp.exp(ref - b)).astype(BF16)
        kd = k * jnp.exp(bl - b)
        vb = v.astype(BF16)
        qd_ref[:, cs] = (qf * jnp.exp(b)).astype(BF16)
        for s in range(rows // blk):
            rs = slice(s * blk, (s + 1) * blk)
            sc = lax.dot_general(qs[rs], ks[rs], (((1,), (1,)), ((), ())),
                                 preferred_element_type=F32)
            oi_ref[rs, cs] = jnp.dot(jnp.where(causal, sc, 0.0).astype(BF16), vb[rs],
                                     preferred_element_type=F32)
        dec = jnp.exp(bl)
        d1 = dec.astype(BF16).astype(F32)
        d2 = (dec - d1).astype(BF16).astype(F32)
        d3 = (dec - d1 - d2).astype(BF16).astype(F32)
        tail = jnp.where(t == 0, d1, jnp.where(t == 1, d2, jnp.where(t == 2, d3, 0.0)))
        a_ref[hh, 0] = jnp.where(even, kd, _shift_rows(tail, -DEC_SEQ)).astype(BF16)
        a_ref[hh, 1] = jnp.where(even, _shift_rows(tail, DEC_SEQ), kd).astype(BF16)
        zero = jnp.zeros_like(vb)
        bv_ref[hh, 0] = jnp.where(even, vb, zero)
        bv_ref[hh, 1] = jnp.where(even, zero, vb)
        ga_ref[:, cs] = jax.nn.silu(z_ref[:, OFF_GA + hh * DK:OFF_GA + (hh + 1) * DK])


def _decode_prep_history(p_in_ref, rw_ref, p_out_ref):
    n_seq = p_in_ref.shape[1]
    for g, w in enumerate(POOL_WINDOWS):
        gc = slice(g * G_B, (g + 1) * G_B)
        acc = None
        suffix = {}
        for j in range(POOL_BUF - 1, -1, -1):
            e = p_in_ref[j, :, gc]
            acc = e if acc is None else acc + e
            suffix[j] = acc
        for tt in range(DEC_SEQ):
            j = POOL_BUF + 1 - w + tt
            rw_ref[g, pl.ds(tt, n_seq, stride=DEC_SEQ), :] = (
                suffix[j] if j < POOL_BUF else jnp.zeros((n_seq, G_B), F32))
    for j in range(POOL_BUF - DEC_SEQ):
        p_out_ref[j] = p_in_ref[j + DEC_SEQ]


def _decode_prep_pool(z_ref, wpool_ref, ps_ref, wpb_ref, ma_ref, gb_ref, p_out_ref,
                      rw_ref, u_ref, yb_ref):
    rows = z_ref.shape[0]
    n_seq = rows // DEC_SEQ
    t = lax.broadcasted_iota(jnp.int32, (rows, G_B), 0) & (DEC_SEQ - 1)
    for g, w in enumerate(POOL_WINDOWS):
        u = z_ref[:, OFF_U + g * G_B:OFF_U + (g + 1) * G_B]
        u_ref[g] = u
        cu = u
        sh = 1
        while sh < min(w, DEC_SEQ):
            cu = cu + jnp.where(t >= sh, _shift_rows(cu, -sh), 0.0)
            sh *= 2
        pooled = (rw_ref[g] + cu) * (1.0 / w) - u
        _pool_epilogue(pooled, z_ref, g, wpool_ref, ps_ref, yb_ref)
    gb_ref[...] = jax.nn.sigmoid(z_ref[:, OFF_MB:OFF_MB + D_MODEL]) * jnp.dot(
        yb_ref[...], wpb_ref[...], preferred_element_type=F32)
    ma_ref[...] = jax.nn.sigmoid(z_ref[:, OFF_MA:OFF_MA + D_MODEL])
    for tt in range(DEC_SEQ):
        for g in range(len(POOL_WINDOWS)):
            p_out_ref[POOL_BUF - DEC_SEQ + tt, :, g * G_B:(g + 1) * G_B] = (
                u_ref[g, pl.ds(tt, n_seq, stride=DEC_SEQ), :])


def _decode_prep(part, x_ref, p_in_ref, ng_ref, win_ref, lbl_ref, wpool_ref, ps_ref, wpb_ref,
                 a_ref, bv_ref, qd_ref, oi_ref, ga_ref, ma_ref, gb_ref, p_out_ref,
                 xs_ref, z_ref, rw_ref, u_ref, yb_ref):
    hgrn = functools.partial(_decode_prep_hgrn, z_ref=z_ref, lbl_ref=lbl_ref, a_ref=a_ref,
                             bv_ref=bv_ref, qd_ref=qd_ref, oi_ref=oi_ref, ga_ref=ga_ref)
    if part == 0:
        for i in range(x_ref.shape[0]):
            xs_ref[i * DEC_SEQ:(i + 1) * DEC_SEQ, :] = x_ref[i]
        _in_proj(xs_ref[...], ng_ref, win_ref, z_ref, 0, OFF_U)
        _decode_prep_history(p_in_ref, rw_ref, p_out_ref)
    elif part == 1:
        _in_proj(xs_ref[...], ng_ref, win_ref, z_ref, OFF_U, D_IN)
    elif part == 2:
        hgrn(range(0, N_HEADS - 1))
    else:
        hgrn(range(N_HEADS - 1, N_HEADS))
        _decode_prep_pool(z_ref, wpool_ref, ps_ref, wpb_ref, ma_ref, gb_ref, p_out_ref,
                          rw_ref, u_ref, yb_ref)


def _decode_state_step(a_ref, bv_ref, qd_ref, s_in_ref, s_out_ref, o_ref, stores):
    n_pairs = BB_SAMPLE // 2
    r8 = lax.broadcasted_iota(jnp.int32, (PAIR_ROWS, DV), 0)
    even = r8 < DEC_SEQ
    ones = (jnp.where((r8 >= DEC_SEQ) & (r8 < DEC_SEQ + 3), 1.0, 0.0).astype(BF16),
            jnp.where(r8 < 3, 1.0, 0.0).astype(BF16))
    pairs_per_store = n_pairs // len(stores)
    for g, store in enumerate(stores):
        for hh in range(N_HEADS):
            for lp in range(g * pairs_per_store, (g + 1) * pairs_per_store):
                rs = slice(lp * PAIR_ROWS, (lp + 1) * PAIR_ROWS)
                for e in range(2):
                    rhs = jnp.concatenate([bv_ref[hh, e, rs, :], ones[e]], axis=1)
                    upd = lax.dot_general(a_ref[hh, e, rs, :], rhs, (((0,), (0,)), ((), ())),
                                          preferred_element_type=F32)
                    s_out_ref[2 * lp + e, hh] = (upd[:, DV:2 * DV] * s_in_ref[2 * lp + e, hh]
                                                 + upd[:, 0:DV])
        store.start()
    for hh in range(N_HEADS):
        cs = slice(hh * DK, (hh + 1) * DK)
        for lp in range(n_pairs):
            rs = slice(lp * PAIR_ROWS, (lp + 1) * PAIR_ROWS)
            q8 = qd_ref[rs, cs]
            o0 = jnp.dot(q8, s_in_ref[2 * lp, hh].astype(BF16), preferred_element_type=F32)
            o1 = jnp.dot(q8, s_in_ref[2 * lp + 1, hh].astype(BF16), preferred_element_type=F32)
            o_ref[rs, cs] = o_ref[rs, cs] + jnp.where(even, o0, o1)


def _decode_finish(o_ref, ga_ref, ma_ref, gb_ref, xs_ref, hg_ref, wpa_ref, wout_ref,
                   fg_ref, y_ref, oa_ref, *, n_seq):
    for hh in range(N_HEADS):
        cs = slice(hh * DK, (hh + 1) * DK)
        oa_ref[:, cs] = (_rms(o_ref[:, cs], hg_ref[...]) * ga_ref[:, cs]).astype(BF16)
    y_a = jnp.dot(oa_ref[...], wpa_ref[...], preferred_element_type=F32)
    merged = ma_ref[...] * y_a + gb_ref[...]
    out = xs_ref[...] + jnp.dot(merged.astype(BF16), wout_ref[...], preferred_element_type=F32)
    y = _rms(out, fg_ref[...])
    for i in range(n_seq):
        y_ref[i] = y[i * DEC_SEQ:(i + 1) * DEC_SEQ, :]


def _state_copies(hbm_ref, buf_ref, sem, blk, slot, to_hbm):
    n = BB_SAMPLE // STATE_DMAS
    copies = []
    for c in range(STATE_DMAS):
        in_hbm = hbm_ref.at[pl.ds(blk * BB_SAMPLE + c * n, n)]
        in_vmem = buf_ref.at[slot, pl.ds(c * n, n)]
        src, dst = (in_vmem, in_hbm) if to_hbm else (in_hbm, in_vmem)
        copies.append(pltpu.make_async_copy(src, dst, sem.at[slot, c]))
    return copies


def _decode_kernel(xp_ref, p_in_ref, s_in_hbm,
                   ng_ref, win_ref, lbl_ref, wpool_ref, ps_ref, wpb_ref,
                   hg_ref, wpa_hbm, wout_hbm, fg_ref,
                   s_out_hbm, y_ref, p_out_ref,
                   a_all, bv_all, qd_all, o_all, ga_all, ma_all, gb_all, xs_all,
                   z_ref, rw_ref, u_ref, yb_ref, oa_ref, s_in_buf, s_out_buf, wpa_ref, wout_ref,
                   in_sem, out_sem, w_sem, *, n_prep):
    s = pl.program_id(0)
    prows = PREP_SEQS * DEC_SEQ
    brows = BB_SAMPLE * DEC_SEQ
    n_state = n_prep * PREP_PARTS
    late_weights = [pltpu.make_async_copy(src, dst, w_sem.at[k]) for k, (src, dst) in
                    enumerate(((wpa_hbm, wpa_ref), (wout_hbm, wout_ref)))]

    def load(blk, slot):
        return _state_copies(s_in_hbm, s_in_buf, in_sem, blk, slot, to_hbm=False)

    def store(blk, slot):
        return _state_copies(s_out_hbm, s_out_buf, out_sem, blk, slot, to_hbm=True)

    def block_rows(blk):
        start = blk * prows
        return pl.ds(start if isinstance(start, int) else pl.multiple_of(start, prows), prows)

    def prep(part, blk):
        r = block_rows(blk)
        _decode_prep(part, xp_ref, p_in_ref, ng_ref, win_ref, lbl_ref, wpool_ref, ps_ref, wpb_ref,
                     a_all.at[:, :, r, :], bv_all.at[:, :, r, :], qd_all.at[r, :],
                     o_all.at[r, :], ga_all.at[r, :], ma_all.at[r, :], gb_all.at[r, :],
                     p_out_ref, xs_all.at[r, :], z_ref, rw_ref, u_ref, yb_ref)

    def finish(blk):
        f = block_rows(blk)
        _decode_finish(o_all.at[f, :], ga_all.at[f, :], ma_all.at[f, :], gb_all.at[f, :],
                       xs_all.at[f, :], hg_ref, wpa_ref, wout_ref, fg_ref, y_ref, oa_ref,
                       n_seq=PREP_SEQS)

    i = s - 1
    blk = lax.div(i, PREP_PARTS)
    part_now = lax.rem(i, PREP_PARTS)
    slot = lax.rem(i, 2)

    @pl.when(s == 0)
    def _():
        for b in range(2):
            for c in load(b, b):
                c.start()
        for c in late_weights:
            c.start()
        for part in range(PREP_PARTS):
            prep(part, 0)
        for c in late_weights:
            c.wait()

    @pl.when(s > 0)
    def _():
        @pl.when((i > 0) & (i + 1 < n_state))
        def _():
            for c in load(i + 1, 1 - slot):
                c.start()
        for c in load(i, slot):
            c.wait()

        @pl.when(i >= 2)
        def _():
            for c in store(i - 2, slot):
                c.wait()
        r = pl.ds(pl.multiple_of(i * brows, brows), brows)
        _decode_state_step(a_all.at[:, :, r, :], bv_all.at[:, :, r, :], qd_all.at[r, :],
                           s_in_buf.at[slot], s_out_buf.at[slot], o_all.at[r, :],
                           stores=store(i, slot))

    for part in range(PREP_PARTS):
        @pl.when((part_now == part) & (blk + 1 < n_prep))
        def _():
            prep(part, blk + 1)

    @pl.when((part_now == 0) & (i > 0))
    def _():
        finish(blk - 1)

    @pl.when(i == n_state - 1)
    def _():
        finish(n_prep - 1)
        for c in store(i - 1, 1 - slot) + store(i, slot):
            c.wait()


def _const_spec(shape):
    n = len(shape)
    return pl.BlockSpec(shape, lambda *_: (0,) * n, pipeline_mode=pl.Buffered(1))


def kernel(x_prompt, x_sample, state_hgrn, state_pool, norm_g, w_in, lb_logits, hgrn_norm_g,
           w_proj_a, w_pool, pool_scale, w_proj_b, w_out, final_norm_g):
    batch, seq, _ = x_prompt.shape
    dec_batch, dec_seq, _ = x_sample.shape
    assert norm_g.shape[0] == 1 and lb_logits.shape[0] == 2, "single-layer decoder only"
    assert seq % TM_PROMPT == 0 and TM_PROMPT % SUB_TILE == 0 and dec_batch % BB_SAMPLE == 0
    assert dec_seq == DEC_SEQ and PAIR_ROWS == SUBLANES and BB_SAMPLE % 2 == 0
    assert dec_batch % PREP_SEQS == 0 and (PREP_SEQS * DEC_SEQ) % SUB_TILE == 0
    assert PREP_SEQS == PREP_PARTS * BB_SAMPLE and BB_SAMPLE % STATE_DMAS == 0
    assert PAST_LEN >= max(POOL_WINDOWS)

    n_groups = len(POOL_WINDOWS)
    tm = TM_PROMPT
    nt = seq // tm
    assert w_in.shape[1:] == (2 * tm, D_IN) and w_proj_a.shape[1] == tm, "weight staging layout"
    assert tm % (WIN_SLOTS * 2 * SUBLANES) == 0, "bf16-tile-aligned w_in staging blocks"
    assert tm // WIN_SLOTS == G_B, "a staging block of the small weights carries one w_pool group"
    hbm = pl.BlockSpec(memory_space=pl.ANY)
    bf16_shapes = [(D_MODEL, D_IN), (W_BRANCH, D_MODEL), (n_groups, G_B, G_B),
                   (W_BRANCH, D_MODEL), (D_MODEL, D_MODEL)]
    y_p, s_p, p_p, win_b, wpa_b, wpool_b, wpb_b, wout_b = pl.pallas_call(
        _prompt_kernel,
        grid=(batch, nt),
        in_specs=[pl.BlockSpec((1, tm, D_MODEL), lambda b, t: (b, t, 0)),
                  _const_spec((1, D_MODEL)), hbm, _const_spec((2, W_BRANCH)), _const_spec((1, DV)),
                  hbm, hbm, _const_spec((1, W_BRANCH)), hbm, hbm, _const_spec((1, D_MODEL))],
        out_specs=[
            pl.BlockSpec((1, tm, D_MODEL), lambda b, t: (b, t, 0)),
            pl.BlockSpec((1, N_HEADS, DK, DV), lambda b, t: (b, 0, 0, 0)),
            pl.BlockSpec((POOL_BUF, batch, W_BRANCH), lambda b, t: (0, 0, 0)),
        ] + [hbm] * len(bf16_shapes),
        out_shape=[
            jax.ShapeDtypeStruct((batch, seq, D_MODEL), F32),
            jax.ShapeDtypeStruct((batch, N_HEADS, DK, DV), F32),
            jax.ShapeDtypeStruct((POOL_BUF, batch, W_BRANCH), F32),
        ] + [jax.ShapeDtypeStruct(s, BF16) for s in bf16_shapes],
        scratch_shapes=[
            pltpu.VMEM((tm, D_IN), F32),
            pltpu.VMEM((N_HEADS, DV, DK), F32),
            pltpu.VMEM((tm + 2 * SUBLANES, W_BRANCH), F32),
            pltpu.VMEM((N_HEADS, tm, (SUB_TILE // CHUNK) * DK), BF16),
            pltpu.VMEM((N_HEADS, tm // CHUNK, DK, DV), BF16),
            pltpu.VMEM((tm, W_BRANCH), BF16),
            pltpu.VMEM((tm, W_BRANCH), BF16),
            pltpu.VMEM((tm, W_BRANCH), BF16),
            pltpu.VMEM((tm, W_BRANCH), BF16),
            pltpu.VMEM((tm, W_BRANCH), BF16),
            pltpu.VMEM((tm, W_BRANCH), BF16),
        ] + [pltpu.VMEM(s, BF16) for s in bf16_shapes] + [
            pltpu.SemaphoreType.DMA((WIN_SLOTS, 5)),
            pltpu.SemaphoreType.DMA((len(bf16_shapes),)),
        ],
        compiler_params=pltpu.CompilerParams(
            dimension_semantics=("arbitrary", "arbitrary"),
            vmem_limit_bytes=VMEM_LIMIT_BYTES),
        name="hgrn2_pool_prompt",
    )(x_prompt, norm_g, w_in[0], lb_logits, hgrn_norm_g, w_proj_a[0],
      w_pool[0].reshape(n_groups * G_B, G_B), pool_scale, w_proj_b[0], w_out[0],
      final_norm_g.reshape(1, D_MODEL))
    weights = (norm_g, win_b, lb_logits, hgrn_norm_g, wpa_b, wpool_b, pool_scale, wpb_b, wout_b,
               final_norm_g.reshape(1, D_MODEL))

    rows = dec_batch * DEC_SEQ
    pool_in = jnp.transpose(state_pool[0], (1, 0, 2))
    ng, win_b, lbl, hg, wpa_b, wpool_b, ps, wpb_b, wout_b, fg = weights
    pseq = PREP_SEQS
    prows = pseq * DEC_SEQ
    n_prep = dec_batch // pseq
    bb = BB_SAMPLE

    def prep_blk(s):
        return jnp.minimum((s + PREP_PARTS - 1) // PREP_PARTS, n_prep - 1)

    def finish_blk(s):
        return jnp.minimum(jnp.maximum(s - 2, 0) // PREP_PARTS, n_prep - 1)

    s_s, y_s, pool_out = pl.pallas_call(
        functools.partial(_decode_kernel, n_prep=n_prep),
        grid=(1 + dec_batch // bb,),
        in_specs=[
            pl.BlockSpec((pseq, DEC_SEQ, D_MODEL), lambda s: (prep_blk(s), 0, 0)),
            pl.BlockSpec((POOL_BUF, pseq, W_BRANCH), lambda s: (0, prep_blk(s), 0),
                         pipeline_mode=pl.Buffered(1)),
            hbm,
            _const_spec(ng.shape), _const_spec(win_b.shape), _const_spec(lbl.shape),
            _const_spec(wpool_b.shape), _const_spec(ps.shape), _const_spec(wpb_b.shape),
            _const_spec(hg.shape), hbm, hbm,
            _const_spec(fg.shape),
        ],
        out_specs=[
            hbm,
            pl.BlockSpec((pseq, DEC_SEQ, D_MODEL), lambda s: (finish_blk(s), 0, 0)),
            pl.BlockSpec((POOL_BUF, pseq, W_BRANCH), lambda s: (0, prep_blk(s), 0)),
        ],
        out_shape=[
            jax.ShapeDtypeStruct((dec_batch, N_HEADS, DK, DV), F32),
            jax.ShapeDtypeStruct(x_sample.shape, F32),
            jax.ShapeDtypeStruct((POOL_BUF, dec_batch, W_BRANCH), F32),
        ],
        scratch_shapes=[
            pltpu.VMEM((N_HEADS, 2, rows, DK), BF16),
            pltpu.VMEM((N_HEADS, 2, rows, DV), BF16),
            pltpu.VMEM((rows, W_BRANCH), BF16),
            pltpu.VMEM((rows, W_BRANCH), F32),
            pltpu.VMEM((rows, W_BRANCH), F32),
            pltpu.VMEM((rows, D_MODEL), F32),
            pltpu.VMEM((rows, D_MODEL), F32),
            pltpu.VMEM((rows, D_MODEL), F32),
            pltpu.VMEM((prows, D_IN), F32),
            pltpu.VMEM((len(POOL_WINDOWS), prows, G_B), F32),
            pltpu.VMEM((len(POOL_WINDOWS), prows, G_B), F32),
            pltpu.VMEM((prows, W_BRANCH), BF16),
            pltpu.VMEM((prows, W_BRANCH), BF16),
            pltpu.VMEM((2, bb, N_HEADS, DK, DV), F32),
            pltpu.VMEM((2, bb, N_HEADS, DK, DV), F32),
            pltpu.VMEM(wpa_b.shape, BF16), pltpu.VMEM(wout_b.shape, BF16),
            pltpu.SemaphoreType.DMA((2, STATE_DMAS)),
            pltpu.SemaphoreType.DMA((2, STATE_DMAS)),
            pltpu.SemaphoreType.DMA((2,)),
        ],
        compiler_params=pltpu.CompilerParams(
            dimension_semantics=("arbitrary",), vmem_limit_bytes=DECODE_VMEM_LIMIT_BYTES),
        name="hgrn2_pool_decode",
    )(x_sample, pool_in, state_hgrn[0], ng, win_b, lbl, wpool_b, ps, wpb_b,
      hg, wpa_b, wout_b, fg)

    p_p = jnp.transpose(p_p, (1, 0, 2))
    p_s = jnp.transpose(pool_out, (1, 0, 2))
    return (y_p, y_s, s_p[None], p_p[None], s_s[None], p_s[None])
```

```python
import functools

import jax
import jax.numpy as jnp
from jax import lax
from jax.experimental import pallas as pl
from jax.experimental.pallas import tpu as pltpu

F32 = jnp.float32
BF16 = jnp.bfloat16

D_MODEL = 1024
W_BRANCH = 512
N_HEADS = 4
DK = 128
DV = 128
CHUNK = 32
POOL_WINDOWS = (2, 4, 8, 16)
G_B = 128
POOL_BUF = 15
PAST_LEN = 16384
EPS = 1e-6
D_IN = 4 * W_BRANCH + 2 * W_BRANCH + 2 * D_MODEL
OFF_Q, OFF_F, OFF_I, OFF_GA = 0, 512, 1024, 1536
OFF_U, OFF_GB, OFF_MA, OFF_MB = 2048, 2560, 3072, 4096

SUBLANES = 8
VMEM_LIMIT_BYTES = 56 * 1024 * 1024
DECODE_VMEM_LIMIT_BYTES = 60 * 1024 * 1024

TM_PROMPT = 512
WIN_SLOTS = 4
SUB_TILE = 256
PREP_SEQS = 64
BB_SAMPLE = 16
STATE_DMAS = 2
DEC_SEQ = 4
PAIR_ROWS = 2 * DEC_SEQ
PREP_PARTS = 4


def _rms(x, g):
    ms = jnp.mean(x * x, axis=-1, keepdims=True)
    return x * lax.rsqrt(ms + EPS) * g


def _lower_bound(lbl_ref):
    l0 = lbl_ref[0:1, :]
    l1 = lbl_ref[1:2, :]
    m = jnp.maximum(l0, l1)
    e0 = jnp.exp(l0 - m)
    e1 = jnp.exp(l1 - m)
    return e0 / (e0 + e1)


def _in_proj(x, ng_ref, win_ref, z_ref, c0=0, c1=D_IN):
    h = _rms(x, ng_ref[...]).astype(BF16)
    z_ref[:, c0:c1] = jnp.dot(h, win_ref[:, c0:c1], preferred_element_type=F32)


def _hgrn_inputs(z_ref, lb, hh):
    c = hh * DK
    q = z_ref[:, OFF_Q + c:OFF_Q + c + DK]
    f = z_ref[:, OFF_F + c:OFF_F + c + DK]
    v = z_ref[:, OFF_I + c:OFF_I + c + DK]
    lbh = lb[:, c:c + DK]
    fg = lbh + (1.0 - lbh) * jax.nn.sigmoid(f)
    logf = jnp.log(fg)
    k = 1.0 - fg
    qf = jax.nn.silu(q) * (DK ** -0.5)
    return qf, k, v, logf


def _group_cumsum(x, group):
    row = lax.broadcasted_iota(jnp.int32, x.shape, 0) & (group - 1)
    s = 1
    while s < group:
        x = x + jnp.where(row >= s, pltpu.roll(x, s, 0), 0.0)
        s *= 2
    return x


def _hgrn_epilogue(o, z_ref, hg_ref, hh, oa_ref):
    c = hh * DK
    ga = z_ref[:, OFF_GA + c:OFF_GA + c + DK]
    on = _rms(o, hg_ref[...]) * jax.nn.silu(ga)
    oa_ref[:, c:c + DK] = on.astype(BF16)


def _pool_epilogue(pooled, z_ref, g, wpool_ref, ps_ref, yb_ref):
    c = g * G_B
    mixed = jnp.dot(pooled.astype(BF16), wpool_ref[g], preferred_element_type=F32)
    gb = z_ref[:, OFF_GB + c:OFF_GB + c + G_B]
    yb = mixed * ps_ref[:, c:c + G_B] * jax.nn.silu(gb)
    yb_ref[:, c:c + G_B] = yb.astype(BF16)


def _weight_copies(hbm_refs, stage_ref, sem):
    win_hbm, wpa_hbm, wpool_hbm, wpb_hbm, wout_hbm = hbm_refs
    rows = stage_ref.shape[0]
    blk = rows // WIN_SLOTS
    n_small = rows // blk

    def block(k):
        slot = k % WIN_SLOTS

        def dst(c0, c1):
            return stage_ref.at[slot * blk:(slot + 1) * blk, c0:c1]
        if k < n_small:
            r0 = k * blk
            pairs = [(wpa_hbm.at[r0:r0 + blk, :], dst(0, D_MODEL)),
                     (wpb_hbm.at[r0:r0 + blk, :], dst(D_MODEL, 2 * D_MODEL)),
                     (wout_hbm.at[r0:r0 + blk, :], dst(2 * D_MODEL, 3 * D_MODEL)),
                     (wout_hbm.at[rows + r0:rows + r0 + blk, :], dst(3 * D_MODEL, 4 * D_MODEL)),
                     (wpool_hbm.at[r0:r0 + blk, :], dst(4 * D_MODEL, 4 * D_MODEL + G_B))]
        else:
            r0 = (k - n_small) * blk
            pairs = [(win_hbm.at[r0:r0 + blk, :], dst(0, D_IN))]
        return [pltpu.make_async_copy(src, d, sem.at[slot, i]) for i, (src, d) in enumerate(pairs)]
    return block, blk, n_small


def _weight_exports(vmem_refs, out_refs, sem):
    return [pltpu.make_async_copy(src, dst, sem.at[i])
            for i, (src, dst) in enumerate(zip(vmem_refs, out_refs))]


def _prompt_kernel(x_ref, ng_ref, win_hbm, lbl_ref, hg_ref, wpa_hbm, wpool_hbm, ps_ref,
                   wpb_hbm, wout_hbm, fg_ref,
                   y_ref, s_out_ref, p_out_ref, win_out, wpa_out, wpool_out, wpb_out, wout_out,
                   z_ref, st_ref, ext_ref, kx_ref, sn_ref, qs_ref, ks_ref, qd_ref, vb_ref,
                   oa_ref, yb_ref, win_ref, wpa_ref, wpool_ref, wpb_ref, wout_ref,
                   load_sem, export_sem):
    tm = TM_PROMPT
    sub = SUB_TILE
    n_sub = tm // sub
    nc = tm // CHUNK
    ncs = sub // CHUNK
    t = pl.program_id(1)
    first = t == 0
    hdr = 2 * SUBLANES
    bf16_weights = (win_ref, wpa_ref, wpool_ref, wpb_ref, wout_ref)
    exports = _weight_exports(bf16_weights, (win_out, wpa_out, wpool_out, wpb_out, wout_out),
                              export_sem)

    @pl.when((pl.program_id(0) == 0) & first)
    def _():
        block, blk, n_small = _weight_copies((win_hbm, wpa_hbm, wpool_hbm, wpb_hbm, wout_hbm),
                                             z_ref, load_sem)
        n_blk = n_small + D_MODEL // blk
        for k in range(WIN_SLOTS):
            for c in block(k):
                c.start()
        st_ref[...] = jnp.zeros_like(st_ref)
        ext_ref[...] = jnp.zeros_like(ext_ref)
        kx_ref[...] = jnp.zeros_like(kx_ref)
        for k in range(n_blk):
            for c in block(k):
                c.wait()
            slot = z_ref.at[(k % WIN_SLOTS) * blk:(k % WIN_SLOTS + 1) * blk, :]
            if k < n_small:
                r = slice(k * blk, (k + 1) * blk)
                wpa_ref[r, :] = slot[:, 0:D_MODEL].astype(BF16)
                wpb_ref[r, :] = slot[:, D_MODEL:2 * D_MODEL].astype(BF16)
                wout_ref[r, :] = slot[:, 2 * D_MODEL:3 * D_MODEL].astype(BF16)
                wout_ref[tm + k * blk:tm + (k + 1) * blk, :] = (
                    slot[:, 3 * D_MODEL:4 * D_MODEL].astype(BF16))
                wpool_ref[k] = slot[:, 4 * D_MODEL:4 * D_MODEL + G_B].astype(BF16)
            else:
                r = slice((k - n_small) * blk, (k - n_small + 1) * blk)
                win_ref[r, :] = slot[...].astype(BF16)
            if k + WIN_SLOTS < n_blk:
                for c in block(k + WIN_SLOTS):
                    c.start()
        for c in exports:
            c.start()

    @pl.when((pl.program_id(0) == pl.num_programs(0) - 1) & (t == pl.num_programs(1) - 1))
    def _():
        for c in exports:
            c.wait()

    x = x_ref[0]
    _in_proj(x, ng_ref, win_ref, z_ref)
    lb = _lower_bound(lbl_ref)

    ri = lax.broadcasted_iota(jnp.int32, (sub, sub), 0)
    ci = lax.broadcasted_iota(jnp.int32, (sub, sub), 1)
    causal = ((ri // CHUNK) == (ci // CHUNK)) & (ci <= ri)

    heads = range(N_HEADS)
    units = [(hh, s) for hh in heads for s in range(n_sub)]

    def rows(s):
        return slice(s * sub, (s + 1) * sub)

    def cols(hh):
        return slice(hh * DK, (hh + 1) * DK)

    dec = []
    for hh in heads:
        qf, k, v, logf = _hgrn_inputs(z_ref, lb, hh)
        b = _group_cumsum(logf, CHUNK)
        b3 = b.reshape(nc, CHUNK, DK)
        ref = b3[:, CHUNK // 2:CHUNK // 2 + 1, :]
        bl = b3[:, CHUNK - 1:CHUNK, :]
        q3 = qf.reshape(nc, CHUNK, DK)
        k3 = k.reshape(nc, CHUNK, DK)
        qs_ref[:, cols(hh)] = (q3 * jnp.exp(b3 - ref)).reshape(tm, DK).astype(BF16)
        ks_ref[:, cols(hh)] = (k3 * jnp.exp(ref - b3)).reshape(tm, DK).astype(BF16)
        qd_ref[:, cols(hh)] = (q3 * jnp.exp(b3)).reshape(tm, DK).astype(BF16)
        vb_ref[:, cols(hh)] = v.astype(BF16)
        kd = (k3 * jnp.exp(bl - b3)).reshape(tm, DK).astype(BF16)
        dec.append(jnp.exp(bl))
        for c in range(nc):
            cc = c % ncs
            kx_ref[hh, c * CHUNK:(c + 1) * CHUNK, cc * DK:(cc + 1) * DK] = (
                kd[c * CHUNK:(c + 1) * CHUNK, :])

    ext_ref[0:hdr, :] = jnp.where(first, 0.0, ext_ref[tm:tm + hdr, :])
    ext_ref[hdr:hdr + tm, :] = z_ref[:, OFF_U:OFF_U + W_BRANCH]
    pos1 = t * tm + lax.broadcasted_iota(jnp.int32, (tm, 1), 0) + 1
    for g, w in enumerate(POOL_WINDOWS):
        c = g * G_B
        s = ext_ref[:, c:c + G_B]
        sh = 1
        while sh < w:
            s = s + pltpu.roll(s, sh, 0)
            sh *= 2
        inv = jnp.where(pos1 >= w, 1.0 / w, 1.0 / pos1.astype(F32))
        pooled = s[hdr:, :] * inv - z_ref[:, OFF_U + c:OFF_U + c + G_B]
        _pool_epilogue(pooled, z_ref, g, wpool_ref, ps_ref, yb_ref)
    for j in range(POOL_BUF):
        r = hdr + tm - POOL_BUF + j
        p_out_ref[j, pl.ds(pl.program_id(0), 1), :] = ext_ref[r:r + 1, :]

    sc = {(hh, s): lax.dot_general(qs_ref[rows(s), cols(hh)], ks_ref[rows(s), cols(hh)],
                                   (((1,), (1,)), ((), ())), preferred_element_type=F32)
          for hh, s in units}
    ut = {(hh, s): lax.dot_general(vb_ref[rows(s), cols(hh)], kx_ref[hh, rows(s), :],
                                   (((0,), (0,)), ((), ())), preferred_element_type=F32)
          for hh, s in units}
    o = {(hh, s): jnp.dot(jnp.where(causal, sc[hh, s], 0.0).astype(BF16),
                          vb_ref[rows(s), cols(hh)], preferred_element_type=F32)
         for hh, s in units}
    for hh in heads:
        st = jnp.where(first, 0.0, st_ref[hh])
        for c in range(nc):
            sn_ref[hh, c] = st.T.astype(BF16)
            st = dec[hh][c] * st + ut[hh, c // ncs][:, (c % ncs) * DK:(c % ncs + 1) * DK]
        st_ref[hh] = st
        s_out_ref[0, hh] = st.T
    wcol = D_MODEL // N_HEADS
    y_b = []
    for hh in heads:
        o_inter = [jnp.dot(qd_ref[c * CHUNK:(c + 1) * CHUNK, cols(hh)], sn_ref[hh, c],
                           preferred_element_type=F32) for c in range(nc)]
        y_b.append(jnp.dot(yb_ref[...], wpb_ref[:, hh * wcol:(hh + 1) * wcol],
                           preferred_element_type=F32))
        o_hh = jnp.concatenate([o[hh, s] for s in range(n_sub)], axis=0)
        _hgrn_epilogue(o_hh + jnp.concatenate(o_inter, axis=0), z_ref, hg_ref, hh, oa_ref)
    gated_b = jax.nn.sigmoid(z_ref[:, OFF_MB:OFF_MB + D_MODEL]) * jnp.concatenate(y_b, axis=1)

    y_a = jnp.dot(oa_ref[...], wpa_ref[...], preferred_element_type=F32)
    merged = jax.nn.sigmoid(z_ref[:, OFF_MA:OFF_MA + D_MODEL]) * y_a + gated_b
    out = x + jnp.dot(merged.astype(BF16), wout_ref[...], preferred_element_type=F32)
    y_ref[0] = _rms(out, fg_ref[...])


def _shift_rows(x, k):
    n = x.shape[0]
    return x if k % n == 0 else pltpu.roll(x, (-k) % n, 0)


def _group_bcast(x, j, group):
    t = lax.broadcasted_iota(jnp.int32, x.shape, 0) & (group - 1)
    out = _shift_rows(x, j - (group - 1))
    for tt in range(group - 2, -1, -1):
        out = jnp.where(t == tt, _shift_rows(x, j - tt), out)
    return out


def _decode_prep_hgrn(heads, z_ref, lbl_ref, a_ref, bv_ref, qd_ref, oi_ref, ga_ref):
    rows = z_ref.shape[0]
    lb = _lower_bound(lbl_ref)
    r = lax.broadcasted_iota(jnp.int32, (rows, DK), 0)
    t = r & (DEC_SEQ - 1)
    even = (r & (PAIR_ROWS - 1)) < DEC_SEQ
    blk = SUB_TILE
    ri = lax.broadcasted_iota(jnp.int32, (blk, blk), 0)
    ci = lax.broadcasted_iota(jnp.int32, (blk, blk), 1)
    causal = ((ri // DEC_SEQ) == (ci // DEC_SEQ)) & (ci <= ri)

    for hh in heads:
        cs = slice(hh * DK, (hh + 1) * DK)
        qf, k, v, logf = _hgrn_inputs(z_ref, lb, hh)
        b = _group_cumsum(logf, DEC_SEQ)
        ref = _group_bcast(b, DEC_SEQ // 2, DEC_SEQ)
        bl = _group_bcast(b, DEC_SEQ - 1, DEC_SEQ)
        qs = (qf * jnp.exp(b - ref)).astype(BF16)
        ks = (k * jnp.exp(ref - b)).astype(BF16)
        kd = k * jnp.exp(bl - b)
        vb = v.astype(BF16)
        qd_ref[:, cs] = (qf * jnp.exp(b)).astype(BF16)
        for s in range(rows // blk):
            rs = slice(s * blk, (s + 1) * blk)
            sc = lax.dot_general(qs[rs], ks[rs], (((1,), (1,)), ((), ())),
                                 preferred_element_type=F32)
            oi_ref[rs, cs] = jnp.dot(jnp.where(causal, sc, 0.0).astype(BF16), vb[rs],
                                     preferred_element_type=F32)
        dec = jnp.exp(bl)
        d1 = dec.astype(BF16).astype(F32)
        d2 = (dec - d1).astype(BF16).astype(F32)
        d3 = (dec - d1 - d2).astype(BF16).astype(F32)
        tail = jnp.where(t == 0, d1, jnp.where(t == 1, d2, jnp.where(t == 2, d3, 0.0)))
        a_ref[hh, 0] = jnp.where(even, kd, _shift_rows(tail, -DEC_SEQ)).astype(BF16)
        a_ref[hh, 1] = jnp.where(even, _shift_rows(tail, DEC_SEQ), kd).astype(BF16)
        zero = jnp.zeros_like(vb)
        bv_ref[hh, 0] = jnp.where(even, vb, zero)
        bv_ref[hh, 1] = jnp.where(even, zero, vb)
        ga_ref[:, cs] = jax.nn.silu(z_ref[:, OFF_GA + hh * DK:OFF_GA + (hh + 1) * DK])


def _decode_prep_history(p_in_ref, rw_ref, p_out_ref):
    n_seq = p_in_ref.shape[1]
    for g, w in enumerate(POOL_WINDOWS):
        gc = slice(g * G_B, (g + 1) * G_B)
        acc = None
        suffix = {}
        for j in range(POOL_BUF - 1, -1, -1):
            e = p_in_ref[j, :, gc]
            acc = e if acc is None else acc + e
            suffix[j] = acc
        for tt in range(DEC_SEQ):
            j = POOL_BUF + 1 - w + tt
            rw_ref[g, pl.ds(tt, n_seq, stride=DEC_SEQ), :] = (
                suffix[j] if j < POOL_BUF else jnp.zeros((n_seq, G_B), F32))
    for j in range(POOL_BUF - DEC_SEQ):
        p_out_ref[j] = p_in_ref[j + DEC_SEQ]


def _decode_prep_pool(z_ref, wpool_ref, ps_ref, wpb_ref, ma_ref, gb_ref, p_out_ref,
                      rw_ref, u_ref, yb_ref):
    rows = z_ref.shape[0]
    n_seq = rows // DEC_SEQ
    t = lax.broadcasted_iota(jnp.int32, (rows, G_B), 0) & (DEC_SEQ - 1)
    for g, w in enumerate(POOL_WINDOWS):
        u = z_ref[:, OFF_U + g * G_B:OFF_U + (g + 1) * G_B]
        u_ref[g] = u
        cu = u
        sh = 1
        while sh < min(w, DEC_SEQ):
            cu = cu + jnp.where(t >= sh, _shift_rows(cu, -sh), 0.0)
            sh *= 2
        pooled = (rw_ref[g] + cu) * (1.0 / w) - u
        _pool_epilogue(pooled, z_ref, g, wpool_ref, ps_ref, yb_ref)
    gb_ref[...] = jax.nn.sigmoid(z_ref[:, OFF_MB:OFF_MB + D_MODEL]) * jnp.dot(
        yb_ref[...], wpb_ref[...], preferred_element_type=F32)
    ma_ref[...] = jax.nn.sigmoid(z_ref[:, OFF_MA:OFF_MA + D_MODEL])
    for tt in range(DEC_SEQ):
        for g in range(len(POOL_WINDOWS)):
            p_out_ref[POOL_BUF - DEC_SEQ + tt, :, g * G_B:(g + 1) * G_B] = (
                u_ref[g, pl.ds(tt, n_seq, stride=DEC_SEQ), :])


def _decode_prep(part, pool_copy, start_next_pool, x_ref, p_in_ref, ng_ref, win_ref, lbl_ref,
                 wpool_ref, ps_ref, wpb_ref,
                 a_ref, bv_ref, qd_ref, oi_ref, ga_ref, ma_ref, gb_ref, p_out_ref,
                 xs_ref, z_ref, rw_ref, u_ref, yb_ref):
    hgrn = functools.partial(_decode_prep_hgrn, z_ref=z_ref, lbl_ref=lbl_ref, a_ref=a_ref,
                             bv_ref=bv_ref, qd_ref=qd_ref, oi_ref=oi_ref, ga_ref=ga_ref)
    if part == 0:
        for i in range(x_ref.shape[0]):
            xs_ref[i * DEC_SEQ:(i + 1) * DEC_SEQ, :] = x_ref[i]
        _in_proj(xs_ref[...], ng_ref, win_ref, z_ref, 0, OFF_U)
        pool_copy.wait()
        _decode_prep_history(p_in_ref, rw_ref, p_out_ref)
    elif part == 1:
        _in_proj(xs_ref[...], ng_ref, win_ref, z_ref, OFF_U, D_IN)
    elif part == 2:
        hgrn(range(0, N_HEADS - 1))
    else:
        hgrn(range(N_HEADS - 1, N_HEADS))
        _decode_prep_pool(z_ref, wpool_ref, ps_ref, wpb_ref, ma_ref, gb_ref, p_out_ref,
                          rw_ref, u_ref, yb_ref)
        start_next_pool()


def _decode_state_step(a_ref, bv_ref, qd_ref, s_in_ref, s_out_ref, o_ref, stores):
    n_pairs = BB_SAMPLE // 2
    r8 = lax.broadcasted_iota(jnp.int32, (PAIR_ROWS, DV), 0)
    even = r8 < DEC_SEQ
    ones = (jnp.where((r8 >= DEC_SEQ) & (r8 < DEC_SEQ + 3), 1.0, 0.0).astype(BF16),
            jnp.where(r8 < 3, 1.0, 0.0).astype(BF16))
    pairs_per_store = n_pairs // len(stores)
    for g, store in enumerate(stores):
        for hh in range(N_HEADS):
            for lp in range(g * pairs_per_store, (g + 1) * pairs_per_store):
                rs = slice(lp * PAIR_ROWS, (lp + 1) * PAIR_ROWS)
                for e in range(2):
                    rhs = jnp.concatenate([bv_ref[hh, e, rs, :], ones[e]], axis=1)
                    upd = lax.dot_general(a_ref[hh, e, rs, :], rhs, (((0,), (0,)), ((), ())),
                                          preferred_element_type=F32)
                    s_out_ref[2 * lp + e, hh] = (upd[:, DV:2 * DV] * s_in_ref[2 * lp + e, hh]
                                                 + upd[:, 0:DV])
        store.start()
    for hh in range(N_HEADS):
        cs = slice(hh * DK, (hh + 1) * DK)
        for lp in range(n_pairs):
            rs = slice(lp * PAIR_ROWS, (lp + 1) * PAIR_ROWS)
            q8 = qd_ref[rs, cs]
            o0 = jnp.dot(q8, s_in_ref[2 * lp, hh].astype(BF16), preferred_element_type=F32)
            o1 = jnp.dot(q8, s_in_ref[2 * lp + 1, hh].astype(BF16), preferred_element_type=F32)
            o_ref[rs, cs] = o_ref[rs, cs] + jnp.where(even, o0, o1)


def _decode_finish(o_ref, ga_ref, ma_ref, gb_ref, xs_ref, hg_ref, wpa_ref, wout_ref,
                   fg_ref, y_ref, oa_ref, *, n_seq):
    for hh in range(N_HEADS):
        cs = slice(hh * DK, (hh + 1) * DK)
        oa_ref[:, cs] = (_rms(o_ref[:, cs], hg_ref[...]) * ga_ref[:, cs]).astype(BF16)
    y_a = jnp.dot(oa_ref[...], wpa_ref[...], preferred_element_type=F32)
    merged = ma_ref[...] * y_a + gb_ref[...]
    out = xs_ref[...] + jnp.dot(merged.astype(BF16), wout_ref[...], preferred_element_type=F32)
    y = _rms(out, fg_ref[...])
    for i in range(n_seq):
        y_ref[i] = y[i * DEC_SEQ:(i + 1) * DEC_SEQ, :]


def _state_copies(hbm_ref, buf_ref, sem, blk, slot, to_hbm):
    n = BB_SAMPLE // STATE_DMAS
    copies = []
    for c in range(STATE_DMAS):
        in_hbm = hbm_ref.at[pl.ds(blk * BB_SAMPLE + c * n, n)]
        in_vmem = buf_ref.at[slot, pl.ds(c * n, n)]
        src, dst = (in_vmem, in_hbm) if to_hbm else (in_hbm, in_vmem)
        copies.append(pltpu.make_async_copy(src, dst, sem.at[slot, c]))
    return copies


def _decode_kernel(xp_ref, p_in_hbm, s_in_hbm,
                   ng_ref, win_ref, lbl_ref, wpool_ref, ps_ref, wpb_ref,
                   hg_ref, wpa_hbm, wout_hbm, fg_ref,
                   s_out_hbm, y_ref, p_out_ref,
                   a_all, bv_all, qd_all, o_all, ga_all, ma_all, gb_all, xs_all,
                   z_ref, rw_ref, u_ref, yb_ref, oa_ref, s_in_buf, s_out_buf, wpa_ref, wout_ref,
                   p_in_ref, in_sem, out_sem, w_sem, pool_sem, *, n_prep):
    s = pl.program_id(0)
    prows = PREP_SEQS * DEC_SEQ
    brows = BB_SAMPLE * DEC_SEQ
    n_state = n_prep * PREP_PARTS
    late_weights = [pltpu.make_async_copy(src, dst, w_sem.at[k]) for k, (src, dst) in
                    enumerate(((wpa_hbm, wpa_ref), (wout_hbm, wout_ref)))]

    def load(blk, slot):
        return _state_copies(s_in_hbm, s_in_buf, in_sem, blk, slot, to_hbm=False)

    def store(blk, slot):
        return _state_copies(s_out_hbm, s_out_buf, out_sem, blk, slot, to_hbm=True)

    def block_rows(blk):
        start = blk * prows
        return pl.ds(start if isinstance(start, int) else pl.multiple_of(start, prows), prows)

    def pool_load(blk):
        return pltpu.make_async_copy(p_in_hbm.at[:, pl.ds(blk * PREP_SEQS, PREP_SEQS), :],
                                     p_in_ref, pool_sem.at[0])

    def prep(part, blk):
        r = block_rows(blk)

        def start_next_pool():
            nxt = blk + 1
            if isinstance(nxt, int):
                if nxt < n_prep:
                    pool_load(nxt).start()
            else:
                pl.when(nxt < n_prep)(lambda: pool_load(nxt).start())
        _decode_prep(part, pool_load(blk), start_next_pool,
                     xp_ref, p_in_ref, ng_ref, win_ref, lbl_ref, wpool_ref, ps_ref, wpb_ref,
                     a_all.at[:, :, r, :], bv_all.at[:, :, r, :], qd_all.at[r, :],
                     o_all.at[r, :], ga_all.at[r, :], ma_all.at[r, :], gb_all.at[r, :],
                     p_out_ref, xs_all.at[r, :], z_ref, rw_ref, u_ref, yb_ref)

    def finish(blk):
        f = block_rows(blk)
        _decode_finish(o_all.at[f, :], ga_all.at[f, :], ma_all.at[f, :], gb_all.at[f, :],
                       xs_all.at[f, :], hg_ref, wpa_ref, wout_ref, fg_ref, y_ref, oa_ref,
                       n_seq=PREP_SEQS)

    i = s - 1
    blk = lax.div(i, PREP_PARTS)
    part_now = lax.rem(i, PREP_PARTS)
    slot = lax.rem(i, 2)

    @pl.when(s == 0)
    def _():
        pool_load(0).start()
        for b in range(2):
            for c in load(b, b):
                c.start()
        for c in late_weights:
            c.start()
        for part in range(PREP_PARTS):
            prep(part, 0)
        for c in late_weights:
            c.wait()

    @pl.when(s > 0)
    def _():
        @pl.when((i > 0) & (i + 1 < n_state))
        def _():
            for c in load(i + 1, 1 - slot):
                c.start()
        for c in load(i, slot):
            c.wait()

        @pl.when(i >= 2)
        def _():
            for c in store(i - 2, slot):
                c.wait()
        r = pl.ds(pl.multiple_of(i * brows, brows), brows)
        _decode_state_step(a_all.at[:, :, r, :], bv_all.at[:, :, r, :], qd_all.at[r, :],
                           s_in_buf.at[slot], s_out_buf.at[slot], o_all.at[r, :],
                           stores=store(i, slot))

    for part in range(PREP_PARTS):
        @pl.when((part_now == part) & (blk + 1 < n_prep))
        def _():
            prep(part, blk + 1)

    @pl.when((part_now == 0) & (i > 0))
    def _():
        finish(blk - 1)

    @pl.when(i == n_state - 1)
    def _():
        finish(n_prep - 1)
        for c in store(i - 1, 1 - slot) + store(i, slot):
            c.wait()


def _const_spec(shape):
    n = len(shape)
    return pl.BlockSpec(shape, lambda *_: (0,) * n, pipeline_mode=pl.Buffered(1))


def kernel(x_prompt, x_sample, state_hgrn, state_pool, norm_g, w_in, lb_logits, hgrn_norm_g,
           w_proj_a, w_pool, pool_scale, w_proj_b, w_out, final_norm_g):
    batch, seq, _ = x_prompt.shape
    dec_batch, dec_seq, _ = x_sample.shape
    assert norm_g.shape[0] == 1 and lb_logits.shape[0] == 2, "single-layer decoder only"
    assert seq % TM_PROMPT == 0 and TM_PROMPT % SUB_TILE == 0 and dec_batch % BB_SAMPLE == 0
    assert dec_seq == DEC_SEQ and PAIR_ROWS == SUBLANES and BB_SAMPLE % 2 == 0
    assert dec_batch % PREP_SEQS == 0 and (PREP_SEQS * DEC_SEQ) % SUB_TILE == 0
    assert PREP_SEQS == PREP_PARTS * BB_SAMPLE and BB_SAMPLE % STATE_DMAS == 0
    assert PAST_LEN >= max(POOL_WINDOWS)

    n_groups = len(POOL_WINDOWS)
    tm = TM_PROMPT
    nt = seq // tm
    assert w_in.shape[1:] == (2 * tm, D_IN) and w_proj_a.shape[1] == tm, "weight staging layout"
    assert tm % (WIN_SLOTS * 2 * SUBLANES) == 0, "bf16-tile-aligned w_in staging blocks"
    assert tm // WIN_SLOTS == G_B, "a staging block of the small weights carries one w_pool group"
    hbm = pl.BlockSpec(memory_space=pl.ANY)
    bf16_shapes = [(D_MODEL, D_IN), (W_BRANCH, D_MODEL), (n_groups, G_B, G_B),
                   (W_BRANCH, D_MODEL), (D_MODEL, D_MODEL)]
    y_p, s_p, p_p, win_b, wpa_b, wpool_b, wpb_b, wout_b = pl.pallas_call(
        _prompt_kernel,
        grid=(batch, nt),
        in_specs=[pl.BlockSpec((1, tm, D_MODEL), lambda b, t: (b, t, 0)),
                  _const_spec((1, D_MODEL)), hbm, _const_spec((2, W_BRANCH)), _const_spec((1, DV)),
                  hbm, hbm, _const_spec((1, W_BRANCH)), hbm, hbm, _const_spec((1, D_MODEL))],
        out_specs=[
            pl.BlockSpec((1, tm, D_MODEL), lambda b, t: (b, t, 0)),
            pl.BlockSpec((1, N_HEADS, DK, DV), lambda b, t: (b, 0, 0, 0)),
            pl.BlockSpec((POOL_BUF, batch, W_BRANCH), lambda b, t: (0, 0, 0)),
        ] + [hbm] * len(bf16_shapes),
        out_shape=[
            jax.ShapeDtypeStruct((batch, seq, D_MODEL), F32),
            jax.ShapeDtypeStruct((batch, N_HEADS, DK, DV), F32),
            jax.ShapeDtypeStruct((POOL_BUF, batch, W_BRANCH), F32),
        ] + [jax.ShapeDtypeStruct(s, BF16) for s in bf16_shapes],
        scratch_shapes=[
            pltpu.VMEM((tm, D_IN), F32),
            pltpu.VMEM((N_HEADS, DV, DK), F32),
            pltpu.VMEM((tm + 2 * SUBLANES, W_BRANCH), F32),
            pltpu.VMEM((N_HEADS, tm, (SUB_TILE // CHUNK) * DK), BF16),
            pltpu.VMEM((N_HEADS, tm // CHUNK, DK, DV), BF16),
            pltpu.VMEM((tm, W_BRANCH), BF16),
            pltpu.VMEM((tm, W_BRANCH), BF16),
            pltpu.VMEM((tm, W_BRANCH), BF16),
            pltpu.VMEM((tm, W_BRANCH), BF16),
            pltpu.VMEM((tm, W_BRANCH), BF16),
            pltpu.VMEM((tm, W_BRANCH), BF16),
        ] + [pltpu.VMEM(s, BF16) for s in bf16_shapes] + [
            pltpu.SemaphoreType.DMA((WIN_SLOTS, 5)),
            pltpu.SemaphoreType.DMA((len(bf16_shapes),)),
        ],
        compiler_params=pltpu.CompilerParams(
            dimension_semantics=("arbitrary", "arbitrary"),
            vmem_limit_bytes=VMEM_LIMIT_BYTES),
        name="hgrn2_pool_prompt",
    )(x_prompt, norm_g, w_in[0], lb_logits, hgrn_norm_g, w_proj_a[0],
      w_pool[0].reshape(n_groups * G_B, G_B), pool_scale, w_proj_b[0], w_out[0],
      final_norm_g.reshape(1, D_MODEL))
    weights = (norm_g, win_b, lb_logits, hgrn_norm_g, wpa_b, wpool_b, pool_scale, wpb_b, wout_b,
               final_norm_g.reshape(1, D_MODEL))

    rows = dec_batch * DEC_SEQ
    pool_in = jnp.transpose(state_pool[0], (1, 0, 2))
    ng, win_b, lbl, hg, wpa_b, wpool_b, ps, wpb_b, wout_b, fg = weights
    pseq = PREP_SEQS
    prows = pseq * DEC_SEQ
    n_prep = dec_batch // pseq
    bb = BB_SAMPLE

    def prep_blk(s):
        return jnp.minimum((s + PREP_PARTS - 1) // PREP_PARTS, n_prep - 1)

    def finish_blk(s):
        return jnp.minimum(jnp.maximum(s - 2, 0) // PREP_PARTS, n_prep - 1)

    s_s, y_s, pool_out = pl.pallas_call(
        functools.partial(_decode_kernel, n_prep=n_prep),
        grid=(1 + dec_batch // bb,),
        in_specs=[
            pl.BlockSpec((pseq, DEC_SEQ, D_MODEL), lambda s: (prep_blk(s), 0, 0)),
            hbm,
            hbm,
            _const_spec(ng.shape), _const_spec(win_b.shape), _const_spec(lbl.shape),
            _const_spec(wpool_b.shape), _const_spec(ps.shape), _const_spec(wpb_b.shape),
            _const_spec(hg.shape), hbm, hbm,
            _const_spec(fg.shape),
        ],
        out_specs=[
            hbm,
            pl.BlockSpec((pseq, DEC_SEQ, D_MODEL), lambda s: (finish_blk(s), 0, 0)),
            pl.BlockSpec((POOL_BUF, pseq, W_BRANCH), lambda s: (0, prep_blk(s), 0)),
        ],
        out_shape=[
            jax.ShapeDtypeStruct((dec_batch, N_HEADS, DK, DV), F32),
            jax.ShapeDtypeStruct(x_sample.shape, F32),
            jax.ShapeDtypeStruct((POOL_BUF, dec_batch, W_BRANCH), F32),
        ],
        scratch_shapes=[
            pltpu.VMEM((N_HEADS, 2, rows, DK), BF16),
            pltpu.VMEM((N_HEADS, 2, rows, DV), BF16),
            pltpu.VMEM((rows, W_BRANCH), BF16),
            pltpu.VMEM((rows, W_BRANCH), F32),
            pltpu.VMEM((rows, W_BRANCH), F32),
            pltpu.VMEM((rows, D_MODEL), F32),
            pltpu.VMEM((rows, D_MODEL), F32),
            pltpu.VMEM((rows, D_MODEL), F32),
            pltpu.VMEM((prows, D_IN), F32),
            pltpu.VMEM((len(POOL_WINDOWS), prows, G_B), F32),
            pltpu.VMEM((len(POOL_WINDOWS), prows, G_B), F32),
            pltpu.VMEM((prows, W_BRANCH), BF16),
            pltpu.VMEM((prows, W_BRANCH), BF16),
            pltpu.VMEM((2, bb, N_HEADS, DK, DV), F32),
            pltpu.VMEM((2, bb, N_HEADS, DK, DV), F32),
            pltpu.VMEM(wpa_b.shape, BF16), pltpu.VMEM(wout_b.shape, BF16),
            pltpu.VMEM((POOL_BUF, pseq, W_BRANCH), F32),
            pltpu.SemaphoreType.DMA((2, STATE_DMAS)),
            pltpu.SemaphoreType.DMA((2, STATE_DMAS)),
            pltpu.SemaphoreType.DMA((2,)),
            pltpu.SemaphoreType.DMA((1,)),
        ],
        compiler_params=pltpu.CompilerParams(
            dimension_semantics=("arbitrary",), vmem_limit_bytes=DECODE_VMEM_LIMIT_BYTES),
        name="hgrn2_pool_decode",
    )(x_sample, pool_in, state_hgrn[0], ng, win_b, lbl, wpool_b, ps, wpb_b,
      hg, wpa_b, wout_b, fg)

    p_p = jnp.transpose(p_p, (1, 0, 2))
    p_s = jnp.transpose(pool_out, (1, 0, 2))
    return (y_p, y_s, s_p[None], p_p[None], s_s[None], p_s[None])
```

```python
import functools

import jax
import jax.numpy as jnp
from jax import lax
from jax.experimental import pallas as pl
from jax.experimental.pallas import tpu as pltpu

F32 = jnp.float32
BF16 = jnp.bfloat16

D_MODEL = 1024
W_BRANCH = 512
N_HEADS = 4
DK = 128
DV = 128
CHUNK = 32
POOL_WINDOWS = (2, 4, 8, 16)
G_B = 128
POOL_BUF = 15
PAST_LEN = 16384
EPS = 1e-6
D_IN = 4 * W_BRANCH + 2 * W_BRANCH + 2 * D_MODEL
OFF_Q, OFF_F, OFF_I, OFF_GA = 0, 512, 1024, 1536
OFF_U, OFF_GB, OFF_MA, OFF_MB = 2048, 2560, 3072, 4096

SUBLANES = 8
VMEM_LIMIT_BYTES = 56 * 1024 * 1024
DECODE_VMEM_LIMIT_BYTES = 60 * 1024 * 1024

TM_PROMPT = 512
WIN_SLOTS = 4
SUB_TILE = 256
PREP_SEQS = 64
BB_SAMPLE = 16
STATE_DMAS = 2
DEC_SEQ = 4
PAIR_ROWS = 2 * DEC_SEQ
PREP_PARTS = 4


def _rms(x, g):
    ms = jnp.mean(x * x, axis=-1, keepdims=True)
    return x * lax.rsqrt(ms + EPS) * g


def _lower_bound(lbl_ref):
    l0 = lbl_ref[0:1, :]
    l1 = lbl_ref[1:2, :]
    m = jnp.maximum(l0, l1)
    e0 = jnp.exp(l0 - m)
    e1 = jnp.exp(l1 - m)
    return e0 / (e0 + e1)


def _in_proj(x, ng_ref, win_ref, z_ref, c0=0, c1=D_IN):
    h = _rms(x, ng_ref[...]).astype(BF16)
    z_ref[:, c0:c1] = jnp.dot(h, win_ref[:, c0:c1], preferred_element_type=F32)


def _hgrn_inputs(z_ref, lb, hh):
    c = hh * DK
    q = z_ref[:, OFF_Q + c:OFF_Q + c + DK]
    f = z_ref[:, OFF_F + c:OFF_F + c + DK]
    v = z_ref[:, OFF_I + c:OFF_I + c + DK]
    lbh = lb[:, c:c + DK]
    fg = lbh + (1.0 - lbh) * jax.nn.sigmoid(f)
    logf = jnp.log(fg)
    k = 1.0 - fg
    qf = jax.nn.silu(q) * (DK ** -0.5)
    return qf, k, v, logf


def _group_cumsum(x, group):
    row = lax.broadcasted_iota(jnp.int32, x.shape, 0) & (group - 1)
    s = 1
    while s < group:
        x = x + jnp.where(row >= s, pltpu.roll(x, s, 0), 0.0)
        s *= 2
    return x


def _hgrn_epilogue(o, z_ref, hg_ref, hh, oa_ref):
    c = hh * DK
    ga = z_ref[:, OFF_GA + c:OFF_GA + c + DK]
    on = _rms(o, hg_ref[...]) * jax.nn.silu(ga)
    oa_ref[:, c:c + DK] = on.astype(BF16)


def _pool_epilogue(pooled, z_ref, g, wpool_ref, ps_ref, yb_ref):
    c = g * G_B
    mixed = jnp.dot(pooled.astype(BF16), wpool_ref[g], preferred_element_type=F32)
    gb = z_ref[:, OFF_GB + c:OFF_GB + c + G_B]
    yb = mixed * ps_ref[:, c:c + G_B] * jax.nn.silu(gb)
    yb_ref[:, c:c + G_B] = yb.astype(BF16)


def _weight_copies(hbm_refs, stage_ref, sem):
    win_hbm, wpa_hbm, wpool_hbm, wpb_hbm, wout_hbm = hbm_refs
    rows = stage_ref.shape[0]
    blk = rows // WIN_SLOTS
    n_small = rows // blk

    def block(k):
        slot = k % WIN_SLOTS

        def dst(c0, c1):
            return stage_ref.at[slot * blk:(slot + 1) * blk, c0:c1]
        if k < n_small:
            r0 = k * blk
            pairs = [(wpa_hbm.at[r0:r0 + blk, :], dst(0, D_MODEL)),
                     (wpb_hbm.at[r0:r0 + blk, :], dst(D_MODEL, 2 * D_MODEL)),
                     (wout_hbm.at[r0:r0 + blk, :], dst(2 * D_MODEL, 3 * D_MODEL)),
                     (wout_hbm.at[rows + r0:rows + r0 + blk, :], dst(3 * D_MODEL, 4 * D_MODEL)),
                     (wpool_hbm.at[r0:r0 + blk, :], dst(4 * D_MODEL, 4 * D_MODEL + G_B))]
        else:
            r0 = (k - n_small) * blk
            pairs = [(win_hbm.at[r0:r0 + blk, :], dst(0, D_IN))]
        return [pltpu.make_async_copy(src, d, sem.at[slot, i]) for i, (src, d) in enumerate(pairs)]
    return block, blk, n_small


def _weight_exports(vmem_refs, out_refs, sem):
    return [pltpu.make_async_copy(src, dst, sem.at[i])
            for i, (src, dst) in enumerate(zip(vmem_refs, out_refs))]


def _prompt_kernel(x_ref, ng_ref, win_hbm, lbl_ref, hg_ref, wpa_hbm, wpool_hbm, ps_ref,
                   wpb_hbm, wout_hbm, fg_ref,
                   y_ref, s_out_ref, p_out_ref, win_out, wpa_out, wpool_out, wpb_out, wout_out,
                   z_ref, st_ref, ext_ref, kx_ref, sn_ref, qs_ref, ks_ref, qd_ref, vb_ref,
                   oa_ref, yb_ref, win_ref, wpa_ref, wpool_ref, wpb_ref, wout_ref,
                   load_sem, export_sem):
    tm = TM_PROMPT
    sub = SUB_TILE
    n_sub = tm // sub
    nc = tm // CHUNK
    ncs = sub // CHUNK
    t = pl.program_id(1)
    first = t == 0
    hdr = 2 * SUBLANES
    bf16_weights = (win_ref, wpa_ref, wpool_ref, wpb_ref, wout_ref)
    exports = _weight_exports(bf16_weights, (win_out, wpa_out, wpool_out, wpb_out, wout_out),
                              export_sem)

    @pl.when((pl.program_id(0) == 0) & first)
    def _():
        block, blk, n_small = _weight_copies((win_hbm, wpa_hbm, wpool_hbm, wpb_hbm, wout_hbm),
                                             z_ref, load_sem)
        n_blk = n_small + D_MODEL // blk
        for k in range(WIN_SLOTS):
            for c in block(k):
                c.start()
        st_ref[...] = jnp.zeros_like(st_ref)
        ext_ref[...] = jnp.zeros_like(ext_ref)
        kx_ref[...] = jnp.zeros_like(kx_ref)
        for k in range(n_blk):
            for c in block(k):
                c.wait()
            slot = z_ref.at[(k % WIN_SLOTS) * blk:(k % WIN_SLOTS + 1) * blk, :]
            if k < n_small:
                r = slice(k * blk, (k + 1) * blk)
                wpa_ref[r, :] = slot[:, 0:D_MODEL].astype(BF16)
                wpb_ref[r, :] = slot[:, D_MODEL:2 * D_MODEL].astype(BF16)
                wout_ref[r, :] = slot[:, 2 * D_MODEL:3 * D_MODEL].astype(BF16)
                wout_ref[tm + k * blk:tm + (k + 1) * blk, :] = (
                    slot[:, 3 * D_MODEL:4 * D_MODEL].astype(BF16))
                wpool_ref[k] = slot[:, 4 * D_MODEL:4 * D_MODEL + G_B].astype(BF16)
            else:
                r = slice((k - n_small) * blk, (k - n_small + 1) * blk)
                win_ref[r, :] = slot[...].astype(BF16)
            if k + WIN_SLOTS < n_blk:
                for c in block(k + WIN_SLOTS):
                    c.start()
        for c in exports:
            c.start()

    @pl.when((pl.program_id(0) == pl.num_programs(0) - 1) & (t == pl.num_programs(1) - 1))
    def _():
        for c in exports:
            c.wait()

    x = x_ref[0]
    _in_proj(x, ng_ref, win_ref, z_ref)
    lb = _lower_bound(lbl_ref)

    ri = lax.broadcasted_iota(jnp.int32, (sub, sub), 0)
    ci = lax.broadcasted_iota(jnp.int32, (sub, sub), 1)
    causal = ((ri // CHUNK) == (ci // CHUNK)) & (ci <= ri)

    heads = range(N_HEADS)
    units = [(hh, s) for hh in heads for s in range(n_sub)]

    def rows(s):
        return slice(s * sub, (s + 1) * sub)

    def cols(hh):
        return slice(hh * DK, (hh + 1) * DK)

    dec = []
    for hh in heads:
        qf, k, v, logf = _hgrn_inputs(z_ref, lb, hh)
        b = _group_cumsum(logf, CHUNK)
        b3 = b.reshape(nc, CHUNK, DK)
        ref = b3[:, CHUNK // 2:CHUNK // 2 + 1, :]
        bl = b3[:, CHUNK - 1:CHUNK, :]
        q3 = qf.reshape(nc, CHUNK, DK)
        k3 = k.reshape(nc, CHUNK, DK)
        qs_ref[:, cols(hh)] = (q3 * jnp.exp(b3 - ref)).reshape(tm, DK).astype(BF16)
        ks_ref[:, cols(hh)] = (k3 * jnp.exp(ref - b3)).reshape(tm, DK).astype(BF16)
        qd_ref[:, cols(hh)] = (q3 * jnp.exp(b3)).reshape(tm, DK).astype(BF16)
        vb_ref[:, cols(hh)] = v.astype(BF16)
        kd = (k3 * jnp.exp(bl - b3)).reshape(tm, DK).astype(BF16)
        dec.append(jnp.exp(bl))
        for c in range(nc):
            cc = c % ncs
            kx_ref[hh, c * CHUNK:(c + 1) * CHUNK, cc * DK:(cc + 1) * DK] = (
                kd[c * CHUNK:(c + 1) * CHUNK, :])

    ext_ref[0:hdr, :] = jnp.where(first, 0.0, ext_ref[tm:tm + hdr, :])
    ext_ref[hdr:hdr + tm, :] = z_ref[:, OFF_U:OFF_U + W_BRANCH]
    pos1 = t * tm + lax.broadcasted_iota(jnp.int32, (tm, 1), 0) + 1
    for g, w in enumerate(POOL_WINDOWS):
        c = g * G_B
        s = ext_ref[:, c:c + G_B]
        sh = 1
        while sh < w:
            s = s + pltpu.roll(s, sh, 0)
            sh *= 2
        inv = jnp.where(pos1 >= w, 1.0 / w, 1.0 / pos1.astype(F32))
        pooled = s[hdr:, :] * inv - z_ref[:, OFF_U + c:OFF_U + c + G_B]
        _pool_epilogue(pooled, z_ref, g, wpool_ref, ps_ref, yb_ref)
    for j in range(POOL_BUF):
        r = hdr + tm - POOL_BUF + j
        p_out_ref[j, pl.ds(pl.program_id(0), 1), :] = ext_ref[r:r + 1, :]

    sc = {(hh, s): lax.dot_general(qs_ref[rows(s), cols(hh)], ks_ref[rows(s), cols(hh)],
                                   (((1,), (1,)), ((), ())), preferred_element_type=F32)
          for hh, s in units}
    ut = {(hh, s): lax.dot_general(vb_ref[rows(s), cols(hh)], kx_ref[hh, rows(s), :],
                                   (((0,), (0,)), ((), ())), preferred_element_type=F32)
          for hh, s in units}
    o = {(hh, s): jnp.dot(jnp.where(causal, sc[hh, s], 0.0).astype(BF16),
                          vb_ref[rows(s), cols(hh)], preferred_element_type=F32)
         for hh, s in units}
    for hh in heads:
        st = jnp.where(first, 0.0, st_ref[hh])
        for c in range(nc):
            sn_ref[hh, c] = st.T.astype(BF16)
            st = dec[hh][c] * st + ut[hh, c // ncs][:, (c % ncs) * DK:(c % ncs + 1) * DK]
        st_ref[hh] = st
        s_out_ref[0, hh] = st.T
    wcol = D_MODEL // N_HEADS
    y_b = []
    for hh in heads:
        o_inter = [jnp.dot(qd_ref[c * CHUNK:(c + 1) * CHUNK, cols(hh)], sn_ref[hh, c],
                           preferred_element_type=F32) for c in range(nc)]
        y_b.append(jnp.dot(yb_ref[...], wpb_ref[:, hh * wcol:(hh + 1) * wcol],
                           preferred_element_type=F32))
        o_hh = jnp.concatenate([o[hh, s] for s in range(n_sub)], axis=0)
        _hgrn_epilogue(o_hh + jnp.concatenate(o_inter, axis=0), z_ref, hg_ref, hh, oa_ref)
    gated_b = jax.nn.sigmoid(z_ref[:, OFF_MB:OFF_MB + D_MODEL]) * jnp.concatenate(y_b, axis=1)

    y_a = jnp.dot(oa_ref[...], wpa_ref[...], preferred_element_type=F32)
    merged = jax.nn.sigmoid(z_ref[:, OFF_MA:OFF_MA + D_MODEL]) * y_a + gated_b
    out = x + jnp.dot(merged.astype(BF16), wout_ref[...], preferred_element_type=F32)
    y_ref[0] = _rms(out, fg_ref[...])


def _shift_rows(x, k):
    n = x.shape[0]
    return x if k % n == 0 else pltpu.roll(x, (-k) % n, 0)


def _group_bcast(x, j, group):
    t = lax.broadcasted_iota(jnp.int32, x.shape, 0) & (group - 1)
    out = _shift_rows(x, j - (group - 1))
    for tt in range(group - 2, -1, -1):
        out = jnp.where(t == tt, _shift_rows(x, j - tt), out)
    return out


def _decode_prep_hgrn(heads, z_ref, lbl_ref, a_ref, bv_ref, qd_ref, oi_ref, ga_ref):
    rows = z_ref.shape[0]
    lb = _lower_bound(lbl_ref)
    r = lax.broadcasted_iota(jnp.int32, (rows, DK), 0)
    t = r & (DEC_SEQ - 1)
    even = (r & (PAIR_ROWS - 1)) < DEC_SEQ
    blk = SUB_TILE
    ri = lax.broadcasted_iota(jnp.int32, (blk, blk), 0)
    ci = lax.broadcasted_iota(jnp.int32, (blk, blk), 1)
    causal = ((ri // DEC_SEQ) == (ci // DEC_SEQ)) & (ci <= ri)

    for hh in heads:
        cs = slice(hh * DK, (hh + 1) * DK)
        qf, k, v, logf = _hgrn_inputs(z_ref, lb, hh)
        b = _group_cumsum(logf, DEC_SEQ)
        ref = _group_bcast(b, DEC_SEQ // 2, DEC_SEQ)
        bl = _group_bcast(b, DEC_SEQ - 1, DEC_SEQ)
        qs = (qf * jnp.exp(b - ref)).astype(BF16)
        ks = (k * jnp.exp(ref - b)).astype(BF16)
        kd = k * jnp.exp(bl - b)
        vb = v.astype(BF16)
        qd_ref[:, cs] = (qf * jnp.exp(b)).astype(BF16)
        for s in range(rows // blk):
            rs = slice(s * blk, (s + 1) * blk)
            sc = lax.dot_general(qs[rs], ks[rs], (((1,), (1,)), ((), ())),
                                 preferred_element_type=F32)
            oi_ref[rs, cs] = jnp.dot(jnp.where(causal, sc, 0.0).astype(BF16), vb[rs],
                                     preferred_element_type=F32)
        dec = jnp.exp(bl)
        d1 = dec.astype(BF16).astype(F32)
        d2 = (dec - d1).astype(BF16).astype(F32)
        d3 = (dec - d1 - d2).astype(BF16).astype(F32)
        tail = jnp.where(t == 0, d1, jnp.where(t == 1, d2, jnp.where(t == 2, d3, 0.0)))
        a_ref[hh, 0] = jnp.where(even, kd, _shift_rows(tail, -DEC_SEQ)).astype(BF16)
        a_ref[hh, 1] = jnp.where(even, _shift_rows(tail, DEC_SEQ), kd).astype(BF16)
        zero = jnp.zeros_like(vb)
        bv_ref[hh, 0] = jnp.where(even, vb, zero)
        bv_ref[hh, 1] = jnp.where(even, zero, vb)
        ga_ref[:, cs] = jax.nn.silu(z_ref[:, OFF_GA + hh * DK:OFF_GA + (hh + 1) * DK])


def _decode_prep_history(p_in_ref, rw_ref, p_out_ref):
    n_seq = p_in_ref.shape[1]
    for g, w in enumerate(POOL_WINDOWS):
        gc = slice(g * G_B, (g + 1) * G_B)
        acc = None
        suffix = {}
        for j in range(POOL_BUF - 1, -1, -1):
            e = p_in_ref[j, :, gc]
            acc = e if acc is None else acc + e
            suffix[j] = acc
        for tt in range(DEC_SEQ):
            j = POOL_BUF + 1 - w + tt
            rw_ref[g, pl.ds(tt, n_seq, stride=DEC_SEQ), :] = (
                suffix[j] if j < POOL_BUF else jnp.zeros((n_seq, G_B), F32))
    for j in range(POOL_BUF - DEC_SEQ):
        p_out_ref[j] = p_in_ref[j + DEC_SEQ]


def _decode_prep_pool(z_ref, wpool_ref, ps_ref, wpb_ref, ma_ref, gb_ref, p_out_ref,
                      rw_ref, u_ref, yb_ref):
    rows = z_ref.shape[0]
    n_seq = rows // DEC_SEQ
    t = lax.broadcasted_iota(jnp.int32, (rows, G_B), 0) & (DEC_SEQ - 1)
    for g, w in enumerate(POOL_WINDOWS):
        u = z_ref[:, OFF_U + g * G_B:OFF_U + (g + 1) * G_B]
        u_ref[g] = u
        cu = u
        sh = 1
        while sh < min(w, DEC_SEQ):
            cu = cu + jnp.where(t >= sh, _shift_rows(cu, -sh), 0.0)
            sh *= 2
        pooled = (rw_ref[g] + cu) * (1.0 / w) - u
        _pool_epilogue(pooled, z_ref, g, wpool_ref, ps_ref, yb_ref)
    gb_ref[...] = jax.nn.sigmoid(z_ref[:, OFF_MB:OFF_MB + D_MODEL]) * jnp.dot(
        yb_ref[...], wpb_ref[...], preferred_element_type=F32)
    ma_ref[...] = jax.nn.sigmoid(z_ref[:, OFF_MA:OFF_MA + D_MODEL])
    for tt in range(DEC_SEQ):
        for g in range(len(POOL_WINDOWS)):
            p_out_ref[POOL_BUF - DEC_SEQ + tt, :, g * G_B:(g + 1) * G_B] = (
                u_ref[g, pl.ds(tt, n_seq, stride=DEC_SEQ), :])


def _decode_prep(part, x_ref, p_in_ref, ng_ref, win_ref, lbl_ref, wpool_ref, ps_ref, wpb_ref,
                 a_ref, bv_ref, qd_ref, oi_ref, ga_ref, ma_ref, gb_ref, p_out_ref,
                 xs_ref, z_ref, rw_ref, u_ref, yb_ref):
    hgrn = functools.partial(_decode_prep_hgrn, z_ref=z_ref, lbl_ref=lbl_ref, a_ref=a_ref,
                             bv_ref=bv_ref, qd_ref=qd_ref, oi_ref=oi_ref, ga_ref=ga_ref)
    if part == 0:
        for i in range(x_ref.shape[0]):
            xs_ref[i * DEC_SEQ:(i + 1) * DEC_SEQ, :] = x_ref[i]
        _in_proj(xs_ref[...], ng_ref, win_ref, z_ref, 0, OFF_U)
        _decode_prep_history(p_in_ref, rw_ref, p_out_ref)
    elif part == 1:
        _in_proj(xs_ref[...], ng_ref, win_ref, z_ref, OFF_U, D_IN)
    elif part == 2:
        hgrn(range(0, N_HEADS - 1))
    else:
        hgrn(range(N_HEADS - 1, N_HEADS))
        _decode_prep_pool(z_ref, wpool_ref, ps_ref, wpb_ref, ma_ref, gb_ref, p_out_ref,
                          rw_ref, u_ref, yb_ref)


def _decode_state_step(a_ref, bv_ref, qd_ref, s_in_ref, s_out_ref, o_ref, stores):
    n_pairs = BB_SAMPLE // 2
    r8 = lax.broadcasted_iota(jnp.int32, (PAIR_ROWS, DV), 0)
    even = r8 < DEC_SEQ
    for hh in range(N_HEADS):
        cs = slice(hh * DK, (hh + 1) * DK)
        for lp in range(n_pairs):
            rs = slice(lp * PAIR_ROWS, (lp + 1) * PAIR_ROWS)
            q8 = qd_ref[rs, cs]
            o0 = jnp.dot(q8, s_in_ref[2 * lp, hh].astype(BF16), preferred_element_type=F32)
            o1 = jnp.dot(q8, s_in_ref[2 * lp + 1, hh].astype(BF16), preferred_element_type=F32)
            o_ref[rs, cs] = o_ref[rs, cs] + jnp.where(even, o0, o1)
    ones = (jnp.where((r8 >= DEC_SEQ) & (r8 < DEC_SEQ + 3), 1.0, 0.0).astype(BF16),
            jnp.where(r8 < 3, 1.0, 0.0).astype(BF16))
    pairs_per_store = n_pairs // len(stores)
    for g, store in enumerate(stores):
        for hh in range(N_HEADS):
            for lp in range(g * pairs_per_store, (g + 1) * pairs_per_store):
                rs = slice(lp * PAIR_ROWS, (lp + 1) * PAIR_ROWS)
                for e in range(2):
                    rhs = jnp.concatenate([bv_ref[hh, e, rs, :], ones[e]], axis=1)
                    upd = lax.dot_general(a_ref[hh, e, rs, :], rhs, (((0,), (0,)), ((), ())),
                                          preferred_element_type=F32)
                    s_out_ref[2 * lp + e, hh] = (upd[:, DV:2 * DV] * s_in_ref[2 * lp + e, hh]
                                                 + upd[:, 0:DV])
        store.start()


def _decode_finish(o_ref, ga_ref, ma_ref, gb_ref, xs_ref, hg_ref, wpa_ref, wout_ref,
                   fg_ref, y_ref, oa_ref, *, n_seq):
    for hh in range(N_HEADS):
        cs = slice(hh * DK, (hh + 1) * DK)
        oa_ref[:, cs] = (_rms(o_ref[:, cs], hg_ref[...]) * ga_ref[:, cs]).astype(BF16)
    y_a = jnp.dot(oa_ref[...], wpa_ref[...], preferred_element_type=F32)
    merged = ma_ref[...] * y_a + gb_ref[...]
    out = xs_ref[...] + jnp.dot(merged.astype(BF16), wout_ref[...], preferred_element_type=F32)
    y = _rms(out, fg_ref[...])
    for i in range(n_seq):
        y_ref[i] = y[i * DEC_SEQ:(i + 1) * DEC_SEQ, :]


def _state_copies(hbm_ref, buf_ref, sem, blk, slot, to_hbm):
    n = BB_SAMPLE // STATE_DMAS
    copies = []
    for c in range(STATE_DMAS):
        in_hbm = hbm_ref.at[pl.ds(blk * BB_SAMPLE + c * n, n)]
        in_vmem = buf_ref.at[slot, pl.ds(c * n, n)]
        src, dst = (in_vmem, in_hbm) if to_hbm else (in_hbm, in_vmem)
        copies.append(pltpu.make_async_copy(src, dst, sem.at[slot, c]))
    return copies


def _decode_kernel(xp_ref, p_in_ref, s_in_hbm,
                   ng_ref, win_ref, lbl_ref, wpool_ref, ps_ref, wpb_ref,
                   hg_ref, wpa_hbm, wout_hbm, fg_ref,
                   s_out_hbm, y_ref, p_out_ref,
                   a_all, bv_all, qd_all, o_all, ga_all, ma_all, gb_all, xs_all,
                   z_ref, rw_ref, u_ref, yb_ref, oa_ref, s_in_buf, s_out_buf, wpa_ref, wout_ref,
                   in_sem, out_sem, w_sem, *, n_prep):
    s = pl.program_id(0)
    prows = PREP_SEQS * DEC_SEQ
    brows = BB_SAMPLE * DEC_SEQ
    n_state = n_prep * PREP_PARTS
    late_weights = [pltpu.make_async_copy(src, dst, w_sem.at[k]) for k, (src, dst) in
                    enumerate(((wpa_hbm, wpa_ref), (wout_hbm, wout_ref)))]

    def load(blk, slot):
        return _state_copies(s_in_hbm, s_in_buf, in_sem, blk, slot, to_hbm=False)

    def store(blk, slot):
        return _state_copies(s_out_hbm, s_out_buf, out_sem, blk, slot, to_hbm=True)

    def block_rows(blk):
        start = blk * prows
        return pl.ds(start if isinstance(start, int) else pl.multiple_of(start, prows), prows)

    def prep(part, blk):
        r = block_rows(blk)
        _decode_prep(part, xp_ref, p_in_ref, ng_ref, win_ref, lbl_ref, wpool_ref, ps_ref, wpb_ref,
                     a_all.at[:, :, r, :], bv_all.at[:, :, r, :], qd_all.at[r, :],
                     o_all.at[r, :], ga_all.at[r, :], ma_all.at[r, :], gb_all.at[r, :],
                     p_out_ref, xs_all.at[r, :], z_ref, rw_ref, u_ref, yb_ref)

    def finish(blk):
        f = block_rows(blk)
        _decode_finish(o_all.at[f, :], ga_all.at[f, :], ma_all.at[f, :], gb_all.at[f, :],
                       xs_all.at[f, :], hg_ref, wpa_ref, wout_ref, fg_ref, y_ref, oa_ref,
                       n_seq=PREP_SEQS)

    i = s - 1
    blk = lax.div(i, PREP_PARTS)
    part_now = lax.rem(i, PREP_PARTS)
    slot = lax.rem(i, 2)

    @pl.when(s == 0)
    def _():
        for b in range(2):
            for c in load(b, b):
                c.start()
        for c in late_weights:
            c.start()
        for part in range(PREP_PARTS):
            prep(part, 0)
        for c in late_weights:
            c.wait()

    @pl.when(s > 0)
    def _():
        @pl.when((i > 0) & (i + 1 < n_state))
        def _():
            for c in load(i + 1, 1 - slot):
                c.start()
        for c in load(i, slot):
            c.wait()

        @pl.when(i >= 2)
        def _():
            for c in store(i - 2, slot):
                c.wait()
        r = pl.ds(pl.multiple_of(i * brows, brows), brows)
        _decode_state_step(a_all.at[:, :, r, :], bv_all.at[:, :, r, :], qd_all.at[r, :],
                           s_in_buf.at[slot], s_out_buf.at[slot], o_all.at[r, :],
                           stores=store(i, slot))

    for part in range(PREP_PARTS):
        @pl.when((part_now == part) & (blk + 1 < n_prep))
        def _():
            prep(part, blk + 1)

    @pl.when((part_now == 0) & (i > 0))
    def _():
        finish(blk - 1)

    @pl.when(i == n_state - 1)
    def _():
        finish(n_prep - 1)
        for c in store(i - 1, 1 - slot) + store(i, slot):
            c.wait()


def _const_spec(shape):
    n = len(shape)
    return pl.BlockSpec(shape, lambda *_: (0,) * n, pipeline_mode=pl.Buffered(1))


def kernel(x_prompt, x_sample, state_hgrn, state_pool, norm_g, w_in, lb_logits, hgrn_norm_g,
           w_proj_a, w_pool, pool_scale, w_proj_b, w_out, final_norm_g):
    batch, seq, _ = x_prompt.shape
    dec_batch, dec_seq, _ = x_sample.shape
    assert norm_g.shape[0] == 1 and lb_logits.shape[0] == 2, "single-layer decoder only"
    assert seq % TM_PROMPT == 0 and TM_PROMPT % SUB_TILE == 0 and dec_batch % BB_SAMPLE == 0
    assert dec_seq == DEC_SEQ and PAIR_ROWS == SUBLANES and BB_SAMPLE % 2 == 0
    assert dec_batch % PREP_SEQS == 0 and (PREP_SEQS * DEC_SEQ) % SUB_TILE == 0
    assert PREP_SEQS == PREP_PARTS * BB_SAMPLE and BB_SAMPLE % STATE_DMAS == 0
    assert PAST_LEN >= max(POOL_WINDOWS)

    n_groups = len(POOL_WINDOWS)
    tm = TM_PROMPT
    nt = seq // tm
    assert w_in.shape[1:] == (2 * tm, D_IN) and w_proj_a.shape[1] == tm, "weight staging layout"
    assert tm % (WIN_SLOTS * 2 * SUBLANES) == 0, "bf16-tile-aligned w_in staging blocks"
    assert tm // WIN_SLOTS == G_B, "a staging block of the small weights carries one w_pool group"
    hbm = pl.BlockSpec(memory_space=pl.ANY)
    bf16_shapes = [(D_MODEL, D_IN), (W_BRANCH, D_MODEL), (n_groups, G_B, G_B),
                   (W_BRANCH, D_MODEL), (D_MODEL, D_MODEL)]
    y_p, s_p, p_p, win_b, wpa_b, wpool_b, wpb_b, wout_b = pl.pallas_call(
        _prompt_kernel,
        grid=(batch, nt),
        in_specs=[pl.BlockSpec((1, tm, D_MODEL), lambda b, t: (b, t, 0)),
                  _const_spec((1, D_MODEL)), hbm, _const_spec((2, W_BRANCH)), _const_spec((1, DV)),
                  hbm, hbm, _const_spec((1, W_BRANCH)), hbm, hbm, _const_spec((1, D_MODEL))],
        out_specs=[
            pl.BlockSpec((1, tm, D_MODEL), lambda b, t: (b, t, 0)),
            pl.BlockSpec((1, N_HEADS, DK, DV), lambda b, t: (b, 0, 0, 0)),
            pl.BlockSpec((POOL_BUF, batch, W_BRANCH), lambda b, t: (0, 0, 0)),
        ] + [hbm] * len(bf16_shapes),
        out_shape=[
            jax.ShapeDtypeStruct((batch, seq, D_MODEL), F32),
            jax.ShapeDtypeStruct((batch, N_HEADS, DK, DV), F32),
            jax.ShapeDtypeStruct((POOL_BUF, batch, W_BRANCH), F32),
        ] + [jax.ShapeDtypeStruct(s, BF16) for s in bf16_shapes],
        scratch_shapes=[
            pltpu.VMEM((tm, D_IN), F32),
            pltpu.VMEM((N_HEADS, DV, DK), F32),
            pltpu.VMEM((tm + 2 * SUBLANES, W_BRANCH), F32),
            pltpu.VMEM((N_HEADS, tm, (SUB_TILE // CHUNK) * DK), BF16),
            pltpu.VMEM((N_HEADS, tm // CHUNK, DK, DV), BF16),
            pltpu.VMEM((tm, W_BRANCH), BF16),
            pltpu.VMEM((tm, W_BRANCH), BF16),
            pltpu.VMEM((tm, W_BRANCH), BF16),
            pltpu.VMEM((tm, W_BRANCH), BF16),
            pltpu.VMEM((tm, W_BRANCH), BF16),
            pltpu.VMEM((tm, W_BRANCH), BF16),
        ] + [pltpu.VMEM(s, BF16) for s in bf16_shapes] + [
            pltpu.SemaphoreType.DMA((WIN_SLOTS, 5)),
            pltpu.SemaphoreType.DMA((len(bf16_shapes),)),
        ],
        compiler_params=pltpu.CompilerParams(
            dimension_semantics=("arbitrary", "arbitrary"),
            vmem_limit_bytes=VMEM_LIMIT_BYTES),
        name="hgrn2_pool_prompt",
    )(x_prompt, norm_g, w_in[0], lb_logits, hgrn_norm_g, w_proj_a[0],
      w_pool[0].reshape(n_groups * G_B, G_B), pool_scale, w_proj_b[0], w_out[0],
      final_norm_g.reshape(1, D_MODEL))
    weights = (norm_g, win_b, lb_logits, hgrn_norm_g, wpa_b, wpool_b, pool_scale, wpb_b, wout_b,
               final_norm_g.reshape(1, D_MODEL))

    rows = dec_batch * DEC_SEQ
    pool_in = jnp.transpose(state_pool[0], (1, 0, 2))
    ng, win_b, lbl, hg, wpa_b, wpool_b, ps, wpb_b, wout_b, fg = weights
    pseq = PREP_SEQS
    prows = pseq * DEC_SEQ
    n_prep = dec_batch // pseq
    bb = BB_SAMPLE

    def prep_blk(s):
        return jnp.minimum((s + PREP_PARTS - 1) // PREP_PARTS, n_prep - 1)

    def finish_blk(s):
        return jnp.minimum(jnp.maximum(s - 2, 0) // PREP_PARTS, n_prep - 1)

    s_s, y_s, pool_out = pl.pallas_call(
        functools.partial(_decode_kernel, n_prep=n_prep),
        grid=(1 + dec_batch // bb,),
        in_specs=[
            pl.BlockSpec((pseq, DEC_SEQ, D_MODEL), lambda s: (prep_blk(s), 0, 0)),
            pl.BlockSpec((POOL_BUF, pseq, W_BRANCH), lambda s: (0, prep_blk(s), 0),
                         pipeline_mode=pl.Buffered(1)),
            hbm,
            _const_spec(ng.shape), _const_spec(win_b.shape), _const_spec(lbl.shape),
            _const_spec(wpool_b.shape), _const_spec(ps.shape), _const_spec(wpb_b.shape),
            _const_spec(hg.shape), hbm, hbm,
            _const_spec(fg.shape),
        ],
        out_specs=[
            hbm,
            pl.BlockSpec((pseq, DEC_SEQ, D_MODEL), lambda s: (finish_blk(s), 0, 0)),
            pl.BlockSpec((POOL_BUF, pseq, W_BRANCH), lambda s: (0, prep_blk(s), 0)),
        ],
        out_shape=[
            jax.ShapeDtypeStruct((dec_batch, N_HEADS, DK, DV), F32),
            jax.ShapeDtypeStruct(x_sample.shape, F32),
            jax.ShapeDtypeStruct((POOL_BUF, dec_batch, W_BRANCH), F32),
        ],
        scratch_shapes=[
            pltpu.VMEM((N_HEADS, 2, rows, DK), BF16),
            pltpu.VMEM((N_HEADS, 2, rows, DV), BF16),
            pltpu.VMEM((rows, W_BRANCH), BF16),
            pltpu.VMEM((rows, W_BRANCH), F32),
            pltpu.VMEM((rows, W_BRANCH), F32),
            pltpu.VMEM((rows, D_MODEL), F32),
            pltpu.VMEM((rows, D_MODEL), F32),
            pltpu.VMEM((rows, D_MODEL), F32),
            pltpu.VMEM((prows, D_IN), F32),
            pltpu.VMEM((len(POOL_WINDOWS), prows, G_B), F32),
            pltpu.VMEM((len(POOL_WINDOWS), prows, G_B), F32),
            pltpu.VMEM((prows, W_BRANCH), BF16),
            pltpu.VMEM((prows, W_BRANCH), BF16),
            pltpu.VMEM((2, bb, N_HEADS, DK, DV), F32),
            pltpu.VMEM((2, bb, N_HEADS, DK, DV), F32),
            pltpu.VMEM(wpa_b.shape, BF16), pltpu.VMEM(wout_b.shape, BF16),
            pltpu.SemaphoreType.DMA((2, STATE_DMAS)),
            pltpu.SemaphoreType.DMA((2, STATE_DMAS)),
            pltpu.SemaphoreType.DMA((2,)),
        ],
        compiler_params=pltpu.CompilerParams(
            dimension_semantics=("arbitrary",), vmem_limit_bytes=DECODE_VMEM_LIMIT_BYTES),
        name="hgrn2_pool_decode",
    )(x_sample, pool_in, state_hgrn[0], ng, win_b, lbl, wpool_b, ps, wpb_b,
      hg, wpa_b, wout_b, fg)

    p_p = jnp.transpose(p_p, (1, 0, 2))
    p_s = jnp.transpose(pool_out, (1, 0, 2))
    return (y_p, y_s, s_p[None], p_p[None], s_s[None], p_s[None])
```

```python
import functools

import jax
import jax.numpy as jnp
from jax import lax
from jax.experimental import pallas as pl
from jax.experimental.pallas import tpu as pltpu

F32 = jnp.float32
BF16 = jnp.bfloat16

D_MODEL = 1024
W_BRANCH = 512
N_HEADS = 4
DK = 128
DV = 128
CHUNK = 32
POOL_WINDOWS = (2, 4, 8, 16)
G_B = 128
POOL_BUF = 15
PAST_LEN = 16384
EPS = 1e-6
D_IN = 4 * W_BRANCH + 2 * W_BRANCH + 2 * D_MODEL
OFF_Q, OFF_F, OFF_I, OFF_GA = 0, 512, 1024, 1536
OFF_U, OFF_GB, OFF_MA, OFF_MB = 2048, 2560, 3072, 4096

SUBLANES = 8
VMEM_LIMIT_BYTES = 56 * 1024 * 1024
DECODE_VMEM_LIMIT_BYTES = 60 * 1024 * 1024

TM_PROMPT = 512
WIN_SLOTS = 4
SUB_TILE = 256
PREP_SEQS = 64
BB_SAMPLE = 16
STATE_DMAS = 2
DEC_SEQ = 4
PAIR_ROWS = 2 * DEC_SEQ
PREP_PARTS = 4


def _rms(x, g):
    ms = jnp.mean(x * x, axis=-1, keepdims=True)
    return x * lax.rsqrt(ms + EPS) * g


def _lower_bound(lbl_ref):
    l0 = lbl_ref[0:1, :]
    l1 = lbl_ref[1:2, :]
    m = jnp.maximum(l0, l1)
    e0 = jnp.exp(l0 - m)
    e1 = jnp.exp(l1 - m)
    return e0 / (e0 + e1)


def _in_proj(x, ng_ref, win_ref, z_ref, c0=0, c1=D_IN):
    h = _rms(x, ng_ref[...]).astype(BF16)
    z_ref[:, c0:c1] = jnp.dot(h, win_ref[:, c0:c1], preferred_element_type=F32)


def _hgrn_inputs(z_ref, lb, hh):
    c = hh * DK
    q = z_ref[:, OFF_Q + c:OFF_Q + c + DK]
    f = z_ref[:, OFF_F + c:OFF_F + c + DK]
    v = z_ref[:, OFF_I + c:OFF_I + c + DK]
    lbh = lb[:, c:c + DK]
    fg = lbh + (1.0 - lbh) * jax.nn.sigmoid(f)
    logf = jnp.log(fg)
    k = 1.0 - fg
    qf = jax.nn.silu(q) * (DK ** -0.5)
    return qf, k, v, logf


def _group_cumsum(x, group):
    row = lax.broadcasted_iota(jnp.int32, x.shape, 0) & (group - 1)
    s = 1
    while s < group:
        x = x + jnp.where(row >= s, pltpu.roll(x, s, 0), 0.0)
        s *= 2
    return x


def _hgrn_epilogue(o, z_ref, hg_ref, hh, oa_ref):
    c = hh * DK
    ga = z_ref[:, OFF_GA + c:OFF_GA + c + DK]
    on = _rms(o, hg_ref[...]) * jax.nn.silu(ga)
    oa_ref[:, c:c + DK] = on.astype(BF16)


def _pool_epilogue(pooled, z_ref, g, wpool_ref, ps_ref, yb_ref):
    c = g * G_B
    mixed = jnp.dot(pooled.astype(BF16), wpool_ref[g], preferred_element_type=F32)
    gb = z_ref[:, OFF_GB + c:OFF_GB + c + G_B]
    yb = mixed * ps_ref[:, c:c + G_B] * jax.nn.silu(gb)
    yb_ref[:, c:c + G_B] = yb.astype(BF16)


def _weight_copies(hbm_refs, stage_ref, sem):
    win_hbm, wpa_hbm, wpool_hbm, wpb_hbm, wout_hbm = hbm_refs
    rows = stage_ref.shape[0]
    blk = rows // WIN_SLOTS
    n_small = rows // blk

    def block(k):
        slot = k % WIN_SLOTS

        def dst(c0, c1):
            return stage_ref.at[slot * blk:(slot + 1) * blk, c0:c1]
        if k < n_small:
            r0 = k * blk
            pairs = [(wpa_hbm.at[r0:r0 + blk, :], dst(0, D_MODEL)),
                     (wpb_hbm.at[r0:r0 + blk, :], dst(D_MODEL, 2 * D_MODEL)),
                     (wout_hbm.at[r0:r0 + blk, :], dst(2 * D_MODEL, 3 * D_MODEL)),
                     (wout_hbm.at[rows + r0:rows + r0 + blk, :], dst(3 * D_MODEL, 4 * D_MODEL)),
                     (wpool_hbm.at[r0:r0 + blk, :], dst(4 * D_MODEL, 4 * D_MODEL + G_B))]
        else:
            r0 = (k - n_small) * blk
            pairs = [(win_hbm.at[r0:r0 + blk, :], dst(0, D_IN))]
        return [pltpu.make_async_copy(src, d, sem.at[slot, i]) for i, (src, d) in enumerate(pairs)]
    return block, blk, n_small


def _weight_exports(vmem_refs, out_refs, sem):
    return [pltpu.make_async_copy(src, dst, sem.at[i])
            for i, (src, dst) in enumerate(zip(vmem_refs, out_refs))]


def _prompt_kernel(x_ref, ng_ref, win_hbm, lbl_ref, hg_ref, wpa_hbm, wpool_hbm, ps_ref,
                   wpb_hbm, wout_hbm, fg_ref,
                   y_ref, s_out_ref, p_out_ref, win_out, wpa_out, wpool_out, wpb_out, wout_out,
                   z_ref, st_ref, ext_ref, kx_ref, sn_ref, qs_ref, ks_ref, qd_ref, vb_ref,
                   oa_ref, yb_ref, win_ref, wpa_ref, wpool_ref, wpb_ref, wout_ref,
                   load_sem, export_sem):
    tm = TM_PROMPT
    sub = SUB_TILE
    n_sub = tm // sub
    nc = tm // CHUNK
    ncs = sub // CHUNK
    t = pl.program_id(1)
    first = t == 0
    hdr = 2 * SUBLANES
    bf16_weights = (win_ref, wpa_ref, wpool_ref, wpb_ref, wout_ref)
    exports = _weight_exports(bf16_weights, (win_out, wpa_out, wpool_out, wpb_out, wout_out),
                              export_sem)

    @pl.when((pl.program_id(0) == 0) & first)
    def _():
        block, blk, n_small = _weight_copies((win_hbm, wpa_hbm, wpool_hbm, wpb_hbm, wout_hbm),
                                             z_ref, load_sem)
        n_blk = n_small + D_MODEL // blk
        for k in range(WIN_SLOTS):
            for c in block(k):
                c.start()
        st_ref[...] = jnp.zeros_like(st_ref)
        ext_ref[...] = jnp.zeros_like(ext_ref)
        kx_ref[...] = jnp.zeros_like(kx_ref)
        for k in range(n_blk):
            for c in block(k):
                c.wait()
            slot = z_ref.at[(k % WIN_SLOTS) * blk:(k % WIN_SLOTS + 1) * blk, :]
            if k < n_small:
                r = slice(k * blk, (k + 1) * blk)
                wpa_ref[r, :] = slot[:, 0:D_MODEL].astype(BF16)
                wpb_ref[r, :] = slot[:, D_MODEL:2 * D_MODEL].astype(BF16)
                wout_ref[r, :] = slot[:, 2 * D_MODEL:3 * D_MODEL].astype(BF16)
                wout_ref[tm + k * blk:tm + (k + 1) * blk, :] = (
                    slot[:, 3 * D_MODEL:4 * D_MODEL].astype(BF16))
                wpool_ref[k] = slot[:, 4 * D_MODEL:4 * D_MODEL + G_B].astype(BF16)
            else:
                r = slice((k - n_small) * blk, (k - n_small + 1) * blk)
                win_ref[r, :] = slot[...].astype(BF16)
            if k + WIN_SLOTS < n_blk:
                for c in block(k + WIN_SLOTS):
                    c.start()
        for c in exports:
            c.start()

    @pl.when((pl.program_id(0) == pl.num_programs(0) - 1) & (t == pl.num_programs(1) - 1))
    def _():
        for c in exports:
            c.wait()

    x = x_ref[0]
    _in_proj(x, ng_ref, win_ref, z_ref)
    lb = _lower_bound(lbl_ref)

    ri = lax.broadcasted_iota(jnp.int32, (sub, sub), 0)
    ci = lax.broadcasted_iota(jnp.int32, (sub, sub), 1)
    causal = ((ri // CHUNK) == (ci // CHUNK)) & (ci <= ri)

    heads = range(N_HEADS)
    units = [(hh, s) for hh in heads for s in range(n_sub)]

    def rows(s):
        return slice(s * sub, (s + 1) * sub)

    def cols(hh):
        return slice(hh * DK, (hh + 1) * DK)

    dec = []
    for hh in heads:
        qf, k, v, logf = _hgrn_inputs(z_ref, lb, hh)
        b = _group_cumsum(logf, CHUNK)
        b3 = b.reshape(nc, CHUNK, DK)
        ref = b3[:, CHUNK // 2:CHUNK // 2 + 1, :]
        bl = b3[:, CHUNK - 1:CHUNK, :]
        q3 = qf.reshape(nc, CHUNK, DK)
        k3 = k.reshape(nc, CHUNK, DK)
        qs_ref[:, cols(hh)] = (q3 * jnp.exp(b3 - ref)).reshape(tm, DK).astype(BF16)
        ks_ref[:, cols(hh)] = (k3 * jnp.exp(ref - b3)).reshape(tm, DK).astype(BF16)
        qd_ref[:, cols(hh)] = (q3 * jnp.exp(b3)).reshape(tm, DK).astype(BF16)
        vb_ref[:, cols(hh)] = v.astype(BF16)
        kd = (k3 * jnp.exp(bl - b3)).reshape(tm, DK).astype(BF16)
        dec.append(jnp.exp(bl))
        for c in range(nc):
            cc = c % ncs
            kx_ref[hh, c * CHUNK:(c + 1) * CHUNK, cc * DK:(cc + 1) * DK] = (
                kd[c * CHUNK:(c + 1) * CHUNK, :])

    ext_ref[0:hdr, :] = jnp.where(first, 0.0, ext_ref[tm:tm + hdr, :])
    ext_ref[hdr:hdr + tm, :] = z_ref[:, OFF_U:OFF_U + W_BRANCH]
    pos1 = t * tm + lax.broadcasted_iota(jnp.int32, (tm, 1), 0) + 1
    for g, w in enumerate(POOL_WINDOWS):
        c = g * G_B
        s = ext_ref[:, c:c + G_B]
        sh = 1
        while sh < w:
            s = s + pltpu.roll(s, sh, 0)
            sh *= 2
        inv = jnp.where(pos1 >= w, 1.0 / w, 1.0 / pos1.astype(F32))
        pooled = s[hdr:, :] * inv - z_ref[:, OFF_U + c:OFF_U + c + G_B]
        _pool_epilogue(pooled, z_ref, g, wpool_ref, ps_ref, yb_ref)
    for j in range(POOL_BUF):
        r = hdr + tm - POOL_BUF + j
        p_out_ref[j, pl.ds(pl.program_id(0), 1), :] = ext_ref[r:r + 1, :]

    sc = {(hh, s): lax.dot_general(qs_ref[rows(s), cols(hh)], ks_ref[rows(s), cols(hh)],
                                   (((1,), (1,)), ((), ())), preferred_element_type=F32)
          for hh, s in units}
    ut = {(hh, s): lax.dot_general(vb_ref[rows(s), cols(hh)], kx_ref[hh, rows(s), :],
                                   (((0,), (0,)), ((), ())), preferred_element_type=F32)
          for hh, s in units}
    o = {(hh, s): jnp.dot(jnp.where(causal, sc[hh, s], 0.0).astype(BF16),
                          vb_ref[rows(s), cols(hh)], preferred_element_type=F32)
         for hh, s in units}
    for hh in heads:
        st = jnp.where(first, 0.0, st_ref[hh])
        for c in range(nc):
            sn_ref[hh, c] = st.T.astype(BF16)
            st = dec[hh][c] * st + ut[hh, c // ncs][:, (c % ncs) * DK:(c % ncs + 1) * DK]
        st_ref[hh] = st
        s_out_ref[0, hh] = st.T
    wcol = D_MODEL // N_HEADS
    y_b = []
    for hh in heads:
        o_inter = [jnp.dot(qd_ref[c * CHUNK:(c + 1) * CHUNK, cols(hh)], sn_ref[hh, c],
                           preferred_element_type=F32) for c in range(nc)]
        y_b.append(jnp.dot(yb_ref[...], wpb_ref[:, hh * wcol:(hh + 1) * wcol],
                           preferred_element_type=F32))
        o_hh = jnp.concatenate([o[hh, s] for s in range(n_sub)], axis=0)
        _hgrn_epilogue(o_hh + jnp.concatenate(o_inter, axis=0), z_ref, hg_ref, hh, oa_ref)
    gated_b = jax.nn.sigmoid(z_ref[:, OFF_MB:OFF_MB + D_MODEL]) * jnp.concatenate(y_b, axis=1)

    y_a = jnp.dot(oa_ref[...], wpa_ref[...], preferred_element_type=F32)
    merged = jax.nn.sigmoid(z_ref[:, OFF_MA:OFF_MA + D_MODEL]) * y_a + gated_b
    out = x + jnp.dot(merged.astype(BF16), wout_ref[...], preferred_element_type=F32)
    y_ref[0] = _rms(out, fg_ref[...])


def _shift_rows(x, k):
    n = x.shape[0]
    return x if k % n == 0 else pltpu.roll(x, (-k) % n, 0)


def _group_bcast(x, j, group):
    t = lax.broadcasted_iota(jnp.int32, x.shape, 0) & (group - 1)
    out = _shift_rows(x, j - (group - 1))
    for tt in range(group - 2, -1, -1):
        out = jnp.where(t == tt, _shift_rows(x, j - tt), out)
    return out


def _decode_prep_hgrn(heads, z_ref, lbl_ref, a_ref, bv_ref, qd_ref, oi_ref, ga_ref):
    rows = z_ref.shape[0]
    lb = _lower_bound(lbl_ref)
    r = lax.broadcasted_iota(jnp.int32, (rows, DK), 0)
    t = r & (DEC_SEQ - 1)
    even = (r & (PAIR_ROWS - 1)) < DEC_SEQ
    blk = SUB_TILE
    ri = lax.broadcasted_iota(jnp.int32, (blk, blk), 0)
    ci = lax.broadcasted_iota(jnp.int32, (blk, blk), 1)
    causal = ((ri // DEC_SEQ) == (ci // DEC_SEQ)) & (ci <= ri)

    for hh in heads:
        cs = slice(hh * DK, (hh + 1) * DK)
        qf, k, v, logf = _hgrn_inputs(z_ref, lb, hh)
        b = _group_cumsum(logf, DEC_SEQ)
        ref = _group_bcast(b, DEC_SEQ // 2, DEC_SEQ)
        bl = _group_bcast(b, DEC_SEQ - 1, DEC_SEQ)
        qs = (qf * jnp.exp(b - ref)).astype(BF16)
        ks = (k * jnp.exp(ref - b)).astype(BF16)
        kd = k * jnp.exp(bl - b)
        vb = v.astype(BF16)
        qd_ref[:, cs] = (qf * jnp.exp(b)).astype(BF16)
        for s in range(rows // blk):
            rs = slice(s * blk, (s + 1) * blk)
            sc = lax.dot_general(qs[rs], ks[rs], (((1,), (1,)), ((), ())),
                                 preferred_element_type=F32)
            oi_ref[rs, cs] = jnp.dot(jnp.where(causal, sc, 0.0).astype(BF16), vb[rs],
                                     preferred_element_type=F32)
        dec = jnp.exp(bl)
        d1 = dec.astype(BF16).astype(F32)
        d2 = (dec - d1).astype(BF16).astype(F32)
        d3 = (dec - d1 - d2).astype(BF16).astype(F32)
        tail = jnp.where(t == 0, d1, jnp.where(t == 1, d2, jnp.where(t == 2, d3, 0.0)))
        a_ref[hh, 0] = jnp.where(even, kd, _shift_rows(tail, -DEC_SEQ)).astype(BF16)
        a_ref[hh, 1] = jnp.where(even, _shift_rows(tail, DEC_SEQ), kd).astype(BF16)
        zero = jnp.zeros_like(vb)
        bv_ref[hh, 0] = jnp.where(even, vb, zero)
        bv_ref[hh, 1] = jnp.where(even, zero, vb)
        ga_ref[:, cs] = jax.nn.silu(z_ref[:, OFF_GA + hh * DK:OFF_GA + (hh + 1) * DK])


def _decode_prep_history(p_in_ref, rw_ref, p_out_ref):
    n_seq = p_in_ref.shape[1]
    for g, w in enumerate(POOL_WINDOWS):
        gc = slice(g * G_B, (g + 1) * G_B)
        acc = None
        suffix = {}
        for j in range(POOL_BUF - 1, -1, -1):
            e = p_in_ref[j, :, gc]
            acc = e if acc is None else acc + e
            suffix[j] = acc
        for tt in range(DEC_SEQ):
            j = POOL_BUF + 1 - w + tt
            rw_ref[g, pl.ds(tt, n_seq, stride=DEC_SEQ), :] = (
                suffix[j] if j < POOL_BUF else jnp.zeros((n_seq, G_B), F32))
    for j in range(POOL_BUF - DEC_SEQ):
        p_out_ref[j] = p_in_ref[j + DEC_SEQ]


def _decode_prep_pool(z_ref, wpool_ref, ps_ref, wpb_ref, ma_ref, gb_ref, p_out_ref,
                      rw_ref, u_ref, yb_ref):
    rows = z_ref.shape[0]
    n_seq = rows // DEC_SEQ
    t = lax.broadcasted_iota(jnp.int32, (rows, G_B), 0) & (DEC_SEQ - 1)
    for g, w in enumerate(POOL_WINDOWS):
        u = z_ref[:, OFF_U + g * G_B:OFF_U + (g + 1) * G_B]
        u_ref[g] = u
        cu = u
        sh = 1
        while sh < min(w, DEC_SEQ):
            cu = cu + jnp.where(t >= sh, _shift_rows(cu, -sh), 0.0)
            sh *= 2
        pooled = (rw_ref[g] + cu) * (1.0 / w) - u
        _pool_epilogue(pooled, z_ref, g, wpool_ref, ps_ref, yb_ref)
    gb_ref[...] = jax.nn.sigmoid(z_ref[:, OFF_MB:OFF_MB + D_MODEL]) * jnp.dot(
        yb_ref[...], wpb_ref[...], preferred_element_type=F32)
    ma_ref[...] = jax.nn.sigmoid(z_ref[:, OFF_MA:OFF_MA + D_MODEL])
    for tt in range(DEC_SEQ):
        for g in range(len(POOL_WINDOWS)):
            p_out_ref[POOL_BUF - DEC_SEQ + tt, :, g * G_B:(g + 1) * G_B] = (
                u_ref[g, pl.ds(tt, n_seq, stride=DEC_SEQ), :])


def _decode_prep(part, x_ref, p_in_ref, ng_ref, win_ref, lbl_ref, wpool_ref, ps_ref, wpb_ref,
                 a_ref, bv_ref, qd_ref, oi_ref, ga_ref, ma_ref, gb_ref, p_out_ref,
                 xs_ref, z_ref, rw_ref, u_ref, yb_ref):
    hgrn = functools.partial(_decode_prep_hgrn, z_ref=z_ref, lbl_ref=lbl_ref, a_ref=a_ref,
                             bv_ref=bv_ref, qd_ref=qd_ref, oi_ref=oi_ref, ga_ref=ga_ref)
    if part == 0:
        for i in range(x_ref.shape[0]):
            xs_ref[i * DEC_SEQ:(i + 1) * DEC_SEQ, :] = x_ref[i]
        _in_proj(xs_ref[...], ng_ref, win_ref, z_ref, 0, OFF_U)
        _decode_prep_history(p_in_ref, rw_ref, p_out_ref)
    elif part == 1:
        _in_proj(xs_ref[...], ng_ref, win_ref, z_ref, OFF_U, D_IN)
    elif part == 2:
        hgrn(range(0, N_HEADS - 1))
    else:
        hgrn(range(N_HEADS - 1, N_HEADS))
        _decode_prep_pool(z_ref, wpool_ref, ps_ref, wpb_ref, ma_ref, gb_ref, p_out_ref,
                          rw_ref, u_ref, yb_ref)


def _decode_state_step(a_ref, bv_ref, qd_ref, s_in_ref, s_out_ref, o_ref, stores):
    n_pairs = BB_SAMPLE // 2
    r8 = lax.broadcasted_iota(jnp.int32, (PAIR_ROWS, DV), 0)
    even = r8 < DEC_SEQ
    for hh in range(N_HEADS):
        cs = slice(hh * DK, (hh + 1) * DK)
        for lp in range(n_pairs):
            rs = slice(lp * PAIR_ROWS, (lp + 1) * PAIR_ROWS)
            q8 = qd_ref[rs, cs]
            o0 = jnp.dot(q8, s_in_ref[2 * lp, hh].astype(BF16), preferred_element_type=F32)
            o1 = jnp.dot(q8, s_in_ref[2 * lp + 1, hh].astype(BF16), preferred_element_type=F32)
            o_ref[rs, cs] = o_ref[rs, cs] + jnp.where(even, o0, o1)
    ones = (jnp.where((r8 >= DEC_SEQ) & (r8 < DEC_SEQ + 3), 1.0, 0.0).astype(BF16),
            jnp.where(r8 < 3, 1.0, 0.0).astype(BF16))
    pairs_per_store = n_pairs // len(stores)
    for g, store in enumerate(stores):
        for hh in range(N_HEADS):
            for lp in range(g * pairs_per_store, (g + 1) * pairs_per_store):
                rs = slice(lp * PAIR_ROWS, (lp + 1) * PAIR_ROWS)
                for e in range(2):
                    rhs = jnp.concatenate([bv_ref[hh, e, rs, :], ones[e]], axis=1)
                    upd = lax.dot_general(a_ref[hh, e, rs, :], rhs, (((0,), (0,)), ((), ())),
                                          preferred_element_type=F32)
                    s_out_ref[2 * lp + e, hh] = (upd[:, DV:2 * DV] * s_in_ref[2 * lp + e, hh]
                                                 + upd[:, 0:DV])
        store.start(priority=1)


def _decode_finish(o_ref, ga_ref, ma_ref, gb_ref, xs_ref, hg_ref, wpa_ref, wout_ref,
                   fg_ref, y_ref, oa_ref, *, n_seq):
    for hh in range(N_HEADS):
        cs = slice(hh * DK, (hh + 1) * DK)
        oa_ref[:, cs] = (_rms(o_ref[:, cs], hg_ref[...]) * ga_ref[:, cs]).astype(BF16)
    y_a = jnp.dot(oa_ref[...], wpa_ref[...], preferred_element_type=F32)
    merged = ma_ref[...] * y_a + gb_ref[...]
    out = xs_ref[...] + jnp.dot(merged.astype(BF16), wout_ref[...], preferred_element_type=F32)
    y = _rms(out, fg_ref[...])
    for i in range(n_seq):
        y_ref[i] = y[i * DEC_SEQ:(i + 1) * DEC_SEQ, :]


def _state_copies(hbm_ref, buf_ref, sem, blk, slot, to_hbm):
    n = BB_SAMPLE // STATE_DMAS
    copies = []
    for c in range(STATE_DMAS):
        in_hbm = hbm_ref.at[pl.ds(blk * BB_SAMPLE + c * n, n)]
        in_vmem = buf_ref.at[slot, pl.ds(c * n, n)]
        src, dst = (in_vmem, in_hbm) if to_hbm else (in_hbm, in_vmem)
        copies.append(pltpu.make_async_copy(src, dst, sem.at[slot, c]))
    return copies


def _decode_kernel(xp_ref, p_in_ref, s_in_hbm,
                   ng_ref, win_ref, lbl_ref, wpool_ref, ps_ref, wpb_ref,
                   hg_ref, wpa_hbm, wout_hbm, fg_ref,
                   s_out_hbm, y_ref, p_out_ref,
                   a_all, bv_all, qd_all, o_all, ga_all, ma_all, gb_all, xs_all,
                   z_ref, rw_ref, u_ref, yb_ref, oa_ref, s_in_buf, s_out_buf, wpa_ref, wout_ref,
                   in_sem, out_sem, w_sem, *, n_prep):
    s = pl.program_id(0)
    prows = PREP_SEQS * DEC_SEQ
    brows = BB_SAMPLE * DEC_SEQ
    n_state = n_prep * PREP_PARTS
    late_weights = [pltpu.make_async_copy(src, dst, w_sem.at[k]) for k, (src, dst) in
                    enumerate(((wpa_hbm, wpa_ref), (wout_hbm, wout_ref)))]

    def load(blk, slot):
        return _state_copies(s_in_hbm, s_in_buf, in_sem, blk, slot, to_hbm=False)

    def store(blk, slot):
        return _state_copies(s_out_hbm, s_out_buf, out_sem, blk, slot, to_hbm=True)

    def block_rows(blk):
        start = blk * prows
        return pl.ds(start if isinstance(start, int) else pl.multiple_of(start, prows), prows)

    def prep(part, blk):
        r = block_rows(blk)
        _decode_prep(part, xp_ref, p_in_ref, ng_ref, win_ref, lbl_ref, wpool_ref, ps_ref, wpb_ref,
                     a_all.at[:, :, r, :], bv_all.at[:, :, r, :], qd_all.at[r, :],
                     o_all.at[r, :], ga_all.at[r, :], ma_all.at[r, :], gb_all.at[r, :],
                     p_out_ref, xs_all.at[r, :], z_ref, rw_ref, u_ref, yb_ref)

    def finish(blk):
        f = block_rows(blk)
        _decode_finish(o_all.at[f, :], ga_all.at[f, :], ma_all.at[f, :], gb_all.at[f, :],
                       xs_all.at[f, :], hg_ref, wpa_ref, wout_ref, fg_ref, y_ref, oa_ref,
                       n_seq=PREP_SEQS)

    i = s - 1
    blk = lax.div(i, PREP_PARTS)
    part_now = lax.rem(i, PREP_PARTS)
    slot = lax.rem(i, 2)

    @pl.when(s == 0)
    def _():
        for b in range(2):
            for c in load(b, b):
                c.start()
        for c in late_weights:
            c.start()
        for part in range(PREP_PARTS):
            prep(part, 0)
        for c in late_weights:
            c.wait()

    @pl.when(s > 0)
    def _():
        @pl.when((i > 0) & (i + 1 < n_state))
        def _():
            for c in load(i + 1, 1 - slot):
                c.start()
        for c in load(i, slot):
            c.wait()

        @pl.when(i >= 2)
        def _():
            for c in store(i - 2, slot):
                c.wait()
        r = pl.ds(pl.multiple_of(i * brows, brows), brows)
        _decode_state_step(a_all.at[:, :, r, :], bv_all.at[:, :, r, :], qd_all.at[r, :],
                           s_in_buf.at[slot], s_out_buf.at[slot], o_all.at[r, :],
                           stores=store(i, slot))

    for part in range(PREP_PARTS):
        @pl.when((part_now == part) & (blk + 1 < n_prep))
        def _():
            prep(part, blk + 1)

    @pl.when((part_now == 0) & (i > 0))
    def _():
        finish(blk - 1)

    @pl.when(i == n_state - 1)
    def _():
        finish(n_prep - 1)
        for c in store(i - 1, 1 - slot) + store(i, slot):
            c.wait()


def _const_spec(shape):
    n = len(shape)
    return pl.BlockSpec(shape, lambda *_: (0,) * n, pipeline_mode=pl.Buffered(1))


def kernel(x_prompt, x_sample, state_hgrn, state_pool, norm_g, w_in, lb_logits, hgrn_norm_g,
           w_proj_a, w_pool, pool_scale, w_proj_b, w_out, final_norm_g):
    batch, seq, _ = x_prompt.shape
    dec_batch, dec_seq, _ = x_sample.shape
    assert norm_g.shape[0] == 1 and lb_logits.shape[0] == 2, "single-layer decoder only"
    assert seq % TM_PROMPT == 0 and TM_PROMPT % SUB_TILE == 0 and dec_batch % BB_SAMPLE == 0
    assert dec_seq == DEC_SEQ and PAIR_ROWS == SUBLANES and BB_SAMPLE % 2 == 0
    assert dec_batch % PREP_SEQS == 0 and (PREP_SEQS * DEC_SEQ) % SUB_TILE == 0
    assert PREP_SEQS == PREP_PARTS * BB_SAMPLE and BB_SAMPLE % STATE_DMAS == 0
    assert PAST_LEN >= max(POOL_WINDOWS)

    n_groups = len(POOL_WINDOWS)
    tm = TM_PROMPT
    nt = seq // tm
    assert w_in.shape[1:] == (2 * tm, D_IN) and w_proj_a.shape[1] == tm, "weight staging layout"
    assert tm % (WIN_SLOTS * 2 * SUBLANES) == 0, "bf16-tile-aligned w_in staging blocks"
    assert tm // WIN_SLOTS == G_B, "a staging block of the small weights carries one w_pool group"
    hbm = pl.BlockSpec(memory_space=pl.ANY)
    bf16_shapes = [(D_MODEL, D_IN), (W_BRANCH, D_MODEL), (n_groups, G_B, G_B),
                   (W_BRANCH, D_MODEL), (D_MODEL, D_MODEL)]
    y_p, s_p, p_p, win_b, wpa_b, wpool_b, wpb_b, wout_b = pl.pallas_call(
        _prompt_kernel,
        grid=(batch, nt),
        in_specs=[pl.BlockSpec((1, tm, D_MODEL), lambda b, t: (b, t, 0)),
                  _const_spec((1, D_MODEL)), hbm, _const_spec((2, W_BRANCH)), _const_spec((1, DV)),
                  hbm, hbm, _const_spec((1, W_BRANCH)), hbm, hbm, _const_spec((1, D_MODEL))],
        out_specs=[
            pl.BlockSpec((1, tm, D_MODEL), lambda b, t: (b, t, 0)),
            pl.BlockSpec((1, N_HEADS, DK, DV), lambda b, t: (b, 0, 0, 0)),
            pl.BlockSpec((POOL_BUF, batch, W_BRANCH), lambda b, t: (0, 0, 0)),
        ] + [hbm] * len(bf16_shapes),
        out_shape=[
            jax.ShapeDtypeStruct((batch, seq, D_MODEL), F32),
            jax.ShapeDtypeStruct((batch, N_HEADS, DK, DV), F32),
            jax.ShapeDtypeStruct((POOL_BUF, batch, W_BRANCH), F32),
        ] + [jax.ShapeDtypeStruct(s, BF16) for s in bf16_shapes],
        scratch_shapes=[
            pltpu.VMEM((tm, D_IN), F32),
            pltpu.VMEM((N_HEADS, DV, DK), F32),
            pltpu.VMEM((tm + 2 * SUBLANES, W_BRANCH), F32),
            pltpu.VMEM((N_HEADS, tm, (SUB_TILE // CHUNK) * DK), BF16),
            pltpu.VMEM((N_HEADS, tm // CHUNK, DK, DV), BF16),
            pltpu.VMEM((tm, W_BRANCH), BF16),
            pltpu.VMEM((tm, W_BRANCH), BF16),
            pltpu.VMEM((tm, W_BRANCH), BF16),
            pltpu.VMEM((tm, W_BRANCH), BF16),
            pltpu.VMEM((tm, W_BRANCH), BF16),
            pltpu.VMEM((tm, W_BRANCH), BF16),
        ] + [pltpu.VMEM(s, BF16) for s in bf16_shapes] + [
            pltpu.SemaphoreType.DMA((WIN_SLOTS, 5)),
            pltpu.SemaphoreType.DMA((len(bf16_shapes),)),
        ],
        compiler_params=pltpu.CompilerParams(
            dimension_semantics=("arbitrary", "arbitrary"),
            vmem_limit_bytes=VMEM_LIMIT_BYTES),
        name="hgrn2_pool_prompt",
    )(x_prompt, norm_g, w_in[0], lb_logits, hgrn_norm_g, w_proj_a[0],
      w_pool[0].reshape(n_groups * G_B, G_B), pool_scale, w_proj_b[0], w_out[0],
      final_norm_g.reshape(1, D_MODEL))
    weights = (norm_g, win_b, lb_logits, hgrn_norm_g, wpa_b, wpool_b, pool_scale, wpb_b, wout_b,
               final_norm_g.reshape(1, D_MODEL))

    rows = dec_batch * DEC_SEQ
    pool_in = jnp.transpose(state_pool[0], (1, 0, 2))
    ng, win_b, lbl, hg, wpa_b, wpool_b, ps, wpb_b, wout_b, fg = weights
    pseq = PREP_SEQS
    prows = pseq * DEC_SEQ
    n_prep = dec_batch // pseq
    bb = BB_SAMPLE

    def prep_blk(s):
        return jnp.minimum((s + PREP_PARTS - 1) // PREP_PARTS, n_prep - 1)

    def finish_blk(s):
        return jnp.minimum(jnp.maximum(s - 2, 0) // PREP_PARTS, n_prep - 1)

    s_s, y_s, pool_out = pl.pallas_call(
        functools.partial(_decode_kernel, n_prep=n_prep),
        grid=(1 + dec_batch // bb,),
        in_specs=[
            pl.BlockSpec((pseq, DEC_SEQ, D_MODEL), lambda s: (prep_blk(s), 0, 0)),
            pl.BlockSpec((POOL_BUF, pseq, W_BRANCH), lambda s: (0, prep_blk(s), 0),
                         pipeline_mode=pl.Buffered(1)),
            hbm,
            _const_spec(ng.shape), _const_spec(win_b.shape), _const_spec(lbl.shape),
            _const_spec(wpool_b.shape), _const_spec(ps.shape), _const_spec(wpb_b.shape),
            _const_spec(hg.shape), hbm, hbm,
            _const_spec(fg.shape),
        ],
        out_specs=[
            hbm,
            pl.BlockSpec((pseq, DEC_SEQ, D_MODEL), lambda s: (finish_blk(s), 0, 0)),
            pl.BlockSpec((POOL_BUF, pseq, W_BRANCH), lambda s: (0, prep_blk(s), 0)),
        ],
        out_shape=[
            jax.ShapeDtypeStruct((dec_batch, N_HEADS, DK, DV), F32),
            jax.ShapeDtypeStruct(x_sample.shape, F32),
            jax.ShapeDtypeStruct((POOL_BUF, dec_batch, W_BRANCH), F32),
        ],
        scratch_shapes=[
            pltpu.VMEM((N_HEADS, 2, rows, DK), BF16),
            pltpu.VMEM((N_HEADS, 2, rows, DV), BF16),
            pltpu.VMEM((rows, W_BRANCH), BF16),
            pltpu.VMEM((rows, W_BRANCH), F32),
            pltpu.VMEM((rows, W_BRANCH), F32),
            pltpu.VMEM((rows, D_MODEL), F32),
            pltpu.VMEM((rows, D_MODEL), F32),
            pltpu.VMEM((rows, D_MODEL), F32),
            pltpu.VMEM((prows, D_IN), F32),
            pltpu.VMEM((len(POOL_WINDOWS), prows, G_B), F32),
            pltpu.VMEM((len(POOL_WINDOWS), prows, G_B), F32),
            pltpu.VMEM((prows, W_BRANCH), BF16),
            pltpu.VMEM((prows, W_BRANCH), BF16),
            pltpu.VMEM((2, bb, N_HEADS, DK, DV), F32),
            pltpu.VMEM((2, bb, N_HEADS, DK, DV), F32),
            pltpu.VMEM(wpa_b.shape, BF16), pltpu.VMEM(wout_b.shape, BF16),
            pltpu.SemaphoreType.DMA((2, STATE_DMAS)),
            pltpu.SemaphoreType.DMA((2, STATE_DMAS)),
            pltpu.SemaphoreType.DMA((2,)),
        ],
        compiler_params=pltpu.CompilerParams(
            dimension_semantics=("arbitrary",), vmem_limit_bytes=DECODE_VMEM_LIMIT_BYTES),
        name="hgrn2_pool_decode",
    )(x_sample, pool_in, state_hgrn[0], ng, win_b, lbl, wpool_b, ps, wpb_b,
      hg, wpa_b, wout_b, fg)

    p_p = jnp.transpose(p_p, (1, 0, 2))
    p_s = jnp.transpose(pool_out, (1, 0, 2))
    return (y_p, y_s, s_p[None], p_p[None], s_s[None], p_s[None])
```
